```python
import math
import jax, jax.numpy as jnp
from jax import lax
import numpy as np

D_MODEL = 1024
BATCH = 16
SEQ = 2048
DEPTH = 4

CHUNK = 64
N_MIXERS = 2
N_GDN = (DEPTH + 1) // 2
N_HGRN = DEPTH // 2
EPS = 1e-6

GDN_HEADS = 8
GDN_HEAD_DIM = D_MODEL // GDN_HEADS
GDN_KEY_WIDTH = GDN_HEADS * GDN_HEAD_DIM
GDN_VAL_WIDTH = GDN_HEADS * GDN_HEAD_DIM
GDN_CONV_WIDTH = 2 * GDN_KEY_WIDTH + GDN_VAL_WIDTH
GDN_IN_WIDTH = GDN_CONV_WIDTH + GDN_VAL_WIDTH + 2 * GDN_HEADS
CONV_K = 4

HGRN_FORGET_DIM = 128
HGRN_HEADS = D_MODEL // HGRN_FORGET_DIM
HGRN_VALUE_DIM = D_MODEL // HGRN_HEADS
HGRN_WIDTH = HGRN_HEADS * HGRN_FORGET_DIM
HGRN_IN_WIDTH = 4 * HGRN_WIDTH
SUB = 16
N_SUB = CHUNK // SUB

MLP_HIDDEN = 4 * D_MODEL

kernel_name = "hybrid_gdn_hgrn2_stream_encoder"


def _rmsnorm(x, w):
    xf = x.astype(jnp.float32)
    y = xf * lax.rsqrt(jnp.mean(xf * xf, axis=-1, keepdims=True) + EPS)
    return (y * w.astype(jnp.float32)).astype(x.dtype)


def _l2norm(x):
    return x * lax.rsqrt(jnp.sum(x * x, axis=-1, keepdims=True) + EPS)


def _causal_conv(x, w):
    K = w.shape[0]
    T = x.shape[1]
    xp = jnp.pad(x, ((0, 0), (K - 1, 0), (0, 0)))
    y = xp[:, 0:T] * w[0]
    for kk in range(1, K):
        y = y + xp[:, kk:kk + T] * w[kk]
    return y


def _to_chunks(a, nc):
    B = a.shape[0]
    H = a.shape[2]
    a = a.reshape((B, nc, CHUNK, H) + a.shape[3:])
    return jnp.moveaxis(a, (1, 3), (0, 2))


def _from_chunks(o):
    nc, B, H, C, Dv = o.shape
    return jnp.moveaxis(o, (0, 2), (1, 3)).reshape(B, nc * C, H, Dv)


def _gdn_chunked(q, k, v, beta, g):
    B, T, H, DK = q.shape
    DV = v.shape[-1]
    nc = T // CHUNK
    q = _to_chunks(q * (DK ** -0.5), nc)
    k = _to_chunks(k, nc)
    v = _to_chunks(v, nc)
    beta = _to_chunks(beta, nc)
    gc = jnp.cumsum(_to_chunks(g, nc), axis=-1)
    causal = jnp.tril(jnp.ones((CHUNK, CHUNK), dtype=bool))
    strict = jnp.tril(jnp.ones((CHUNK, CHUNK), dtype=bool), k=-1)
    decay = jnp.exp(jnp.where(causal, gc[..., :, None] - gc[..., None, :], -jnp.inf))
    kb = k * beta[..., None]
    L = jnp.where(strict, jnp.einsum('nbhik,nbhjk->nbhij', kb, k) * decay, 0.0)
    rhs = jnp.concatenate([v * beta[..., None], kb * jnp.exp(gc)[..., None]], axis=-1)
    sol = lax.linalg.triangular_solve(L, rhs, left_side=True, lower=True, unit_diagonal=True)
    u = sol[..., :DV]
    w = sol[..., DV:]
    a_qk = jnp.where(causal, jnp.einsum('nbhik,nbhjk->nbhij', q, k) * decay, 0.0)
    q_dec = q * jnp.exp(gc)[..., None]
    k_dec = k * jnp.exp(gc[..., -1:] - gc)[..., None]
    chunk_decay = jnp.exp(gc[..., -1])

    def step(S, xs):
        qd, a_c, u_c, w_c, kd, dl = xs
        v_new = u_c - jnp.einsum('bhck,bhkv->bhcv', w_c, S)
        o = jnp.einsum('bhck,bhkv->bhcv', qd, S) + jnp.einsum('bhij,bhjv->bhiv', a_c, v_new)
        S = S * dl[..., None, None] + jnp.einsum('bhck,bhcv->bhkv', kd, v_new)
        return S, o

    S0 = jnp.zeros((B, H, DK, DV), jnp.float32)
    _, o = lax.scan(step, S0, (q_dec, a_qk, u, w, k_dec, chunk_decay))
    return _from_chunks(o)


def _hgrn2_chunked(q, k, v, g):
    B, T, H, DK = q.shape
    DV = v.shape[-1]
    nc = T // CHUNK
    q = _to_chunks(q * (DK ** -0.5), nc)
    k = _to_chunks(k, nc)
    v = _to_chunks(v, nc)
    g = _to_chunks(g, nc)
    gc = jnp.cumsum(g, axis=-2)
    gb = (gc - g)[:, :, :, ::SUB]
    pos = jnp.arange(CHUNK)
    off_mask = (pos[None, :] // SUB) < jnp.arange(N_SUB)[:, None]
    diag_mask = jnp.tril(jnp.ones((SUB, SUB), dtype=bool))

    def step(S, xs):
        q_c, k_c, v_c, gc_c, gb_c = xs
        qs = q_c.reshape(B, H, N_SUB, SUB, DK)
        ks = k_c.reshape(B, H, N_SUB, SUB, DK)
        vs = v_c.reshape(B, H, N_SUB, SUB, DV)
        gs = gc_c.reshape(B, H, N_SUB, SUB, DK)
        q_off = qs * jnp.exp(gs - gb_c[:, :, :, None])
        k_off = k_c[:, :, None] * jnp.exp(
            jnp.where(off_mask[:, :, None], gb_c[:, :, :, None] - gc_c[:, :, None], -jnp.inf))
        a_off = jnp.einsum('bhsik,bhsjk->bhsij', q_off, k_off)
        dec = jnp.exp(jnp.where(diag_mask[:, :, None],
                                gs[..., :, None, :] - gs[..., None, :, :], -jnp.inf))
        a_diag = jnp.einsum('bhsik,bhsjk,bhsijk->bhsij', qs, ks, dec)
        o = (jnp.einsum('bhsij,bhjv->bhsiv', a_off, v_c)
             + jnp.einsum('bhsij,bhsjv->bhsiv', a_diag, vs)).reshape(B, H, CHUNK, DV)
        o = o + jnp.einsum('bhck,bhkv->bhcv', q_c * jnp.exp(gc_c), S)
        g_last = gc_c[:, :, -1]
        S = S * jnp.exp(g_last)[..., None] + jnp.einsum(
            'bhck,bhcv->bhkv', k_c * jnp.exp(g_last[:, :, None] - gc_c), v_c)
        return S, o

    S0 = jnp.zeros((B, H, DK, DV), jnp.float32)
    _, o = lax.scan(step, S0, (q, k, v, gc, gb))
    return _from_chunks(o)


def _gated_deltanet(y, w_in, conv_w, a_log, dt_bias, onorm_w, w_out):
    B, T, _ = y.shape
    f32 = jnp.float32
    proj = y @ w_in
    qkv = proj[..., :GDN_CONV_WIDTH]
    gate = proj[..., GDN_CONV_WIDTH:GDN_CONV_WIDTH + GDN_VAL_WIDTH]
    a = proj[..., GDN_CONV_WIDTH + GDN_VAL_WIDTH:GDN_CONV_WIDTH + GDN_VAL_WIDTH + GDN_HEADS]
    b = proj[..., GDN_CONV_WIDTH + GDN_VAL_WIDTH + GDN_HEADS:]
    qkv = jax.nn.silu(_causal_conv(qkv, conv_w)).astype(f32)
    q = _l2norm(qkv[..., :GDN_KEY_WIDTH].reshape(B, T, GDN_HEADS, GDN_HEAD_DIM))
    k = _l2norm(qkv[..., GDN_KEY_WIDTH:2 * GDN_KEY_WIDTH].reshape(B, T, GDN_HEADS, GDN_HEAD_DIM))
    v = qkv[..., 2 * GDN_KEY_WIDTH:].reshape(B, T, GDN_HEADS, GDN_HEAD_DIM)
    beta = jax.nn.sigmoid(b.astype(f32))
    g = -jnp.exp(a_log.astype(f32)) * jax.nn.softplus(a.astype(f32) + dt_bias.astype(f32))
    o = _gdn_chunked(q, k, v, beta, g)
    o = _rmsnorm(o, onorm_w) * jax.nn.silu(gate.astype(f32)).reshape(B, T, GDN_HEADS, GDN_HEAD_DIM)
    return o.reshape(B, T, GDN_VAL_WIDTH).astype(y.dtype) @ w_out


def _hgrn2(y, w_in, lb, gnorm_w, w_out):
    B, T, _ = y.shape
    f32 = jnp.float32
    proj = y @ w_in
    q, f, i, gate = jnp.split(proj, 4, axis=-1)
    f = f.astype(f32)
    log_forget = jnp.logaddexp(jnp.log(lb), jnp.log1p(-lb) + jax.nn.log_sigmoid(f))
    k = (1.0 - lb) * jax.nn.sigmoid(-f)
    q = jax.nn.silu(q.astype(f32))
    heads = lambda t: t.reshape(B, T, HGRN_HEADS, -1)
    o = _hgrn2_chunked(heads(q), heads(k), heads(i.astype(f32)), heads(log_forget))
    o = _rmsnorm(o.reshape(B, T, HGRN_WIDTH), gnorm_w) * jax.nn.silu(gate.astype(f32))
    return o.astype(y.dtype) @ w_out


def _sq_relu_mlp(h, w_up, w_down):
    return jnp.square(jax.nn.relu(h @ w_up)) @ w_down


def _fwd_setup_inputs(seed: int = 0) -> dict:
    key = jax.random.key(seed)
    ks = jax.random.split(key, 17)
    f32 = jnp.float32
    nrm = lambda kk, shape: jax.random.normal(kk, shape, f32)
    x = nrm(ks[0], (BATCH, SEQ, D_MODEL))
    gdn_w_in = nrm(ks[1], (N_GDN, D_MODEL, GDN_IN_WIDTH)) * D_MODEL ** -0.5
    gdn_conv = nrm(ks[2], (N_GDN, CONV_K, GDN_CONV_WIDTH)) * CONV_K ** -0.5
    gdn_a_log = jnp.log(jax.random.uniform(ks[3], (N_GDN, GDN_HEADS), f32, 1.0, 16.0))
    dt = jnp.exp(jax.random.uniform(ks[4], (N_GDN, GDN_HEADS), f32, math.log(1e-3), math.log(1e-1)))
    gdn_dt_bias = dt + jnp.log(-jnp.expm1(-dt))
    gdn_onorm = 1.0 + 0.02 * nrm(ks[5], (N_GDN, GDN_HEAD_DIM))
    gdn_w_out = nrm(ks[6], (N_GDN, GDN_VAL_WIDTH, D_MODEL)) * GDN_VAL_WIDTH ** -0.5
    hgrn_w_in = nrm(ks[7], (N_HGRN, D_MODEL, HGRN_IN_WIDTH)) * D_MODEL ** -0.5
    hgrn_lb_logits = 0.1 * nrm(ks[8], (DEPTH, HGRN_WIDTH))
    hgrn_gnorm = 1.0 + 0.02 * nrm(ks[9], (N_HGRN, HGRN_WIDTH))
    hgrn_w_out = nrm(ks[10], (N_HGRN, HGRN_WIDTH, D_MODEL)) * HGRN_WIDTH ** -0.5
    norm_mix = 1.0 + 0.02 * nrm(ks[11], (DEPTH, D_MODEL))
    norm_mlp = 1.0 + 0.02 * nrm(ks[12], (DEPTH, D_MODEL))
    mlp_w_up = nrm(ks[13], (DEPTH, D_MODEL, MLP_HIDDEN)) * D_MODEL ** -0.5
    mlp_w_down = nrm(ks[14], (DEPTH, MLP_HIDDEN, D_MODEL)) * MLP_HIDDEN ** -0.5
    norm_final = 1.0 + 0.02 * nrm(ks[15], (D_MODEL,))
    return {"x": x, "gdn_w_in": gdn_w_in, "gdn_conv": gdn_conv, "gdn_a_log": gdn_a_log,
            "gdn_dt_bias": gdn_dt_bias, "gdn_onorm": gdn_onorm, "gdn_w_out": gdn_w_out,
            "hgrn_w_in": hgrn_w_in, "hgrn_lb_logits": hgrn_lb_logits, "hgrn_gnorm": hgrn_gnorm,
            "hgrn_w_out": hgrn_w_out, "norm_mix": norm_mix, "norm_mlp": norm_mlp,
            "mlp_w_up": mlp_w_up, "mlp_w_down": mlp_w_down, "norm_final": norm_final}


def _fwd_reference(x, gdn_w_in, gdn_conv, gdn_a_log, gdn_dt_bias, gdn_onorm, gdn_w_out,
              hgrn_w_in, hgrn_lb_logits, hgrn_gnorm, hgrn_w_out, norm_mix, norm_mlp,
              mlp_w_up, mlp_w_down, norm_final):
    sm = jax.nn.softmax(hgrn_lb_logits.astype(jnp.float32), axis=0)
    lower_bounds = jnp.cumsum(sm, axis=0) - sm[0]
    h = x
    for i in range(DEPTH):
        j = i // N_MIXERS
        y = _rmsnorm(h, norm_mix[i])
        if i % N_MIXERS == 0:
            y = _gated_deltanet(y, gdn_w_in[j], gdn_conv[j], gdn_a_log[j], gdn_dt_bias[j],
                                gdn_onorm[j], gdn_w_out[j])
        else:
            y = _hgrn2(y, hgrn_w_in[j], lower_bounds[i], hgrn_gnorm[j], hgrn_w_out[j])
        h = h + y.astype(h.dtype)
        h = h + _sq_relu_mlp(_rmsnorm(h, norm_mlp[i]), mlp_w_up[i], mlp_w_down[i]).astype(h.dtype)
    return _rmsnorm(h, norm_final)


import jax as _jax
import jax.numpy as _jnp

TWIN_FORMAT = 'train_step'
FWD_PARAMS = ['x', 'gdn_w_in', 'gdn_conv', 'gdn_a_log', 'gdn_dt_bias', 'gdn_onorm', 'gdn_w_out', 'hgrn_w_in', 'hgrn_lb_logits', 'hgrn_gnorm', 'hgrn_w_out', 'norm_mix', 'norm_mlp', 'mlp_w_up', 'mlp_w_down', 'norm_final']
TWIN_WEIGHTS = ['gdn_w_in', 'gdn_conv', 'gdn_a_log', 'gdn_dt_bias', 'gdn_onorm', 'gdn_w_out', 'hgrn_w_in', 'hgrn_lb_logits', 'hgrn_gnorm', 'hgrn_w_out', 'norm_mix', 'norm_mlp', 'mlp_w_up', 'mlp_w_down', 'norm_final']
TWIN_DIFF_INPUT = 'x'
TWIN_INPUTS = ['x', 'gdn_w_in', 'gdn_conv', 'gdn_a_log', 'gdn_dt_bias', 'gdn_onorm', 'gdn_w_out', 'hgrn_w_in', 'hgrn_lb_logits', 'hgrn_gnorm', 'hgrn_w_out', 'norm_mix', 'norm_mlp', 'mlp_w_up', 'mlp_w_down', 'norm_final', 'loss_target', 'm_gdn_w_in', 'm_gdn_conv', 'm_gdn_a_log', 'm_gdn_dt_bias', 'm_gdn_onorm', 'm_gdn_w_out', 'm_hgrn_w_in', 'm_hgrn_lb_logits', 'm_hgrn_gnorm', 'm_hgrn_w_out', 'm_norm_mix', 'm_norm_mlp', 'm_mlp_w_up', 'm_mlp_w_down', 'm_norm_final', 'v_gdn_w_in', 'v_gdn_conv', 'v_gdn_a_log', 'v_gdn_dt_bias', 'v_gdn_onorm', 'v_gdn_w_out', 'v_hgrn_w_in', 'v_hgrn_lb_logits', 'v_hgrn_gnorm', 'v_hgrn_w_out', 'v_norm_mix', 'v_norm_mlp', 'v_mlp_w_up', 'v_mlp_w_down', 'v_norm_final']
TWIN_OUTPUTS = ['loss', 'grad_x', 'grad_gdn_w_in', 'grad_gdn_conv', 'grad_gdn_a_log', 'grad_gdn_dt_bias', 'grad_gdn_onorm', 'grad_gdn_w_out', 'grad_hgrn_w_in', 'grad_hgrn_lb_logits', 'grad_hgrn_gnorm', 'grad_hgrn_w_out', 'grad_norm_mix', 'grad_norm_mlp', 'grad_mlp_w_up', 'grad_mlp_w_down', 'grad_norm_final', 'delta_gdn_w_in', 'delta_gdn_conv', 'delta_gdn_a_log', 'delta_gdn_dt_bias', 'delta_gdn_onorm', 'delta_gdn_w_out', 'delta_hgrn_w_in', 'delta_hgrn_lb_logits', 'delta_hgrn_gnorm', 'delta_hgrn_w_out', 'delta_norm_mix', 'delta_norm_mlp', 'delta_mlp_w_up', 'delta_mlp_w_down', 'delta_norm_final', 'new_m_gdn_w_in', 'new_m_gdn_conv', 'new_m_gdn_a_log', 'new_m_gdn_dt_bias', 'new_m_gdn_onorm', 'new_m_gdn_w_out', 'new_m_hgrn_w_in', 'new_m_hgrn_lb_logits', 'new_m_hgrn_gnorm', 'new_m_hgrn_w_out', 'new_m_norm_mix', 'new_m_norm_mlp', 'new_m_mlp_w_up', 'new_m_mlp_w_down', 'new_m_norm_final', 'new_v_gdn_w_in', 'new_v_gdn_conv', 'new_v_gdn_a_log', 'new_v_gdn_dt_bias', 'new_v_gdn_onorm', 'new_v_gdn_w_out', 'new_v_hgrn_w_in', 'new_v_hgrn_lb_logits', 'new_v_hgrn_gnorm', 'new_v_hgrn_w_out', 'new_v_norm_mix', 'new_v_norm_mlp', 'new_v_mlp_w_up', 'new_v_mlp_w_down', 'new_v_norm_final']
TWIN_LEAF_KINDS = {'loss': 'loss', 'grad_x': 'grad_x', 'grad_gdn_w_in': 'grad_w', 'grad_gdn_conv': 'grad_w', 'grad_gdn_a_log': 'grad_w', 'grad_gdn_dt_bias': 'grad_w', 'grad_gdn_onorm': 'grad_w', 'grad_gdn_w_out': 'grad_w', 'grad_hgrn_w_in': 'grad_w', 'grad_hgrn_lb_logits': 'grad_w', 'grad_hgrn_gnorm': 'grad_w', 'grad_hgrn_w_out': 'grad_w', 'grad_norm_mix': 'grad_w', 'grad_norm_mlp': 'grad_w', 'grad_mlp_w_up': 'grad_w', 'grad_mlp_w_down': 'grad_w', 'grad_norm_final': 'grad_w', 'delta_gdn_w_in': 'delta_w', 'delta_gdn_conv': 'delta_w', 'delta_gdn_a_log': 'delta_w', 'delta_gdn_dt_bias': 'delta_w', 'delta_gdn_onorm': 'delta_w', 'delta_gdn_w_out': 'delta_w', 'delta_hgrn_w_in': 'delta_w', 'delta_hgrn_lb_logits': 'delta_w', 'delta_hgrn_gnorm': 'delta_w', 'delta_hgrn_w_out': 'delta_w', 'delta_norm_mix': 'delta_w', 'delta_norm_mlp': 'delta_w', 'delta_mlp_w_up': 'delta_w', 'delta_mlp_w_down': 'delta_w', 'delta_norm_final': 'delta_w', 'new_m_gdn_w_in': 'new_m', 'new_m_gdn_conv': 'new_m', 'new_m_gdn_a_log': 'new_m', 'new_m_gdn_dt_bias': 'new_m', 'new_m_gdn_onorm': 'new_m', 'new_m_gdn_w_out': 'new_m', 'new_m_hgrn_w_in': 'new_m', 'new_m_hgrn_lb_logits': 'new_m', 'new_m_hgrn_gnorm': 'new_m', 'new_m_hgrn_w_out': 'new_m', 'new_m_norm_mix': 'new_m', 'new_m_norm_mlp': 'new_m', 'new_m_mlp_w_up': 'new_m', 'new_m_mlp_w_down': 'new_m', 'new_m_norm_final': 'new_m', 'new_v_gdn_w_in': 'new_v', 'new_v_gdn_conv': 'new_v', 'new_v_gdn_a_log': 'new_v', 'new_v_gdn_dt_bias': 'new_v', 'new_v_gdn_onorm': 'new_v', 'new_v_gdn_w_out': 'new_v', 'new_v_hgrn_w_in': 'new_v', 'new_v_hgrn_lb_logits': 'new_v', 'new_v_hgrn_gnorm': 'new_v', 'new_v_hgrn_w_out': 'new_v', 'new_v_norm_mix': 'new_v', 'new_v_norm_mlp': 'new_v', 'new_v_mlp_w_up': 'new_v', 'new_v_mlp_w_down': 'new_v', 'new_v_norm_final': 'new_v'}


def _forward(args):
    return _fwd_reference(*[args[k] for k in FWD_PARAMS])


def _output_shape():
    out = _jax.eval_shape(lambda: _forward(_fwd_setup_inputs(0)))
    return out.shape, out.dtype

N_MICROBATCH = 1
ADAM_LR = 0.001
ADAM_B1 = 0.9
ADAM_B2 = 0.999
ADAM_EPS = 1e-08
ADAM_WD = 0.01
ADAM_STEP = 10
PER_EXAMPLE_BATCH_AXIS = {'x': 0, 'loss_target': 0}
SHARED_INPUTS = []
_WEIGHT_DTYPES = {'gdn_w_in': _jnp.float32, 'gdn_conv': _jnp.float32, 'gdn_a_log': _jnp.float32, 'gdn_dt_bias': _jnp.float32, 'gdn_onorm': _jnp.float32, 'gdn_w_out': _jnp.float32, 'hgrn_w_in': _jnp.float32, 'hgrn_lb_logits': _jnp.float32, 'hgrn_gnorm': _jnp.float32, 'hgrn_w_out': _jnp.float32, 'norm_mix': _jnp.float32, 'norm_mlp': _jnp.float32, 'mlp_w_up': _jnp.float32, 'mlp_w_down': _jnp.float32, 'norm_final': _jnp.float32}
MOMENT_SCALE = {'gdn_w_in': 8.199741e-02, 'gdn_conv': 7.569127e-02, 'gdn_a_log': 8.922085e-01, 'gdn_dt_bias': 8.337814e-01, 'gdn_onorm': 2.856936e-01, 'gdn_w_out': 9.717848e-02, 'hgrn_w_in': 4.449546e-02, 'hgrn_lb_logits': 3.013119e-03, 'hgrn_gnorm': 6.041339e-02, 'hgrn_w_out': 6.253359e-02, 'norm_mix': 1.319635e-01, 'norm_mlp': 1.455090e-01, 'mlp_w_up': 6.863879e-02, 'mlp_w_down': 1.328966e-01, 'norm_final': 3.268797e+01}


def _to_microbatches(a, axis):
    t = _jnp.moveaxis(a, axis, 0)
    t = t.reshape((N_MICROBATCH, t.shape[0] // N_MICROBATCH) + t.shape[1:])
    return _jnp.moveaxis(t, 1, axis + 1)


def setup_inputs(seed: int = 0) -> dict:
    inp = _fwd_setup_inputs(seed)
    key = _jax.random.fold_in(_jax.random.key(seed), 7919)
    shape, _ = _output_shape()
    out = dict(inp)
    out["loss_target"] = _jax.random.normal(_jax.random.fold_in(key, 0), shape, _jnp.float32)
    for i, name in enumerate(TWIN_WEIGHTS):
        w = inp[name].astype(_jnp.float32)
        if MOMENT_SCALE is None:
            s = _jnp.sqrt(_jnp.mean(_jnp.square(w)) + 1e-30)
        else:
            s = MOMENT_SCALE[name]
        km, kv = _jax.random.split(_jax.random.fold_in(key, i + 1))
        out[name] = w
        out["m_" + name] = s * _jax.random.normal(km, w.shape, _jnp.float32)
        out["v_" + name] = (s * s) * _jax.random.uniform(kv, w.shape, _jnp.float32, 0.5, 1.5)
    if N_MICROBATCH > 1:
        for name, axis in PER_EXAMPLE_BATCH_AXIS.items():
            out[name] = _to_microbatches(out[name], axis)
    return {'x': out['x'], 'gdn_w_in': out['gdn_w_in'], 'gdn_conv': out['gdn_conv'], 'gdn_a_log': out['gdn_a_log'], 'gdn_dt_bias': out['gdn_dt_bias'], 'gdn_onorm': out['gdn_onorm'], 'gdn_w_out': out['gdn_w_out'], 'hgrn_w_in': out['hgrn_w_in'], 'hgrn_lb_logits': out['hgrn_lb_logits'], 'hgrn_gnorm': out['hgrn_gnorm'], 'hgrn_w_out': out['hgrn_w_out'], 'norm_mix': out['norm_mix'], 'norm_mlp': out['norm_mlp'], 'mlp_w_up': out['mlp_w_up'], 'mlp_w_down': out['mlp_w_down'], 'norm_final': out['norm_final'], 'loss_target': out['loss_target'], 'm_gdn_w_in': out['m_gdn_w_in'], 'm_gdn_conv': out['m_gdn_conv'], 'm_gdn_a_log': out['m_gdn_a_log'], 'm_gdn_dt_bias': out['m_gdn_dt_bias'], 'm_gdn_onorm': out['m_gdn_onorm'], 'm_gdn_w_out': out['m_gdn_w_out'], 'm_hgrn_w_in': out['m_hgrn_w_in'], 'm_hgrn_lb_logits': out['m_hgrn_lb_logits'], 'm_hgrn_gnorm': out['m_hgrn_gnorm'], 'm_hgrn_w_out': out['m_hgrn_w_out'], 'm_norm_mix': out['m_norm_mix'], 'm_norm_mlp': out['m_norm_mlp'], 'm_mlp_w_up': out['m_mlp_w_up'], 'm_mlp_w_down': out['m_mlp_w_down'], 'm_norm_final': out['m_norm_final'], 'v_gdn_w_in': out['v_gdn_w_in'], 'v_gdn_conv': out['v_gdn_conv'], 'v_gdn_a_log': out['v_gdn_a_log'], 'v_gdn_dt_bias': out['v_gdn_dt_bias'], 'v_gdn_onorm': out['v_gdn_onorm'], 'v_gdn_w_out': out['v_gdn_w_out'], 'v_hgrn_w_in': out['v_hgrn_w_in'], 'v_hgrn_lb_logits': out['v_hgrn_lb_logits'], 'v_hgrn_gnorm': out['v_hgrn_gnorm'], 'v_hgrn_w_out': out['v_hgrn_w_out'], 'v_norm_mix': out['v_norm_mix'], 'v_norm_mlp': out['v_norm_mlp'], 'v_mlp_w_up': out['v_mlp_w_up'], 'v_mlp_w_down': out['v_mlp_w_down'], 'v_norm_final': out['v_norm_final']}


def _loss(weights, diff, rest, loss_target):
    with _jax.named_scope("forward"):
        args = {**rest, TWIN_DIFF_INPUT: diff, **{k: w.astype(_WEIGHT_DTYPES[k]) for k, w in weights.items()}}
        y = _forward(args)
    with _jax.named_scope("loss_head"):
        err = _jnp.square(y.astype(_jnp.float32) - loss_target)
        return 0.5 * _jnp.sum(_jnp.mean(err, axis=-1)) if err.ndim else 0.5 * err


def _adamw(w, g, m, v):
    m = ADAM_B1 * m + (1.0 - ADAM_B1) * g
    v = ADAM_B2 * v + (1.0 - ADAM_B2) * _jnp.square(g)
    m_hat = m / (1.0 - ADAM_B1 ** ADAM_STEP)
    v_hat = v / (1.0 - ADAM_B2 ** ADAM_STEP)
    delta = -ADAM_LR * (m_hat / (_jnp.sqrt(v_hat) + ADAM_EPS) + ADAM_WD * w)
    return delta, m, v


def reference(x, gdn_w_in, gdn_conv, gdn_a_log, gdn_dt_bias, gdn_onorm, gdn_w_out, hgrn_w_in, hgrn_lb_logits, hgrn_gnorm, hgrn_w_out, norm_mix, norm_mlp, mlp_w_up, mlp_w_down, norm_final, loss_target, m_gdn_w_in, m_gdn_conv, m_gdn_a_log, m_gdn_dt_bias, m_gdn_onorm, m_gdn_w_out, m_hgrn_w_in, m_hgrn_lb_logits, m_hgrn_gnorm, m_hgrn_w_out, m_norm_mix, m_norm_mlp, m_mlp_w_up, m_mlp_w_down, m_norm_final, v_gdn_w_in, v_gdn_conv, v_gdn_a_log, v_gdn_dt_bias, v_gdn_onorm, v_gdn_w_out, v_hgrn_w_in, v_hgrn_lb_logits, v_hgrn_gnorm, v_hgrn_w_out, v_norm_mix, v_norm_mlp, v_mlp_w_up, v_mlp_w_down, v_norm_final):
    given = dict(x=x, gdn_w_in=gdn_w_in, gdn_conv=gdn_conv, gdn_a_log=gdn_a_log, gdn_dt_bias=gdn_dt_bias, gdn_onorm=gdn_onorm, gdn_w_out=gdn_w_out, hgrn_w_in=hgrn_w_in, hgrn_lb_logits=hgrn_lb_logits, hgrn_gnorm=hgrn_gnorm, hgrn_w_out=hgrn_w_out, norm_mix=norm_mix, norm_mlp=norm_mlp, mlp_w_up=mlp_w_up, mlp_w_down=mlp_w_down, norm_final=norm_final, loss_target=loss_target, m_gdn_w_in=m_gdn_w_in, m_gdn_conv=m_gdn_conv, m_gdn_a_log=m_gdn_a_log, m_gdn_dt_bias=m_gdn_dt_bias, m_gdn_onorm=m_gdn_onorm, m_gdn_w_out=m_gdn_w_out, m_hgrn_w_in=m_hgrn_w_in, m_hgrn_lb_logits=m_hgrn_lb_logits, m_hgrn_gnorm=m_hgrn_gnorm, m_hgrn_w_out=m_hgrn_w_out, m_norm_mix=m_norm_mix, m_norm_mlp=m_norm_mlp, m_mlp_w_up=m_mlp_w_up, m_mlp_w_down=m_mlp_w_down, m_norm_final=m_norm_final, v_gdn_w_in=v_gdn_w_in, v_gdn_conv=v_gdn_conv, v_gdn_a_log=v_gdn_a_log, v_gdn_dt_bias=v_gdn_dt_bias, v_gdn_onorm=v_gdn_onorm, v_gdn_w_out=v_gdn_w_out, v_hgrn_w_in=v_hgrn_w_in, v_hgrn_lb_logits=v_hgrn_lb_logits, v_hgrn_gnorm=v_hgrn_gnorm, v_hgrn_w_out=v_hgrn_w_out, v_norm_mix=v_norm_mix, v_norm_mlp=v_norm_mlp, v_mlp_w_up=v_mlp_w_up, v_mlp_w_down=v_mlp_w_down, v_norm_final=v_norm_final)
    weights = {n: given[n] for n in TWIN_WEIGHTS}
    shared = {n: given[n] for n in SHARED_INPUTS}
    per_example = {n: given[n] for n in ['x']}
    grad_fn = _jax.value_and_grad(_loss, argnums=(0, 1))

    def one_microbatch(ex, loss_target):
        ex = dict(ex)
        diff = ex.pop(TWIN_DIFF_INPUT)
        return grad_fn(weights, diff, {**shared, **ex}, loss_target)

    if N_MICROBATCH == 1:
        loss, (grad_w, grad_x) = one_microbatch(per_example, given["loss_target"])
    else:
        def body(carry, xs):
            loss_sum, grad_sum = carry
            l_k, (gw_k, gx_k) = one_microbatch(xs[0], xs[1])
            with _jax.named_scope("update"):
                return (loss_sum + l_k, _jax.tree.map(_jnp.add, grad_sum, gw_k)), gx_k

        init = (_jnp.zeros((), _jnp.float32), _jax.tree.map(_jnp.zeros_like, weights))
        (loss, grad_w), grad_x = _jax.lax.scan(body, init, (per_example, given["loss_target"]))
    with _jax.named_scope("update"):
        delta_w, new_m, new_v = {}, {}, {}
        for n in TWIN_WEIGHTS:
            delta_w[n], new_m[n], new_v[n] = _adamw(weights[n], grad_w[n], given["m_" + n], given["v_" + n])
    return (loss, grad_x, *[grad_w[n] for n in TWIN_WEIGHTS], *[delta_w[n] for n in TWIN_WEIGHTS],
            *[new_m[n] for n in TWIN_WEIGHTS], *[new_v[n] for n in TWIN_WEIGHTS])
```

```python
import functools
import math

import jax
import jax.numpy as jnp
from jax import lax
from jax.experimental import pallas as pl
from jax.experimental.pallas import tpu as pltpu

F32 = jnp.float32
BF16 = jnp.bfloat16
HI = lax.Precision.HIGHEST

D_MODEL = 1024
HEADS = 8
HEAD_DIM = 128
CHUNK = 64
SUB = 16
CONV_K = 4
HALO = 8
DEPTH = 4
EPS = 1e-6
MLP_HIDDEN = 4 * D_MODEL
GDN_MAIN = 4 * D_MODEL
GDN_IN = GDN_MAIN + 2 * HEADS
GDN_IN_PAD = GDN_MAIN + 128
NEG = -1e30

ADAM_LR = 0.001
ADAM_B1 = 0.9
ADAM_B2 = 0.999
ADAM_EPS = 1e-08
ADAM_WD = 0.01
ADAM_STEP = 10

VMEM_LIMIT = 48 * 1024 * 1024

MESH = pl.DeviceIdType.MESH


def _cparams(sem=None, **kw):
    if sem is not None:
        kw["dimension_semantics"] = sem
    return pltpu.CompilerParams(vmem_limit_bytes=VMEM_LIMIT, **kw)


def _iota(shape, dim):
    return lax.broadcasted_iota(jnp.int32, shape, dim)


def _mm(a, b):
    return lax.dot_general(a, b, (((1,), (0,)), ((), ())), precision=HI, preferred_element_type=F32)


def _mm_nt(a, b):
    return lax.dot_general(a, b, (((1,), (1,)), ((), ())), precision=HI, preferred_element_type=F32)


def _mm_tn(a, b):
    return lax.dot_general(a, b, (((0,), (0,)), ((), ())), precision=HI, preferred_element_type=F32)


@functools.partial(jax.custom_vjp, nondiff_argnums=(1,))
def _roll_rows(x, d):
    return pltpu.roll(x, d, 0)


def _roll_rows_fwd(x, d):
    return pltpu.roll(x, d, 0), None


def _roll_rows_bwd(d, _, g):
    return (pltpu.roll(g, g.shape[0] - d, 0),)


_roll_rows.defvjp(_roll_rows_fwd, _roll_rows_bwd)


def _sigmoid(x):
    return 1.0 / (1.0 + jnp.exp(-x))


def _silu(x):
    return x * _sigmoid(x)


def _softplus(x):
    return jnp.maximum(x, 0.0) + jnp.log(1.0 + jnp.exp(-jnp.abs(x)))


def _log_sigmoid(x):
    return jnp.minimum(x, 0.0) - jnp.log(1.0 + jnp.exp(-jnp.abs(x)))


def _logaddexp(a, b):
    return jnp.maximum(a, b) + jnp.log(1.0 + jnp.exp(-jnp.abs(a - b)))


def _row_to_col(row):
    n = row.shape[1]
    eye = _iota((n, n), 0) == _iota((n, n), 1)
    return jnp.sum(jnp.where(eye, jnp.broadcast_to(row, (n, n)), 0.0), axis=1, keepdims=True)


def _col_to_row(col):
    n = col.shape[0]
    eye = _iota((n, n), 0) == _iota((n, n), 1)
    return jnp.sum(jnp.where(eye, jnp.broadcast_to(col, (n, n)), 0.0), axis=0, keepdims=True)


def _pick_row(x, r):
    return jnp.sum(jnp.where(_iota(x.shape, 0) == r, x, 0.0), axis=0, keepdims=True)


def _pick_lane(x, l):
    return jnp.sum(jnp.where(_iota(x.shape, 1) == l, x, 0.0), axis=1, keepdims=True)


def _unit_lower_inverse(L):
    n = L.shape[0]
    r, c = _iota((n, n), 0), _iota((n, n), 1)
    eye = jnp.where(r == c, 1.0, 0.0).astype(F32)
    Ld = jnp.where((r // SUB) == (c // SUB), L, 0.0)
    Lo = L - Ld
    P = eye - Ld
    Lp = Ld
    for _ in range(int(math.log2(SUB)) - 1):
        Lp = _mm(Lp, Lp)
        P = P + _mm(P, Lp)
    N = _mm(P, Lo)
    X = eye - N
    X = X + _mm(X, _mm(N, N))
    return _mm(X, P)


def _shift_rows(x, halo, d):
    if d == 0:
        return x
    xr = _roll_rows(x, d)
    hr = _roll_rows(halo, d)
    hfull = jnp.concatenate([hr, jnp.zeros((x.shape[0] - HALO, x.shape[1]), F32)], axis=0)
    return jnp.where(_iota(x.shape, 0) >= d, xr, hfull)


def _causal_conv_chunk(x, halo, w):
    y = None
    for kk in range(CONV_K):
        t = _shift_rows(x, halo, CONV_K - 1 - kk) * _pick_row(w, kk)
        y = t if y is None else y + t
    return y


def _gdn_chunk(xq, xk, xv, hq, hk, hv, gate, ab, S, cwq, cwk, cwv, alog, dtb, onorm, *, head):
    C = xq.shape[0]
    q = _silu(_causal_conv_chunk(xq, hq, cwq))
    k = _silu(_causal_conv_chunk(xk, hk, cwk))
    v = _silu(_causal_conv_chunk(xv, hv, cwv))
    q = q * lax.rsqrt(jnp.sum(q * q, axis=1, keepdims=True) + EPS) * (HEAD_DIM ** -0.5)
    k = k * lax.rsqrt(jnp.sum(k * k, axis=1, keepdims=True) + EPS)
    a_col = _pick_lane(ab, head)
    b_col = _pick_lane(ab, HEADS + head)
    alog_h = _pick_lane(alog, head)
    dtb_h = _pick_lane(dtb, head)
    beta = _sigmoid(b_col)
    g = -jnp.exp(alog_h) * _softplus(a_col + dtb_h)
    r, c = _iota((C, C), 0), _iota((C, C), 1)
    g_row = _col_to_row(g)
    gc = jnp.sum(jnp.where(c <= r, jnp.broadcast_to(g_row, (C, C)), 0.0), axis=1, keepdims=True)
    gc_row = jnp.sum(jnp.where(r <= c, jnp.broadcast_to(g, (C, C)), 0.0), axis=0, keepdims=True)
    decay = jnp.exp(jnp.where(r >= c, gc - gc_row, NEG))
    kb = k * beta
    L = jnp.where(r > c, _mm_nt(kb, k) * decay, 0.0)
    T = _unit_lower_inverse(L)
    egc = jnp.exp(gc)
    u = _mm(T, v * beta)
    w = _mm(T, kb * egc)
    A = jnp.where(r >= c, _mm_nt(q, k) * decay, 0.0)
    gl = _pick_row(gc, C - 1)
    v_new = u - _mm(w, S)
    o = _mm(q * egc, S) + _mm(A, v_new)
    S_next = S * jnp.exp(gl) + _mm_tn(k * jnp.exp(gl - gc), v_new)
    o = o * lax.rsqrt(jnp.mean(o * o, axis=1, keepdims=True) + EPS) * onorm
    return o * _silu(gate), S_next


def _hgrn_lower_bound(lbl, layer):
    e = jnp.exp(lbl - jnp.max(lbl, axis=0, keepdims=True))
    sm = e / jnp.sum(e, axis=0, keepdims=True)
    r = _iota(lbl.shape, 0)
    return jnp.sum(jnp.where((r >= 1) & (r <= layer), sm, 0.0), axis=0, keepdims=True)


def _hgrn_chunk(qp, fp, v, S, lbl, *, layer):
    C = qp.shape[0]
    lb = _hgrn_lower_bound(lbl, layer)
    lf = _logaddexp(jnp.log(lb), jnp.log(1.0 - lb) + _log_sigmoid(fp))
    k = (1.0 - lb) * _sigmoid(-fp)
    q = _silu(qp) * (HEAD_DIM ** -0.5)
    r, c = _iota((C, C), 0), _iota((C, C), 1)
    gc = _mm(jnp.where(r >= c, 1.0, 0.0).astype(F32), lf)
    gml = gc - lf
    row = _iota(qp.shape, 0)
    gb_rows = jnp.zeros_like(gc)
    gbs = []
    for s in range(C // SUB):
        gb_s = _pick_row(gml, s * SUB)
        gbs.append(gb_s)
        gb_rows = jnp.where((row // SUB) == s, gb_s, gb_rows)
    q_off = q * jnp.exp(gc - gb_rows)
    a_off = jnp.zeros((C, C), F32)
    for s in range(1, C // SUB):
        k_off = k * jnp.exp(jnp.where(row < s * SUB, gbs[s] - gc, NEG))
        a_off = a_off + jnp.where((r // SUB) == s, _mm_nt(q_off, k_off), 0.0)
    o = _mm(a_off, v) + _mm(q * jnp.exp(gc), S)
    for d in range(SUB):
        k_d = _roll_rows(k, d) if d else k
        gc_d = _roll_rows(gc, d) if d else gc
        v_d = _roll_rows(v, d) if d else v
        e = jnp.exp(jnp.where((row % SUB) >= d, gc - gc_d, NEG))
        o = o + jnp.sum(q * k_d * e, axis=1, keepdims=True) * v_d
    gl = _pick_row(gc, C - 1)
    S_next = S * _row_to_col(jnp.exp(gl)) + _mm_tn(k * jnp.exp(gl - gc), v)
    return o, S_next


def _chunk_specs(nc, col0, width=HEAD_DIM):
    return pl.BlockSpec((CHUNK, width), lambda b, n, h: (b * nc + n, col0 + h))


def _chunk_specs_rev(nc, col0):
    return pl.BlockSpec((CHUNK, HEAD_DIM), lambda b, n, h: (b * nc + (nc - 1 - n), col0 + h))


def _halo_spec(nc, col0, rev):
    per = CHUNK // HALO

    def imap(b, n, h):
        nn = (nc - 1 - n) if rev else n
        return (jnp.maximum((b * nc + nn) * per - 1, 0), col0 + h)
    return pl.BlockSpec((HALO, HEAD_DIM), imap)


def _const_spec(shape):
    nd = len(shape)
    return pl.BlockSpec(shape, lambda b, n, h: (0,) * nd)


def gdn_forward(proj, conv_w, alog, dtb, onorm, bsz):
    n_tok = proj.shape[0]
    nc = n_tok // bsz // CHUNK

    def body(xq, xk, xv, hq, hk, hv, gate, ab, cwq, cwk, cwv, al, dt, on, o_ref, ssave_ref, s_ref):
        n, h = pl.program_id(1), pl.program_id(2)

        @pl.when(n == 0)
        def _():
            s_ref[h] = jnp.zeros((HEAD_DIM, HEAD_DIM), F32)

        S = s_ref[h]
        ssave_ref[...] = S
        keep = jnp.where(n > 0, 1.0, 0.0).astype(F32)
        out, s_next = _gdn_chunk(xq[...], xk[...], xv[...], hq[...] * keep, hk[...] * keep, hv[...] * keep,
                                 gate[...], ab[...], S, cwq[...], cwk[...], cwv[...], al[...], dt[...], on[...],
                                 head=h)
        o_ref[...] = out.astype(o_ref.dtype)
        s_ref[h] = s_next

    hb = HEADS
    in_specs = [_chunk_specs(nc, 0), _chunk_specs(nc, hb), _chunk_specs(nc, 2 * hb),
                _halo_spec(nc, 0, False), _halo_spec(nc, hb, False), _halo_spec(nc, 2 * hb, False),
                _chunk_specs(nc, 3 * hb),
                pl.BlockSpec((CHUNK, 128), lambda b, n, h: (b * nc + n, 4 * hb)),
                pl.BlockSpec((CONV_K, HEAD_DIM), lambda b, n, h: (0, h)),
                pl.BlockSpec((CONV_K, HEAD_DIM), lambda b, n, h: (0, hb + h)),
                pl.BlockSpec((CONV_K, HEAD_DIM), lambda b, n, h: (0, 2 * hb + h)),
                _const_spec((1, 128)), _const_spec((1, 128)), _const_spec((1, 128))]
    out_specs = [_chunk_specs(nc, 0),
                 pl.BlockSpec((None, None, None, HEAD_DIM, HEAD_DIM), lambda b, n, h: (b, n, h, 0, 0))]
    return pl.pallas_call(
        body, name="gdn_fwd", grid=(bsz, nc, HEADS), in_specs=in_specs, out_specs=out_specs,
        out_shape=(jax.ShapeDtypeStruct((n_tok, D_MODEL), BF16),
                   jax.ShapeDtypeStruct((bsz, nc, HEADS, HEAD_DIM, HEAD_DIM), F32)),
        scratch_shapes=[pltpu.VMEM((HEADS, HEAD_DIM, HEAD_DIM), F32)],
        compiler_params=_cparams(("arbitrary", "arbitrary", "arbitrary")),
    )(proj, proj, proj, proj, proj, proj, proj, proj, conv_w, conv_w, conv_w, alog, dtb, onorm)


def gdn_backward(proj, conv_w, alog, dtb, onorm, s_saved, d_out, bsz):
    n_tok = proj.shape[0]
    nc = n_tok // bsz // CHUNK

    def body(xq, xk, xv, hq, hk, hv, gate, ab, cwq, cwk, cwv, al, dt, on, ssave, do,
             dq_ref, dk_ref, dv_ref, dg_ref, dab_ref, dcw_ref, dal_ref, ddt_ref, don_ref, ds_ref, dhalo_ref):
        b, n, h = pl.program_id(0), pl.program_id(1), pl.program_id(2)
        nr = nc - 1 - n

        @pl.when((b == 0) & (n == 0) & (h == 0))
        def _():
            dcw_ref[...] = jnp.zeros_like(dcw_ref)
            dal_ref[...] = jnp.zeros_like(dal_ref)
            ddt_ref[...] = jnp.zeros_like(ddt_ref)
            don_ref[...] = jnp.zeros_like(don_ref)

        @pl.when(n == 0)
        def _():
            ds_ref[h] = jnp.zeros((HEAD_DIM, HEAD_DIM), F32)
            dhalo_ref[h] = jnp.zeros((3, HALO, HEAD_DIM), F32)

        keep = jnp.where(nr > 0, 1.0, 0.0).astype(F32)
        args = (xq[...], xk[...], xv[...], hq[...] * keep, hk[...] * keep, hv[...] * keep, gate[...], ab[...],
                ssave[...], cwq[...], cwk[...], cwv[...], al[...], dt[...], on[...])
        _, vjp = jax.vjp(functools.partial(_gdn_chunk, head=h), *args)
        (gxq, gxk, gxv, ghq, ghk, ghv, ggate, gab, gS, gcq, gck, gcv, gal, gdt, gon) = vjp((do[...], ds_ref[h]))
        pad = jnp.zeros((CHUNK - HALO, HEAD_DIM), F32)
        dq_ref[...] = gxq + jnp.concatenate([pad, dhalo_ref[h, 0]], axis=0)
        dk_ref[...] = gxk + jnp.concatenate([pad, dhalo_ref[h, 1]], axis=0)
        dv_ref[...] = gxv + jnp.concatenate([pad, dhalo_ref[h, 2]], axis=0)
        dg_ref[...] = ggate

        @pl.when(h == 0)
        def _():
            dab_ref[...] = gab

        @pl.when(h > 0)
        def _():
            dab_ref[...] += gab

        dcw_ref[h, 0] += gcq
        dcw_ref[h, 1] += gck
        dcw_ref[h, 2] += gcv
        dal_ref[...] += gal
        ddt_ref[...] += gdt
        don_ref[...] += gon
        ds_ref[h] = gS
        dhalo_ref[h, 0] = ghq * keep
        dhalo_ref[h, 1] = ghk * keep
        dhalo_ref[h, 2] = ghv * keep

    hb = HEADS
    in_specs = [_chunk_specs_rev(nc, 0), _chunk_specs_rev(nc, hb), _chunk_specs_rev(nc, 2 * hb),
                _halo_spec(nc, 0, True), _halo_spec(nc, hb, True), _halo_spec(nc, 2 * hb, True),
                _chunk_specs_rev(nc, 3 * hb),
                pl.BlockSpec((CHUNK, 128), lambda b, n, h: (b * nc + (nc - 1 - n), 4 * hb)),
                pl.BlockSpec((CONV_K, HEAD_DIM), lambda b, n, h: (0, h)),
                pl.BlockSpec((CONV_K, HEAD_DIM), lambda b, n, h: (0, hb + h)),
                pl.BlockSpec((CONV_K, HEAD_DIM), lambda b, n, h: (0, 2 * hb + h)),
                _const_spec((1, 128)), _const_spec((1, 128)), _const_spec((1, 128)),
                pl.BlockSpec((None, None, None, HEAD_DIM, HEAD_DIM), lambda b, n, h: (b, nc - 1 - n, h, 0, 0)),
                _chunk_specs_rev(nc, 0)]
    tok = jax.ShapeDtypeStruct((n_tok, D_MODEL), F32)
    out_specs = [_chunk_specs_rev(nc, 0)] * 4 + [
        pl.BlockSpec((CHUNK, 128), lambda b, n, h: (b * nc + (nc - 1 - n), 0)),
        _const_spec((HEADS, 3, CONV_K, HEAD_DIM)), _const_spec((1, 128)), _const_spec((1, 128)), _const_spec((1, 128))]
    row = jax.ShapeDtypeStruct((1, 128), F32)
    return pl.pallas_call(
        body, name="gdn_bwd", grid=(bsz, nc, HEADS), in_specs=in_specs, out_specs=out_specs,
        out_shape=(tok, tok, tok, tok, jax.ShapeDtypeStruct((n_tok, 128), F32),
                   jax.ShapeDtypeStruct((HEADS, 3, CONV_K, HEAD_DIM), F32), row, row, row),
        scratch_shapes=[pltpu.VMEM((HEADS, HEAD_DIM, HEAD_DIM), F32), pltpu.VMEM((HEADS, 3, HALO, HEAD_DIM), F32)],
        compiler_params=_cparams(("arbitrary", "arbitrary", "arbitrary")),
    )(proj, proj, proj, proj, proj, proj, proj, proj, conv_w, conv_w, conv_w, alog, dtb, onorm, s_saved, d_out)


def hgrn_forward(proj, lbl, layer, bsz):
    n_tok = proj.shape[0]
    nc = n_tok // bsz // CHUNK

    def body(qp, fp, vi, lb, o_ref, ssave_ref, s_ref):
        n, h = pl.program_id(1), pl.program_id(2)

        @pl.when(n == 0)
        def _():
            s_ref[h] = jnp.zeros((HEAD_DIM, HEAD_DIM), F32)

        S = s_ref[h]
        ssave_ref[...] = S
        o, s_next = _hgrn_chunk(qp[...], fp[...], vi[...], S, lb[...], layer=layer)
        o_ref[...] = o
        s_ref[h] = s_next

    hb = HEADS
    in_specs = [_chunk_specs(nc, 0), _chunk_specs(nc, hb), _chunk_specs(nc, 2 * hb),
                pl.BlockSpec((DEPTH, HEAD_DIM), lambda b, n, h: (0, h))]
    out_specs = [_chunk_specs(nc, 0),
                 pl.BlockSpec((None, None, None, HEAD_DIM, HEAD_DIM), lambda b, n, h: (b, n, h, 0, 0))]
    return pl.pallas_call(
        body, name=f"hgrn_fwd{layer}", grid=(bsz, nc, HEADS), in_specs=in_specs, out_specs=out_specs,
        out_shape=(jax.ShapeDtypeStruct((n_tok, D_MODEL), F32),
                   jax.ShapeDtypeStruct((bsz, nc, HEADS, HEAD_DIM, HEAD_DIM), F32)),
        scratch_shapes=[pltpu.VMEM((HEADS, HEAD_DIM, HEAD_DIM), F32)],
        compiler_params=_cparams(("arbitrary", "arbitrary", "arbitrary")),
    )(proj, proj, proj, lbl)


def hgrn_backward(proj, lbl, s_saved, d_o, layer, bsz):
    n_tok = proj.shape[0]
    nc = n_tok // bsz // CHUNK

    def body(qp, fp, vi, lb, ssave, do, dq_ref, df_ref, dv_ref, dlb_ref, ds_ref):
        b, n, h = pl.program_id(0), pl.program_id(1), pl.program_id(2)

        @pl.when((b == 0) & (n == 0) & (h == 0))
        def _():
            dlb_ref[...] = jnp.zeros_like(dlb_ref)

        @pl.when(n == 0)
        def _():
            ds_ref[h] = jnp.zeros((HEAD_DIM, HEAD_DIM), F32)

        _, vjp = jax.vjp(functools.partial(_hgrn_chunk, layer=layer), qp[...], fp[...], vi[...], ssave[...], lb[...])
        gq, gf, gv, gS, glb = vjp((do[...], ds_ref[h]))
        dq_ref[...] = gq
        df_ref[...] = gf
        dv_ref[...] = gv
        dlb_ref[h] += glb
        ds_ref[h] = gS

    hb = HEADS
    in_specs = [_chunk_specs_rev(nc, 0), _chunk_specs_rev(nc, hb), _chunk_specs_rev(nc, 2 * hb),
                pl.BlockSpec((DEPTH, HEAD_DIM), lambda b, n, h: (0, h)),
                pl.BlockSpec((None, None, None, HEAD_DIM, HEAD_DIM), lambda b, n, h: (b, nc - 1 - n, h, 0, 0)),
                _chunk_specs_rev(nc, 0)]
    tok = jax.ShapeDtypeStruct((n_tok, D_MODEL), F32)
    out_specs = [_chunk_specs_rev(nc, 0)] * 3 + [_const_spec((HEADS, DEPTH, HEAD_DIM))]
    return pl.pallas_call(
        body, name=f"hgrn_bwd{layer}", grid=(bsz, nc, HEADS), in_specs=in_specs, out_specs=out_specs,
        out_shape=(tok, tok, tok, jax.ShapeDtypeStruct((HEADS, DEPTH, HEAD_DIM), F32)),
        scratch_shapes=[pltpu.VMEM((HEADS, HEAD_DIM, HEAD_DIM), F32)],
        compiler_params=_cparams(("arbitrary", "arbitrary", "arbitrary")),
    )(proj, proj, proj, lbl, s_saved, d_o)


ROW_TILE = 512


def _tile(n):
    for cand in (1024, 512, 1408, 384, 256, 128):
        if n % cand == 0:
            return cand
    return n


def _rmsnorm(x, w):
    return x * lax.rsqrt(jnp.mean(x * x, axis=1, keepdims=True) + EPS) * w


def norm_matmul(h, nw, w, relu2, name):
    n_tok, d = h.shape
    n_out = w.shape[1]
    tm, tn = min(n_tok, ROW_TILE), _tile(n_out)

    def body(h_ref, nw_ref, w_ref, y_ref, *outs):
        @pl.when(pl.program_id(1) == 0)
        def _():
            y_ref[...] = _rmsnorm(h_ref[...], nw_ref[...]).astype(BF16)

        acc = jnp.dot(y_ref[...], w_ref[...], preferred_element_type=F32)
        if relu2:
            r = jnp.maximum(acc, 0.0)
            outs[0][...] = (r * r).astype(BF16)
            outs[1][...] = r.astype(BF16)
        else:
            outs[0][...] = acc

    o_spec = pl.BlockSpec((tm, tn), lambda i, j: (i, j))
    if relu2:
        outs = (jax.ShapeDtypeStruct((n_tok, n_out), BF16),) * 2
    else:
        outs = (jax.ShapeDtypeStruct((n_tok, n_out), F32),)
    return pl.pallas_call(
        body, name=name, grid=(n_tok // tm, n_out // tn),
        in_specs=[pl.BlockSpec((tm, d), lambda i, j: (i, 0)), pl.BlockSpec((1, d), lambda i, j: (0, 0)),
                  pl.BlockSpec((d, tn), lambda i, j: (0, j))],
        out_specs=[pl.BlockSpec((tm, d), lambda i, j: (i, 0))] + [o_spec] * len(outs),
        out_shape=(jax.ShapeDtypeStruct((n_tok, d), BF16),) + outs,
        compiler_params=_cparams(("parallel", "arbitrary")),
    )(h, nw, w)


def matmul(a, b, mode, name, out_dtype=F32, extra=None, epilogue=None, shards=1):
    if mode == "nn":
        (m, k), n = a.shape, b.shape[1]
    elif mode == "nt":
        (m, k), n = a.shape, b.shape[0]
    else:
        (k, m), n = a.shape, b.shape[1]
    tm, tk = min(m, ROW_TILE), _tile(k)
    tn = _tile(n // shards)
    nk = k // tk
    a_spec = pl.BlockSpec((tk, tm), lambda i, j, kk: (kk, i)) if mode == "tn" else pl.BlockSpec((tm, tk), lambda i, j, kk: (i, kk))
    b_spec = pl.BlockSpec((tn, tk), lambda i, j, kk: (j, kk)) if mode == "nt" else pl.BlockSpec((tk, tn), lambda i, j, kk: (kk, j))
    dims = {"nn": (((1,), (0,)), ((), ())), "nt": (((1,), (1,)), ((), ())), "tn": (((0,), (0,)), ((), ()))}[mode]

    def body(*refs):
        if extra is not None:
            a_ref, b_ref, e_ref, o_ref, acc_ref = refs
        else:
            a_ref, b_ref, o_ref, acc_ref = refs
        kk = pl.program_id(2)

        @pl.when(kk == 0)
        def _():
            acc_ref[...] = jnp.zeros_like(acc_ref)

        acc_ref[...] += lax.dot_general(a_ref[...].astype(BF16), b_ref[...].astype(BF16), dims,
                                        preferred_element_type=F32)

        @pl.when(kk == nk - 1)
        def _():
            acc = acc_ref[...]
            if epilogue == "add":
                acc = e_ref[...] + acc
            elif epilogue == "mul2":
                acc = acc * (2.0 * e_ref[...].astype(F32))
            o_ref[...] = acc.astype(o_ref.dtype)

    in_specs = [a_spec, b_spec]
    args = [a, b]
    if extra is not None:
        in_specs.append(pl.BlockSpec((tm, tn), lambda i, j, kk: (i, j)))
        args.append(extra)
    if shards > 1:
        per = n // shards // tn
        out_spec = pl.BlockSpec((None, tm, tn), lambda i, j, kk: (j // per, i, j % per))
        out_shape = jax.ShapeDtypeStruct((shards, m, n // shards), out_dtype)
    else:
        out_spec = pl.BlockSpec((tm, tn), lambda i, j, kk: (i, j))
        out_shape = jax.ShapeDtypeStruct((m, n), out_dtype)
    return pl.pallas_call(
        body, name=name, grid=(m // tm, n // tn, nk), in_specs=in_specs, out_specs=out_spec, out_shape=out_shape,
        scratch_shapes=[pltpu.VMEM((tm, tn), F32)],
        compiler_params=_cparams(("parallel", "parallel", "arbitrary")),
    )(*args)


def norm_backward(h, nw, dy, dres, name):
    n_tok, d = h.shape
    tm = min(n_tok, ROW_TILE)

    def body(h_ref, nw_ref, dy_ref, dr_ref, dh_ref, dnw_ref):
        @pl.when(pl.program_id(0) == 0)
        def _():
            dnw_ref[...] = jnp.zeros_like(dnw_ref)

        _, vjp = jax.vjp(_rmsnorm, h_ref[...], nw_ref[...])
        gh, gw = vjp(dy_ref[...])
        dh_ref[...] = dr_ref[...] + gh
        dnw_ref[...] += gw

    row = pl.BlockSpec((tm, d), lambda i: (i, 0))
    vec = pl.BlockSpec((1, d), lambda i: (0, 0))
    return pl.pallas_call(
        body, name=name, grid=(n_tok // tm,), in_specs=[row, vec, row, row], out_specs=[row, vec],
        out_shape=(jax.ShapeDtypeStruct((n_tok, d), F32), jax.ShapeDtypeStruct((1, d), F32)),
        compiler_params=_cparams(("arbitrary",)),
    )(h, nw, dy, dres)


def _hgrn_post(o, gate, gw):
    return _rmsnorm(o, gw) * _silu(gate)


def hgrn_post_forward(o, proj, gw, name):
    n_tok, d = o.shape
    tm = min(n_tok, ROW_TILE)

    def body(o_ref, g_ref, w_ref, y_ref):
        y_ref[...] = _hgrn_post(o_ref[...], g_ref[...], w_ref[...]).astype(BF16)

    row = pl.BlockSpec((tm, d), lambda i: (i, 0))
    return pl.pallas_call(
        body, name=name, grid=(n_tok // tm,),
        in_specs=[row, pl.BlockSpec((tm, d), lambda i: (i, 3)), pl.BlockSpec((1, d), lambda i: (0, 0))],
        out_specs=row, out_shape=jax.ShapeDtypeStruct((n_tok, d), BF16),
        compiler_params=_cparams(("parallel",)),
    )(o, proj, gw)


def hgrn_post_backward(o, proj, gw, dy, name):
    n_tok, d = o.shape
    tm = min(n_tok, ROW_TILE)

    def body(o_ref, g_ref, w_ref, dy_ref, do_ref, dg_ref, dw_ref):
        @pl.when(pl.program_id(0) == 0)
        def _():
            dw_ref[...] = jnp.zeros_like(dw_ref)

        _, vjp = jax.vjp(_hgrn_post, o_ref[...], g_ref[...], w_ref[...])
        go, gg, gw_ = vjp(dy_ref[...])
        do_ref[...] = go
        dg_ref[...] = gg
        dw_ref[...] += gw_

    row = pl.BlockSpec((tm, d), lambda i: (i, 0))
    vec = pl.BlockSpec((1, d), lambda i: (0, 0))
    tok = jax.ShapeDtypeStruct((n_tok, d), F32)
    return pl.pallas_call(
        body, name=name, grid=(n_tok // tm,),
        in_specs=[row, pl.BlockSpec((tm, d), lambda i: (i, 3)), vec, row], out_specs=[row, row, vec],
        out_shape=(tok, tok, jax.ShapeDtypeStruct((1, d), F32)),
        compiler_params=_cparams(("arbitrary",)),
    )(o, proj, gw, dy)


def loss_head(h, nw, target):
    n_tok, d = h.shape
    tm = min(n_tok, ROW_TILE)

    def body(h_ref, nw_ref, t_ref, loss_ref, dh_ref, dnw_ref):
        @pl.when(pl.program_id(0) == 0)
        def _():
            dnw_ref[...] = jnp.zeros_like(dnw_ref)
            loss_ref[...] = jnp.zeros_like(loss_ref)

        out, vjp = jax.vjp(_rmsnorm, h_ref[...], nw_ref[...])
        err = out - t_ref[...]
        part = 0.5 * jnp.sum(jnp.sum(err * err, axis=1, keepdims=True), axis=0, keepdims=True) / d
        loss_ref[...] += jnp.broadcast_to(part, loss_ref.shape)
        gh, gw = vjp(err / d)
        dh_ref[...] = gh
        dnw_ref[...] += gw

    row = pl.BlockSpec((tm, d), lambda i: (i, 0))
    vec = pl.BlockSpec((1, d), lambda i: (0, 0))
    return pl.pallas_call(
        body, name="loss_head", grid=(n_tok // tm,), in_specs=[row, vec, row],
        out_specs=[pl.BlockSpec((1, 128), lambda i: (0, 0)), row, vec],
        out_shape=(jax.ShapeDtypeStruct((1, 128), F32), jax.ShapeDtypeStruct((n_tok, d), F32),
                   jax.ShapeDtypeStruct((1, d), F32)),
        compiler_params=_cparams(("arbitrary",)),
    )(h, nw, target)


def _rows2d(shape):
    if len(shape) == 1:
        return (1, shape[0])
    return (math.prod(shape[:-1]), shape[-1])


def adamw(w, g, m, v, name):
    shape = w.shape
    r, c = _rows2d(shape)
    tr = r if r <= 256 else 256
    c1 = 1.0 / (1.0 - ADAM_B1 ** ADAM_STEP)
    c2 = 1.0 / (1.0 - ADAM_B2 ** ADAM_STEP)

    def body(w_ref, g_ref, m_ref, v_ref, d_ref, nm_ref, nv_ref):
        gg = g_ref[...]
        nm = ADAM_B1 * m_ref[...] + (1.0 - ADAM_B1) * gg
        nv = ADAM_B2 * v_ref[...] + (1.0 - ADAM_B2) * (gg * gg)
        d_ref[...] = -ADAM_LR * ((nm * c1) / (jnp.sqrt(nv * c2) + ADAM_EPS) + ADAM_WD * w_ref[...])
        nm_ref[...] = nm
        nv_ref[...] = nv

    spec = pl.BlockSpec((tr, c), lambda i: (i, 0))
    sds = jax.ShapeDtypeStruct((r, c), F32)
    outs = pl.pallas_call(
        body, name=name, grid=(r // tr,), in_specs=[spec] * 4, out_specs=[spec] * 3, out_shape=(sds,) * 3,
        compiler_params=_cparams(("parallel",)),
    )(w.reshape(r, c), g.reshape(r, c), m.reshape(r, c), v.reshape(r, c))
    return tuple(o.reshape(shape) for o in outs)


def add_slots(parts, name):
    s, r, c = parts.shape
    tr = r if r <= 256 else 256

    def body(p_ref, o_ref):
        acc = p_ref[0]
        for t in range(1, s):
            acc = acc + p_ref[t]
        o_ref[...] = acc

    return pl.pallas_call(
        body, name=name, grid=(r // tr,), in_specs=[pl.BlockSpec((s, tr, c), lambda i: (0, i, 0))],
        out_specs=pl.BlockSpec((tr, c), lambda i: (i, 0)), out_shape=jax.ShapeDtypeStruct((r, c), F32),
        compiler_params=_cparams(("parallel",)),
    )(parts)


def add_pair(a, b, name):
    s, r, c = a.shape
    tr = r if r <= 256 else 256

    def body(a_ref, b_ref, o_ref):
        o_ref[...] = a_ref[...] + b_ref[...]

    spec = pl.BlockSpec((None, tr, c), lambda t, i: (t, i, 0))
    return pl.pallas_call(
        body, name=name, grid=(s, r // tr), in_specs=[spec, spec], out_specs=spec,
        out_shape=jax.ShapeDtypeStruct((s, r, c), F32), compiler_params=_cparams(("parallel", "parallel")),
    )(a, b)


N_CHIPS = 4
N_DEV = 8
_ANY = pl.BlockSpec(memory_space=pl.ANY)


def _mesh_pos():
    return lax.axis_index("x"), lax.axis_index("y"), lax.axis_index("c")


def _other_chips(x, y):
    ps = [(1 - x, y), (x, 1 - y), (1 - x, 1 - y)]
    return [(p, 2 * p[0] + p[1]) for p in ps]


def _remote(src, dst, send_sem, recv_sem, dev):
    return pltpu.make_async_remote_copy(src_ref=src, dst_ref=dst, send_sem=send_sem, recv_sem=recv_sem,
                                        device_id=dev, device_id_type=MESH)


def gather_chips(arrs, name):
    n = len(arrs)

    def body(*refs):
        ins, outs = refs[:n], refs[n:2 * n]
        send_sems, recv_sems, local_sems = refs[2 * n:]
        x, y, c = _mesh_pos()
        me = 2 * x + y
        peers = _other_chips(x, y)
        locs, sends = [], []
        for a in range(n):
            cp = pltpu.make_async_copy(ins[a], outs[a].at[me], local_sems.at[a])
            cp.start()
            locs.append(cp)
            for j, ((px, py), _) in enumerate(peers):
                r = _remote(ins[a], outs[a].at[me], send_sems.at[a, j], recv_sems.at[a, j], (px, py, c))
                r.start()
                sends.append(r)
        for a in range(n):
            for j, ((px, py), t) in enumerate(peers):
                _remote(ins[a], outs[a].at[t], send_sems.at[a, j], recv_sems.at[a, j], (px, py, c)).wait_recv()
        for r in sends:
            r.wait_send()
        for cp in locs:
            cp.wait()

    return pl.pallas_call(
        body, name=name, in_specs=[_ANY] * n, out_specs=[_ANY] * n,
        out_shape=[jax.ShapeDtypeStruct((N_CHIPS,) + a.shape, a.dtype) for a in arrs],
        scratch_shapes=[pltpu.SemaphoreType.DMA((n, 3)), pltpu.SemaphoreType.DMA((n, 3)), pltpu.SemaphoreType.DMA((n,))],
    )(*arrs)


def exchange_sibling_half(gs, name):
    n = len(gs)

    def body(*refs):
        ins, outs = refs[:n], refs[n:2 * n]
        send_sems, recv_sems = refs[2 * n:]
        x, y, c = _mesh_pos()
        sends = []
        for a in range(n):
            half = gs[a].shape[1] // 2
            src = ins[a].at[:, pl.ds((1 - c) * half, half), :]
            r = _remote(src, outs[a], send_sems.at[a], recv_sems.at[a], (x, y, 1 - c))
            r.start()
            sends.append(r)
        for r in sends:
            r.wait_recv()
        for r in sends:
            r.wait_send()

    return pl.pallas_call(
        body, name=name, in_specs=[_ANY] * n, out_specs=[_ANY] * n,
        out_shape=[jax.ShapeDtypeStruct((g.shape[0], g.shape[1] // 2, g.shape[2]), g.dtype) for g in gs],
        scratch_shapes=[pltpu.SemaphoreType.DMA((n,)), pltpu.SemaphoreType.DMA((n,))],
    )(*gs)


def exchange_chips(ps, name):
    n = len(ps)

    def body(*refs):
        ins, outs = refs[:n], refs[n:2 * n]
        send_sems, recv_sems, local_sems = refs[2 * n:]
        x, y, c = _mesh_pos()
        me = 2 * x + y
        peers = _other_chips(x, y)
        locs, sends = [], []
        for a in range(n):
            cp = pltpu.make_async_copy(ins[a].at[me], outs[a].at[me], local_sems.at[a])
            cp.start()
            locs.append(cp)
            for j, ((px, py), t) in enumerate(peers):
                r = _remote(ins[a].at[t], outs[a].at[me], send_sems.at[a, j], recv_sems.at[a, j], (px, py, c))
                r.start()
                sends.append(r)
        for a in range(n):
            for j, ((px, py), t) in enumerate(peers):
                _remote(ins[a].at[t], outs[a].at[t], send_sems.at[a, j], recv_sems.at[a, j], (px, py, c)).wait_recv()
        for r in sends:
            r.wait_send()
        for cp in locs:
            cp.wait()

    return pl.pallas_call(
        body, name=name, in_specs=[_ANY] * n, out_specs=[_ANY] * n,
        out_shape=[jax.ShapeDtypeStruct(p.shape, p.dtype) for p in ps],
        scratch_shapes=[pltpu.SemaphoreType.DMA((n, 3)), pltpu.SemaphoreType.DMA((n, 3)), pltpu.SemaphoreType.DMA((n,))],
    )(*ps)


def share_sibling(groups, name):
    flat = [f for grp in groups for f in grp]
    n = len(flat)
    nw = len(groups)

    def body(*refs):
        ins, outs = refs[:n], refs[n:n + nw]
        send_sems, recv_sems, local_sems = refs[n + nw:]
        x, y, c = _mesh_pos()
        locs, sends, k = [], [], 0
        for w, grp in enumerate(groups):
            for l in range(len(grp)):
                half = grp[l].shape[0]
                dst = outs[w].at[l, pl.ds(c * half, half), :]
                cp = pltpu.make_async_copy(ins[k], dst, local_sems.at[k])
                cp.start()
                locs.append(cp)
                r = _remote(ins[k], dst, send_sems.at[k], recv_sems.at[k], (x, y, 1 - c))
                r.start()
                sends.append(r)
                k += 1
        k = 0
        for w, grp in enumerate(groups):
            for l in range(len(grp)):
                half = grp[l].shape[0]
                theirs = outs[w].at[l, pl.ds((1 - c) * half, half), :]
                _remote(ins[k], theirs, send_sems.at[k], recv_sems.at[k], (x, y, 1 - c)).wait_recv()
                k += 1
        for r in sends:
            r.wait_send()
        for cp in locs:
            cp.wait()

    return pl.pallas_call(
        body, name=name, in_specs=[_ANY] * n, out_specs=[_ANY] * nw,
        out_shape=[jax.ShapeDtypeStruct((len(grp), 2 * grp[0].shape[0], grp[0].shape[1]), F32) for grp in groups],
        scratch_shapes=[pltpu.SemaphoreType.DMA((n,)), pltpu.SemaphoreType.DMA((n,)), pltpu.SemaphoreType.DMA((n,))],
    )(*flat)


def gather_all(small, name):
    flips = [(fx, fy, fc) for fx in (0, 1) for fy in (0, 1) for fc in (0, 1)][1:]

    def body(in_ref, out_ref, send_sems, recv_sems, local_sem):
        x, y, c = _mesh_pos()
        me = 4 * x + 2 * y + c
        cp = pltpu.make_async_copy(in_ref, out_ref.at[me], local_sem)
        cp.start()
        peers = [((1 - x) if fx else x, (1 - y) if fy else y, (1 - c) if fc else c) for fx, fy, fc in flips]
        sends = []
        for k, p in enumerate(peers):
            r = _remote(in_ref, out_ref.at[me], send_sems.at[k], recv_sems.at[k], p)
            r.start()
            sends.append(r)
        for k, p in enumerate(peers):
            _remote(in_ref, out_ref.at[4 * p[0] + 2 * p[1] + p[2]], send_sems.at[k], recv_sems.at[k], p).wait_recv()
        for r in sends:
            r.wait_send()
        cp.wait()

    return pl.pallas_call(
        body, name=name, in_specs=[_ANY], out_specs=_ANY,
        out_shape=jax.ShapeDtypeStruct((N_DEV,) + small.shape, small.dtype),
        scratch_shapes=[pltpu.SemaphoreType.DMA((N_DEV - 1,)), pltpu.SemaphoreType.DMA((N_DEV - 1,)),
                        pltpu.SemaphoreType.DMA],
    )(small)


BIG = ("gdn_w_in", "gdn_w_out", "hgrn_w_in", "hgrn_w_out", "mlp_w_up", "mlp_w_down")
WEIGHTS = ("gdn_w_in", "gdn_conv", "gdn_a_log", "gdn_dt_bias", "gdn_onorm", "gdn_w_out", "hgrn_w_in", "hgrn_lb_logits",
           "hgrn_gnorm", "hgrn_w_out", "norm_mix", "norm_mlp", "mlp_w_up", "mlp_w_down", "norm_final")
SMALL_ROWS = 48


def _pad_lanes(v, n):
    return jnp.pad(v, [(0, 0)] * (v.ndim - 1) + [(0, n - v.shape[-1])])


def _local_grads(x, target, w):
    bsz, t_len, d = x.shape
    n_tok = bsz * t_len
    h = x.reshape(n_tok, d)
    lbl = w["hgrn_lb_logits"]
    saved = []
    for i in range(DEPTH):
        j = i // 2
        nmix = w["norm_mix"][i][None, :]
        if i % 2 == 0:
            y, proj = norm_matmul(h, nmix, w["gdn_w_in"][j], False, f"in_proj{i}")
            al = _pad_lanes(w["gdn_a_log"][j][None, :], 128)
            dtb = _pad_lanes(w["gdn_dt_bias"][j][None, :], 128)
            on = w["gdn_onorm"][j][None, :]
            og, ssave = gdn_forward(proj, w["gdn_conv"][j], al, dtb, on, bsz)
            mix = (proj, ssave, al, dtb, on)
            w_out = w["gdn_w_out"][j]
        else:
            y, proj = norm_matmul(h, nmix, w["hgrn_w_in"][j], False, f"in_proj{i}")
            o, ssave = hgrn_forward(proj, lbl, i, bsz)
            gn = w["hgrn_gnorm"][j][None, :]
            og = hgrn_post_forward(o, proj, gn, f"hgrn_post{i}")
            mix = (proj, ssave, o, gn)
            w_out = w["hgrn_w_out"][j]
        h1 = matmul(og, w_out, "nn", f"out_proj{i}", extra=h, epilogue="add")
        nmlp = w["norm_mlp"][i][None, :]
        z, a, r = norm_matmul(h1, nmlp, w["mlp_w_up"][i], True, f"mlp_up{i}")
        h2 = matmul(a, w["mlp_w_down"][i], "nn", f"mlp_down{i}", extra=h1, epilogue="add")
        saved.append((h, nmix, y, mix, og, w_out, h1, nmlp, z, a, r))
        h = h2
    loss_row, dh, d_nf = loss_head(h, w["norm_final"][None, :], target.reshape(n_tok, d))

    big = {k: [None] * (DEPTH if k.startswith("mlp") else DEPTH // 2) for k in BIG}
    d_nmix, d_nmlp = [None] * DEPTH, [None] * DEPTH
    d_conv, d_alog, d_dtb, d_onorm, d_gnorm = [None] * 2, [None] * 2, [None] * 2, [None] * 2, [None] * 2
    d_lbl = jnp.zeros((DEPTH, d), F32)
    for i in reversed(range(DEPTH)):
        j = i // 2
        h_in, nmix, y, mix, og, w_out, h1, nmlp, z, a, r = saved[i]
        du = matmul(dh, w["mlp_w_down"][i], "nt", f"d_mlp_act{i}", out_dtype=BF16, extra=r, epilogue="mul2")
        big["mlp_w_down"][i] = matmul(a, dh, "tn", f"dw_down{i}").reshape(N_CHIPS, -1, d)
        dz = matmul(du, w["mlp_w_up"][i], "nt", f"d_mlp_in{i}")
        big["mlp_w_up"][i] = matmul(z, du, "tn", f"dw_up{i}", shards=N_CHIPS)
        dh1, d_nmlp[i] = norm_backward(h1, nmlp, dz, dh, f"d_norm_mlp{i}")
        dog = matmul(dh1, w_out, "nt", f"d_mix_out{i}")
        dw_out = matmul(og, dh1, "tn", f"dw_out{i}").reshape(N_CHIPS, -1, d)
        if i % 2 == 0:
            proj, ssave, al, dtb, on = mix
            dq, dk, dv, dg, dab, dcw, dal, ddt, don = gdn_backward(proj, w["gdn_conv"][j], al, dtb, on, ssave, dog, bsz)
            dproj = jnp.concatenate([dq, dk, dv, dg, dab], axis=1)
            dy = matmul(dproj, w["gdn_w_in"][j], "nt", f"d_in_proj{i}")
            dw_in = matmul(y, dproj, "tn", f"dw_in{i}")[:, :GDN_IN]
            big["gdn_w_in"][j] = jnp.transpose(dw_in.reshape(d, N_CHIPS, GDN_IN // N_CHIPS), (1, 0, 2))
            big["gdn_w_out"][j] = dw_out
            d_conv[j] = jnp.transpose(dcw, (2, 1, 0, 3)).reshape(CONV_K, 3 * d)
            d_alog[j], d_dtb[j], d_onorm[j] = dal[0, :HEADS], ddt[0, :HEADS], don[0]
        else:
            proj, ssave, o, gn = mix
            do_raw, dgate, dgn = hgrn_post_backward(o, proj, gn, dog, f"d_hgrn_post{i}")
            dq, df, dv, dlb = hgrn_backward(proj, lbl, ssave, do_raw, i, bsz)
            dproj = jnp.concatenate([dq, df, dv, dgate], axis=1)
            dy = matmul(dproj, w["hgrn_w_in"][j], "nt", f"d_in_proj{i}")
            big["hgrn_w_in"][j] = matmul(y, dproj, "tn", f"dw_in{i}", shards=N_CHIPS)
            big["hgrn_w_out"][j] = dw_out
            d_gnorm[j] = dgn[0]
            d_lbl = d_lbl + jnp.transpose(dlb, (1, 0, 2)).reshape(DEPTH, d)
        dh, d_nmix[i] = norm_backward(h_in, nmix, dy, dh1, f"d_norm_mix{i}")
    small = {
        "gdn_conv": jnp.stack(d_conv), "gdn_a_log": jnp.stack(d_alog), "gdn_dt_bias": jnp.stack(d_dtb),
        "gdn_onorm": jnp.stack(d_onorm), "hgrn_lb_logits": d_lbl, "hgrn_gnorm": jnp.stack(d_gnorm),
        "norm_mix": jnp.concatenate(d_nmix, axis=0), "norm_mlp": jnp.concatenate(d_nmlp, axis=0), "norm_final": d_nf[0],
    }
    return loss_row, dh.reshape(x.shape), big, small


_SMALL_LAYOUT = {
    "norm_mix": (0, 4, D_MODEL), "norm_mlp": (4, 4, D_MODEL), "norm_final": (8, 1, D_MODEL),
    "hgrn_lb_logits": (9, 4, D_MODEL), "gdn_onorm": (13, 2, 128), "gdn_a_log": (15, 2, HEADS),
    "gdn_dt_bias": (17, 2, HEADS), "loss": (19, 1, 128), "gdn_conv": (20, 24, D_MODEL), "hgrn_gnorm": (44, 2, D_MODEL),
}


def _pack_small(small, loss_row):
    rows = []
    for name, (_, nrow, lanes) in _SMALL_LAYOUT.items():
        v = loss_row if name == "loss" else small[name]
        rows.append(_pad_lanes(v.reshape(nrow, -1), D_MODEL))
    rows.append(jnp.zeros((SMALL_ROWS - 46, D_MODEL), F32))
    return jnp.concatenate(rows, axis=0)


def _unpack_small(packed, name, shape):
    first, nrow, lanes = _SMALL_LAYOUT[name]
    return packed[first:first + nrow, :lanes].reshape(shape)


def kernel(x, gdn_w_in, gdn_conv, gdn_a_log, gdn_dt_bias, gdn_onorm, gdn_w_out, hgrn_w_in, hgrn_lb_logits, hgrn_gnorm, hgrn_w_out, norm_mix, norm_mlp, mlp_w_up, mlp_w_down, norm_final, loss_target, m_gdn_w_in, m_gdn_conv, m_gdn_a_log, m_gdn_dt_bias, m_gdn_onorm, m_gdn_w_out, m_hgrn_w_in, m_hgrn_lb_logits, m_hgrn_gnorm, m_hgrn_w_out, m_norm_mix, m_norm_mlp, m_mlp_w_up, m_mlp_w_down, m_norm_final, v_gdn_w_in, v_gdn_conv, v_gdn_a_log, v_gdn_dt_bias, v_gdn_onorm, v_gdn_w_out, v_hgrn_w_in, v_hgrn_lb_logits, v_hgrn_gnorm, v_hgrn_w_out, v_norm_mix, v_norm_mlp, v_mlp_w_up, v_mlp_w_down, v_norm_final):
    p = dict(gdn_w_in=gdn_w_in, gdn_conv=gdn_conv, gdn_a_log=gdn_a_log, gdn_dt_bias=gdn_dt_bias, gdn_onorm=gdn_onorm,
             gdn_w_out=gdn_w_out, hgrn_w_in=hgrn_w_in, hgrn_lb_logits=hgrn_lb_logits, hgrn_gnorm=hgrn_gnorm,
             hgrn_w_out=hgrn_w_out, norm_mix=norm_mix, norm_mlp=norm_mlp, mlp_w_up=mlp_w_up, mlp_w_down=mlp_w_down,
             norm_final=norm_final)
    m = dict(gdn_w_in=m_gdn_w_in, gdn_conv=m_gdn_conv, gdn_a_log=m_gdn_a_log, gdn_dt_bias=m_gdn_dt_bias,
             gdn_onorm=m_gdn_onorm, gdn_w_out=m_gdn_w_out, hgrn_w_in=m_hgrn_w_in, hgrn_lb_logits=m_hgrn_lb_logits,
             hgrn_gnorm=m_hgrn_gnorm, hgrn_w_out=m_hgrn_w_out, norm_mix=m_norm_mix, norm_mlp=m_norm_mlp,
             mlp_w_up=m_mlp_w_up, mlp_w_down=m_mlp_w_down, norm_final=m_norm_final)
    v = dict(gdn_w_in=v_gdn_w_in, gdn_conv=v_gdn_conv, gdn_a_log=v_gdn_a_log, gdn_dt_bias=v_gdn_dt_bias,
             gdn_onorm=v_gdn_onorm, gdn_w_out=v_gdn_w_out, hgrn_w_in=v_hgrn_w_in, hgrn_lb_logits=v_hgrn_lb_logits,
             hgrn_gnorm=v_hgrn_gnorm, hgrn_w_out=v_hgrn_w_out, norm_mix=v_norm_mix, norm_mlp=v_norm_mlp,
             mlp_w_up=v_mlp_w_up, mlp_w_down=v_mlp_w_down, norm_final=v_norm_final)
    xi, yi, ci = _mesh_pos()
    chip = 2 * xi + yi
    d = D_MODEL

    sharded = list(BIG) + ["gdn_conv", "hgrn_gnorm"]
    got = dict(zip(sharded, gather_chips([p[k].astype(BF16) if k in BIG else p[k] for k in sharded], "gather_weights")))
    cols = lambda g: jnp.transpose(g, (1, 2, 0, 3)).reshape(g.shape[1], g.shape[2], -1)
    rows = lambda g: jnp.transpose(g, (1, 0, 2, 3)).reshape(g.shape[1], -1, g.shape[3])
    w = dict(p)
    w["gdn_w_in"] = _pad_lanes(cols(got["gdn_w_in"]), GDN_IN_PAD)
    w["hgrn_w_in"] = cols(got["hgrn_w_in"])
    w["mlp_w_up"] = cols(got["mlp_w_up"])
    w["gdn_conv"] = cols(got["gdn_conv"])
    w["gdn_w_out"] = rows(got["gdn_w_out"])
    w["hgrn_w_out"] = rows(got["hgrn_w_out"])
    w["mlp_w_down"] = rows(got["mlp_w_down"])
    w["hgrn_gnorm"] = jnp.transpose(got["hgrn_gnorm"], (1, 0, 2)).reshape(got["hgrn_gnorm"].shape[1], -1)

    loss_row, grad_x, big, small = _local_grads(x, loss_target, w)

    order = [(k, l) for k in BIG for l in range(len(big[k]))]
    gs = [big[k][l] for k, l in order]
    theirs = exchange_sibling_half(gs, "reduce_cores")
    mine = [lax.dynamic_slice_in_dim(g, ci * (g.shape[1] // 2), g.shape[1] // 2, axis=1) for g in gs]
    ps = [add_pair(a, b, f"add_cores{n}") for n, (a, b) in enumerate(zip(mine, theirs))]
    slots = exchange_chips(ps, "reduce_chips")
    halves = [add_slots(s, f"add_chips{n}") for n, s in enumerate(slots)]
    groups = [[halves[n] for n, (k, l) in enumerate(order) if k == name] for name in BIG]
    grads = dict(zip(BIG, share_sibling(groups, "share_cores")))

    total = add_slots(gather_all(_pack_small(small, loss_row), "gather_small"), "add_small")
    loss = total[_SMALL_LAYOUT["loss"][0], 0]
    for name in WEIGHTS:
        if name in BIG:
            continue
        if name == "gdn_conv":
            full = _unpack_small(total, name, (2, CONV_K, 3 * d))
            grads[name] = lax.dynamic_slice_in_dim(full, chip * (3 * d // N_CHIPS), 3 * d // N_CHIPS, axis=2)
        elif name == "hgrn_gnorm":
            full = _unpack_small(total, name, (2, d))
            grads[name] = lax.dynamic_slice_in_dim(full, chip * (d // N_CHIPS), d // N_CHIPS, axis=1)
        else:
            grads[name] = _unpack_small(total, name, p[name].shape)

    delta, new_m, new_v = {}, {}, {}
    for name in WEIGHTS:
        delta[name], new_m[name], new_v[name] = adamw(p[name], grads[name], m[name], v[name], f"adamw_{name}")
    return (loss, grad_x, *[grads[n] for n in WEIGHTS], *[delta[n] for n in WEIGHTS],
            *[new_m[n] for n in WEIGHTS], *[new_v[n] for n in WEIGHTS])
```

```python
import functools
import math

import jax
import jax.numpy as jnp
from jax import lax
from jax.experimental import pallas as pl
from jax.experimental.pallas import tpu as pltpu

F32 = jnp.float32
BF16 = jnp.bfloat16
HI = lax.Precision.HIGHEST

D_MODEL = 1024
HEADS = 8
HEAD_DIM = 128
CHUNK = 64
SUB = 16
CONV_K = 4
HALO = 8
DEPTH = 4
EPS = 1e-6
MLP_HIDDEN = 4 * D_MODEL
GDN_MAIN = 4 * D_MODEL
GDN_IN = GDN_MAIN + 2 * HEADS
GDN_IN_PAD = GDN_MAIN + 128
NEG = -1e30

ADAM_LR = 0.001
ADAM_B1 = 0.9
ADAM_B2 = 0.999
ADAM_EPS = 1e-08
ADAM_WD = 0.01
ADAM_STEP = 10

VMEM_LIMIT = 48 * 1024 * 1024

MESH = pl.DeviceIdType.MESH


def _cparams(sem=None, **kw):
    if sem is not None:
        kw["dimension_semantics"] = sem
    return pltpu.CompilerParams(vmem_limit_bytes=VMEM_LIMIT, **kw)


def _iota(shape, dim):
    return lax.broadcasted_iota(jnp.int32, shape, dim)


_DIMS = {"nn": (((1,), (0,)), ((), ())), "nt": (((1,), (1,)), ((), ())), "tn": (((0,), (0,)), ((), ()))}


def _dot(a, b, mode):
    return lax.dot_general(a.astype(BF16), b.astype(BF16), _DIMS[mode], preferred_element_type=F32)


@functools.partial(jax.custom_vjp, nondiff_argnums=(2,))
def _mmx(a, b, mode):
    return _dot(a, b, mode)


def _mmx_fwd(a, b, mode):
    return _dot(a, b, mode), (a, b)


def _mmx_bwd(mode, res, g):
    a, b = res
    if mode == "nn":
        return _dot(g, b, "nt"), _dot(a, g, "tn")
    if mode == "nt":
        return _dot(g, b, "nn"), _dot(g, a, "tn")
    return _dot(b, g, "nt"), _dot(a, g, "nn")


_mmx.defvjp(_mmx_fwd, _mmx_bwd)


def _mm(a, b):
    return _mmx(a, b, "nn")


def _mm_nt(a, b):
    return _mmx(a, b, "nt")


def _mm_tn(a, b):
    return _mmx(a, b, "tn")


@functools.partial(jax.custom_vjp, nondiff_argnums=(1,))
def _roll_rows(x, d):
    return pltpu.roll(x, d, 0)


def _roll_rows_fwd(x, d):
    return pltpu.roll(x, d, 0), None


def _roll_rows_bwd(d, _, g):
    return (pltpu.roll(g, g.shape[0] - d, 0),)


_roll_rows.defvjp(_roll_rows_fwd, _roll_rows_bwd)


def _sigmoid(x):
    return 1.0 / (1.0 + jnp.exp(-x))


def _silu(x):
    return x * _sigmoid(x)


def _softplus(x):
    return jnp.maximum(x, 0.0) + jnp.log(1.0 + jnp.exp(-jnp.abs(x)))


def _log_sigmoid(x):
    return jnp.minimum(x, 0.0) - jnp.log(1.0 + jnp.exp(-jnp.abs(x)))


def _logaddexp(a, b):
    return jnp.maximum(a, b) + jnp.log(1.0 + jnp.exp(-jnp.abs(a - b)))


def _row_to_col(row):
    n = row.shape[1]
    eye = _iota((n, n), 0) == _iota((n, n), 1)
    return jnp.sum(jnp.where(eye, jnp.broadcast_to(row, (n, n)), 0.0), axis=1, keepdims=True)


def _col_to_row(col):
    n = col.shape[0]
    eye = _iota((n, n), 0) == _iota((n, n), 1)
    return jnp.sum(jnp.where(eye, jnp.broadcast_to(col, (n, n)), 0.0), axis=0, keepdims=True)


def _pick_row(x, r):
    return jnp.sum(jnp.where(_iota(x.shape, 0) == r, x, 0.0), axis=0, keepdims=True)


def _pick_lane(x, l):
    return jnp.sum(jnp.where(_iota(x.shape, 1) == l, x, 0.0), axis=1, keepdims=True)


def _each(f, *lists):
    return [f(*t) for t in zip(*lists)]


def _unit_lower_inverse(Ls):
    n = Ls[0].shape[0]
    r, c = _iota((n, n), 0), _iota((n, n), 1)
    eye = jnp.where(r == c, 1.0, 0.0).astype(F32)
    Ld = _each(lambda L: jnp.where((r // SUB) == (c // SUB), L, 0.0), Ls)
    Lo = _each(lambda L, d: L - d, Ls, Ld)
    P = _each(lambda d: eye - d, Ld)
    Lp = Ld
    for _ in range(int(math.log2(SUB)) - 1):
        Lp = _each(lambda x: _mm(x, x), Lp)
        P = _each(lambda p, x: p + _mm(p, x), P, Lp)
    N = _each(_mm, P, Lo)
    N2 = _each(lambda x: _mm(x, x), N)
    X = _each(lambda x, x2: (eye - x) + _mm(eye - x, x2), N, N2)
    return _each(_mm, X, P)


def _shift_rows(x, halo, d):
    if d == 0:
        return x
    xr = _roll_rows(x, d)
    hr = _roll_rows(halo, d)
    hfull = jnp.concatenate([hr, jnp.zeros((x.shape[0] - HALO, x.shape[1]), F32)], axis=0)
    return jnp.where(_iota(x.shape, 0) >= d, xr, hfull)


def _causal_conv_chunk(x, halo, w):
    y = None
    for kk in range(CONV_K):
        t = _shift_rows(x, halo, CONV_K - 1 - kk) * _pick_row(w, kk)
        y = t if y is None else y + t
    return y


def _gdn_chunk(xq, xk, xv, hq, hk, hv, gate, ab, S, cwq, cwk, cwv, alog, dtb, onorm, *, heads):
    C = xq[0].shape[0]
    q = _each(lambda x, h, w: _silu(_causal_conv_chunk(x, h, w)), xq, hq, cwq)
    k = _each(lambda x, h, w: _silu(_causal_conv_chunk(x, h, w)), xk, hk, cwk)
    v = _each(lambda x, h, w: _silu(_causal_conv_chunk(x, h, w)), xv, hv, cwv)
    q = _each(lambda t: t * lax.rsqrt(jnp.sum(t * t, axis=1, keepdims=True) + EPS) * (HEAD_DIM ** -0.5), q)
    k = _each(lambda t: t * lax.rsqrt(jnp.sum(t * t, axis=1, keepdims=True) + EPS), k)
    beta = [_sigmoid(_pick_lane(ab, HEADS + h)) for h in heads]
    g = [-jnp.exp(_pick_lane(alog, h)) * _softplus(_pick_lane(ab, h) + _pick_lane(dtb, h)) for h in heads]
    r, c = _iota((C, C), 0), _iota((C, C), 1)
    gc = _each(lambda t: jnp.sum(jnp.where(c <= r, jnp.broadcast_to(_col_to_row(t), (C, C)), 0.0), axis=1,
                                 keepdims=True), g)
    gc_row = _each(lambda t: jnp.sum(jnp.where(r <= c, jnp.broadcast_to(t, (C, C)), 0.0), axis=0, keepdims=True), g)
    decay = _each(lambda a, b: jnp.exp(jnp.where(r >= c, a - b, NEG)), gc, gc_row)
    kb = _each(lambda a, b: a * b, k, beta)
    L = _each(lambda a, b, d: jnp.where(r > c, _mm_nt(a, b) * d, 0.0), kb, k, decay)
    A = _each(lambda a, b, d: jnp.where(r >= c, _mm_nt(a, b) * d, 0.0), q, k, decay)
    T = _unit_lower_inverse(L)
    egc = _each(jnp.exp, gc)
    u = _each(lambda t, a, b: _mm(t, a * b), T, v, beta)
    w = _each(lambda t, a, e: _mm(t, a * e), T, kb, egc)
    gl = _each(lambda t: _pick_row(t, C - 1), gc)
    v_new = _each(lambda a, b, s: a - _mm(b, s), u, w, S)
    o = _each(lambda a, e, s, m, vn: _mm(a * e, s) + _mm(m, vn), q, egc, S, A, v_new)
    S_next = _each(lambda s, l, a, t, vn: s * jnp.exp(l) + _mm_tn(a * jnp.exp(l - t), vn), S, gl, k, gc, v_new)
    o = _each(lambda t, gt: t * lax.rsqrt(jnp.mean(t * t, axis=1, keepdims=True) + EPS) * onorm * _silu(gt), o, gate)
    return o, S_next


def _hgrn_lower_bound(lbl, layer):
    e = jnp.exp(lbl - jnp.max(lbl, axis=0, keepdims=True))
    sm = e / jnp.sum(e, axis=0, keepdims=True)
    r = _iota(lbl.shape, 0)
    return jnp.sum(jnp.where((r >= 1) & (r <= layer), sm, 0.0), axis=0, keepdims=True)


def _hgrn_chunk(qp, fp, v, S, lbl, *, layer):
    C = qp.shape[0]
    lb = _hgrn_lower_bound(lbl, layer)
    lf = _logaddexp(jnp.log(lb), jnp.log(1.0 - lb) + _log_sigmoid(fp))
    k = (1.0 - lb) * _sigmoid(-fp)
    q = _silu(qp) * (HEAD_DIM ** -0.5)
    r = _iota((C, C), 0)
    row = _iota(qp.shape, 0)
    gc = lf
    step = 1
    while step < C:
        gc = gc + jnp.where(row >= step, _roll_rows(gc, step), 0.0)
        step *= 2
    gml = gc - lf
    gb_rows = jnp.zeros_like(gc)
    gbs = []
    for s in range(C // SUB):
        gb_s = _pick_row(gml, s * SUB)
        gbs.append(gb_s)
        gb_rows = jnp.where((row // SUB) == s, gb_s, gb_rows)
    q_off = q * jnp.exp(gc - gb_rows)
    a_off = jnp.zeros((C, C), F32)
    for s in range(1, C // SUB):
        k_off = k * jnp.exp(jnp.where(row < s * SUB, gbs[s] - gc, NEG))
        a_off = a_off + jnp.where((r // SUB) == s, _mm_nt(q_off, k_off), 0.0)
    o = _mm(a_off, v) + _mm(q * jnp.exp(gc), S)
    for d in range(SUB):
        k_d = _roll_rows(k, d) if d else k
        gc_d = _roll_rows(gc, d) if d else gc
        v_d = _roll_rows(v, d) if d else v
        e = jnp.exp(jnp.where((row % SUB) >= d, gc - gc_d, NEG))
        o = o + jnp.sum(q * k_d * e, axis=1, keepdims=True) * v_d
    gl = _pick_row(gc, C - 1)
    S_next = S * _row_to_col(jnp.exp(gl)) + _mm_tn(k * jnp.exp(gl - gc), v)
    return o, S_next


HB = 8
HBW = HB * HEAD_DIM


def _chunk_specs(nc, col0, rev=False):
    return pl.BlockSpec((CHUNK, HBW), lambda b, n, g: (b * nc + ((nc - 1 - n) if rev else n), col0 // HB + g))


def _chunk_specs_rev(nc, col0):
    return _chunk_specs(nc, col0, True)


def _halo_spec(nc, col0, rev):
    per = CHUNK // HALO

    def imap(b, n, g):
        nn = (nc - 1 - n) if rev else n
        return (jnp.maximum((b * nc + nn) * per - 1, 0), col0 // HB + g)
    return pl.BlockSpec((HALO, HBW), imap)


def _const_spec(shape):
    nd = len(shape)
    return pl.BlockSpec(shape, lambda b, n, g: (0,) * nd)


def _state_spec(nc, rev=False):
    return pl.BlockSpec((None, None, HB, HEAD_DIM, HEAD_DIM),
                        lambda b, n, g: (b, (nc - 1 - n) if rev else n, g, 0, 0))


def _lanes(hh):
    return slice(hh * HEAD_DIM, (hh + 1) * HEAD_DIM)


def _head(g, hh):
    return hh if HB == HEADS else g * HB + hh


def gdn_forward(proj, conv_w, alog, dtb, onorm, bsz):
    n_tok = proj.shape[0]
    nc = n_tok // bsz // CHUNK

    def body(xq, xk, xv, hq, hk, hv, gate, ab, cwq, cwk, cwv, al, dt, on, o_ref, ssave_ref, s_ref):
        n, g = pl.program_id(1), pl.program_id(2)
        keep = jnp.where(n > 0, 1.0, 0.0).astype(F32)

        @pl.when(n == 0)
        def _():
            for hh in range(HB):
                s_ref[_head(g, hh)] = jnp.zeros((HEAD_DIM, HEAD_DIM), F32)

        heads = [_head(g, hh) for hh in range(HB)]
        per_head = lambda ref, scale=None: [ref[:, _lanes(hh)] if scale is None else ref[:, _lanes(hh)] * scale
                                            for hh in range(HB)]
        S = [s_ref[h] for h in heads]
        for hh in range(HB):
            ssave_ref[hh] = S[hh]
        outs, s_next = _gdn_chunk(per_head(xq), per_head(xk), per_head(xv), per_head(hq, keep), per_head(hk, keep),
                                  per_head(hv, keep), per_head(gate), ab[...], S, per_head(cwq), per_head(cwk),
                                  per_head(cwv), al[...], dt[...], on[...], heads=heads)
        for hh in range(HB):
            o_ref[:, _lanes(hh)] = outs[hh].astype(o_ref.dtype)
            s_ref[heads[hh]] = s_next[hh]

    hb = HEADS
    cw_spec = lambda col0: pl.BlockSpec((CONV_K, HBW), lambda b, n, g: (0, col0 // HB + g))
    in_specs = [_chunk_specs(nc, 0), _chunk_specs(nc, hb), _chunk_specs(nc, 2 * hb),
                _halo_spec(nc, 0, False), _halo_spec(nc, hb, False), _halo_spec(nc, 2 * hb, False),
                _chunk_specs(nc, 3 * hb),
                pl.BlockSpec((CHUNK, 128), lambda b, n, g: (b * nc + n, 4 * hb)),
                cw_spec(0), cw_spec(hb), cw_spec(2 * hb),
                _const_spec((1, 128)), _const_spec((1, 128)), _const_spec((1, 128))]
    out_specs = [_chunk_specs(nc, 0), _state_spec(nc)]
    return pl.pallas_call(
        body, name="gdn_fwd", grid=(bsz, nc, HEADS // HB), in_specs=in_specs, out_specs=out_specs,
        out_shape=(jax.ShapeDtypeStruct((n_tok, D_MODEL), BF16),
                   jax.ShapeDtypeStruct((bsz, nc, HEADS, HEAD_DIM, HEAD_DIM), F32)),
        scratch_shapes=[pltpu.VMEM((HEADS, HEAD_DIM, HEAD_DIM), F32)],
        compiler_params=_cparams(("arbitrary", "arbitrary", "arbitrary")),
    )(proj, proj, proj, proj, proj, proj, proj, proj, conv_w, conv_w, conv_w, alog, dtb, onorm)


def gdn_backward(proj, conv_w, alog, dtb, onorm, s_saved, d_out, bsz):
    n_tok = proj.shape[0]
    nc = n_tok // bsz // CHUNK

    def body(xq, xk, xv, hq, hk, hv, gate, ab, cwq, cwk, cwv, al, dt, on, ssave, do,
             dq_ref, dk_ref, dv_ref, dg_ref, dab_ref, dcw_ref, dal_ref, ddt_ref, don_ref, ds_ref, dhalo_ref):
        b, n, g = pl.program_id(0), pl.program_id(1), pl.program_id(2)
        nr = nc - 1 - n

        @pl.when((b == 0) & (n == 0) & (g == 0))
        def _():
            dcw_ref[...] = jnp.zeros_like(dcw_ref)
            dal_ref[...] = jnp.zeros_like(dal_ref)
            ddt_ref[...] = jnp.zeros_like(ddt_ref)
            don_ref[...] = jnp.zeros_like(don_ref)

        @pl.when(n == 0)
        def _():
            for hh in range(HB):
                ds_ref[_head(g, hh)] = jnp.zeros((HEAD_DIM, HEAD_DIM), F32)
                dhalo_ref[_head(g, hh)] = jnp.zeros((3, HALO, HEAD_DIM), F32)

        keep = jnp.where(nr > 0, 1.0, 0.0).astype(F32)
        pad = jnp.zeros((CHUNK - HALO, HEAD_DIM), F32)
        heads = [_head(g, hh) for hh in range(HB)]
        per_head = lambda ref, scale=None: [ref[:, _lanes(hh)] if scale is None else ref[:, _lanes(hh)] * scale
                                            for hh in range(HB)]
        args = (per_head(xq), per_head(xk), per_head(xv), per_head(hq, keep), per_head(hk, keep), per_head(hv, keep),
                per_head(gate), ab[...], [ssave[hh] for hh in range(HB)], per_head(cwq), per_head(cwk), per_head(cwv),
                al[...], dt[...], on[...])
        _, vjp = jax.vjp(functools.partial(_gdn_chunk, heads=heads), *args)
        (gxq, gxk, gxv, ghq, ghk, ghv, ggate, gab_sum, gS, gcq, gck, gcv, gal_sum, gdt_sum, gon_sum) = vjp(
            ([do[:, _lanes(hh)] for hh in range(HB)], [ds_ref[h] for h in heads]))
        for hh in range(HB):
            head, ln = heads[hh], _lanes(hh)
            dq_ref[:, ln] = gxq[hh] + jnp.concatenate([pad, dhalo_ref[head, 0]], axis=0)
            dk_ref[:, ln] = gxk[hh] + jnp.concatenate([pad, dhalo_ref[head, 1]], axis=0)
            dv_ref[:, ln] = gxv[hh] + jnp.concatenate([pad, dhalo_ref[head, 2]], axis=0)
            dg_ref[:, ln] = ggate[hh]
            dcw_ref[head, 0] += gcq[hh]
            dcw_ref[head, 1] += gck[hh]
            dcw_ref[head, 2] += gcv[hh]
            ds_ref[head] = gS[hh]
            dhalo_ref[head, 0] = ghq[hh] * keep
            dhalo_ref[head, 1] = ghk[hh] * keep
            dhalo_ref[head, 2] = ghv[hh] * keep

        @pl.when(g == 0)
        def _():
            dab_ref[...] = gab_sum

        @pl.when(g > 0)
        def _():
            dab_ref[...] += gab_sum

        dal_ref[...] += gal_sum
        ddt_ref[...] += gdt_sum
        don_ref[...] += gon_sum

    hb = HEADS
    cw_spec = lambda col0: pl.BlockSpec((CONV_K, HBW), lambda b, n, g: (0, col0 // HB + g))
    in_specs = [_chunk_specs_rev(nc, 0), _chunk_specs_rev(nc, hb), _chunk_specs_rev(nc, 2 * hb),
                _halo_spec(nc, 0, True), _halo_spec(nc, hb, True), _halo_spec(nc, 2 * hb, True),
                _chunk_specs_rev(nc, 3 * hb),
                pl.BlockSpec((CHUNK, 128), lambda b, n, g: (b * nc + (nc - 1 - n), 4 * hb)),
                cw_spec(0), cw_spec(hb), cw_spec(2 * hb),
                _const_spec((1, 128)), _const_spec((1, 128)), _const_spec((1, 128)),
                _state_spec(nc, True),
                _chunk_specs_rev(nc, 0)]
    tok = jax.ShapeDtypeStruct((n_tok, D_MODEL), F32)
    out_specs = [_chunk_specs_rev(nc, 0)] * 4 + [
        pl.BlockSpec((CHUNK, 128), lambda b, n, g: (b * nc + (nc - 1 - n), 0)),
        _const_spec((HEADS, 3, CONV_K, HEAD_DIM)), _const_spec((1, 128)), _const_spec((1, 128)), _const_spec((1, 128))]
    row = jax.ShapeDtypeStruct((1, 128), F32)
    return pl.pallas_call(
        body, name="gdn_bwd", grid=(bsz, nc, HEADS // HB), in_specs=in_specs, out_specs=out_specs,
        out_shape=(tok, tok, tok, tok, jax.ShapeDtypeStruct((n_tok, 128), F32),
                   jax.ShapeDtypeStruct((HEADS, 3, CONV_K, HEAD_DIM), F32), row, row, row),
        scratch_shapes=[pltpu.VMEM((HEADS, HEAD_DIM, HEAD_DIM), F32), pltpu.VMEM((HEADS, 3, HALO, HEAD_DIM), F32)],
        compiler_params=_cparams(("arbitrary", "arbitrary", "arbitrary")),
    )(proj, proj, proj, proj, proj, proj, proj, proj, conv_w, conv_w, conv_w, alog, dtb, onorm, s_saved, d_out)


def hgrn_forward(proj, lbl, layer, bsz):
    n_tok = proj.shape[0]
    nc = n_tok // bsz // CHUNK

    def body(qp, fp, vi, lb, o_ref, ssave_ref, s_ref):
        n, g = pl.program_id(1), pl.program_id(2)

        @pl.when(n == 0)
        def _():
            for hh in range(HB):
                s_ref[_head(g, hh)] = jnp.zeros((HEAD_DIM, HEAD_DIM), F32)

        for hh in range(HB):
            head = _head(g, hh)
            ln = _lanes(hh)
            S = s_ref[head]
            ssave_ref[hh] = S
            o, s_next = _hgrn_chunk(qp[:, ln], fp[:, ln], vi[:, ln], S, lb[:, ln], layer=layer)
            o_ref[:, ln] = o
            s_ref[head] = s_next

    hb = HEADS
    in_specs = [_chunk_specs(nc, 0), _chunk_specs(nc, hb), _chunk_specs(nc, 2 * hb),
                pl.BlockSpec((DEPTH, HBW), lambda b, n, g: (0, g))]
    out_specs = [_chunk_specs(nc, 0), _state_spec(nc)]
    return pl.pallas_call(
        body, name=f"hgrn_fwd{layer}", grid=(bsz, nc, HEADS // HB), in_specs=in_specs, out_specs=out_specs,
        out_shape=(jax.ShapeDtypeStruct((n_tok, D_MODEL), F32),
                   jax.ShapeDtypeStruct((bsz, nc, HEADS, HEAD_DIM, HEAD_DIM), F32)),
        scratch_shapes=[pltpu.VMEM((HEADS, HEAD_DIM, HEAD_DIM), F32)],
        compiler_params=_cparams(("arbitrary", "arbitrary", "arbitrary")),
    )(proj, proj, proj, lbl)


def hgrn_backward(proj, lbl, s_saved, d_o, layer, bsz):
    n_tok = proj.shape[0]
    nc = n_tok // bsz // CHUNK

    def body(qp, fp, vi, lb, ssave, do, dq_ref, df_ref, dv_ref, dlb_ref, ds_ref):
        b, n, g = pl.program_id(0), pl.program_id(1), pl.program_id(2)

        @pl.when((b == 0) & (n == 0) & (g == 0))
        def _():
            dlb_ref[...] = jnp.zeros_like(dlb_ref)

        @pl.when(n == 0)
        def _():
            for hh in range(HB):
                ds_ref[_head(g, hh)] = jnp.zeros((HEAD_DIM, HEAD_DIM), F32)

        for hh in range(HB):
            head = _head(g, hh)
            ln = _lanes(hh)
            _, vjp = jax.vjp(functools.partial(_hgrn_chunk, layer=layer), qp[:, ln], fp[:, ln], vi[:, ln], ssave[hh],
                             lb[:, ln])
            gq, gf, gv, gS, glb = vjp((do[:, ln], ds_ref[head]))
            dq_ref[:, ln] = gq
            df_ref[:, ln] = gf
            dv_ref[:, ln] = gv
            dlb_ref[head] += glb
            ds_ref[head] = gS

    hb = HEADS
    in_specs = [_chunk_specs_rev(nc, 0), _chunk_specs_rev(nc, hb), _chunk_specs_rev(nc, 2 * hb),
                pl.BlockSpec((DEPTH, HBW), lambda b, n, g: (0, g)),
                _state_spec(nc, True),
                _chunk_specs_rev(nc, 0)]
    tok = jax.ShapeDtypeStruct((n_tok, D_MODEL), F32)
    out_specs = [_chunk_specs_rev(nc, 0)] * 3 + [_const_spec((HEADS, DEPTH, HEAD_DIM))]
    return pl.pallas_call(
        body, name=f"hgrn_bwd{layer}", grid=(bsz, nc, HEADS // HB), in_specs=in_specs, out_specs=out_specs,
        out_shape=(tok, tok, tok, jax.ShapeDtypeStruct((HEADS, DEPTH, HEAD_DIM), F32)),
        scratch_shapes=[pltpu.VMEM((HEADS, HEAD_DIM, HEAD_DIM), F32)],
        compiler_params=_cparams(("arbitrary", "arbitrary", "arbitrary")),
    )(proj, proj, proj, lbl, s_saved, d_o)


ROW_TILE = 512


def _tile(n):
    for cand in (1024, 512, 1408, 384, 256, 128):
        if n % cand == 0:
            return cand
    return n


def _rmsnorm(x, w):
    return x * lax.rsqrt(jnp.mean(x * x, axis=1, keepdims=True) + EPS) * w


def norm_matmul(h, nw, w, relu2, name):
    n_tok, d = h.shape
    n_out = w.shape[1]
    tm, tn = min(n_tok, ROW_TILE), _tile(n_out)

    def body(h_ref, nw_ref, w_ref, y_ref, *outs):
        @pl.when(pl.program_id(1) == 0)
        def _():
            y_ref[...] = _rmsnorm(h_ref[...], nw_ref[...]).astype(BF16)

        acc = jnp.dot(y_ref[...], w_ref[...], preferred_element_type=F32)
        if relu2:
            r = jnp.maximum(acc, 0.0)
            outs[0][...] = (r * r).astype(BF16)
            outs[1][...] = r.astype(BF16)
        else:
            outs[0][...] = acc

    o_spec = pl.BlockSpec((tm, tn), lambda i, j: (i, j))
    if relu2:
        outs = (jax.ShapeDtypeStruct((n_tok, n_out), BF16),) * 2
    else:
        outs = (jax.ShapeDtypeStruct((n_tok, n_out), F32),)
    return pl.pallas_call(
        body, name=name, grid=(n_tok // tm, n_out // tn),
        in_specs=[pl.BlockSpec((tm, d), lambda i, j: (i, 0)), pl.BlockSpec((1, d), lambda i, j: (0, 0)),
                  pl.BlockSpec((d, tn), lambda i, j: (0, j))],
        out_specs=[pl.BlockSpec((tm, d), lambda i, j: (i, 0))] + [o_spec] * len(outs),
        out_shape=(jax.ShapeDtypeStruct((n_tok, d), BF16),) + outs,
        compiler_params=_cparams(("parallel", "arbitrary")),
    )(h, nw, w)


def matmul(a, b, mode, name, out_dtype=F32, extra=None, epilogue=None, shards=1):
    if mode == "nn":
        (m, k), n = a.shape, b.shape[1]
    elif mode == "nt":
        (m, k), n = a.shape, b.shape[0]
    else:
        (k, m), n = a.shape, b.shape[1]
    tm, tk = min(m, ROW_TILE), _tile(k)
    tn = _tile(n // shards)
    nk = k // tk
    a_spec = pl.BlockSpec((tk, tm), lambda i, j, kk: (kk, i)) if mode == "tn" else pl.BlockSpec((tm, tk), lambda i, j, kk: (i, kk))
    b_spec = pl.BlockSpec((tn, tk), lambda i, j, kk: (j, kk)) if mode == "nt" else pl.BlockSpec((tk, tn), lambda i, j, kk: (kk, j))
    dims = {"nn": (((1,), (0,)), ((), ())), "nt": (((1,), (1,)), ((), ())), "tn": (((0,), (0,)), ((), ()))}[mode]

    def body(*refs):
        if extra is not None:
            a_ref, b_ref, e_ref, o_ref, acc_ref = refs
        else:
            a_ref, b_ref, o_ref, acc_ref = refs
        kk = pl.program_id(2)

        @pl.when(kk == 0)
        def _():
            acc_ref[...] = jnp.zeros_like(acc_ref)

        acc_ref[...] += lax.dot_general(a_ref[...].astype(BF16), b_ref[...].astype(BF16), dims,
                                        preferred_element_type=F32)

        @pl.when(kk == nk - 1)
        def _():
            acc = acc_ref[...]
            if epilogue == "add":
                acc = e_ref[...] + acc
            elif epilogue == "mul2":
                acc = acc * (2.0 * e_ref[...].astype(F32))
            o_ref[...] = acc.astype(o_ref.dtype)

    in_specs = [a_spec, b_spec]
    args = [a, b]
    if extra is not None:
        in_specs.append(pl.BlockSpec((tm, tn), lambda i, j, kk: (i, j)))
        args.append(extra)
    if shards > 1:
        per = n // shards // tn
        out_spec = pl.BlockSpec((None, tm, tn), lambda i, j, kk: (j // per, i, j % per))
        out_shape = jax.ShapeDtypeStruct((shards, m, n // shards), out_dtype)
    else:
        out_spec = pl.BlockSpec((tm, tn), lambda i, j, kk: (i, j))
        out_shape = jax.ShapeDtypeStruct((m, n), out_dtype)
    return pl.pallas_call(
        body, name=name, grid=(m // tm, n // tn, nk), in_specs=in_specs, out_specs=out_spec, out_shape=out_shape,
        scratch_shapes=[pltpu.VMEM((tm, tn), F32)],
        compiler_params=_cparams(("parallel", "parallel", "arbitrary")),
    )(*args)


def norm_backward(h, nw, dy, dres, name):
    n_tok, d = h.shape
    tm = min(n_tok, ROW_TILE)

    def body(h_ref, nw_ref, dy_ref, dr_ref, dh_ref, dnw_ref):
        @pl.when(pl.program_id(0) == 0)
        def _():
            dnw_ref[...] = jnp.zeros_like(dnw_ref)

        _, vjp = jax.vjp(_rmsnorm, h_ref[...], nw_ref[...])
        gh, gw = vjp(dy_ref[...])
        dh_ref[...] = dr_ref[...] + gh
        dnw_ref[...] += gw

    row = pl.BlockSpec((tm, d), lambda i: (i, 0))
    vec = pl.BlockSpec((1, d), lambda i: (0, 0))
    return pl.pallas_call(
        body, name=name, grid=(n_tok // tm,), in_specs=[row, vec, row, row], out_specs=[row, vec],
        out_shape=(jax.ShapeDtypeStruct((n_tok, d), F32), jax.ShapeDtypeStruct((1, d), F32)),
        compiler_params=_cparams(("arbitrary",)),
    )(h, nw, dy, dres)


def _hgrn_post(o, gate, gw):
    return _rmsnorm(o, gw) * _silu(gate)


def hgrn_post_forward(o, proj, gw, name):
    n_tok, d = o.shape
    tm = min(n_tok, ROW_TILE)

    def body(o_ref, g_ref, w_ref, y_ref):
        y_ref[...] = _hgrn_post(o_ref[...], g_ref[...], w_ref[...]).astype(BF16)

    row = pl.BlockSpec((tm, d), lambda i: (i, 0))
    return pl.pallas_call(
        body, name=name, grid=(n_tok // tm,),
        in_specs=[row, pl.BlockSpec((tm, d), lambda i: (i, 3)), pl.BlockSpec((1, d), lambda i: (0, 0))],
        out_specs=row, out_shape=jax.ShapeDtypeStruct((n_tok, d), BF16),
        compiler_params=_cparams(("parallel",)),
    )(o, proj, gw)


def hgrn_post_backward(o, proj, gw, dy, name):
    n_tok, d = o.shape
    tm = min(n_tok, ROW_TILE)

    def body(o_ref, g_ref, w_ref, dy_ref, do_ref, dg_ref, dw_ref):
        @pl.when(pl.program_id(0) == 0)
        def _():
            dw_ref[...] = jnp.zeros_like(dw_ref)

        _, vjp = jax.vjp(_hgrn_post, o_ref[...], g_ref[...], w_ref[...])
        go, gg, gw_ = vjp(dy_ref[...])
        do_ref[...] = go
        dg_ref[...] = gg
        dw_ref[...] += gw_

    row = pl.BlockSpec((tm, d), lambda i: (i, 0))
    vec = pl.BlockSpec((1, d), lambda i: (0, 0))
    tok = jax.ShapeDtypeStruct((n_tok, d), F32)
    return pl.pallas_call(
        body, name=name, grid=(n_tok // tm,),
        in_specs=[row, pl.BlockSpec((tm, d), lambda i: (i, 3)), vec, row], out_specs=[row, row, vec],
        out_shape=(tok, tok, jax.ShapeDtypeStruct((1, d), F32)),
        compiler_params=_cparams(("arbitrary",)),
    )(o, proj, gw, dy)


def loss_head(h, nw, target):
    n_tok, d = h.shape
    tm = min(n_tok, ROW_TILE)

    def body(h_ref, nw_ref, t_ref, loss_ref, dh_ref, dnw_ref):
        @pl.when(pl.program_id(0) == 0)
        def _():
            dnw_ref[...] = jnp.zeros_like(dnw_ref)
            loss_ref[...] = jnp.zeros_like(loss_ref)

        out, vjp = jax.vjp(_rmsnorm, h_ref[...], nw_ref[...])
        err = out - t_ref[...]
        part = 0.5 * jnp.sum(jnp.sum(err * err, axis=1, keepdims=True), axis=0, keepdims=True) / d
        loss_ref[...] += jnp.broadcast_to(part, loss_ref.shape)
        gh, gw = vjp(err / d)
        dh_ref[...] = gh
        dnw_ref[...] += gw

    row = pl.BlockSpec((tm, d), lambda i: (i, 0))
    vec = pl.BlockSpec((1, d), lambda i: (0, 0))
    return pl.pallas_call(
        body, name="loss_head", grid=(n_tok // tm,), in_specs=[row, vec, row],
        out_specs=[pl.BlockSpec((1, 128), lambda i: (0, 0)), row, vec],
        out_shape=(jax.ShapeDtypeStruct((1, 128), F32), jax.ShapeDtypeStruct((n_tok, d), F32),
                   jax.ShapeDtypeStruct((1, d), F32)),
        compiler_params=_cparams(("arbitrary",)),
    )(h, nw, target)


def _rows2d(shape):
    if len(shape) == 1:
        return (1, shape[0])
    return (math.prod(shape[:-1]), shape[-1])


def adamw(w, g, m, v, name):
    shape = w.shape
    r, c = _rows2d(shape)
    tr = r if r <= 256 else 256
    c1 = 1.0 / (1.0 - ADAM_B1 ** ADAM_STEP)
    c2 = 1.0 / (1.0 - ADAM_B2 ** ADAM_STEP)

    def body(w_ref, g_ref, m_ref, v_ref, d_ref, nm_ref, nv_ref):
        gg = g_ref[...]
        nm = ADAM_B1 * m_ref[...] + (1.0 - ADAM_B1) * gg
        nv = ADAM_B2 * v_ref[...] + (1.0 - ADAM_B2) * (gg * gg)
        d_ref[...] = -ADAM_LR * ((nm * c1) / (jnp.sqrt(nv * c2) + ADAM_EPS) + ADAM_WD * w_ref[...])
        nm_ref[...] = nm
        nv_ref[...] = nv

    spec = pl.BlockSpec((tr, c), lambda i: (i, 0))
    sds = jax.ShapeDtypeStruct((r, c), F32)
    outs = pl.pallas_call(
        body, name=name, grid=(r // tr,), in_specs=[spec] * 4, out_specs=[spec] * 3, out_shape=(sds,) * 3,
        compiler_params=_cparams(("parallel",)),
    )(w.reshape(r, c), g.reshape(r, c), m.reshape(r, c), v.reshape(r, c))
    return tuple(o.reshape(shape) for o in outs)


def add_slots(parts, name):
    s, r, c = parts.shape
    tr = r if r <= 256 else 256

    def body(p_ref, o_ref):
        acc = p_ref[0]
        for t in range(1, s):
            acc = acc + p_ref[t]
        o_ref[...] = acc

    return pl.pallas_call(
        body, name=name, grid=(r // tr,), in_specs=[pl.BlockSpec((s, tr, c), lambda i: (0, i, 0))],
        out_specs=pl.BlockSpec((tr, c), lambda i: (i, 0)), out_shape=jax.ShapeDtypeStruct((r, c), F32),
        compiler_params=_cparams(("parallel",)),
    )(parts)


def add_pair(a, b, name):
    s, r, c = a.shape
    tr = r if r <= 256 else 256

    def body(a_ref, b_ref, o_ref):
        o_ref[...] = a_ref[...] + b_ref[...]

    spec = pl.BlockSpec((None, tr, c), lambda t, i: (t, i, 0))
    return pl.pallas_call(
        body, name=name, grid=(s, r // tr), in_specs=[spec, spec], out_specs=spec,
        out_shape=jax.ShapeDtypeStruct((s, r, c), F32), compiler_params=_cparams(("parallel", "parallel")),
    )(a, b)


N_CHIPS = 4
N_DEV = 8
_ANY = pl.BlockSpec(memory_space=pl.ANY)


def _mesh_pos():
    return lax.axis_index("x"), lax.axis_index("y"), lax.axis_index("c")


def _other_chips(x, y):
    ps = [(1 - x, y), (x, 1 - y), (1 - x, 1 - y)]
    return [(p, 2 * p[0] + p[1]) for p in ps]


def _remote(src, dst, send_sem, recv_sem, dev):
    return pltpu.make_async_remote_copy(src_ref=src, dst_ref=dst, send_sem=send_sem, recv_sem=recv_sem,
                                        device_id=dev, device_id_type=MESH)


def gather_chips(arrs, name):
    n = len(arrs)

    def body(*refs):
        ins, outs = refs[:n], refs[n:2 * n]
        send_sems, recv_sems, local_sems = refs[2 * n:]
        x, y, c = _mesh_pos()
        me = 2 * x + y
        peers = _other_chips(x, y)
        locs, sends = [], []
        for a in range(n):
            cp = pltpu.make_async_copy(ins[a], outs[a].at[me], local_sems.at[a])
            cp.start()
            locs.append(cp)
            for j, ((px, py), _) in enumerate(peers):
                r = _remote(ins[a], outs[a].at[me], send_sems.at[a, j], recv_sems.at[a, j], (px, py, c))
                r.start()
                sends.append(r)
        for a in range(n):
            for j, ((px, py), t) in enumerate(peers):
                _remote(ins[a], outs[a].at[t], send_sems.at[a, j], recv_sems.at[a, j], (px, py, c)).wait_recv()
        for r in sends:
            r.wait_send()
        for cp in locs:
            cp.wait()

    return pl.pallas_call(
        body, name=name, in_specs=[_ANY] * n, out_specs=[_ANY] * n,
        out_shape=[jax.ShapeDtypeStruct((N_CHIPS,) + a.shape, a.dtype) for a in arrs],
        scratch_shapes=[pltpu.SemaphoreType.DMA((n, 3)), pltpu.SemaphoreType.DMA((n, 3)), pltpu.SemaphoreType.DMA((n,))],
    )(*arrs)


def exchange_sibling_half(gs, name):
    n = len(gs)

    def body(*refs):
        ins, outs = refs[:n], refs[n:2 * n]
        send_sems, recv_sems = refs[2 * n:]
        x, y, c = _mesh_pos()
        sends = []
        for a in range(n):
            half = gs[a].shape[1] // 2
            src = ins[a].at[:, pl.ds((1 - c) * half, half), :]
            r = _remote(src, outs[a], send_sems.at[a], recv_sems.at[a], (x, y, 1 - c))
            r.start()
            sends.append(r)
        for r in sends:
            r.wait_recv()
        for r in sends:
            r.wait_send()

    return pl.pallas_call(
        body, name=name, in_specs=[_ANY] * n, out_specs=[_ANY] * n,
        out_shape=[jax.ShapeDtypeStruct((g.shape[0], g.shape[1] // 2, g.shape[2]), g.dtype) for g in gs],
        scratch_shapes=[pltpu.SemaphoreType.DMA((n,)), pltpu.SemaphoreType.DMA((n,))],
    )(*gs)


def exchange_chips(ps, name):
    n = len(ps)

    def body(*refs):
        ins, outs = refs[:n], refs[n:2 * n]
        send_sems, recv_sems, local_sems = refs[2 * n:]
        x, y, c = _mesh_pos()
        me = 2 * x + y
        peers = _other_chips(x, y)
        locs, sends = [], []
        for a in range(n):
            cp = pltpu.make_async_copy(ins[a].at[me], outs[a].at[me], local_sems.at[a])
            cp.start()
            locs.append(cp)
            for j, ((px, py), t) in enumerate(peers):
                r = _remote(ins[a].at[t], outs[a].at[me], send_sems.at[a, j], recv_sems.at[a, j], (px, py, c))
                r.start()
                sends.append(r)
        for a in range(n):
            for j, ((px, py), t) in enumerate(peers):
                _remote(ins[a].at[t], outs[a].at[t], send_sems.at[a, j], recv_sems.at[a, j], (px, py, c)).wait_recv()
        for r in sends:
            r.wait_send()
        for cp in locs:
            cp.wait()

    return pl.pallas_call(
        body, name=name, in_specs=[_ANY] * n, out_specs=[_ANY] * n,
        out_shape=[jax.ShapeDtypeStruct(p.shape, p.dtype) for p in ps],
        scratch_shapes=[pltpu.SemaphoreType.DMA((n, 3)), pltpu.SemaphoreType.DMA((n, 3)), pltpu.SemaphoreType.DMA((n,))],
    )(*ps)


def share_sibling(groups, name):
    flat = [f for grp in groups for f in grp]
    n = len(flat)
    nw = len(groups)

    def body(*refs):
        ins, outs = refs[:n], refs[n:n + nw]
        send_sems, recv_sems, local_sems = refs[n + nw:]
        x, y, c = _mesh_pos()
        locs, sends, k = [], [], 0
        for w, grp in enumerate(groups):
            for l in range(len(grp)):
                half = grp[l].shape[0]
                dst = outs[w].at[l, pl.ds(c * half, half), :]
                cp = pltpu.make_async_copy(ins[k], dst, local_sems.at[k])
                cp.start()
                locs.append(cp)
                r = _remote(ins[k], dst, send_sems.at[k], recv_sems.at[k], (x, y, 1 - c))
                r.start()
                sends.append(r)
                k += 1
        k = 0
        for w, grp in enumerate(groups):
            for l in range(len(grp)):
                half = grp[l].shape[0]
                theirs = outs[w].at[l, pl.ds((1 - c) * half, half), :]
                _remote(ins[k], theirs, send_sems.at[k], recv_sems.at[k], (x, y, 1 - c)).wait_recv()
                k += 1
        for r in sends:
            r.wait_send()
        for cp in locs:
            cp.wait()

    return pl.pallas_call(
        body, name=name, in_specs=[_ANY] * n, out_specs=[_ANY] * nw,
        out_shape=[jax.ShapeDtypeStruct((len(grp), 2 * grp[0].shape[0], grp[0].shape[1]), F32) for grp in groups],
        scratch_shapes=[pltpu.SemaphoreType.DMA((n,)), pltpu.SemaphoreType.DMA((n,)), pltpu.SemaphoreType.DMA((n,))],
    )(*flat)


def gather_all(small, name):
    flips = [(fx, fy, fc) for fx in (0, 1) for fy in (0, 1) for fc in (0, 1)][1:]

    def body(in_ref, out_ref, send_sems, recv_sems, local_sem):
        x, y, c = _mesh_pos()
        me = 4 * x + 2 * y + c
        cp = pltpu.make_async_copy(in_ref, out_ref.at[me], local_sem)
        cp.start()
        peers = [((1 - x) if fx else x, (1 - y) if fy else y, (1 - c) if fc else c) for fx, fy, fc in flips]
        sends = []
        for k, p in enumerate(peers):
            r = _remote(in_ref, out_ref.at[me], send_sems.at[k], recv_sems.at[k], p)
            r.start()
            sends.append(r)
        for k, p in enumerate(peers):
            _remote(in_ref, out_ref.at[4 * p[0] + 2 * p[1] + p[2]], send_sems.at[k], recv_sems.at[k], p).wait_recv()
        for r in sends:
            r.wait_send()
        cp.wait()

    return pl.pallas_call(
        body, name=name, in_specs=[_ANY], out_specs=_ANY,
        out_shape=jax.ShapeDtypeStruct((N_DEV,) + small.shape, small.dtype),
        scratch_shapes=[pltpu.SemaphoreType.DMA((N_DEV - 1,)), pltpu.SemaphoreType.DMA((N_DEV - 1,)),
                        pltpu.SemaphoreType.DMA],
    )(small)


BIG = ("gdn_w_in", "gdn_w_out", "hgrn_w_in", "hgrn_w_out", "mlp_w_up", "mlp_w_down")
WEIGHTS = ("gdn_w_in", "gdn_conv", "gdn_a_log", "gdn_dt_bias", "gdn_onorm", "gdn_w_out", "hgrn_w_in", "hgrn_lb_logits",
           "hgrn_gnorm", "hgrn_w_out", "norm_mix", "norm_mlp", "mlp_w_up", "mlp_w_down", "norm_final")
SMALL_ROWS = 96


def _pad_lanes(v, n):
    return jnp.pad(v, [(0, 0)] * (v.ndim - 1) + [(0, n - v.shape[-1])])


def _local_grads(x, target, w):
    bsz, t_len, d = x.shape
    n_tok = bsz * t_len
    h = x.reshape(n_tok, d)
    lbl = w["hgrn_lb_logits"]
    saved = []
    for i in range(DEPTH):
        j = i // 2
        nmix = w["norm_mix"][i][None, :]
        if i % 2 == 0:
            y, proj = norm_matmul(h, nmix, w["gdn_w_in"][j], False, f"in_proj{i}")
            al = _pad_lanes(w["gdn_a_log"][j][None, :], 128)
            dtb = _pad_lanes(w["gdn_dt_bias"][j][None, :], 128)
            on = w["gdn_onorm"][j][None, :]
            og, ssave = gdn_forward(proj, w["gdn_conv"][j], al, dtb, on, bsz)
            mix = (proj, ssave, al, dtb, on)
            w_out = w["gdn_w_out"][j]
        else:
            y, proj = norm_matmul(h, nmix, w["hgrn_w_in"][j], False, f"in_proj{i}")
            o, ssave = hgrn_forward(proj, lbl, i, bsz)
            gn = w["hgrn_gnorm"][j][None, :]
            og = hgrn_post_forward(o, proj, gn, f"hgrn_post{i}")
            mix = (proj, ssave, o, gn)
            w_out = w["hgrn_w_out"][j]
        h1 = matmul(og, w_out, "nn", f"out_proj{i}", extra=h, epilogue="add")
        nmlp = w["norm_mlp"][i][None, :]
        z, a, r = norm_matmul(h1, nmlp, w["mlp_w_up"][i], True, f"mlp_up{i}")
        h2 = matmul(a, w["mlp_w_down"][i], "nn", f"mlp_down{i}", extra=h1, epilogue="add")
        saved.append((h, nmix, y, mix, og, w_out, h1, nmlp, z, a, r))
        h = h2
    loss_row, dh, d_nf = loss_head(h, w["norm_final"][None, :], target.reshape(n_tok, d))

    big = {k: [None] * (DEPTH if k.startswith("mlp") else DEPTH // 2) for k in BIG}
    d_nmix, d_nmlp = [None] * DEPTH, [None] * DEPTH
    d_conv, d_alog, d_dtb, d_onorm, d_gnorm = [None] * 2, [None] * 2, [None] * 2, [None] * 2, [None] * 2
    d_lbl = jnp.zeros((DEPTH, d), F32)
    for i in reversed(range(DEPTH)):
        j = i // 2
        h_in, nmix, y, mix, og, w_out, h1, nmlp, z, a, r = saved[i]
        du = matmul(dh, w["mlp_w_down"][i], "nt", f"d_mlp_act{i}", out_dtype=BF16, extra=r, epilogue="mul2")
        big["mlp_w_down"][i] = matmul(a, dh, "tn", f"dw_down{i}").reshape(N_CHIPS, -1, d)
        dz = matmul(du, w["mlp_w_up"][i], "nt", f"d_mlp_in{i}")
        big["mlp_w_up"][i] = matmul(z, du, "tn", f"dw_up{i}", shards=N_CHIPS)
        dh1, d_nmlp[i] = norm_backward(h1, nmlp, dz, dh, f"d_norm_mlp{i}")
        dog = matmul(dh1, w_out, "nt", f"d_mix_out{i}")
        dw_out = matmul(og, dh1, "tn", f"dw_out{i}").reshape(N_CHIPS, -1, d)
        if i % 2 == 0:
            proj, ssave, al, dtb, on = mix
            dq, dk, dv, dg, dab, dcw, dal, ddt, don = gdn_backward(proj, w["gdn_conv"][j], al, dtb, on, ssave, dog, bsz)
            dproj = jnp.concatenate([dq, dk, dv, dg, dab], axis=1)
            dy = matmul(dproj, w["gdn_w_in"][j], "nt", f"d_in_proj{i}")
            dw_in = matmul(y, dproj, "tn", f"dw_in{i}")[:, :GDN_IN]
            big["gdn_w_in"][j] = jnp.transpose(dw_in.reshape(d, N_CHIPS, GDN_IN // N_CHIPS), (1, 0, 2))
            big["gdn_w_out"][j] = dw_out
            d_conv[j] = jnp.transpose(dcw, (2, 1, 0, 3)).reshape(CONV_K, 3 * d)
            d_alog[j], d_dtb[j], d_onorm[j] = dal[0, :HEADS], ddt[0, :HEADS], don[0]
        else:
            proj, ssave, o, gn = mix
            do_raw, dgate, dgn = hgrn_post_backward(o, proj, gn, dog, f"d_hgrn_post{i}")
            dq, df, dv, dlb = hgrn_backward(proj, lbl, ssave, do_raw, i, bsz)
            dproj = jnp.concatenate([dq, df, dv, dgate], axis=1)
            dy = matmul(dproj, w["hgrn_w_in"][j], "nt", f"d_in_proj{i}")
            big["hgrn_w_in"][j] = matmul(y, dproj, "tn", f"dw_in{i}", shards=N_CHIPS)
            big["hgrn_w_out"][j] = dw_out
            d_gnorm[j] = dgn[0]
            d_lbl = d_lbl + jnp.transpose(dlb, (1, 0, 2)).reshape(DEPTH, d)
        dh, d_nmix[i] = norm_backward(h_in, nmix, dy, dh1, f"d_norm_mix{i}")
    small = {
        "gdn_conv": jnp.stack(d_conv), "gdn_a_log": jnp.stack(d_alog), "gdn_dt_bias": jnp.stack(d_dtb),
        "gdn_onorm": jnp.stack(d_onorm), "hgrn_lb_logits": d_lbl, "hgrn_gnorm": jnp.stack(d_gnorm),
        "norm_mix": jnp.concatenate(d_nmix, axis=0), "norm_mlp": jnp.concatenate(d_nmlp, axis=0), "norm_final": d_nf[0],
    }
    return loss_row, dh.reshape(x.shape), big, small


_SMALL_LAYOUT = {
    "norm_mix": (0, 4, D_MODEL), "norm_mlp": (8, 4, D_MODEL), "norm_final": (16, 1, D_MODEL),
    "hgrn_lb_logits": (24, 4, D_MODEL), "gdn_onorm": (32, 2, 128), "gdn_a_log": (40, 2, HEADS),
    "gdn_dt_bias": (48, 2, HEADS), "loss": (56, 1, 128), "gdn_conv": (64, 24, D_MODEL), "hgrn_gnorm": (88, 2, D_MODEL),
}


def _pack_small(small, loss_row):
    rows = []
    for name, (first, nrow, lanes) in _SMALL_LAYOUT.items():
        v = loss_row if name == "loss" else small[name]
        v = _pad_lanes(v.reshape(nrow, -1), D_MODEL)
        rows.append(jnp.pad(v, ((0, -nrow % 8), (0, 0))))
    return jnp.concatenate(rows, axis=0)


def _unpack_small(packed, name, shape):
    first, nrow, lanes = _SMALL_LAYOUT[name]
    return packed[first:first + nrow, :lanes].reshape(shape)


def kernel(x, gdn_w_in, gdn_conv, gdn_a_log, gdn_dt_bias, gdn_onorm, gdn_w_out, hgrn_w_in, hgrn_lb_logits, hgrn_gnorm, hgrn_w_out, norm_mix, norm_mlp, mlp_w_up, mlp_w_down, norm_final, loss_target, m_gdn_w_in, m_gdn_conv, m_gdn_a_log, m_gdn_dt_bias, m_gdn_onorm, m_gdn_w_out, m_hgrn_w_in, m_hgrn_lb_logits, m_hgrn_gnorm, m_hgrn_w_out, m_norm_mix, m_norm_mlp, m_mlp_w_up, m_mlp_w_down, m_norm_final, v_gdn_w_in, v_gdn_conv, v_gdn_a_log, v_gdn_dt_bias, v_gdn_onorm, v_gdn_w_out, v_hgrn_w_in, v_hgrn_lb_logits, v_hgrn_gnorm, v_hgrn_w_out, v_norm_mix, v_norm_mlp, v_mlp_w_up, v_mlp_w_down, v_norm_final):
    p = dict(gdn_w_in=gdn_w_in, gdn_conv=gdn_conv, gdn_a_log=gdn_a_log, gdn_dt_bias=gdn_dt_bias, gdn_onorm=gdn_onorm,
             gdn_w_out=gdn_w_out, hgrn_w_in=hgrn_w_in, hgrn_lb_logits=hgrn_lb_logits, hgrn_gnorm=hgrn_gnorm,
             hgrn_w_out=hgrn_w_out, norm_mix=norm_mix, norm_mlp=norm_mlp, mlp_w_up=mlp_w_up, mlp_w_down=mlp_w_down,
             norm_final=norm_final)
    m = dict(gdn_w_in=m_gdn_w_in, gdn_conv=m_gdn_conv, gdn_a_log=m_gdn_a_log, gdn_dt_bias=m_gdn_dt_bias,
             gdn_onorm=m_gdn_onorm, gdn_w_out=m_gdn_w_out, hgrn_w_in=m_hgrn_w_in, hgrn_lb_logits=m_hgrn_lb_logits,
             hgrn_gnorm=m_hgrn_gnorm, hgrn_w_out=m_hgrn_w_out, norm_mix=m_norm_mix, norm_mlp=m_norm_mlp,
             mlp_w_up=m_mlp_w_up, mlp_w_down=m_mlp_w_down, norm_final=m_norm_final)
    v = dict(gdn_w_in=v_gdn_w_in, gdn_conv=v_gdn_conv, gdn_a_log=v_gdn_a_log, gdn_dt_bias=v_gdn_dt_bias,
             gdn_onorm=v_gdn_onorm, gdn_w_out=v_gdn_w_out, hgrn_w_in=v_hgrn_w_in, hgrn_lb_logits=v_hgrn_lb_logits,
             hgrn_gnorm=v_hgrn_gnorm, hgrn_w_out=v_hgrn_w_out, norm_mix=v_norm_mix, norm_mlp=v_norm_mlp,
             mlp_w_up=v_mlp_w_up, mlp_w_down=v_mlp_w_down, norm_final=v_norm_final)
    xi, yi, ci = _mesh_pos()
    chip = 2 * xi + yi
    d = D_MODEL

    sharded = list(BIG) + ["gdn_conv", "hgrn_gnorm"]
    got = dict(zip(sharded, gather_chips([p[k].astype(BF16) if k in BIG else p[k] for k in sharded], "gather_weights")))
    cols = lambda g: jnp.transpose(g, (1, 2, 0, 3)).reshape(g.shape[1], g.shape[2], -1)
    rows = lambda g: jnp.transpose(g, (1, 0, 2, 3)).reshape(g.shape[1], -1, g.shape[3])
    w = dict(p)
    w["gdn_w_in"] = _pad_lanes(cols(got["gdn_w_in"]), GDN_IN_PAD)
    w["hgrn_w_in"] = cols(got["hgrn_w_in"])
    w["mlp_w_up"] = cols(got["mlp_w_up"])
    w["gdn_conv"] = cols(got["gdn_conv"])
    w["gdn_w_out"] = rows(got["gdn_w_out"])
    w["hgrn_w_out"] = rows(got["hgrn_w_out"])
    w["mlp_w_down"] = rows(got["mlp_w_down"])
    w["hgrn_gnorm"] = jnp.transpose(got["hgrn_gnorm"], (1, 0, 2)).reshape(got["hgrn_gnorm"].shape[1], -1)

    loss_row, grad_x, big, small = _local_grads(x, loss_target, w)

    order = [(k, l) for k in BIG for l in range(len(big[k]))]
    gs = [big[k][l] for k, l in order]
    theirs = exchange_sibling_half(gs, "reduce_cores")
    mine = [lax.dynamic_slice_in_dim(g, ci * (g.shape[1] // 2), g.shape[1] // 2, axis=1) for g in gs]
    ps = [add_pair(a, b, f"add_cores{n}") for n, (a, b) in enumerate(zip(mine, theirs))]
    slots = exchange_chips(ps, "reduce_chips")
    halves = [add_slots(s, f"add_chips{n}") for n, s in enumerate(slots)]
    groups = [[halves[n] for n, (k, l) in enumerate(order) if k == name] for name in BIG]
    grads = dict(zip(BIG, share_sibling(groups, "share_cores")))

    total = add_slots(gather_all(_pack_small(small, loss_row), "gather_small"), "add_small")
    loss = total[_SMALL_LAYOUT["loss"][0], 0]
    for name in WEIGHTS:
        if name in BIG:
            continue
        if name == "gdn_conv":
            full = _unpack_small(total, name, (2, CONV_K, 3 * d))
            grads[name] = lax.dynamic_slice_in_dim(full, chip * (3 * d // N_CHIPS), 3 * d // N_CHIPS, axis=2)
        elif name == "hgrn_gnorm":
            full = _unpack_small(total, name, (2, d))
            grads[name] = lax.dynamic_slice_in_dim(full, chip * (d // N_CHIPS), d // N_CHIPS, axis=1)
        else:
            grads[name] = _unpack_small(total, name, p[name].shape)

    delta, new_m, new_v = {}, {}, {}
    for name in WEIGHTS:
        delta[name], new_m[name], new_v[name] = adamw(p[name], grads[name], m[name], v[name], f"adamw_{name}")
    return (loss, grad_x, *[grads[n] for n in WEIGHTS], *[delta[n] for n in WEIGHTS],
            *[new_m[n] for n in WEIGHTS], *[new_v[n] for n in WEIGHTS])
```

```python
import functools
import math

import jax
import jax.numpy as jnp
from jax import lax
from jax.experimental import pallas as pl
from jax.experimental.pallas import tpu as pltpu

F32 = jnp.float32
BF16 = jnp.bfloat16
HI = lax.Precision.HIGHEST

D_MODEL = 1024
HEADS = 8
HEAD_DIM = 128
CHUNK = 64
SUB = 16
CONV_K = 4
HALO = 8
DEPTH = 4
EPS = 1e-6
MLP_HIDDEN = 4 * D_MODEL
GDN_MAIN = 4 * D_MODEL
GDN_IN = GDN_MAIN + 2 * HEADS
GDN_IN_PAD = GDN_MAIN + 128
NEG = -1e30

ADAM_LR = 0.001
ADAM_B1 = 0.9
ADAM_B2 = 0.999
ADAM_EPS = 1e-08
ADAM_WD = 0.01
ADAM_STEP = 10

VMEM_LIMIT = 48 * 1024 * 1024

MESH = pl.DeviceIdType.MESH


def _cparams(sem=None, **kw):
    if sem is not None:
        kw["dimension_semantics"] = sem
    return pltpu.CompilerParams(vmem_limit_bytes=VMEM_LIMIT, **kw)


def _iota(shape, dim):
    return lax.broadcasted_iota(jnp.int32, shape, dim)


_DIMS = {"nn": (((1,), (0,)), ((), ())), "nt": (((1,), (1,)), ((), ())), "tn": (((0,), (0,)), ((), ()))}


def _dot(a, b, mode):
    return lax.dot_general(a.astype(BF16), b.astype(BF16), _DIMS[mode], preferred_element_type=F32)


@functools.partial(jax.custom_vjp, nondiff_argnums=(2,))
def _mmx(a, b, mode):
    return _dot(a, b, mode)


def _mmx_fwd(a, b, mode):
    return _dot(a, b, mode), (a, b)


def _mmx_bwd(mode, res, g):
    a, b = res
    if mode == "nn":
        return _dot(g, b, "nt"), _dot(a, g, "tn")
    if mode == "nt":
        return _dot(g, b, "nn"), _dot(g, a, "tn")
    return _dot(b, g, "nt"), _dot(a, g, "nn")


_mmx.defvjp(_mmx_fwd, _mmx_bwd)


def _mm(a, b):
    return _mmx(a, b, "nn")


def _mm_nt(a, b):
    return _mmx(a, b, "nt")


def _mm_tn(a, b):
    return _mmx(a, b, "tn")


@functools.partial(jax.custom_vjp, nondiff_argnums=(1,))
def _roll_rows(x, d):
    return pltpu.roll(x, d, 0)


def _roll_rows_fwd(x, d):
    return pltpu.roll(x, d, 0), None


def _roll_rows_bwd(d, _, g):
    return (pltpu.roll(g, g.shape[0] - d, 0),)


_roll_rows.defvjp(_roll_rows_fwd, _roll_rows_bwd)


def _sigmoid(x):
    return 1.0 / (1.0 + jnp.exp(-x))


def _silu(x):
    return x * _sigmoid(x)


def _softplus(x):
    return jnp.maximum(x, 0.0) + jnp.log(1.0 + jnp.exp(-jnp.abs(x)))


def _log_sigmoid(x):
    return jnp.minimum(x, 0.0) - jnp.log(1.0 + jnp.exp(-jnp.abs(x)))


def _logaddexp(a, b):
    return jnp.maximum(a, b) + jnp.log(1.0 + jnp.exp(-jnp.abs(a - b)))


def _row_to_col(row):
    n = row.shape[1]
    eye = _iota((n, n), 0) == _iota((n, n), 1)
    return jnp.sum(jnp.where(eye, jnp.broadcast_to(row, (n, n)), 0.0), axis=1, keepdims=True)


def _col_to_row(col):
    n = col.shape[0]
    eye = _iota((n, n), 0) == _iota((n, n), 1)
    return jnp.sum(jnp.where(eye, jnp.broadcast_to(col, (n, n)), 0.0), axis=0, keepdims=True)


def _pick_row(x, r):
    return jnp.sum(jnp.where(_iota(x.shape, 0) == r, x, 0.0), axis=0, keepdims=True)


def _pick_lane(x, l):
    return jnp.sum(jnp.where(_iota(x.shape, 1) == l, x, 0.0), axis=1, keepdims=True)


def _each(f, *lists):
    return [f(*t) for t in zip(*lists)]


def _unit_lower_inverse(Ls):
    n = Ls[0].shape[0]
    r, c = _iota((n, n), 0), _iota((n, n), 1)
    eye = jnp.where(r == c, 1.0, 0.0).astype(F32)
    Ld = _each(lambda L: jnp.where((r // SUB) == (c // SUB), L, 0.0), Ls)
    Lo = _each(lambda L, d: L - d, Ls, Ld)
    P = _each(lambda d: eye - d, Ld)
    Lp = Ld
    for _ in range(int(math.log2(SUB)) - 1):
        Lp = _each(lambda x: _mm(x, x), Lp)
        P = _each(lambda p, x: p + _mm(p, x), P, Lp)
    N = _each(_mm, P, Lo)
    N2 = _each(lambda x: _mm(x, x), N)
    X = _each(lambda x, x2: (eye - x) + _mm(eye - x, x2), N, N2)
    return _each(_mm, X, P)


def _shift_rows(x, halo, d):
    if d == 0:
        return x
    xr = _roll_rows(x, d)
    hr = _roll_rows(halo, d)
    hfull = jnp.concatenate([hr, jnp.zeros((x.shape[0] - HALO, x.shape[1]), F32)], axis=0)
    return jnp.where(_iota(x.shape, 0) >= d, xr, hfull)


def _causal_conv_chunk(x, halo, w):
    y = None
    for kk in range(CONV_K):
        t = _shift_rows(x, halo, CONV_K - 1 - kk) * _pick_row(w, kk)
        y = t if y is None else y + t
    return y


def _gdn_chunk(xq, xk, xv, hq, hk, hv, gate, ab, S, cwq, cwk, cwv, alog, dtb, onorm, *, heads):
    C = xq[0].shape[0]
    q = _each(lambda x, h, w: _silu(_causal_conv_chunk(x, h, w)), xq, hq, cwq)
    k = _each(lambda x, h, w: _silu(_causal_conv_chunk(x, h, w)), xk, hk, cwk)
    v = _each(lambda x, h, w: _silu(_causal_conv_chunk(x, h, w)), xv, hv, cwv)
    q = _each(lambda t: t * lax.rsqrt(jnp.sum(t * t, axis=1, keepdims=True) + EPS) * (HEAD_DIM ** -0.5), q)
    k = _each(lambda t: t * lax.rsqrt(jnp.sum(t * t, axis=1, keepdims=True) + EPS), k)
    beta = [_sigmoid(_pick_lane(ab, HEADS + h)) for h in heads]
    g = [-jnp.exp(_pick_lane(alog, h)) * _softplus(_pick_lane(ab, h) + _pick_lane(dtb, h)) for h in heads]
    r, c = _iota((C, C), 0), _iota((C, C), 1)
    gc = _each(lambda t: jnp.sum(jnp.where(c <= r, jnp.broadcast_to(_col_to_row(t), (C, C)), 0.0), axis=1,
                                 keepdims=True), g)
    gc_row = _each(lambda t: jnp.sum(jnp.where(r <= c, jnp.broadcast_to(t, (C, C)), 0.0), axis=0, keepdims=True), g)
    decay = _each(lambda a, b: jnp.exp(jnp.where(r >= c, a - b, NEG)), gc, gc_row)
    kb = _each(lambda a, b: a * b, k, beta)
    L = _each(lambda a, b, d: jnp.where(r > c, _mm_nt(a, b) * d, 0.0), kb, k, decay)
    A = _each(lambda a, b, d: jnp.where(r >= c, _mm_nt(a, b) * d, 0.0), q, k, decay)
    T = _unit_lower_inverse(L)
    egc = _each(jnp.exp, gc)
    u = _each(lambda t, a, b: _mm(t, a * b), T, v, beta)
    w = _each(lambda t, a, e: _mm(t, a * e), T, kb, egc)
    gl = _each(lambda t: _pick_row(t, C - 1), gc)
    v_new = _each(lambda a, b, s: a - _mm(b, s), u, w, S)
    o = _each(lambda a, e, s, m, vn: _mm(a * e, s) + _mm(m, vn), q, egc, S, A, v_new)
    S_next = _each(lambda s, l, a, t, vn: s * jnp.exp(l) + _mm_tn(a * jnp.exp(l - t), vn), S, gl, k, gc, v_new)
    o = _each(lambda t, gt: t * lax.rsqrt(jnp.mean(t * t, axis=1, keepdims=True) + EPS) * onorm * _silu(gt), o, gate)
    return o, S_next


def _hgrn_lower_bound(lbl, layer):
    e = jnp.exp(lbl - jnp.max(lbl, axis=0, keepdims=True))
    sm = e / jnp.sum(e, axis=0, keepdims=True)
    r = _iota(lbl.shape, 0)
    return jnp.sum(jnp.where((r >= 1) & (r <= layer), sm, 0.0), axis=0, keepdims=True)


def _hgrn_chunk(qp, fp, v, S, lbl, *, layer):
    C = qp.shape[0]
    lb = _hgrn_lower_bound(lbl, layer)
    lf = _logaddexp(jnp.log(lb), jnp.log(1.0 - lb) + _log_sigmoid(fp))
    k = (1.0 - lb) * _sigmoid(-fp)
    q = _silu(qp) * (HEAD_DIM ** -0.5)
    r = _iota((C, C), 0)
    row = _iota(qp.shape, 0)
    gc = lf
    step = 1
    while step < C:
        gc = gc + jnp.where(row >= step, _roll_rows(gc, step), 0.0)
        step *= 2
    gml = gc - lf
    gb_rows = jnp.zeros_like(gc)
    gbs = []
    for s in range(C // SUB):
        gb_s = _pick_row(gml, s * SUB)
        gbs.append(gb_s)
        gb_rows = jnp.where((row // SUB) == s, gb_s, gb_rows)
    q_off = q * jnp.exp(gc - gb_rows)
    a_off = jnp.zeros((C, C), F32)
    for s in range(1, C // SUB):
        k_off = k * jnp.exp(jnp.where(row < s * SUB, gbs[s] - gc, NEG))
        a_off = a_off + jnp.where((r // SUB) == s, _mm_nt(q_off, k_off), 0.0)
    o = _mm(a_off, v) + _mm(q * jnp.exp(gc), S)
    for d in range(SUB):
        k_d = _roll_rows(k, d) if d else k
        gc_d = _roll_rows(gc, d) if d else gc
        v_d = _roll_rows(v, d) if d else v
        e = jnp.exp(jnp.where((row % SUB) >= d, gc - gc_d, NEG))
        o = o + jnp.sum(q * k_d * e, axis=1, keepdims=True) * v_d
    gl = _pick_row(gc, C - 1)
    S_next = S * _row_to_col(jnp.exp(gl)) + _mm_tn(k * jnp.exp(gl - gc), v)
    return o, S_next


N_CHIPS = 4
N_DEV = 8
_ANY = pl.BlockSpec(memory_space=pl.ANY)


def _mesh_pos():
    return lax.axis_index("x"), lax.axis_index("y"), lax.axis_index("c")


def _other_chips(x, y):
    ps = [(1 - x, y), (x, 1 - y), (1 - x, 1 - y)]
    return [(p, 2 * p[0] + p[1]) for p in ps]


def _remote(src, dst, send_sem, recv_sem, dev):
    return pltpu.make_async_remote_copy(src_ref=src, dst_ref=dst, send_sem=send_sem, recv_sem=recv_sem,
                                        device_id=dev, device_id_type=MESH)


class Carry:
    def __init__(self, ins, out_shapes, sems, start, finish):
        self.ins, self.out_shapes, self.sems, self.start, self.finish = list(ins), list(out_shapes), list(sems), start, finish


def _pcall(body, *, name, grid, in_specs, out_specs, out_shape, scratch_shapes=(), dims, args, carries=()):
    in_specs, out_specs, out_shape = list(in_specs), list(out_specs), list(out_shape)
    scratch_shapes, args = list(scratch_shapes), list(args)
    n_in, n_out, n_scr = len(in_specs), len(out_shape), len(scratch_shapes)
    carries = [c for c in carries if c is not None]
    if not carries:
        res = pl.pallas_call(body, name=name, grid=grid, in_specs=in_specs, out_specs=out_specs, out_shape=out_shape,
                             scratch_shapes=scratch_shapes, compiler_params=_cparams(dims))(*args)
        return list(res), []
    ci = [len(c.ins) for c in carries]
    co = [len(c.out_shapes) for c in carries]
    cs = [len(c.sems) for c in carries]

    def split(seq, sizes):
        out, k = [], 0
        for s in sizes:
            out.append(seq[k:k + s])
            k += s
        return out

    def carried(*refs):
        ins, cins, outs, couts, scr, sems = split(refs, [n_in, sum(ci), n_out, sum(co), n_scr, sum(cs)])
        cins, couts, sems = split(cins, ci), split(couts, co), split(sems, cs)
        ids = [pl.program_id(a) for a in range(len(grid))]
        first, last = ids[0] == 0, ids[0] == grid[0] - 1
        for a in range(1, len(grid)):
            first, last = first & (ids[a] == 0), last & (ids[a] == grid[a] - 1)

        @pl.when(first)
        def _():
            for c, i, o, s in zip(carries, cins, couts, sems):
                c.start(i, o, s)

        body(*ins, *outs, *scr)

        @pl.when(last)
        def _():
            for c, i, o, s in zip(carries, cins, couts, sems):
                c.finish(i, o, s)

    res = pl.pallas_call(
        carried, name=name, grid=grid,
        in_specs=in_specs + [_ANY] * sum(ci), out_specs=out_specs + [_ANY] * sum(co),
        out_shape=out_shape + [s for c in carries for s in c.out_shapes],
        scratch_shapes=scratch_shapes + [s for c in carries for s in c.sems],
        compiler_params=_cparams(("arbitrary",) * len(grid)),
    )(*args, *[a for c in carries for a in c.ins])
    return list(res[:n_out]), split(list(res[n_out:]), co)


def run_carries(carries, name):
    ci = [len(c.ins) for c in carries]
    co = [len(c.out_shapes) for c in carries]
    cs = [len(c.sems) for c in carries]

    def split(seq, sizes):
        out, k = [], 0
        for s in sizes:
            out.append(seq[k:k + s])
            k += s
        return out

    def body(*refs):
        cins, couts, sems = split(refs, [sum(ci), sum(co), sum(cs)])
        cins, couts, sems = split(cins, ci), split(couts, co), split(sems, cs)
        for c, i, o, s in zip(carries, cins, couts, sems):
            c.start(i, o, s)
        for c, i, o, s in zip(carries, cins, couts, sems):
            c.finish(i, o, s)

    res = pl.pallas_call(
        body, name=name, in_specs=[_ANY] * sum(ci), out_specs=[_ANY] * sum(co),
        out_shape=[s for c in carries for s in c.out_shapes], scratch_shapes=[s for c in carries for s in c.sems],
    )(*[a for c in carries for a in c.ins])
    return split(list(res), co)


def gather_carry(arrs):
    n = len(arrs)

    def copies(ins, outs, sems):
        send_sems, recv_sems, local_sems = sems
        x, y, c = _mesh_pos()
        me = 2 * x + y
        peers = _other_chips(x, y)
        locs = [pltpu.make_async_copy(ins[a], outs[a].at[me], local_sems.at[a]) for a in range(n)]
        sends = [_remote(ins[a], outs[a].at[me], send_sems.at[a, j], recv_sems.at[a, j], (px, py, c))
                 for a in range(n) for j, ((px, py), _) in enumerate(peers)]
        recvs = [_remote(ins[a], outs[a].at[t], send_sems.at[a, j], recv_sems.at[a, j], (px, py, c))
                 for a in range(n) for j, ((px, py), t) in enumerate(peers)]
        return locs, sends, recvs

    def start(ins, outs, sems):
        locs, sends, _ = copies(ins, outs, sems)
        for cp in locs + sends:
            cp.start()

    def finish(ins, outs, sems):
        locs, sends, recvs = copies(ins, outs, sems)
        for r in recvs:
            r.wait_recv()
        for r in sends:
            r.wait_send()
        for cp in locs:
            cp.wait()

    return Carry(arrs, [jax.ShapeDtypeStruct((N_CHIPS,) + a.shape, a.dtype) for a in arrs],
                 [pltpu.SemaphoreType.DMA((n, 3)), pltpu.SemaphoreType.DMA((n, 3)), pltpu.SemaphoreType.DMA((n,))],
                 start, finish)


def sibling_half_carry(gs):
    n = len(gs)

    def copies(ins, outs, sems):
        send_sems, recv_sems = sems
        x, y, c = _mesh_pos()
        out = []
        for a in range(n):
            half = gs[a].shape[1] // 2
            out.append(_remote(ins[a].at[:, pl.ds((1 - c) * half, half), :], outs[a], send_sems.at[a], recv_sems.at[a],
                               (x, y, 1 - c)))
        return out

    def start(ins, outs, sems):
        for r in copies(ins, outs, sems):
            r.start()

    def finish(ins, outs, sems):
        cps = copies(ins, outs, sems)
        for r in cps:
            r.wait_recv()
        for r in cps:
            r.wait_send()

    return Carry(gs, [jax.ShapeDtypeStruct((g.shape[0], g.shape[1] // 2, g.shape[2]), g.dtype) for g in gs],
                 [pltpu.SemaphoreType.DMA((n,)), pltpu.SemaphoreType.DMA((n,))], start, finish)


def chips_carry(ps):
    n = len(ps)

    def copies(ins, outs, sems):
        send_sems, recv_sems, local_sems = sems
        x, y, c = _mesh_pos()
        me = 2 * x + y
        peers = _other_chips(x, y)
        locs = [pltpu.make_async_copy(ins[a].at[me], outs[a].at[me], local_sems.at[a]) for a in range(n)]
        sends = [_remote(ins[a].at[t], outs[a].at[me], send_sems.at[a, j], recv_sems.at[a, j], (px, py, c))
                 for a in range(n) for j, ((px, py), t) in enumerate(peers)]
        recvs = [_remote(ins[a].at[t], outs[a].at[t], send_sems.at[a, j], recv_sems.at[a, j], (px, py, c))
                 for a in range(n) for j, ((px, py), t) in enumerate(peers)]
        return locs, sends, recvs

    def start(ins, outs, sems):
        locs, sends, _ = copies(ins, outs, sems)
        for cp in locs + sends:
            cp.start()

    def finish(ins, outs, sems):
        locs, sends, recvs = copies(ins, outs, sems)
        for r in recvs:
            r.wait_recv()
        for r in sends:
            r.wait_send()
        for cp in locs:
            cp.wait()

    return Carry(ps, [jax.ShapeDtypeStruct(p.shape, p.dtype) for p in ps],
                 [pltpu.SemaphoreType.DMA((n, 3)), pltpu.SemaphoreType.DMA((n, 3)), pltpu.SemaphoreType.DMA((n,))],
                 start, finish)


def sibling_copy_carry(arrs):
    n = len(arrs)

    def copies(ins, outs, sems):
        send_sems, recv_sems = sems
        x, y, c = _mesh_pos()
        return [_remote(ins[a], outs[a], send_sems.at[a], recv_sems.at[a], (x, y, 1 - c)) for a in range(n)]

    def start(ins, outs, sems):
        for r in copies(ins, outs, sems):
            r.start()

    def finish(ins, outs, sems):
        cps = copies(ins, outs, sems)
        for r in cps:
            r.wait_recv()
        for r in cps:
            r.wait_send()

    return Carry(arrs, [jax.ShapeDtypeStruct(a.shape, a.dtype) for a in arrs],
                 [pltpu.SemaphoreType.DMA((n,)), pltpu.SemaphoreType.DMA((n,))], start, finish)


def gather_all_carry(small):
    flips = [(fx, fy, fc) for fx in (0, 1) for fy in (0, 1) for fc in (0, 1)][1:]

    def copies(ins, outs, sems):
        send_sems, recv_sems, local_sem = sems
        in_ref, out_ref = ins[0], outs[0]
        x, y, c = _mesh_pos()
        me = 4 * x + 2 * y + c
        peers = [((1 - x) if fx else x, (1 - y) if fy else y, (1 - c) if fc else c) for fx, fy, fc in flips]
        loc = pltpu.make_async_copy(in_ref, out_ref.at[me], local_sem.at[0])
        sends = [_remote(in_ref, out_ref.at[me], send_sems.at[k], recv_sems.at[k], p) for k, p in enumerate(peers)]
        recvs = [_remote(in_ref, out_ref.at[4 * p[0] + 2 * p[1] + p[2]], send_sems.at[k], recv_sems.at[k], p)
                 for k, p in enumerate(peers)]
        return loc, sends, recvs

    def start(ins, outs, sems):
        loc, sends, _ = copies(ins, outs, sems)
        loc.start()
        for r in sends:
            r.start()

    def finish(ins, outs, sems):
        loc, sends, recvs = copies(ins, outs, sems)
        for r in recvs:
            r.wait_recv()
        for r in sends:
            r.wait_send()
        loc.wait()

    return Carry([small], [jax.ShapeDtypeStruct((N_DEV,) + small.shape, small.dtype)],
                 [pltpu.SemaphoreType.DMA((N_DEV - 1,)), pltpu.SemaphoreType.DMA((N_DEV - 1,)),
                  pltpu.SemaphoreType.DMA((1,))], start, finish)


HB = 8
HBW = HB * HEAD_DIM


def _chunk_specs(nc, col0, rev=False):
    return pl.BlockSpec((CHUNK, HBW), lambda b, n, g: (b * nc + ((nc - 1 - n) if rev else n), col0 // HB + g))


def _chunk_specs_rev(nc, col0):
    return _chunk_specs(nc, col0, True)


def _halo_spec(nc, col0, rev):
    per = CHUNK // HALO

    def imap(b, n, g):
        nn = (nc - 1 - n) if rev else n
        return (jnp.maximum((b * nc + nn) * per - 1, 0), col0 // HB + g)
    return pl.BlockSpec((HALO, HBW), imap)


def _const_spec(shape):
    nd = len(shape)
    return pl.BlockSpec(shape, lambda b, n, g: (0,) * nd)


def _state_spec(nc, rev=False):
    return pl.BlockSpec((None, None, HB, HEAD_DIM, HEAD_DIM),
                        lambda b, n, g: (b, (nc - 1 - n) if rev else n, g, 0, 0))


def _lanes(hh):
    return slice(hh * HEAD_DIM, (hh + 1) * HEAD_DIM)


def _head(g, hh):
    return hh if HB == HEADS else g * HB + hh


def gdn_forward(proj, conv_w, alog, dtb, onorm, bsz, name, carries=()):
    n_tok = proj.shape[0]
    nc = n_tok // bsz // CHUNK

    def body(xq, xk, xv, hq, hk, hv, gate, ab, cwq, cwk, cwv, al, dt, on, o_ref, ssave_ref, s_ref):
        n, g = pl.program_id(1), pl.program_id(2)
        keep = jnp.where(n > 0, 1.0, 0.0).astype(F32)

        @pl.when(n == 0)
        def _():
            for hh in range(HB):
                s_ref[_head(g, hh)] = jnp.zeros((HEAD_DIM, HEAD_DIM), F32)

        heads = [_head(g, hh) for hh in range(HB)]
        per_head = lambda ref, scale=None: [ref[:, _lanes(hh)] if scale is None else ref[:, _lanes(hh)] * scale
                                            for hh in range(HB)]
        S = [s_ref[h] for h in heads]
        for hh in range(HB):
            ssave_ref[hh] = S[hh]
        outs, s_next = _gdn_chunk(per_head(xq), per_head(xk), per_head(xv), per_head(hq, keep), per_head(hk, keep),
                                  per_head(hv, keep), per_head(gate), ab[...], S, per_head(cwq), per_head(cwk),
                                  per_head(cwv), al[...], dt[...], on[...], heads=heads)
        for hh in range(HB):
            o_ref[:, _lanes(hh)] = outs[hh].astype(o_ref.dtype)
            s_ref[heads[hh]] = s_next[hh]

    hb = HEADS
    cw_spec = lambda col0: pl.BlockSpec((CONV_K, HBW), lambda b, n, g: (0, col0 // HB + g))
    in_specs = [_chunk_specs(nc, 0), _chunk_specs(nc, hb), _chunk_specs(nc, 2 * hb),
                _halo_spec(nc, 0, False), _halo_spec(nc, hb, False), _halo_spec(nc, 2 * hb, False),
                _chunk_specs(nc, 3 * hb),
                pl.BlockSpec((CHUNK, 128), lambda b, n, g: (b * nc + n, 4 * hb)),
                cw_spec(0), cw_spec(hb), cw_spec(2 * hb),
                _const_spec((1, 128)), _const_spec((1, 128)), _const_spec((1, 128))]
    out_specs = [_chunk_specs(nc, 0), _state_spec(nc)]
    return _pcall(
        body, name=name, grid=(bsz, nc, HEADS // HB), in_specs=in_specs, out_specs=out_specs,
        out_shape=(jax.ShapeDtypeStruct((n_tok, D_MODEL), BF16),
                   jax.ShapeDtypeStruct((bsz, nc, HEADS, HEAD_DIM, HEAD_DIM), F32)),
        scratch_shapes=[pltpu.VMEM((HEADS, HEAD_DIM, HEAD_DIM), F32)],
        dims=("arbitrary", "arbitrary", "arbitrary"),
        args=(proj, proj, proj, proj, proj, proj, proj, proj, conv_w, conv_w, conv_w, alog, dtb, onorm),
        carries=carries)


def gdn_backward(proj, conv_w, alog, dtb, onorm, s_saved, d_out, bsz, name, carries=()):
    n_tok = proj.shape[0]
    nc = n_tok // bsz // CHUNK
    assert HB == HEADS

    def body(xq, xk, xv, hq, hk, hv, gate, ab, cwq, cwk, cwv, al, dt, on, ssave, do,
             dp_ref, dcw_ref, dal_ref, ddt_ref, don_ref, ds_ref, dhalo_ref):
        b, n, g = pl.program_id(0), pl.program_id(1), pl.program_id(2)
        nr = nc - 1 - n

        @pl.when((b == 0) & (n == 0) & (g == 0))
        def _():
            dcw_ref[...] = jnp.zeros_like(dcw_ref)
            dal_ref[...] = jnp.zeros_like(dal_ref)
            ddt_ref[...] = jnp.zeros_like(ddt_ref)
            don_ref[...] = jnp.zeros_like(don_ref)

        @pl.when(n == 0)
        def _():
            for hh in range(HB):
                ds_ref[_head(g, hh)] = jnp.zeros((HEAD_DIM, HEAD_DIM), F32)
                dhalo_ref[_head(g, hh)] = jnp.zeros((3, HALO, HEAD_DIM), F32)

        keep = jnp.where(nr > 0, 1.0, 0.0).astype(F32)
        pad = jnp.zeros((CHUNK - HALO, HEAD_DIM), F32)
        heads = [_head(g, hh) for hh in range(HB)]
        per_head = lambda ref, scale=None: [ref[:, _lanes(hh)] if scale is None else ref[:, _lanes(hh)] * scale
                                            for hh in range(HB)]
        args = (per_head(xq), per_head(xk), per_head(xv), per_head(hq, keep), per_head(hk, keep), per_head(hv, keep),
                per_head(gate), ab[...], [ssave[hh] for hh in range(HB)], per_head(cwq), per_head(cwk), per_head(cwv),
                al[...], dt[...], on[...])
        _, vjp = jax.vjp(functools.partial(_gdn_chunk, heads=heads), *args)
        (gxq, gxk, gxv, ghq, ghk, ghv, ggate, gab_sum, gS, gcq, gck, gcv, gal_sum, gdt_sum, gon_sum) = vjp(
            ([do[:, _lanes(hh)] for hh in range(HB)], [ds_ref[h] for h in heads]))
        d = D_MODEL
        for hh in range(HB):
            head, c0 = heads[hh], hh * HEAD_DIM
            dp_ref[:, c0:c0 + HEAD_DIM] = gxq[hh] + jnp.concatenate([pad, dhalo_ref[head, 0]], axis=0)
            dp_ref[:, d + c0:d + c0 + HEAD_DIM] = gxk[hh] + jnp.concatenate([pad, dhalo_ref[head, 1]], axis=0)
            dp_ref[:, 2 * d + c0:2 * d + c0 + HEAD_DIM] = gxv[hh] + jnp.concatenate([pad, dhalo_ref[head, 2]], axis=0)
            dp_ref[:, 3 * d + c0:3 * d + c0 + HEAD_DIM] = ggate[hh]
            dcw_ref[head, 0] += gcq[hh]
            dcw_ref[head, 1] += gck[hh]
            dcw_ref[head, 2] += gcv[hh]
            ds_ref[head] = gS[hh]
            dhalo_ref[head, 0] = ghq[hh] * keep
            dhalo_ref[head, 1] = ghk[hh] * keep
            dhalo_ref[head, 2] = ghv[hh] * keep

        dp_ref[:, GDN_MAIN:GDN_IN_PAD] = gab_sum
        dal_ref[...] += gal_sum
        ddt_ref[...] += gdt_sum
        don_ref[...] += gon_sum

    hb = HEADS
    cw_spec = lambda col0: pl.BlockSpec((CONV_K, HBW), lambda b, n, g: (0, col0 // HB + g))
    in_specs = [_chunk_specs_rev(nc, 0), _chunk_specs_rev(nc, hb), _chunk_specs_rev(nc, 2 * hb),
                _halo_spec(nc, 0, True), _halo_spec(nc, hb, True), _halo_spec(nc, 2 * hb, True),
                _chunk_specs_rev(nc, 3 * hb),
                pl.BlockSpec((CHUNK, 128), lambda b, n, g: (b * nc + (nc - 1 - n), 4 * hb)),
                cw_spec(0), cw_spec(hb), cw_spec(2 * hb),
                _const_spec((1, 128)), _const_spec((1, 128)), _const_spec((1, 128)),
                _state_spec(nc, True),
                _chunk_specs_rev(nc, 0)]
    out_specs = [
        pl.BlockSpec((CHUNK, GDN_IN_PAD), lambda b, n, g: (b * nc + (nc - 1 - n), 0)),
        _const_spec((HEADS, 3, CONV_K, HEAD_DIM)), _const_spec((1, 128)), _const_spec((1, 128)), _const_spec((1, 128))]
    row = jax.ShapeDtypeStruct((1, 128), F32)
    return _pcall(
        body, name=name, grid=(bsz, nc, HEADS // HB), in_specs=in_specs, out_specs=out_specs,
        out_shape=(jax.ShapeDtypeStruct((n_tok, GDN_IN_PAD), F32),
                   jax.ShapeDtypeStruct((HEADS, 3, CONV_K, HEAD_DIM), F32), row, row, row),
        scratch_shapes=[pltpu.VMEM((HEADS, HEAD_DIM, HEAD_DIM), F32), pltpu.VMEM((HEADS, 3, HALO, HEAD_DIM), F32)],
        dims=("arbitrary", "arbitrary", "arbitrary"),
        args=(proj, proj, proj, proj, proj, proj, proj, proj, conv_w, conv_w, conv_w, alog, dtb, onorm, s_saved, d_out),
        carries=carries)


def hgrn_forward(proj, lbl, layer, bsz, carries=()):
    n_tok = proj.shape[0]
    nc = n_tok // bsz // CHUNK

    def body(qp, fp, vi, lb, o_ref, ssave_ref, s_ref):
        n, g = pl.program_id(1), pl.program_id(2)

        @pl.when(n == 0)
        def _():
            for hh in range(HB):
                s_ref[_head(g, hh)] = jnp.zeros((HEAD_DIM, HEAD_DIM), F32)

        for hh in range(HB):
            head = _head(g, hh)
            ln = _lanes(hh)
            S = s_ref[head]
            ssave_ref[hh] = S
            o, s_next = _hgrn_chunk(qp[:, ln], fp[:, ln], vi[:, ln], S, lb[:, ln], layer=layer)
            o_ref[:, ln] = o
            s_ref[head] = s_next

    hb = HEADS
    in_specs = [_chunk_specs(nc, 0), _chunk_specs(nc, hb), _chunk_specs(nc, 2 * hb),
                pl.BlockSpec((DEPTH, HBW), lambda b, n, g: (0, g))]
    out_specs = [_chunk_specs(nc, 0), _state_spec(nc)]
    return _pcall(
        body, name=f"hgrn_fwd{layer}", grid=(bsz, nc, HEADS // HB), in_specs=in_specs, out_specs=out_specs,
        out_shape=(jax.ShapeDtypeStruct((n_tok, D_MODEL), F32),
                   jax.ShapeDtypeStruct((bsz, nc, HEADS, HEAD_DIM, HEAD_DIM), F32)),
        scratch_shapes=[pltpu.VMEM((HEADS, HEAD_DIM, HEAD_DIM), F32)],
        dims=("arbitrary", "arbitrary", "arbitrary"), args=(proj, proj, proj, lbl), carries=carries)


def hgrn_backward(proj, lbl, s_saved, d_o, d_gate, layer, bsz, carries=()):
    n_tok = proj.shape[0]
    nc = n_tok // bsz // CHUNK
    assert HB == HEADS

    def body(qp, fp, vi, lb, ssave, do, dgt, dp_ref, dlb_ref, ds_ref):
        b, n, g = pl.program_id(0), pl.program_id(1), pl.program_id(2)

        @pl.when((b == 0) & (n == 0) & (g == 0))
        def _():
            dlb_ref[...] = jnp.zeros_like(dlb_ref)

        @pl.when(n == 0)
        def _():
            for hh in range(HB):
                ds_ref[_head(g, hh)] = jnp.zeros((HEAD_DIM, HEAD_DIM), F32)

        for hh in range(HB):
            head = _head(g, hh)
            ln = _lanes(hh)
            _, vjp = jax.vjp(functools.partial(_hgrn_chunk, layer=layer), qp[:, ln], fp[:, ln], vi[:, ln], ssave[hh],
                             lb[:, ln])
            gq, gf, gv, gS, glb = vjp((do[:, ln], ds_ref[head]))
            c0, d = hh * HEAD_DIM, D_MODEL
            dp_ref[:, c0:c0 + HEAD_DIM] = gq
            dp_ref[:, d + c0:d + c0 + HEAD_DIM] = gf
            dp_ref[:, 2 * d + c0:2 * d + c0 + HEAD_DIM] = gv
            dlb_ref[head] += glb
            ds_ref[head] = gS
        dp_ref[:, 3 * D_MODEL:4 * D_MODEL] = dgt[...]

    hb = HEADS
    in_specs = [_chunk_specs_rev(nc, 0), _chunk_specs_rev(nc, hb), _chunk_specs_rev(nc, 2 * hb),
                pl.BlockSpec((DEPTH, HBW), lambda b, n, g: (0, g)),
                _state_spec(nc, True),
                _chunk_specs_rev(nc, 0), _chunk_specs_rev(nc, 0)]
    out_specs = [pl.BlockSpec((CHUNK, 4 * D_MODEL), lambda b, n, g: (b * nc + (nc - 1 - n), 0)),
                 _const_spec((HEADS, DEPTH, HEAD_DIM))]
    return _pcall(
        body, name=f"hgrn_bwd{layer}", grid=(bsz, nc, HEADS // HB), in_specs=in_specs, out_specs=out_specs,
        out_shape=(jax.ShapeDtypeStruct((n_tok, 4 * D_MODEL), F32), jax.ShapeDtypeStruct((HEADS, DEPTH, HEAD_DIM), F32)),
        scratch_shapes=[pltpu.VMEM((HEADS, HEAD_DIM, HEAD_DIM), F32)],
        dims=("arbitrary", "arbitrary", "arbitrary"), args=(proj, proj, proj, lbl, s_saved, d_o, d_gate),
        carries=carries)


ROW_TILE = 512


def _tile(n):
    for cand in (1024, 512, 1408, 384, 256, 128):
        if n % cand == 0:
            return cand
    return n


def _rmsnorm(x, w):
    return x * lax.rsqrt(jnp.mean(x * x, axis=1, keepdims=True) + EPS) * w


def norm_matmul(h, nw, w, relu2, name, carries=()):
    n_tok, d = h.shape
    slots = w.ndim == 3
    n_out = w.shape[0] * w.shape[2] if slots else w.shape[1]
    tm, tn = min(n_tok, ROW_TILE), _tile(w.shape[2] if slots else n_out)
    if slots:
        per = w.shape[2] // tn
        w_spec = pl.BlockSpec((None, d, tn), lambda i, j: (j // per, 0, j % per))
    else:
        w_spec = pl.BlockSpec((d, tn), lambda i, j: (0, j))

    def body(h_ref, nw_ref, w_ref, y_ref, *outs):
        @pl.when(pl.program_id(1) == 0)
        def _():
            y_ref[...] = _rmsnorm(h_ref[...], nw_ref[...]).astype(BF16)

        acc = jnp.dot(y_ref[...], w_ref[...], preferred_element_type=F32)
        if relu2:
            r = jnp.maximum(acc, 0.0)
            outs[0][...] = (r * r).astype(BF16)
            outs[1][...] = r.astype(BF16)
        else:
            outs[0][...] = acc

    o_spec = pl.BlockSpec((tm, tn), lambda i, j: (i, j))
    if relu2:
        outs = (jax.ShapeDtypeStruct((n_tok, n_out), BF16),) * 2
    else:
        outs = (jax.ShapeDtypeStruct((n_tok, n_out), F32),)
    return _pcall(
        body, name=name, grid=(n_tok // tm, n_out // tn),
        in_specs=[pl.BlockSpec((tm, d), lambda i, j: (i, 0)), pl.BlockSpec((1, d), lambda i, j: (0, 0)), w_spec],
        out_specs=[pl.BlockSpec((tm, d), lambda i, j: (i, 0))] + [o_spec] * len(outs),
        out_shape=(jax.ShapeDtypeStruct((n_tok, d), BF16),) + outs,
        dims=("parallel", "arbitrary"), args=(h, nw, w), carries=carries)


def matmul(a, b, mode, name, out_dtype=F32, extra=None, epilogue=None, shards=1, carries=()):
    slots = mode == "nt" and b.ndim == 3
    if mode == "nn":
        (m, k), n = a.shape, b.shape[1]
    elif mode == "nt":
        (m, k), n = a.shape, (b.shape[1] if slots else b.shape[0])
    else:
        (k, m), n = a.shape, b.shape[1]
    tm, tk = min(m, ROW_TILE), _tile(b.shape[2] if slots else k)
    tn = _tile(n // shards)
    nk = k // tk
    a_spec = pl.BlockSpec((tk, tm), lambda i, j, kk: (kk, i)) if mode == "tn" else pl.BlockSpec((tm, tk), lambda i, j, kk: (i, kk))
    if slots:
        kper = b.shape[2] // tk
        b_spec = pl.BlockSpec((None, tn, tk), lambda i, j, kk: (kk // kper, j, kk % kper))
    elif mode == "nt":
        b_spec = pl.BlockSpec((tn, tk), lambda i, j, kk: (j, kk))
    else:
        b_spec = pl.BlockSpec((tk, tn), lambda i, j, kk: (kk, j))
    dims = _DIMS[mode]

    def body(*refs):
        if extra is not None:
            a_ref, b_ref, e_ref, o_ref, acc_ref = refs
        else:
            a_ref, b_ref, o_ref, acc_ref = refs
        kk = pl.program_id(2)

        @pl.when(kk == 0)
        def _():
            acc_ref[...] = jnp.zeros_like(acc_ref)

        acc_ref[...] += lax.dot_general(a_ref[...].astype(BF16), b_ref[...].astype(BF16), dims,
                                        preferred_element_type=F32)

        @pl.when(kk == nk - 1)
        def _():
            acc = acc_ref[...]
            if epilogue == "add":
                acc = e_ref[...] + acc
            elif epilogue == "mul2":
                acc = acc * (2.0 * e_ref[...].astype(F32))
            o_ref[...] = acc.astype(o_ref.dtype)

    in_specs = [a_spec, b_spec]
    args = [a, b]
    if extra is not None:
        in_specs.append(pl.BlockSpec((tm, tn), lambda i, j, kk: (i, j)))
        args.append(extra)
    if shards > 1:
        per = n // shards // tn
        out_spec = pl.BlockSpec((None, tm, tn), lambda i, j, kk: (j // per, i, j % per))
        out_shape = jax.ShapeDtypeStruct((shards, m, n // shards), out_dtype)
    else:
        out_spec = pl.BlockSpec((tm, tn), lambda i, j, kk: (i, j))
        out_shape = jax.ShapeDtypeStruct((m, n), out_dtype)
    (res,), cres = _pcall(
        body, name=name, grid=(m // tm, n // tn, nk), in_specs=in_specs, out_specs=[out_spec], out_shape=[out_shape],
        scratch_shapes=[pltpu.VMEM((tm, tn), F32)], dims=("parallel", "parallel", "arbitrary"), args=args,
        carries=carries)
    return res, cres


def norm_backward(h, nw, dy, dres, name):
    n_tok, d = h.shape
    tm = min(n_tok, ROW_TILE)

    def body(h_ref, nw_ref, dy_ref, dr_ref, dh_ref, dnw_ref):
        @pl.when(pl.program_id(0) == 0)
        def _():
            dnw_ref[...] = jnp.zeros_like(dnw_ref)

        _, vjp = jax.vjp(_rmsnorm, h_ref[...], nw_ref[...])
        gh, gw = vjp(dy_ref[...])
        dh_ref[...] = dr_ref[...] + gh
        dnw_ref[...] += gw

    row = pl.BlockSpec((tm, d), lambda i: (i, 0))
    vec = pl.BlockSpec((1, d), lambda i: (0, 0))
    return pl.pallas_call(
        body, name=name, grid=(n_tok // tm,), in_specs=[row, vec, row, row], out_specs=[row, vec],
        out_shape=(jax.ShapeDtypeStruct((n_tok, d), F32), jax.ShapeDtypeStruct((1, d), F32)),
        compiler_params=_cparams(("arbitrary",)),
    )(h, nw, dy, dres)


def _hgrn_post(o, gate, gw):
    return _rmsnorm(o, gw) * _silu(gate)


def hgrn_post_forward(o, proj, gw, name):
    n_tok, d = o.shape
    tm = min(n_tok, ROW_TILE)

    def body(o_ref, g_ref, w_ref, y_ref):
        y_ref[...] = _hgrn_post(o_ref[...], g_ref[...], w_ref[...]).astype(BF16)

    row = pl.BlockSpec((tm, d), lambda i: (i, 0))
    return pl.pallas_call(
        body, name=name, grid=(n_tok // tm,),
        in_specs=[row, pl.BlockSpec((tm, d), lambda i: (i, 3)), pl.BlockSpec((1, d), lambda i: (0, 0))],
        out_specs=row, out_shape=jax.ShapeDtypeStruct((n_tok, d), BF16),
        compiler_params=_cparams(("parallel",)),
    )(o, proj, gw)


def hgrn_post_backward(o, proj, gw, dy, name):
    n_tok, d = o.shape
    tm = min(n_tok, ROW_TILE)

    def body(o_ref, g_ref, w_ref, dy_ref, do_ref, dg_ref, dw_ref):
        @pl.when(pl.program_id(0) == 0)
        def _():
            dw_ref[...] = jnp.zeros_like(dw_ref)

        _, vjp = jax.vjp(_hgrn_post, o_ref[...], g_ref[...], w_ref[...])
        go, gg, gw_ = vjp(dy_ref[...])
        do_ref[...] = go
        dg_ref[...] = gg
        dw_ref[...] += gw_

    row = pl.BlockSpec((tm, d), lambda i: (i, 0))
    vec = pl.BlockSpec((1, d), lambda i: (0, 0))
    tok = jax.ShapeDtypeStruct((n_tok, d), F32)
    return pl.pallas_call(
        body, name=name, grid=(n_tok // tm,),
        in_specs=[row, pl.BlockSpec((tm, d), lambda i: (i, 3)), vec, row], out_specs=[row, row, vec],
        out_shape=(tok, tok, jax.ShapeDtypeStruct((1, d), F32)),
        compiler_params=_cparams(("arbitrary",)),
    )(o, proj, gw, dy)


def loss_head(h, nw, target):
    n_tok, d = h.shape
    tm = min(n_tok, ROW_TILE)

    def body(h_ref, nw_ref, t_ref, loss_ref, dh_ref, dnw_ref):
        @pl.when(pl.program_id(0) == 0)
        def _():
            dnw_ref[...] = jnp.zeros_like(dnw_ref)
            loss_ref[...] = jnp.zeros_like(loss_ref)

        out, vjp = jax.vjp(_rmsnorm, h_ref[...], nw_ref[...])
        err = out - t_ref[...]
        part = 0.5 * jnp.sum(jnp.sum(err * err, axis=1, keepdims=True), axis=0, keepdims=True) / d
        loss_ref[...] += jnp.broadcast_to(part, loss_ref.shape)
        gh, gw = vjp(err / d)
        dh_ref[...] = gh
        dnw_ref[...] += gw

    row = pl.BlockSpec((tm, d), lambda i: (i, 0))
    vec = pl.BlockSpec((1, d), lambda i: (0, 0))
    return pl.pallas_call(
        body, name="loss_head", grid=(n_tok // tm,), in_specs=[row, vec, row],
        out_specs=[pl.BlockSpec((1, 128), lambda i: (0, 0)), row, vec],
        out_shape=(jax.ShapeDtypeStruct((1, 128), F32), jax.ShapeDtypeStruct((n_tok, d), F32),
                   jax.ShapeDtypeStruct((1, d), F32)),
        compiler_params=_cparams(("arbitrary",)),
    )(h, nw, target)


def _rows2d(shape):
    if len(shape) == 1:
        return (1, shape[0])
    return (math.prod(shape[:-1]), shape[-1])


def adamw(w, g, m, v, name):
    shape = w.shape
    r, c = _rows2d(shape)
    tr = r if r <= 256 else 256
    c1 = 1.0 / (1.0 - ADAM_B1 ** ADAM_STEP)
    c2 = 1.0 / (1.0 - ADAM_B2 ** ADAM_STEP)

    def body(w_ref, g_ref, m_ref, v_ref, d_ref, nm_ref, nv_ref):
        gg = g_ref[...]
        nm = ADAM_B1 * m_ref[...] + (1.0 - ADAM_B1) * gg
        nv = ADAM_B2 * v_ref[...] + (1.0 - ADAM_B2) * (gg * gg)
        d_ref[...] = -ADAM_LR * ((nm * c1) / (jnp.sqrt(nv * c2) + ADAM_EPS) + ADAM_WD * w_ref[...])
        nm_ref[...] = nm
        nv_ref[...] = nv

    spec = pl.BlockSpec((tr, c), lambda i: (i, 0))
    sds = jax.ShapeDtypeStruct((r, c), F32)
    outs = pl.pallas_call(
        body, name=name, grid=(r // tr,), in_specs=[spec] * 4, out_specs=[spec] * 3, out_shape=(sds,) * 3,
        compiler_params=_cparams(("parallel",)),
    )(w.reshape(r, c), g.reshape(r, c), m.reshape(r, c), v.reshape(r, c))
    return tuple(o.reshape(shape) for o in outs)


def add_slots(parts, name):
    s, r, c = parts.shape
    tr = r if r <= 256 else 256

    def body(p_ref, o_ref):
        acc = p_ref[0]
        for t in range(1, s):
            acc = acc + p_ref[t]
        o_ref[...] = acc

    return pl.pallas_call(
        body, name=name, grid=(r // tr,), in_specs=[pl.BlockSpec((s, tr, c), lambda i: (0, i, 0))],
        out_specs=pl.BlockSpec((tr, c), lambda i: (i, 0)), out_shape=jax.ShapeDtypeStruct((r, c), F32),
        compiler_params=_cparams(("parallel",)),
    )(parts)


def add_pair(a, b, name):
    s, r, c = a.shape
    tr = r if r <= 256 else 256

    def body(a_ref, b_ref, o_ref):
        o_ref[...] = a_ref[...] + b_ref[...]

    spec = pl.BlockSpec((None, tr, c), lambda t, i: (t, i, 0))
    return pl.pallas_call(
        body, name=name, grid=(s, r // tr), in_specs=[spec, spec], out_specs=spec,
        out_shape=jax.ShapeDtypeStruct((s, r, c), F32), compiler_params=_cparams(("parallel", "parallel")),
    )(a, b)


N_CHIPS = 4
N_DEV = 8
_ANY = pl.BlockSpec(memory_space=pl.ANY)


def _mesh_pos():
    return lax.axis_index("x"), lax.axis_index("y"), lax.axis_index("c")


def _other_chips(x, y):
    ps = [(1 - x, y), (x, 1 - y), (1 - x, 1 - y)]
    return [(p, 2 * p[0] + p[1]) for p in ps]


def _remote(src, dst, send_sem, recv_sem, dev):
    return pltpu.make_async_remote_copy(src_ref=src, dst_ref=dst, send_sem=send_sem, recv_sem=recv_sem,
                                        device_id=dev, device_id_type=MESH)


def gather_chips(arrs, name):
    n = len(arrs)

    def body(*refs):
        ins, outs = refs[:n], refs[n:2 * n]
        send_sems, recv_sems, local_sems = refs[2 * n:]
        x, y, c = _mesh_pos()
        me = 2 * x + y
        peers = _other_chips(x, y)
        locs, sends = [], []
        for a in range(n):
            cp = pltpu.make_async_copy(ins[a], outs[a].at[me], local_sems.at[a])
            cp.start()
            locs.append(cp)
            for j, ((px, py), _) in enumerate(peers):
                r = _remote(ins[a], outs[a].at[me], send_sems.at[a, j], recv_sems.at[a, j], (px, py, c))
                r.start()
                sends.append(r)
        for a in range(n):
            for j, ((px, py), t) in enumerate(peers):
                _remote(ins[a], outs[a].at[t], send_sems.at[a, j], recv_sems.at[a, j], (px, py, c)).wait_recv()
        for r in sends:
            r.wait_send()
        for cp in locs:
            cp.wait()

    return pl.pallas_call(
        body, name=name, in_specs=[_ANY] * n, out_specs=[_ANY] * n,
        out_shape=[jax.ShapeDtypeStruct((N_CHIPS,) + a.shape, a.dtype) for a in arrs],
        scratch_shapes=[pltpu.SemaphoreType.DMA((n, 3)), pltpu.SemaphoreType.DMA((n, 3)), pltpu.SemaphoreType.DMA((n,))],
    )(*arrs)


def exchange_sibling_half(gs, name):
    n = len(gs)

    def body(*refs):
        ins, outs = refs[:n], refs[n:2 * n]
        send_sems, recv_sems = refs[2 * n:]
        x, y, c = _mesh_pos()
        sends = []
        for a in range(n):
            half = gs[a].shape[1] // 2
            src = ins[a].at[:, pl.ds((1 - c) * half, half), :]
            r = _remote(src, outs[a], send_sems.at[a], recv_sems.at[a], (x, y, 1 - c))
            r.start()
            sends.append(r)
        for r in sends:
            r.wait_recv()
        for r in sends:
            r.wait_send()

    return pl.pallas_call(
        body, name=name, in_specs=[_ANY] * n, out_specs=[_ANY] * n,
        out_shape=[jax.ShapeDtypeStruct((g.shape[0], g.shape[1] // 2, g.shape[2]), g.dtype) for g in gs],
        scratch_shapes=[pltpu.SemaphoreType.DMA((n,)), pltpu.SemaphoreType.DMA((n,))],
    )(*gs)


def exchange_chips(ps, name):
    n = len(ps)

    def body(*refs):
        ins, outs = refs[:n], refs[n:2 * n]
        send_sems, recv_sems, local_sems = refs[2 * n:]
        x, y, c = _mesh_pos()
        me = 2 * x + y
        peers = _other_chips(x, y)
        locs, sends = [], []
        for a in range(n):
            cp = pltpu.make_async_copy(ins[a].at[me], outs[a].at[me], local_sems.at[a])
            cp.start()
            locs.append(cp)
            for j, ((px, py), t) in enumerate(peers):
                r = _remote(ins[a].at[t], outs[a].at[me], send_sems.at[a, j], recv_sems.at[a, j], (px, py, c))
                r.start()
                sends.append(r)
        for a in range(n):
            for j, ((px, py), t) in enumerate(peers):
                _remote(ins[a].at[t], outs[a].at[t], send_sems.at[a, j], recv_sems.at[a, j], (px, py, c)).wait_recv()
        for r in sends:
            r.wait_send()
        for cp in locs:
            cp.wait()

    return pl.pallas_call(
        body, name=name, in_specs=[_ANY] * n, out_specs=[_ANY] * n,
        out_shape=[jax.ShapeDtypeStruct(p.shape, p.dtype) for p in ps],
        scratch_shapes=[pltpu.SemaphoreType.DMA((n, 3)), pltpu.SemaphoreType.DMA((n, 3)), pltpu.SemaphoreType.DMA((n,))],
    )(*ps)


def share_sibling(groups, name):
    flat = [f for grp in groups for f in grp]
    n = len(flat)
    nw = len(groups)

    def body(*refs):
        ins, outs = refs[:n], refs[n:n + nw]
        send_sems, recv_sems, local_sems = refs[n + nw:]
        x, y, c = _mesh_pos()
        locs, sends, k = [], [], 0
        for w, grp in enumerate(groups):
            for l in range(len(grp)):
                half = grp[l].shape[0]
                dst = outs[w].at[l, pl.ds(c * half, half), :]
                cp = pltpu.make_async_copy(ins[k], dst, local_sems.at[k])
                cp.start()
                locs.append(cp)
                r = _remote(ins[k], dst, send_sems.at[k], recv_sems.at[k], (x, y, 1 - c))
                r.start()
                sends.append(r)
                k += 1
        k = 0
        for w, grp in enumerate(groups):
            for l in range(len(grp)):
                half = grp[l].shape[0]
                theirs = outs[w].at[l, pl.ds((1 - c) * half, half), :]
                _remote(ins[k], theirs, send_sems.at[k], recv_sems.at[k], (x, y, 1 - c)).wait_recv()
                k += 1
        for r in sends:
            r.wait_send()
        for cp in locs:
            cp.wait()

    return pl.pallas_call(
        body, name=name, in_specs=[_ANY] * n, out_specs=[_ANY] * nw,
        out_shape=[jax.ShapeDtypeStruct((len(grp), 2 * grp[0].shape[0], grp[0].shape[1]), F32) for grp in groups],
        scratch_shapes=[pltpu.SemaphoreType.DMA((n,)), pltpu.SemaphoreType.DMA((n,)), pltpu.SemaphoreType.DMA((n,))],
    )(*flat)


def gather_all(small, name):
    flips = [(fx, fy, fc) for fx in (0, 1) for fy in (0, 1) for fc in (0, 1)][1:]

    def body(in_ref, out_ref, send_sems, recv_sems, local_sem):
        x, y, c = _mesh_pos()
        me = 4 * x + 2 * y + c
        cp = pltpu.make_async_copy(in_ref, out_ref.at[me], local_sem)
        cp.start()
        peers = [((1 - x) if fx else x, (1 - y) if fy else y, (1 - c) if fc else c) for fx, fy, fc in flips]
        sends = []
        for k, p in enumerate(peers):
            r = _remote(in_ref, out_ref.at[me], send_sems.at[k], recv_sems.at[k], p)
            r.start()
            sends.append(r)
        for k, p in enumerate(peers):
            _remote(in_ref, out_ref.at[4 * p[0] + 2 * p[1] + p[2]], send_sems.at[k], recv_sems.at[k], p).wait_recv()
        for r in sends:
            r.wait_send()
        cp.wait()

    return pl.pallas_call(
        body, name=name, in_specs=[_ANY], out_specs=_ANY,
        out_shape=jax.ShapeDtypeStruct((N_DEV,) + small.shape, small.dtype),
        scratch_shapes=[pltpu.SemaphoreType.DMA((N_DEV - 1,)), pltpu.SemaphoreType.DMA((N_DEV - 1,)),
                        pltpu.SemaphoreType.DMA],
    )(small)


BIG = ("gdn_w_in", "gdn_w_out", "hgrn_w_in", "hgrn_w_out", "mlp_w_up", "mlp_w_down")
WEIGHTS = ("gdn_w_in", "gdn_conv", "gdn_a_log", "gdn_dt_bias", "gdn_onorm", "gdn_w_out", "hgrn_w_in", "hgrn_lb_logits",
           "hgrn_gnorm", "hgrn_w_out", "norm_mix", "norm_mlp", "mlp_w_up", "mlp_w_down", "norm_final")
SMALL_ROWS = 96


def _pad_lanes(v, n):
    return jnp.pad(v, [(0, 0)] * (v.ndim - 1) + [(0, n - v.shape[-1])])


def _local_grads(x, target, w):
    bsz, t_len, d = x.shape
    n_tok = bsz * t_len
    h = x.reshape(n_tok, d)
    lbl = w["hgrn_lb_logits"]
    saved = []
    for i in range(DEPTH):
        j = i // 2
        nmix = w["norm_mix"][i][None, :]
        if i % 2 == 0:
            y, proj = norm_matmul(h, nmix, w["gdn_w_in"][j], False, f"in_proj{i}")
            al = _pad_lanes(w["gdn_a_log"][j][None, :], 128)
            dtb = _pad_lanes(w["gdn_dt_bias"][j][None, :], 128)
            on = w["gdn_onorm"][j][None, :]
            og, ssave = gdn_forward(proj, w["gdn_conv"][j], al, dtb, on, bsz)
            mix = (proj, ssave, al, dtb, on)
            w_out = w["gdn_w_out"][j]
        else:
            y, proj = norm_matmul(h, nmix, w["hgrn_w_in"][j], False, f"in_proj{i}")
            o, ssave = hgrn_forward(proj, lbl, i, bsz)
            gn = w["hgrn_gnorm"][j][None, :]
            og = hgrn_post_forward(o, proj, gn, f"hgrn_post{i}")
            mix = (proj, ssave, o, gn)
            w_out = w["hgrn_w_out"][j]
        h1 = matmul(og, w_out, "nn", f"out_proj{i}", extra=h, epilogue="add")
        nmlp = w["norm_mlp"][i][None, :]
        z, a, r = norm_matmul(h1, nmlp, w["mlp_w_up"][i], True, f"mlp_up{i}")
        h2 = matmul(a, w["mlp_w_down"][i], "nn", f"mlp_down{i}", extra=h1, epilogue="add")
        saved.append((h, nmix, y, mix, og, w_out, h1, nmlp, z, a, r))
        h = h2
    loss_row, dh, d_nf = loss_head(h, w["norm_final"][None, :], target.reshape(n_tok, d))

    big = {k: [None] * (DEPTH if k.startswith("mlp") else DEPTH // 2) for k in BIG}
    d_nmix, d_nmlp = [None] * DEPTH, [None] * DEPTH
    d_conv, d_alog, d_dtb, d_onorm, d_gnorm = [None] * 2, [None] * 2, [None] * 2, [None] * 2, [None] * 2
    d_lbl = jnp.zeros((DEPTH, d), F32)
    for i in reversed(range(DEPTH)):
        j = i // 2
        h_in, nmix, y, mix, og, w_out, h1, nmlp, z, a, r = saved[i]
        du = matmul(dh, w["mlp_w_down"][i], "nt", f"d_mlp_act{i}", out_dtype=BF16, extra=r, epilogue="mul2")
        big["mlp_w_down"][i] = matmul(a, dh, "tn", f"dw_down{i}").reshape(N_CHIPS, -1, d)
        dz = matmul(du, w["mlp_w_up"][i], "nt", f"d_mlp_in{i}")
        big["mlp_w_up"][i] = matmul(z, du, "tn", f"dw_up{i}", shards=N_CHIPS)
        dh1, d_nmlp[i] = norm_backward(h1, nmlp, dz, dh, f"d_norm_mlp{i}")
        dog = matmul(dh1, w_out, "nt", f"d_mix_out{i}")
        dw_out = matmul(og, dh1, "tn", f"dw_out{i}").reshape(N_CHIPS, -1, d)
        if i % 2 == 0:
            proj, ssave, al, dtb, on = mix
            dq, dk, dv, dg, dab, dcw, dal, ddt, don = gdn_backward(proj, w["gdn_conv"][j], al, dtb, on, ssave, dog, bsz)
            dproj = jnp.concatenate([dq, dk, dv, dg, dab], axis=1)
            dy = matmul(dproj, w["gdn_w_in"][j], "nt", f"d_in_proj{i}")
            dw_in = matmul(y, dproj, "tn", f"dw_in{i}")[:, :GDN_IN]
            big["gdn_w_in"][j] = jnp.transpose(dw_in.reshape(d, N_CHIPS, GDN_IN // N_CHIPS), (1, 0, 2))
            big["gdn_w_out"][j] = dw_out
            d_conv[j] = jnp.transpose(dcw, (2, 1, 0, 3)).reshape(CONV_K, 3 * d)
            d_alog[j], d_dtb[j], d_onorm[j] = dal[0, :HEADS], ddt[0, :HEADS], don[0]
        else:
            proj, ssave, o, gn = mix
            do_raw, dgate, dgn = hgrn_post_backward(o, proj, gn, dog, f"d_hgrn_post{i}")
            dq, df, dv, dlb = hgrn_backward(proj, lbl, ssave, do_raw, i, bsz)
            dproj = jnp.concatenate([dq, df, dv, dgate], axis=1)
            dy = matmul(dproj, w["hgrn_w_in"][j], "nt", f"d_in_proj{i}")
            big["hgrn_w_in"][j] = matmul(y, dproj, "tn", f"dw_in{i}", shards=N_CHIPS)
            big["hgrn_w_out"][j] = dw_out
            d_gnorm[j] = dgn[0]
            d_lbl = d_lbl + jnp.transpose(dlb, (1, 0, 2)).reshape(DEPTH, d)
        dh, d_nmix[i] = norm_backward(h_in, nmix, dy, dh1, f"d_norm_mix{i}")
    small = {
        "gdn_conv": jnp.stack(d_conv), "gdn_a_log": jnp.stack(d_alog), "gdn_dt_bias": jnp.stack(d_dtb),
        "gdn_onorm": jnp.stack(d_onorm), "hgrn_lb_logits": d_lbl, "hgrn_gnorm": jnp.stack(d_gnorm),
        "norm_mix": jnp.concatenate(d_nmix, axis=0), "norm_mlp": jnp.concatenate(d_nmlp, axis=0), "norm_final": d_nf[0],
    }
    return loss_row, dh.reshape(x.shape), big, small


_SMALL_LAYOUT = {
    "norm_mix": (0, 4, D_MODEL), "norm_mlp": (8, 4, D_MODEL), "norm_final": (16, 1, D_MODEL),
    "hgrn_lb_logits": (24, 4, D_MODEL), "gdn_onorm": (32, 2, 128), "gdn_a_log": (40, 2, HEADS),
    "gdn_dt_bias": (48, 2, HEADS), "loss": (56, 1, 128), "gdn_conv": (64, 24, D_MODEL), "hgrn_gnorm": (88, 2, D_MODEL),
}


def _pack_small(small, loss_row):
    rows = []
    for name, (first, nrow, lanes) in _SMALL_LAYOUT.items():
        v = loss_row if name == "loss" else small[name]
        v = _pad_lanes(v.reshape(nrow, -1), D_MODEL)
        rows.append(jnp.pad(v, ((0, -nrow % 8), (0, 0))))
    return jnp.concatenate(rows, axis=0)


def _unpack_small(packed, name, shape):
    first, nrow, lanes = _SMALL_LAYOUT[name]
    return packed[first:first + nrow, :lanes].reshape(shape)


def kernel(x, gdn_w_in, gdn_conv, gdn_a_log, gdn_dt_bias, gdn_onorm, gdn_w_out, hgrn_w_in, hgrn_lb_logits, hgrn_gnorm, hgrn_w_out, norm_mix, norm_mlp, mlp_w_up, mlp_w_down, norm_final, loss_target, m_gdn_w_in, m_gdn_conv, m_gdn_a_log, m_gdn_dt_bias, m_gdn_onorm, m_gdn_w_out, m_hgrn_w_in, m_hgrn_lb_logits, m_hgrn_gnorm, m_hgrn_w_out, m_norm_mix, m_norm_mlp, m_mlp_w_up, m_mlp_w_down, m_norm_final, v_gdn_w_in, v_gdn_conv, v_gdn_a_log, v_gdn_dt_bias, v_gdn_onorm, v_gdn_w_out, v_hgrn_w_in, v_hgrn_lb_logits, v_hgrn_gnorm, v_hgrn_w_out, v_norm_mix, v_norm_mlp, v_mlp_w_up, v_mlp_w_down, v_norm_final):
    p = dict(gdn_w_in=gdn_w_in, gdn_conv=gdn_conv, gdn_a_log=gdn_a_log, gdn_dt_bias=gdn_dt_bias, gdn_onorm=gdn_onorm,
             gdn_w_out=gdn_w_out, hgrn_w_in=hgrn_w_in, hgrn_lb_logits=hgrn_lb_logits, hgrn_gnorm=hgrn_gnorm,
             hgrn_w_out=hgrn_w_out, norm_mix=norm_mix, norm_mlp=norm_mlp, mlp_w_up=mlp_w_up, mlp_w_down=mlp_w_down,
             norm_final=norm_final)
    m = dict(gdn_w_in=m_gdn_w_in, gdn_conv=m_gdn_conv, gdn_a_log=m_gdn_a_log, gdn_dt_bias=m_gdn_dt_bias,
             gdn_onorm=m_gdn_onorm, gdn_w_out=m_gdn_w_out, hgrn_w_in=m_hgrn_w_in, hgrn_lb_logits=m_hgrn_lb_logits,
             hgrn_gnorm=m_hgrn_gnorm, hgrn_w_out=m_hgrn_w_out, norm_mix=m_norm_mix, norm_mlp=m_norm_mlp,
             mlp_w_up=m_mlp_w_up, mlp_w_down=m_mlp_w_down, norm_final=m_norm_final)
    v = dict(gdn_w_in=v_gdn_w_in, gdn_conv=v_gdn_conv, gdn_a_log=v_gdn_a_log, gdn_dt_bias=v_gdn_dt_bias,
             gdn_onorm=v_gdn_onorm, gdn_w_out=v_gdn_w_out, hgrn_w_in=v_hgrn_w_in, hgrn_lb_logits=v_hgrn_lb_logits,
             hgrn_gnorm=v_hgrn_gnorm, hgrn_w_out=v_hgrn_w_out, norm_mix=v_norm_mix, norm_mlp=v_norm_mlp,
             mlp_w_up=v_mlp_w_up, mlp_w_down=v_mlp_w_down, norm_final=v_norm_final)
    xi, yi, ci = _mesh_pos()
    chip = 2 * xi + yi
    d = D_MODEL

    sharded = list(BIG) + ["gdn_conv", "hgrn_gnorm"]
    got = dict(zip(sharded, gather_chips([p[k].astype(BF16) if k in BIG else p[k] for k in sharded], "gather_weights")))
    cols = lambda g: jnp.transpose(g, (1, 2, 0, 3)).reshape(g.shape[1], g.shape[2], -1)
    rows = lambda g: jnp.transpose(g, (1, 0, 2, 3)).reshape(g.shape[1], -1, g.shape[3])
    w = dict(p)
    w["gdn_w_in"] = _pad_lanes(cols(got["gdn_w_in"]), GDN_IN_PAD)
    w["hgrn_w_in"] = cols(got["hgrn_w_in"])
    w["mlp_w_up"] = cols(got["mlp_w_up"])
    w["gdn_conv"] = cols(got["gdn_conv"])
    w["gdn_w_out"] = rows(got["gdn_w_out"])
    w["hgrn_w_out"] = rows(got["hgrn_w_out"])
    w["mlp_w_down"] = rows(got["mlp_w_down"])
    w["hgrn_gnorm"] = jnp.transpose(got["hgrn_gnorm"], (1, 0, 2)).reshape(got["hgrn_gnorm"].shape[1], -1)

    loss_row, grad_x, big, small = _local_grads(x, loss_target, w)

    order = [(k, l) for k in BIG for l in range(len(big[k]))]
    gs = [big[k][l] for k, l in order]
    theirs = exchange_sibling_half(gs, "reduce_cores")
    mine = [lax.dynamic_slice_in_dim(g, ci * (g.shape[1] // 2), g.shape[1] // 2, axis=1) for g in gs]
    ps = [add_pair(a, b, f"add_cores{n}") for n, (a, b) in enumerate(zip(mine, theirs))]
    slots = exchange_chips(ps, "reduce_chips")
    halves = [add_slots(s, f"add_chips{n}") for n, s in enumerate(slots)]
    groups = [[halves[n] for n, (k, l) in enumerate(order) if k == name] for name in BIG]
    grads = dict(zip(BIG, share_sibling(groups, "share_cores")))

    total = add_slots(gather_all(_pack_small(small, loss_row), "gather_small"), "add_small")
    loss = total[_SMALL_LAYOUT["loss"][0], 0]
    for name in WEIGHTS:
        if name in BIG:
            continue
        if name == "gdn_conv":
            full = _unpack_small(total, name, (2, CONV_K, 3 * d))
            grads[name] = lax.dynamic_slice_in_dim(full, chip * (3 * d // N_CHIPS), 3 * d // N_CHIPS, axis=2)
        elif name == "hgrn_gnorm":
            full = _unpack_small(total, name, (2, d))
            grads[name] = lax.dynamic_slice_in_dim(full, chip * (d // N_CHIPS), d // N_CHIPS, axis=1)
        else:
            grads[name] = _unpack_small(total, name, p[name].shape)

    delta, new_m, new_v = {}, {}, {}
    for name in WEIGHTS:
        delta[name], new_m[name], new_v[name] = adamw(p[name], grads[name], m[name], v[name], f"adamw_{name}")
    return (loss, grad_x, *[grads[n] for n in WEIGHTS], *[delta[n] for n in WEIGHTS],
            *[new_m[n] for n in WEIGHTS], *[new_v[n] for n in WEIGHTS])


def add_core_halves(g, theirs, core, name):
    s, r, c = g.shape
    r2 = r // 2
    tr = min(r2, 256)
    nb = r2 // tr

    def body(core_ref, g_ref, t_ref, o_ref):
        o_ref[...] = g_ref[...] + t_ref[...]

    grid_spec = pltpu.PrefetchScalarGridSpec(
        num_scalar_prefetch=1, grid=(s, nb),
        in_specs=[pl.BlockSpec((None, tr, c), lambda t, i, cr: (t, cr[0] * nb + i, 0)),
                  pl.BlockSpec((None, tr, c), lambda t, i, cr: (t, i, 0))],
        out_specs=pl.BlockSpec((None, tr, c), lambda t, i, cr: (t, i, 0)))
    return pl.pallas_call(body, name=name, grid_spec=grid_spec, out_shape=jax.ShapeDtypeStruct((s, r2, c), F32),
                          compiler_params=_cparams(("parallel", "parallel")))(core, g, theirs)


def add_chip_slots(slots, name):
    n_l = len(slots)
    s, r2, c = slots[0].shape
    tr = min(r2, 256)
    nb = r2 // tr

    def body(*refs):
        ins, o_ref = refs[:n_l], refs[n_l]
        for k in range(n_l):
            @pl.when(pl.program_id(0) == k)
            def _(k=k):
                acc = ins[k][0]
                for t in range(1, s):
                    acc = acc + ins[k][t]
                o_ref[...] = acc

    in_specs = [pl.BlockSpec((s, tr, c), lambda l, i, k=k: (0, jnp.where(l == k, i, 0), 0)) for k in range(n_l)]
    return pl.pallas_call(
        body, name=name, grid=(n_l, nb), in_specs=in_specs, out_specs=pl.BlockSpec((None, tr, c), lambda l, i: (l, i, 0)),
        out_shape=jax.ShapeDtypeStruct((n_l, r2, c), F32), compiler_params=_cparams(("arbitrary", "arbitrary")),
    )(*slots)


def adamw_halves(w, m, v, mine, theirs, name):
    n_l, r, c = w.shape
    r2 = r // 2
    tr = min(r2, 256)
    nb = r2 // tr
    c1 = 1.0 / (1.0 - ADAM_B1 ** ADAM_STEP)
    c2 = 1.0 / (1.0 - ADAM_B2 ** ADAM_STEP)

    def body(w_ref, m_ref, v_ref, mine_ref, theirs_ref, g_ref, d_ref, nm_ref, nv_ref):
        my_half = (pl.program_id(1) // nb) == lax.axis_index("c")
        gg = jnp.where(my_half, mine_ref[...], theirs_ref[...])
        nm = ADAM_B1 * m_ref[...] + (1.0 - ADAM_B1) * gg
        nv = ADAM_B2 * v_ref[...] + (1.0 - ADAM_B2) * (gg * gg)
        g_ref[...] = gg
        d_ref[...] = -ADAM_LR * ((nm * c1) / (jnp.sqrt(nv * c2) + ADAM_EPS) + ADAM_WD * w_ref[...])
        nm_ref[...] = nm
        nv_ref[...] = nv

    full = pl.BlockSpec((None, tr, c), lambda l, i: (l, i, 0))
    half = pl.BlockSpec((None, tr, c), lambda l, i: (l, i % nb, 0))
    sds = jax.ShapeDtypeStruct((n_l, r, c), F32)
    return pl.pallas_call(
        body, name=name, grid=(n_l, r // tr), in_specs=[full, full, full, half, half], out_specs=[full] * 4,
        out_shape=(sds,) * 4, compiler_params=_cparams(("parallel", "parallel")),
    )(w, m, v, mine, theirs)


def _layer_weight(kind, i):
    if kind == "up":
        return "mlp_w_up", i
    if kind == "down":
        return "mlp_w_down", i
    return ("gdn_w_" if i % 2 == 0 else "hgrn_w_") + kind, i // 2


def kernel(x, gdn_w_in, gdn_conv, gdn_a_log, gdn_dt_bias, gdn_onorm, gdn_w_out, hgrn_w_in, hgrn_lb_logits, hgrn_gnorm, hgrn_w_out, norm_mix, norm_mlp, mlp_w_up, mlp_w_down, norm_final, loss_target, m_gdn_w_in, m_gdn_conv, m_gdn_a_log, m_gdn_dt_bias, m_gdn_onorm, m_gdn_w_out, m_hgrn_w_in, m_hgrn_lb_logits, m_hgrn_gnorm, m_hgrn_w_out, m_norm_mix, m_norm_mlp, m_mlp_w_up, m_mlp_w_down, m_norm_final, v_gdn_w_in, v_gdn_conv, v_gdn_a_log, v_gdn_dt_bias, v_gdn_onorm, v_gdn_w_out, v_hgrn_w_in, v_hgrn_lb_logits, v_hgrn_gnorm, v_hgrn_w_out, v_norm_mix, v_norm_mlp, v_mlp_w_up, v_mlp_w_down, v_norm_final):
    p = dict(gdn_w_in=gdn_w_in, gdn_conv=gdn_conv, gdn_a_log=gdn_a_log, gdn_dt_bias=gdn_dt_bias, gdn_onorm=gdn_onorm,
             gdn_w_out=gdn_w_out, hgrn_w_in=hgrn_w_in, hgrn_lb_logits=hgrn_lb_logits, hgrn_gnorm=hgrn_gnorm,
             hgrn_w_out=hgrn_w_out, norm_mix=norm_mix, norm_mlp=norm_mlp, mlp_w_up=mlp_w_up, mlp_w_down=mlp_w_down,
             norm_final=norm_final)
    m = dict(gdn_w_in=m_gdn_w_in, gdn_conv=m_gdn_conv, gdn_a_log=m_gdn_a_log, gdn_dt_bias=m_gdn_dt_bias,
             gdn_onorm=m_gdn_onorm, gdn_w_out=m_gdn_w_out, hgrn_w_in=m_hgrn_w_in, hgrn_lb_logits=m_hgrn_lb_logits,
             hgrn_gnorm=m_hgrn_gnorm, hgrn_w_out=m_hgrn_w_out, norm_mix=m_norm_mix, norm_mlp=m_norm_mlp,
             mlp_w_up=m_mlp_w_up, mlp_w_down=m_mlp_w_down, norm_final=m_norm_final)
    v = dict(gdn_w_in=v_gdn_w_in, gdn_conv=v_gdn_conv, gdn_a_log=v_gdn_a_log, gdn_dt_bias=v_gdn_dt_bias,
             gdn_onorm=v_gdn_onorm, gdn_w_out=v_gdn_w_out, hgrn_w_in=v_hgrn_w_in, hgrn_lb_logits=v_hgrn_lb_logits,
             hgrn_gnorm=v_hgrn_gnorm, hgrn_w_out=v_hgrn_w_out, norm_mix=v_norm_mix, norm_mlp=v_norm_mlp,
             mlp_w_up=v_mlp_w_up, mlp_w_down=v_mlp_w_down, norm_final=v_norm_final)
    xi, yi, ci = _mesh_pos()
    chip = 2 * xi + yi
    core = jnp.reshape(ci, (1,)).astype(jnp.int32)
    d = D_MODEL
    bsz, t_len, _ = x.shape
    n_tok = bsz * t_len

    def shard(kind, i):
        name, idx = _layer_weight(kind, i)
        return p[name][idx].astype(BF16)

    def w_in_of(i, slots):
        if i % 2 == 0:
            return _pad_lanes(jnp.transpose(slots, (1, 0, 2)).reshape(d, GDN_IN), GDN_IN_PAD)
        return slots

    (first,) = run_carries([gather_carry([shard("in", 0), p["gdn_conv"], p["hgrn_gnorm"]])], "gather_first")
    conv = jnp.transpose(first[1], (1, 2, 0, 3)).reshape(DEPTH // 2, CONV_K, 3 * d)
    gnorm = jnp.transpose(first[2], (1, 0, 2)).reshape(DEPTH // 2, d)
    lbl = p["hgrn_lb_logits"]
    h = x.reshape(n_tok, d)
    next_in, next_out = first[0], None
    saved = []
    for i in range(DEPTH):
        j = i // 2
        w_in = w_in_of(i, next_in)
        nmix = p["norm_mix"][i][None, :]
        (y, proj), got = norm_matmul(h, nmix, w_in, False, f"in_proj{i}",
                                     [gather_carry([shard("out", 0)])] if i == 0 else [])
        if i == 0:
            next_out = got[0][0]
        ride = [gather_carry([shard("up", i), shard("down", i)])]
        if i % 2 == 0:
            al = _pad_lanes(p["gdn_a_log"][j][None, :], 128)
            dtb = _pad_lanes(p["gdn_dt_bias"][j][None, :], 128)
            on = p["gdn_onorm"][j][None, :]
            (og, ssave), got = gdn_forward(proj, conv[j], al, dtb, on, bsz, f"gdn_fwd{i}", ride)
            mix = (proj, ssave, al, dtb, on)
        else:
            (o, ssave), got = hgrn_forward(proj, lbl, i, bsz, ride)
            gn = gnorm[j][None, :]
            og = hgrn_post_forward(o, proj, gn, f"hgrn_post{i}")
            mix = (proj, ssave, o, gn)
        w_up, w_down = got[0][0], got[0][1].reshape(MLP_HIDDEN, d)
        w_out = next_out.reshape(d, d)
        h1, _ = matmul(og, w_out, "nn", f"out_proj{i}", extra=h, epilogue="add")
        nmlp = p["norm_mlp"][i][None, :]
        (z, a, r), got = norm_matmul(h1, nmlp, w_up, True, f"mlp_up{i}",
                                     [gather_carry([shard("in", i + 1)])] if i + 1 < DEPTH else [])
        if i + 1 < DEPTH:
            next_in = got[0][0]
        h2, got = matmul(a, w_down, "nn", f"mlp_down{i}", extra=h1, epilogue="add",
                         carries=[gather_carry([shard("out", i + 1)])] if i + 1 < DEPTH else [])
        if i + 1 < DEPTH:
            next_out = got[0][0]
        saved.append((h, nmix, y, mix, og, w_in, w_out, h1, nmlp, z, a, r, w_up, w_down))
        h = h2
    loss_row, dh, d_nf = loss_head(h, p["norm_final"][None, :], loss_target.reshape(n_tok, d))

    G = {k: [None] * DEPTH for k in ("in", "out", "up", "down")}
    P = {k: [None] * DEPTH for k in ("in", "out", "up", "down")}
    slots = {k: [None] * DEPTH for k in ("in", "out", "up", "down")}
    d_nmix, d_nmlp = [None] * DEPTH, [None] * DEPTH
    d_conv, d_alog, d_dtb, d_onorm, d_gnorm = [None] * 2, [None] * 2, [None] * 2, [None] * 2, [None] * 2
    d_lbl = jnp.zeros((DEPTH, d), F32)
    for i in reversed(range(DEPTH)):
        j = i // 2
        h_in, nmix, y, mix, og, w_in, w_out, h1, nmlp, z, a, r, w_up, w_down = saved[i]
        ride = [sibling_half_carry([G["in"][i + 1], G["out"][i + 1]])] if i + 1 < DEPTH else []
        du, got = matmul(dh, w_down, "nt", f"d_mlp_act{i}", out_dtype=BF16, extra=r, epilogue="mul2", carries=ride)
        if i + 1 < DEPTH:
            P["in"][i + 1] = add_core_halves(G["in"][i + 1], got[0][0], core, f"add_cores_in{i + 1}")
            P["out"][i + 1] = add_core_halves(G["out"][i + 1], got[0][1], core, f"add_cores_out{i + 1}")
        G["down"][i] = matmul(a, dh, "tn", f"dw_down{i}")[0].reshape(N_CHIPS, -1, d)
        dz, _ = matmul(du, w_up, "nt", f"d_mlp_in{i}")
        G["up"][i], _ = matmul(z, du, "tn", f"dw_up{i}", shards=N_CHIPS)
        dh1, d_nmlp[i] = norm_backward(h1, nmlp, dz, dh, f"d_norm_mlp{i}")
        dog, _ = matmul(dh1, w_out, "nt", f"d_mix_out{i}")
        G["out"][i] = matmul(og, dh1, "tn", f"dw_out{i}")[0].reshape(N_CHIPS, -1, d)
        ride = [sibling_half_carry([G["up"][i], G["down"][i]])]
        if i + 1 < DEPTH:
            ride.append(chips_carry([P[k][i + 1] for k in ("in", "out", "up", "down")]))
        if i % 2 == 0:
            proj, ssave, al, dtb, on = mix
            (dproj, dcw, dal, ddt, don), got = gdn_backward(proj, conv[j], al, dtb, on, ssave, dog, bsz, f"gdn_bwd{i}", ride)
            d_conv[j] = jnp.transpose(dcw, (2, 1, 0, 3)).reshape(CONV_K, 3 * d)
            d_alog[j], d_dtb[j], d_onorm[j] = dal[0, :HEADS], ddt[0, :HEADS], don[0]
        else:
            proj, ssave, o, gn = mix
            do_raw, dgate, dgn = hgrn_post_backward(o, proj, gn, dog, f"d_hgrn_post{i}")
            (dproj, dlb), got = hgrn_backward(proj, lbl, ssave, do_raw, dgate, i, bsz, ride)
            d_gnorm[j] = dgn[0]
            d_lbl = d_lbl + jnp.transpose(dlb, (1, 0, 2)).reshape(DEPTH, d)
        P["up"][i] = add_core_halves(G["up"][i], got[0][0], core, f"add_cores_up{i}")
        P["down"][i] = add_core_halves(G["down"][i], got[0][1], core, f"add_cores_down{i}")
        if i + 1 < DEPTH:
            for k, s in zip(("in", "out", "up", "down"), got[1]):
                slots[k][i + 1] = s
        ride = [chips_carry([P["up"][0], P["down"][0]])] if i == 0 else []
        dy, got = matmul(dproj, w_in, "nt", f"d_in_proj{i}", carries=ride)
        if i == 0:
            slots["up"][0], slots["down"][0] = got[0]
        if i % 2 == 0:
            dw_in = matmul(y, dproj, "tn", f"dw_in{i}")[0][:, :GDN_IN]
            G["in"][i] = jnp.transpose(dw_in.reshape(d, N_CHIPS, GDN_IN // N_CHIPS), (1, 0, 2))
        else:
            G["in"][i], _ = matmul(y, dproj, "tn", f"dw_in{i}", shards=N_CHIPS)
        dh, d_nmix[i] = norm_backward(h_in, nmix, dy, dh1, f"d_norm_mix{i}")
    grad_x = dh.reshape(x.shape)

    (got,) = run_carries([sibling_half_carry([G["in"][0], G["out"][0]])], "reduce_cores_last")
    P["in"][0] = add_core_halves(G["in"][0], got[0], core, "add_cores_in0")
    P["out"][0] = add_core_halves(G["out"][0], got[1], core, "add_cores_out0")
    (got,) = run_carries([chips_carry([P["in"][0], P["out"][0]])], "reduce_chips_last")
    slots["in"][0], slots["out"][0] = got
    by_weight = {}
    for kind in ("in", "out", "up", "down"):
        for i in range(DEPTH):
            by_weight.setdefault(_layer_weight(kind, i)[0], []).append(slots[kind][i])
    mine = {name: add_chip_slots(by_weight[name], f"add_chips_{name}") for name in BIG}
    small = {
        "gdn_conv": jnp.stack(d_conv), "gdn_a_log": jnp.stack(d_alog), "gdn_dt_bias": jnp.stack(d_dtb),
        "gdn_onorm": jnp.stack(d_onorm), "hgrn_lb_logits": d_lbl, "hgrn_gnorm": jnp.stack(d_gnorm),
        "norm_mix": jnp.concatenate(d_nmix, axis=0), "norm_mlp": jnp.concatenate(d_nmlp, axis=0), "norm_final": d_nf[0],
    }
    theirs, (blocks,) = run_carries([sibling_copy_carry([mine[name] for name in BIG]),
                                     gather_all_carry(_pack_small(small, loss_row))], "share_cores")
    theirs = dict(zip(BIG, theirs))

    total = add_slots(blocks, "add_small")
    loss = total[_SMALL_LAYOUT["loss"][0], 0]
    grads, delta, new_m, new_v = {}, {}, {}, {}
    for name in WEIGHTS:
        if name in BIG:
            grads[name], delta[name], new_m[name], new_v[name] = adamw_halves(
                p[name], m[name], v[name], mine[name], theirs[name], f"adamw_{name}")
            continue
        if name == "gdn_conv":
            full = _unpack_small(total, name, (2, CONV_K, 3 * d))
            grads[name] = lax.dynamic_slice_in_dim(full, chip * (3 * d // N_CHIPS), 3 * d // N_CHIPS, axis=2)
        elif name == "hgrn_gnorm":
            full = _unpack_small(total, name, (2, d))
            grads[name] = lax.dynamic_slice_in_dim(full, chip * (d // N_CHIPS), d // N_CHIPS, axis=1)
        else:
            grads[name] = _unpack_small(total, name, p[name].shape)
        delta[name], new_m[name], new_v[name] = adamw(p[name], grads[name], m[name], v[name], f"adamw_{name}")
    return (loss, grad_x, *[grads[n] for n in WEIGHTS], *[delta[n] for n in WEIGHTS],
            *[new_m[n] for n in WEIGHTS], *[new_v[n] for n in WEIGHTS])
```

```python
import functools
import math

import jax
import jax.numpy as jnp
from jax import lax
from jax.experimental import pallas as pl
from jax.experimental.pallas import tpu as pltpu

F32 = jnp.float32
BF16 = jnp.bfloat16
HI = lax.Precision.HIGHEST

D_MODEL = 1024
HEADS = 8
HEAD_DIM = 128
CHUNK = 64
SUB = 16
CONV_K = 4
HALO = 8
DEPTH = 4
EPS = 1e-6
MLP_HIDDEN = 4 * D_MODEL
GDN_MAIN = 4 * D_MODEL
GDN_IN = GDN_MAIN + 2 * HEADS
GDN_IN_PAD = GDN_MAIN + 128
NEG = -1e30

ADAM_LR = 0.001
ADAM_B1 = 0.9
ADAM_B2 = 0.999
ADAM_EPS = 1e-08
ADAM_WD = 0.01
ADAM_STEP = 10

VMEM_LIMIT = 48 * 1024 * 1024

MESH = pl.DeviceIdType.MESH


def _cparams(sem=None, **kw):
    if sem is not None:
        kw["dimension_semantics"] = sem
    return pltpu.CompilerParams(vmem_limit_bytes=VMEM_LIMIT, **kw)


def _iota(shape, dim):
    return lax.broadcasted_iota(jnp.int32, shape, dim)


_DIMS = {"nn": (((1,), (0,)), ((), ())), "nt": (((1,), (1,)), ((), ())), "tn": (((0,), (0,)), ((), ()))}


def _dot(a, b, mode):
    return lax.dot_general(a.astype(BF16), b.astype(BF16), _DIMS[mode], preferred_element_type=F32)


@functools.partial(jax.custom_vjp, nondiff_argnums=(2,))
def _mmx(a, b, mode):
    return _dot(a, b, mode)


def _mmx_fwd(a, b, mode):
    return _dot(a, b, mode), (a, b)


def _mmx_bwd(mode, res, g):
    a, b = res
    if mode == "nn":
        return _dot(g, b, "nt"), _dot(a, g, "tn")
    if mode == "nt":
        return _dot(g, b, "nn"), _dot(g, a, "tn")
    return _dot(b, g, "nt"), _dot(a, g, "nn")


_mmx.defvjp(_mmx_fwd, _mmx_bwd)


def _mm(a, b):
    return _mmx(a, b, "nn")


def _mm_nt(a, b):
    return _mmx(a, b, "nt")


def _mm_tn(a, b):
    return _mmx(a, b, "tn")


@functools.partial(jax.custom_vjp, nondiff_argnums=(1,))
def _roll_rows(x, d):
    return pltpu.roll(x, d, 0)


def _roll_rows_fwd(x, d):
    return pltpu.roll(x, d, 0), None


def _roll_rows_bwd(d, _, g):
    return (pltpu.roll(g, g.shape[0] - d, 0),)


_roll_rows.defvjp(_roll_rows_fwd, _roll_rows_bwd)


def _sigmoid(x):
    return 1.0 / (1.0 + jnp.exp(-x))


def _silu(x):
    return x * _sigmoid(x)


def _softplus(x):
    return jnp.maximum(x, 0.0) + jnp.log(1.0 + jnp.exp(-jnp.abs(x)))


def _log_sigmoid(x):
    return jnp.minimum(x, 0.0) - jnp.log(1.0 + jnp.exp(-jnp.abs(x)))


def _logaddexp(a, b):
    return jnp.maximum(a, b) + jnp.log(1.0 + jnp.exp(-jnp.abs(a - b)))


def _row_to_col(row):
    n = row.shape[1]
    eye = _iota((n, n), 0) == _iota((n, n), 1)
    return jnp.sum(jnp.where(eye, jnp.broadcast_to(row, (n, n)), 0.0), axis=1, keepdims=True)


def _col_to_row(col):
    n = col.shape[0]
    eye = _iota((n, n), 0) == _iota((n, n), 1)
    return jnp.sum(jnp.where(eye, jnp.broadcast_to(col, (n, n)), 0.0), axis=0, keepdims=True)


def _pick_row(x, r):
    return jnp.sum(jnp.where(_iota(x.shape, 0) == r, x, 0.0), axis=0, keepdims=True)


def _pick_lane(x, l):
    return jnp.sum(jnp.where(_iota(x.shape, 1) == l, x, 0.0), axis=1, keepdims=True)


def _each(f, *lists):
    return [f(*t) for t in zip(*lists)]


def _unit_lower_inverse(Ls):
    n = Ls[0].shape[0]
    r, c = _iota((n, n), 0), _iota((n, n), 1)
    eye = jnp.where(r == c, 1.0, 0.0).astype(F32)
    Ld = _each(lambda L: jnp.where((r // SUB) == (c // SUB), L, 0.0), Ls)
    Lo = _each(lambda L, d: L - d, Ls, Ld)
    P = _each(lambda d: eye - d, Ld)
    Lp = Ld
    for _ in range(int(math.log2(SUB)) - 1):
        Lp = _each(lambda x: _mm(x, x), Lp)
        P = _each(lambda p, x: p + _mm(p, x), P, Lp)
    N = _each(_mm, P, Lo)
    N2 = _each(lambda x: _mm(x, x), N)
    X = _each(lambda x, x2: (eye - x) + _mm(eye - x, x2), N, N2)
    return _each(_mm, X, P)


def _shift_rows(x, halo, d):
    if d == 0:
        return x
    xr = _roll_rows(x, d)
    hr = _roll_rows(halo, d)
    hfull = jnp.concatenate([hr, jnp.zeros((x.shape[0] - HALO, x.shape[1]), F32)], axis=0)
    return jnp.where(_iota(x.shape, 0) >= d, xr, hfull)


def _causal_conv_chunk(x, halo, w):
    y = None
    for kk in range(CONV_K):
        t = _shift_rows(x, halo, CONV_K - 1 - kk) * _pick_row(w, kk)
        y = t if y is None else y + t
    return y


def _gdn_chunk(xq, xk, xv, hq, hk, hv, gate, ab, S, cwq, cwk, cwv, alog, dtb, onorm, *, heads):
    C = xq[0].shape[0]
    q = _each(lambda x, h, w: _silu(_causal_conv_chunk(x, h, w)), xq, hq, cwq)
    k = _each(lambda x, h, w: _silu(_causal_conv_chunk(x, h, w)), xk, hk, cwk)
    v = _each(lambda x, h, w: _silu(_causal_conv_chunk(x, h, w)), xv, hv, cwv)
    q = _each(lambda t: t * lax.rsqrt(jnp.sum(t * t, axis=1, keepdims=True) + EPS) * (HEAD_DIM ** -0.5), q)
    k = _each(lambda t: t * lax.rsqrt(jnp.sum(t * t, axis=1, keepdims=True) + EPS), k)
    beta = [_sigmoid(_pick_lane(ab, HEADS + h)) for h in heads]
    g = [-jnp.exp(_pick_lane(alog, h)) * _softplus(_pick_lane(ab, h) + _pick_lane(dtb, h)) for h in heads]
    r, c = _iota((C, C), 0), _iota((C, C), 1)
    gc = _each(lambda t: jnp.sum(jnp.where(c <= r, jnp.broadcast_to(_col_to_row(t), (C, C)), 0.0), axis=1,
                                 keepdims=True), g)
    gc_row = _each(lambda t: jnp.sum(jnp.where(r <= c, jnp.broadcast_to(t, (C, C)), 0.0), axis=0, keepdims=True), g)
    decay = _each(lambda a, b: jnp.exp(jnp.where(r >= c, a - b, NEG)), gc, gc_row)
    kb = _each(lambda a, b: a * b, k, beta)
    L = _each(lambda a, b, d: jnp.where(r > c, _mm_nt(a, b) * d, 0.0), kb, k, decay)
    A = _each(lambda a, b, d: jnp.where(r >= c, _mm_nt(a, b) * d, 0.0), q, k, decay)
    T = _unit_lower_inverse(L)
    egc = _each(jnp.exp, gc)
    u = _each(lambda t, a, b: _mm(t, a * b), T, v, beta)
    w = _each(lambda t, a, e: _mm(t, a * e), T, kb, egc)
    gl = _each(lambda t: _pick_row(t, C - 1), gc)
    v_new = _each(lambda a, b, s: a - _mm(b, s), u, w, S)
    o = _each(lambda a, e, s, m, vn: _mm(a * e, s) + _mm(m, vn), q, egc, S, A, v_new)
    S_next = _each(lambda s, l, a, t, vn: s * jnp.exp(l) + _mm_tn(a * jnp.exp(l - t), vn), S, gl, k, gc, v_new)
    o = _each(lambda t, gt: t * lax.rsqrt(jnp.mean(t * t, axis=1, keepdims=True) + EPS) * onorm * _silu(gt), o, gate)
    return o, S_next


def _hgrn_lower_bound(lbl, layer):
    e = jnp.exp(lbl - jnp.max(lbl, axis=0, keepdims=True))
    sm = e / jnp.sum(e, axis=0, keepdims=True)
    r = _iota(lbl.shape, 0)
    return jnp.sum(jnp.where((r >= 1) & (r <= layer), sm, 0.0), axis=0, keepdims=True)


def _hgrn_chunk(qp, fp, v, S, lbl, *, layer):
    C = qp.shape[0]
    lb = _hgrn_lower_bound(lbl, layer)
    lf = _logaddexp(jnp.log(lb), jnp.log(1.0 - lb) + _log_sigmoid(fp))
    k = (1.0 - lb) * _sigmoid(-fp)
    q = _silu(qp) * (HEAD_DIM ** -0.5)
    r = _iota((C, C), 0)
    row = _iota(qp.shape, 0)
    gc = lf
    step = 1
    while step < C:
        gc = gc + jnp.where(row >= step, _roll_rows(gc, step), 0.0)
        step *= 2
    gml = gc - lf
    gb_rows = jnp.zeros_like(gc)
    gbs = []
    for s in range(C // SUB):
        gb_s = _pick_row(gml, s * SUB)
        gbs.append(gb_s)
        gb_rows = jnp.where((row // SUB) == s, gb_s, gb_rows)
    q_off = q * jnp.exp(gc - gb_rows)
    a_off = jnp.zeros((C, C), F32)
    for s in range(1, C // SUB):
        k_off = k * jnp.exp(jnp.where(row < s * SUB, gbs[s] - gc, NEG))
        a_off = a_off + jnp.where((r // SUB) == s, _mm_nt(q_off, k_off), 0.0)
    o = _mm(a_off, v) + _mm(q * jnp.exp(gc), S)
    for d in range(SUB):
        k_d = _roll_rows(k, d) if d else k
        gc_d = _roll_rows(gc, d) if d else gc
        v_d = _roll_rows(v, d) if d else v
        e = jnp.exp(jnp.where((row % SUB) >= d, gc - gc_d, NEG))
        o = o + jnp.sum(q * k_d * e, axis=1, keepdims=True) * v_d
    gl = _pick_row(gc, C - 1)
    S_next = S * _row_to_col(jnp.exp(gl)) + _mm_tn(k * jnp.exp(gl - gc), v)
    return o, S_next


N_CHIPS = 4
N_DEV = 8
_ANY = pl.BlockSpec(memory_space=pl.ANY)


def _mesh_pos():
    return lax.axis_index("x"), lax.axis_index("y"), lax.axis_index("c")


def _other_chips(x, y):
    ps = [(1 - x, y), (x, 1 - y), (1 - x, 1 - y)]
    return [(p, 2 * p[0] + p[1]) for p in ps]


def _remote(src, dst, send_sem, recv_sem, dev):
    return pltpu.make_async_remote_copy(src_ref=src, dst_ref=dst, send_sem=send_sem, recv_sem=recv_sem,
                                        device_id=dev, device_id_type=MESH)


class Carry:
    def __init__(self, ins, out_shapes, sems, start, finish):
        self.ins, self.out_shapes, self.sems, self.start, self.finish = list(ins), list(out_shapes), list(sems), start, finish


def _pcall(body, *, name, grid, in_specs, out_specs, out_shape, scratch_shapes=(), dims, args, carries=()):
    in_specs, out_specs, out_shape = list(in_specs), list(out_specs), list(out_shape)
    scratch_shapes, args = list(scratch_shapes), list(args)
    n_in, n_out, n_scr = len(in_specs), len(out_shape), len(scratch_shapes)
    carries = [c for c in carries if c is not None]
    if not carries:
        res = pl.pallas_call(body, name=name, grid=grid, in_specs=in_specs, out_specs=out_specs, out_shape=out_shape,
                             scratch_shapes=scratch_shapes, compiler_params=_cparams(dims))(*args)
        return list(res), []
    ci = [len(c.ins) for c in carries]
    co = [len(c.out_shapes) for c in carries]
    cs = [len(c.sems) for c in carries]

    def split(seq, sizes):
        out, k = [], 0
        for s in sizes:
            out.append(seq[k:k + s])
            k += s
        return out

    def carried(*refs):
        ins, cins, outs, couts, scr, sems = split(refs, [n_in, sum(ci), n_out, sum(co), n_scr, sum(cs)])
        cins, couts, sems = split(cins, ci), split(couts, co), split(sems, cs)
        ids = [pl.program_id(a) for a in range(len(grid))]
        first, last = ids[0] == 0, ids[0] == grid[0] - 1
        for a in range(1, len(grid)):
            first, last = first & (ids[a] == 0), last & (ids[a] == grid[a] - 1)

        @pl.when(first)
        def _():
            for c, i, o, s in zip(carries, cins, couts, sems):
                c.start(i, o, s)

        body(*ins, *outs, *scr)

        @pl.when(last)
        def _():
            for c, i, o, s in zip(carries, cins, couts, sems):
                c.finish(i, o, s)

    res = pl.pallas_call(
        carried, name=name, grid=grid,
        in_specs=in_specs + [_ANY] * sum(ci), out_specs=out_specs + [_ANY] * sum(co),
        out_shape=out_shape + [s for c in carries for s in c.out_shapes],
        scratch_shapes=scratch_shapes + [s for c in carries for s in c.sems],
        compiler_params=_cparams(("arbitrary",) * len(grid)),
    )(*args, *[a for c in carries for a in c.ins])
    return list(res[:n_out]), split(list(res[n_out:]), co)


def run_carries(carries, name):
    ci = [len(c.ins) for c in carries]
    co = [len(c.out_shapes) for c in carries]
    cs = [len(c.sems) for c in carries]

    def split(seq, sizes):
        out, k = [], 0
        for s in sizes:
            out.append(seq[k:k + s])
            k += s
        return out

    def body(*refs):
        cins, couts, sems = split(refs, [sum(ci), sum(co), sum(cs)])
        cins, couts, sems = split(cins, ci), split(couts, co), split(sems, cs)
        for c, i, o, s in zip(carries, cins, couts, sems):
            c.start(i, o, s)
        for c, i, o, s in zip(carries, cins, couts, sems):
            c.finish(i, o, s)

    res = pl.pallas_call(
        body, name=name, in_specs=[_ANY] * sum(ci), out_specs=[_ANY] * sum(co),
        out_shape=[s for c in carries for s in c.out_shapes], scratch_shapes=[s for c in carries for s in c.sems],
    )(*[a for c in carries for a in c.ins])
    return split(list(res), co)


def gather_carry(arrs):
    n = len(arrs)

    def copies(ins, outs, sems):
        send_sems, recv_sems, local_sems = sems
        x, y, c = _mesh_pos()
        me = 2 * x + y
        peers = _other_chips(x, y)
        locs = [pltpu.make_async_copy(ins[a], outs[a].at[me], local_sems.at[a]) for a in range(n)]
        sends = [_remote(ins[a], outs[a].at[me], send_sems.at[a, j], recv_sems.at[a, j], (px, py, c))
                 for a in range(n) for j, ((px, py), _) in enumerate(peers)]
        recvs = [_remote(ins[a], outs[a].at[t], send_sems.at[a, j], recv_sems.at[a, j], (px, py, c))
                 for a in range(n) for j, ((px, py), t) in enumerate(peers)]
        return locs, sends, recvs

    def start(ins, outs, sems):
        locs, sends, _ = copies(ins, outs, sems)
        for cp in locs + sends:
            cp.start()

    def finish(ins, outs, sems):
        locs, sends, recvs = copies(ins, outs, sems)
        for r in recvs:
            r.wait_recv()
        for r in sends:
            r.wait_send()
        for cp in locs:
            cp.wait()

    return Carry(arrs, [jax.ShapeDtypeStruct((N_CHIPS,) + a.shape, a.dtype) for a in arrs],
                 [pltpu.SemaphoreType.DMA((n, 3)), pltpu.SemaphoreType.DMA((n, 3)), pltpu.SemaphoreType.DMA((n,))],
                 start, finish)


def sibling_half_carry(gs):
    n = len(gs)

    def copies(ins, outs, sems):
        send_sems, recv_sems = sems
        x, y, c = _mesh_pos()
        out = []
        for a in range(n):
            half = gs[a].shape[1] // 2
            out.append(_remote(ins[a].at[:, pl.ds((1 - c) * half, half), :], outs[a], send_sems.at[a], recv_sems.at[a],
                               (x, y, 1 - c)))
        return out

    def start(ins, outs, sems):
        for r in copies(ins, outs, sems):
            r.start()

    def finish(ins, outs, sems):
        cps = copies(ins, outs, sems)
        for r in cps:
            r.wait_recv()
        for r in cps:
            r.wait_send()

    return Carry(gs, [jax.ShapeDtypeStruct((g.shape[0], g.shape[1] // 2, g.shape[2]), g.dtype) for g in gs],
                 [pltpu.SemaphoreType.DMA((n,)), pltpu.SemaphoreType.DMA((n,))], start, finish)


def chips_carry(ps):
    n = len(ps)

    def copies(ins, outs, sems):
        send_sems, recv_sems, local_sems = sems
        x, y, c = _mesh_pos()
        me = 2 * x + y
        peers = _other_chips(x, y)
        locs = [pltpu.make_async_copy(ins[a].at[me], outs[a].at[me], local_sems.at[a]) for a in range(n)]
        sends = [_remote(ins[a].at[t], outs[a].at[me], send_sems.at[a, j], recv_sems.at[a, j], (px, py, c))
                 for a in range(n) for j, ((px, py), t) in enumerate(peers)]
        recvs = [_remote(ins[a].at[t], outs[a].at[t], send_sems.at[a, j], recv_sems.at[a, j], (px, py, c))
                 for a in range(n) for j, ((px, py), t) in enumerate(peers)]
        return locs, sends, recvs

    def start(ins, outs, sems):
        locs, sends, _ = copies(ins, outs, sems)
        for cp in locs + sends:
            cp.start()

    def finish(ins, outs, sems):
        locs, sends, recvs = copies(ins, outs, sems)
        for r in recvs:
            r.wait_recv()
        for r in sends:
            r.wait_send()
        for cp in locs:
            cp.wait()

    return Carry(ps, [jax.ShapeDtypeStruct(p.shape, p.dtype) for p in ps],
                 [pltpu.SemaphoreType.DMA((n, 3)), pltpu.SemaphoreType.DMA((n, 3)), pltpu.SemaphoreType.DMA((n,))],
                 start, finish)


def sibling_copy_carry(arrs):
    n = len(arrs)

    def copies(ins, outs, sems):
        send_sems, recv_sems = sems
        x, y, c = _mesh_pos()
        return [_remote(ins[a], outs[a], send_sems.at[a], recv_sems.at[a], (x, y, 1 - c)) for a in range(n)]

    def start(ins, outs, sems):
        for r in copies(ins, outs, sems):
            r.start()

    def finish(ins, outs, sems):
        cps = copies(ins, outs, sems)
        for r in cps:
            r.wait_recv()
        for r in cps:
            r.wait_send()

    return Carry(arrs, [jax.ShapeDtypeStruct(a.shape, a.dtype) for a in arrs],
                 [pltpu.SemaphoreType.DMA((n,)), pltpu.SemaphoreType.DMA((n,))], start, finish)


def gather_all_carry(small):
    flips = [(fx, fy, fc) for fx in (0, 1) for fy in (0, 1) for fc in (0, 1)][1:]

    def copies(ins, outs, sems):
        send_sems, recv_sems, local_sem = sems
        in_ref, out_ref = ins[0], outs[0]
        x, y, c = _mesh_pos()
        me = 4 * x + 2 * y + c
        peers = [((1 - x) if fx else x, (1 - y) if fy else y, (1 - c) if fc else c) for fx, fy, fc in flips]
        loc = pltpu.make_async_copy(in_ref, out_ref.at[me], local_sem.at[0])
        sends = [_remote(in_ref, out_ref.at[me], send_sems.at[k], recv_sems.at[k], p) for k, p in enumerate(peers)]
        recvs = [_remote(in_ref, out_ref.at[4 * p[0] + 2 * p[1] + p[2]], send_sems.at[k], recv_sems.at[k], p)
                 for k, p in enumerate(peers)]
        return loc, sends, recvs

    def start(ins, outs, sems):
        loc, sends, _ = copies(ins, outs, sems)
        loc.start()
        for r in sends:
            r.start()

    def finish(ins, outs, sems):
        loc, sends, recvs = copies(ins, outs, sems)
        for r in recvs:
            r.wait_recv()
        for r in sends:
            r.wait_send()
        loc.wait()

    return Carry([small], [jax.ShapeDtypeStruct((N_DEV,) + small.shape, small.dtype)],
                 [pltpu.SemaphoreType.DMA((N_DEV - 1,)), pltpu.SemaphoreType.DMA((N_DEV - 1,)),
                  pltpu.SemaphoreType.DMA((1,))], start, finish)


HB = 8
HBW = HB * HEAD_DIM


def _chunk_specs(nc, col0, rev=False):
    return pl.BlockSpec((CHUNK, HBW), lambda b, n, g: (b * nc + ((nc - 1 - n) if rev else n), col0 // HB + g))


def _chunk_specs_rev(nc, col0):
    return _chunk_specs(nc, col0, True)


def _halo_spec(nc, col0, rev):
    per = CHUNK // HALO

    def imap(b, n, g):
        nn = (nc - 1 - n) if rev else n
        return (jnp.maximum((b * nc + nn) * per - 1, 0), col0 // HB + g)
    return pl.BlockSpec((HALO, HBW), imap)


def _const_spec(shape):
    nd = len(shape)
    return pl.BlockSpec(shape, lambda b, n, g: (0,) * nd)


def _state_spec(nc, rev=False):
    return pl.BlockSpec((None, None, HB, HEAD_DIM, HEAD_DIM),
                        lambda b, n, g: (b, (nc - 1 - n) if rev else n, g, 0, 0))


def _lanes(hh):
    return slice(hh * HEAD_DIM, (hh + 1) * HEAD_DIM)


def _head(g, hh):
    return hh if HB == HEADS else g * HB + hh


def gdn_forward(proj, conv_w, alog, dtb, onorm, bsz, name, carries=()):
    n_tok = proj.shape[0]
    nc = n_tok // bsz // CHUNK

    def body(xq, xk, xv, hq, hk, hv, gate, ab, cwq, cwk, cwv, al, dt, on, o_ref, ssave_ref, s_ref):
        n, g = pl.program_id(1), pl.program_id(2)
        keep = jnp.where(n > 0, 1.0, 0.0).astype(F32)

        @pl.when(n == 0)
        def _():
            for hh in range(HB):
                s_ref[_head(g, hh)] = jnp.zeros((HEAD_DIM, HEAD_DIM), F32)

        heads = [_head(g, hh) for hh in range(HB)]
        per_head = lambda ref, scale=None: [ref[:, _lanes(hh)] if scale is None else ref[:, _lanes(hh)] * scale
                                            for hh in range(HB)]
        S = [s_ref[h] for h in heads]
        for hh in range(HB):
            ssave_ref[hh] = S[hh]
        outs, s_next = _gdn_chunk(per_head(xq), per_head(xk), per_head(xv), per_head(hq, keep), per_head(hk, keep),
                                  per_head(hv, keep), per_head(gate), ab[...], S, per_head(cwq), per_head(cwk),
                                  per_head(cwv), al[...], dt[...], on[...], heads=heads)
        for hh in range(HB):
            o_ref[:, _lanes(hh)] = outs[hh].astype(o_ref.dtype)
            s_ref[heads[hh]] = s_next[hh]

    hb = HEADS
    cw_spec = lambda col0: pl.BlockSpec((CONV_K, HBW), lambda b, n, g: (0, col0 // HB + g))
    in_specs = [_chunk_specs(nc, 0), _chunk_specs(nc, hb), _chunk_specs(nc, 2 * hb),
                _halo_spec(nc, 0, False), _halo_spec(nc, hb, False), _halo_spec(nc, 2 * hb, False),
                _chunk_specs(nc, 3 * hb),
                pl.BlockSpec((CHUNK, 128), lambda b, n, g: (b * nc + n, 4 * hb)),
                cw_spec(0), cw_spec(hb), cw_spec(2 * hb),
                _const_spec((1, 128)), _const_spec((1, 128)), _const_spec((1, 128))]
    out_specs = [_chunk_specs(nc, 0), _state_spec(nc)]
    return _pcall(
        body, name=name, grid=(bsz, nc, HEADS // HB), in_specs=in_specs, out_specs=out_specs,
        out_shape=(jax.ShapeDtypeStruct((n_tok, D_MODEL), BF16),
                   jax.ShapeDtypeStruct((bsz, nc, HEADS, HEAD_DIM, HEAD_DIM), F32)),
        scratch_shapes=[pltpu.VMEM((HEADS, HEAD_DIM, HEAD_DIM), F32)],
        dims=("arbitrary", "arbitrary", "arbitrary"),
        args=(proj, proj, proj, proj, proj, proj, proj, proj, conv_w, conv_w, conv_w, alog, dtb, onorm),
        carries=carries)


def gdn_backward(proj, conv_w, alog, dtb, onorm, s_saved, d_out, bsz, name, carries=()):
    n_tok = proj.shape[0]
    nc = n_tok // bsz // CHUNK
    assert HB == HEADS

    def body(xq, xk, xv, hq, hk, hv, gate, ab, cwq, cwk, cwv, al, dt, on, ssave, do,
             dp_ref, dcw_ref, dal_ref, ddt_ref, don_ref, ds_ref, dhalo_ref):
        b, n, g = pl.program_id(0), pl.program_id(1), pl.program_id(2)
        nr = nc - 1 - n

        @pl.when((b == 0) & (n == 0) & (g == 0))
        def _():
            dcw_ref[...] = jnp.zeros_like(dcw_ref)
            dal_ref[...] = jnp.zeros_like(dal_ref)
            ddt_ref[...] = jnp.zeros_like(ddt_ref)
            don_ref[...] = jnp.zeros_like(don_ref)

        @pl.when(n == 0)
        def _():
            for hh in range(HB):
                ds_ref[_head(g, hh)] = jnp.zeros((HEAD_DIM, HEAD_DIM), F32)
                dhalo_ref[_head(g, hh)] = jnp.zeros((3, HALO, HEAD_DIM), F32)

        keep = jnp.where(nr > 0, 1.0, 0.0).astype(F32)
        pad = jnp.zeros((CHUNK - HALO, HEAD_DIM), F32)
        heads = [_head(g, hh) for hh in range(HB)]
        per_head = lambda ref, scale=None: [ref[:, _lanes(hh)] if scale is None else ref[:, _lanes(hh)] * scale
                                            for hh in range(HB)]
        args = (per_head(xq), per_head(xk), per_head(xv), per_head(hq, keep), per_head(hk, keep), per_head(hv, keep),
                per_head(gate), ab[...], [ssave[hh] for hh in range(HB)], per_head(cwq), per_head(cwk), per_head(cwv),
                al[...], dt[...], on[...])
        _, vjp = jax.vjp(functools.partial(_gdn_chunk, heads=heads), *args)
        (gxq, gxk, gxv, ghq, ghk, ghv, ggate, gab_sum, gS, gcq, gck, gcv, gal_sum, gdt_sum, gon_sum) = vjp(
            ([do[:, _lanes(hh)] for hh in range(HB)], [ds_ref[h] for h in heads]))
        d = D_MODEL
        for hh in range(HB):
            head, c0 = heads[hh], hh * HEAD_DIM
            for part, gx in enumerate((gxq, gxk, gxv)):
                full = gx[hh] + jnp.concatenate([pad, dhalo_ref[head, part]], axis=0)
                dp_ref[:, part * d + c0:part * d + c0 + HEAD_DIM] = full.astype(dp_ref.dtype)
            dp_ref[:, 3 * d + c0:3 * d + c0 + HEAD_DIM] = ggate[hh].astype(dp_ref.dtype)
            dcw_ref[head, 0] += gcq[hh]
            dcw_ref[head, 1] += gck[hh]
            dcw_ref[head, 2] += gcv[hh]
            ds_ref[head] = gS[hh]
            dhalo_ref[head, 0] = ghq[hh] * keep
            dhalo_ref[head, 1] = ghk[hh] * keep
            dhalo_ref[head, 2] = ghv[hh] * keep

        dp_ref[:, GDN_MAIN:GDN_IN_PAD] = gab_sum.astype(dp_ref.dtype)
        dal_ref[...] += gal_sum
        ddt_ref[...] += gdt_sum
        don_ref[...] += gon_sum

    hb = HEADS
    cw_spec = lambda col0: pl.BlockSpec((CONV_K, HBW), lambda b, n, g: (0, col0 // HB + g))
    in_specs = [_chunk_specs_rev(nc, 0), _chunk_specs_rev(nc, hb), _chunk_specs_rev(nc, 2 * hb),
                _halo_spec(nc, 0, True), _halo_spec(nc, hb, True), _halo_spec(nc, 2 * hb, True),
                _chunk_specs_rev(nc, 3 * hb),
                pl.BlockSpec((CHUNK, 128), lambda b, n, g: (b * nc + (nc - 1 - n), 4 * hb)),
                cw_spec(0), cw_spec(hb), cw_spec(2 * hb),
                _const_spec((1, 128)), _const_spec((1, 128)), _const_spec((1, 128)),
                _state_spec(nc, True),
                _chunk_specs_rev(nc, 0)]
    out_specs = [
        pl.BlockSpec((CHUNK, GDN_IN_PAD), lambda b, n, g: (b * nc + (nc - 1 - n), 0)),
        _const_spec((HEADS, 3, CONV_K, HEAD_DIM)), _const_spec((1, 128)), _const_spec((1, 128)), _const_spec((1, 128))]
    row = jax.ShapeDtypeStruct((1, 128), F32)
    return _pcall(
        body, name=name, grid=(bsz, nc, HEADS // HB), in_specs=in_specs, out_specs=out_specs,
        out_shape=(jax.ShapeDtypeStruct((n_tok, GDN_IN_PAD), BF16),
                   jax.ShapeDtypeStruct((HEADS, 3, CONV_K, HEAD_DIM), F32), row, row, row),
        scratch_shapes=[pltpu.VMEM((HEADS, HEAD_DIM, HEAD_DIM), F32), pltpu.VMEM((HEADS, 3, HALO, HEAD_DIM), F32)],
        dims=("arbitrary", "arbitrary", "arbitrary"),
        args=(proj, proj, proj, proj, proj, proj, proj, proj, conv_w, conv_w, conv_w, alog, dtb, onorm, s_saved, d_out),
        carries=carries)


def hgrn_forward(proj, lbl, layer, bsz, carries=()):
    n_tok = proj.shape[0]
    nc = n_tok // bsz // CHUNK

    def body(qp, fp, vi, lb, o_ref, ssave_ref, s_ref):
        n, g = pl.program_id(1), pl.program_id(2)

        @pl.when(n == 0)
        def _():
            for hh in range(HB):
                s_ref[_head(g, hh)] = jnp.zeros((HEAD_DIM, HEAD_DIM), F32)

        for hh in range(HB):
            head = _head(g, hh)
            ln = _lanes(hh)
            S = s_ref[head]
            ssave_ref[hh] = S
            o, s_next = _hgrn_chunk(qp[:, ln], fp[:, ln], vi[:, ln], S, lb[:, ln], layer=layer)
            o_ref[:, ln] = o
            s_ref[head] = s_next

    hb = HEADS
    in_specs = [_chunk_specs(nc, 0), _chunk_specs(nc, hb), _chunk_specs(nc, 2 * hb),
                pl.BlockSpec((DEPTH, HBW), lambda b, n, g: (0, g))]
    out_specs = [_chunk_specs(nc, 0), _state_spec(nc)]
    return _pcall(
        body, name=f"hgrn_fwd{layer}", grid=(bsz, nc, HEADS // HB), in_specs=in_specs, out_specs=out_specs,
        out_shape=(jax.ShapeDtypeStruct((n_tok, D_MODEL), F32),
                   jax.ShapeDtypeStruct((bsz, nc, HEADS, HEAD_DIM, HEAD_DIM), F32)),
        scratch_shapes=[pltpu.VMEM((HEADS, HEAD_DIM, HEAD_DIM), F32)],
        dims=("arbitrary", "arbitrary", "arbitrary"), args=(proj, proj, proj, lbl), carries=carries)


def hgrn_backward(proj, lbl, s_saved, d_o, d_gate, layer, bsz, carries=()):
    n_tok = proj.shape[0]
    nc = n_tok // bsz // CHUNK
    assert HB == HEADS

    def body(qp, fp, vi, lb, ssave, do, dgt, dp_ref, dlb_ref, ds_ref):
        b, n, g = pl.program_id(0), pl.program_id(1), pl.program_id(2)

        @pl.when((b == 0) & (n == 0) & (g == 0))
        def _():
            dlb_ref[...] = jnp.zeros_like(dlb_ref)

        @pl.when(n == 0)
        def _():
            for hh in range(HB):
                ds_ref[_head(g, hh)] = jnp.zeros((HEAD_DIM, HEAD_DIM), F32)

        for hh in range(HB):
            head = _head(g, hh)
            ln = _lanes(hh)
            _, vjp = jax.vjp(functools.partial(_hgrn_chunk, layer=layer), qp[:, ln], fp[:, ln], vi[:, ln], ssave[hh],
                             lb[:, ln])
            gq, gf, gv, gS, glb = vjp((do[:, ln], ds_ref[head]))
            c0, d = hh * HEAD_DIM, D_MODEL
            dp_ref[:, c0:c0 + HEAD_DIM] = gq.astype(dp_ref.dtype)
            dp_ref[:, d + c0:d + c0 + HEAD_DIM] = gf.astype(dp_ref.dtype)
            dp_ref[:, 2 * d + c0:2 * d + c0 + HEAD_DIM] = gv.astype(dp_ref.dtype)
            dlb_ref[head] += glb
            ds_ref[head] = gS
        dp_ref[:, 3 * D_MODEL:4 * D_MODEL] = dgt[...]

    hb = HEADS
    in_specs = [_chunk_specs_rev(nc, 0), _chunk_specs_rev(nc, hb), _chunk_specs_rev(nc, 2 * hb),
                pl.BlockSpec((DEPTH, HBW), lambda b, n, g: (0, g)),
                _state_spec(nc, True),
                _chunk_specs_rev(nc, 0), _chunk_specs_rev(nc, 0)]
    out_specs = [pl.BlockSpec((CHUNK, 4 * D_MODEL), lambda b, n, g: (b * nc + (nc - 1 - n), 0)),
                 _const_spec((HEADS, DEPTH, HEAD_DIM))]
    return _pcall(
        body, name=f"hgrn_bwd{layer}", grid=(bsz, nc, HEADS // HB), in_specs=in_specs, out_specs=out_specs,
        out_shape=(jax.ShapeDtypeStruct((n_tok, 4 * D_MODEL), BF16), jax.ShapeDtypeStruct((HEADS, DEPTH, HEAD_DIM), F32)),
        scratch_shapes=[pltpu.VMEM((HEADS, HEAD_DIM, HEAD_DIM), F32)],
        dims=("arbitrary", "arbitrary", "arbitrary"), args=(proj, proj, proj, lbl, s_saved, d_o, d_gate),
        carries=carries)


ROW_TILE = 512
MM_ROW_TILE = 1024


def _tile(n):
    for cand in (1024, 512, 1408, 384, 256, 128):
        if n % cand == 0:
            return cand
    return n


def _rmsnorm(x, w):
    return x * lax.rsqrt(jnp.mean(x * x, axis=1, keepdims=True) + EPS) * w


def norm_matmul(h, nw, w, relu2, name, carries=()):
    n_tok, d = h.shape
    slots = w.ndim == 3
    n_out = w.shape[0] * w.shape[2] if slots else w.shape[1]
    tm, tn = min(n_tok, MM_ROW_TILE), _tile(w.shape[2] if slots else n_out)
    if slots:
        per = w.shape[2] // tn
        w_spec = pl.BlockSpec((None, d, tn), lambda i, j: (j // per, 0, j % per))
    else:
        w_spec = pl.BlockSpec((d, tn), lambda i, j: (0, j))

    def body(h_ref, nw_ref, w_ref, y_ref, *outs):
        @pl.when(pl.program_id(1) == 0)
        def _():
            y_ref[...] = _rmsnorm(h_ref[...], nw_ref[...]).astype(BF16)

        acc = jnp.dot(y_ref[...], w_ref[...], preferred_element_type=F32)
        if relu2:
            r = jnp.maximum(acc, 0.0)
            outs[0][...] = (r * r).astype(BF16)
            outs[1][...] = r.astype(BF16)
        else:
            outs[0][...] = acc

    o_spec = pl.BlockSpec((tm, tn), lambda i, j: (i, j))
    if relu2:
        outs = (jax.ShapeDtypeStruct((n_tok, n_out), BF16),) * 2
    else:
        outs = (jax.ShapeDtypeStruct((n_tok, n_out), F32),)
    return _pcall(
        body, name=name, grid=(n_tok // tm, n_out // tn),
        in_specs=[pl.BlockSpec((tm, d), lambda i, j: (i, 0)), pl.BlockSpec((1, d), lambda i, j: (0, 0)), w_spec],
        out_specs=[pl.BlockSpec((tm, d), lambda i, j: (i, 0))] + [o_spec] * len(outs),
        out_shape=(jax.ShapeDtypeStruct((n_tok, d), BF16),) + outs,
        dims=("parallel", "arbitrary"), args=(h, nw, w), carries=carries)


def matmul(a, b, mode, name, out_dtype=F32, extra=None, epilogue=None, shards=1, carries=()):
    slots = mode == "nt" and b.ndim == 3
    if mode == "nn":
        (m, k), n = a.shape, b.shape[1]
    elif mode == "nt":
        (m, k), n = a.shape, (b.shape[1] if slots else b.shape[0])
    else:
        (k, m), n = a.shape, b.shape[1]
    tm, tk = min(m, MM_ROW_TILE), _tile(b.shape[2] if slots else k)
    tn = _tile(n // shards)
    nk = k // tk
    a_spec = pl.BlockSpec((tk, tm), lambda i, j, kk: (kk, i)) if mode == "tn" else pl.BlockSpec((tm, tk), lambda i, j, kk: (i, kk))
    if slots:
        kper = b.shape[2] // tk
        b_spec = pl.BlockSpec((None, tn, tk), lambda i, j, kk: (kk // kper, j, kk % kper))
    elif mode == "nt":
        b_spec = pl.BlockSpec((tn, tk), lambda i, j, kk: (j, kk))
    else:
        b_spec = pl.BlockSpec((tk, tn), lambda i, j, kk: (kk, j))
    dims = _DIMS[mode]

    def body(*refs):
        if extra is not None:
            a_ref, b_ref, e_ref, o_ref = refs[:4]
        else:
            (a_ref, b_ref, o_ref), e_ref = refs[:3], None
        kk = pl.program_id(2)
        part = lax.dot_general(a_ref[...].astype(BF16), b_ref[...].astype(BF16), dims, preferred_element_type=F32)

        def finish(acc):
            if epilogue == "add":
                acc = e_ref[...] + acc
            elif epilogue == "mul2":
                acc = acc * (2.0 * e_ref[...].astype(F32))
            o_ref[...] = acc.astype(o_ref.dtype)

        if nk == 1:
            finish(part)
            return
        acc_ref = refs[-1]

        @pl.when(kk == 0)
        def _():
            acc_ref[...] = part

        @pl.when(kk > 0)
        def _():
            acc_ref[...] += part

        @pl.when(kk == nk - 1)
        def _():
            finish(acc_ref[...])

    in_specs = [a_spec, b_spec]
    args = [a, b]
    if extra is not None:
        in_specs.append(pl.BlockSpec((tm, tn), lambda i, j, kk: (i, j)))
        args.append(extra)
    if shards > 1:
        per = n // shards // tn
        out_spec = pl.BlockSpec((None, tm, tn), lambda i, j, kk: (j // per, i, j % per))
        out_shape = jax.ShapeDtypeStruct((shards, m, n // shards), out_dtype)
    else:
        out_spec = pl.BlockSpec((tm, tn), lambda i, j, kk: (i, j))
        out_shape = jax.ShapeDtypeStruct((m, n), out_dtype)
    (res,), cres = _pcall(
        body, name=name, grid=(m // tm, n // tn, nk), in_specs=in_specs, out_specs=[out_spec], out_shape=[out_shape],
        scratch_shapes=[pltpu.VMEM((tm, tn), F32)] if nk > 1 else [], dims=("parallel", "parallel", "arbitrary"),
        args=args, carries=carries)
    return res, cres


def norm_backward(h, nw, dy, dres, name):
    n_tok, d = h.shape
    tm = min(n_tok, ROW_TILE)

    def body(h_ref, nw_ref, dy_ref, dr_ref, dh_ref, dhb_ref, dnw_ref):
        @pl.when(pl.program_id(0) == 0)
        def _():
            dnw_ref[...] = jnp.zeros_like(dnw_ref)

        _, vjp = jax.vjp(_rmsnorm, h_ref[...], nw_ref[...])
        gh, gw = vjp(dy_ref[...])
        dh = dr_ref[...] + gh
        dh_ref[...] = dh
        dhb_ref[...] = dh.astype(BF16)
        dnw_ref[...] += gw

    row = pl.BlockSpec((tm, d), lambda i: (i, 0))
    vec = pl.BlockSpec((1, d), lambda i: (0, 0))
    return pl.pallas_call(
        body, name=name, grid=(n_tok // tm,), in_specs=[row, vec, row, row], out_specs=[row, row, vec],
        out_shape=(jax.ShapeDtypeStruct((n_tok, d), F32), jax.ShapeDtypeStruct((n_tok, d), BF16),
                   jax.ShapeDtypeStruct((1, d), F32)),
        compiler_params=_cparams(("arbitrary",)),
    )(h, nw, dy, dres)


def _hgrn_post(o, gate, gw):
    return _rmsnorm(o, gw) * _silu(gate)


def hgrn_post_forward(o, proj, gw, name):
    n_tok, d = o.shape
    tm = min(n_tok, ROW_TILE)

    def body(o_ref, g_ref, w_ref, y_ref):
        y_ref[...] = _hgrn_post(o_ref[...], g_ref[...], w_ref[...]).astype(BF16)

    row = pl.BlockSpec((tm, d), lambda i: (i, 0))
    return pl.pallas_call(
        body, name=name, grid=(n_tok // tm,),
        in_specs=[row, pl.BlockSpec((tm, d), lambda i: (i, 3)), pl.BlockSpec((1, d), lambda i: (0, 0))],
        out_specs=row, out_shape=jax.ShapeDtypeStruct((n_tok, d), BF16),
        compiler_params=_cparams(("parallel",)),
    )(o, proj, gw)


def hgrn_post_backward(o, proj, gw, dy, name):
    n_tok, d = o.shape
    tm = min(n_tok, ROW_TILE)

    def body(o_ref, g_ref, w_ref, dy_ref, do_ref, dg_ref, dw_ref):
        @pl.when(pl.program_id(0) == 0)
        def _():
            dw_ref[...] = jnp.zeros_like(dw_ref)

        _, vjp = jax.vjp(_hgrn_post, o_ref[...], g_ref[...], w_ref[...])
        go, gg, gw_ = vjp(dy_ref[...])
        do_ref[...] = go
        dg_ref[...] = gg.astype(BF16)
        dw_ref[...] += gw_

    row = pl.BlockSpec((tm, d), lambda i: (i, 0))
    vec = pl.BlockSpec((1, d), lambda i: (0, 0))
    return pl.pallas_call(
        body, name=name, grid=(n_tok // tm,),
        in_specs=[row, pl.BlockSpec((tm, d), lambda i: (i, 3)), vec, row], out_specs=[row, row, vec],
        out_shape=(jax.ShapeDtypeStruct((n_tok, d), F32), jax.ShapeDtypeStruct((n_tok, d), BF16),
                   jax.ShapeDtypeStruct((1, d), F32)),
        compiler_params=_cparams(("arbitrary",)),
    )(o, proj, gw, dy)


def loss_head(h, nw, target):
    n_tok, d = h.shape
    tm = min(n_tok, ROW_TILE)

    def body(h_ref, nw_ref, t_ref, loss_ref, dh_ref, dhb_ref, dnw_ref):
        @pl.when(pl.program_id(0) == 0)
        def _():
            dnw_ref[...] = jnp.zeros_like(dnw_ref)
            loss_ref[...] = jnp.zeros_like(loss_ref)

        out, vjp = jax.vjp(_rmsnorm, h_ref[...], nw_ref[...])
        err = out - t_ref[...]
        part = 0.5 * jnp.sum(jnp.sum(err * err, axis=1, keepdims=True), axis=0, keepdims=True) / d
        loss_ref[...] += jnp.broadcast_to(part, loss_ref.shape)
        gh, gw = vjp(err / d)
        dh_ref[...] = gh
        dhb_ref[...] = gh.astype(BF16)
        dnw_ref[...] += gw

    row = pl.BlockSpec((tm, d), lambda i: (i, 0))
    vec = pl.BlockSpec((1, d), lambda i: (0, 0))
    return pl.pallas_call(
        body, name="loss_head", grid=(n_tok // tm,), in_specs=[row, vec, row],
        out_specs=[pl.BlockSpec((1, 128), lambda i: (0, 0)), row, row, vec],
        out_shape=(jax.ShapeDtypeStruct((1, 128), F32), jax.ShapeDtypeStruct((n_tok, d), F32),
                   jax.ShapeDtypeStruct((n_tok, d), BF16), jax.ShapeDtypeStruct((1, d), F32)),
        compiler_params=_cparams(("arbitrary",)),
    )(h, nw, target)


def _rows2d(shape):
    if len(shape) == 1:
        return (1, shape[0])
    return (math.prod(shape[:-1]), shape[-1])


def adamw(w, g, m, v, name):
    shape = w.shape
    r, c = _rows2d(shape)
    tr = r if r <= 256 else 256
    c1 = 1.0 / (1.0 - ADAM_B1 ** ADAM_STEP)
    c2 = 1.0 / (1.0 - ADAM_B2 ** ADAM_STEP)

    def body(w_ref, g_ref, m_ref, v_ref, d_ref, nm_ref, nv_ref):
        gg = g_ref[...]
        nm = ADAM_B1 * m_ref[...] + (1.0 - ADAM_B1) * gg
        nv = ADAM_B2 * v_ref[...] + (1.0 - ADAM_B2) * (gg * gg)
        d_ref[...] = -ADAM_LR * ((nm * c1) / (jnp.sqrt(nv * c2) + ADAM_EPS) + ADAM_WD * w_ref[...])
        nm_ref[...] = nm
        nv_ref[...] = nv

    spec = pl.BlockSpec((tr, c), lambda i: (i, 0))
    sds = jax.ShapeDtypeStruct((r, c), F32)
    outs = pl.pallas_call(
        body, name=name, grid=(r // tr,), in_specs=[spec] * 4, out_specs=[spec] * 3, out_shape=(sds,) * 3,
        compiler_params=_cparams(("parallel",)),
    )(w.reshape(r, c), g.reshape(r, c), m.reshape(r, c), v.reshape(r, c))
    return tuple(o.reshape(shape) for o in outs)


def add_slots(parts, name):
    s, r, c = parts.shape
    tr = r if r <= 256 else 256

    def body(p_ref, o_ref):
        acc = p_ref[0]
        for t in range(1, s):
            acc = acc + p_ref[t]
        o_ref[...] = acc

    return pl.pallas_call(
        body, name=name, grid=(r // tr,), in_specs=[pl.BlockSpec((s, tr, c), lambda i: (0, i, 0))],
        out_specs=pl.BlockSpec((tr, c), lambda i: (i, 0)), out_shape=jax.ShapeDtypeStruct((r, c), F32),
        compiler_params=_cparams(("parallel",)),
    )(parts)


def add_pair(a, b, name):
    s, r, c = a.shape
    tr = r if r <= 256 else 256

    def body(a_ref, b_ref, o_ref):
        o_ref[...] = a_ref[...] + b_ref[...]

    spec = pl.BlockSpec((None, tr, c), lambda t, i: (t, i, 0))
    return pl.pallas_call(
        body, name=name, grid=(s, r // tr), in_specs=[spec, spec], out_specs=spec,
        out_shape=jax.ShapeDtypeStruct((s, r, c), F32), compiler_params=_cparams(("parallel", "parallel")),
    )(a, b)


N_CHIPS = 4
N_DEV = 8
_ANY = pl.BlockSpec(memory_space=pl.ANY)


def _mesh_pos():
    return lax.axis_index("x"), lax.axis_index("y"), lax.axis_index("c")


def _other_chips(x, y):
    ps = [(1 - x, y), (x, 1 - y), (1 - x, 1 - y)]
    return [(p, 2 * p[0] + p[1]) for p in ps]


def _remote(src, dst, send_sem, recv_sem, dev):
    return pltpu.make_async_remote_copy(src_ref=src, dst_ref=dst, send_sem=send_sem, recv_sem=recv_sem,
                                        device_id=dev, device_id_type=MESH)


def gather_chips(arrs, name):
    n = len(arrs)

    def body(*refs):
        ins, outs = refs[:n], refs[n:2 * n]
        send_sems, recv_sems, local_sems = refs[2 * n:]
        x, y, c = _mesh_pos()
        me = 2 * x + y
        peers = _other_chips(x, y)
        locs, sends = [], []
        for a in range(n):
            cp = pltpu.make_async_copy(ins[a], outs[a].at[me], local_sems.at[a])
            cp.start()
            locs.append(cp)
            for j, ((px, py), _) in enumerate(peers):
                r = _remote(ins[a], outs[a].at[me], send_sems.at[a, j], recv_sems.at[a, j], (px, py, c))
                r.start()
                sends.append(r)
        for a in range(n):
            for j, ((px, py), t) in enumerate(peers):
                _remote(ins[a], outs[a].at[t], send_sems.at[a, j], recv_sems.at[a, j], (px, py, c)).wait_recv()
        for r in sends:
            r.wait_send()
        for cp in locs:
            cp.wait()

    return pl.pallas_call(
        body, name=name, in_specs=[_ANY] * n, out_specs=[_ANY] * n,
        out_shape=[jax.ShapeDtypeStruct((N_CHIPS,) + a.shape, a.dtype) for a in arrs],
        scratch_shapes=[pltpu.SemaphoreType.DMA((n, 3)), pltpu.SemaphoreType.DMA((n, 3)), pltpu.SemaphoreType.DMA((n,))],
    )(*arrs)


def exchange_sibling_half(gs, name):
    n = len(gs)

    def body(*refs):
        ins, outs = refs[:n], refs[n:2 * n]
        send_sems, recv_sems = refs[2 * n:]
        x, y, c = _mesh_pos()
        sends = []
        for a in range(n):
            half = gs[a].shape[1] // 2
            src = ins[a].at[:, pl.ds((1 - c) * half, half), :]
            r = _remote(src, outs[a], send_sems.at[a], recv_sems.at[a], (x, y, 1 - c))
            r.start()
            sends.append(r)
        for r in sends:
            r.wait_recv()
        for r in sends:
            r.wait_send()

    return pl.pallas_call(
        body, name=name, in_specs=[_ANY] * n, out_specs=[_ANY] * n,
        out_shape=[jax.ShapeDtypeStruct((g.shape[0], g.shape[1] // 2, g.shape[2]), g.dtype) for g in gs],
        scratch_shapes=[pltpu.SemaphoreType.DMA((n,)), pltpu.SemaphoreType.DMA((n,))],
    )(*gs)


def exchange_chips(ps, name):
    n = len(ps)

    def body(*refs):
        ins, outs = refs[:n], refs[n:2 * n]
        send_sems, recv_sems, local_sems = refs[2 * n:]
        x, y, c = _mesh_pos()
        me = 2 * x + y
        peers = _other_chips(x, y)
        locs, sends = [], []
        for a in range(n):
            cp = pltpu.make_async_copy(ins[a].at[me], outs[a].at[me], local_sems.at[a])
            cp.start()
            locs.append(cp)
            for j, ((px, py), t) in enumerate(peers):
                r = _remote(ins[a].at[t], outs[a].at[me], send_sems.at[a, j], recv_sems.at[a, j], (px, py, c))
                r.start()
                sends.append(r)
        for a in range(n):
            for j, ((px, py), t) in enumerate(peers):
                _remote(ins[a].at[t], outs[a].at[t], send_sems.at[a, j], recv_sems.at[a, j], (px, py, c)).wait_recv()
        for r in sends:
            r.wait_send()
        for cp in locs:
            cp.wait()

    return pl.pallas_call(
        body, name=name, in_specs=[_ANY] * n, out_specs=[_ANY] * n,
        out_shape=[jax.ShapeDtypeStruct(p.shape, p.dtype) for p in ps],
        scratch_shapes=[pltpu.SemaphoreType.DMA((n, 3)), pltpu.SemaphoreType.DMA((n, 3)), pltpu.SemaphoreType.DMA((n,))],
    )(*ps)


def share_sibling(groups, name):
    flat = [f for grp in groups for f in grp]
    n = len(flat)
    nw = len(groups)

    def body(*refs):
        ins, outs = refs[:n], refs[n:n + nw]
        send_sems, recv_sems, local_sems = refs[n + nw:]
        x, y, c = _mesh_pos()
        locs, sends, k = [], [], 0
        for w, grp in enumerate(groups):
            for l in range(len(grp)):
                half = grp[l].shape[0]
                dst = outs[w].at[l, pl.ds(c * half, half), :]
                cp = pltpu.make_async_copy(ins[k], dst, local_sems.at[k])
                cp.start()
                locs.append(cp)
                r = _remote(ins[k], dst, send_sems.at[k], recv_sems.at[k], (x, y, 1 - c))
                r.start()
                sends.append(r)
                k += 1
        k = 0
        for w, grp in enumerate(groups):
            for l in range(len(grp)):
                half = grp[l].shape[0]
                theirs = outs[w].at[l, pl.ds((1 - c) * half, half), :]
                _remote(ins[k], theirs, send_sems.at[k], recv_sems.at[k], (x, y, 1 - c)).wait_recv()
                k += 1
        for r in sends:
            r.wait_send()
        for cp in locs:
            cp.wait()

    return pl.pallas_call(
        body, name=name, in_specs=[_ANY] * n, out_specs=[_ANY] * nw,
        out_shape=[jax.ShapeDtypeStruct((len(grp), 2 * grp[0].shape[0], grp[0].shape[1]), F32) for grp in groups],
        scratch_shapes=[pltpu.SemaphoreType.DMA((n,)), pltpu.SemaphoreType.DMA((n,)), pltpu.SemaphoreType.DMA((n,))],
    )(*flat)


def gather_all(small, name):
    flips = [(fx, fy, fc) for fx in (0, 1) for fy in (0, 1) for fc in (0, 1)][1:]

    def body(in_ref, out_ref, send_sems, recv_sems, local_sem):
        x, y, c = _mesh_pos()
        me = 4 * x + 2 * y + c
        cp = pltpu.make_async_copy(in_ref, out_ref.at[me], local_sem)
        cp.start()
        peers = [((1 - x) if fx else x, (1 - y) if fy else y, (1 - c) if fc else c) for fx, fy, fc in flips]
        sends = []
        for k, p in enumerate(peers):
            r = _remote(in_ref, out_ref.at[me], send_sems.at[k], recv_sems.at[k], p)
            r.start()
            sends.append(r)
        for k, p in enumerate(peers):
            _remote(in_ref, out_ref.at[4 * p[0] + 2 * p[1] + p[2]], send_sems.at[k], recv_sems.at[k], p).wait_recv()
        for r in sends:
            r.wait_send()
        cp.wait()

    return pl.pallas_call(
        body, name=name, in_specs=[_ANY], out_specs=_ANY,
        out_shape=jax.ShapeDtypeStruct((N_DEV,) + small.shape, small.dtype),
        scratch_shapes=[pltpu.SemaphoreType.DMA((N_DEV - 1,)), pltpu.SemaphoreType.DMA((N_DEV - 1,)),
                        pltpu.SemaphoreType.DMA],
    )(small)


BIG = ("gdn_w_in", "gdn_w_out", "hgrn_w_in", "hgrn_w_out", "mlp_w_up", "mlp_w_down")
WEIGHTS = ("gdn_w_in", "gdn_conv", "gdn_a_log", "gdn_dt_bias", "gdn_onorm", "gdn_w_out", "hgrn_w_in", "hgrn_lb_logits",
           "hgrn_gnorm", "hgrn_w_out", "norm_mix", "norm_mlp", "mlp_w_up", "mlp_w_down", "norm_final")
SMALL_ROWS = 96


def _pad_lanes(v, n):
    return jnp.pad(v, [(0, 0)] * (v.ndim - 1) + [(0, n - v.shape[-1])])


def _local_grads(x, target, w):
    bsz, t_len, d = x.shape
    n_tok = bsz * t_len
    h = x.reshape(n_tok, d)
    lbl = w["hgrn_lb_logits"]
    saved = []
    for i in range(DEPTH):
        j = i // 2
        nmix = w["norm_mix"][i][None, :]
        if i % 2 == 0:
            y, proj = norm_matmul(h, nmix, w["gdn_w_in"][j], False, f"in_proj{i}")
            al = _pad_lanes(w["gdn_a_log"][j][None, :], 128)
            dtb = _pad_lanes(w["gdn_dt_bias"][j][None, :], 128)
            on = w["gdn_onorm"][j][None, :]
            og, ssave = gdn_forward(proj, w["gdn_conv"][j], al, dtb, on, bsz)
            mix = (proj, ssave, al, dtb, on)
            w_out = w["gdn_w_out"][j]
        else:
            y, proj = norm_matmul(h, nmix, w["hgrn_w_in"][j], False, f"in_proj{i}")
            o, ssave = hgrn_forward(proj, lbl, i, bsz)
            gn = w["hgrn_gnorm"][j][None, :]
            og = hgrn_post_forward(o, proj, gn, f"hgrn_post{i}")
            mix = (proj, ssave, o, gn)
            w_out = w["hgrn_w_out"][j]
        h1 = matmul(og, w_out, "nn", f"out_proj{i}", extra=h, epilogue="add")
        nmlp = w["norm_mlp"][i][None, :]
        z, a, r = norm_matmul(h1, nmlp, w["mlp_w_up"][i], True, f"mlp_up{i}")
        h2 = matmul(a, w["mlp_w_down"][i], "nn", f"mlp_down{i}", extra=h1, epilogue="add")
        saved.append((h, nmix, y, mix, og, w_out, h1, nmlp, z, a, r))
        h = h2
    loss_row, dh, d_nf = loss_head(h, w["norm_final"][None, :], target.reshape(n_tok, d))

    big = {k: [None] * (DEPTH if k.startswith("mlp") else DEPTH // 2) for k in BIG}
    d_nmix, d_nmlp = [None] * DEPTH, [None] * DEPTH
    d_conv, d_alog, d_dtb, d_onorm, d_gnorm = [None] * 2, [None] * 2, [None] * 2, [None] * 2, [None] * 2
    d_lbl = jnp.zeros((DEPTH, d), F32)
    for i in reversed(range(DEPTH)):
        j = i // 2
        h_in, nmix, y, mix, og, w_out, h1, nmlp, z, a, r = saved[i]
        du = matmul(dh, w["mlp_w_down"][i], "nt", f"d_mlp_act{i}", out_dtype=BF16, extra=r, epilogue="mul2")
        big["mlp_w_down"][i] = matmul(a, dh, "tn", f"dw_down{i}").reshape(N_CHIPS, -1, d)
        dz = matmul(du, w["mlp_w_up"][i], "nt", f"d_mlp_in{i}")
        big["mlp_w_up"][i] = matmul(z, du, "tn", f"dw_up{i}", shards=N_CHIPS)
        dh1, d_nmlp[i] = norm_backward(h1, nmlp, dz, dh, f"d_norm_mlp{i}")
        dog = matmul(dh1, w_out, "nt", f"d_mix_out{i}")
        dw_out = matmul(og, dh1, "tn", f"dw_out{i}").reshape(N_CHIPS, -1, d)
        if i % 2 == 0:
            proj, ssave, al, dtb, on = mix
            dq, dk, dv, dg, dab, dcw, dal, ddt, don = gdn_backward(proj, w["gdn_conv"][j], al, dtb, on, ssave, dog, bsz)
            dproj = jnp.concatenate([dq, dk, dv, dg, dab], axis=1)
            dy = matmul(dproj, w["gdn_w_in"][j], "nt", f"d_in_proj{i}")
            dw_in = matmul(y, dproj, "tn", f"dw_in{i}")[:, :GDN_IN]
            big["gdn_w_in"][j] = jnp.transpose(dw_in.reshape(d, N_CHIPS, GDN_IN // N_CHIPS), (1, 0, 2))
            big["gdn_w_out"][j] = dw_out
            d_conv[j] = jnp.transpose(dcw, (2, 1, 0, 3)).reshape(CONV_K, 3 * d)
            d_alog[j], d_dtb[j], d_onorm[j] = dal[0, :HEADS], ddt[0, :HEADS], don[0]
        else:
            proj, ssave, o, gn = mix
            do_raw, dgate, dgn = hgrn_post_backward(o, proj, gn, dog, f"d_hgrn_post{i}")
            dq, df, dv, dlb = hgrn_backward(proj, lbl, ssave, do_raw, i, bsz)
            dproj = jnp.concatenate([dq, df, dv, dgate], axis=1)
            dy = matmul(dproj, w["hgrn_w_in"][j], "nt", f"d_in_proj{i}")
            big["hgrn_w_in"][j] = matmul(y, dproj, "tn", f"dw_in{i}", shards=N_CHIPS)
            big["hgrn_w_out"][j] = dw_out
            d_gnorm[j] = dgn[0]
            d_lbl = d_lbl + jnp.transpose(dlb, (1, 0, 2)).reshape(DEPTH, d)
        dh, d_nmix[i] = norm_backward(h_in, nmix, dy, dh1, f"d_norm_mix{i}")
    small = {
        "gdn_conv": jnp.stack(d_conv), "gdn_a_log": jnp.stack(d_alog), "gdn_dt_bias": jnp.stack(d_dtb),
        "gdn_onorm": jnp.stack(d_onorm), "hgrn_lb_logits": d_lbl, "hgrn_gnorm": jnp.stack(d_gnorm),
        "norm_mix": jnp.concatenate(d_nmix, axis=0), "norm_mlp": jnp.concatenate(d_nmlp, axis=0), "norm_final": d_nf[0],
    }
    return loss_row, dh.reshape(x.shape), big, small


_SMALL_LAYOUT = {
    "norm_mix": (0, 4, D_MODEL), "norm_mlp": (8, 4, D_MODEL), "norm_final": (16, 1, D_MODEL),
    "hgrn_lb_logits": (24, 4, D_MODEL), "gdn_onorm": (32, 2, 128), "gdn_a_log": (40, 2, HEADS),
    "gdn_dt_bias": (48, 2, HEADS), "loss": (56, 1, 128), "gdn_conv": (64, 24, D_MODEL), "hgrn_gnorm": (88, 2, D_MODEL),
}


def _pack_small(small, loss_row):
    rows = []
    for name, (first, nrow, lanes) in _SMALL_LAYOUT.items():
        v = loss_row if name == "loss" else small[name]
        v = _pad_lanes(v.reshape(nrow, -1), D_MODEL)
        rows.append(jnp.pad(v, ((0, -nrow % 8), (0, 0))))
    return jnp.concatenate(rows, axis=0)


def _unpack_small(packed, name, shape):
    first, nrow, lanes = _SMALL_LAYOUT[name]
    return packed[first:first + nrow, :lanes].reshape(shape)


def kernel(x, gdn_w_in, gdn_conv, gdn_a_log, gdn_dt_bias, gdn_onorm, gdn_w_out, hgrn_w_in, hgrn_lb_logits, hgrn_gnorm, hgrn_w_out, norm_mix, norm_mlp, mlp_w_up, mlp_w_down, norm_final, loss_target, m_gdn_w_in, m_gdn_conv, m_gdn_a_log, m_gdn_dt_bias, m_gdn_onorm, m_gdn_w_out, m_hgrn_w_in, m_hgrn_lb_logits, m_hgrn_gnorm, m_hgrn_w_out, m_norm_mix, m_norm_mlp, m_mlp_w_up, m_mlp_w_down, m_norm_final, v_gdn_w_in, v_gdn_conv, v_gdn_a_log, v_gdn_dt_bias, v_gdn_onorm, v_gdn_w_out, v_hgrn_w_in, v_hgrn_lb_logits, v_hgrn_gnorm, v_hgrn_w_out, v_norm_mix, v_norm_mlp, v_mlp_w_up, v_mlp_w_down, v_norm_final):
    p = dict(gdn_w_in=gdn_w_in, gdn_conv=gdn_conv, gdn_a_log=gdn_a_log, gdn_dt_bias=gdn_dt_bias, gdn_onorm=gdn_onorm,
             gdn_w_out=gdn_w_out, hgrn_w_in=hgrn_w_in, hgrn_lb_logits=hgrn_lb_logits, hgrn_gnorm=hgrn_gnorm,
             hgrn_w_out=hgrn_w_out, norm_mix=norm_mix, norm_mlp=norm_mlp, mlp_w_up=mlp_w_up, mlp_w_down=mlp_w_down,
             norm_final=norm_final)
    m = dict(gdn_w_in=m_gdn_w_in, gdn_conv=m_gdn_conv, gdn_a_log=m_gdn_a_log, gdn_dt_bias=m_gdn_dt_bias,
             gdn_onorm=m_gdn_onorm, gdn_w_out=m_gdn_w_out, hgrn_w_in=m_hgrn_w_in, hgrn_lb_logits=m_hgrn_lb_logits,
             hgrn_gnorm=m_hgrn_gnorm, hgrn_w_out=m_hgrn_w_out, norm_mix=m_norm_mix, norm_mlp=m_norm_mlp,
             mlp_w_up=m_mlp_w_up, mlp_w_down=m_mlp_w_down, norm_final=m_norm_final)
    v = dict(gdn_w_in=v_gdn_w_in, gdn_conv=v_gdn_conv, gdn_a_log=v_gdn_a_log, gdn_dt_bias=v_gdn_dt_bias,
             gdn_onorm=v_gdn_onorm, gdn_w_out=v_gdn_w_out, hgrn_w_in=v_hgrn_w_in, hgrn_lb_logits=v_hgrn_lb_logits,
             hgrn_gnorm=v_hgrn_gnorm, hgrn_w_out=v_hgrn_w_out, norm_mix=v_norm_mix, norm_mlp=v_norm_mlp,
             mlp_w_up=v_mlp_w_up, mlp_w_down=v_mlp_w_down, norm_final=v_norm_final)
    xi, yi, ci = _mesh_pos()
    chip = 2 * xi + yi
    d = D_MODEL

    sharded = list(BIG) + ["gdn_conv", "hgrn_gnorm"]
    got = dict(zip(sharded, gather_chips([p[k].astype(BF16) if k in BIG else p[k] for k in sharded], "gather_weights")))
    cols = lambda g: jnp.transpose(g, (1, 2, 0, 3)).reshape(g.shape[1], g.shape[2], -1)
    rows = lambda g: jnp.transpose(g, (1, 0, 2, 3)).reshape(g.shape[1], -1, g.shape[3])
    w = dict(p)
    w["gdn_w_in"] = _pad_lanes(cols(got["gdn_w_in"]), GDN_IN_PAD)
    w["hgrn_w_in"] = cols(got["hgrn_w_in"])
    w["mlp_w_up"] = cols(got["mlp_w_up"])
    w["gdn_conv"] = cols(got["gdn_conv"])
    w["gdn_w_out"] = rows(got["gdn_w_out"])
    w["hgrn_w_out"] = rows(got["hgrn_w_out"])
    w["mlp_w_down"] = rows(got["mlp_w_down"])
    w["hgrn_gnorm"] = jnp.transpose(got["hgrn_gnorm"], (1, 0, 2)).reshape(got["hgrn_gnorm"].shape[1], -1)

    loss_row, grad_x, big, small = _local_grads(x, loss_target, w)

    order = [(k, l) for k in BIG for l in range(len(big[k]))]
    gs = [big[k][l] for k, l in order]
    theirs = exchange_sibling_half(gs, "reduce_cores")
    mine = [lax.dynamic_slice_in_dim(g, ci * (g.shape[1] // 2), g.shape[1] // 2, axis=1) for g in gs]
    ps = [add_pair(a, b, f"add_cores{n}") for n, (a, b) in enumerate(zip(mine, theirs))]
    slots = exchange_chips(ps, "reduce_chips")
    halves = [add_slots(s, f"add_chips{n}") for n, s in enumerate(slots)]
    groups = [[halves[n] for n, (k, l) in enumerate(order) if k == name] for name in BIG]
    grads = dict(zip(BIG, share_sibling(groups, "share_cores")))

    total = add_slots(gather_all(_pack_small(small, loss_row), "gather_small"), "add_small")
    loss = total[_SMALL_LAYOUT["loss"][0], 0]
    for name in WEIGHTS:
        if name in BIG:
            continue
        if name == "gdn_conv":
            full = _unpack_small(total, name, (2, CONV_K, 3 * d))
            grads[name] = lax.dynamic_slice_in_dim(full, chip * (3 * d // N_CHIPS), 3 * d // N_CHIPS, axis=2)
        elif name == "hgrn_gnorm":
            full = _unpack_small(total, name, (2, d))
            grads[name] = lax.dynamic_slice_in_dim(full, chip * (d // N_CHIPS), d // N_CHIPS, axis=1)
        else:
            grads[name] = _unpack_small(total, name, p[name].shape)

    delta, new_m, new_v = {}, {}, {}
    for name in WEIGHTS:
        delta[name], new_m[name], new_v[name] = adamw(p[name], grads[name], m[name], v[name], f"adamw_{name}")
    return (loss, grad_x, *[grads[n] for n in WEIGHTS], *[delta[n] for n in WEIGHTS],
            *[new_m[n] for n in WEIGHTS], *[new_v[n] for n in WEIGHTS])


def add_core_halves(g, theirs, core, name):
    s, r, c = g.shape
    r2 = r // 2
    tr = min(r2, 256)
    nb = r2 // tr

    def body(core_ref, g_ref, t_ref, o_ref):
        o_ref[...] = g_ref[...] + t_ref[...]

    grid_spec = pltpu.PrefetchScalarGridSpec(
        num_scalar_prefetch=1, grid=(s, nb),
        in_specs=[pl.BlockSpec((None, tr, c), lambda t, i, cr: (t, cr[0] * nb + i, 0)),
                  pl.BlockSpec((None, tr, c), lambda t, i, cr: (t, i, 0))],
        out_specs=pl.BlockSpec((None, tr, c), lambda t, i, cr: (t, i, 0)))
    return pl.pallas_call(body, name=name, grid_spec=grid_spec, out_shape=jax.ShapeDtypeStruct((s, r2, c), F32),
                          compiler_params=_cparams(("parallel", "parallel")))(core, g, theirs)


def add_chip_slots(slots, name):
    n_l = len(slots)
    s, r2, c = slots[0].shape
    tr = min(r2, 256)
    nb = r2 // tr

    def body(*refs):
        ins, o_ref = refs[:n_l], refs[n_l]
        for k in range(n_l):
            @pl.when(pl.program_id(0) == k)
            def _(k=k):
                acc = ins[k][0]
                for t in range(1, s):
                    acc = acc + ins[k][t]
                o_ref[...] = acc

    in_specs = [pl.BlockSpec((s, tr, c), lambda l, i, k=k: (0, jnp.where(l == k, i, 0), 0)) for k in range(n_l)]
    return pl.pallas_call(
        body, name=name, grid=(n_l, nb), in_specs=in_specs, out_specs=pl.BlockSpec((None, tr, c), lambda l, i: (l, i, 0)),
        out_shape=jax.ShapeDtypeStruct((n_l, r2, c), F32), compiler_params=_cparams(("arbitrary", "arbitrary")),
    )(*slots)


def adamw_halves(w, m, v, mine, theirs, name):
    n_l, r, c = w.shape
    r2 = r // 2
    tr = min(r2, 256)
    nb = r2 // tr
    c1 = 1.0 / (1.0 - ADAM_B1 ** ADAM_STEP)
    c2 = 1.0 / (1.0 - ADAM_B2 ** ADAM_STEP)

    def body(w_ref, m_ref, v_ref, mine_ref, theirs_ref, g_ref, d_ref, nm_ref, nv_ref):
        my_half = (pl.program_id(1) // nb) == lax.axis_index("c")
        gg = jnp.where(my_half, mine_ref[...], theirs_ref[...])
        nm = ADAM_B1 * m_ref[...] + (1.0 - ADAM_B1) * gg
        nv = ADAM_B2 * v_ref[...] + (1.0 - ADAM_B2) * (gg * gg)
        g_ref[...] = gg
        d_ref[...] = -ADAM_LR * ((nm * c1) / (jnp.sqrt(nv * c2) + ADAM_EPS) + ADAM_WD * w_ref[...])
        nm_ref[...] = nm
        nv_ref[...] = nv

    full = pl.BlockSpec((None, tr, c), lambda l, i: (l, i, 0))
    half = pl.BlockSpec((None, tr, c), lambda l, i: (l, i % nb, 0))
    sds = jax.ShapeDtypeStruct((n_l, r, c), F32)
    return pl.pallas_call(
        body, name=name, grid=(n_l, r // tr), in_specs=[full, full, full, half, half], out_specs=[full] * 4,
        out_shape=(sds,) * 4, compiler_params=_cparams(("parallel", "parallel")),
    )(w, m, v, mine, theirs)


def _layer_weight(kind, i):
    if kind == "up":
        return "mlp_w_up", i
    if kind == "down":
        return "mlp_w_down", i
    return ("gdn_w_" if i % 2 == 0 else "hgrn_w_") + kind, i // 2


def kernel(x, gdn_w_in, gdn_conv, gdn_a_log, gdn_dt_bias, gdn_onorm, gdn_w_out, hgrn_w_in, hgrn_lb_logits, hgrn_gnorm, hgrn_w_out, norm_mix, norm_mlp, mlp_w_up, mlp_w_down, norm_final, loss_target, m_gdn_w_in, m_gdn_conv, m_gdn_a_log, m_gdn_dt_bias, m_gdn_onorm, m_gdn_w_out, m_hgrn_w_in, m_hgrn_lb_logits, m_hgrn_gnorm, m_hgrn_w_out, m_norm_mix, m_norm_mlp, m_mlp_w_up, m_mlp_w_down, m_norm_final, v_gdn_w_in, v_gdn_conv, v_gdn_a_log, v_gdn_dt_bias, v_gdn_onorm, v_gdn_w_out, v_hgrn_w_in, v_hgrn_lb_logits, v_hgrn_gnorm, v_hgrn_w_out, v_norm_mix, v_norm_mlp, v_mlp_w_up, v_mlp_w_down, v_norm_final):
    p = dict(gdn_w_in=gdn_w_in, gdn_conv=gdn_conv, gdn_a_log=gdn_a_log, gdn_dt_bias=gdn_dt_bias, gdn_onorm=gdn_onorm,
             gdn_w_out=gdn_w_out, hgrn_w_in=hgrn_w_in, hgrn_lb_logits=hgrn_lb_logits, hgrn_gnorm=hgrn_gnorm,
             hgrn_w_out=hgrn_w_out, norm_mix=norm_mix, norm_mlp=norm_mlp, mlp_w_up=mlp_w_up, mlp_w_down=mlp_w_down,
             norm_final=norm_final)
    m = dict(gdn_w_in=m_gdn_w_in, gdn_conv=m_gdn_conv, gdn_a_log=m_gdn_a_log, gdn_dt_bias=m_gdn_dt_bias,
             gdn_onorm=m_gdn_onorm, gdn_w_out=m_gdn_w_out, hgrn_w_in=m_hgrn_w_in, hgrn_lb_logits=m_hgrn_lb_logits,
             hgrn_gnorm=m_hgrn_gnorm, hgrn_w_out=m_hgrn_w_out, norm_mix=m_norm_mix, norm_mlp=m_norm_mlp,
             mlp_w_up=m_mlp_w_up, mlp_w_down=m_mlp_w_down, norm_final=m_norm_final)
    v = dict(gdn_w_in=v_gdn_w_in, gdn_conv=v_gdn_conv, gdn_a_log=v_gdn_a_log, gdn_dt_bias=v_gdn_dt_bias,
             gdn_onorm=v_gdn_onorm, gdn_w_out=v_gdn_w_out, hgrn_w_in=v_hgrn_w_in, hgrn_lb_logits=v_hgrn_lb_logits,
             hgrn_gnorm=v_hgrn_gnorm, hgrn_w_out=v_hgrn_w_out, norm_mix=v_norm_mix, norm_mlp=v_norm_mlp,
             mlp_w_up=v_mlp_w_up, mlp_w_down=v_mlp_w_down, norm_final=v_norm_final)
    xi, yi, ci = _mesh_pos()
    chip = 2 * xi + yi
    core = jnp.reshape(ci, (1,)).astype(jnp.int32)
    d = D_MODEL
    bsz, t_len, _ = x.shape
    n_tok = bsz * t_len

    def shard(kind, i):
        name, idx = _layer_weight(kind, i)
        return p[name][idx].astype(BF16)

    def w_in_of(i, slots):
        if i % 2 == 0:
            return _pad_lanes(jnp.transpose(slots, (1, 0, 2)).reshape(d, GDN_IN), GDN_IN_PAD)
        return slots

    (first,) = run_carries([gather_carry([shard("in", 0), p["gdn_conv"], p["hgrn_gnorm"]])], "gather_first")
    conv = jnp.transpose(first[1], (1, 2, 0, 3)).reshape(DEPTH // 2, CONV_K, 3 * d)
    gnorm = jnp.transpose(first[2], (1, 0, 2)).reshape(DEPTH // 2, d)
    lbl = p["hgrn_lb_logits"]
    h = x.reshape(n_tok, d)
    next_in, next_out = first[0], None
    saved = []
    for i in range(DEPTH):
        j = i // 2
        w_in = w_in_of(i, next_in)
        nmix = p["norm_mix"][i][None, :]
        (y, proj), got = norm_matmul(h, nmix, w_in, False, f"in_proj{i}",
                                     [gather_carry([shard("out", 0)])] if i == 0 else [])
        if i == 0:
            next_out = got[0][0]
        ride = [gather_carry([shard("up", i), shard("down", i)])]
        if i % 2 == 0:
            al = _pad_lanes(p["gdn_a_log"][j][None, :], 128)
            dtb = _pad_lanes(p["gdn_dt_bias"][j][None, :], 128)
            on = p["gdn_onorm"][j][None, :]
            (og, ssave), got = gdn_forward(proj, conv[j], al, dtb, on, bsz, f"gdn_fwd{i}", ride)
            mix = (proj, ssave, al, dtb, on)
        else:
            (o, ssave), got = hgrn_forward(proj, lbl, i, bsz, ride)
            gn = gnorm[j][None, :]
            og = hgrn_post_forward(o, proj, gn, f"hgrn_post{i}")
            mix = (proj, ssave, o, gn)
        w_up, w_down = got[0][0], got[0][1].reshape(MLP_HIDDEN, d)
        w_out = next_out.reshape(d, d)
        h1, _ = matmul(og, w_out, "nn", f"out_proj{i}", extra=h, epilogue="add")
        nmlp = p["norm_mlp"][i][None, :]
        (z, a, r), got = norm_matmul(h1, nmlp, w_up, True, f"mlp_up{i}",
                                     [gather_carry([shard("in", i + 1)])] if i + 1 < DEPTH else [])
        if i + 1 < DEPTH:
            next_in = got[0][0]
        h2, got = matmul(a, w_down, "nn", f"mlp_down{i}", extra=h1, epilogue="add",
                         carries=[gather_carry([shard("out", i + 1)])] if i + 1 < DEPTH else [])
        if i + 1 < DEPTH:
            next_out = got[0][0]
        saved.append((h, nmix, y, mix, og, w_in, w_out, h1, nmlp, z, a, r, w_up, w_down))
        h = h2
    loss_row, dh, dhb, d_nf = loss_head(h, p["norm_final"][None, :], loss_target.reshape(n_tok, d))

    G = {k: [None] * DEPTH for k in ("in", "out", "up", "down")}
    P = {k: [None] * DEPTH for k in ("in", "out", "up", "down")}
    slots = {k: [None] * DEPTH for k in ("in", "out", "up", "down")}
    d_nmix, d_nmlp = [None] * DEPTH, [None] * DEPTH
    d_conv, d_alog, d_dtb, d_onorm, d_gnorm = [None] * 2, [None] * 2, [None] * 2, [None] * 2, [None] * 2
    d_lbl = jnp.zeros((DEPTH, d), F32)
    for i in reversed(range(DEPTH)):
        j = i // 2
        h_in, nmix, y, mix, og, w_in, w_out, h1, nmlp, z, a, r, w_up, w_down = saved[i]
        ride = [sibling_half_carry([G["in"][i + 1], G["out"][i + 1]])] if i + 1 < DEPTH else []
        du, got = matmul(dhb, w_down, "nt", f"d_mlp_act{i}", out_dtype=BF16, extra=r, epilogue="mul2", carries=ride)
        if i + 1 < DEPTH:
            P["in"][i + 1] = add_core_halves(G["in"][i + 1], got[0][0], core, f"add_cores_in{i + 1}")
            P["out"][i + 1] = add_core_halves(G["out"][i + 1], got[0][1], core, f"add_cores_out{i + 1}")
        G["down"][i] = matmul(a, dhb, "tn", f"dw_down{i}")[0].reshape(N_CHIPS, -1, d)
        dz, _ = matmul(du, w_up, "nt", f"d_mlp_in{i}")
        G["up"][i], _ = matmul(z, du, "tn", f"dw_up{i}", shards=N_CHIPS)
        dh1, dh1b, d_nmlp[i] = norm_backward(h1, nmlp, dz, dh, f"d_norm_mlp{i}")
        dog, _ = matmul(dh1b, w_out, "nt", f"d_mix_out{i}")
        G["out"][i] = matmul(og, dh1b, "tn", f"dw_out{i}")[0].reshape(N_CHIPS, -1, d)
        ride = [sibling_half_carry([G["up"][i], G["down"][i]])]
        if i + 1 < DEPTH:
            ride.append(chips_carry([P[k][i + 1] for k in ("in", "out", "up", "down")]))
        if i % 2 == 0:
            proj, ssave, al, dtb, on = mix
            (dproj, dcw, dal, ddt, don), got = gdn_backward(proj, conv[j], al, dtb, on, ssave, dog, bsz, f"gdn_bwd{i}", ride)
            d_conv[j] = jnp.transpose(dcw, (2, 1, 0, 3)).reshape(CONV_K, 3 * d)
            d_alog[j], d_dtb[j], d_onorm[j] = dal[0, :HEADS], ddt[0, :HEADS], don[0]
        else:
            proj, ssave, o, gn = mix
            do_raw, dgate, dgn = hgrn_post_backward(o, proj, gn, dog, f"d_hgrn_post{i}")
            (dproj, dlb), got = hgrn_backward(proj, lbl, ssave, do_raw, dgate, i, bsz, ride)
            d_gnorm[j] = dgn[0]
            d_lbl = d_lbl + jnp.transpose(dlb, (1, 0, 2)).reshape(DEPTH, d)
        P["up"][i] = add_core_halves(G["up"][i], got[0][0], core, f"add_cores_up{i}")
        P["down"][i] = add_core_halves(G["down"][i], got[0][1], core, f"add_cores_down{i}")
        if i + 1 < DEPTH:
            for k, s in zip(("in", "out", "up", "down"), got[1]):
                slots[k][i + 1] = s
        ride = [chips_carry([P["up"][0], P["down"][0]])] if i == 0 else []
        dy, got = matmul(dproj, w_in, "nt", f"d_in_proj{i}", carries=ride)
        if i == 0:
            slots["up"][0], slots["down"][0] = got[0]
        if i % 2 == 0:
            dw_in = matmul(y, dproj, "tn", f"dw_in{i}")[0][:, :GDN_IN]
            G["in"][i] = jnp.transpose(dw_in.reshape(d, N_CHIPS, GDN_IN // N_CHIPS), (1, 0, 2))
        else:
            G["in"][i], _ = matmul(y, dproj, "tn", f"dw_in{i}", shards=N_CHIPS)
        dh, dhb, d_nmix[i] = norm_backward(h_in, nmix, dy, dh1, f"d_norm_mix{i}")
    grad_x = dh.reshape(x.shape)

    (got,) = run_carries([sibling_half_carry([G["in"][0], G["out"][0]])], "reduce_cores_last")
    P["in"][0] = add_core_halves(G["in"][0], got[0], core, "add_cores_in0")
    P["out"][0] = add_core_halves(G["out"][0], got[1], core, "add_cores_out0")
    (got,) = run_carries([chips_carry([P["in"][0], P["out"][0]])], "reduce_chips_last")
    slots["in"][0], slots["out"][0] = got
    by_weight = {}
    for kind in ("in", "out", "up", "down"):
        for i in range(DEPTH):
            by_weight.setdefault(_layer_weight(kind, i)[0], []).append(slots[kind][i])
    mine = {name: add_chip_slots(by_weight[name], f"add_chips_{name}") for name in BIG}
    small = {
        "gdn_conv": jnp.stack(d_conv), "gdn_a_log": jnp.stack(d_alog), "gdn_dt_bias": jnp.stack(d_dtb),
        "gdn_onorm": jnp.stack(d_onorm), "hgrn_lb_logits": d_lbl, "hgrn_gnorm": jnp.stack(d_gnorm),
        "norm_mix": jnp.concatenate(d_nmix, axis=0), "norm_mlp": jnp.concatenate(d_nmlp, axis=0), "norm_final": d_nf[0],
    }
    theirs, (blocks,) = run_carries([sibling_copy_carry([mine[name] for name in BIG]),
                                     gather_all_carry(_pack_small(small, loss_row))], "share_cores")
    theirs = dict(zip(BIG, theirs))

    total = add_slots(blocks, "add_small")
    loss = total[_SMALL_LAYOUT["loss"][0], 0]
    grads, delta, new_m, new_v = {}, {}, {}, {}
    for name in WEIGHTS:
        if name in BIG:
            grads[name], delta[name], new_m[name], new_v[name] = adamw_halves(
                p[name], m[name], v[name], mine[name], theirs[name], f"adamw_{name}")
            continue
        if name == "gdn_conv":
            full = _unpack_small(total, name, (2, CONV_K, 3 * d))
            grads[name] = lax.dynamic_slice_in_dim(full, chip * (3 * d // N_CHIPS), 3 * d // N_CHIPS, axis=2)
        elif name == "hgrn_gnorm":
            full = _unpack_small(total, name, (2, d))
            grads[name] = lax.dynamic_slice_in_dim(full, chip * (d // N_CHIPS), d // N_CHIPS, axis=1)
        else:
            grads[name] = _unpack_small(total, name, p[name].shape)
        delta[name], new_m[name], new_v[name] = adamw(p[name], grads[name], m[name], v[name], f"adamw_{name}")
    return (loss, grad_x, *[grads[n] for n in WEIGHTS], *[delta[n] for n in WEIGHTS],
            *[new_m[n] for n in WEIGHTS], *[new_v[n] for n in WEIGHTS])
```

```python
import functools
import math

import jax
import jax.numpy as jnp
from jax import lax
from jax.experimental import pallas as pl
from jax.experimental.pallas import tpu as pltpu

F32 = jnp.float32
BF16 = jnp.bfloat16
HI = lax.Precision.HIGHEST

D_MODEL = 1024
HEADS = 8
HEAD_DIM = 128
CHUNK = 64
SUB = 16
CONV_K = 4
HALO = 8
DEPTH = 4
EPS = 1e-6
MLP_HIDDEN = 4 * D_MODEL
GDN_MAIN = 4 * D_MODEL
GDN_IN = GDN_MAIN + 2 * HEADS
GDN_IN_PAD = GDN_MAIN + 128
NEG = -1e30

ADAM_LR = 0.001
ADAM_B1 = 0.9
ADAM_B2 = 0.999
ADAM_EPS = 1e-08
ADAM_WD = 0.01
ADAM_STEP = 10

VMEM_LIMIT = 48 * 1024 * 1024

MESH = pl.DeviceIdType.MESH


def _cparams(sem=None, **kw):
    if sem is not None:
        kw["dimension_semantics"] = sem
    return pltpu.CompilerParams(vmem_limit_bytes=VMEM_LIMIT, **kw)


def _iota(shape, dim):
    return lax.broadcasted_iota(jnp.int32, shape, dim)


_DIMS = {"nn": (((1,), (0,)), ((), ())), "nt": (((1,), (1,)), ((), ())), "tn": (((0,), (0,)), ((), ()))}


def _dot(a, b, mode):
    return lax.dot_general(a.astype(BF16), b.astype(BF16), _DIMS[mode], preferred_element_type=F32)


@functools.partial(jax.custom_vjp, nondiff_argnums=(2,))
def _mmx(a, b, mode):
    return _dot(a, b, mode)


def _mmx_fwd(a, b, mode):
    return _dot(a, b, mode), (a, b)


def _mmx_bwd(mode, res, g):
    a, b = res
    if mode == "nn":
        return _dot(g, b, "nt"), _dot(a, g, "tn")
    if mode == "nt":
        return _dot(g, b, "nn"), _dot(g, a, "tn")
    return _dot(b, g, "nt"), _dot(a, g, "nn")


_mmx.defvjp(_mmx_fwd, _mmx_bwd)


def _mm_f32(a, b):
    return lax.dot_general(a, b, _DIMS["nn"], precision=HI, preferred_element_type=F32)


def _mm(a, b):
    return _mmx(a, b, "nn")


def _mm_nt(a, b):
    return _mmx(a, b, "nt")


def _mm_tn(a, b):
    return _mmx(a, b, "tn")


@functools.partial(jax.custom_vjp, nondiff_argnums=(1,))
def _roll_rows(x, d):
    return pltpu.roll(x, d, 0)


def _roll_rows_fwd(x, d):
    return pltpu.roll(x, d, 0), None


def _roll_rows_bwd(d, _, g):
    return (pltpu.roll(g, g.shape[0] - d, 0),)


_roll_rows.defvjp(_roll_rows_fwd, _roll_rows_bwd)


def _sigmoid(x):
    return 1.0 / (1.0 + jnp.exp(-x))


def _silu(x):
    return x * _sigmoid(x)


def _softplus(x):
    return jnp.maximum(x, 0.0) + jnp.log(1.0 + jnp.exp(-jnp.abs(x)))


def _log_sigmoid(x):
    return jnp.minimum(x, 0.0) - jnp.log(1.0 + jnp.exp(-jnp.abs(x)))


def _logaddexp(a, b):
    return jnp.maximum(a, b) + jnp.log(1.0 + jnp.exp(-jnp.abs(a - b)))


def _row_to_col(row):
    n = row.shape[1]
    eye = _iota((n, n), 0) == _iota((n, n), 1)
    return jnp.sum(jnp.where(eye, jnp.broadcast_to(row, (n, n)), 0.0), axis=1, keepdims=True)


def _col_to_row(col):
    n = col.shape[0]
    eye = _iota((n, n), 0) == _iota((n, n), 1)
    return jnp.sum(jnp.where(eye, jnp.broadcast_to(col, (n, n)), 0.0), axis=0, keepdims=True)


def _pick_row(x, r):
    return jnp.sum(jnp.where(_iota(x.shape, 0) == r, x, 0.0), axis=0, keepdims=True)


def _pick_lane(x, l):
    return jnp.sum(jnp.where(_iota(x.shape, 1) == l, x, 0.0), axis=1, keepdims=True)


def _each(f, *lists):
    return [f(*t) for t in zip(*lists)]


def _unit_lower_inverse(Ls):
    n = Ls[0].shape[0]
    r, c = _iota((n, n), 0), _iota((n, n), 1)
    eye = jnp.where(r == c, 1.0, 0.0).astype(F32)
    Ld = _each(lambda L: jnp.where((r // SUB) == (c // SUB), L, 0.0), Ls)
    Lo = _each(lambda L, d: L - d, Ls, Ld)
    P = _each(lambda d: eye - d, Ld)
    Lp = Ld
    for _ in range(int(math.log2(SUB)) - 1):
        Lp = _each(lambda x: _mm(x, x), Lp)
        P = _each(lambda p, x: p + _mm(p, x), P, Lp)
    N = _each(_mm, P, Lo)
    N2 = _each(lambda x: _mm(x, x), N)
    X = _each(lambda x, x2: (eye - x) + _mm(eye - x, x2), N, N2)
    return _each(_mm, X, P)


def _shift_rows(x, halo, d):
    if d == 0:
        return x
    xr = _roll_rows(x, d)
    hr = _roll_rows(halo, d)
    hfull = jnp.concatenate([hr, jnp.zeros((x.shape[0] - HALO, x.shape[1]), F32)], axis=0)
    return jnp.where(_iota(x.shape, 0) >= d, xr, hfull)


def _causal_conv_chunk(x, halo, w):
    y = None
    for kk in range(CONV_K):
        t = _shift_rows(x, halo, CONV_K - 1 - kk) * _pick_row(w, kk)
        y = t if y is None else y + t
    return y


def _gdn_chunk(xq, xk, xv, hq, hk, hv, gate, ab, S, cwq, cwk, cwv, alog, dtb, onorm, *, heads):
    C = xq[0].shape[0]
    q = _each(lambda x, h, w: _silu(_causal_conv_chunk(x, h, w)), xq, hq, cwq)
    k = _each(lambda x, h, w: _silu(_causal_conv_chunk(x, h, w)), xk, hk, cwk)
    v = _each(lambda x, h, w: _silu(_causal_conv_chunk(x, h, w)), xv, hv, cwv)
    q = _each(lambda t: t * lax.rsqrt(jnp.sum(t * t, axis=1, keepdims=True) + EPS) * (HEAD_DIM ** -0.5), q)
    k = _each(lambda t: t * lax.rsqrt(jnp.sum(t * t, axis=1, keepdims=True) + EPS), k)
    beta = [_sigmoid(_pick_lane(ab, HEADS + h)) for h in heads]
    g = [-jnp.exp(_pick_lane(alog, h)) * _softplus(_pick_lane(ab, h) + _pick_lane(dtb, h)) for h in heads]
    r, c = _iota((C, C), 0), _iota((C, C), 1)
    gc = _each(lambda t: jnp.sum(jnp.where(c <= r, jnp.broadcast_to(_col_to_row(t), (C, C)), 0.0), axis=1,
                                 keepdims=True), g)
    gc_row = _each(lambda t: jnp.sum(jnp.where(r <= c, jnp.broadcast_to(t, (C, C)), 0.0), axis=0, keepdims=True), g)
    decay = _each(lambda a, b: jnp.exp(jnp.where(r >= c, a - b, NEG)), gc, gc_row)
    kb = _each(lambda a, b: a * b, k, beta)
    L = _each(lambda a, b, d: jnp.where(r > c, _mm_nt(a, b) * d, 0.0), kb, k, decay)
    A = _each(lambda a, b, d: jnp.where(r >= c, _mm_nt(a, b) * d, 0.0), q, k, decay)
    T = _unit_lower_inverse(L)
    egc = _each(jnp.exp, gc)
    u = _each(lambda t, a, b: _mm(t, a * b), T, v, beta)
    w = _each(lambda t, a, e: _mm(t, a * e), T, kb, egc)
    gl = _each(lambda t: _pick_row(t, C - 1), gc)
    v_new = _each(lambda a, b, s: a - _mm(b, s), u, w, S)
    o = _each(lambda a, e, s, m, vn: _mm(a * e, s) + _mm(m, vn), q, egc, S, A, v_new)
    S_next = _each(lambda s, l, a, t, vn: s * jnp.exp(l) + _mm_tn(a * jnp.exp(l - t), vn), S, gl, k, gc, v_new)
    o = _each(lambda t, gt: t * lax.rsqrt(jnp.mean(t * t, axis=1, keepdims=True) + EPS) * onorm * _silu(gt), o, gate)
    return o, S_next


def _hgrn_lower_bound(lbl, layer):
    e = jnp.exp(lbl - jnp.max(lbl, axis=0, keepdims=True))
    sm = e / jnp.sum(e, axis=0, keepdims=True)
    r = _iota(lbl.shape, 0)
    return jnp.sum(jnp.where((r >= 1) & (r <= layer), sm, 0.0), axis=0, keepdims=True)


_LEVELS = (1, 2, 4, 8, 16, 32)


def _prefix_matrix(n):
    r, c = _iota((n, n), 0), _iota((n, n), 1)
    parts = [jnp.where(c <= r, 1.0, 0.0)]
    for s in _LEVELS:
        parts.append(jnp.where(c < (r // (2 * s)) * (2 * s) + s, 1.0, 0.0))
    return jnp.concatenate(parts, axis=0).astype(F32)


def _prefix_sums_of(x):
    n = x.shape[0]
    y = lax.dot_general(_prefix_matrix(n), x, _DIMS["nn"], precision=HI, preferred_element_type=F32)
    return tuple(y[t * n:(t + 1) * n] for t in range(len(_LEVELS) + 1))


@jax.custom_vjp
def _prefix_sums(x):
    return _prefix_sums_of(x)


def _prefix_sums_fwd(x):
    return _prefix_sums_of(x), None


def _prefix_sums_bwd(_, gs):
    g = jnp.concatenate(gs, axis=0)
    return (lax.dot_general(_prefix_matrix(gs[0].shape[0]), g, _DIMS["tn"], precision=HI, preferred_element_type=F32),)


_prefix_sums.defvjp(_prefix_sums_fwd, _prefix_sums_bwd)


def _hgrn_chunk(qp, fp, v, S, lbl, *, layer):
    C = qp[0].shape[0]
    lb = _each(lambda l: _hgrn_lower_bound(l, layer), lbl)
    lf = _each(lambda l, f: _logaddexp(jnp.log(l), jnp.log(1.0 - l) + _log_sigmoid(f)), lb, fp)
    k = _each(lambda l, f: (1.0 - l) * _sigmoid(-f), lb, fp)
    q = _each(lambda x: _silu(x) * (HEAD_DIM ** -0.5), qp)
    r, c = _iota((C, C), 0), _iota((C, C), 1)
    row = _iota(qp[0].shape, 0)
    sums = _each(_prefix_sums, lf)
    gc = [t[0] for t in sums]
    a = _each(lambda x, y: jnp.where(r == c, _mm_nt(x, y), 0.0), q, k)
    for n, s in enumerate(_LEVELS):
        ref = [t[n + 1] for t in sums]
        upper = (row % (2 * s)) >= s
        same = (r // (2 * s)) == (c // (2 * s))
        q_s = _each(lambda x, g, m: x * jnp.exp(jnp.where(upper, g - m, NEG)), q, gc, ref)
        k_s = _each(lambda x, g, m: x * jnp.exp(jnp.where(upper, NEG, m - g)), k, gc, ref)
        a = _each(lambda t, x, y: t + jnp.where(same, _mm_nt(x, y), 0.0), a, q_s, k_s)
    o = _each(lambda t, x, g, vv, st: _mm(t, vv) + _mm(x * jnp.exp(g), st), a, q, gc, v, S)
    gl = _each(lambda g: _pick_row(g, C - 1), gc)
    S_next = _each(lambda st, l, x, g, vv: st * _row_to_col(jnp.exp(l)) + _mm_tn(x * jnp.exp(l - g), vv),
                   S, gl, k, gc, v)
    return o, S_next


N_CHIPS = 4
N_DEV = 8
_ANY = pl.BlockSpec(memory_space=pl.ANY)


def _mesh_pos():
    return lax.axis_index("x"), lax.axis_index("y"), lax.axis_index("c")


def _other_chips(x, y):
    ps = [(1 - x, y), (x, 1 - y), (1 - x, 1 - y)]
    return [(p, 2 * p[0] + p[1]) for p in ps]


def _remote(src, dst, send_sem, recv_sem, dev):
    return pltpu.make_async_remote_copy(src_ref=src, dst_ref=dst, send_sem=send_sem, recv_sem=recv_sem,
                                        device_id=dev, device_id_type=MESH)


class Carry:
    def __init__(self, ins, out_shapes, sems, start, finish):
        self.ins, self.out_shapes, self.sems, self.start, self.finish = list(ins), list(out_shapes), list(sems), start, finish


def _pcall(body, *, name, grid, in_specs, out_specs, out_shape, scratch_shapes=(), dims, args, carries=()):
    in_specs, out_specs, out_shape = list(in_specs), list(out_specs), list(out_shape)
    scratch_shapes, args = list(scratch_shapes), list(args)
    n_in, n_out, n_scr = len(in_specs), len(out_shape), len(scratch_shapes)
    carries = [c for c in carries if c is not None]
    if not carries:
        res = pl.pallas_call(body, name=name, grid=grid, in_specs=in_specs, out_specs=out_specs, out_shape=out_shape,
                             scratch_shapes=scratch_shapes, compiler_params=_cparams(dims))(*args)
        return list(res), []
    ci = [len(c.ins) for c in carries]
    co = [len(c.out_shapes) for c in carries]
    cs = [len(c.sems) for c in carries]

    def split(seq, sizes):
        out, k = [], 0
        for s in sizes:
            out.append(seq[k:k + s])
            k += s
        return out

    def carried(*refs):
        ins, cins, outs, couts, scr, sems = split(refs, [n_in, sum(ci), n_out, sum(co), n_scr, sum(cs)])
        cins, couts, sems = split(cins, ci), split(couts, co), split(sems, cs)
        ids = [pl.program_id(a) for a in range(len(grid))]
        first, last = ids[0] == 0, ids[0] == grid[0] - 1
        for a in range(1, len(grid)):
            first, last = first & (ids[a] == 0), last & (ids[a] == grid[a] - 1)

        @pl.when(first)
        def _():
            for c, i, o, s in zip(carries, cins, couts, sems):
                c.start(i, o, s)

        body(*ins, *outs, *scr)

        @pl.when(last)
        def _():
            for c, i, o, s in zip(carries, cins, couts, sems):
                c.finish(i, o, s)

    res = pl.pallas_call(
        carried, name=name, grid=grid,
        in_specs=in_specs + [_ANY] * sum(ci), out_specs=out_specs + [_ANY] * sum(co),
        out_shape=out_shape + [s for c in carries for s in c.out_shapes],
        scratch_shapes=scratch_shapes + [s for c in carries for s in c.sems],
        compiler_params=_cparams(("arbitrary",) * len(grid)),
    )(*args, *[a for c in carries for a in c.ins])
    return list(res[:n_out]), split(list(res[n_out:]), co)


def run_carries(carries, name):
    ci = [len(c.ins) for c in carries]
    co = [len(c.out_shapes) for c in carries]
    cs = [len(c.sems) for c in carries]

    def split(seq, sizes):
        out, k = [], 0
        for s in sizes:
            out.append(seq[k:k + s])
            k += s
        return out

    def body(*refs):
        cins, couts, sems = split(refs, [sum(ci), sum(co), sum(cs)])
        cins, couts, sems = split(cins, ci), split(couts, co), split(sems, cs)
        for c, i, o, s in zip(carries, cins, couts, sems):
            c.start(i, o, s)
        for c, i, o, s in zip(carries, cins, couts, sems):
            c.finish(i, o, s)

    res = pl.pallas_call(
        body, name=name, in_specs=[_ANY] * sum(ci), out_specs=[_ANY] * sum(co),
        out_shape=[s for c in carries for s in c.out_shapes], scratch_shapes=[s for c in carries for s in c.sems],
    )(*[a for c in carries for a in c.ins])
    return split(list(res), co)


def gather_carry(arrs):
    n = len(arrs)

    def copies(ins, outs, sems):
        send_sems, recv_sems, local_sems = sems
        x, y, c = _mesh_pos()
        me = 2 * x + y
        peers = _other_chips(x, y)
        locs = [pltpu.make_async_copy(ins[a], outs[a].at[me], local_sems.at[a]) for a in range(n)]
        sends = [_remote(ins[a], outs[a].at[me], send_sems.at[a, j], recv_sems.at[a, j], (px, py, c))
                 for a in range(n) for j, ((px, py), _) in enumerate(peers)]
        recvs = [_remote(ins[a], outs[a].at[t], send_sems.at[a, j], recv_sems.at[a, j], (px, py, c))
                 for a in range(n) for j, ((px, py), t) in enumerate(peers)]
        return locs, sends, recvs

    def start(ins, outs, sems):
        locs, sends, _ = copies(ins, outs, sems)
        for cp in locs + sends:
            cp.start()

    def finish(ins, outs, sems):
        locs, sends, recvs = copies(ins, outs, sems)
        for r in recvs:
            r.wait_recv()
        for r in sends:
            r.wait_send()
        for cp in locs:
            cp.wait()

    return Carry(arrs, [jax.ShapeDtypeStruct((N_CHIPS,) + a.shape, a.dtype) for a in arrs],
                 [pltpu.SemaphoreType.DMA((n, 3)), pltpu.SemaphoreType.DMA((n, 3)), pltpu.SemaphoreType.DMA((n,))],
                 start, finish)


def sibling_half_carry(gs):
    n = len(gs)

    def copies(ins, outs, sems):
        send_sems, recv_sems = sems
        x, y, c = _mesh_pos()
        out = []
        for a in range(n):
            half = gs[a].shape[1] // 2
            out.append(_remote(ins[a].at[:, pl.ds((1 - c) * half, half), :], outs[a], send_sems.at[a], recv_sems.at[a],
                               (x, y, 1 - c)))
        return out

    def start(ins, outs, sems):
        for r in copies(ins, outs, sems):
            r.start()

    def finish(ins, outs, sems):
        cps = copies(ins, outs, sems)
        for r in cps:
            r.wait_recv()
        for r in cps:
            r.wait_send()

    return Carry(gs, [jax.ShapeDtypeStruct((g.shape[0], g.shape[1] // 2, g.shape[2]), g.dtype) for g in gs],
                 [pltpu.SemaphoreType.DMA((n,)), pltpu.SemaphoreType.DMA((n,))], start, finish)


def chips_carry(ps):
    n = len(ps)

    def copies(ins, outs, sems):
        send_sems, recv_sems, local_sems = sems
        x, y, c = _mesh_pos()
        me = 2 * x + y
        peers = _other_chips(x, y)
        locs = [pltpu.make_async_copy(ins[a].at[me], outs[a].at[me], local_sems.at[a]) for a in range(n)]
        sends = [_remote(ins[a].at[t], outs[a].at[me], send_sems.at[a, j], recv_sems.at[a, j], (px, py, c))
                 for a in range(n) for j, ((px, py), t) in enumerate(peers)]
        recvs = [_remote(ins[a].at[t], outs[a].at[t], send_sems.at[a, j], recv_sems.at[a, j], (px, py, c))
                 for a in range(n) for j, ((px, py), t) in enumerate(peers)]
        return locs, sends, recvs

    def start(ins, outs, sems):
        locs, sends, _ = copies(ins, outs, sems)
        for cp in locs + sends:
            cp.start()

    def finish(ins, outs, sems):
        locs, sends, recvs = copies(ins, outs, sems)
        for r in recvs:
            r.wait_recv()
        for r in sends:
            r.wait_send()
        for cp in locs:
            cp.wait()

    return Carry(ps, [jax.ShapeDtypeStruct(p.shape, p.dtype) for p in ps],
                 [pltpu.SemaphoreType.DMA((n, 3)), pltpu.SemaphoreType.DMA((n, 3)), pltpu.SemaphoreType.DMA((n,))],
                 start, finish)


def sibling_copy_carry(arrs):
    n = len(arrs)

    def copies(ins, outs, sems):
        send_sems, recv_sems = sems
        x, y, c = _mesh_pos()
        return [_remote(ins[a], outs[a], send_sems.at[a], recv_sems.at[a], (x, y, 1 - c)) for a in range(n)]

    def start(ins, outs, sems):
        for r in copies(ins, outs, sems):
            r.start()

    def finish(ins, outs, sems):
        cps = copies(ins, outs, sems)
        for r in cps:
            r.wait_recv()
        for r in cps:
            r.wait_send()

    return Carry(arrs, [jax.ShapeDtypeStruct(a.shape, a.dtype) for a in arrs],
                 [pltpu.SemaphoreType.DMA((n,)), pltpu.SemaphoreType.DMA((n,))], start, finish)


def gather_all_carry(small):
    flips = [(fx, fy, fc) for fx in (0, 1) for fy in (0, 1) for fc in (0, 1)][1:]

    def copies(ins, outs, sems):
        send_sems, recv_sems, local_sem = sems
        in_ref, out_ref = ins[0], outs[0]
        x, y, c = _mesh_pos()
        me = 4 * x + 2 * y + c
        peers = [((1 - x) if fx else x, (1 - y) if fy else y, (1 - c) if fc else c) for fx, fy, fc in flips]
        loc = pltpu.make_async_copy(in_ref, out_ref.at[me], local_sem.at[0])
        sends = [_remote(in_ref, out_ref.at[me], send_sems.at[k], recv_sems.at[k], p) for k, p in enumerate(peers)]
        recvs = [_remote(in_ref, out_ref.at[4 * p[0] + 2 * p[1] + p[2]], send_sems.at[k], recv_sems.at[k], p)
                 for k, p in enumerate(peers)]
        return loc, sends, recvs

    def start(ins, outs, sems):
        loc, sends, _ = copies(ins, outs, sems)
        loc.start()
        for r in sends:
            r.start()

    def finish(ins, outs, sems):
        loc, sends, recvs = copies(ins, outs, sems)
        for r in recvs:
            r.wait_recv()
        for r in sends:
            r.wait_send()
        loc.wait()

    return Carry([small], [jax.ShapeDtypeStruct((N_DEV,) + small.shape, small.dtype)],
                 [pltpu.SemaphoreType.DMA((N_DEV - 1,)), pltpu.SemaphoreType.DMA((N_DEV - 1,)),
                  pltpu.SemaphoreType.DMA((1,))], start, finish)


HB = 8
HBW = HB * HEAD_DIM


def _chunk_specs(nc, col0, rev=False):
    return pl.BlockSpec((CHUNK, HBW), lambda b, n, g: (b * nc + ((nc - 1 - n) if rev else n), col0 // HB + g))


def _chunk_specs_rev(nc, col0):
    return _chunk_specs(nc, col0, True)


def _halo_spec(nc, col0, rev):
    per = CHUNK // HALO

    def imap(b, n, g):
        nn = (nc - 1 - n) if rev else n
        return (jnp.maximum((b * nc + nn) * per - 1, 0), col0 // HB + g)
    return pl.BlockSpec((HALO, HBW), imap)


def _const_spec(shape):
    nd = len(shape)
    return pl.BlockSpec(shape, lambda b, n, g: (0,) * nd)


def _state_spec(nc, rev=False):
    return pl.BlockSpec((None, None, HB, HEAD_DIM, HEAD_DIM),
                        lambda b, n, g: (b, (nc - 1 - n) if rev else n, g, 0, 0))


def _lanes(hh):
    return slice(hh * HEAD_DIM, (hh + 1) * HEAD_DIM)


def _head(g, hh):
    return hh if HB == HEADS else g * HB + hh


def gdn_forward(proj, conv_w, alog, dtb, onorm, bsz, name, carries=()):
    n_tok = proj.shape[0]
    nc = n_tok // bsz // CHUNK

    def body(xq, xk, xv, hq, hk, hv, gate, ab, cwq, cwk, cwv, al, dt, on, o_ref, ssave_ref, s_ref):
        n, g = pl.program_id(1), pl.program_id(2)
        keep = jnp.where(n > 0, 1.0, 0.0).astype(F32)

        @pl.when(n == 0)
        def _():
            for hh in range(HB):
                s_ref[_head(g, hh)] = jnp.zeros((HEAD_DIM, HEAD_DIM), F32)

        heads = [_head(g, hh) for hh in range(HB)]
        per_head = lambda ref, scale=None: [ref[:, _lanes(hh)] if scale is None else ref[:, _lanes(hh)] * scale
                                            for hh in range(HB)]
        S = [s_ref[h] for h in heads]
        for hh in range(HB):
            ssave_ref[hh] = S[hh]
        outs, s_next = _gdn_chunk(per_head(xq), per_head(xk), per_head(xv), per_head(hq, keep), per_head(hk, keep),
                                  per_head(hv, keep), per_head(gate), ab[...], S, per_head(cwq), per_head(cwk),
                                  per_head(cwv), al[...], dt[...], on[...], heads=heads)
        for hh in range(HB):
            o_ref[:, _lanes(hh)] = outs[hh].astype(o_ref.dtype)
            s_ref[heads[hh]] = s_next[hh]

    hb = HEADS
    cw_spec = lambda col0: pl.BlockSpec((CONV_K, HBW), lambda b, n, g: (0, col0 // HB + g))
    in_specs = [_chunk_specs(nc, 0), _chunk_specs(nc, hb), _chunk_specs(nc, 2 * hb),
                _halo_spec(nc, 0, False), _halo_spec(nc, hb, False), _halo_spec(nc, 2 * hb, False),
                _chunk_specs(nc, 3 * hb),
                pl.BlockSpec((CHUNK, 128), lambda b, n, g: (b * nc + n, 4 * hb)),
                cw_spec(0), cw_spec(hb), cw_spec(2 * hb),
                _const_spec((1, 128)), _const_spec((1, 128)), _const_spec((1, 128))]
    out_specs = [_chunk_specs(nc, 0), _state_spec(nc)]
    return _pcall(
        body, name=name, grid=(bsz, nc, HEADS // HB), in_specs=in_specs, out_specs=out_specs,
        out_shape=(jax.ShapeDtypeStruct((n_tok, D_MODEL), BF16),
                   jax.ShapeDtypeStruct((bsz, nc, HEADS, HEAD_DIM, HEAD_DIM), F32)),
        scratch_shapes=[pltpu.VMEM((HEADS, HEAD_DIM, HEAD_DIM), F32)],
        dims=("arbitrary", "arbitrary", "arbitrary"),
        args=(proj, proj, proj, proj, proj, proj, proj, proj, conv_w, conv_w, conv_w, alog, dtb, onorm),
        carries=carries)


def gdn_backward(proj, conv_w, alog, dtb, onorm, s_saved, d_out, bsz, name, carries=()):
    n_tok = proj.shape[0]
    nc = n_tok // bsz // CHUNK
    assert HB == HEADS

    def body(xq, xk, xv, hq, hk, hv, gate, ab, cwq, cwk, cwv, al, dt, on, ssave, do,
             dp_ref, dcw_ref, dal_ref, ddt_ref, don_ref, ds_ref, dhalo_ref):
        b, n, g = pl.program_id(0), pl.program_id(1), pl.program_id(2)
        nr = nc - 1 - n

        @pl.when((b == 0) & (n == 0) & (g == 0))
        def _():
            dcw_ref[...] = jnp.zeros_like(dcw_ref)
            dal_ref[...] = jnp.zeros_like(dal_ref)
            ddt_ref[...] = jnp.zeros_like(ddt_ref)
            don_ref[...] = jnp.zeros_like(don_ref)

        @pl.when(n == 0)
        def _():
            for hh in range(HB):
                ds_ref[_head(g, hh)] = jnp.zeros((HEAD_DIM, HEAD_DIM), F32)
                dhalo_ref[_head(g, hh)] = jnp.zeros((3, HALO, HEAD_DIM), F32)

        keep = jnp.where(nr > 0, 1.0, 0.0).astype(F32)
        pad = jnp.zeros((CHUNK - HALO, HEAD_DIM), F32)
        heads = [_head(g, hh) for hh in range(HB)]
        per_head = lambda ref, scale=None: [ref[:, _lanes(hh)] if scale is None else ref[:, _lanes(hh)] * scale
                                            for hh in range(HB)]
        args = (per_head(xq), per_head(xk), per_head(xv), per_head(hq, keep), per_head(hk, keep), per_head(hv, keep),
                per_head(gate), ab[...], [ssave[hh] for hh in range(HB)], per_head(cwq), per_head(cwk), per_head(cwv),
                al[...], dt[...], on[...])
        _, vjp = jax.vjp(functools.partial(_gdn_chunk, heads=heads), *args)
        (gxq, gxk, gxv, ghq, ghk, ghv, ggate, gab_sum, gS, gcq, gck, gcv, gal_sum, gdt_sum, gon_sum) = vjp(
            ([do[:, _lanes(hh)] for hh in range(HB)], [ds_ref[h] for h in heads]))
        d = D_MODEL
        for hh in range(HB):
            head, c0 = heads[hh], hh * HEAD_DIM
            for part, gx in enumerate((gxq, gxk, gxv)):
                full = gx[hh] + jnp.concatenate([pad, dhalo_ref[head, part]], axis=0)
                dp_ref[:, part * d + c0:part * d + c0 + HEAD_DIM] = full.astype(dp_ref.dtype)
            dp_ref[:, 3 * d + c0:3 * d + c0 + HEAD_DIM] = ggate[hh].astype(dp_ref.dtype)
            dcw_ref[head, 0] += gcq[hh]
            dcw_ref[head, 1] += gck[hh]
            dcw_ref[head, 2] += gcv[hh]
            ds_ref[head] = gS[hh]
            dhalo_ref[head, 0] = ghq[hh] * keep
            dhalo_ref[head, 1] = ghk[hh] * keep
            dhalo_ref[head, 2] = ghv[hh] * keep

        dp_ref[:, GDN_MAIN:GDN_IN_PAD] = gab_sum.astype(dp_ref.dtype)
        dal_ref[...] += gal_sum
        ddt_ref[...] += gdt_sum
        don_ref[...] += gon_sum

    hb = HEADS
    cw_spec = lambda col0: pl.BlockSpec((CONV_K, HBW), lambda b, n, g: (0, col0 // HB + g))
    in_specs = [_chunk_specs_rev(nc, 0), _chunk_specs_rev(nc, hb), _chunk_specs_rev(nc, 2 * hb),
                _halo_spec(nc, 0, True), _halo_spec(nc, hb, True), _halo_spec(nc, 2 * hb, True),
                _chunk_specs_rev(nc, 3 * hb),
                pl.BlockSpec((CHUNK, 128), lambda b, n, g: (b * nc + (nc - 1 - n), 4 * hb)),
                cw_spec(0), cw_spec(hb), cw_spec(2 * hb),
                _const_spec((1, 128)), _const_spec((1, 128)), _const_spec((1, 128)),
                _state_spec(nc, True),
                _chunk_specs_rev(nc, 0)]
    out_specs = [
        pl.BlockSpec((CHUNK, GDN_IN_PAD), lambda b, n, g: (b * nc + (nc - 1 - n), 0)),
        _const_spec((HEADS, 3, CONV_K, HEAD_DIM)), _const_spec((1, 128)), _const_spec((1, 128)), _const_spec((1, 128))]
    row = jax.ShapeDtypeStruct((1, 128), F32)
    return _pcall(
        body, name=name, grid=(bsz, nc, HEADS // HB), in_specs=in_specs, out_specs=out_specs,
        out_shape=(jax.ShapeDtypeStruct((n_tok, GDN_IN_PAD), BF16),
                   jax.ShapeDtypeStruct((HEADS, 3, CONV_K, HEAD_DIM), F32), row, row, row),
        scratch_shapes=[pltpu.VMEM((HEADS, HEAD_DIM, HEAD_DIM), F32), pltpu.VMEM((HEADS, 3, HALO, HEAD_DIM), F32)],
        dims=("arbitrary", "arbitrary", "arbitrary"),
        args=(proj, proj, proj, proj, proj, proj, proj, proj, conv_w, conv_w, conv_w, alog, dtb, onorm, s_saved, d_out),
        carries=carries)


def hgrn_forward(proj, lbl, layer, bsz, carries=()):
    n_tok = proj.shape[0]
    nc = n_tok // bsz // CHUNK

    def body(qp, fp, vi, lb, o_ref, ssave_ref, s_ref):
        n, g = pl.program_id(1), pl.program_id(2)

        @pl.when(n == 0)
        def _():
            for hh in range(HB):
                s_ref[_head(g, hh)] = jnp.zeros((HEAD_DIM, HEAD_DIM), F32)

        heads = [_head(g, hh) for hh in range(HB)]
        per_head = lambda ref: [ref[:, _lanes(hh)] for hh in range(HB)]
        S = [s_ref[h] for h in heads]
        for hh in range(HB):
            ssave_ref[hh] = S[hh]
        outs, s_next = _hgrn_chunk(per_head(qp), per_head(fp), per_head(vi), S, per_head(lb), layer=layer)
        for hh in range(HB):
            o_ref[:, _lanes(hh)] = outs[hh]
            s_ref[heads[hh]] = s_next[hh]

    hb = HEADS
    in_specs = [_chunk_specs(nc, 0), _chunk_specs(nc, hb), _chunk_specs(nc, 2 * hb),
                pl.BlockSpec((DEPTH, HBW), lambda b, n, g: (0, g))]
    out_specs = [_chunk_specs(nc, 0), _state_spec(nc)]
    return _pcall(
        body, name=f"hgrn_fwd{layer}", grid=(bsz, nc, HEADS // HB), in_specs=in_specs, out_specs=out_specs,
        out_shape=(jax.ShapeDtypeStruct((n_tok, D_MODEL), F32),
                   jax.ShapeDtypeStruct((bsz, nc, HEADS, HEAD_DIM, HEAD_DIM), F32)),
        scratch_shapes=[pltpu.VMEM((HEADS, HEAD_DIM, HEAD_DIM), F32)],
        dims=("arbitrary", "arbitrary", "arbitrary"), args=(proj, proj, proj, lbl), carries=carries)


def hgrn_backward(proj, lbl, s_saved, d_o, d_gate, layer, bsz, carries=()):
    n_tok = proj.shape[0]
    nc = n_tok // bsz // CHUNK
    assert HB == HEADS

    def body(qp, fp, vi, lb, ssave, do, dgt, dp_ref, dlb_ref, ds_ref):
        b, n, g = pl.program_id(0), pl.program_id(1), pl.program_id(2)

        @pl.when((b == 0) & (n == 0) & (g == 0))
        def _():
            dlb_ref[...] = jnp.zeros_like(dlb_ref)

        @pl.when(n == 0)
        def _():
            for hh in range(HB):
                ds_ref[_head(g, hh)] = jnp.zeros((HEAD_DIM, HEAD_DIM), F32)

        heads = [_head(g, hh) for hh in range(HB)]
        per_head = lambda ref: [ref[:, _lanes(hh)] for hh in range(HB)]
        _, vjp = jax.vjp(functools.partial(_hgrn_chunk, layer=layer), per_head(qp), per_head(fp), per_head(vi),
                         [ssave[hh] for hh in range(HB)], per_head(lb))
        gq, gf, gv, gS, glb = vjp((per_head(do), [ds_ref[h] for h in heads]))
        for hh in range(HB):
            head, c0, d = heads[hh], hh * HEAD_DIM, D_MODEL
            dp_ref[:, c0:c0 + HEAD_DIM] = gq[hh].astype(dp_ref.dtype)
            dp_ref[:, d + c0:d + c0 + HEAD_DIM] = gf[hh].astype(dp_ref.dtype)
            dp_ref[:, 2 * d + c0:2 * d + c0 + HEAD_DIM] = gv[hh].astype(dp_ref.dtype)
            dlb_ref[head] += glb[hh]
            ds_ref[head] = gS[hh]
        dp_ref[:, 3 * D_MODEL:4 * D_MODEL] = dgt[...]

    hb = HEADS
    in_specs = [_chunk_specs_rev(nc, 0), _chunk_specs_rev(nc, hb), _chunk_specs_rev(nc, 2 * hb),
                pl.BlockSpec((DEPTH, HBW), lambda b, n, g: (0, g)),
                _state_spec(nc, True),
                _chunk_specs_rev(nc, 0), _chunk_specs_rev(nc, 0)]
    out_specs = [pl.BlockSpec((CHUNK, 4 * D_MODEL), lambda b, n, g: (b * nc + (nc - 1 - n), 0)),
                 _const_spec((HEADS, DEPTH, HEAD_DIM))]
    return _pcall(
        body, name=f"hgrn_bwd{layer}", grid=(bsz, nc, HEADS // HB), in_specs=in_specs, out_specs=out_specs,
        out_shape=(jax.ShapeDtypeStruct((n_tok, 4 * D_MODEL), BF16), jax.ShapeDtypeStruct((HEADS, DEPTH, HEAD_DIM), F32)),
        scratch_shapes=[pltpu.VMEM((HEADS, HEAD_DIM, HEAD_DIM), F32)],
        dims=("arbitrary", "arbitrary", "arbitrary"), args=(proj, proj, proj, lbl, s_saved, d_o, d_gate),
        carries=carries)


ROW_TILE = 512
MM_ROW_TILE = 1024
MM_VMEM_BUDGET = 36 * 1024 * 1024


def _tile(n):
    for cand in (1024, 512, 1408, 384, 256, 128):
        if n % cand == 0:
            return cand
    return n


def _rmsnorm(x, w):
    return x * lax.rsqrt(jnp.mean(x * x, axis=1, keepdims=True) + EPS) * w


def norm_matmul(h, nw, w, relu2, name, carries=()):
    n_tok, d = h.shape
    slots = w.ndim == 3
    n_out = w.shape[0] * w.shape[2] if slots else w.shape[1]
    tm, tn = min(n_tok, MM_ROW_TILE), _tile(w.shape[2] if slots else n_out)
    if slots:
        per = w.shape[2] // tn
        w_spec = pl.BlockSpec((None, d, tn), lambda i, j: (j // per, 0, j % per))
    else:
        w_spec = pl.BlockSpec((d, tn), lambda i, j: (0, j))

    def body(h_ref, nw_ref, w_ref, y_ref, *outs):
        @pl.when(pl.program_id(1) == 0)
        def _():
            y_ref[...] = _rmsnorm(h_ref[...], nw_ref[...]).astype(BF16)

        acc = jnp.dot(y_ref[...], w_ref[...], preferred_element_type=F32)
        if relu2:
            r = jnp.maximum(acc, 0.0)
            outs[0][...] = (r * r).astype(BF16)
            outs[1][...] = r.astype(BF16)
        else:
            outs[0][...] = acc

    o_spec = pl.BlockSpec((tm, tn), lambda i, j: (i, j))
    if relu2:
        outs = (jax.ShapeDtypeStruct((n_tok, n_out), BF16),) * 2
    else:
        outs = (jax.ShapeDtypeStruct((n_tok, n_out), F32),)
    return _pcall(
        body, name=name, grid=(n_tok // tm, n_out // tn),
        in_specs=[pl.BlockSpec((tm, d), lambda i, j: (i, 0)), pl.BlockSpec((1, d), lambda i, j: (0, 0)), w_spec],
        out_specs=[pl.BlockSpec((tm, d), lambda i, j: (i, 0))] + [o_spec] * len(outs),
        out_shape=(jax.ShapeDtypeStruct((n_tok, d), BF16),) + outs,
        dims=("parallel", "arbitrary"), args=(h, nw, w), carries=carries)


def _mm_tiles(m, n, k, extra_bytes):
    tn = _tile(n)
    for tm in (1024, 512, 256, 128):
        if m % tm == 0 and 2 * (2 * tm * k + 2 * k * tn + (4 + extra_bytes) * tm * tn) <= MM_VMEM_BUDGET:
            return tm, tn
    return min(m, 128), tn


def matmul(a, b, mode, name, out_dtype=F32, extra=None, epilogue=None, shards=1, carries=()):
    slots = b.shape[0] if (mode == "nt" and b.ndim == 3) else 0
    if mode == "nn":
        (m, k), n = a.shape, b.shape[1]
    elif mode == "nt":
        (m, k), n = a.shape, (b.shape[1] if slots else b.shape[0])
    else:
        (k, m), n = a.shape, b.shape[1]
    tm, tn = _mm_tiles(m, n // shards, k, 0 if extra is None else extra.dtype.itemsize)
    a_spec = pl.BlockSpec((k, tm), lambda i, j: (0, i)) if mode == "tn" else pl.BlockSpec((tm, k), lambda i, j: (i, 0))
    if slots:
        kb = b.shape[2]
        b_specs = [pl.BlockSpec((None, tn, kb), lambda i, j, s=s: (s, j, 0)) for s in range(slots)]
    elif mode == "nt":
        b_specs = [pl.BlockSpec((tn, k), lambda i, j: (j, 0))]
    else:
        b_specs = [pl.BlockSpec((k, tn), lambda i, j: (0, j))]
    nb = len(b_specs)
    dims = _DIMS[mode]

    def body(*refs):
        a_ref, b_refs = refs[0], refs[1:1 + nb]
        e_ref = refs[1 + nb] if extra is not None else None
        o_ref = refs[-1]
        if slots:
            acc = None
            for s in range(slots):
                part = lax.dot_general(a_ref[:, s * kb:(s + 1) * kb].astype(BF16), b_refs[s][...].astype(BF16), dims,
                                       preferred_element_type=F32)
                acc = part if acc is None else acc + part
        else:
            acc = lax.dot_general(a_ref[...].astype(BF16), b_refs[0][...].astype(BF16), dims,
                                  preferred_element_type=F32)
        if epilogue == "add":
            acc = e_ref[...] + acc
        elif epilogue == "mul2":
            acc = acc * (2.0 * e_ref[...].astype(F32))
        o_ref[...] = acc.astype(o_ref.dtype)

    in_specs = [a_spec] + b_specs
    args = [a] + [b] * nb
    if extra is not None:
        in_specs.append(pl.BlockSpec((tm, tn), lambda i, j: (i, j)))
        args.append(extra)
    if shards > 1:
        per = n // shards // tn
        out_spec = pl.BlockSpec((None, tm, tn), lambda i, j: (j // per, i, j % per))
        out_shape = jax.ShapeDtypeStruct((shards, m, n // shards), out_dtype)
    else:
        out_spec = pl.BlockSpec((tm, tn), lambda i, j: (i, j))
        out_shape = jax.ShapeDtypeStruct((m, n), out_dtype)
    (res,), cres = _pcall(
        body, name=name, grid=(m // tm, n // tn), in_specs=in_specs, out_specs=[out_spec], out_shape=[out_shape],
        dims=("parallel", "arbitrary"), args=args, carries=carries)
    return res, cres


def norm_backward(h, nw, dy, dres, name):
    n_tok, d = h.shape
    tm = min(n_tok, ROW_TILE)

    def body(h_ref, nw_ref, dy_ref, dr_ref, dh_ref, dhb_ref, dnw_ref):
        @pl.when(pl.program_id(0) == 0)
        def _():
            dnw_ref[...] = jnp.zeros_like(dnw_ref)

        _, vjp = jax.vjp(_rmsnorm, h_ref[...], nw_ref[...])
        gh, gw = vjp(dy_ref[...])
        dh = dr_ref[...] + gh
        dh_ref[...] = dh
        dhb_ref[...] = dh.astype(BF16)
        dnw_ref[...] += gw

    row = pl.BlockSpec((tm, d), lambda i: (i, 0))
    vec = pl.BlockSpec((1, d), lambda i: (0, 0))
    return pl.pallas_call(
        body, name=name, grid=(n_tok // tm,), in_specs=[row, vec, row, row], out_specs=[row, row, vec],
        out_shape=(jax.ShapeDtypeStruct((n_tok, d), F32), jax.ShapeDtypeStruct((n_tok, d), BF16),
                   jax.ShapeDtypeStruct((1, d), F32)),
        compiler_params=_cparams(("arbitrary",)),
    )(h, nw, dy, dres)


def _hgrn_post(o, gate, gw):
    return _rmsnorm(o, gw) * _silu(gate)


def hgrn_post_forward(o, proj, gw, name):
    n_tok, d = o.shape
    tm = min(n_tok, ROW_TILE)

    def body(o_ref, g_ref, w_ref, y_ref):
        y_ref[...] = _hgrn_post(o_ref[...], g_ref[...], w_ref[...]).astype(BF16)

    row = pl.BlockSpec((tm, d), lambda i: (i, 0))
    return pl.pallas_call(
        body, name=name, grid=(n_tok // tm,),
        in_specs=[row, pl.BlockSpec((tm, d), lambda i: (i, 3)), pl.BlockSpec((1, d), lambda i: (0, 0))],
        out_specs=row, out_shape=jax.ShapeDtypeStruct((n_tok, d), BF16),
        compiler_params=_cparams(("parallel",)),
    )(o, proj, gw)


def hgrn_post_backward(o, proj, gw, dy, name):
    n_tok, d = o.shape
    tm = min(n_tok, ROW_TILE)

    def body(o_ref, g_ref, w_ref, dy_ref, do_ref, dg_ref, dw_ref):
        @pl.when(pl.program_id(0) == 0)
        def _():
            dw_ref[...] = jnp.zeros_like(dw_ref)

        _, vjp = jax.vjp(_hgrn_post, o_ref[...], g_ref[...], w_ref[...])
        go, gg, gw_ = vjp(dy_ref[...])
        do_ref[...] = go
        dg_ref[...] = gg.astype(BF16)
        dw_ref[...] += gw_

    row = pl.BlockSpec((tm, d), lambda i: (i, 0))
    vec = pl.BlockSpec((1, d), lambda i: (0, 0))
    return pl.pallas_call(
        body, name=name, grid=(n_tok // tm,),
        in_specs=[row, pl.BlockSpec((tm, d), lambda i: (i, 3)), vec, row], out_specs=[row, row, vec],
        out_shape=(jax.ShapeDtypeStruct((n_tok, d), F32), jax.ShapeDtypeStruct((n_tok, d), BF16),
                   jax.ShapeDtypeStruct((1, d), F32)),
        compiler_params=_cparams(("arbitrary",)),
    )(o, proj, gw, dy)


def loss_head(h, nw, target):
    n_tok, d = h.shape
    tm = min(n_tok, ROW_TILE)

    def body(h_ref, nw_ref, t_ref, loss_ref, dh_ref, dhb_ref, dnw_ref):
        @pl.when(pl.program_id(0) == 0)
        def _():
            dnw_ref[...] = jnp.zeros_like(dnw_ref)
            loss_ref[...] = jnp.zeros_like(loss_ref)

        out, vjp = jax.vjp(_rmsnorm, h_ref[...], nw_ref[...])
        err = out - t_ref[...]
        part = 0.5 * jnp.sum(jnp.sum(err * err, axis=1, keepdims=True), axis=0, keepdims=True) / d
        loss_ref[...] += jnp.broadcast_to(part, loss_ref.shape)
        gh, gw = vjp(err / d)
        dh_ref[...] = gh
        dhb_ref[...] = gh.astype(BF16)
        dnw_ref[...] += gw

    row = pl.BlockSpec((tm, d), lambda i: (i, 0))
    vec = pl.BlockSpec((1, d), lambda i: (0, 0))
    return pl.pallas_call(
        body, name="loss_head", grid=(n_tok // tm,), in_specs=[row, vec, row],
        out_specs=[pl.BlockSpec((1, 128), lambda i: (0, 0)), row, row, vec],
        out_shape=(jax.ShapeDtypeStruct((1, 128), F32), jax.ShapeDtypeStruct((n_tok, d), F32),
                   jax.ShapeDtypeStruct((n_tok, d), BF16), jax.ShapeDtypeStruct((1, d), F32)),
        compiler_params=_cparams(("arbitrary",)),
    )(h, nw, target)


def _rows2d(shape):
    if len(shape) == 1:
        return (1, shape[0])
    return (math.prod(shape[:-1]), shape[-1])


def adamw(w, g, m, v, name):
    shape = w.shape
    r, c = _rows2d(shape)
    tr = r if r <= 256 else 256
    c1 = 1.0 / (1.0 - ADAM_B1 ** ADAM_STEP)
    c2 = 1.0 / (1.0 - ADAM_B2 ** ADAM_STEP)

    def body(w_ref, g_ref, m_ref, v_ref, d_ref, nm_ref, nv_ref):
        gg = g_ref[...]
        nm = ADAM_B1 * m_ref[...] + (1.0 - ADAM_B1) * gg
        nv = ADAM_B2 * v_ref[...] + (1.0 - ADAM_B2) * (gg * gg)
        d_ref[...] = -ADAM_LR * ((nm * c1) / (jnp.sqrt(nv * c2) + ADAM_EPS) + ADAM_WD * w_ref[...])
        nm_ref[...] = nm
        nv_ref[...] = nv

    spec = pl.BlockSpec((tr, c), lambda i: (i, 0))
    sds = jax.ShapeDtypeStruct((r, c), F32)
    outs = pl.pallas_call(
        body, name=name, grid=(r // tr,), in_specs=[spec] * 4, out_specs=[spec] * 3, out_shape=(sds,) * 3,
        compiler_params=_cparams(("parallel",)),
    )(w.reshape(r, c), g.reshape(r, c), m.reshape(r, c), v.reshape(r, c))
    return tuple(o.reshape(shape) for o in outs)


def add_slots(parts, name):
    s, r, c = parts.shape
    tr = r if r <= 256 else 256

    def body(p_ref, o_ref):
        acc = p_ref[0]
        for t in range(1, s):
            acc = acc + p_ref[t]
        o_ref[...] = acc

    return pl.pallas_call(
        body, name=name, grid=(r // tr,), in_specs=[pl.BlockSpec((s, tr, c), lambda i: (0, i, 0))],
        out_specs=pl.BlockSpec((tr, c), lambda i: (i, 0)), out_shape=jax.ShapeDtypeStruct((r, c), F32),
        compiler_params=_cparams(("parallel",)),
    )(parts)


def add_pair(a, b, name):
    s, r, c = a.shape
    tr = r if r <= 256 else 256

    def body(a_ref, b_ref, o_ref):
        o_ref[...] = a_ref[...] + b_ref[...]

    spec = pl.BlockSpec((None, tr, c), lambda t, i: (t, i, 0))
    return pl.pallas_call(
        body, name=name, grid=(s, r // tr), in_specs=[spec, spec], out_specs=spec,
        out_shape=jax.ShapeDtypeStruct((s, r, c), F32), compiler_params=_cparams(("parallel", "parallel")),
    )(a, b)


N_CHIPS = 4
N_DEV = 8
_ANY = pl.BlockSpec(memory_space=pl.ANY)


def _mesh_pos():
    return lax.axis_index("x"), lax.axis_index("y"), lax.axis_index("c")


def _other_chips(x, y):
    ps = [(1 - x, y), (x, 1 - y), (1 - x, 1 - y)]
    return [(p, 2 * p[0] + p[1]) for p in ps]


def _remote(src, dst, send_sem, recv_sem, dev):
    return pltpu.make_async_remote_copy(src_ref=src, dst_ref=dst, send_sem=send_sem, recv_sem=recv_sem,
                                        device_id=dev, device_id_type=MESH)


def gather_chips(arrs, name):
    n = len(arrs)

    def body(*refs):
        ins, outs = refs[:n], refs[n:2 * n]
        send_sems, recv_sems, local_sems = refs[2 * n:]
        x, y, c = _mesh_pos()
        me = 2 * x + y
        peers = _other_chips(x, y)
        locs, sends = [], []
        for a in range(n):
            cp = pltpu.make_async_copy(ins[a], outs[a].at[me], local_sems.at[a])
            cp.start()
            locs.append(cp)
            for j, ((px, py), _) in enumerate(peers):
                r = _remote(ins[a], outs[a].at[me], send_sems.at[a, j], recv_sems.at[a, j], (px, py, c))
                r.start()
                sends.append(r)
        for a in range(n):
            for j, ((px, py), t) in enumerate(peers):
                _remote(ins[a], outs[a].at[t], send_sems.at[a, j], recv_sems.at[a, j], (px, py, c)).wait_recv()
        for r in sends:
            r.wait_send()
        for cp in locs:
            cp.wait()

    return pl.pallas_call(
        body, name=name, in_specs=[_ANY] * n, out_specs=[_ANY] * n,
        out_shape=[jax.ShapeDtypeStruct((N_CHIPS,) + a.shape, a.dtype) for a in arrs],
        scratch_shapes=[pltpu.SemaphoreType.DMA((n, 3)), pltpu.SemaphoreType.DMA((n, 3)), pltpu.SemaphoreType.DMA((n,))],
    )(*arrs)


def exchange_sibling_half(gs, name):
    n = len(gs)

    def body(*refs):
        ins, outs = refs[:n], refs[n:2 * n]
        send_sems, recv_sems = refs[2 * n:]
        x, y, c = _mesh_pos()
        sends = []
        for a in range(n):
            half = gs[a].shape[1] // 2
            src = ins[a].at[:, pl.ds((1 - c) * half, half), :]
            r = _remote(src, outs[a], send_sems.at[a], recv_sems.at[a], (x, y, 1 - c))
            r.start()
            sends.append(r)
        for r in sends:
            r.wait_recv()
        for r in sends:
            r.wait_send()

    return pl.pallas_call(
        body, name=name, in_specs=[_ANY] * n, out_specs=[_ANY] * n,
        out_shape=[jax.ShapeDtypeStruct((g.shape[0], g.shape[1] // 2, g.shape[2]), g.dtype) for g in gs],
        scratch_shapes=[pltpu.SemaphoreType.DMA((n,)), pltpu.SemaphoreType.DMA((n,))],
    )(*gs)


def exchange_chips(ps, name):
    n = len(ps)

    def body(*refs):
        ins, outs = refs[:n], refs[n:2 * n]
        send_sems, recv_sems, local_sems = refs[2 * n:]
        x, y, c = _mesh_pos()
        me = 2 * x + y
        peers = _other_chips(x, y)
        locs, sends = [], []
        for a in range(n):
            cp = pltpu.make_async_copy(ins[a].at[me], outs[a].at[me], local_sems.at[a])
            cp.start()
            locs.append(cp)
            for j, ((px, py), t) in enumerate(peers):
                r = _remote(ins[a].at[t], outs[a].at[me], send_sems.at[a, j], recv_sems.at[a, j], (px, py, c))
                r.start()
                sends.append(r)
        for a in range(n):
            for j, ((px, py), t) in enumerate(peers):
                _remote(ins[a].at[t], outs[a].at[t], send_sems.at[a, j], recv_sems.at[a, j], (px, py, c)).wait_recv()
        for r in sends:
            r.wait_send()
        for cp in locs:
            cp.wait()

    return pl.pallas_call(
        body, name=name, in_specs=[_ANY] * n, out_specs=[_ANY] * n,
        out_shape=[jax.ShapeDtypeStruct(p.shape, p.dtype) for p in ps],
        scratch_shapes=[pltpu.SemaphoreType.DMA((n, 3)), pltpu.SemaphoreType.DMA((n, 3)), pltpu.SemaphoreType.DMA((n,))],
    )(*ps)


def share_sibling(groups, name):
    flat = [f for grp in groups for f in grp]
    n = len(flat)
    nw = len(groups)

    def body(*refs):
        ins, outs = refs[:n], refs[n:n + nw]
        send_sems, recv_sems, local_sems = refs[n + nw:]
        x, y, c = _mesh_pos()
        locs, sends, k = [], [], 0
        for w, grp in enumerate(groups):
            for l in range(len(grp)):
                half = grp[l].shape[0]
                dst = outs[w].at[l, pl.ds(c * half, half), :]
                cp = pltpu.make_async_copy(ins[k], dst, local_sems.at[k])
                cp.start()
                locs.append(cp)
                r = _remote(ins[k], dst, send_sems.at[k], recv_sems.at[k], (x, y, 1 - c))
                r.start()
                sends.append(r)
                k += 1
        k = 0
        for w, grp in enumerate(groups):
            for l in range(len(grp)):
                half = grp[l].shape[0]
                theirs = outs[w].at[l, pl.ds((1 - c) * half, half), :]
                _remote(ins[k], theirs, send_sems.at[k], recv_sems.at[k], (x, y, 1 - c)).wait_recv()
                k += 1
        for r in sends:
            r.wait_send()
        for cp in locs:
            cp.wait()

    return pl.pallas_call(
        body, name=name, in_specs=[_ANY] * n, out_specs=[_ANY] * nw,
        out_shape=[jax.ShapeDtypeStruct((len(grp), 2 * grp[0].shape[0], grp[0].shape[1]), F32) for grp in groups],
        scratch_shapes=[pltpu.SemaphoreType.DMA((n,)), pltpu.SemaphoreType.DMA((n,)), pltpu.SemaphoreType.DMA((n,))],
    )(*flat)


def gather_all(small, name):
    flips = [(fx, fy, fc) for fx in (0, 1) for fy in (0, 1) for fc in (0, 1)][1:]

    def body(in_ref, out_ref, send_sems, recv_sems, local_sem):
        x, y, c = _mesh_pos()
        me = 4 * x + 2 * y + c
        cp = pltpu.make_async_copy(in_ref, out_ref.at[me], local_sem)
        cp.start()
        peers = [((1 - x) if fx else x, (1 - y) if fy else y, (1 - c) if fc else c) for fx, fy, fc in flips]
        sends = []
        for k, p in enumerate(peers):
            r = _remote(in_ref, out_ref.at[me], send_sems.at[k], recv_sems.at[k], p)
            r.start()
            sends.append(r)
        for k, p in enumerate(peers):
            _remote(in_ref, out_ref.at[4 * p[0] + 2 * p[1] + p[2]], send_sems.at[k], recv_sems.at[k], p).wait_recv()
        for r in sends:
            r.wait_send()
        cp.wait()

    return pl.pallas_call(
        body, name=name, in_specs=[_ANY], out_specs=_ANY,
        out_shape=jax.ShapeDtypeStruct((N_DEV,) + small.shape, small.dtype),
        scratch_shapes=[pltpu.SemaphoreType.DMA((N_DEV - 1,)), pltpu.SemaphoreType.DMA((N_DEV - 1,)),
                        pltpu.SemaphoreType.DMA],
    )(small)


BIG = ("gdn_w_in", "gdn_w_out", "hgrn_w_in", "hgrn_w_out", "mlp_w_up", "mlp_w_down")
WEIGHTS = ("gdn_w_in", "gdn_conv", "gdn_a_log", "gdn_dt_bias", "gdn_onorm", "gdn_w_out", "hgrn_w_in", "hgrn_lb_logits",
           "hgrn_gnorm", "hgrn_w_out", "norm_mix", "norm_mlp", "mlp_w_up", "mlp_w_down", "norm_final")
SMALL_ROWS = 96


def _pad_lanes(v, n):
    return jnp.pad(v, [(0, 0)] * (v.ndim - 1) + [(0, n - v.shape[-1])])


def _local_grads(x, target, w):
    bsz, t_len, d = x.shape
    n_tok = bsz * t_len
    h = x.reshape(n_tok, d)
    lbl = w["hgrn_lb_logits"]
    saved = []
    for i in range(DEPTH):
        j = i // 2
        nmix = w["norm_mix"][i][None, :]
        if i % 2 == 0:
            y, proj = norm_matmul(h, nmix, w["gdn_w_in"][j], False, f"in_proj{i}")
            al = _pad_lanes(w["gdn_a_log"][j][None, :], 128)
            dtb = _pad_lanes(w["gdn_dt_bias"][j][None, :], 128)
            on = w["gdn_onorm"][j][None, :]
            og, ssave = gdn_forward(proj, w["gdn_conv"][j], al, dtb, on, bsz)
            mix = (proj, ssave, al, dtb, on)
            w_out = w["gdn_w_out"][j]
        else:
            y, proj = norm_matmul(h, nmix, w["hgrn_w_in"][j], False, f"in_proj{i}")
            o, ssave = hgrn_forward(proj, lbl, i, bsz)
            gn = w["hgrn_gnorm"][j][None, :]
            og = hgrn_post_forward(o, proj, gn, f"hgrn_post{i}")
            mix = (proj, ssave, o, gn)
            w_out = w["hgrn_w_out"][j]
        h1 = matmul(og, w_out, "nn", f"out_proj{i}", extra=h, epilogue="add")
        nmlp = w["norm_mlp"][i][None, :]
        z, a, r = norm_matmul(h1, nmlp, w["mlp_w_up"][i], True, f"mlp_up{i}")
        h2 = matmul(a, w["mlp_w_down"][i], "nn", f"mlp_down{i}", extra=h1, epilogue="add")
        saved.append((h, nmix, y, mix, og, w_out, h1, nmlp, z, a, r))
        h = h2
    loss_row, dh, d_nf = loss_head(h, w["norm_final"][None, :], target.reshape(n_tok, d))

    big = {k: [None] * (DEPTH if k.startswith("mlp") else DEPTH // 2) for k in BIG}
    d_nmix, d_nmlp = [None] * DEPTH, [None] * DEPTH
    d_conv, d_alog, d_dtb, d_onorm, d_gnorm = [None] * 2, [None] * 2, [None] * 2, [None] * 2, [None] * 2
    d_lbl = jnp.zeros((DEPTH, d), F32)
    for i in reversed(range(DEPTH)):
        j = i // 2
        h_in, nmix, y, mix, og, w_out, h1, nmlp, z, a, r = saved[i]
        du = matmul(dh, w["mlp_w_down"][i], "nt", f"d_mlp_act{i}", out_dtype=BF16, extra=r, epilogue="mul2")
        big["mlp_w_down"][i] = matmul(a, dh, "tn", f"dw_down{i}").reshape(N_CHIPS, -1, d)
        dz = matmul(du, w["mlp_w_up"][i], "nt", f"d_mlp_in{i}")
        big["mlp_w_up"][i] = matmul(z, du, "tn", f"dw_up{i}", shards=N_CHIPS)
        dh1, d_nmlp[i] = norm_backward(h1, nmlp, dz, dh, f"d_norm_mlp{i}")
        dog = matmul(dh1, w_out, "nt", f"d_mix_out{i}")
        dw_out = matmul(og, dh1, "tn", f"dw_out{i}").reshape(N_CHIPS, -1, d)
        if i % 2 == 0:
            proj, ssave, al, dtb, on = mix
            dq, dk, dv, dg, dab, dcw, dal, ddt, don = gdn_backward(proj, w["gdn_conv"][j], al, dtb, on, ssave, dog, bsz)
            dproj = jnp.concatenate([dq, dk, dv, dg, dab], axis=1)
            dy = matmul(dproj, w["gdn_w_in"][j], "nt", f"d_in_proj{i}")
            dw_in = matmul(y, dproj, "tn", f"dw_in{i}")[:, :GDN_IN]
            big["gdn_w_in"][j] = jnp.transpose(dw_in.reshape(d, N_CHIPS, GDN_IN // N_CHIPS), (1, 0, 2))
            big["gdn_w_out"][j] = dw_out
            d_conv[j] = jnp.transpose(dcw, (2, 1, 0, 3)).reshape(CONV_K, 3 * d)
            d_alog[j], d_dtb[j], d_onorm[j] = dal[0, :HEADS], ddt[0, :HEADS], don[0]
        else:
            proj, ssave, o, gn = mix
            do_raw, dgate, dgn = hgrn_post_backward(o, proj, gn, dog, f"d_hgrn_post{i}")
            dq, df, dv, dlb = hgrn_backward(proj, lbl, ssave, do_raw, i, bsz)
            dproj = jnp.concatenate([dq, df, dv, dgate], axis=1)
            dy = matmul(dproj, w["hgrn_w_in"][j], "nt", f"d_in_proj{i}")
            big["hgrn_w_in"][j] = matmul(y, dproj, "tn", f"dw_in{i}", shards=N_CHIPS)
            big["hgrn_w_out"][j] = dw_out
            d_gnorm[j] = dgn[0]
            d_lbl = d_lbl + jnp.transpose(dlb, (1, 0, 2)).reshape(DEPTH, d)
        dh, d_nmix[i] = norm_backward(h_in, nmix, dy, dh1, f"d_norm_mix{i}")
    small = {
        "gdn_conv": jnp.stack(d_conv), "gdn_a_log": jnp.stack(d_alog), "gdn_dt_bias": jnp.stack(d_dtb),
        "gdn_onorm": jnp.stack(d_onorm), "hgrn_lb_logits": d_lbl, "hgrn_gnorm": jnp.stack(d_gnorm),
        "norm_mix": jnp.concatenate(d_nmix, axis=0), "norm_mlp": jnp.concatenate(d_nmlp, axis=0), "norm_final": d_nf[0],
    }
    return loss_row, dh.reshape(x.shape), big, small


_SMALL_LAYOUT = {
    "norm_mix": (0, 4, D_MODEL), "norm_mlp": (8, 4, D_MODEL), "norm_final": (16, 1, D_MODEL),
    "hgrn_lb_logits": (24, 4, D_MODEL), "gdn_onorm": (32, 2, 128), "gdn_a_log": (40, 2, HEADS),
    "gdn_dt_bias": (48, 2, HEADS), "loss": (56, 1, 128), "gdn_conv": (64, 24, D_MODEL), "hgrn_gnorm": (88, 2, D_MODEL),
}


def _pack_small(small, loss_row):
    rows = []
    for name, (first, nrow, lanes) in _SMALL_LAYOUT.items():
        v = loss_row if name == "loss" else small[name]
        v = _pad_lanes(v.reshape(nrow, -1), D_MODEL)
        rows.append(jnp.pad(v, ((0, -nrow % 8), (0, 0))))
    return jnp.concatenate(rows, axis=0)


def _unpack_small(packed, name, shape):
    first, nrow, lanes = _SMALL_LAYOUT[name]
    return packed[first:first + nrow, :lanes].reshape(shape)


def kernel(x, gdn_w_in, gdn_conv, gdn_a_log, gdn_dt_bias, gdn_onorm, gdn_w_out, hgrn_w_in, hgrn_lb_logits, hgrn_gnorm, hgrn_w_out, norm_mix, norm_mlp, mlp_w_up, mlp_w_down, norm_final, loss_target, m_gdn_w_in, m_gdn_conv, m_gdn_a_log, m_gdn_dt_bias, m_gdn_onorm, m_gdn_w_out, m_hgrn_w_in, m_hgrn_lb_logits, m_hgrn_gnorm, m_hgrn_w_out, m_norm_mix, m_norm_mlp, m_mlp_w_up, m_mlp_w_down, m_norm_final, v_gdn_w_in, v_gdn_conv, v_gdn_a_log, v_gdn_dt_bias, v_gdn_onorm, v_gdn_w_out, v_hgrn_w_in, v_hgrn_lb_logits, v_hgrn_gnorm, v_hgrn_w_out, v_norm_mix, v_norm_mlp, v_mlp_w_up, v_mlp_w_down, v_norm_final):
    p = dict(gdn_w_in=gdn_w_in, gdn_conv=gdn_conv, gdn_a_log=gdn_a_log, gdn_dt_bias=gdn_dt_bias, gdn_onorm=gdn_onorm,
             gdn_w_out=gdn_w_out, hgrn_w_in=hgrn_w_in, hgrn_lb_logits=hgrn_lb_logits, hgrn_gnorm=hgrn_gnorm,
             hgrn_w_out=hgrn_w_out, norm_mix=norm_mix, norm_mlp=norm_mlp, mlp_w_up=mlp_w_up, mlp_w_down=mlp_w_down,
             norm_final=norm_final)
    m = dict(gdn_w_in=m_gdn_w_in, gdn_conv=m_gdn_conv, gdn_a_log=m_gdn_a_log, gdn_dt_bias=m_gdn_dt_bias,
             gdn_onorm=m_gdn_onorm, gdn_w_out=m_gdn_w_out, hgrn_w_in=m_hgrn_w_in, hgrn_lb_logits=m_hgrn_lb_logits,
             hgrn_gnorm=m_hgrn_gnorm, hgrn_w_out=m_hgrn_w_out, norm_mix=m_norm_mix, norm_mlp=m_norm_mlp,
             mlp_w_up=m_mlp_w_up, mlp_w_down=m_mlp_w_down, norm_final=m_norm_final)
    v = dict(gdn_w_in=v_gdn_w_in, gdn_conv=v_gdn_conv, gdn_a_log=v_gdn_a_log, gdn_dt_bias=v_gdn_dt_bias,
             gdn_onorm=v_gdn_onorm, gdn_w_out=v_gdn_w_out, hgrn_w_in=v_hgrn_w_in, hgrn_lb_logits=v_hgrn_lb_logits,
             hgrn_gnorm=v_hgrn_gnorm, hgrn_w_out=v_hgrn_w_out, norm_mix=v_norm_mix, norm_mlp=v_norm_mlp,
             mlp_w_up=v_mlp_w_up, mlp_w_down=v_mlp_w_down, norm_final=v_norm_final)
    xi, yi, ci = _mesh_pos()
    chip = 2 * xi + yi
    d = D_MODEL

    sharded = list(BIG) + ["gdn_conv", "hgrn_gnorm"]
    got = dict(zip(sharded, gather_chips([p[k].astype(BF16) if k in BIG else p[k] for k in sharded], "gather_weights")))
    cols = lambda g: jnp.transpose(g, (1, 2, 0, 3)).reshape(g.shape[1], g.shape[2], -1)
    rows = lambda g: jnp.transpose(g, (1, 0, 2, 3)).reshape(g.shape[1], -1, g.shape[3])
    w = dict(p)
    w["gdn_w_in"] = _pad_lanes(cols(got["gdn_w_in"]), GDN_IN_PAD)
    w["hgrn_w_in"] = cols(got["hgrn_w_in"])
    w["mlp_w_up"] = cols(got["mlp_w_up"])
    w["gdn_conv"] = cols(got["gdn_conv"])
    w["gdn_w_out"] = rows(got["gdn_w_out"])
    w["hgrn_w_out"] = rows(got["hgrn_w_out"])
    w["mlp_w_down"] = rows(got["mlp_w_down"])
    w["hgrn_gnorm"] = jnp.transpose(got["hgrn_gnorm"], (1, 0, 2)).reshape(got["hgrn_gnorm"].shape[1], -1)

    loss_row, grad_x, big, small = _local_grads(x, loss_target, w)

    order = [(k, l) for k in BIG for l in range(len(big[k]))]
    gs = [big[k][l] for k, l in order]
    theirs = exchange_sibling_half(gs, "reduce_cores")
    mine = [lax.dynamic_slice_in_dim(g, ci * (g.shape[1] // 2), g.shape[1] // 2, axis=1) for g in gs]
    ps = [add_pair(a, b, f"add_cores{n}") for n, (a, b) in enumerate(zip(mine, theirs))]
    slots = exchange_chips(ps, "reduce_chips")
    halves = [add_slots(s, f"add_chips{n}") for n, s in enumerate(slots)]
    groups = [[halves[n] for n, (k, l) in enumerate(order) if k == name] for name in BIG]
    grads = dict(zip(BIG, share_sibling(groups, "share_cores")))

    total = add_slots(gather_all(_pack_small(small, loss_row), "gather_small"), "add_small")
    loss = total[_SMALL_LAYOUT["loss"][0], 0]
    for name in WEIGHTS:
        if name in BIG:
            continue
        if name == "gdn_conv":
            full = _unpack_small(total, name, (2, CONV_K, 3 * d))
            grads[name] = lax.dynamic_slice_in_dim(full, chip * (3 * d // N_CHIPS), 3 * d // N_CHIPS, axis=2)
        elif name == "hgrn_gnorm":
            full = _unpack_small(total, name, (2, d))
            grads[name] = lax.dynamic_slice_in_dim(full, chip * (d // N_CHIPS), d // N_CHIPS, axis=1)
        else:
            grads[name] = _unpack_small(total, name, p[name].shape)

    delta, new_m, new_v = {}, {}, {}
    for name in WEIGHTS:
        delta[name], new_m[name], new_v[name] = adamw(p[name], grads[name], m[name], v[name], f"adamw_{name}")
    return (loss, grad_x, *[grads[n] for n in WEIGHTS], *[delta[n] for n in WEIGHTS],
            *[new_m[n] for n in WEIGHTS], *[new_v[n] for n in WEIGHTS])


def add_core_halves(g, theirs, core, name):
    s, r, c = g.shape
    r2 = r // 2
    tr = min(r2, 256)
    nb = r2 // tr

    def body(core_ref, g_ref, t_ref, o_ref):
        o_ref[...] = g_ref[...] + t_ref[...]

    grid_spec = pltpu.PrefetchScalarGridSpec(
        num_scalar_prefetch=1, grid=(s, nb),
        in_specs=[pl.BlockSpec((None, tr, c), lambda t, i, cr: (t, cr[0] * nb + i, 0)),
                  pl.BlockSpec((None, tr, c), lambda t, i, cr: (t, i, 0))],
        out_specs=pl.BlockSpec((None, tr, c), lambda t, i, cr: (t, i, 0)))
    return pl.pallas_call(body, name=name, grid_spec=grid_spec, out_shape=jax.ShapeDtypeStruct((s, r2, c), F32),
                          compiler_params=_cparams(("parallel", "parallel")))(core, g, theirs)


def add_chip_slots(slots, name):
    n_l = len(slots)
    s, r2, c = slots[0].shape
    tr = min(r2, 256)
    nb = r2 // tr

    def body(*refs):
        ins, o_ref = refs[:n_l], refs[n_l]
        for k in range(n_l):
            @pl.when(pl.program_id(0) == k)
            def _(k=k):
                acc = ins[k][0]
                for t in range(1, s):
                    acc = acc + ins[k][t]
                o_ref[...] = acc

    in_specs = [pl.BlockSpec((s, tr, c), lambda l, i, k=k: (0, jnp.where(l == k, i, 0), 0)) for k in range(n_l)]
    return pl.pallas_call(
        body, name=name, grid=(n_l, nb), in_specs=in_specs, out_specs=pl.BlockSpec((None, tr, c), lambda l, i: (l, i, 0)),
        out_shape=jax.ShapeDtypeStruct((n_l, r2, c), F32), compiler_params=_cparams(("arbitrary", "arbitrary")),
    )(*slots)


def adamw_halves(w, m, v, mine, theirs, name):
    n_l, r, c = w.shape
    r2 = r // 2
    tr = min(r2, 256)
    nb = r2 // tr
    c1 = 1.0 / (1.0 - ADAM_B1 ** ADAM_STEP)
    c2 = 1.0 / (1.0 - ADAM_B2 ** ADAM_STEP)

    def body(w_ref, m_ref, v_ref, mine_ref, theirs_ref, g_ref, d_ref, nm_ref, nv_ref):
        my_half = (pl.program_id(1) // nb) == lax.axis_index("c")
        gg = jnp.where(my_half, mine_ref[...], theirs_ref[...])
        nm = ADAM_B1 * m_ref[...] + (1.0 - ADAM_B1) * gg
        nv = ADAM_B2 * v_ref[...] + (1.0 - ADAM_B2) * (gg * gg)
        g_ref[...] = gg
        d_ref[...] = -ADAM_LR * ((nm * c1) / (jnp.sqrt(nv * c2) + ADAM_EPS) + ADAM_WD * w_ref[...])
        nm_ref[...] = nm
        nv_ref[...] = nv

    full = pl.BlockSpec((None, tr, c), lambda l, i: (l, i, 0))
    half = pl.BlockSpec((None, tr, c), lambda l, i: (l, i % nb, 0))
    sds = jax.ShapeDtypeStruct((n_l, r, c), F32)
    return pl.pallas_call(
        body, name=name, grid=(n_l, r // tr), in_specs=[full, full, full, half, half], out_specs=[full] * 4,
        out_shape=(sds,) * 4, compiler_params=_cparams(("parallel", "parallel")),
    )(w, m, v, mine, theirs)


def _layer_weight(kind, i):
    if kind == "up":
        return "mlp_w_up", i
    if kind == "down":
        return "mlp_w_down", i
    return ("gdn_w_" if i % 2 == 0 else "hgrn_w_") + kind, i // 2


def kernel(x, gdn_w_in, gdn_conv, gdn_a_log, gdn_dt_bias, gdn_onorm, gdn_w_out, hgrn_w_in, hgrn_lb_logits, hgrn_gnorm, hgrn_w_out, norm_mix, norm_mlp, mlp_w_up, mlp_w_down, norm_final, loss_target, m_gdn_w_in, m_gdn_conv, m_gdn_a_log, m_gdn_dt_bias, m_gdn_onorm, m_gdn_w_out, m_hgrn_w_in, m_hgrn_lb_logits, m_hgrn_gnorm, m_hgrn_w_out, m_norm_mix, m_norm_mlp, m_mlp_w_up, m_mlp_w_down, m_norm_final, v_gdn_w_in, v_gdn_conv, v_gdn_a_log, v_gdn_dt_bias, v_gdn_onorm, v_gdn_w_out, v_hgrn_w_in, v_hgrn_lb_logits, v_hgrn_gnorm, v_hgrn_w_out, v_norm_mix, v_norm_mlp, v_mlp_w_up, v_mlp_w_down, v_norm_final):
    p = dict(gdn_w_in=gdn_w_in, gdn_conv=gdn_conv, gdn_a_log=gdn_a_log, gdn_dt_bias=gdn_dt_bias, gdn_onorm=gdn_onorm,
             gdn_w_out=gdn_w_out, hgrn_w_in=hgrn_w_in, hgrn_lb_logits=hgrn_lb_logits, hgrn_gnorm=hgrn_gnorm,
             hgrn_w_out=hgrn_w_out, norm_mix=norm_mix, norm_mlp=norm_mlp, mlp_w_up=mlp_w_up, mlp_w_down=mlp_w_down,
             norm_final=norm_final)
    m = dict(gdn_w_in=m_gdn_w_in, gdn_conv=m_gdn_conv, gdn_a_log=m_gdn_a_log, gdn_dt_bias=m_gdn_dt_bias,
             gdn_onorm=m_gdn_onorm, gdn_w_out=m_gdn_w_out, hgrn_w_in=m_hgrn_w_in, hgrn_lb_logits=m_hgrn_lb_logits,
             hgrn_gnorm=m_hgrn_gnorm, hgrn_w_out=m_hgrn_w_out, norm_mix=m_norm_mix, norm_mlp=m_norm_mlp,
             mlp_w_up=m_mlp_w_up, mlp_w_down=m_mlp_w_down, norm_final=m_norm_final)
    v = dict(gdn_w_in=v_gdn_w_in, gdn_conv=v_gdn_conv, gdn_a_log=v_gdn_a_log, gdn_dt_bias=v_gdn_dt_bias,
             gdn_onorm=v_gdn_onorm, gdn_w_out=v_gdn_w_out, hgrn_w_in=v_hgrn_w_in, hgrn_lb_logits=v_hgrn_lb_logits,
             hgrn_gnorm=v_hgrn_gnorm, hgrn_w_out=v_hgrn_w_out, norm_mix=v_norm_mix, norm_mlp=v_norm_mlp,
             mlp_w_up=v_mlp_w_up, mlp_w_down=v_mlp_w_down, norm_final=v_norm_final)
    xi, yi, ci = _mesh_pos()
    chip = 2 * xi + yi
    core = jnp.reshape(ci, (1,)).astype(jnp.int32)
    d = D_MODEL
    bsz, t_len, _ = x.shape
    n_tok = bsz * t_len

    def shard(kind, i):
        name, idx = _layer_weight(kind, i)
        return p[name][idx].astype(BF16)

    def w_in_of(i, slots):
        if i % 2 == 0:
            return _pad_lanes(jnp.transpose(slots, (1, 0, 2)).reshape(d, GDN_IN), GDN_IN_PAD)
        return slots

    (first,) = run_carries([gather_carry([shard("in", 0), p["gdn_conv"], p["hgrn_gnorm"]])], "gather_first")
    conv = jnp.transpose(first[1], (1, 2, 0, 3)).reshape(DEPTH // 2, CONV_K, 3 * d)
    gnorm = jnp.transpose(first[2], (1, 0, 2)).reshape(DEPTH // 2, d)
    lbl = p["hgrn_lb_logits"]
    h = x.reshape(n_tok, d)
    next_in, next_out = first[0], None
    saved = []
    for i in range(DEPTH):
        j = i // 2
        w_in = w_in_of(i, next_in)
        nmix = p["norm_mix"][i][None, :]
        (y, proj), got = norm_matmul(h, nmix, w_in, False, f"in_proj{i}",
                                     [gather_carry([shard("out", 0)])] if i == 0 else [])
        if i == 0:
            next_out = got[0][0]
        ride = [gather_carry([shard("up", i), shard("down", i)])]
        if i % 2 == 0:
            al = _pad_lanes(p["gdn_a_log"][j][None, :], 128)
            dtb = _pad_lanes(p["gdn_dt_bias"][j][None, :], 128)
            on = p["gdn_onorm"][j][None, :]
            (og, ssave), got = gdn_forward(proj, conv[j], al, dtb, on, bsz, f"gdn_fwd{i}", ride)
            mix = (proj, ssave, al, dtb, on)
        else:
            (o, ssave), got = hgrn_forward(proj, lbl, i, bsz, ride)
            gn = gnorm[j][None, :]
            og = hgrn_post_forward(o, proj, gn, f"hgrn_post{i}")
            mix = (proj, ssave, o, gn)
        w_up, w_down = got[0][0], got[0][1].reshape(MLP_HIDDEN, d)
        w_out = next_out.reshape(d, d)
        h1, _ = matmul(og, w_out, "nn", f"out_proj{i}", extra=h, epilogue="add")
        nmlp = p["norm_mlp"][i][None, :]
        (z, a, r), got = norm_matmul(h1, nmlp, w_up, True, f"mlp_up{i}",
                                     [gather_carry([shard("in", i + 1)])] if i + 1 < DEPTH else [])
        if i + 1 < DEPTH:
            next_in = got[0][0]
        h2, got = matmul(a, w_down, "nn", f"mlp_down{i}", extra=h1, epilogue="add",
                         carries=[gather_carry([shard("out", i + 1)])] if i + 1 < DEPTH else [])
        if i + 1 < DEPTH:
            next_out = got[0][0]
        saved.append((h, nmix, y, mix, og, w_in, w_out, h1, nmlp, z, a, r, w_up, w_down))
        h = h2
    loss_row, dh, dhb, d_nf = loss_head(h, p["norm_final"][None, :], loss_target.reshape(n_tok, d))

    G = {k: [None] * DEPTH for k in ("in", "out", "up", "down")}
    P = {k: [None] * DEPTH for k in ("in", "out", "up", "down")}
    slots = {k: [None] * DEPTH for k in ("in", "out", "up", "down")}
    d_nmix, d_nmlp = [None] * DEPTH, [None] * DEPTH
    d_conv, d_alog, d_dtb, d_onorm, d_gnorm = [None] * 2, [None] * 2, [None] * 2, [None] * 2, [None] * 2
    d_lbl = jnp.zeros((DEPTH, d), F32)
    for i in reversed(range(DEPTH)):
        j = i // 2
        h_in, nmix, y, mix, og, w_in, w_out, h1, nmlp, z, a, r, w_up, w_down = saved[i]
        ride = [sibling_half_carry([G["in"][i + 1], G["out"][i + 1]])] if i + 1 < DEPTH else []
        du, got = matmul(dhb, w_down, "nt", f"d_mlp_act{i}", out_dtype=BF16, extra=r, epilogue="mul2", carries=ride)
        if i + 1 < DEPTH:
            P["in"][i + 1] = add_core_halves(G["in"][i + 1], got[0][0], core, f"add_cores_in{i + 1}")
            P["out"][i + 1] = add_core_halves(G["out"][i + 1], got[0][1], core, f"add_cores_out{i + 1}")
        G["down"][i] = matmul(a, dhb, "tn", f"dw_down{i}")[0].reshape(N_CHIPS, -1, d)
        dz, _ = matmul(du, w_up, "nt", f"d_mlp_in{i}")
        G["up"][i], _ = matmul(z, du, "tn", f"dw_up{i}", shards=N_CHIPS)
        dh1, dh1b, d_nmlp[i] = norm_backward(h1, nmlp, dz, dh, f"d_norm_mlp{i}")
        dog, _ = matmul(dh1b, w_out, "nt", f"d_mix_out{i}")
        G["out"][i] = matmul(og, dh1b, "tn", f"dw_out{i}")[0].reshape(N_CHIPS, -1, d)
        ride = [sibling_half_carry([G["up"][i], G["down"][i]])]
        if i + 1 < DEPTH:
            ride.append(chips_carry([P[k][i + 1] for k in ("in", "out", "up", "down")]))
        if i % 2 == 0:
            proj, ssave, al, dtb, on = mix
            (dproj, dcw, dal, ddt, don), got = gdn_backward(proj, conv[j], al, dtb, on, ssave, dog, bsz, f"gdn_bwd{i}", ride)
            d_conv[j] = jnp.transpose(dcw, (2, 1, 0, 3)).reshape(CONV_K, 3 * d)
            d_alog[j], d_dtb[j], d_onorm[j] = dal[0, :HEADS], ddt[0, :HEADS], don[0]
        else:
            proj, ssave, o, gn = mix
            do_raw, dgate, dgn = hgrn_post_backward(o, proj, gn, dog, f"d_hgrn_post{i}")
            (dproj, dlb), got = hgrn_backward(proj, lbl, ssave, do_raw, dgate, i, bsz, ride)
            d_gnorm[j] = dgn[0]
            d_lbl = d_lbl + jnp.transpose(dlb, (1, 0, 2)).reshape(DEPTH, d)
        P["up"][i] = add_core_halves(G["up"][i], got[0][0], core, f"add_cores_up{i}")
        P["down"][i] = add_core_halves(G["down"][i], got[0][1], core, f"add_cores_down{i}")
        if i + 1 < DEPTH:
            for k, s in zip(("in", "out", "up", "down"), got[1]):
                slots[k][i + 1] = s
        ride = [chips_carry([P["up"][0], P["down"][0]])] if i == 0 else []
        dy, got = matmul(dproj, w_in, "nt", f"d_in_proj{i}", carries=ride)
        if i == 0:
            slots["up"][0], slots["down"][0] = got[0]
        if i % 2 == 0:
            dw_in = matmul(y, dproj, "tn", f"dw_in{i}")[0][:, :GDN_IN]
            G["in"][i] = jnp.transpose(dw_in.reshape(d, N_CHIPS, GDN_IN // N_CHIPS), (1, 0, 2))
        else:
            G["in"][i], _ = matmul(y, dproj, "tn", f"dw_in{i}", shards=N_CHIPS)
        dh, dhb, d_nmix[i] = norm_backward(h_in, nmix, dy, dh1, f"d_norm_mix{i}")
    grad_x = dh.reshape(x.shape)

    (got,) = run_carries([sibling_half_carry([G["in"][0], G["out"][0]])], "reduce_cores_last")
    P["in"][0] = add_core_halves(G["in"][0], got[0], core, "add_cores_in0")
    P["out"][0] = add_core_halves(G["out"][0], got[1], core, "add_cores_out0")
    (got,) = run_carries([chips_carry([P["in"][0], P["out"][0]])], "reduce_chips_last")
    slots["in"][0], slots["out"][0] = got
    by_weight = {}
    for kind in ("in", "out", "up", "down"):
        for i in range(DEPTH):
            by_weight.setdefault(_layer_weight(kind, i)[0], []).append(slots[kind][i])
    mine = {name: add_chip_slots(by_weight[name], f"add_chips_{name}") for name in BIG}
    small = {
        "gdn_conv": jnp.stack(d_conv), "gdn_a_log": jnp.stack(d_alog), "gdn_dt_bias": jnp.stack(d_dtb),
        "gdn_onorm": jnp.stack(d_onorm), "hgrn_lb_logits": d_lbl, "hgrn_gnorm": jnp.stack(d_gnorm),
        "norm_mix": jnp.concatenate(d_nmix, axis=0), "norm_mlp": jnp.concatenate(d_nmlp, axis=0), "norm_final": d_nf[0],
    }
    theirs, (blocks,) = run_carries([sibling_copy_carry([mine[name] for name in BIG]),
                                     gather_all_carry(_pack_small(small, loss_row))], "share_cores")
    theirs = dict(zip(BIG, theirs))

    total = add_slots(blocks, "add_small")
    loss = total[_SMALL_LAYOUT["loss"][0], 0]
    grads, delta, new_m, new_v = {}, {}, {}, {}
    for name in WEIGHTS:
        if name in BIG:
            grads[name], delta[name], new_m[name], new_v[name] = adamw_halves(
                p[name], m[name], v[name], mine[name], theirs[name], f"adamw_{name}")
            continue
        if name == "gdn_conv":
            full = _unpack_small(total, name, (2, CONV_K, 3 * d))
            grads[name] = lax.dynamic_slice_in_dim(full, chip * (3 * d // N_CHIPS), 3 * d // N_CHIPS, axis=2)
        elif name == "hgrn_gnorm":
            full = _unpack_small(total, name, (2, d))
            grads[name] = lax.dynamic_slice_in_dim(full, chip * (d // N_CHIPS), d // N_CHIPS, axis=1)
        else:
            grads[name] = _unpack_small(total, name, p[name].shape)
        delta[name], new_m[name], new_v[name] = adamw(p[name], grads[name], m[name], v[name], f"adamw_{name}")
    return (loss, grad_x, *[grads[n] for n in WEIGHTS], *[delta[n] for n in WEIGHTS],
            *[new_m[n] for n in WEIGHTS], *[new_v[n] for n in WEIGHTS])
```

```python
import functools
import math

import jax
import jax.numpy as jnp
from jax import lax
from jax.experimental import pallas as pl
from jax.experimental.pallas import tpu as pltpu

F32 = jnp.float32
BF16 = jnp.bfloat16
HI = lax.Precision.HIGHEST

D_MODEL = 1024
HEADS = 8
HEAD_DIM = 128
CHUNK = 64
SUB = 16
CONV_K = 4
HALO = 8
DEPTH = 4
EPS = 1e-6
MLP_HIDDEN = 4 * D_MODEL
GDN_MAIN = 4 * D_MODEL
GDN_IN = GDN_MAIN + 2 * HEADS
GDN_IN_PAD = GDN_MAIN + 128
NEG = -1e30

ADAM_LR = 0.001
ADAM_B1 = 0.9
ADAM_B2 = 0.999
ADAM_EPS = 1e-08
ADAM_WD = 0.01
ADAM_STEP = 10

VMEM_LIMIT = 48 * 1024 * 1024

MESH = pl.DeviceIdType.MESH


def _cparams(sem=None, **kw):
    if sem is not None:
        kw["dimension_semantics"] = sem
    return pltpu.CompilerParams(vmem_limit_bytes=VMEM_LIMIT, **kw)


def _iota(shape, dim):
    return lax.broadcasted_iota(jnp.int32, shape, dim)


_DIMS = {"nn": (((1,), (0,)), ((), ())), "nt": (((1,), (1,)), ((), ())), "tn": (((0,), (0,)), ((), ()))}


def _dot(a, b, mode):
    return lax.dot_general(a.astype(BF16), b.astype(BF16), _DIMS[mode], preferred_element_type=F32)


@functools.partial(jax.custom_vjp, nondiff_argnums=(2,))
def _mmx(a, b, mode):
    return _dot(a, b, mode)


def _mmx_fwd(a, b, mode):
    return _dot(a, b, mode), (a, b)


def _mmx_bwd(mode, res, g):
    a, b = res
    if mode == "nn":
        return _dot(g, b, "nt"), _dot(a, g, "tn")
    if mode == "nt":
        return _dot(g, b, "nn"), _dot(g, a, "tn")
    return _dot(b, g, "nt"), _dot(a, g, "nn")


_mmx.defvjp(_mmx_fwd, _mmx_bwd)


def _mm_f32(a, b):
    return lax.dot_general(a, b, _DIMS["nn"], precision=HI, preferred_element_type=F32)


def _mm(a, b):
    return _mmx(a, b, "nn")


def _mm_nt(a, b):
    return _mmx(a, b, "nt")


def _mm_tn(a, b):
    return _mmx(a, b, "tn")


@functools.partial(jax.custom_vjp, nondiff_argnums=(1,))
def _roll_rows(x, d):
    return pltpu.roll(x, d, 0)


def _roll_rows_fwd(x, d):
    return pltpu.roll(x, d, 0), None


def _roll_rows_bwd(d, _, g):
    return (pltpu.roll(g, g.shape[0] - d, 0),)


_roll_rows.defvjp(_roll_rows_fwd, _roll_rows_bwd)


def _sigmoid(x):
    return 1.0 / (1.0 + jnp.exp(-x))


def _silu(x):
    return x * _sigmoid(x)


def _softplus(x):
    return jnp.maximum(x, 0.0) + jnp.log(1.0 + jnp.exp(-jnp.abs(x)))


def _log_sigmoid(x):
    return jnp.minimum(x, 0.0) - jnp.log(1.0 + jnp.exp(-jnp.abs(x)))


def _logaddexp(a, b):
    return jnp.maximum(a, b) + jnp.log(1.0 + jnp.exp(-jnp.abs(a - b)))


def _row_to_col(row):
    n = row.shape[1]
    eye = _iota((n, n), 0) == _iota((n, n), 1)
    return jnp.sum(jnp.where(eye, jnp.broadcast_to(row, (n, n)), 0.0), axis=1, keepdims=True)


def _col_to_row(col):
    n = col.shape[0]
    eye = _iota((n, n), 0) == _iota((n, n), 1)
    return jnp.sum(jnp.where(eye, jnp.broadcast_to(col, (n, n)), 0.0), axis=0, keepdims=True)


def _pick_row(x, r):
    return jnp.sum(jnp.where(_iota(x.shape, 0) == r, x, 0.0), axis=0, keepdims=True)


def _pick_lane(x, l):
    return jnp.sum(jnp.where(_iota(x.shape, 1) == l, x, 0.0), axis=1, keepdims=True)


def _each(f, *lists):
    return [f(*t) for t in zip(*lists)]


def _unit_lower_inverse(Ls):
    n = Ls[0].shape[0]
    r, c = _iota((n, n), 0), _iota((n, n), 1)
    eye = jnp.where(r == c, 1.0, 0.0).astype(F32)
    Ld = _each(lambda L: jnp.where((r // SUB) == (c // SUB), L, 0.0), Ls)
    Lo = _each(lambda L, d: L - d, Ls, Ld)
    P = _each(lambda d: eye - d, Ld)
    Lp = Ld
    for _ in range(int(math.log2(SUB)) - 1):
        Lp = _each(lambda x: _mm(x, x), Lp)
        P = _each(lambda p, x: p + _mm(p, x), P, Lp)
    N = _each(_mm, P, Lo)
    N2 = _each(lambda x: _mm(x, x), N)
    X = _each(lambda x, x2: (eye - x) + _mm(eye - x, x2), N, N2)
    return _each(_mm, X, P)


def _shift_rows(x, halo, d):
    if d == 0:
        return x
    xr = _roll_rows(x, d)
    hr = _roll_rows(halo, d)
    hfull = jnp.concatenate([hr, jnp.zeros((x.shape[0] - HALO, x.shape[1]), F32)], axis=0)
    return jnp.where(_iota(x.shape, 0) >= d, xr, hfull)


def _causal_conv_chunk(x, halo, w):
    y = None
    for kk in range(CONV_K):
        t = _shift_rows(x, halo, CONV_K - 1 - kk) * _pick_row(w, kk)
        y = t if y is None else y + t
    return y


def _gdn_chunk(xq, xk, xv, hq, hk, hv, gate, ab, S, cwq, cwk, cwv, alog, dtb, onorm, *, heads):
    C = xq[0].shape[0]
    q = _each(lambda x, h, w: _silu(_causal_conv_chunk(x, h, w)), xq, hq, cwq)
    k = _each(lambda x, h, w: _silu(_causal_conv_chunk(x, h, w)), xk, hk, cwk)
    v = _each(lambda x, h, w: _silu(_causal_conv_chunk(x, h, w)), xv, hv, cwv)
    q = _each(lambda t: t * lax.rsqrt(jnp.sum(t * t, axis=1, keepdims=True) + EPS) * (HEAD_DIM ** -0.5), q)
    k = _each(lambda t: t * lax.rsqrt(jnp.sum(t * t, axis=1, keepdims=True) + EPS), k)
    beta = [_sigmoid(_pick_lane(ab, HEADS + h)) for h in heads]
    g = [-jnp.exp(_pick_lane(alog, h)) * _softplus(_pick_lane(ab, h) + _pick_lane(dtb, h)) for h in heads]
    r, c = _iota((C, C), 0), _iota((C, C), 1)
    gc = _each(lambda t: jnp.sum(jnp.where(c <= r, jnp.broadcast_to(_col_to_row(t), (C, C)), 0.0), axis=1,
                                 keepdims=True), g)
    gc_row = _each(lambda t: jnp.sum(jnp.where(r <= c, jnp.broadcast_to(t, (C, C)), 0.0), axis=0, keepdims=True), g)
    decay = _each(lambda a, b: jnp.exp(jnp.where(r >= c, a - b, NEG)), gc, gc_row)
    kb = _each(lambda a, b: a * b, k, beta)
    L = _each(lambda a, b, d: jnp.where(r > c, _mm_nt(a, b) * d, 0.0), kb, k, decay)
    A = _each(lambda a, b, d: jnp.where(r >= c, _mm_nt(a, b) * d, 0.0), q, k, decay)
    T = _unit_lower_inverse(L)
    egc = _each(jnp.exp, gc)
    u = _each(lambda t, a, b: _mm(t, a * b), T, v, beta)
    w = _each(lambda t, a, e: _mm(t, a * e), T, kb, egc)
    gl = _each(lambda t: _pick_row(t, C - 1), gc)
    v_new = _each(lambda a, b, s: a - _mm(b, s), u, w, S)
    o = _each(lambda a, e, s, m, vn: _mm(a * e, s) + _mm(m, vn), q, egc, S, A, v_new)
    S_next = _each(lambda s, l, a, t, vn: s * jnp.exp(l) + _mm_tn(a * jnp.exp(l - t), vn), S, gl, k, gc, v_new)
    o = _each(lambda t, gt: t * lax.rsqrt(jnp.mean(t * t, axis=1, keepdims=True) + EPS) * onorm * _silu(gt), o, gate)
    return o, S_next


def _hgrn_lower_bound(lbl, layer):
    e = jnp.exp(lbl - jnp.max(lbl, axis=0, keepdims=True))
    sm = e / jnp.sum(e, axis=0, keepdims=True)
    r = _iota(lbl.shape, 0)
    return jnp.sum(jnp.where((r >= 1) & (r <= layer), sm, 0.0), axis=0, keepdims=True)


_LEVELS = (1, 2, 4, 8, 16, 32)


def _prefix_matrix(n):
    r, c = _iota((n, n), 0), _iota((n, n), 1)
    parts = [jnp.where(c <= r, 1.0, 0.0)]
    for s in _LEVELS:
        parts.append(jnp.where(c < (r // (2 * s)) * (2 * s) + s, 1.0, 0.0))
    return jnp.concatenate(parts, axis=0).astype(F32)


def _prefix_sums_of(x):
    n = x.shape[0]
    y = lax.dot_general(_prefix_matrix(n), x, _DIMS["nn"], precision=HI, preferred_element_type=F32)
    return tuple(y[t * n:(t + 1) * n] for t in range(len(_LEVELS) + 1))


@jax.custom_vjp
def _prefix_sums(x):
    return _prefix_sums_of(x)


def _prefix_sums_fwd(x):
    return _prefix_sums_of(x), None


def _prefix_sums_bwd(_, gs):
    g = jnp.concatenate(gs, axis=0)
    return (lax.dot_general(_prefix_matrix(gs[0].shape[0]), g, _DIMS["tn"], precision=HI, preferred_element_type=F32),)


_prefix_sums.defvjp(_prefix_sums_fwd, _prefix_sums_bwd)


def _hgrn_chunk(qp, fp, v, S, lbl, *, layer):
    C = qp[0].shape[0]
    lb = _each(lambda l: _hgrn_lower_bound(l, layer), lbl)
    lf = _each(lambda l, f: _logaddexp(jnp.log(l), jnp.log(1.0 - l) + _log_sigmoid(f)), lb, fp)
    k = _each(lambda l, f: (1.0 - l) * _sigmoid(-f), lb, fp)
    q = _each(lambda x: _silu(x) * (HEAD_DIM ** -0.5), qp)
    r, c = _iota((C, C), 0), _iota((C, C), 1)
    row = _iota(qp[0].shape, 0)
    sums = _each(_prefix_sums, lf)
    gc = [t[0] for t in sums]
    a = _each(lambda x, y: jnp.where(r == c, _mm_nt(x, y), 0.0), q, k)
    for n, s in enumerate(_LEVELS):
        ref = [t[n + 1] for t in sums]
        upper = (row % (2 * s)) >= s
        same = (r // (2 * s)) == (c // (2 * s))
        q_s = _each(lambda x, g, m: x * jnp.exp(jnp.where(upper, g - m, NEG)), q, gc, ref)
        k_s = _each(lambda x, g, m: x * jnp.exp(jnp.where(upper, NEG, m - g)), k, gc, ref)
        a = _each(lambda t, x, y: t + jnp.where(same, _mm_nt(x, y), 0.0), a, q_s, k_s)
    o = _each(lambda t, x, g, vv, st: _mm(t, vv) + _mm(x * jnp.exp(g), st), a, q, gc, v, S)
    gl = _each(lambda g: _pick_row(g, C - 1), gc)
    S_next = _each(lambda st, l, x, g, vv: st * _row_to_col(jnp.exp(l)) + _mm_tn(x * jnp.exp(l - g), vv),
                   S, gl, k, gc, v)
    return o, S_next


N_CHIPS = 4
N_DEV = 8
_ANY = pl.BlockSpec(memory_space=pl.ANY)


def _mesh_pos():
    return lax.axis_index("x"), lax.axis_index("y"), lax.axis_index("c")


def _other_chips(x, y):
    ps = [(1 - x, y), (x, 1 - y), (1 - x, 1 - y)]
    return [(p, 2 * p[0] + p[1]) for p in ps]


def _remote(src, dst, send_sem, recv_sem, dev):
    return pltpu.make_async_remote_copy(src_ref=src, dst_ref=dst, send_sem=send_sem, recv_sem=recv_sem,
                                        device_id=dev, device_id_type=MESH)


class Carry:
    def __init__(self, ins, out_shapes, sems, start, finish):
        self.ins, self.out_shapes, self.sems, self.start, self.finish = list(ins), list(out_shapes), list(sems), start, finish


def _pcall(body, *, name, grid, in_specs, out_specs, out_shape, scratch_shapes=(), dims, args, carries=()):
    in_specs, out_specs, out_shape = list(in_specs), list(out_specs), list(out_shape)
    scratch_shapes, args = list(scratch_shapes), list(args)
    n_in, n_out, n_scr = len(in_specs), len(out_shape), len(scratch_shapes)
    carries = [c for c in carries if c is not None]
    if not carries:
        res = pl.pallas_call(body, name=name, grid=grid, in_specs=in_specs, out_specs=out_specs, out_shape=out_shape,
                             scratch_shapes=scratch_shapes, compiler_params=_cparams(dims))(*args)
        return list(res), []
    ci = [len(c.ins) for c in carries]
    co = [len(c.out_shapes) for c in carries]
    cs = [len(c.sems) for c in carries]

    def split(seq, sizes):
        out, k = [], 0
        for s in sizes:
            out.append(seq[k:k + s])
            k += s
        return out

    def carried(*refs):
        ins, cins, outs, couts, scr, sems = split(refs, [n_in, sum(ci), n_out, sum(co), n_scr, sum(cs)])
        cins, couts, sems = split(cins, ci), split(couts, co), split(sems, cs)
        ids = [pl.program_id(a) for a in range(len(grid))]
        first, last = ids[0] == 0, ids[0] == grid[0] - 1
        for a in range(1, len(grid)):
            first, last = first & (ids[a] == 0), last & (ids[a] == grid[a] - 1)

        @pl.when(first)
        def _():
            for c, i, o, s in zip(carries, cins, couts, sems):
                c.start(i, o, s)

        body(*ins, *outs, *scr)

        @pl.when(last)
        def _():
            for c, i, o, s in zip(carries, cins, couts, sems):
                c.finish(i, o, s)

    res = pl.pallas_call(
        carried, name=name, grid=grid,
        in_specs=in_specs + [_ANY] * sum(ci), out_specs=out_specs + [_ANY] * sum(co),
        out_shape=out_shape + [s for c in carries for s in c.out_shapes],
        scratch_shapes=scratch_shapes + [s for c in carries for s in c.sems],
        compiler_params=_cparams(("arbitrary",) * len(grid)),
    )(*args, *[a for c in carries for a in c.ins])
    return list(res[:n_out]), split(list(res[n_out:]), co)


def run_carries(carries, name):
    ci = [len(c.ins) for c in carries]
    co = [len(c.out_shapes) for c in carries]
    cs = [len(c.sems) for c in carries]

    def split(seq, sizes):
        out, k = [], 0
        for s in sizes:
            out.append(seq[k:k + s])
            k += s
        return out

    def body(*refs):
        cins, couts, sems = split(refs, [sum(ci), sum(co), sum(cs)])
        cins, couts, sems = split(cins, ci), split(couts, co), split(sems, cs)
        for c, i, o, s in zip(carries, cins, couts, sems):
            c.start(i, o, s)
        for c, i, o, s in zip(carries, cins, couts, sems):
            c.finish(i, o, s)

    res = pl.pallas_call(
        body, name=name, in_specs=[_ANY] * sum(ci), out_specs=[_ANY] * sum(co),
        out_shape=[s for c in carries for s in c.out_shapes], scratch_shapes=[s for c in carries for s in c.sems],
    )(*[a for c in carries for a in c.ins])
    return split(list(res), co)


def gather_carry(arrs):
    n = len(arrs)
    split = [a.ndim == 2 and a.shape[0] % 32 == 0 for a in arrs]

    def plan(ins, outs, sems):
        send_sems, recv_sems, pass_send, pass_recv, local_sems = sems
        x, y, c = _mesh_pos()
        me = 2 * x + y
        peers = _other_chips(x, y)
        locs = [pltpu.make_async_copy(ins[a], outs[a].at[me], local_sems.at[a]) for a in range(n)]
        sends, recvs, passes, pass_recvs = [], [], [], []
        for a in range(n):
            half = arrs[a].shape[0] // 2
            mine, other = pl.ds(c * half, half), pl.ds((1 - c) * half, half)
            for j, ((px, py), t) in enumerate(peers):
                sem = (send_sems.at[a, j], recv_sems.at[a, j])
                if split[a]:
                    sends.append(_remote(ins[a].at[mine], outs[a].at[me, mine], *sem, (px, py, c)))
                    recvs.append(_remote(ins[a].at[mine], outs[a].at[t, mine], *sem, (px, py, c)))
                    psem = (pass_send.at[a, j], pass_recv.at[a, j])
                    passes.append(_remote(outs[a].at[t, mine], outs[a].at[t, mine], *psem, (x, y, 1 - c)))
                    pass_recvs.append(_remote(outs[a].at[t, other], outs[a].at[t, other], *psem, (x, y, 1 - c)))
                else:
                    sends.append(_remote(ins[a], outs[a].at[me], *sem, (px, py, c)))
                    recvs.append(_remote(ins[a], outs[a].at[t], *sem, (px, py, c)))
                    passes.append(None)
                    pass_recvs.append(None)
        return locs, sends, recvs, passes, pass_recvs

    def start(ins, outs, sems):
        locs, sends, _, _, _ = plan(ins, outs, sems)
        for cp in locs + sends:
            cp.start()

    def finish(ins, outs, sems):
        locs, sends, recvs, passes, pass_recvs = plan(ins, outs, sems)
        for r, p in zip(recvs, passes):
            r.wait_recv()
            if p is not None:
                p.start()
        for p in pass_recvs:
            if p is not None:
                p.wait_recv()
        for r in sends + [p for p in passes if p is not None]:
            r.wait_send()
        for cp in locs:
            cp.wait()

    return Carry(arrs, [jax.ShapeDtypeStruct((N_CHIPS,) + a.shape, a.dtype) for a in arrs],
                 [pltpu.SemaphoreType.DMA((n, 3)), pltpu.SemaphoreType.DMA((n, 3)), pltpu.SemaphoreType.DMA((n, 3)),
                  pltpu.SemaphoreType.DMA((n, 3)), pltpu.SemaphoreType.DMA((n,))], start, finish)


def sibling_half_carry(gs):
    n = len(gs)

    def copies(ins, outs, sems):
        send_sems, recv_sems = sems
        x, y, c = _mesh_pos()
        out = []
        for a in range(n):
            half = gs[a].shape[1] // 2
            out.append(_remote(ins[a].at[:, pl.ds((1 - c) * half, half), :], outs[a], send_sems.at[a], recv_sems.at[a],
                               (x, y, 1 - c)))
        return out

    def start(ins, outs, sems):
        for r in copies(ins, outs, sems):
            r.start()

    def finish(ins, outs, sems):
        cps = copies(ins, outs, sems)
        for r in cps:
            r.wait_recv()
        for r in cps:
            r.wait_send()

    return Carry(gs, [jax.ShapeDtypeStruct((g.shape[0], g.shape[1] // 2, g.shape[2]), g.dtype) for g in gs],
                 [pltpu.SemaphoreType.DMA((n,)), pltpu.SemaphoreType.DMA((n,))], start, finish)


def chips_carry(ps):
    n = len(ps)

    def copies(ins, outs, sems):
        send_sems, recv_sems, local_sems = sems
        x, y, c = _mesh_pos()
        me = 2 * x + y
        peers = _other_chips(x, y)
        locs = [pltpu.make_async_copy(ins[a].at[me], outs[a].at[me], local_sems.at[a]) for a in range(n)]
        sends = [_remote(ins[a].at[t], outs[a].at[me], send_sems.at[a, j], recv_sems.at[a, j], (px, py, c))
                 for a in range(n) for j, ((px, py), t) in enumerate(peers)]
        recvs = [_remote(ins[a].at[t], outs[a].at[t], send_sems.at[a, j], recv_sems.at[a, j], (px, py, c))
                 for a in range(n) for j, ((px, py), t) in enumerate(peers)]
        return locs, sends, recvs

    def start(ins, outs, sems):
        locs, sends, _ = copies(ins, outs, sems)
        for cp in locs + sends:
            cp.start()

    def finish(ins, outs, sems):
        locs, sends, recvs = copies(ins, outs, sems)
        for r in recvs:
            r.wait_recv()
        for r in sends:
            r.wait_send()
        for cp in locs:
            cp.wait()

    return Carry(ps, [jax.ShapeDtypeStruct(p.shape, p.dtype) for p in ps],
                 [pltpu.SemaphoreType.DMA((n, 3)), pltpu.SemaphoreType.DMA((n, 3)), pltpu.SemaphoreType.DMA((n,))],
                 start, finish)


def sibling_copy_carry(arrs):
    n = len(arrs)

    def copies(ins, outs, sems):
        send_sems, recv_sems = sems
        x, y, c = _mesh_pos()
        return [_remote(ins[a], outs[a], send_sems.at[a], recv_sems.at[a], (x, y, 1 - c)) for a in range(n)]

    def start(ins, outs, sems):
        for r in copies(ins, outs, sems):
            r.start()

    def finish(ins, outs, sems):
        cps = copies(ins, outs, sems)
        for r in cps:
            r.wait_recv()
        for r in cps:
            r.wait_send()

    return Carry(arrs, [jax.ShapeDtypeStruct(a.shape, a.dtype) for a in arrs],
                 [pltpu.SemaphoreType.DMA((n,)), pltpu.SemaphoreType.DMA((n,))], start, finish)


def gather_all_carry(small):
    flips = [(fx, fy, fc) for fx in (0, 1) for fy in (0, 1) for fc in (0, 1)][1:]

    def copies(ins, outs, sems):
        send_sems, recv_sems, local_sem = sems
        in_ref, out_ref = ins[0], outs[0]
        x, y, c = _mesh_pos()
        me = 4 * x + 2 * y + c
        peers = [((1 - x) if fx else x, (1 - y) if fy else y, (1 - c) if fc else c) for fx, fy, fc in flips]
        loc = pltpu.make_async_copy(in_ref, out_ref.at[me], local_sem.at[0])
        sends = [_remote(in_ref, out_ref.at[me], send_sems.at[k], recv_sems.at[k], p) for k, p in enumerate(peers)]
        recvs = [_remote(in_ref, out_ref.at[4 * p[0] + 2 * p[1] + p[2]], send_sems.at[k], recv_sems.at[k], p)
                 for k, p in enumerate(peers)]
        return loc, sends, recvs

    def start(ins, outs, sems):
        loc, sends, _ = copies(ins, outs, sems)
        loc.start()
        for r in sends:
            r.start()

    def finish(ins, outs, sems):
        loc, sends, recvs = copies(ins, outs, sems)
        for r in recvs:
            r.wait_recv()
        for r in sends:
            r.wait_send()
        loc.wait()

    return Carry([small], [jax.ShapeDtypeStruct((N_DEV,) + small.shape, small.dtype)],
                 [pltpu.SemaphoreType.DMA((N_DEV - 1,)), pltpu.SemaphoreType.DMA((N_DEV - 1,)),
                  pltpu.SemaphoreType.DMA((1,))], start, finish)


HB = 8
HBW = HB * HEAD_DIM


def _chunk_specs(nc, col0, rev=False):
    return pl.BlockSpec((CHUNK, HBW), lambda b, n, g: (b * nc + ((nc - 1 - n) if rev else n), col0 // HB + g))


def _chunk_specs_rev(nc, col0):
    return _chunk_specs(nc, col0, True)


def _halo_spec(nc, col0, rev):
    per = CHUNK // HALO

    def imap(b, n, g):
        nn = (nc - 1 - n) if rev else n
        return (jnp.maximum((b * nc + nn) * per - 1, 0), col0 // HB + g)
    return pl.BlockSpec((HALO, HBW), imap)


def _const_spec(shape):
    nd = len(shape)
    return pl.BlockSpec(shape, lambda b, n, g: (0,) * nd)


def _state_spec(nc, rev=False):
    return pl.BlockSpec((None, None, HB, HEAD_DIM, HEAD_DIM),
                        lambda b, n, g: (b, (nc - 1 - n) if rev else n, g, 0, 0))


def _lanes(hh):
    return slice(hh * HEAD_DIM, (hh + 1) * HEAD_DIM)


def _head(g, hh):
    return hh if HB == HEADS else g * HB + hh


def gdn_forward(proj, conv_w, alog, dtb, onorm, bsz, name, carries=()):
    n_tok = proj.shape[0]
    nc = n_tok // bsz // CHUNK

    def body(xq, xk, xv, hq, hk, hv, gate, ab, cwq, cwk, cwv, al, dt, on, o_ref, ssave_ref, s_ref):
        n, g = pl.program_id(1), pl.program_id(2)
        keep = jnp.where(n > 0, 1.0, 0.0).astype(F32)

        @pl.when(n == 0)
        def _():
            for hh in range(HB):
                s_ref[_head(g, hh)] = jnp.zeros((HEAD_DIM, HEAD_DIM), F32)

        heads = [_head(g, hh) for hh in range(HB)]
        per_head = lambda ref, scale=None: [ref[:, _lanes(hh)] if scale is None else ref[:, _lanes(hh)] * scale
                                            for hh in range(HB)]
        S = [s_ref[h] for h in heads]
        for hh in range(HB):
            ssave_ref[hh] = S[hh]
        outs, s_next = _gdn_chunk(per_head(xq), per_head(xk), per_head(xv), per_head(hq, keep), per_head(hk, keep),
                                  per_head(hv, keep), per_head(gate), ab[...], S, per_head(cwq), per_head(cwk),
                                  per_head(cwv), al[...], dt[...], on[...], heads=heads)
        for hh in range(HB):
            o_ref[:, _lanes(hh)] = outs[hh].astype(o_ref.dtype)
            s_ref[heads[hh]] = s_next[hh]

    hb = HEADS
    cw_spec = lambda col0: pl.BlockSpec((CONV_K, HBW), lambda b, n, g: (0, col0 // HB + g))
    in_specs = [_chunk_specs(nc, 0), _chunk_specs(nc, hb), _chunk_specs(nc, 2 * hb),
                _halo_spec(nc, 0, False), _halo_spec(nc, hb, False), _halo_spec(nc, 2 * hb, False),
                _chunk_specs(nc, 3 * hb),
                pl.BlockSpec((CHUNK, 128), lambda b, n, g: (b * nc + n, 4 * hb)),
                cw_spec(0), cw_spec(hb), cw_spec(2 * hb),
                _const_spec((1, 128)), _const_spec((1, 128)), _const_spec((1, 128))]
    out_specs = [_chunk_specs(nc, 0), _state_spec(nc)]
    return _pcall(
        body, name=name, grid=(bsz, nc, HEADS // HB), in_specs=in_specs, out_specs=out_specs,
        out_shape=(jax.ShapeDtypeStruct((n_tok, D_MODEL), BF16),
                   jax.ShapeDtypeStruct((bsz, nc, HEADS, HEAD_DIM, HEAD_DIM), F32)),
        scratch_shapes=[pltpu.VMEM((HEADS, HEAD_DIM, HEAD_DIM), F32)],
        dims=("arbitrary", "arbitrary", "arbitrary"),
        args=(proj, proj, proj, proj, proj, proj, proj, proj, conv_w, conv_w, conv_w, alog, dtb, onorm),
        carries=carries)


def gdn_backward(proj, conv_w, alog, dtb, onorm, s_saved, d_out, bsz, name, carries=()):
    n_tok = proj.shape[0]
    nc = n_tok // bsz // CHUNK
    assert HB == HEADS

    def body(xq, xk, xv, hq, hk, hv, gate, ab, cwq, cwk, cwv, al, dt, on, ssave, do,
             dp_ref, dcw_ref, dal_ref, ddt_ref, don_ref, ds_ref, dhalo_ref):
        b, n, g = pl.program_id(0), pl.program_id(1), pl.program_id(2)
        nr = nc - 1 - n

        @pl.when((b == 0) & (n == 0) & (g == 0))
        def _():
            dcw_ref[...] = jnp.zeros_like(dcw_ref)
            dal_ref[...] = jnp.zeros_like(dal_ref)
            ddt_ref[...] = jnp.zeros_like(ddt_ref)
            don_ref[...] = jnp.zeros_like(don_ref)

        @pl.when(n == 0)
        def _():
            for hh in range(HB):
                ds_ref[_head(g, hh)] = jnp.zeros((HEAD_DIM, HEAD_DIM), F32)
                dhalo_ref[_head(g, hh)] = jnp.zeros((3, HALO, HEAD_DIM), F32)

        keep = jnp.where(nr > 0, 1.0, 0.0).astype(F32)
        pad = jnp.zeros((CHUNK - HALO, HEAD_DIM), F32)
        heads = [_head(g, hh) for hh in range(HB)]
        per_head = lambda ref, scale=None: [ref[:, _lanes(hh)] if scale is None else ref[:, _lanes(hh)] * scale
                                            for hh in range(HB)]
        args = (per_head(xq), per_head(xk), per_head(xv), per_head(hq, keep), per_head(hk, keep), per_head(hv, keep),
                per_head(gate), ab[...], [ssave[hh] for hh in range(HB)], per_head(cwq), per_head(cwk), per_head(cwv),
                al[...], dt[...], on[...])
        _, vjp = jax.vjp(functools.partial(_gdn_chunk, heads=heads), *args)
        (gxq, gxk, gxv, ghq, ghk, ghv, ggate, gab_sum, gS, gcq, gck, gcv, gal_sum, gdt_sum, gon_sum) = vjp(
            ([do[:, _lanes(hh)] for hh in range(HB)], [ds_ref[h] for h in heads]))
        d = D_MODEL
        for hh in range(HB):
            head, c0 = heads[hh], hh * HEAD_DIM
            for part, gx in enumerate((gxq, gxk, gxv)):
                full = gx[hh] + jnp.concatenate([pad, dhalo_ref[head, part]], axis=0)
                dp_ref[:, part * d + c0:part * d + c0 + HEAD_DIM] = full.astype(dp_ref.dtype)
            dp_ref[:, 3 * d + c0:3 * d + c0 + HEAD_DIM] = ggate[hh].astype(dp_ref.dtype)
            dcw_ref[head, 0] += gcq[hh]
            dcw_ref[head, 1] += gck[hh]
            dcw_ref[head, 2] += gcv[hh]
            ds_ref[head] = gS[hh]
            dhalo_ref[head, 0] = ghq[hh] * keep
            dhalo_ref[head, 1] = ghk[hh] * keep
            dhalo_ref[head, 2] = ghv[hh] * keep

        dp_ref[:, GDN_MAIN:GDN_IN_PAD] = gab_sum.astype(dp_ref.dtype)
        dal_ref[...] += gal_sum
        ddt_ref[...] += gdt_sum
        don_ref[...] += gon_sum

    hb = HEADS
    cw_spec = lambda col0: pl.BlockSpec((CONV_K, HBW), lambda b, n, g: (0, col0 // HB + g))
    in_specs = [_chunk_specs_rev(nc, 0), _chunk_specs_rev(nc, hb), _chunk_specs_rev(nc, 2 * hb),
                _halo_spec(nc, 0, True), _halo_spec(nc, hb, True), _halo_spec(nc, 2 * hb, True),
                _chunk_specs_rev(nc, 3 * hb),
                pl.BlockSpec((CHUNK, 128), lambda b, n, g: (b * nc + (nc - 1 - n), 4 * hb)),
                cw_spec(0), cw_spec(hb), cw_spec(2 * hb),
                _const_spec((1, 128)), _const_spec((1, 128)), _const_spec((1, 128)),
                _state_spec(nc, True),
                _chunk_specs_rev(nc, 0)]
    out_specs = [
        pl.BlockSpec((CHUNK, GDN_IN_PAD), lambda b, n, g: (b * nc + (nc - 1 - n), 0)),
        _const_spec((HEADS, 3, CONV_K, HEAD_DIM)), _const_spec((1, 128)), _const_spec((1, 128)), _const_spec((1, 128))]
    row = jax.ShapeDtypeStruct((1, 128), F32)
    return _pcall(
        body, name=name, grid=(bsz, nc, HEADS // HB), in_specs=in_specs, out_specs=out_specs,
        out_shape=(jax.ShapeDtypeStruct((n_tok, GDN_IN_PAD), BF16),
                   jax.ShapeDtypeStruct((HEADS, 3, CONV_K, HEAD_DIM), F32), row, row, row),
        scratch_shapes=[pltpu.VMEM((HEADS, HEAD_DIM, HEAD_DIM), F32), pltpu.VMEM((HEADS, 3, HALO, HEAD_DIM), F32)],
        dims=("arbitrary", "arbitrary", "arbitrary"),
        args=(proj, proj, proj, proj, proj, proj, proj, proj, conv_w, conv_w, conv_w, alog, dtb, onorm, s_saved, d_out),
        carries=carries)


def hgrn_forward(proj, lbl, layer, bsz, carries=()):
    n_tok = proj.shape[0]
    nc = n_tok // bsz // CHUNK

    def body(qp, fp, vi, lb, o_ref, ssave_ref, s_ref):
        n, g = pl.program_id(1), pl.program_id(2)

        @pl.when(n == 0)
        def _():
            for hh in range(HB):
                s_ref[_head(g, hh)] = jnp.zeros((HEAD_DIM, HEAD_DIM), F32)

        heads = [_head(g, hh) for hh in range(HB)]
        per_head = lambda ref: [ref[:, _lanes(hh)] for hh in range(HB)]
        S = [s_ref[h] for h in heads]
        for hh in range(HB):
            ssave_ref[hh] = S[hh]
        outs, s_next = _hgrn_chunk(per_head(qp), per_head(fp), per_head(vi), S, per_head(lb), layer=layer)
        for hh in range(HB):
            o_ref[:, _lanes(hh)] = outs[hh]
            s_ref[heads[hh]] = s_next[hh]

    hb = HEADS
    in_specs = [_chunk_specs(nc, 0), _chunk_specs(nc, hb), _chunk_specs(nc, 2 * hb),
                pl.BlockSpec((DEPTH, HBW), lambda b, n, g: (0, g))]
    out_specs = [_chunk_specs(nc, 0), _state_spec(nc)]
    return _pcall(
        body, name=f"hgrn_fwd{layer}", grid=(bsz, nc, HEADS // HB), in_specs=in_specs, out_specs=out_specs,
        out_shape=(jax.ShapeDtypeStruct((n_tok, D_MODEL), F32),
                   jax.ShapeDtypeStruct((bsz, nc, HEADS, HEAD_DIM, HEAD_DIM), F32)),
        scratch_shapes=[pltpu.VMEM((HEADS, HEAD_DIM, HEAD_DIM), F32)],
        dims=("arbitrary", "arbitrary", "arbitrary"), args=(proj, proj, proj, lbl), carries=carries)


def hgrn_backward(proj, lbl, s_saved, d_o, d_gate, layer, bsz, carries=()):
    n_tok = proj.shape[0]
    nc = n_tok // bsz // CHUNK
    assert HB == HEADS

    def body(qp, fp, vi, lb, ssave, do, dgt, dp_ref, dlb_ref, ds_ref):
        b, n, g = pl.program_id(0), pl.program_id(1), pl.program_id(2)

        @pl.when((b == 0) & (n == 0) & (g == 0))
        def _():
            dlb_ref[...] = jnp.zeros_like(dlb_ref)

        @pl.when(n == 0)
        def _():
            for hh in range(HB):
                ds_ref[_head(g, hh)] = jnp.zeros((HEAD_DIM, HEAD_DIM), F32)

        heads = [_head(g, hh) for hh in range(HB)]
        per_head = lambda ref: [ref[:, _lanes(hh)] for hh in range(HB)]
        _, vjp = jax.vjp(functools.partial(_hgrn_chunk, layer=layer), per_head(qp), per_head(fp), per_head(vi),
                         [ssave[hh] for hh in range(HB)], per_head(lb))
        gq, gf, gv, gS, glb = vjp((per_head(do), [ds_ref[h] for h in heads]))
        for hh in range(HB):
            head, c0, d = heads[hh], hh * HEAD_DIM, D_MODEL
            dp_ref[:, c0:c0 + HEAD_DIM] = gq[hh].astype(dp_ref.dtype)
            dp_ref[:, d + c0:d + c0 + HEAD_DIM] = gf[hh].astype(dp_ref.dtype)
            dp_ref[:, 2 * d + c0:2 * d + c0 + HEAD_DIM] = gv[hh].astype(dp_ref.dtype)
            dlb_ref[head] += glb[hh]
            ds_ref[head] = gS[hh]
        dp_ref[:, 3 * D_MODEL:4 * D_MODEL] = dgt[...]

    hb = HEADS
    in_specs = [_chunk_specs_rev(nc, 0), _chunk_specs_rev(nc, hb), _chunk_specs_rev(nc, 2 * hb),
                pl.BlockSpec((DEPTH, HBW), lambda b, n, g: (0, g)),
                _state_spec(nc, True),
                _chunk_specs_rev(nc, 0), _chunk_specs_rev(nc, 0)]
    out_specs = [pl.BlockSpec((CHUNK, 4 * D_MODEL), lambda b, n, g: (b * nc + (nc - 1 - n), 0)),
                 _const_spec((HEADS, DEPTH, HEAD_DIM))]
    return _pcall(
        body, name=f"hgrn_bwd{layer}", grid=(bsz, nc, HEADS // HB), in_specs=in_specs, out_specs=out_specs,
        out_shape=(jax.ShapeDtypeStruct((n_tok, 4 * D_MODEL), BF16), jax.ShapeDtypeStruct((HEADS, DEPTH, HEAD_DIM), F32)),
        scratch_shapes=[pltpu.VMEM((HEADS, HEAD_DIM, HEAD_DIM), F32)],
        dims=("arbitrary", "arbitrary", "arbitrary"), args=(proj, proj, proj, lbl, s_saved, d_o, d_gate),
        carries=carries)


ROW_TILE = 512
MM_ROW_TILE = 1024
MM_VMEM_BUDGET = 36 * 1024 * 1024


def _tile(n):
    for cand in (1024, 512, 1408, 384, 256, 128):
        if n % cand == 0:
            return cand
    return n


def _rmsnorm(x, w):
    return x * lax.rsqrt(jnp.mean(x * x, axis=1, keepdims=True) + EPS) * w


def norm_matmul(h, nw, w, relu2, name, carries=()):
    n_tok, d = h.shape
    slots = w.ndim == 3
    n_out = w.shape[0] * w.shape[2] if slots else w.shape[1]
    tm, tn = min(n_tok, MM_ROW_TILE), _tile(w.shape[2] if slots else n_out)
    if slots:
        per = w.shape[2] // tn
        w_spec = pl.BlockSpec((None, d, tn), lambda i, j: (j // per, 0, j % per))
    else:
        w_spec = pl.BlockSpec((d, tn), lambda i, j: (0, j))

    def body(h_ref, nw_ref, w_ref, y_ref, *outs):
        @pl.when(pl.program_id(1) == 0)
        def _():
            y_ref[...] = _rmsnorm(h_ref[...], nw_ref[...]).astype(BF16)

        acc = jnp.dot(y_ref[...], w_ref[...], preferred_element_type=F32)
        if relu2:
            r = jnp.maximum(acc, 0.0)
            outs[0][...] = (r * r).astype(BF16)
            outs[1][...] = r.astype(BF16)
        else:
            outs[0][...] = acc

    o_spec = pl.BlockSpec((tm, tn), lambda i, j: (i, j))
    if relu2:
        outs = (jax.ShapeDtypeStruct((n_tok, n_out), BF16),) * 2
    else:
        outs = (jax.ShapeDtypeStruct((n_tok, n_out), F32),)
    return _pcall(
        body, name=name, grid=(n_tok // tm, n_out // tn),
        in_specs=[pl.BlockSpec((tm, d), lambda i, j: (i, 0)), pl.BlockSpec((1, d), lambda i, j: (0, 0)), w_spec],
        out_specs=[pl.BlockSpec((tm, d), lambda i, j: (i, 0))] + [o_spec] * len(outs),
        out_shape=(jax.ShapeDtypeStruct((n_tok, d), BF16),) + outs,
        dims=("parallel", "arbitrary"), args=(h, nw, w), carries=carries)


def _mm_tiles(m, n, k, extra_bytes):
    tn = _tile(n)
    for tm in (1024, 512, 256, 128):
        if m % tm == 0 and 2 * (2 * tm * k + 2 * k * tn + (4 + extra_bytes) * tm * tn) <= MM_VMEM_BUDGET:
            return tm, tn
    return min(m, 128), tn


def matmul(a, b, mode, name, out_dtype=F32, extra=None, epilogue=None, shards=1, carries=()):
    slots = b.shape[0] if (mode == "nt" and b.ndim == 3) else 0
    if mode == "nn":
        (m, k), n = a.shape, b.shape[1]
    elif mode == "nt":
        (m, k), n = a.shape, (b.shape[1] if slots else b.shape[0])
    else:
        (k, m), n = a.shape, b.shape[1]
    tm, tn = _mm_tiles(m, n // shards, k, 0 if extra is None else extra.dtype.itemsize)
    a_spec = pl.BlockSpec((k, tm), lambda i, j: (0, i)) if mode == "tn" else pl.BlockSpec((tm, k), lambda i, j: (i, 0))
    if slots:
        kb = b.shape[2]
        b_specs = [pl.BlockSpec((None, tn, kb), lambda i, j, s=s: (s, j, 0)) for s in range(slots)]
    elif mode == "nt":
        b_specs = [pl.BlockSpec((tn, k), lambda i, j: (j, 0))]
    else:
        b_specs = [pl.BlockSpec((k, tn), lambda i, j: (0, j))]
    nb = len(b_specs)
    dims = _DIMS[mode]

    def body(*refs):
        a_ref, b_refs = refs[0], refs[1:1 + nb]
        e_ref = refs[1 + nb] if extra is not None else None
        o_ref = refs[-1]
        if slots:
            acc = None
            for s in range(slots):
                part = lax.dot_general(a_ref[:, s * kb:(s + 1) * kb].astype(BF16), b_refs[s][...].astype(BF16), dims,
                                       preferred_element_type=F32)
                acc = part if acc is None else acc + part
        else:
            acc = lax.dot_general(a_ref[...].astype(BF16), b_refs[0][...].astype(BF16), dims,
                                  preferred_element_type=F32)
        if epilogue == "add":
            acc = e_ref[...] + acc
        elif epilogue == "mul2":
            acc = acc * (2.0 * e_ref[...].astype(F32))
        o_ref[...] = acc.astype(o_ref.dtype)

    in_specs = [a_spec] + b_specs
    args = [a] + [b] * nb
    if extra is not None:
        in_specs.append(pl.BlockSpec((tm, tn), lambda i, j: (i, j)))
        args.append(extra)
    if shards > 1:
        per = n // shards // tn
        out_spec = pl.BlockSpec((None, tm, tn), lambda i, j: (j // per, i, j % per))
        out_shape = jax.ShapeDtypeStruct((shards, m, n // shards), out_dtype)
    else:
        out_spec = pl.BlockSpec((tm, tn), lambda i, j: (i, j))
        out_shape = jax.ShapeDtypeStruct((m, n), out_dtype)
    (res,), cres = _pcall(
        body, name=name, grid=(m // tm, n // tn), in_specs=in_specs, out_specs=[out_spec], out_shape=[out_shape],
        dims=("parallel", "arbitrary"), args=args, carries=carries)
    return res, cres


def norm_backward(h, nw, dy, dres, name):
    n_tok, d = h.shape
    tm = min(n_tok, ROW_TILE)

    def body(h_ref, nw_ref, dy_ref, dr_ref, dh_ref, dhb_ref, dnw_ref):
        @pl.when(pl.program_id(0) == 0)
        def _():
            dnw_ref[...] = jnp.zeros_like(dnw_ref)

        _, vjp = jax.vjp(_rmsnorm, h_ref[...], nw_ref[...])
        gh, gw = vjp(dy_ref[...])
        dh = dr_ref[...] + gh
        dh_ref[...] = dh
        dhb_ref[...] = dh.astype(BF16)
        dnw_ref[...] += gw

    row = pl.BlockSpec((tm, d), lambda i: (i, 0))
    vec = pl.BlockSpec((1, d), lambda i: (0, 0))
    return pl.pallas_call(
        body, name=name, grid=(n_tok // tm,), in_specs=[row, vec, row, row], out_specs=[row, row, vec],
        out_shape=(jax.ShapeDtypeStruct((n_tok, d), F32), jax.ShapeDtypeStruct((n_tok, d), BF16),
                   jax.ShapeDtypeStruct((1, d), F32)),
        compiler_params=_cparams(("arbitrary",)),
    )(h, nw, dy, dres)


def _hgrn_post(o, gate, gw):
    return _rmsnorm(o, gw) * _silu(gate)


def hgrn_post_forward(o, proj, gw, name):
    n_tok, d = o.shape
    tm = min(n_tok, ROW_TILE)

    def body(o_ref, g_ref, w_ref, y_ref):
        y_ref[...] = _hgrn_post(o_ref[...], g_ref[...], w_ref[...]).astype(BF16)

    row = pl.BlockSpec((tm, d), lambda i: (i, 0))
    return pl.pallas_call(
        body, name=name, grid=(n_tok // tm,),
        in_specs=[row, pl.BlockSpec((tm, d), lambda i: (i, 3)), pl.BlockSpec((1, d), lambda i: (0, 0))],
        out_specs=row, out_shape=jax.ShapeDtypeStruct((n_tok, d), BF16),
        compiler_params=_cparams(("parallel",)),
    )(o, proj, gw)


def hgrn_post_backward(o, proj, gw, dy, name):
    n_tok, d = o.shape
    tm = min(n_tok, ROW_TILE)

    def body(o_ref, g_ref, w_ref, dy_ref, do_ref, dg_ref, dw_ref):
        @pl.when(pl.program_id(0) == 0)
        def _():
            dw_ref[...] = jnp.zeros_like(dw_ref)

        _, vjp = jax.vjp(_hgrn_post, o_ref[...], g_ref[...], w_ref[...])
        go, gg, gw_ = vjp(dy_ref[...])
        do_ref[...] = go
        dg_ref[...] = gg.astype(BF16)
        dw_ref[...] += gw_

    row = pl.BlockSpec((tm, d), lambda i: (i, 0))
    vec = pl.BlockSpec((1, d), lambda i: (0, 0))
    return pl.pallas_call(
        body, name=name, grid=(n_tok // tm,),
        in_specs=[row, pl.BlockSpec((tm, d), lambda i: (i, 3)), vec, row], out_specs=[row, row, vec],
        out_shape=(jax.ShapeDtypeStruct((n_tok, d), F32), jax.ShapeDtypeStruct((n_tok, d), BF16),
                   jax.ShapeDtypeStruct((1, d), F32)),
        compiler_params=_cparams(("arbitrary",)),
    )(o, proj, gw, dy)


def loss_head(h, nw, target):
    n_tok, d = h.shape
    tm = min(n_tok, ROW_TILE)

    def body(h_ref, nw_ref, t_ref, loss_ref, dh_ref, dhb_ref, dnw_ref):
        @pl.when(pl.program_id(0) == 0)
        def _():
            dnw_ref[...] = jnp.zeros_like(dnw_ref)
            loss_ref[...] = jnp.zeros_like(loss_ref)

        out, vjp = jax.vjp(_rmsnorm, h_ref[...], nw_ref[...])
        err = out - t_ref[...]
        part = 0.5 * jnp.sum(jnp.sum(err * err, axis=1, keepdims=True), axis=0, keepdims=True) / d
        loss_ref[...] += jnp.broadcast_to(part, loss_ref.shape)
        gh, gw = vjp(err / d)
        dh_ref[...] = gh
        dhb_ref[...] = gh.astype(BF16)
        dnw_ref[...] += gw

    row = pl.BlockSpec((tm, d), lambda i: (i, 0))
    vec = pl.BlockSpec((1, d), lambda i: (0, 0))
    return pl.pallas_call(
        body, name="loss_head", grid=(n_tok // tm,), in_specs=[row, vec, row],
        out_specs=[pl.BlockSpec((1, 128), lambda i: (0, 0)), row, row, vec],
        out_shape=(jax.ShapeDtypeStruct((1, 128), F32), jax.ShapeDtypeStruct((n_tok, d), F32),
                   jax.ShapeDtypeStruct((n_tok, d), BF16), jax.ShapeDtypeStruct((1, d), F32)),
        compiler_params=_cparams(("arbitrary",)),
    )(h, nw, target)


def _rows2d(shape):
    if len(shape) == 1:
        return (1, shape[0])
    return (math.prod(shape[:-1]), shape[-1])


def adamw(w, g, m, v, name):
    shape = w.shape
    r, c = _rows2d(shape)
    tr = r if r <= 256 else 256
    c1 = 1.0 / (1.0 - ADAM_B1 ** ADAM_STEP)
    c2 = 1.0 / (1.0 - ADAM_B2 ** ADAM_STEP)

    def body(w_ref, g_ref, m_ref, v_ref, d_ref, nm_ref, nv_ref):
        gg = g_ref[...]
        nm = ADAM_B1 * m_ref[...] + (1.0 - ADAM_B1) * gg
        nv = ADAM_B2 * v_ref[...] + (1.0 - ADAM_B2) * (gg * gg)
        d_ref[...] = -ADAM_LR * ((nm * c1) / (jnp.sqrt(nv * c2) + ADAM_EPS) + ADAM_WD * w_ref[...])
        nm_ref[...] = nm
        nv_ref[...] = nv

    spec = pl.BlockSpec((tr, c), lambda i: (i, 0))
    sds = jax.ShapeDtypeStruct((r, c), F32)
    outs = pl.pallas_call(
        body, name=name, grid=(r // tr,), in_specs=[spec] * 4, out_specs=[spec] * 3, out_shape=(sds,) * 3,
        compiler_params=_cparams(("parallel",)),
    )(w.reshape(r, c), g.reshape(r, c), m.reshape(r, c), v.reshape(r, c))
    return tuple(o.reshape(shape) for o in outs)


def add_slots(parts, name):
    s, r, c = parts.shape
    tr = r if r <= 256 else 256

    def body(p_ref, o_ref):
        acc = p_ref[0]
        for t in range(1, s):
            acc = acc + p_ref[t]
        o_ref[...] = acc

    return pl.pallas_call(
        body, name=name, grid=(r // tr,), in_specs=[pl.BlockSpec((s, tr, c), lambda i: (0, i, 0))],
        out_specs=pl.BlockSpec((tr, c), lambda i: (i, 0)), out_shape=jax.ShapeDtypeStruct((r, c), F32),
        compiler_params=_cparams(("parallel",)),
    )(parts)


def add_pair(a, b, name):
    s, r, c = a.shape
    tr = r if r <= 256 else 256

    def body(a_ref, b_ref, o_ref):
        o_ref[...] = a_ref[...] + b_ref[...]

    spec = pl.BlockSpec((None, tr, c), lambda t, i: (t, i, 0))
    return pl.pallas_call(
        body, name=name, grid=(s, r // tr), in_specs=[spec, spec], out_specs=spec,
        out_shape=jax.ShapeDtypeStruct((s, r, c), F32), compiler_params=_cparams(("parallel", "parallel")),
    )(a, b)


N_CHIPS = 4
N_DEV = 8
_ANY = pl.BlockSpec(memory_space=pl.ANY)


def _mesh_pos():
    return lax.axis_index("x"), lax.axis_index("y"), lax.axis_index("c")


def _other_chips(x, y):
    ps = [(1 - x, y), (x, 1 - y), (1 - x, 1 - y)]
    return [(p, 2 * p[0] + p[1]) for p in ps]


def _remote(src, dst, send_sem, recv_sem, dev):
    return pltpu.make_async_remote_copy(src_ref=src, dst_ref=dst, send_sem=send_sem, recv_sem=recv_sem,
                                        device_id=dev, device_id_type=MESH)


def gather_chips(arrs, name):
    n = len(arrs)

    def body(*refs):
        ins, outs = refs[:n], refs[n:2 * n]
        send_sems, recv_sems, local_sems = refs[2 * n:]
        x, y, c = _mesh_pos()
        me = 2 * x + y
        peers = _other_chips(x, y)
        locs, sends = [], []
        for a in range(n):
            cp = pltpu.make_async_copy(ins[a], outs[a].at[me], local_sems.at[a])
            cp.start()
            locs.append(cp)
            for j, ((px, py), _) in enumerate(peers):
                r = _remote(ins[a], outs[a].at[me], send_sems.at[a, j], recv_sems.at[a, j], (px, py, c))
                r.start()
                sends.append(r)
        for a in range(n):
            for j, ((px, py), t) in enumerate(peers):
                _remote(ins[a], outs[a].at[t], send_sems.at[a, j], recv_sems.at[a, j], (px, py, c)).wait_recv()
        for r in sends:
            r.wait_send()
        for cp in locs:
            cp.wait()

    return pl.pallas_call(
        body, name=name, in_specs=[_ANY] * n, out_specs=[_ANY] * n,
        out_shape=[jax.ShapeDtypeStruct((N_CHIPS,) + a.shape, a.dtype) for a in arrs],
        scratch_shapes=[pltpu.SemaphoreType.DMA((n, 3)), pltpu.SemaphoreType.DMA((n, 3)), pltpu.SemaphoreType.DMA((n,))],
    )(*arrs)


def exchange_sibling_half(gs, name):
    n = len(gs)

    def body(*refs):
        ins, outs = refs[:n], refs[n:2 * n]
        send_sems, recv_sems = refs[2 * n:]
        x, y, c = _mesh_pos()
        sends = []
        for a in range(n):
            half = gs[a].shape[1] // 2
            src = ins[a].at[:, pl.ds((1 - c) * half, half), :]
            r = _remote(src, outs[a], send_sems.at[a], recv_sems.at[a], (x, y, 1 - c))
            r.start()
            sends.append(r)
        for r in sends:
            r.wait_recv()
        for r in sends:
            r.wait_send()

    return pl.pallas_call(
        body, name=name, in_specs=[_ANY] * n, out_specs=[_ANY] * n,
        out_shape=[jax.ShapeDtypeStruct((g.shape[0], g.shape[1] // 2, g.shape[2]), g.dtype) for g in gs],
        scratch_shapes=[pltpu.SemaphoreType.DMA((n,)), pltpu.SemaphoreType.DMA((n,))],
    )(*gs)


def exchange_chips(ps, name):
    n = len(ps)

    def body(*refs):
        ins, outs = refs[:n], refs[n:2 * n]
        send_sems, recv_sems, local_sems = refs[2 * n:]
        x, y, c = _mesh_pos()
        me = 2 * x + y
        peers = _other_chips(x, y)
        locs, sends = [], []
        for a in range(n):
            cp = pltpu.make_async_copy(ins[a].at[me], outs[a].at[me], local_sems.at[a])
            cp.start()
            locs.append(cp)
            for j, ((px, py), t) in enumerate(peers):
                r = _remote(ins[a].at[t], outs[a].at[me], send_sems.at[a, j], recv_sems.at[a, j], (px, py, c))
                r.start()
                sends.append(r)
        for a in range(n):
            for j, ((px, py), t) in enumerate(peers):
                _remote(ins[a].at[t], outs[a].at[t], send_sems.at[a, j], recv_sems.at[a, j], (px, py, c)).wait_recv()
        for r in sends:
            r.wait_send()
        for cp in locs:
            cp.wait()

    return pl.pallas_call(
        body, name=name, in_specs=[_ANY] * n, out_specs=[_ANY] * n,
        out_shape=[jax.ShapeDtypeStruct(p.shape, p.dtype) for p in ps],
        scratch_shapes=[pltpu.SemaphoreType.DMA((n, 3)), pltpu.SemaphoreType.DMA((n, 3)), pltpu.SemaphoreType.DMA((n,))],
    )(*ps)


def share_sibling(groups, name):
    flat = [f for grp in groups for f in grp]
    n = len(flat)
    nw = len(groups)

    def body(*refs):
        ins, outs = refs[:n], refs[n:n + nw]
        send_sems, recv_sems, local_sems = refs[n + nw:]
        x, y, c = _mesh_pos()
        locs, sends, k = [], [], 0
        for w, grp in enumerate(groups):
            for l in range(len(grp)):
                half = grp[l].shape[0]
                dst = outs[w].at[l, pl.ds(c * half, half), :]
                cp = pltpu.make_async_copy(ins[k], dst, local_sems.at[k])
                cp.start()
                locs.append(cp)
                r = _remote(ins[k], dst, send_sems.at[k], recv_sems.at[k], (x, y, 1 - c))
                r.start()
                sends.append(r)
                k += 1
        k = 0
        for w, grp in enumerate(groups):
            for l in range(len(grp)):
                half = grp[l].shape[0]
                theirs = outs[w].at[l, pl.ds((1 - c) * half, half), :]
                _remote(ins[k], theirs, send_sems.at[k], recv_sems.at[k], (x, y, 1 - c)).wait_recv()
                k += 1
        for r in sends:
            r.wait_send()
        for cp in locs:
            cp.wait()

    return pl.pallas_call(
        body, name=name, in_specs=[_ANY] * n, out_specs=[_ANY] * nw,
        out_shape=[jax.ShapeDtypeStruct((len(grp), 2 * grp[0].shape[0], grp[0].shape[1]), F32) for grp in groups],
        scratch_shapes=[pltpu.SemaphoreType.DMA((n,)), pltpu.SemaphoreType.DMA((n,)), pltpu.SemaphoreType.DMA((n,))],
    )(*flat)


def gather_all(small, name):
    flips = [(fx, fy, fc) for fx in (0, 1) for fy in (0, 1) for fc in (0, 1)][1:]

    def body(in_ref, out_ref, send_sems, recv_sems, local_sem):
        x, y, c = _mesh_pos()
        me = 4 * x + 2 * y + c
        cp = pltpu.make_async_copy(in_ref, out_ref.at[me], local_sem)
        cp.start()
        peers = [((1 - x) if fx else x, (1 - y) if fy else y, (1 - c) if fc else c) for fx, fy, fc in flips]
        sends = []
        for k, p in enumerate(peers):
            r = _remote(in_ref, out_ref.at[me], send_sems.at[k], recv_sems.at[k], p)
            r.start()
            sends.append(r)
        for k, p in enumerate(peers):
            _remote(in_ref, out_ref.at[4 * p[0] + 2 * p[1] + p[2]], send_sems.at[k], recv_sems.at[k], p).wait_recv()
        for r in sends:
            r.wait_send()
        cp.wait()

    return pl.pallas_call(
        body, name=name, in_specs=[_ANY], out_specs=_ANY,
        out_shape=jax.ShapeDtypeStruct((N_DEV,) + small.shape, small.dtype),
        scratch_shapes=[pltpu.SemaphoreType.DMA((N_DEV - 1,)), pltpu.SemaphoreType.DMA((N_DEV - 1,)),
                        pltpu.SemaphoreType.DMA],
    )(small)


BIG = ("gdn_w_in", "gdn_w_out", "hgrn_w_in", "hgrn_w_out", "mlp_w_up", "mlp_w_down")
WEIGHTS = ("gdn_w_in", "gdn_conv", "gdn_a_log", "gdn_dt_bias", "gdn_onorm", "gdn_w_out", "hgrn_w_in", "hgrn_lb_logits",
           "hgrn_gnorm", "hgrn_w_out", "norm_mix", "norm_mlp", "mlp_w_up", "mlp_w_down", "norm_final")
SMALL_ROWS = 96


def _pad_lanes(v, n):
    return jnp.pad(v, [(0, 0)] * (v.ndim - 1) + [(0, n - v.shape[-1])])


def _local_grads(x, target, w):
    bsz, t_len, d = x.shape
    n_tok = bsz * t_len
    h = x.reshape(n_tok, d)
    lbl = w["hgrn_lb_logits"]
    saved = []
    for i in range(DEPTH):
        j = i // 2
        nmix = w["norm_mix"][i][None, :]
        if i % 2 == 0:
            y, proj = norm_matmul(h, nmix, w["gdn_w_in"][j], False, f"in_proj{i}")
            al = _pad_lanes(w["gdn_a_log"][j][None, :], 128)
            dtb = _pad_lanes(w["gdn_dt_bias"][j][None, :], 128)
            on = w["gdn_onorm"][j][None, :]
            og, ssave = gdn_forward(proj, w["gdn_conv"][j], al, dtb, on, bsz)
            mix = (proj, ssave, al, dtb, on)
            w_out = w["gdn_w_out"][j]
        else:
            y, proj = norm_matmul(h, nmix, w["hgrn_w_in"][j], False, f"in_proj{i}")
            o, ssave = hgrn_forward(proj, lbl, i, bsz)
            gn = w["hgrn_gnorm"][j][None, :]
            og = hgrn_post_forward(o, proj, gn, f"hgrn_post{i}")
            mix = (proj, ssave, o, gn)
            w_out = w["hgrn_w_out"][j]
        h1 = matmul(og, w_out, "nn", f"out_proj{i}", extra=h, epilogue="add")
        nmlp = w["norm_mlp"][i][None, :]
        z, a, r = norm_matmul(h1, nmlp, w["mlp_w_up"][i], True, f"mlp_up{i}")
        h2 = matmul(a, w["mlp_w_down"][i], "nn", f"mlp_down{i}", extra=h1, epilogue="add")
        saved.append((h, nmix, y, mix, og, w_out, h1, nmlp, z, a, r))
        h = h2
    loss_row, dh, d_nf = loss_head(h, w["norm_final"][None, :], target.reshape(n_tok, d))

    big = {k: [None] * (DEPTH if k.startswith("mlp") else DEPTH // 2) for k in BIG}
    d_nmix, d_nmlp = [None] * DEPTH, [None] * DEPTH
    d_conv, d_alog, d_dtb, d_onorm, d_gnorm = [None] * 2, [None] * 2, [None] * 2, [None] * 2, [None] * 2
    d_lbl = jnp.zeros((DEPTH, d), F32)
    for i in reversed(range(DEPTH)):
        j = i // 2
        h_in, nmix, y, mix, og, w_out, h1, nmlp, z, a, r = saved[i]
        du = matmul(dh, w["mlp_w_down"][i], "nt", f"d_mlp_act{i}", out_dtype=BF16, extra=r, epilogue="mul2")
        big["mlp_w_down"][i] = matmul(a, dh, "tn", f"dw_down{i}").reshape(N_CHIPS, -1, d)
        dz = matmul(du, w["mlp_w_up"][i], "nt", f"d_mlp_in{i}")
        big["mlp_w_up"][i] = matmul(z, du, "tn", f"dw_up{i}", shards=N_CHIPS)
        dh1, d_nmlp[i] = norm_backward(h1, nmlp, dz, dh, f"d_norm_mlp{i}")
        dog = matmul(dh1, w_out, "nt", f"d_mix_out{i}")
        dw_out = matmul(og, dh1, "tn", f"dw_out{i}").reshape(N_CHIPS, -1, d)
        if i % 2 == 0:
            proj, ssave, al, dtb, on = mix
            dq, dk, dv, dg, dab, dcw, dal, ddt, don = gdn_backward(proj, w["gdn_conv"][j], al, dtb, on, ssave, dog, bsz)
            dproj = jnp.concatenate([dq, dk, dv, dg, dab], axis=1)
            dy = matmul(dproj, w["gdn_w_in"][j], "nt", f"d_in_proj{i}")
            dw_in = matmul(y, dproj, "tn", f"dw_in{i}")[:, :GDN_IN]
            big["gdn_w_in"][j] = jnp.transpose(dw_in.reshape(d, N_CHIPS, GDN_IN // N_CHIPS), (1, 0, 2))
            big["gdn_w_out"][j] = dw_out
            d_conv[j] = jnp.transpose(dcw, (2, 1, 0, 3)).reshape(CONV_K, 3 * d)
            d_alog[j], d_dtb[j], d_onorm[j] = dal[0, :HEADS], ddt[0, :HEADS], don[0]
        else:
            proj, ssave, o, gn = mix
            do_raw, dgate, dgn = hgrn_post_backward(o, proj, gn, dog, f"d_hgrn_post{i}")
            dq, df, dv, dlb = hgrn_backward(proj, lbl, ssave, do_raw, i, bsz)
            dproj = jnp.concatenate([dq, df, dv, dgate], axis=1)
            dy = matmul(dproj, w["hgrn_w_in"][j], "nt", f"d_in_proj{i}")
            big["hgrn_w_in"][j] = matmul(y, dproj, "tn", f"dw_in{i}", shards=N_CHIPS)
            big["hgrn_w_out"][j] = dw_out
            d_gnorm[j] = dgn[0]
            d_lbl = d_lbl + jnp.transpose(dlb, (1, 0, 2)).reshape(DEPTH, d)
        dh, d_nmix[i] = norm_backward(h_in, nmix, dy, dh1, f"d_norm_mix{i}")
    small = {
        "gdn_conv": jnp.stack(d_conv), "gdn_a_log": jnp.stack(d_alog), "gdn_dt_bias": jnp.stack(d_dtb),
        "gdn_onorm": jnp.stack(d_onorm), "hgrn_lb_logits": d_lbl, "hgrn_gnorm": jnp.stack(d_gnorm),
        "norm_mix": jnp.concatenate(d_nmix, axis=0), "norm_mlp": jnp.concatenate(d_nmlp, axis=0), "norm_final": d_nf[0],
    }
    return loss_row, dh.reshape(x.shape), big, small


_SMALL_LAYOUT = {
    "norm_mix": (0, 4, D_MODEL), "norm_mlp": (8, 4, D_MODEL), "norm_final": (16, 1, D_MODEL),
    "hgrn_lb_logits": (24, 4, D_MODEL), "gdn_onorm": (32, 2, 128), "gdn_a_log": (40, 2, HEADS),
    "gdn_dt_bias": (48, 2, HEADS), "loss": (56, 1, 128), "gdn_conv": (64, 24, D_MODEL), "hgrn_gnorm": (88, 2, D_MODEL),
}


def _pack_small(small, loss_row):
    rows = []
    for name, (first, nrow, lanes) in _SMALL_LAYOUT.items():
        v = loss_row if name == "loss" else small[name]
        v = _pad_lanes(v.reshape(nrow, -1), D_MODEL)
        rows.append(jnp.pad(v, ((0, -nrow % 8), (0, 0))))
    return jnp.concatenate(rows, axis=0)


def _unpack_small(packed, name, shape):
    first, nrow, lanes = _SMALL_LAYOUT[name]
    return packed[first:first + nrow, :lanes].reshape(shape)


def kernel(x, gdn_w_in, gdn_conv, gdn_a_log, gdn_dt_bias, gdn_onorm, gdn_w_out, hgrn_w_in, hgrn_lb_logits, hgrn_gnorm, hgrn_w_out, norm_mix, norm_mlp, mlp_w_up, mlp_w_down, norm_final, loss_target, m_gdn_w_in, m_gdn_conv, m_gdn_a_log, m_gdn_dt_bias, m_gdn_onorm, m_gdn_w_out, m_hgrn_w_in, m_hgrn_lb_logits, m_hgrn_gnorm, m_hgrn_w_out, m_norm_mix, m_norm_mlp, m_mlp_w_up, m_mlp_w_down, m_norm_final, v_gdn_w_in, v_gdn_conv, v_gdn_a_log, v_gdn_dt_bias, v_gdn_onorm, v_gdn_w_out, v_hgrn_w_in, v_hgrn_lb_logits, v_hgrn_gnorm, v_hgrn_w_out, v_norm_mix, v_norm_mlp, v_mlp_w_up, v_mlp_w_down, v_norm_final):
    p = dict(gdn_w_in=gdn_w_in, gdn_conv=gdn_conv, gdn_a_log=gdn_a_log, gdn_dt_bias=gdn_dt_bias, gdn_onorm=gdn_onorm,
             gdn_w_out=gdn_w_out, hgrn_w_in=hgrn_w_in, hgrn_lb_logits=hgrn_lb_logits, hgrn_gnorm=hgrn_gnorm,
             hgrn_w_out=hgrn_w_out, norm_mix=norm_mix, norm_mlp=norm_mlp, mlp_w_up=mlp_w_up, mlp_w_down=mlp_w_down,
             norm_final=norm_final)
    m = dict(gdn_w_in=m_gdn_w_in, gdn_conv=m_gdn_conv, gdn_a_log=m_gdn_a_log, gdn_dt_bias=m_gdn_dt_bias,
             gdn_onorm=m_gdn_onorm, gdn_w_out=m_gdn_w_out, hgrn_w_in=m_hgrn_w_in, hgrn_lb_logits=m_hgrn_lb_logits,
             hgrn_gnorm=m_hgrn_gnorm, hgrn_w_out=m_hgrn_w_out, norm_mix=m_norm_mix, norm_mlp=m_norm_mlp,
             mlp_w_up=m_mlp_w_up, mlp_w_down=m_mlp_w_down, norm_final=m_norm_final)
    v = dict(gdn_w_in=v_gdn_w_in, gdn_conv=v_gdn_conv, gdn_a_log=v_gdn_a_log, gdn_dt_bias=v_gdn_dt_bias,
             gdn_onorm=v_gdn_onorm, gdn_w_out=v_gdn_w_out, hgrn_w_in=v_hgrn_w_in, hgrn_lb_logits=v_hgrn_lb_logits,
             hgrn_gnorm=v_hgrn_gnorm, hgrn_w_out=v_hgrn_w_out, norm_mix=v_norm_mix, norm_mlp=v_norm_mlp,
             mlp_w_up=v_mlp_w_up, mlp_w_down=v_mlp_w_down, norm_final=v_norm_final)
    xi, yi, ci = _mesh_pos()
    chip = 2 * xi + yi
    d = D_MODEL

    sharded = list(BIG) + ["gdn_conv", "hgrn_gnorm"]
    got = dict(zip(sharded, gather_chips([p[k].astype(BF16) if k in BIG else p[k] for k in sharded], "gather_weights")))
    cols = lambda g: jnp.transpose(g, (1, 2, 0, 3)).reshape(g.shape[1], g.shape[2], -1)
    rows = lambda g: jnp.transpose(g, (1, 0, 2, 3)).reshape(g.shape[1], -1, g.shape[3])
    w = dict(p)
    w["gdn_w_in"] = _pad_lanes(cols(got["gdn_w_in"]), GDN_IN_PAD)
    w["hgrn_w_in"] = cols(got["hgrn_w_in"])
    w["mlp_w_up"] = cols(got["mlp_w_up"])
    w["gdn_conv"] = cols(got["gdn_conv"])
    w["gdn_w_out"] = rows(got["gdn_w_out"])
    w["hgrn_w_out"] = rows(got["hgrn_w_out"])
    w["mlp_w_down"] = rows(got["mlp_w_down"])
    w["hgrn_gnorm"] = jnp.transpose(got["hgrn_gnorm"], (1, 0, 2)).reshape(got["hgrn_gnorm"].shape[1], -1)

    loss_row, grad_x, big, small = _local_grads(x, loss_target, w)

    order = [(k, l) for k in BIG for l in range(len(big[k]))]
    gs = [big[k][l] for k, l in order]
    theirs = exchange_sibling_half(gs, "reduce_cores")
    mine = [lax.dynamic_slice_in_dim(g, ci * (g.shape[1] // 2), g.shape[1] // 2, axis=1) for g in gs]
    ps = [add_pair(a, b, f"add_cores{n}") for n, (a, b) in enumerate(zip(mine, theirs))]
    slots = exchange_chips(ps, "reduce_chips")
    halves = [add_slots(s, f"add_chips{n}") for n, s in enumerate(slots)]
    groups = [[halves[n] for n, (k, l) in enumerate(order) if k == name] for name in BIG]
    grads = dict(zip(BIG, share_sibling(groups, "share_cores")))

    total = add_slots(gather_all(_pack_small(small, loss_row), "gather_small"), "add_small")
    loss = total[_SMALL_LAYOUT["loss"][0], 0]
    for name in WEIGHTS:
        if name in BIG:
            continue
        if name == "gdn_conv":
            full = _unpack_small(total, name, (2, CONV_K, 3 * d))
            grads[name] = lax.dynamic_slice_in_dim(full, chip * (3 * d // N_CHIPS), 3 * d // N_CHIPS, axis=2)
        elif name == "hgrn_gnorm":
            full = _unpack_small(total, name, (2, d))
            grads[name] = lax.dynamic_slice_in_dim(full, chip * (d // N_CHIPS), d // N_CHIPS, axis=1)
        else:
            grads[name] = _unpack_small(total, name, p[name].shape)

    delta, new_m, new_v = {}, {}, {}
    for name in WEIGHTS:
        delta[name], new_m[name], new_v[name] = adamw(p[name], grads[name], m[name], v[name], f"adamw_{name}")
    return (loss, grad_x, *[grads[n] for n in WEIGHTS], *[delta[n] for n in WEIGHTS],
            *[new_m[n] for n in WEIGHTS], *[new_v[n] for n in WEIGHTS])


def add_core_halves(g, theirs, core, name):
    s, r, c = g.shape
    r2 = r // 2
    tr = min(r2, 256)
    nb = r2 // tr

    def body(core_ref, g_ref, t_ref, o_ref):
        o_ref[...] = g_ref[...] + t_ref[...]

    grid_spec = pltpu.PrefetchScalarGridSpec(
        num_scalar_prefetch=1, grid=(s, nb),
        in_specs=[pl.BlockSpec((None, tr, c), lambda t, i, cr: (t, cr[0] * nb + i, 0)),
                  pl.BlockSpec((None, tr, c), lambda t, i, cr: (t, i, 0))],
        out_specs=pl.BlockSpec((None, tr, c), lambda t, i, cr: (t, i, 0)))
    return pl.pallas_call(body, name=name, grid_spec=grid_spec, out_shape=jax.ShapeDtypeStruct((s, r2, c), F32),
                          compiler_params=_cparams(("parallel", "parallel")))(core, g, theirs)


def add_chip_slots(slots, name):
    n_l = len(slots)
    s, r2, c = slots[0].shape
    tr = min(r2, 256)
    nb = r2 // tr

    def body(*refs):
        ins, o_ref = refs[:n_l], refs[n_l]
        for k in range(n_l):
            @pl.when(pl.program_id(0) == k)
            def _(k=k):
                acc = ins[k][0]
                for t in range(1, s):
                    acc = acc + ins[k][t]
                o_ref[...] = acc

    in_specs = [pl.BlockSpec((s, tr, c), lambda l, i, k=k: (0, jnp.where(l == k, i, 0), 0)) for k in range(n_l)]
    return pl.pallas_call(
        body, name=name, grid=(n_l, nb), in_specs=in_specs, out_specs=pl.BlockSpec((None, tr, c), lambda l, i: (l, i, 0)),
        out_shape=jax.ShapeDtypeStruct((n_l, r2, c), F32), compiler_params=_cparams(("arbitrary", "arbitrary")),
    )(*slots)


def adamw_halves(w, m, v, mine, theirs, name):
    n_l, r, c = w.shape
    r2 = r // 2
    tr = min(r2, 256)
    nb = r2 // tr
    c1 = 1.0 / (1.0 - ADAM_B1 ** ADAM_STEP)
    c2 = 1.0 / (1.0 - ADAM_B2 ** ADAM_STEP)

    def body(w_ref, m_ref, v_ref, mine_ref, theirs_ref, g_ref, d_ref, nm_ref, nv_ref):
        my_half = (pl.program_id(1) // nb) == lax.axis_index("c")
        gg = jnp.where(my_half, mine_ref[...], theirs_ref[...])
        nm = ADAM_B1 * m_ref[...] + (1.0 - ADAM_B1) * gg
        nv = ADAM_B2 * v_ref[...] + (1.0 - ADAM_B2) * (gg * gg)
        g_ref[...] = gg
        d_ref[...] = -ADAM_LR * ((nm * c1) / (jnp.sqrt(nv * c2) + ADAM_EPS) + ADAM_WD * w_ref[...])
        nm_ref[...] = nm
        nv_ref[...] = nv

    full = pl.BlockSpec((None, tr, c), lambda l, i: (l, i, 0))
    half = pl.BlockSpec((None, tr, c), lambda l, i: (l, i % nb, 0))
    sds = jax.ShapeDtypeStruct((n_l, r, c), F32)
    return pl.pallas_call(
        body, name=name, grid=(n_l, r // tr), in_specs=[full, full, full, half, half], out_specs=[full] * 4,
        out_shape=(sds,) * 4, compiler_params=_cparams(("parallel", "parallel")),
    )(w, m, v, mine, theirs)


def _layer_weight(kind, i):
    if kind == "up":
        return "mlp_w_up", i
    if kind == "down":
        return "mlp_w_down", i
    return ("gdn_w_" if i % 2 == 0 else "hgrn_w_") + kind, i // 2


def kernel(x, gdn_w_in, gdn_conv, gdn_a_log, gdn_dt_bias, gdn_onorm, gdn_w_out, hgrn_w_in, hgrn_lb_logits, hgrn_gnorm, hgrn_w_out, norm_mix, norm_mlp, mlp_w_up, mlp_w_down, norm_final, loss_target, m_gdn_w_in, m_gdn_conv, m_gdn_a_log, m_gdn_dt_bias, m_gdn_onorm, m_gdn_w_out, m_hgrn_w_in, m_hgrn_lb_logits, m_hgrn_gnorm, m_hgrn_w_out, m_norm_mix, m_norm_mlp, m_mlp_w_up, m_mlp_w_down, m_norm_final, v_gdn_w_in, v_gdn_conv, v_gdn_a_log, v_gdn_dt_bias, v_gdn_onorm, v_gdn_w_out, v_hgrn_w_in, v_hgrn_lb_logits, v_hgrn_gnorm, v_hgrn_w_out, v_norm_mix, v_norm_mlp, v_mlp_w_up, v_mlp_w_down, v_norm_final):
    p = dict(gdn_w_in=gdn_w_in, gdn_conv=gdn_conv, gdn_a_log=gdn_a_log, gdn_dt_bias=gdn_dt_bias, gdn_onorm=gdn_onorm,
             gdn_w_out=gdn_w_out, hgrn_w_in=hgrn_w_in, hgrn_lb_logits=hgrn_lb_logits, hgrn_gnorm=hgrn_gnorm,
             hgrn_w_out=hgrn_w_out, norm_mix=norm_mix, norm_mlp=norm_mlp, mlp_w_up=mlp_w_up, mlp_w_down=mlp_w_down,
             norm_final=norm_final)
    m = dict(gdn_w_in=m_gdn_w_in, gdn_conv=m_gdn_conv, gdn_a_log=m_gdn_a_log, gdn_dt_bias=m_gdn_dt_bias,
             gdn_onorm=m_gdn_onorm, gdn_w_out=m_gdn_w_out, hgrn_w_in=m_hgrn_w_in, hgrn_lb_logits=m_hgrn_lb_logits,
             hgrn_gnorm=m_hgrn_gnorm, hgrn_w_out=m_hgrn_w_out, norm_mix=m_norm_mix, norm_mlp=m_norm_mlp,
             mlp_w_up=m_mlp_w_up, mlp_w_down=m_mlp_w_down, norm_final=m_norm_final)
    v = dict(gdn_w_in=v_gdn_w_in, gdn_conv=v_gdn_conv, gdn_a_log=v_gdn_a_log, gdn_dt_bias=v_gdn_dt_bias,
             gdn_onorm=v_gdn_onorm, gdn_w_out=v_gdn_w_out, hgrn_w_in=v_hgrn_w_in, hgrn_lb_logits=v_hgrn_lb_logits,
             hgrn_gnorm=v_hgrn_gnorm, hgrn_w_out=v_hgrn_w_out, norm_mix=v_norm_mix, norm_mlp=v_norm_mlp,
             mlp_w_up=v_mlp_w_up, mlp_w_down=v_mlp_w_down, norm_final=v_norm_final)
    xi, yi, ci = _mesh_pos()
    chip = 2 * xi + yi
    core = jnp.reshape(ci, (1,)).astype(jnp.int32)
    d = D_MODEL
    bsz, t_len, _ = x.shape
    n_tok = bsz * t_len

    def shard(kind, i):
        name, idx = _layer_weight(kind, i)
        return p[name][idx].astype(BF16)

    def w_in_of(i, slots):
        if i % 2 == 0:
            return _pad_lanes(jnp.transpose(slots, (1, 0, 2)).reshape(d, GDN_IN), GDN_IN_PAD)
        return slots

    (first,) = run_carries([gather_carry([shard("in", 0), p["gdn_conv"], p["hgrn_gnorm"]])], "gather_first")
    conv = jnp.transpose(first[1], (1, 2, 0, 3)).reshape(DEPTH // 2, CONV_K, 3 * d)
    gnorm = jnp.transpose(first[2], (1, 0, 2)).reshape(DEPTH // 2, d)
    lbl = p["hgrn_lb_logits"]
    h = x.reshape(n_tok, d)
    next_in, next_out = first[0], None
    saved = []
    for i in range(DEPTH):
        j = i // 2
        w_in = w_in_of(i, next_in)
        nmix = p["norm_mix"][i][None, :]
        (y, proj), got = norm_matmul(h, nmix, w_in, False, f"in_proj{i}",
                                     [gather_carry([shard("out", 0)])] if i == 0 else [])
        if i == 0:
            next_out = got[0][0]
        ride = [gather_carry([shard("up", i), shard("down", i)])]
        if i % 2 == 0:
            al = _pad_lanes(p["gdn_a_log"][j][None, :], 128)
            dtb = _pad_lanes(p["gdn_dt_bias"][j][None, :], 128)
            on = p["gdn_onorm"][j][None, :]
            (og, ssave), got = gdn_forward(proj, conv[j], al, dtb, on, bsz, f"gdn_fwd{i}", ride)
            mix = (proj, ssave, al, dtb, on)
        else:
            (o, ssave), got = hgrn_forward(proj, lbl, i, bsz, ride)
            gn = gnorm[j][None, :]
            og = hgrn_post_forward(o, proj, gn, f"hgrn_post{i}")
            mix = (proj, ssave, o, gn)
        w_up, w_down = got[0][0], got[0][1].reshape(MLP_HIDDEN, d)
        w_out = next_out.reshape(d, d)
        h1, _ = matmul(og, w_out, "nn", f"out_proj{i}", extra=h, epilogue="add")
        nmlp = p["norm_mlp"][i][None, :]
        (z, a, r), got = norm_matmul(h1, nmlp, w_up, True, f"mlp_up{i}",
                                     [gather_carry([shard("in", i + 1)])] if i + 1 < DEPTH else [])
        if i + 1 < DEPTH:
            next_in = got[0][0]
        h2, got = matmul(a, w_down, "nn", f"mlp_down{i}", extra=h1, epilogue="add",
                         carries=[gather_carry([shard("out", i + 1)])] if i + 1 < DEPTH else [])
        if i + 1 < DEPTH:
            next_out = got[0][0]
        saved.append((h, nmix, y, mix, og, w_in, w_out, h1, nmlp, z, a, r, w_up, w_down))
        h = h2
    loss_row, dh, dhb, d_nf = loss_head(h, p["norm_final"][None, :], loss_target.reshape(n_tok, d))

    G = {k: [None] * DEPTH for k in ("in", "out", "up", "down")}
    P = {k: [None] * DEPTH for k in ("in", "out", "up", "down")}
    slots = {k: [None] * DEPTH for k in ("in", "out", "up", "down")}
    d_nmix, d_nmlp = [None] * DEPTH, [None] * DEPTH
    d_conv, d_alog, d_dtb, d_onorm, d_gnorm = [None] * 2, [None] * 2, [None] * 2, [None] * 2, [None] * 2
    d_lbl = jnp.zeros((DEPTH, d), F32)
    for i in reversed(range(DEPTH)):
        j = i // 2
        h_in, nmix, y, mix, og, w_in, w_out, h1, nmlp, z, a, r, w_up, w_down = saved[i]
        ride = [sibling_half_carry([G["in"][i + 1], G["out"][i + 1]])] if i + 1 < DEPTH else []
        du, got = matmul(dhb, w_down, "nt", f"d_mlp_act{i}", out_dtype=BF16, extra=r, epilogue="mul2", carries=ride)
        if i + 1 < DEPTH:
            P["in"][i + 1] = add_core_halves(G["in"][i + 1], got[0][0], core, f"add_cores_in{i + 1}")
            P["out"][i + 1] = add_core_halves(G["out"][i + 1], got[0][1], core, f"add_cores_out{i + 1}")
        G["down"][i] = matmul(a, dhb, "tn", f"dw_down{i}")[0].reshape(N_CHIPS, -1, d)
        dz, _ = matmul(du, w_up, "nt", f"d_mlp_in{i}")
        G["up"][i], _ = matmul(z, du, "tn", f"dw_up{i}", shards=N_CHIPS)
        dh1, dh1b, d_nmlp[i] = norm_backward(h1, nmlp, dz, dh, f"d_norm_mlp{i}")
        dog, _ = matmul(dh1b, w_out, "nt", f"d_mix_out{i}")
        G["out"][i] = matmul(og, dh1b, "tn", f"dw_out{i}")[0].reshape(N_CHIPS, -1, d)
        ride = [sibling_half_carry([G["up"][i], G["down"][i]])]
        if i + 1 < DEPTH:
            ride.append(chips_carry([P[k][i + 1] for k in ("in", "out", "up", "down")]))
        if i % 2 == 0:
            proj, ssave, al, dtb, on = mix
            (dproj, dcw, dal, ddt, don), got = gdn_backward(proj, conv[j], al, dtb, on, ssave, dog, bsz, f"gdn_bwd{i}", ride)
            d_conv[j] = jnp.transpose(dcw, (2, 1, 0, 3)).reshape(CONV_K, 3 * d)
            d_alog[j], d_dtb[j], d_onorm[j] = dal[0, :HEADS], ddt[0, :HEADS], don[0]
        else:
            proj, ssave, o, gn = mix
            do_raw, dgate, dgn = hgrn_post_backward(o, proj, gn, dog, f"d_hgrn_post{i}")
            (dproj, dlb), got = hgrn_backward(proj, lbl, ssave, do_raw, dgate, i, bsz, ride)
            d_gnorm[j] = dgn[0]
            d_lbl = d_lbl + jnp.transpose(dlb, (1, 0, 2)).reshape(DEPTH, d)
        P["up"][i] = add_core_halves(G["up"][i], got[0][0], core, f"add_cores_up{i}")
        P["down"][i] = add_core_halves(G["down"][i], got[0][1], core, f"add_cores_down{i}")
        if i + 1 < DEPTH:
            for k, s in zip(("in", "out", "up", "down"), got[1]):
                slots[k][i + 1] = s
        ride = [chips_carry([P["up"][0], P["down"][0]])] if i == 0 else []
        dy, got = matmul(dproj, w_in, "nt", f"d_in_proj{i}", carries=ride)
        if i == 0:
            slots["up"][0], slots["down"][0] = got[0]
        if i % 2 == 0:
            dw_in = matmul(y, dproj, "tn", f"dw_in{i}")[0][:, :GDN_IN]
            G["in"][i] = jnp.transpose(dw_in.reshape(d, N_CHIPS, GDN_IN // N_CHIPS), (1, 0, 2))
        else:
            G["in"][i], _ = matmul(y, dproj, "tn", f"dw_in{i}", shards=N_CHIPS)
        dh, dhb, d_nmix[i] = norm_backward(h_in, nmix, dy, dh1, f"d_norm_mix{i}")
    grad_x = dh.reshape(x.shape)

    (got,) = run_carries([sibling_half_carry([G["in"][0], G["out"][0]])], "reduce_cores_last")
    P["in"][0] = add_core_halves(G["in"][0], got[0], core, "add_cores_in0")
    P["out"][0] = add_core_halves(G["out"][0], got[1], core, "add_cores_out0")
    (got,) = run_carries([chips_carry([P["in"][0], P["out"][0]])], "reduce_chips_last")
    slots["in"][0], slots["out"][0] = got
    by_weight = {}
    for kind in ("in", "out", "up", "down"):
        for i in range(DEPTH):
            by_weight.setdefault(_layer_weight(kind, i)[0], []).append(slots[kind][i])
    mine = {name: add_chip_slots(by_weight[name], f"add_chips_{name}") for name in BIG}
    small = {
        "gdn_conv": jnp.stack(d_conv), "gdn_a_log": jnp.stack(d_alog), "gdn_dt_bias": jnp.stack(d_dtb),
        "gdn_onorm": jnp.stack(d_onorm), "hgrn_lb_logits": d_lbl, "hgrn_gnorm": jnp.stack(d_gnorm),
        "norm_mix": jnp.concatenate(d_nmix, axis=0), "norm_mlp": jnp.concatenate(d_nmlp, axis=0), "norm_final": d_nf[0],
    }
    theirs, (blocks,) = run_carries([sibling_copy_carry([mine[name] for name in BIG]),
                                     gather_all_carry(_pack_small(small, loss_row))], "share_cores")
    theirs = dict(zip(BIG, theirs))

    total = add_slots(blocks, "add_small")
    loss = total[_SMALL_LAYOUT["loss"][0], 0]
    grads, delta, new_m, new_v = {}, {}, {}, {}
    for name in WEIGHTS:
        if name in BIG:
            grads[name], delta[name], new_m[name], new_v[name] = adamw_halves(
                p[name], m[name], v[name], mine[name], theirs[name], f"adamw_{name}")
            continue
        if name == "gdn_conv":
            full = _unpack_small(total, name, (2, CONV_K, 3 * d))
            grads[name] = lax.dynamic_slice_in_dim(full, chip * (3 * d // N_CHIPS), 3 * d // N_CHIPS, axis=2)
        elif name == "hgrn_gnorm":
            full = _unpack_small(total, name, (2, d))
            grads[name] = lax.dynamic_slice_in_dim(full, chip * (d // N_CHIPS), d // N_CHIPS, axis=1)
        else:
            grads[name] = _unpack_small(total, name, p[name].shape)
        delta[name], new_m[name], new_v[name] = adamw(p[name], grads[name], m[name], v[name], f"adamw_{name}")
    return (loss, grad_x, *[grads[n] for n in WEIGHTS], *[delta[n] for n in WEIGHTS],
            *[new_m[n] for n in WEIGHTS], *[new_v[n] for n in WEIGHTS])
```

```python
import functools
import math

import jax
import jax.numpy as jnp
from jax import lax
from jax.experimental import pallas as pl
from jax.experimental.pallas import tpu as pltpu

F32 = jnp.float32
BF16 = jnp.bfloat16
HI = lax.Precision.HIGHEST

D_MODEL = 1024
HEADS = 8
HEAD_DIM = 128
CHUNK = 64
SUB = 16
CONV_K = 4
HALO = 8
DEPTH = 4
EPS = 1e-6
MLP_HIDDEN = 4 * D_MODEL
GDN_MAIN = 4 * D_MODEL
GDN_IN = GDN_MAIN + 2 * HEADS
GDN_IN_PAD = GDN_MAIN + 128
NEG = -1e30

ADAM_LR = 0.001
ADAM_B1 = 0.9
ADAM_B2 = 0.999
ADAM_EPS = 1e-08
ADAM_WD = 0.01
ADAM_STEP = 10

VMEM_LIMIT = 48 * 1024 * 1024

MESH = pl.DeviceIdType.MESH


def _cparams(sem=None, **kw):
    if sem is not None:
        kw["dimension_semantics"] = sem
    return pltpu.CompilerParams(vmem_limit_bytes=VMEM_LIMIT, **kw)


def _iota(shape, dim):
    return lax.broadcasted_iota(jnp.int32, shape, dim)


_DIMS = {"nn": (((1,), (0,)), ((), ())), "nt": (((1,), (1,)), ((), ())), "tn": (((0,), (0,)), ((), ()))}


def _dot(a, b, mode):
    return lax.dot_general(a.astype(BF16), b.astype(BF16), _DIMS[mode], preferred_element_type=F32)


@functools.partial(jax.custom_vjp, nondiff_argnums=(2,))
def _mmx(a, b, mode):
    return _dot(a, b, mode)


def _mmx_fwd(a, b, mode):
    return _dot(a, b, mode), (a, b)


def _mmx_bwd(mode, res, g):
    a, b = res
    if mode == "nn":
        return _dot(g, b, "nt"), _dot(a, g, "tn")
    if mode == "nt":
        return _dot(g, b, "nn"), _dot(g, a, "tn")
    return _dot(b, g, "nt"), _dot(a, g, "nn")


_mmx.defvjp(_mmx_fwd, _mmx_bwd)


def _mm_f32(a, b):
    return lax.dot_general(a, b, _DIMS["nn"], precision=HI, preferred_element_type=F32)


def _mm(a, b):
    return _mmx(a, b, "nn")


def _mm_nt(a, b):
    return _mmx(a, b, "nt")


def _mm_tn(a, b):
    return _mmx(a, b, "tn")


@functools.partial(jax.custom_vjp, nondiff_argnums=(1,))
def _roll_rows(x, d):
    return pltpu.roll(x, d, 0)


def _roll_rows_fwd(x, d):
    return pltpu.roll(x, d, 0), None


def _roll_rows_bwd(d, _, g):
    return (pltpu.roll(g, g.shape[0] - d, 0),)


_roll_rows.defvjp(_roll_rows_fwd, _roll_rows_bwd)


def _sigmoid(x):
    return 1.0 / (1.0 + jnp.exp(-x))


def _silu(x):
    return x * _sigmoid(x)


def _softplus(x):
    return jnp.maximum(x, 0.0) + jnp.log(1.0 + jnp.exp(-jnp.abs(x)))


def _log_sigmoid(x):
    return jnp.minimum(x, 0.0) - jnp.log(1.0 + jnp.exp(-jnp.abs(x)))


def _logaddexp(a, b):
    return jnp.maximum(a, b) + jnp.log(1.0 + jnp.exp(-jnp.abs(a - b)))


def _row_to_col(row):
    n = row.shape[1]
    eye = _iota((n, n), 0) == _iota((n, n), 1)
    return jnp.sum(jnp.where(eye, jnp.broadcast_to(row, (n, n)), 0.0), axis=1, keepdims=True)


def _col_to_row(col):
    n = col.shape[0]
    eye = _iota((n, n), 0) == _iota((n, n), 1)
    return jnp.sum(jnp.where(eye, jnp.broadcast_to(col, (n, n)), 0.0), axis=0, keepdims=True)


def _pick_row(x, r):
    return jnp.sum(jnp.where(_iota(x.shape, 0) == r, x, 0.0), axis=0, keepdims=True)


def _pick_lane(x, l):
    return jnp.sum(jnp.where(_iota(x.shape, 1) == l, x, 0.0), axis=1, keepdims=True)


def _each(f, *lists):
    return [f(*t) for t in zip(*lists)]


def _unit_lower_inverse(Ls):
    n = Ls[0].shape[0]
    r, c = _iota((n, n), 0), _iota((n, n), 1)
    eye = jnp.where(r == c, 1.0, 0.0).astype(F32)
    Ld = _each(lambda L: jnp.where((r // SUB) == (c // SUB), L, 0.0), Ls)
    Lo = _each(lambda L, d: L - d, Ls, Ld)
    P = _each(lambda d: eye - d, Ld)
    Lp = Ld
    for _ in range(int(math.log2(SUB)) - 1):
        Lp = _each(lambda x: _mm(x, x), Lp)
        P = _each(lambda p, x: p + _mm(p, x), P, Lp)
    N = _each(_mm, P, Lo)
    N2 = _each(lambda x: _mm(x, x), N)
    X = _each(lambda x, x2: (eye - x) + _mm(eye - x, x2), N, N2)
    return _each(_mm, X, P)


def _shift_rows(x, halo, d):
    if d == 0:
        return x
    xr = _roll_rows(x, d)
    hr = _roll_rows(halo, d)
    hfull = jnp.concatenate([hr, jnp.zeros((x.shape[0] - HALO, x.shape[1]), F32)], axis=0)
    return jnp.where(_iota(x.shape, 0) >= d, xr, hfull)


def _causal_conv_chunk(x, halo, w):
    y = None
    for kk in range(CONV_K):
        t = _shift_rows(x, halo, CONV_K - 1 - kk) * _pick_row(w, kk)
        y = t if y is None else y + t
    return y


def _gdn_chunk(xq, xk, xv, hq, hk, hv, gate, ab, S, cwq, cwk, cwv, alog, dtb, onorm, *, heads):
    C = xq[0].shape[0]
    q = _each(lambda x, h, w: _silu(_causal_conv_chunk(x, h, w)), xq, hq, cwq)
    k = _each(lambda x, h, w: _silu(_causal_conv_chunk(x, h, w)), xk, hk, cwk)
    v = _each(lambda x, h, w: _silu(_causal_conv_chunk(x, h, w)), xv, hv, cwv)
    q = _each(lambda t: t * lax.rsqrt(jnp.sum(t * t, axis=1, keepdims=True) + EPS) * (HEAD_DIM ** -0.5), q)
    k = _each(lambda t: t * lax.rsqrt(jnp.sum(t * t, axis=1, keepdims=True) + EPS), k)
    beta_all = _sigmoid(ab)
    g_all = -jnp.exp(alog) * _softplus(ab + dtb)
    beta = [_pick_lane(beta_all, HEADS + h) for h in heads]
    g = [_pick_lane(g_all, h) for h in heads]
    r, c = _iota((C, C), 0), _iota((C, C), 1)
    gc = _each(lambda t: jnp.sum(jnp.where(c <= r, jnp.broadcast_to(_col_to_row(t), (C, C)), 0.0), axis=1,
                                 keepdims=True), g)
    gc_row = _each(lambda t: jnp.sum(jnp.where(r <= c, jnp.broadcast_to(t, (C, C)), 0.0), axis=0, keepdims=True), g)
    decay = _each(lambda a, b: jnp.exp(jnp.where(r >= c, a - b, NEG)), gc, gc_row)
    kb = _each(lambda a, b: a * b, k, beta)
    L = _each(lambda a, b, d: jnp.where(r > c, _mm_nt(a, b) * d, 0.0), kb, k, decay)
    A = _each(lambda a, b, d: jnp.where(r >= c, _mm_nt(a, b) * d, 0.0), q, k, decay)
    T = _unit_lower_inverse(L)
    egc = _each(jnp.exp, gc)
    u = _each(lambda t, a, b: _mm(t, a * b), T, v, beta)
    w = _each(lambda t, a, e: _mm(t, a * e), T, kb, egc)
    gl = _each(lambda t: _pick_row(t, C - 1), gc)
    v_new = _each(lambda a, b, s: a - _mm(b, s), u, w, S)
    o = _each(lambda a, e, s, m, vn: _mm(a * e, s) + _mm(m, vn), q, egc, S, A, v_new)
    S_next = _each(lambda s, l, a, t, vn: s * jnp.exp(l) + _mm_tn(a * jnp.exp(l - t), vn), S, gl, k, gc, v_new)
    o = _each(lambda t, gt: t * lax.rsqrt(jnp.mean(t * t, axis=1, keepdims=True) + EPS) * onorm * _silu(gt), o, gate)
    return o, S_next


def _hgrn_lower_bound(lbl, layer):
    e = jnp.exp(lbl - jnp.max(lbl, axis=0, keepdims=True))
    sm = e / jnp.sum(e, axis=0, keepdims=True)
    r = _iota(lbl.shape, 0)
    return jnp.sum(jnp.where((r >= 1) & (r <= layer), sm, 0.0), axis=0, keepdims=True)


_LEVELS = (1, 2, 4, 8, 16, 32)


def _prefix_matrix(n):
    r, c = _iota((n, n), 0), _iota((n, n), 1)
    parts = [jnp.where(c <= r, 1.0, 0.0)]
    for s in _LEVELS:
        parts.append(jnp.where(c < (r // (2 * s)) * (2 * s) + s, 1.0, 0.0))
    return jnp.concatenate(parts, axis=0).astype(F32)


def _prefix_sums_of(x):
    n = x.shape[0]
    y = lax.dot_general(_prefix_matrix(n), x, _DIMS["nn"], precision=HI, preferred_element_type=F32)
    return tuple(y[t * n:(t + 1) * n] for t in range(len(_LEVELS) + 1))


@jax.custom_vjp
def _prefix_sums(x):
    return _prefix_sums_of(x)


def _prefix_sums_fwd(x):
    return _prefix_sums_of(x), None


def _prefix_sums_bwd(_, gs):
    g = jnp.concatenate(gs, axis=0)
    return (lax.dot_general(_prefix_matrix(gs[0].shape[0]), g, _DIMS["tn"], precision=HI, preferred_element_type=F32),)


_prefix_sums.defvjp(_prefix_sums_fwd, _prefix_sums_bwd)


def _hgrn_chunk(qp, fp, v, S, lbl, *, layer):
    C = qp[0].shape[0]
    lb = _each(lambda l: _hgrn_lower_bound(l, layer), lbl)
    lf = _each(lambda l, f: _logaddexp(jnp.log(l), jnp.log(1.0 - l) + _log_sigmoid(f)), lb, fp)
    k = _each(lambda l, f: (1.0 - l) * _sigmoid(-f), lb, fp)
    q = _each(lambda x: _silu(x) * (HEAD_DIM ** -0.5), qp)
    r, c = _iota((C, C), 0), _iota((C, C), 1)
    row = _iota(qp[0].shape, 0)
    sums = _each(_prefix_sums, lf)
    gc = [t[0] for t in sums]
    a = _each(lambda x, y: jnp.where(r == c, _mm_nt(x, y), 0.0), q, k)
    for n, s in enumerate(_LEVELS):
        ref = [t[n + 1] for t in sums]
        upper = (row % (2 * s)) >= s
        same = (r // (2 * s)) == (c // (2 * s))
        q_s = _each(lambda x, g, m: x * jnp.exp(jnp.where(upper, g - m, NEG)), q, gc, ref)
        k_s = _each(lambda x, g, m: x * jnp.exp(jnp.where(upper, NEG, m - g)), k, gc, ref)
        a = _each(lambda t, x, y: t + jnp.where(same, _mm_nt(x, y), 0.0), a, q_s, k_s)
    o = _each(lambda t, x, g, vv, st: _mm(t, vv) + _mm(x * jnp.exp(g), st), a, q, gc, v, S)
    gl = _each(lambda g: _pick_row(g, C - 1), gc)
    S_next = _each(lambda st, l, x, g, vv: st * _row_to_col(jnp.exp(l)) + _mm_tn(x * jnp.exp(l - g), vv),
                   S, gl, k, gc, v)
    return o, S_next


N_CHIPS = 4
N_DEV = 8
_ANY = pl.BlockSpec(memory_space=pl.ANY)


def _mesh_pos():
    return lax.axis_index("x"), lax.axis_index("y"), lax.axis_index("c")


def _other_chips(x, y):
    ps = [(1 - x, y), (x, 1 - y), (1 - x, 1 - y)]
    return [(p, 2 * p[0] + p[1]) for p in ps]


def _remote(src, dst, send_sem, recv_sem, dev):
    return pltpu.make_async_remote_copy(src_ref=src, dst_ref=dst, send_sem=send_sem, recv_sem=recv_sem,
                                        device_id=dev, device_id_type=MESH)


class Carry:
    def __init__(self, ins, out_shapes, sems, start, finish):
        self.ins, self.out_shapes, self.sems, self.start, self.finish = list(ins), list(out_shapes), list(sems), start, finish


def _pcall(body, *, name, grid, in_specs, out_specs, out_shape, scratch_shapes=(), dims, args, carries=()):
    in_specs, out_specs, out_shape = list(in_specs), list(out_specs), list(out_shape)
    scratch_shapes, args = list(scratch_shapes), list(args)
    n_in, n_out, n_scr = len(in_specs), len(out_shape), len(scratch_shapes)
    carries = [c for c in carries if c is not None]
    if not carries:
        res = pl.pallas_call(body, name=name, grid=grid, in_specs=in_specs, out_specs=out_specs, out_shape=out_shape,
                             scratch_shapes=scratch_shapes, compiler_params=_cparams(dims))(*args)
        return list(res), []
    ci = [len(c.ins) for c in carries]
    co = [len(c.out_shapes) for c in carries]
    cs = [len(c.sems) for c in carries]

    def split(seq, sizes):
        out, k = [], 0
        for s in sizes:
            out.append(seq[k:k + s])
            k += s
        return out

    def carried(*refs):
        ins, cins, outs, couts, scr, sems = split(refs, [n_in, sum(ci), n_out, sum(co), n_scr, sum(cs)])
        cins, couts, sems = split(cins, ci), split(couts, co), split(sems, cs)
        ids = [pl.program_id(a) for a in range(len(grid))]
        first, last = ids[0] == 0, ids[0] == grid[0] - 1
        for a in range(1, len(grid)):
            first, last = first & (ids[a] == 0), last & (ids[a] == grid[a] - 1)

        @pl.when(first)
        def _():
            for c, i, o, s in zip(carries, cins, couts, sems):
                c.start(i, o, s)

        body(*ins, *outs, *scr)

        @pl.when(last)
        def _():
            for c, i, o, s in zip(carries, cins, couts, sems):
                c.finish(i, o, s)

    res = pl.pallas_call(
        carried, name=name, grid=grid,
        in_specs=in_specs + [_ANY] * sum(ci), out_specs=out_specs + [_ANY] * sum(co),
        out_shape=out_shape + [s for c in carries for s in c.out_shapes],
        scratch_shapes=scratch_shapes + [s for c in carries for s in c.sems],
        compiler_params=_cparams(("arbitrary",) * len(grid)),
    )(*args, *[a for c in carries for a in c.ins])
    return list(res[:n_out]), split(list(res[n_out:]), co)


def run_carries(carries, name):
    ci = [len(c.ins) for c in carries]
    co = [len(c.out_shapes) for c in carries]
    cs = [len(c.sems) for c in carries]

    def split(seq, sizes):
        out, k = [], 0
        for s in sizes:
            out.append(seq[k:k + s])
            k += s
        return out

    def body(*refs):
        cins, couts, sems = split(refs, [sum(ci), sum(co), sum(cs)])
        cins, couts, sems = split(cins, ci), split(couts, co), split(sems, cs)
        for c, i, o, s in zip(carries, cins, couts, sems):
            c.start(i, o, s)
        for c, i, o, s in zip(carries, cins, couts, sems):
            c.finish(i, o, s)

    res = pl.pallas_call(
        body, name=name, in_specs=[_ANY] * sum(ci), out_specs=[_ANY] * sum(co),
        out_shape=[s for c in carries for s in c.out_shapes], scratch_shapes=[s for c in carries for s in c.sems],
    )(*[a for c in carries for a in c.ins])
    return split(list(res), co)


def gather_carry(arrs):
    n = len(arrs)
    split = [a.ndim == 2 and a.shape[0] % 32 == 0 for a in arrs]

    def plan(ins, outs, sems):
        send_sems, recv_sems, pass_send, pass_recv, local_sems = sems
        x, y, c = _mesh_pos()
        me = 2 * x + y
        peers = _other_chips(x, y)
        locs = [pltpu.make_async_copy(ins[a], outs[a].at[me], local_sems.at[a]) for a in range(n)]
        sends, recvs, passes, pass_recvs = [], [], [], []
        for a in range(n):
            half = arrs[a].shape[0] // 2
            mine, other = pl.ds(c * half, half), pl.ds((1 - c) * half, half)
            for j, ((px, py), t) in enumerate(peers):
                sem = (send_sems.at[a, j], recv_sems.at[a, j])
                if split[a]:
                    sends.append(_remote(ins[a].at[mine], outs[a].at[me, mine], *sem, (px, py, c)))
                    recvs.append(_remote(ins[a].at[mine], outs[a].at[t, mine], *sem, (px, py, c)))
                    psem = (pass_send.at[a, j], pass_recv.at[a, j])
                    passes.append(_remote(outs[a].at[t, mine], outs[a].at[t, mine], *psem, (x, y, 1 - c)))
                    pass_recvs.append(_remote(outs[a].at[t, other], outs[a].at[t, other], *psem, (x, y, 1 - c)))
                else:
                    sends.append(_remote(ins[a], outs[a].at[me], *sem, (px, py, c)))
                    recvs.append(_remote(ins[a], outs[a].at[t], *sem, (px, py, c)))
                    passes.append(None)
                    pass_recvs.append(None)
        return locs, sends, recvs, passes, pass_recvs

    def start(ins, outs, sems):
        locs, sends, _, _, _ = plan(ins, outs, sems)
        for cp in locs + sends:
            cp.start()

    def finish(ins, outs, sems):
        locs, sends, recvs, passes, pass_recvs = plan(ins, outs, sems)
        for r, p in zip(recvs, passes):
            r.wait_recv()
            if p is not None:
                p.start()
        for p in pass_recvs:
            if p is not None:
                p.wait_recv()
        for r in sends + [p for p in passes if p is not None]:
            r.wait_send()
        for cp in locs:
            cp.wait()

    return Carry(arrs, [jax.ShapeDtypeStruct((N_CHIPS,) + a.shape, a.dtype) for a in arrs],
                 [pltpu.SemaphoreType.DMA((n, 3)), pltpu.SemaphoreType.DMA((n, 3)), pltpu.SemaphoreType.DMA((n, 3)),
                  pltpu.SemaphoreType.DMA((n, 3)), pltpu.SemaphoreType.DMA((n,))], start, finish)


def sibling_half_carry(gs):
    n = len(gs)

    def copies(ins, outs, sems):
        send_sems, recv_sems = sems
        x, y, c = _mesh_pos()
        out = []
        for a in range(n):
            half = gs[a].shape[1] // 2
            out.append(_remote(ins[a].at[:, pl.ds((1 - c) * half, half), :], outs[a], send_sems.at[a], recv_sems.at[a],
                               (x, y, 1 - c)))
        return out

    def start(ins, outs, sems):
        for r in copies(ins, outs, sems):
            r.start()

    def finish(ins, outs, sems):
        cps = copies(ins, outs, sems)
        for r in cps:
            r.wait_recv()
        for r in cps:
            r.wait_send()

    return Carry(gs, [jax.ShapeDtypeStruct((g.shape[0], g.shape[1] // 2, g.shape[2]), g.dtype) for g in gs],
                 [pltpu.SemaphoreType.DMA((n,)), pltpu.SemaphoreType.DMA((n,))], start, finish)


def chips_carry(ps):
    n = len(ps)

    def copies(ins, outs, sems):
        send_sems, recv_sems, local_sems = sems
        x, y, c = _mesh_pos()
        me = 2 * x + y
        peers = _other_chips(x, y)
        locs = [pltpu.make_async_copy(ins[a].at[me], outs[a].at[me], local_sems.at[a]) for a in range(n)]
        sends = [_remote(ins[a].at[t], outs[a].at[me], send_sems.at[a, j], recv_sems.at[a, j], (px, py, c))
                 for a in range(n) for j, ((px, py), t) in enumerate(peers)]
        recvs = [_remote(ins[a].at[t], outs[a].at[t], send_sems.at[a, j], recv_sems.at[a, j], (px, py, c))
                 for a in range(n) for j, ((px, py), t) in enumerate(peers)]
        return locs, sends, recvs

    def start(ins, outs, sems):
        locs, sends, _ = copies(ins, outs, sems)
        for cp in locs + sends:
            cp.start()

    def finish(ins, outs, sems):
        locs, sends, recvs = copies(ins, outs, sems)
        for r in recvs:
            r.wait_recv()
        for r in sends:
            r.wait_send()
        for cp in locs:
            cp.wait()

    return Carry(ps, [jax.ShapeDtypeStruct(p.shape, p.dtype) for p in ps],
                 [pltpu.SemaphoreType.DMA((n, 3)), pltpu.SemaphoreType.DMA((n, 3)), pltpu.SemaphoreType.DMA((n,))],
                 start, finish)


def sibling_copy_carry(arrs):
    n = len(arrs)

    def copies(ins, outs, sems):
        send_sems, recv_sems = sems
        x, y, c = _mesh_pos()
        return [_remote(ins[a], outs[a], send_sems.at[a], recv_sems.at[a], (x, y, 1 - c)) for a in range(n)]

    def start(ins, outs, sems):
        for r in copies(ins, outs, sems):
            r.start()

    def finish(ins, outs, sems):
        cps = copies(ins, outs, sems)
        for r in cps:
            r.wait_recv()
        for r in cps:
            r.wait_send()

    return Carry(arrs, [jax.ShapeDtypeStruct(a.shape, a.dtype) for a in arrs],
                 [pltpu.SemaphoreType.DMA((n,)), pltpu.SemaphoreType.DMA((n,))], start, finish)


def gather_all_carry(small):
    flips = [(fx, fy, fc) for fx in (0, 1) for fy in (0, 1) for fc in (0, 1)][1:]

    def copies(ins, outs, sems):
        send_sems, recv_sems, local_sem = sems
        in_ref, out_ref = ins[0], outs[0]
        x, y, c = _mesh_pos()
        me = 4 * x + 2 * y + c
        peers = [((1 - x) if fx else x, (1 - y) if fy else y, (1 - c) if fc else c) for fx, fy, fc in flips]
        loc = pltpu.make_async_copy(in_ref, out_ref.at[me], local_sem.at[0])
        sends = [_remote(in_ref, out_ref.at[me], send_sems.at[k], recv_sems.at[k], p) for k, p in enumerate(peers)]
        recvs = [_remote(in_ref, out_ref.at[4 * p[0] + 2 * p[1] + p[2]], send_sems.at[k], recv_sems.at[k], p)
                 for k, p in enumerate(peers)]
        return loc, sends, recvs

    def start(ins, outs, sems):
        loc, sends, _ = copies(ins, outs, sems)
        loc.start()
        for r in sends:
            r.start()

    def finish(ins, outs, sems):
        loc, sends, recvs = copies(ins, outs, sems)
        for r in recvs:
            r.wait_recv()
        for r in sends:
            r.wait_send()
        loc.wait()

    return Carry([small], [jax.ShapeDtypeStruct((N_DEV,) + small.shape, small.dtype)],
                 [pltpu.SemaphoreType.DMA((N_DEV - 1,)), pltpu.SemaphoreType.DMA((N_DEV - 1,)),
                  pltpu.SemaphoreType.DMA((1,))], start, finish)


HB = 8
HBW = HB * HEAD_DIM


def _chunk_specs(nc, col0, rev=False):
    return pl.BlockSpec((CHUNK, HBW), lambda b, n, g: (b * nc + ((nc - 1 - n) if rev else n), col0 // HB + g))


def _chunk_specs_rev(nc, col0):
    return _chunk_specs(nc, col0, True)


def _halo_spec(nc, col0, rev):
    per = CHUNK // HALO

    def imap(b, n, g):
        nn = (nc - 1 - n) if rev else n
        return (jnp.maximum((b * nc + nn) * per - 1, 0), col0 // HB + g)
    return pl.BlockSpec((HALO, HBW), imap)


def _const_spec(shape):
    nd = len(shape)
    return pl.BlockSpec(shape, lambda b, n, g: (0,) * nd)


def _state_spec(nc, rev=False):
    return pl.BlockSpec((None, None, HB, HEAD_DIM, HEAD_DIM),
                        lambda b, n, g: (b, (nc - 1 - n) if rev else n, g, 0, 0))


def _lanes(hh):
    return slice(hh * HEAD_DIM, (hh + 1) * HEAD_DIM)


def _head(g, hh):
    return hh if HB == HEADS else g * HB + hh


def gdn_forward(proj, conv_w, alog, dtb, onorm, bsz, name, carries=()):
    n_tok = proj.shape[0]
    nc = n_tok // bsz // CHUNK

    def body(xq, xk, xv, hq, hk, hv, gate, ab, cwq, cwk, cwv, al, dt, on, o_ref, ssave_ref, s_ref):
        n, g = pl.program_id(1), pl.program_id(2)
        keep = jnp.where(n > 0, 1.0, 0.0).astype(F32)

        @pl.when(n == 0)
        def _():
            for hh in range(HB):
                s_ref[_head(g, hh)] = jnp.zeros((HEAD_DIM, HEAD_DIM), F32)

        heads = [_head(g, hh) for hh in range(HB)]
        per_head = lambda ref, scale=None: [ref[:, _lanes(hh)] if scale is None else ref[:, _lanes(hh)] * scale
                                            for hh in range(HB)]
        S = [s_ref[h] for h in heads]
        for hh in range(HB):
            ssave_ref[hh] = S[hh]
        outs, s_next = _gdn_chunk(per_head(xq), per_head(xk), per_head(xv), per_head(hq, keep), per_head(hk, keep),
                                  per_head(hv, keep), per_head(gate), ab[...], S, per_head(cwq), per_head(cwk),
                                  per_head(cwv), al[...], dt[...], on[...], heads=heads)
        for hh in range(HB):
            o_ref[:, _lanes(hh)] = outs[hh].astype(o_ref.dtype)
            s_ref[heads[hh]] = s_next[hh]

    hb = HEADS
    cw_spec = lambda col0: pl.BlockSpec((CONV_K, HBW), lambda b, n, g: (0, col0 // HB + g))
    in_specs = [_chunk_specs(nc, 0), _chunk_specs(nc, hb), _chunk_specs(nc, 2 * hb),
                _halo_spec(nc, 0, False), _halo_spec(nc, hb, False), _halo_spec(nc, 2 * hb, False),
                _chunk_specs(nc, 3 * hb),
                pl.BlockSpec((CHUNK, 128), lambda b, n, g: (b * nc + n, 4 * hb)),
                cw_spec(0), cw_spec(hb), cw_spec(2 * hb),
                _const_spec((1, 128)), _const_spec((1, 128)), _const_spec((1, 128))]
    out_specs = [_chunk_specs(nc, 0), _state_spec(nc)]
    return _pcall(
        body, name=name, grid=(bsz, nc, HEADS // HB), in_specs=in_specs, out_specs=out_specs,
        out_shape=(jax.ShapeDtypeStruct((n_tok, D_MODEL), BF16),
                   jax.ShapeDtypeStruct((bsz, nc, HEADS, HEAD_DIM, HEAD_DIM), F32)),
        scratch_shapes=[pltpu.VMEM((HEADS, HEAD_DIM, HEAD_DIM), F32)],
        dims=("arbitrary", "arbitrary", "arbitrary"),
        args=(proj, proj, proj, proj, proj, proj, proj, proj, conv_w, conv_w, conv_w, alog, dtb, onorm),
        carries=carries)


def gdn_backward(proj, conv_w, alog, dtb, onorm, s_saved, d_out, bsz, name, carries=()):
    n_tok = proj.shape[0]
    nc = n_tok // bsz // CHUNK
    assert HB == HEADS

    def body(xq, xk, xv, hq, hk, hv, gate, ab, cwq, cwk, cwv, al, dt, on, ssave, do,
             dp_ref, dcw_ref, dal_ref, ddt_ref, don_ref, ds_ref, dhalo_ref):
        b, n, g = pl.program_id(0), pl.program_id(1), pl.program_id(2)
        nr = nc - 1 - n

        @pl.when((b == 0) & (n == 0) & (g == 0))
        def _():
            dcw_ref[...] = jnp.zeros_like(dcw_ref)
            dal_ref[...] = jnp.zeros_like(dal_ref)
            ddt_ref[...] = jnp.zeros_like(ddt_ref)
            don_ref[...] = jnp.zeros_like(don_ref)

        @pl.when(n == 0)
        def _():
            for hh in range(HB):
                ds_ref[_head(g, hh)] = jnp.zeros((HEAD_DIM, HEAD_DIM), F32)
                dhalo_ref[_head(g, hh)] = jnp.zeros((3, HALO, HEAD_DIM), F32)

        keep = jnp.where(nr > 0, 1.0, 0.0).astype(F32)
        pad = jnp.zeros((CHUNK - HALO, HEAD_DIM), F32)
        heads = [_head(g, hh) for hh in range(HB)]
        per_head = lambda ref, scale=None: [ref[:, _lanes(hh)] if scale is None else ref[:, _lanes(hh)] * scale
                                            for hh in range(HB)]
        args = (per_head(xq), per_head(xk), per_head(xv), per_head(hq, keep), per_head(hk, keep), per_head(hv, keep),
                per_head(gate), ab[...], [ssave[hh] for hh in range(HB)], per_head(cwq), per_head(cwk), per_head(cwv),
                al[...], dt[...], on[...])
        _, vjp = jax.vjp(functools.partial(_gdn_chunk, heads=heads), *args)
        (gxq, gxk, gxv, ghq, ghk, ghv, ggate, gab_sum, gS, gcq, gck, gcv, gal_sum, gdt_sum, gon_sum) = vjp(
            ([do[:, _lanes(hh)] for hh in range(HB)], [ds_ref[h] for h in heads]))
        d = D_MODEL
        for hh in range(HB):
            head, c0 = heads[hh], hh * HEAD_DIM
            for part, gx in enumerate((gxq, gxk, gxv)):
                full = gx[hh] + jnp.concatenate([pad, dhalo_ref[head, part]], axis=0)
                dp_ref[:, part * d + c0:part * d + c0 + HEAD_DIM] = full.astype(dp_ref.dtype)
            dp_ref[:, 3 * d + c0:3 * d + c0 + HEAD_DIM] = ggate[hh].astype(dp_ref.dtype)
            dcw_ref[head, 0] += gcq[hh]
            dcw_ref[head, 1] += gck[hh]
            dcw_ref[head, 2] += gcv[hh]
            ds_ref[head] = gS[hh]
            dhalo_ref[head, 0] = ghq[hh] * keep
            dhalo_ref[head, 1] = ghk[hh] * keep
            dhalo_ref[head, 2] = ghv[hh] * keep

        dp_ref[:, GDN_MAIN:GDN_IN_PAD] = gab_sum.astype(dp_ref.dtype)
        dal_ref[...] += gal_sum
        ddt_ref[...] += gdt_sum
        don_ref[...] += gon_sum

    hb = HEADS
    cw_spec = lambda col0: pl.BlockSpec((CONV_K, HBW), lambda b, n, g: (0, col0 // HB + g))
    in_specs = [_chunk_specs_rev(nc, 0), _chunk_specs_rev(nc, hb), _chunk_specs_rev(nc, 2 * hb),
                _halo_spec(nc, 0, True), _halo_spec(nc, hb, True), _halo_spec(nc, 2 * hb, True),
                _chunk_specs_rev(nc, 3 * hb),
                pl.BlockSpec((CHUNK, 128), lambda b, n, g: (b * nc + (nc - 1 - n), 4 * hb)),
                cw_spec(0), cw_spec(hb), cw_spec(2 * hb),
                _const_spec((1, 128)), _const_spec((1, 128)), _const_spec((1, 128)),
                _state_spec(nc, True),
                _chunk_specs_rev(nc, 0)]
    out_specs = [
        pl.BlockSpec((CHUNK, GDN_IN_PAD), lambda b, n, g: (b * nc + (nc - 1 - n), 0)),
        _const_spec((HEADS, 3, CONV_K, HEAD_DIM)), _const_spec((1, 128)), _const_spec((1, 128)), _const_spec((1, 128))]
    row = jax.ShapeDtypeStruct((1, 128), F32)
    return _pcall(
        body, name=name, grid=(bsz, nc, HEADS // HB), in_specs=in_specs, out_specs=out_specs,
        out_shape=(jax.ShapeDtypeStruct((n_tok, GDN_IN_PAD), BF16),
                   jax.ShapeDtypeStruct((HEADS, 3, CONV_K, HEAD_DIM), F32), row, row, row),
        scratch_shapes=[pltpu.VMEM((HEADS, HEAD_DIM, HEAD_DIM), F32), pltpu.VMEM((HEADS, 3, HALO, HEAD_DIM), F32)],
        dims=("arbitrary", "arbitrary", "arbitrary"),
        args=(proj, proj, proj, proj, proj, proj, proj, proj, conv_w, conv_w, conv_w, alog, dtb, onorm, s_saved, d_out),
        carries=carries)


def hgrn_forward(proj, lbl, layer, bsz, carries=()):
    n_tok = proj.shape[0]
    nc = n_tok // bsz // CHUNK

    def body(qp, fp, vi, lb, o_ref, ssave_ref, s_ref):
        n, g = pl.program_id(1), pl.program_id(2)

        @pl.when(n == 0)
        def _():
            for hh in range(HB):
                s_ref[_head(g, hh)] = jnp.zeros((HEAD_DIM, HEAD_DIM), F32)

        heads = [_head(g, hh) for hh in range(HB)]
        per_head = lambda ref: [ref[:, _lanes(hh)] for hh in range(HB)]
        S = [s_ref[h] for h in heads]
        for hh in range(HB):
            ssave_ref[hh] = S[hh]
        outs, s_next = _hgrn_chunk(per_head(qp), per_head(fp), per_head(vi), S, per_head(lb), layer=layer)
        for hh in range(HB):
            o_ref[:, _lanes(hh)] = outs[hh]
            s_ref[heads[hh]] = s_next[hh]

    hb = HEADS
    in_specs = [_chunk_specs(nc, 0), _chunk_specs(nc, hb), _chunk_specs(nc, 2 * hb),
                pl.BlockSpec((DEPTH, HBW), lambda b, n, g: (0, g))]
    out_specs = [_chunk_specs(nc, 0), _state_spec(nc)]
    return _pcall(
        body, name=f"hgrn_fwd{layer}", grid=(bsz, nc, HEADS // HB), in_specs=in_specs, out_specs=out_specs,
        out_shape=(jax.ShapeDtypeStruct((n_tok, D_MODEL), F32),
                   jax.ShapeDtypeStruct((bsz, nc, HEADS, HEAD_DIM, HEAD_DIM), F32)),
        scratch_shapes=[pltpu.VMEM((HEADS, HEAD_DIM, HEAD_DIM), F32)],
        dims=("arbitrary", "arbitrary", "arbitrary"), args=(proj, proj, proj, lbl), carries=carries)


def hgrn_backward(proj, lbl, s_saved, d_o, d_gate, layer, bsz, carries=()):
    n_tok = proj.shape[0]
    nc = n_tok // bsz // CHUNK
    assert HB == HEADS

    def body(qp, fp, vi, lb, ssave, do, dgt, dp_ref, dlb_ref, ds_ref):
        b, n, g = pl.program_id(0), pl.program_id(1), pl.program_id(2)

        @pl.when((b == 0) & (n == 0) & (g == 0))
        def _():
            dlb_ref[...] = jnp.zeros_like(dlb_ref)

        @pl.when(n == 0)
        def _():
            for hh in range(HB):
                ds_ref[_head(g, hh)] = jnp.zeros((HEAD_DIM, HEAD_DIM), F32)

        heads = [_head(g, hh) for hh in range(HB)]
        per_head = lambda ref: [ref[:, _lanes(hh)] for hh in range(HB)]
        _, vjp = jax.vjp(functools.partial(_hgrn_chunk, layer=layer), per_head(qp), per_head(fp), per_head(vi),
                         [ssave[hh] for hh in range(HB)], per_head(lb))
        gq, gf, gv, gS, glb = vjp((per_head(do), [ds_ref[h] for h in heads]))
        for hh in range(HB):
            head, c0, d = heads[hh], hh * HEAD_DIM, D_MODEL
            dp_ref[:, c0:c0 + HEAD_DIM] = gq[hh].astype(dp_ref.dtype)
            dp_ref[:, d + c0:d + c0 + HEAD_DIM] = gf[hh].astype(dp_ref.dtype)
            dp_ref[:, 2 * d + c0:2 * d + c0 + HEAD_DIM] = gv[hh].astype(dp_ref.dtype)
            dlb_ref[head] += glb[hh]
            ds_ref[head] = gS[hh]
        dp_ref[:, 3 * D_MODEL:4 * D_MODEL] = dgt[...]

    hb = HEADS
    in_specs = [_chunk_specs_rev(nc, 0), _chunk_specs_rev(nc, hb), _chunk_specs_rev(nc, 2 * hb),
                pl.BlockSpec((DEPTH, HBW), lambda b, n, g: (0, g)),
                _state_spec(nc, True),
                _chunk_specs_rev(nc, 0), _chunk_specs_rev(nc, 0)]
    out_specs = [pl.BlockSpec((CHUNK, 4 * D_MODEL), lambda b, n, g: (b * nc + (nc - 1 - n), 0)),
                 _const_spec((HEADS, DEPTH, HEAD_DIM))]
    return _pcall(
        body, name=f"hgrn_bwd{layer}", grid=(bsz, nc, HEADS // HB), in_specs=in_specs, out_specs=out_specs,
        out_shape=(jax.ShapeDtypeStruct((n_tok, 4 * D_MODEL), BF16), jax.ShapeDtypeStruct((HEADS, DEPTH, HEAD_DIM), F32)),
        scratch_shapes=[pltpu.VMEM((HEADS, HEAD_DIM, HEAD_DIM), F32)],
        dims=("arbitrary", "arbitrary", "arbitrary"), args=(proj, proj, proj, lbl, s_saved, d_o, d_gate),
        carries=carries)


ROW_TILE = 512
MM_ROW_TILE = 1024
MM_VMEM_BUDGET = 36 * 1024 * 1024


def _tile(n):
    for cand in (1024, 512, 1408, 384, 256, 128):
        if n % cand == 0:
            return cand
    return n


def _rmsnorm(x, w):
    return x * lax.rsqrt(jnp.mean(x * x, axis=1, keepdims=True) + EPS) * w


def norm_matmul(h, nw, w, relu2, name, carries=()):
    n_tok, d = h.shape
    slots = w.ndim == 3
    n_out = w.shape[0] * w.shape[2] if slots else w.shape[1]
    tm, tn = min(n_tok, MM_ROW_TILE), _tile(w.shape[2] if slots else n_out)
    if slots:
        per = w.shape[2] // tn
        w_spec = pl.BlockSpec((None, d, tn), lambda i, j: (j // per, 0, j % per))
    else:
        w_spec = pl.BlockSpec((d, tn), lambda i, j: (0, j))

    def body(h_ref, nw_ref, w_ref, y_ref, *outs):
        @pl.when(pl.program_id(1) == 0)
        def _():
            y_ref[...] = _rmsnorm(h_ref[...], nw_ref[...]).astype(BF16)

        acc = jnp.dot(y_ref[...], w_ref[...], preferred_element_type=F32)
        if relu2:
            r = jnp.maximum(acc, 0.0)
            outs[0][...] = (r * r).astype(BF16)
            outs[1][...] = r.astype(BF16)
        else:
            outs[0][...] = acc

    o_spec = pl.BlockSpec((tm, tn), lambda i, j: (i, j))
    if relu2:
        outs = (jax.ShapeDtypeStruct((n_tok, n_out), BF16),) * 2
    else:
        outs = (jax.ShapeDtypeStruct((n_tok, n_out), F32),)
    return _pcall(
        body, name=name, grid=(n_tok // tm, n_out // tn),
        in_specs=[pl.BlockSpec((tm, d), lambda i, j: (i, 0)), pl.BlockSpec((1, d), lambda i, j: (0, 0)), w_spec],
        out_specs=[pl.BlockSpec((tm, d), lambda i, j: (i, 0))] + [o_spec] * len(outs),
        out_shape=(jax.ShapeDtypeStruct((n_tok, d), BF16),) + outs,
        dims=("parallel", "arbitrary"), args=(h, nw, w), carries=carries)


def _mm_tiles(m, n, k, extra_bytes):
    tn = _tile(n)
    for tm in (1024, 512, 256, 128):
        if m % tm == 0 and 2 * (2 * tm * k + 2 * k * tn + (4 + extra_bytes) * tm * tn) <= MM_VMEM_BUDGET:
            return tm, tn
    return min(m, 128), tn


def matmul(a, b, mode, name, out_dtype=F32, extra=None, epilogue=None, shards=1, carries=()):
    slots = b.shape[0] if (mode == "nt" and b.ndim == 3) else 0
    if mode == "nn":
        (m, k), n = a.shape, b.shape[1]
    elif mode == "nt":
        (m, k), n = a.shape, (b.shape[1] if slots else b.shape[0])
    else:
        (k, m), n = a.shape, b.shape[1]
    tm, tn = _mm_tiles(m, n // shards, k, 0 if extra is None else extra.dtype.itemsize)
    a_spec = pl.BlockSpec((k, tm), lambda i, j: (0, i)) if mode == "tn" else pl.BlockSpec((tm, k), lambda i, j: (i, 0))
    if slots:
        kb = b.shape[2]
        b_specs = [pl.BlockSpec((None, tn, kb), lambda i, j, s=s: (s, j, 0)) for s in range(slots)]
    elif mode == "nt":
        b_specs = [pl.BlockSpec((tn, k), lambda i, j: (j, 0))]
    else:
        b_specs = [pl.BlockSpec((k, tn), lambda i, j: (0, j))]
    nb = len(b_specs)
    dims = _DIMS[mode]

    def body(*refs):
        a_ref, b_refs = refs[0], refs[1:1 + nb]
        e_ref = refs[1 + nb] if extra is not None else None
        o_ref = refs[-1]
        if slots:
            acc = None
            for s in range(slots):
                part = lax.dot_general(a_ref[:, s * kb:(s + 1) * kb].astype(BF16), b_refs[s][...].astype(BF16), dims,
                                       preferred_element_type=F32)
                acc = part if acc is None else acc + part
        else:
            acc = lax.dot_general(a_ref[...].astype(BF16), b_refs[0][...].astype(BF16), dims,
                                  preferred_element_type=F32)
        if epilogue == "add":
            acc = e_ref[...] + acc
        elif epilogue == "mul2":
            acc = acc * (2.0 * e_ref[...].astype(F32))
        o_ref[...] = acc.astype(o_ref.dtype)

    in_specs = [a_spec] + b_specs
    args = [a] + [b] * nb
    if extra is not None:
        in_specs.append(pl.BlockSpec((tm, tn), lambda i, j: (i, j)))
        args.append(extra)
    if shards > 1:
        per = n // shards // tn
        out_spec = pl.BlockSpec((None, tm, tn), lambda i, j: (j // per, i, j % per))
        out_shape = jax.ShapeDtypeStruct((shards, m, n // shards), out_dtype)
    else:
        out_spec = pl.BlockSpec((tm, tn), lambda i, j: (i, j))
        out_shape = jax.ShapeDtypeStruct((m, n), out_dtype)
    (res,), cres = _pcall(
        body, name=name, grid=(m // tm, n // tn), in_specs=in_specs, out_specs=[out_spec], out_shape=[out_shape],
        dims=("parallel", "arbitrary"), args=args, carries=carries)
    return res, cres


def norm_backward(h, nw, dy, dres, name, carries=()):
    n_tok, d = h.shape
    tm = min(n_tok, ROW_TILE)

    def body(h_ref, nw_ref, dy_ref, dr_ref, dh_ref, dhb_ref, dnw_ref):
        @pl.when(pl.program_id(0) == 0)
        def _():
            dnw_ref[...] = jnp.zeros_like(dnw_ref)

        _, vjp = jax.vjp(_rmsnorm, h_ref[...], nw_ref[...])
        gh, gw = vjp(dy_ref[...])
        dh = dr_ref[...] + gh
        dh_ref[...] = dh
        dhb_ref[...] = dh.astype(BF16)
        dnw_ref[...] += gw

    row = pl.BlockSpec((tm, d), lambda i: (i, 0))
    vec = pl.BlockSpec((1, d), lambda i: (0, 0))
    return _pcall(
        body, name=name, grid=(n_tok // tm,), in_specs=[row, vec, row, row], out_specs=[row, row, vec],
        out_shape=(jax.ShapeDtypeStruct((n_tok, d), F32), jax.ShapeDtypeStruct((n_tok, d), BF16),
                   jax.ShapeDtypeStruct((1, d), F32)),
        dims=("arbitrary",), args=(h, nw, dy, dres), carries=carries)


def _hgrn_post(o, gate, gw):
    return _rmsnorm(o, gw) * _silu(gate)


def hgrn_post_forward(o, proj, gw, name):
    n_tok, d = o.shape
    tm = min(n_tok, ROW_TILE)

    def body(o_ref, g_ref, w_ref, y_ref):
        y_ref[...] = _hgrn_post(o_ref[...], g_ref[...], w_ref[...]).astype(BF16)

    row = pl.BlockSpec((tm, d), lambda i: (i, 0))
    return pl.pallas_call(
        body, name=name, grid=(n_tok // tm,),
        in_specs=[row, pl.BlockSpec((tm, d), lambda i: (i, 3)), pl.BlockSpec((1, d), lambda i: (0, 0))],
        out_specs=row, out_shape=jax.ShapeDtypeStruct((n_tok, d), BF16),
        compiler_params=_cparams(("parallel",)),
    )(o, proj, gw)


def hgrn_post_backward(o, proj, gw, dy, name):
    n_tok, d = o.shape
    tm = min(n_tok, ROW_TILE)

    def body(o_ref, g_ref, w_ref, dy_ref, do_ref, dg_ref, dw_ref):
        @pl.when(pl.program_id(0) == 0)
        def _():
            dw_ref[...] = jnp.zeros_like(dw_ref)

        _, vjp = jax.vjp(_hgrn_post, o_ref[...], g_ref[...], w_ref[...])
        go, gg, gw_ = vjp(dy_ref[...])
        do_ref[...] = go
        dg_ref[...] = gg.astype(BF16)
        dw_ref[...] += gw_

    row = pl.BlockSpec((tm, d), lambda i: (i, 0))
    vec = pl.BlockSpec((1, d), lambda i: (0, 0))
    return pl.pallas_call(
        body, name=name, grid=(n_tok // tm,),
        in_specs=[row, pl.BlockSpec((tm, d), lambda i: (i, 3)), vec, row], out_specs=[row, row, vec],
        out_shape=(jax.ShapeDtypeStruct((n_tok, d), F32), jax.ShapeDtypeStruct((n_tok, d), BF16),
                   jax.ShapeDtypeStruct((1, d), F32)),
        compiler_params=_cparams(("arbitrary",)),
    )(o, proj, gw, dy)


def loss_head(h, nw, target):
    n_tok, d = h.shape
    tm = min(n_tok, ROW_TILE)

    def body(h_ref, nw_ref, t_ref, loss_ref, dh_ref, dhb_ref, dnw_ref):
        @pl.when(pl.program_id(0) == 0)
        def _():
            dnw_ref[...] = jnp.zeros_like(dnw_ref)
            loss_ref[...] = jnp.zeros_like(loss_ref)

        out, vjp = jax.vjp(_rmsnorm, h_ref[...], nw_ref[...])
        err = out - t_ref[...]
        part = 0.5 * jnp.sum(jnp.sum(err * err, axis=1, keepdims=True), axis=0, keepdims=True) / d
        loss_ref[...] += jnp.broadcast_to(part, loss_ref.shape)
        gh, gw = vjp(err / d)
        dh_ref[...] = gh
        dhb_ref[...] = gh.astype(BF16)
        dnw_ref[...] += gw

    row = pl.BlockSpec((tm, d), lambda i: (i, 0))
    vec = pl.BlockSpec((1, d), lambda i: (0, 0))
    return pl.pallas_call(
        body, name="loss_head", grid=(n_tok // tm,), in_specs=[row, vec, row],
        out_specs=[pl.BlockSpec((1, 128), lambda i: (0, 0)), row, row, vec],
        out_shape=(jax.ShapeDtypeStruct((1, 128), F32), jax.ShapeDtypeStruct((n_tok, d), F32),
                   jax.ShapeDtypeStruct((n_tok, d), BF16), jax.ShapeDtypeStruct((1, d), F32)),
        compiler_params=_cparams(("arbitrary",)),
    )(h, nw, target)


def _rows2d(shape):
    if len(shape) == 1:
        return (1, shape[0])
    return (math.prod(shape[:-1]), shape[-1])


def adamw(w, g, m, v, name):
    shape = w.shape
    r, c = _rows2d(shape)
    tr = r if r <= 256 else 256
    c1 = 1.0 / (1.0 - ADAM_B1 ** ADAM_STEP)
    c2 = 1.0 / (1.0 - ADAM_B2 ** ADAM_STEP)

    def body(w_ref, g_ref, m_ref, v_ref, d_ref, nm_ref, nv_ref):
        gg = g_ref[...]
        nm = ADAM_B1 * m_ref[...] + (1.0 - ADAM_B1) * gg
        nv = ADAM_B2 * v_ref[...] + (1.0 - ADAM_B2) * (gg * gg)
        d_ref[...] = -ADAM_LR * ((nm * c1) / (jnp.sqrt(nv * c2) + ADAM_EPS) + ADAM_WD * w_ref[...])
        nm_ref[...] = nm
        nv_ref[...] = nv

    spec = pl.BlockSpec((tr, c), lambda i: (i, 0))
    sds = jax.ShapeDtypeStruct((r, c), F32)
    outs = pl.pallas_call(
        body, name=name, grid=(r // tr,), in_specs=[spec] * 4, out_specs=[spec] * 3, out_shape=(sds,) * 3,
        compiler_params=_cparams(("parallel",)),
    )(w.reshape(r, c), g.reshape(r, c), m.reshape(r, c), v.reshape(r, c))
    return tuple(o.reshape(shape) for o in outs)


def add_slots(parts, name):
    s, r, c = parts.shape
    tr = r if r <= 256 else 256

    def body(p_ref, o_ref):
        acc = p_ref[0]
        for t in range(1, s):
            acc = acc + p_ref[t]
        o_ref[...] = acc

    return pl.pallas_call(
        body, name=name, grid=(r // tr,), in_specs=[pl.BlockSpec((s, tr, c), lambda i: (0, i, 0))],
        out_specs=pl.BlockSpec((tr, c), lambda i: (i, 0)), out_shape=jax.ShapeDtypeStruct((r, c), F32),
        compiler_params=_cparams(("parallel",)),
    )(parts)


def add_pair(a, b, name):
    s, r, c = a.shape
    tr = r if r <= 256 else 256

    def body(a_ref, b_ref, o_ref):
        o_ref[...] = a_ref[...] + b_ref[...]

    spec = pl.BlockSpec((None, tr, c), lambda t, i: (t, i, 0))
    return pl.pallas_call(
        body, name=name, grid=(s, r // tr), in_specs=[spec, spec], out_specs=spec,
        out_shape=jax.ShapeDtypeStruct((s, r, c), F32), compiler_params=_cparams(("parallel", "parallel")),
    )(a, b)


N_CHIPS = 4
N_DEV = 8
_ANY = pl.BlockSpec(memory_space=pl.ANY)


def _mesh_pos():
    return lax.axis_index("x"), lax.axis_index("y"), lax.axis_index("c")


def _other_chips(x, y):
    ps = [(1 - x, y), (x, 1 - y), (1 - x, 1 - y)]
    return [(p, 2 * p[0] + p[1]) for p in ps]


def _remote(src, dst, send_sem, recv_sem, dev):
    return pltpu.make_async_remote_copy(src_ref=src, dst_ref=dst, send_sem=send_sem, recv_sem=recv_sem,
                                        device_id=dev, device_id_type=MESH)


def gather_chips(arrs, name):
    n = len(arrs)

    def body(*refs):
        ins, outs = refs[:n], refs[n:2 * n]
        send_sems, recv_sems, local_sems = refs[2 * n:]
        x, y, c = _mesh_pos()
        me = 2 * x + y
        peers = _other_chips(x, y)
        locs, sends = [], []
        for a in range(n):
            cp = pltpu.make_async_copy(ins[a], outs[a].at[me], local_sems.at[a])
            cp.start()
            locs.append(cp)
            for j, ((px, py), _) in enumerate(peers):
                r = _remote(ins[a], outs[a].at[me], send_sems.at[a, j], recv_sems.at[a, j], (px, py, c))
                r.start()
                sends.append(r)
        for a in range(n):
            for j, ((px, py), t) in enumerate(peers):
                _remote(ins[a], outs[a].at[t], send_sems.at[a, j], recv_sems.at[a, j], (px, py, c)).wait_recv()
        for r in sends:
            r.wait_send()
        for cp in locs:
            cp.wait()

    return pl.pallas_call(
        body, name=name, in_specs=[_ANY] * n, out_specs=[_ANY] * n,
        out_shape=[jax.ShapeDtypeStruct((N_CHIPS,) + a.shape, a.dtype) for a in arrs],
        scratch_shapes=[pltpu.SemaphoreType.DMA((n, 3)), pltpu.SemaphoreType.DMA((n, 3)), pltpu.SemaphoreType.DMA((n,))],
    )(*arrs)


def exchange_sibling_half(gs, name):
    n = len(gs)

    def body(*refs):
        ins, outs = refs[:n], refs[n:2 * n]
        send_sems, recv_sems = refs[2 * n:]
        x, y, c = _mesh_pos()
        sends = []
        for a in range(n):
            half = gs[a].shape[1] // 2
            src = ins[a].at[:, pl.ds((1 - c) * half, half), :]
            r = _remote(src, outs[a], send_sems.at[a], recv_sems.at[a], (x, y, 1 - c))
            r.start()
            sends.append(r)
        for r in sends:
            r.wait_recv()
        for r in sends:
            r.wait_send()

    return pl.pallas_call(
        body, name=name, in_specs=[_ANY] * n, out_specs=[_ANY] * n,
        out_shape=[jax.ShapeDtypeStruct((g.shape[0], g.shape[1] // 2, g.shape[2]), g.dtype) for g in gs],
        scratch_shapes=[pltpu.SemaphoreType.DMA((n,)), pltpu.SemaphoreType.DMA((n,))],
    )(*gs)


def exchange_chips(ps, name):
    n = len(ps)

    def body(*refs):
        ins, outs = refs[:n], refs[n:2 * n]
        send_sems, recv_sems, local_sems = refs[2 * n:]
        x, y, c = _mesh_pos()
        me = 2 * x + y
        peers = _other_chips(x, y)
        locs, sends = [], []
        for a in range(n):
            cp = pltpu.make_async_copy(ins[a].at[me], outs[a].at[me], local_sems.at[a])
            cp.start()
            locs.append(cp)
            for j, ((px, py), t) in enumerate(peers):
                r = _remote(ins[a].at[t], outs[a].at[me], send_sems.at[a, j], recv_sems.at[a, j], (px, py, c))
                r.start()
                sends.append(r)
        for a in range(n):
            for j, ((px, py), t) in enumerate(peers):
                _remote(ins[a].at[t], outs[a].at[t], send_sems.at[a, j], recv_sems.at[a, j], (px, py, c)).wait_recv()
        for r in sends:
            r.wait_send()
        for cp in locs:
            cp.wait()

    return pl.pallas_call(
        body, name=name, in_specs=[_ANY] * n, out_specs=[_ANY] * n,
        out_shape=[jax.ShapeDtypeStruct(p.shape, p.dtype) for p in ps],
        scratch_shapes=[pltpu.SemaphoreType.DMA((n, 3)), pltpu.SemaphoreType.DMA((n, 3)), pltpu.SemaphoreType.DMA((n,))],
    )(*ps)


def share_sibling(groups, name):
    flat = [f for grp in groups for f in grp]
    n = len(flat)
    nw = len(groups)

    def body(*refs):
        ins, outs = refs[:n], refs[n:n + nw]
        send_sems, recv_sems, local_sems = refs[n + nw:]
        x, y, c = _mesh_pos()
        locs, sends, k = [], [], 0
        for w, grp in enumerate(groups):
            for l in range(len(grp)):
                half = grp[l].shape[0]
                dst = outs[w].at[l, pl.ds(c * half, half), :]
                cp = pltpu.make_async_copy(ins[k], dst, local_sems.at[k])
                cp.start()
                locs.append(cp)
                r = _remote(ins[k], dst, send_sems.at[k], recv_sems.at[k], (x, y, 1 - c))
                r.start()
                sends.append(r)
                k += 1
        k = 0
        for w, grp in enumerate(groups):
            for l in range(len(grp)):
                half = grp[l].shape[0]
                theirs = outs[w].at[l, pl.ds((1 - c) * half, half), :]
                _remote(ins[k], theirs, send_sems.at[k], recv_sems.at[k], (x, y, 1 - c)).wait_recv()
                k += 1
        for r in sends:
            r.wait_send()
        for cp in locs:
            cp.wait()

    return pl.pallas_call(
        body, name=name, in_specs=[_ANY] * n, out_specs=[_ANY] * nw,
        out_shape=[jax.ShapeDtypeStruct((len(grp), 2 * grp[0].shape[0], grp[0].shape[1]), F32) for grp in groups],
        scratch_shapes=[pltpu.SemaphoreType.DMA((n,)), pltpu.SemaphoreType.DMA((n,)), pltpu.SemaphoreType.DMA((n,))],
    )(*flat)


def gather_all(small, name):
    flips = [(fx, fy, fc) for fx in (0, 1) for fy in (0, 1) for fc in (0, 1)][1:]

    def body(in_ref, out_ref, send_sems, recv_sems, local_sem):
        x, y, c = _mesh_pos()
        me = 4 * x + 2 * y + c
        cp = pltpu.make_async_copy(in_ref, out_ref.at[me], local_sem)
        cp.start()
        peers = [((1 - x) if fx else x, (1 - y) if fy else y, (1 - c) if fc else c) for fx, fy, fc in flips]
        sends = []
        for k, p in enumerate(peers):
            r = _remote(in_ref, out_ref.at[me], send_sems.at[k], recv_sems.at[k], p)
            r.start()
            sends.append(r)
        for k, p in enumerate(peers):
            _remote(in_ref, out_ref.at[4 * p[0] + 2 * p[1] + p[2]], send_sems.at[k], recv_sems.at[k], p).wait_recv()
        for r in sends:
            r.wait_send()
        cp.wait()

    return pl.pallas_call(
        body, name=name, in_specs=[_ANY], out_specs=_ANY,
        out_shape=jax.ShapeDtypeStruct((N_DEV,) + small.shape, small.dtype),
        scratch_shapes=[pltpu.SemaphoreType.DMA((N_DEV - 1,)), pltpu.SemaphoreType.DMA((N_DEV - 1,)),
                        pltpu.SemaphoreType.DMA],
    )(small)


BIG = ("gdn_w_in", "gdn_w_out", "hgrn_w_in", "hgrn_w_out", "mlp_w_up", "mlp_w_down")
WEIGHTS = ("gdn_w_in", "gdn_conv", "gdn_a_log", "gdn_dt_bias", "gdn_onorm", "gdn_w_out", "hgrn_w_in", "hgrn_lb_logits",
           "hgrn_gnorm", "hgrn_w_out", "norm_mix", "norm_mlp", "mlp_w_up", "mlp_w_down", "norm_final")
SMALL_ROWS = 96


def _pad_lanes(v, n):
    return jnp.pad(v, [(0, 0)] * (v.ndim - 1) + [(0, n - v.shape[-1])])


def _local_grads(x, target, w):
    bsz, t_len, d = x.shape
    n_tok = bsz * t_len
    h = x.reshape(n_tok, d)
    lbl = w["hgrn_lb_logits"]
    saved = []
    for i in range(DEPTH):
        j = i // 2
        nmix = w["norm_mix"][i][None, :]
        if i % 2 == 0:
            y, proj = norm_matmul(h, nmix, w["gdn_w_in"][j], False, f"in_proj{i}")
            al = _pad_lanes(w["gdn_a_log"][j][None, :], 128)
            dtb = _pad_lanes(w["gdn_dt_bias"][j][None, :], 128)
            on = w["gdn_onorm"][j][None, :]
            og, ssave = gdn_forward(proj, w["gdn_conv"][j], al, dtb, on, bsz)
            mix = (proj, ssave, al, dtb, on)
            w_out = w["gdn_w_out"][j]
        else:
            y, proj = norm_matmul(h, nmix, w["hgrn_w_in"][j], False, f"in_proj{i}")
            o, ssave = hgrn_forward(proj, lbl, i, bsz)
            gn = w["hgrn_gnorm"][j][None, :]
            og = hgrn_post_forward(o, proj, gn, f"hgrn_post{i}")
            mix = (proj, ssave, o, gn)
            w_out = w["hgrn_w_out"][j]
        h1 = matmul(og, w_out, "nn", f"out_proj{i}", extra=h, epilogue="add")
        nmlp = w["norm_mlp"][i][None, :]
        z, a, r = norm_matmul(h1, nmlp, w["mlp_w_up"][i], True, f"mlp_up{i}")
        h2 = matmul(a, w["mlp_w_down"][i], "nn", f"mlp_down{i}", extra=h1, epilogue="add")
        saved.append((h, nmix, y, mix, og, w_out, h1, nmlp, z, a, r))
        h = h2
    loss_row, dh, d_nf = loss_head(h, w["norm_final"][None, :], target.reshape(n_tok, d))

    big = {k: [None] * (DEPTH if k.startswith("mlp") else DEPTH // 2) for k in BIG}
    d_nmix, d_nmlp = [None] * DEPTH, [None] * DEPTH
    d_conv, d_alog, d_dtb, d_onorm, d_gnorm = [None] * 2, [None] * 2, [None] * 2, [None] * 2, [None] * 2
    d_lbl = jnp.zeros((DEPTH, d), F32)
    for i in reversed(range(DEPTH)):
        j = i // 2
        h_in, nmix, y, mix, og, w_out, h1, nmlp, z, a, r = saved[i]
        du = matmul(dh, w["mlp_w_down"][i], "nt", f"d_mlp_act{i}", out_dtype=BF16, extra=r, epilogue="mul2")
        big["mlp_w_down"][i] = matmul(a, dh, "tn", f"dw_down{i}").reshape(N_CHIPS, -1, d)
        dz = matmul(du, w["mlp_w_up"][i], "nt", f"d_mlp_in{i}")
        big["mlp_w_up"][i] = matmul(z, du, "tn", f"dw_up{i}", shards=N_CHIPS)
        dh1, d_nmlp[i] = norm_backward(h1, nmlp, dz, dh, f"d_norm_mlp{i}")
        dog = matmul(dh1, w_out, "nt", f"d_mix_out{i}")
        dw_out = matmul(og, dh1, "tn", f"dw_out{i}").reshape(N_CHIPS, -1, d)
        if i % 2 == 0:
            proj, ssave, al, dtb, on = mix
            dq, dk, dv, dg, dab, dcw, dal, ddt, don = gdn_backward(proj, w["gdn_conv"][j], al, dtb, on, ssave, dog, bsz)
            dproj = jnp.concatenate([dq, dk, dv, dg, dab], axis=1)
            dy = matmul(dproj, w["gdn_w_in"][j], "nt", f"d_in_proj{i}")
            dw_in = matmul(y, dproj, "tn", f"dw_in{i}")[:, :GDN_IN]
            big["gdn_w_in"][j] = jnp.transpose(dw_in.reshape(d, N_CHIPS, GDN_IN // N_CHIPS), (1, 0, 2))
            big["gdn_w_out"][j] = dw_out
            d_conv[j] = jnp.transpose(dcw, (2, 1, 0, 3)).reshape(CONV_K, 3 * d)
            d_alog[j], d_dtb[j], d_onorm[j] = dal[0, :HEADS], ddt[0, :HEADS], don[0]
        else:
            proj, ssave, o, gn = mix
            do_raw, dgate, dgn = hgrn_post_backward(o, proj, gn, dog, f"d_hgrn_post{i}")
            dq, df, dv, dlb = hgrn_backward(proj, lbl, ssave, do_raw, i, bsz)
            dproj = jnp.concatenate([dq, df, dv, dgate], axis=1)
            dy = matmul(dproj, w["hgrn_w_in"][j], "nt", f"d_in_proj{i}")
            big["hgrn_w_in"][j] = matmul(y, dproj, "tn", f"dw_in{i}", shards=N_CHIPS)
            big["hgrn_w_out"][j] = dw_out
            d_gnorm[j] = dgn[0]
            d_lbl = d_lbl + jnp.transpose(dlb, (1, 0, 2)).reshape(DEPTH, d)
        dh, d_nmix[i] = norm_backward(h_in, nmix, dy, dh1, f"d_norm_mix{i}")
    small = {
        "gdn_conv": jnp.stack(d_conv), "gdn_a_log": jnp.stack(d_alog), "gdn_dt_bias": jnp.stack(d_dtb),
        "gdn_onorm": jnp.stack(d_onorm), "hgrn_lb_logits": d_lbl, "hgrn_gnorm": jnp.stack(d_gnorm),
        "norm_mix": jnp.concatenate(d_nmix, axis=0), "norm_mlp": jnp.concatenate(d_nmlp, axis=0), "norm_final": d_nf[0],
    }
    return loss_row, dh.reshape(x.shape), big, small


_SMALL_LAYOUT = {
    "norm_mix": (0, 4, D_MODEL), "norm_mlp": (8, 4, D_MODEL), "norm_final": (16, 1, D_MODEL),
    "hgrn_lb_logits": (24, 4, D_MODEL), "gdn_onorm": (32, 2, 128), "gdn_a_log": (40, 2, HEADS),
    "gdn_dt_bias": (48, 2, HEADS), "loss": (56, 1, 128), "gdn_conv": (64, 24, D_MODEL), "hgrn_gnorm": (88, 2, D_MODEL),
}


def _pack_small(small, loss_row):
    rows = []
    for name, (first, nrow, lanes) in _SMALL_LAYOUT.items():
        v = loss_row if name == "loss" else small[name]
        v = _pad_lanes(v.reshape(nrow, -1), D_MODEL)
        rows.append(jnp.pad(v, ((0, -nrow % 8), (0, 0))))
    return jnp.concatenate(rows, axis=0)


def _unpack_small(packed, name, shape):
    first, nrow, lanes = _SMALL_LAYOUT[name]
    return packed[first:first + nrow, :lanes].reshape(shape)


def kernel(x, gdn_w_in, gdn_conv, gdn_a_log, gdn_dt_bias, gdn_onorm, gdn_w_out, hgrn_w_in, hgrn_lb_logits, hgrn_gnorm, hgrn_w_out, norm_mix, norm_mlp, mlp_w_up, mlp_w_down, norm_final, loss_target, m_gdn_w_in, m_gdn_conv, m_gdn_a_log, m_gdn_dt_bias, m_gdn_onorm, m_gdn_w_out, m_hgrn_w_in, m_hgrn_lb_logits, m_hgrn_gnorm, m_hgrn_w_out, m_norm_mix, m_norm_mlp, m_mlp_w_up, m_mlp_w_down, m_norm_final, v_gdn_w_in, v_gdn_conv, v_gdn_a_log, v_gdn_dt_bias, v_gdn_onorm, v_gdn_w_out, v_hgrn_w_in, v_hgrn_lb_logits, v_hgrn_gnorm, v_hgrn_w_out, v_norm_mix, v_norm_mlp, v_mlp_w_up, v_mlp_w_down, v_norm_final):
    p = dict(gdn_w_in=gdn_w_in, gdn_conv=gdn_conv, gdn_a_log=gdn_a_log, gdn_dt_bias=gdn_dt_bias, gdn_onorm=gdn_onorm,
             gdn_w_out=gdn_w_out, hgrn_w_in=hgrn_w_in, hgrn_lb_logits=hgrn_lb_logits, hgrn_gnorm=hgrn_gnorm,
             hgrn_w_out=hgrn_w_out, norm_mix=norm_mix, norm_mlp=norm_mlp, mlp_w_up=mlp_w_up, mlp_w_down=mlp_w_down,
             norm_final=norm_final)
    m = dict(gdn_w_in=m_gdn_w_in, gdn_conv=m_gdn_conv, gdn_a_log=m_gdn_a_log, gdn_dt_bias=m_gdn_dt_bias,
             gdn_onorm=m_gdn_onorm, gdn_w_out=m_gdn_w_out, hgrn_w_in=m_hgrn_w_in, hgrn_lb_logits=m_hgrn_lb_logits,
             hgrn_gnorm=m_hgrn_gnorm, hgrn_w_out=m_hgrn_w_out, norm_mix=m_norm_mix, norm_mlp=m_norm_mlp,
             mlp_w_up=m_mlp_w_up, mlp_w_down=m_mlp_w_down, norm_final=m_norm_final)
    v = dict(gdn_w_in=v_gdn_w_in, gdn_conv=v_gdn_conv, gdn_a_log=v_gdn_a_log, gdn_dt_bias=v_gdn_dt_bias,
             gdn_onorm=v_gdn_onorm, gdn_w_out=v_gdn_w_out, hgrn_w_in=v_hgrn_w_in, hgrn_lb_logits=v_hgrn_lb_logits,
             hgrn_gnorm=v_hgrn_gnorm, hgrn_w_out=v_hgrn_w_out, norm_mix=v_norm_mix, norm_mlp=v_norm_mlp,
             mlp_w_up=v_mlp_w_up, mlp_w_down=v_mlp_w_down, norm_final=v_norm_final)
    xi, yi, ci = _mesh_pos()
    chip = 2 * xi + yi
    d = D_MODEL

    sharded = list(BIG) + ["gdn_conv", "hgrn_gnorm"]
    got = dict(zip(sharded, gather_chips([p[k].astype(BF16) if k in BIG else p[k] for k in sharded], "gather_weights")))
    cols = lambda g: jnp.transpose(g, (1, 2, 0, 3)).reshape(g.shape[1], g.shape[2], -1)
    rows = lambda g: jnp.transpose(g, (1, 0, 2, 3)).reshape(g.shape[1], -1, g.shape[3])
    w = dict(p)
    w["gdn_w_in"] = _pad_lanes(cols(got["gdn_w_in"]), GDN_IN_PAD)
    w["hgrn_w_in"] = cols(got["hgrn_w_in"])
    w["mlp_w_up"] = cols(got["mlp_w_up"])
    w["gdn_conv"] = cols(got["gdn_conv"])
    w["gdn_w_out"] = rows(got["gdn_w_out"])
    w["hgrn_w_out"] = rows(got["hgrn_w_out"])
    w["mlp_w_down"] = rows(got["mlp_w_down"])
    w["hgrn_gnorm"] = jnp.transpose(got["hgrn_gnorm"], (1, 0, 2)).reshape(got["hgrn_gnorm"].shape[1], -1)

    loss_row, grad_x, big, small = _local_grads(x, loss_target, w)

    order = [(k, l) for k in BIG for l in range(len(big[k]))]
    gs = [big[k][l] for k, l in order]
    theirs = exchange_sibling_half(gs, "reduce_cores")
    mine = [lax.dynamic_slice_in_dim(g, ci * (g.shape[1] // 2), g.shape[1] // 2, axis=1) for g in gs]
    ps = [add_pair(a, b, f"add_cores{n}") for n, (a, b) in enumerate(zip(mine, theirs))]
    slots = exchange_chips(ps, "reduce_chips")
    halves = [add_slots(s, f"add_chips{n}") for n, s in enumerate(slots)]
    groups = [[halves[n] for n, (k, l) in enumerate(order) if k == name] for name in BIG]
    grads = dict(zip(BIG, share_sibling(groups, "share_cores")))

    total = add_slots(gather_all(_pack_small(small, loss_row), "gather_small"), "add_small")
    loss = total[_SMALL_LAYOUT["loss"][0], 0]
    for name in WEIGHTS:
        if name in BIG:
            continue
        if name == "gdn_conv":
            full = _unpack_small(total, name, (2, CONV_K, 3 * d))
            grads[name] = lax.dynamic_slice_in_dim(full, chip * (3 * d // N_CHIPS), 3 * d // N_CHIPS, axis=2)
        elif name == "hgrn_gnorm":
            full = _unpack_small(total, name, (2, d))
            grads[name] = lax.dynamic_slice_in_dim(full, chip * (d // N_CHIPS), d // N_CHIPS, axis=1)
        else:
            grads[name] = _unpack_small(total, name, p[name].shape)

    delta, new_m, new_v = {}, {}, {}
    for name in WEIGHTS:
        delta[name], new_m[name], new_v[name] = adamw(p[name], grads[name], m[name], v[name], f"adamw_{name}")
    return (loss, grad_x, *[grads[n] for n in WEIGHTS], *[delta[n] for n in WEIGHTS],
            *[new_m[n] for n in WEIGHTS], *[new_v[n] for n in WEIGHTS])


def add_core_halves(g, theirs, core, name):
    s, r, c = g.shape
    r2 = r // 2
    tr = min(r2, 256)
    nb = r2 // tr

    def body(core_ref, g_ref, t_ref, o_ref):
        o_ref[...] = g_ref[...] + t_ref[...]

    grid_spec = pltpu.PrefetchScalarGridSpec(
        num_scalar_prefetch=1, grid=(s, nb),
        in_specs=[pl.BlockSpec((None, tr, c), lambda t, i, cr: (t, cr[0] * nb + i, 0)),
                  pl.BlockSpec((None, tr, c), lambda t, i, cr: (t, i, 0))],
        out_specs=pl.BlockSpec((None, tr, c), lambda t, i, cr: (t, i, 0)))
    return pl.pallas_call(body, name=name, grid_spec=grid_spec, out_shape=jax.ShapeDtypeStruct((s, r2, c), F32),
                          compiler_params=_cparams(("parallel", "parallel")))(core, g, theirs)


def add_chip_slots(slots, name):
    n_l = len(slots)
    s, r2, c = slots[0].shape
    tr = min(r2, 256)
    nb = r2 // tr

    def body(*refs):
        ins, o_ref = refs[:n_l], refs[n_l]
        for k in range(n_l):
            @pl.when(pl.program_id(0) == k)
            def _(k=k):
                acc = ins[k][0]
                for t in range(1, s):
                    acc = acc + ins[k][t]
                o_ref[...] = acc

    in_specs = [pl.BlockSpec((s, tr, c), lambda l, i, k=k: (0, jnp.where(l == k, i, 0), 0)) for k in range(n_l)]
    return pl.pallas_call(
        body, name=name, grid=(n_l, nb), in_specs=in_specs, out_specs=pl.BlockSpec((None, tr, c), lambda l, i: (l, i, 0)),
        out_shape=jax.ShapeDtypeStruct((n_l, r2, c), F32), compiler_params=_cparams(("arbitrary", "arbitrary")),
    )(*slots)


def adamw_halves(w, m, v, mine, theirs, name):
    n_l, r, c = w.shape
    r2 = r // 2
    tr = min(r2, 256)
    nb = r2 // tr
    c1 = 1.0 / (1.0 - ADAM_B1 ** ADAM_STEP)
    c2 = 1.0 / (1.0 - ADAM_B2 ** ADAM_STEP)

    def body(w_ref, m_ref, v_ref, mine_ref, theirs_ref, g_ref, d_ref, nm_ref, nv_ref):
        my_half = (pl.program_id(1) // nb) == lax.axis_index("c")
        gg = jnp.where(my_half, mine_ref[...], theirs_ref[...])
        nm = ADAM_B1 * m_ref[...] + (1.0 - ADAM_B1) * gg
        nv = ADAM_B2 * v_ref[...] + (1.0 - ADAM_B2) * (gg * gg)
        g_ref[...] = gg
        d_ref[...] = -ADAM_LR * ((nm * c1) / (jnp.sqrt(nv * c2) + ADAM_EPS) + ADAM_WD * w_ref[...])
        nm_ref[...] = nm
        nv_ref[...] = nv

    full = pl.BlockSpec((None, tr, c), lambda l, i: (l, i, 0))
    half = pl.BlockSpec((None, tr, c), lambda l, i: (l, i % nb, 0))
    sds = jax.ShapeDtypeStruct((n_l, r, c), F32)
    return pl.pallas_call(
        body, name=name, grid=(n_l, r // tr), in_specs=[full, full, full, half, half], out_specs=[full] * 4,
        out_shape=(sds,) * 4, compiler_params=_cparams(("parallel", "parallel")),
    )(w, m, v, mine, theirs)


def _layer_weight(kind, i):
    if kind == "up":
        return "mlp_w_up", i
    if kind == "down":
        return "mlp_w_down", i
    return ("gdn_w_" if i % 2 == 0 else "hgrn_w_") + kind, i // 2


def kernel(x, gdn_w_in, gdn_conv, gdn_a_log, gdn_dt_bias, gdn_onorm, gdn_w_out, hgrn_w_in, hgrn_lb_logits, hgrn_gnorm, hgrn_w_out, norm_mix, norm_mlp, mlp_w_up, mlp_w_down, norm_final, loss_target, m_gdn_w_in, m_gdn_conv, m_gdn_a_log, m_gdn_dt_bias, m_gdn_onorm, m_gdn_w_out, m_hgrn_w_in, m_hgrn_lb_logits, m_hgrn_gnorm, m_hgrn_w_out, m_norm_mix, m_norm_mlp, m_mlp_w_up, m_mlp_w_down, m_norm_final, v_gdn_w_in, v_gdn_conv, v_gdn_a_log, v_gdn_dt_bias, v_gdn_onorm, v_gdn_w_out, v_hgrn_w_in, v_hgrn_lb_logits, v_hgrn_gnorm, v_hgrn_w_out, v_norm_mix, v_norm_mlp, v_mlp_w_up, v_mlp_w_down, v_norm_final):
    p = dict(gdn_w_in=gdn_w_in, gdn_conv=gdn_conv, gdn_a_log=gdn_a_log, gdn_dt_bias=gdn_dt_bias, gdn_onorm=gdn_onorm,
             gdn_w_out=gdn_w_out, hgrn_w_in=hgrn_w_in, hgrn_lb_logits=hgrn_lb_logits, hgrn_gnorm=hgrn_gnorm,
             hgrn_w_out=hgrn_w_out, norm_mix=norm_mix, norm_mlp=norm_mlp, mlp_w_up=mlp_w_up, mlp_w_down=mlp_w_down,
             norm_final=norm_final)
    m = dict(gdn_w_in=m_gdn_w_in, gdn_conv=m_gdn_conv, gdn_a_log=m_gdn_a_log, gdn_dt_bias=m_gdn_dt_bias,
             gdn_onorm=m_gdn_onorm, gdn_w_out=m_gdn_w_out, hgrn_w_in=m_hgrn_w_in, hgrn_lb_logits=m_hgrn_lb_logits,
             hgrn_gnorm=m_hgrn_gnorm, hgrn_w_out=m_hgrn_w_out, norm_mix=m_norm_mix, norm_mlp=m_norm_mlp,
             mlp_w_up=m_mlp_w_up, mlp_w_down=m_mlp_w_down, norm_final=m_norm_final)
    v = dict(gdn_w_in=v_gdn_w_in, gdn_conv=v_gdn_conv, gdn_a_log=v_gdn_a_log, gdn_dt_bias=v_gdn_dt_bias,
             gdn_onorm=v_gdn_onorm, gdn_w_out=v_gdn_w_out, hgrn_w_in=v_hgrn_w_in, hgrn_lb_logits=v_hgrn_lb_logits,
             hgrn_gnorm=v_hgrn_gnorm, hgrn_w_out=v_hgrn_w_out, norm_mix=v_norm_mix, norm_mlp=v_norm_mlp,
             mlp_w_up=v_mlp_w_up, mlp_w_down=v_mlp_w_down, norm_final=v_norm_final)
    xi, yi, ci = _mesh_pos()
    chip = 2 * xi + yi
    core = jnp.reshape(ci, (1,)).astype(jnp.int32)
    d = D_MODEL
    bsz, t_len, _ = x.shape
    n_tok = bsz * t_len

    def shard(kind, i):
        name, idx = _layer_weight(kind, i)
        return p[name][idx].astype(BF16)

    def w_in_of(i, slots):
        if i % 2 == 0:
            return _pad_lanes(jnp.transpose(slots, (1, 0, 2)).reshape(d, GDN_IN), GDN_IN_PAD)
        return slots

    (first,) = run_carries([gather_carry([shard("in", 0), p["gdn_conv"], p["hgrn_gnorm"]])], "gather_first")
    conv = jnp.transpose(first[1], (1, 2, 0, 3)).reshape(DEPTH // 2, CONV_K, 3 * d)
    gnorm = jnp.transpose(first[2], (1, 0, 2)).reshape(DEPTH // 2, d)
    lbl = p["hgrn_lb_logits"]
    h = x.reshape(n_tok, d)
    next_in, next_out = first[0], None
    saved = []
    for i in range(DEPTH):
        j = i // 2
        w_in = w_in_of(i, next_in)
        nmix = p["norm_mix"][i][None, :]
        (y, proj), got = norm_matmul(h, nmix, w_in, False, f"in_proj{i}",
                                     [gather_carry([shard("out", 0)])] if i == 0 else [])
        if i == 0:
            next_out = got[0][0]
        ride = [gather_carry([shard("up", i), shard("down", i)])]
        if i % 2 == 0:
            al = _pad_lanes(p["gdn_a_log"][j][None, :], 128)
            dtb = _pad_lanes(p["gdn_dt_bias"][j][None, :], 128)
            on = p["gdn_onorm"][j][None, :]
            (og, ssave), got = gdn_forward(proj, conv[j], al, dtb, on, bsz, f"gdn_fwd{i}", ride)
            mix = (proj, ssave, al, dtb, on)
        else:
            (o, ssave), got = hgrn_forward(proj, lbl, i, bsz, ride)
            gn = gnorm[j][None, :]
            og = hgrn_post_forward(o, proj, gn, f"hgrn_post{i}")
            mix = (proj, ssave, o, gn)
        w_up, w_down = got[0][0], got[0][1].reshape(MLP_HIDDEN, d)
        w_out = next_out.reshape(d, d)
        h1, _ = matmul(og, w_out, "nn", f"out_proj{i}", extra=h, epilogue="add")
        nmlp = p["norm_mlp"][i][None, :]
        (z, a, r), got = norm_matmul(h1, nmlp, w_up, True, f"mlp_up{i}",
                                     [gather_carry([shard("in", i + 1)])] if i + 1 < DEPTH else [])
        if i + 1 < DEPTH:
            next_in = got[0][0]
        h2, got = matmul(a, w_down, "nn", f"mlp_down{i}", extra=h1, epilogue="add",
                         carries=[gather_carry([shard("out", i + 1)])] if i + 1 < DEPTH else [])
        if i + 1 < DEPTH:
            next_out = got[0][0]
        saved.append((h, nmix, y, mix, og, w_in, w_out, h1, nmlp, z, a, r, w_up, w_down))
        h = h2
    loss_row, dh, dhb, d_nf = loss_head(h, p["norm_final"][None, :], loss_target.reshape(n_tok, d))

    G = {k: [None] * DEPTH for k in ("in", "out", "up", "down")}
    P = {k: [None] * DEPTH for k in ("in", "out", "up", "down")}
    slots = {k: [None] * DEPTH for k in ("in", "out", "up", "down")}
    d_nmix, d_nmlp = [None] * DEPTH, [None] * DEPTH
    d_conv, d_alog, d_dtb, d_onorm, d_gnorm = [None] * 2, [None] * 2, [None] * 2, [None] * 2, [None] * 2
    d_lbl = jnp.zeros((DEPTH, d), F32)
    for i in reversed(range(DEPTH)):
        j = i // 2
        h_in, nmix, y, mix, og, w_in, w_out, h1, nmlp, z, a, r, w_up, w_down = saved[i]
        ride = [sibling_half_carry([G["in"][i + 1]])] if i + 1 < DEPTH else []
        du, got = matmul(dhb, w_down, "nt", f"d_mlp_act{i}", out_dtype=BF16, extra=r, epilogue="mul2", carries=ride)
        if i + 1 < DEPTH:
            P["in"][i + 1] = add_core_halves(G["in"][i + 1], got[0][0], core, f"add_cores_in{i + 1}")
        G["down"][i] = matmul(a, dhb, "tn", f"dw_down{i}")[0].reshape(N_CHIPS, -1, d)
        dz, _ = matmul(du, w_up, "nt", f"d_mlp_in{i}")
        G["up"][i], _ = matmul(z, du, "tn", f"dw_up{i}", shards=N_CHIPS)
        (dh1, dh1b, d_nmlp[i]), got = norm_backward(h1, nmlp, dz, dh, f"d_norm_mlp{i}",
                                                    [sibling_half_carry([G["up"][i], G["down"][i]])])
        P["up"][i] = add_core_halves(G["up"][i], got[0][0], core, f"add_cores_up{i}")
        P["down"][i] = add_core_halves(G["down"][i], got[0][1], core, f"add_cores_down{i}")
        dog, _ = matmul(dh1b, w_out, "nt", f"d_mix_out{i}")
        G["out"][i] = matmul(og, dh1b, "tn", f"dw_out{i}")[0].reshape(N_CHIPS, -1, d)
        to_chips = [("up", i), ("down", i)] + ([("in", i + 1), ("out", i + 1)] if i + 1 < DEPTH else [])
        ride = [sibling_half_carry([G["out"][i]]), chips_carry([P[k][l] for k, l in to_chips])]
        if i % 2 == 0:
            proj, ssave, al, dtb, on = mix
            (dproj, dcw, dal, ddt, don), got = gdn_backward(proj, conv[j], al, dtb, on, ssave, dog, bsz, f"gdn_bwd{i}", ride)
            d_conv[j] = jnp.transpose(dcw, (2, 1, 0, 3)).reshape(CONV_K, 3 * d)
            d_alog[j], d_dtb[j], d_onorm[j] = dal[0, :HEADS], ddt[0, :HEADS], don[0]
        else:
            proj, ssave, o, gn = mix
            do_raw, dgate, dgn = hgrn_post_backward(o, proj, gn, dog, f"d_hgrn_post{i}")
            (dproj, dlb), got = hgrn_backward(proj, lbl, ssave, do_raw, dgate, i, bsz, ride)
            d_gnorm[j] = dgn[0]
            d_lbl = d_lbl + jnp.transpose(dlb, (1, 0, 2)).reshape(DEPTH, d)
        P["out"][i] = add_core_halves(G["out"][i], got[0][0], core, f"add_cores_out{i}")
        for (k, l), s in zip(to_chips, got[1]):
            slots[k][l] = s
        if i % 2 == 0:
            dw_in = matmul(y, dproj, "tn", f"dw_in{i}")[0][:, :GDN_IN]
            G["in"][i] = jnp.transpose(dw_in.reshape(d, N_CHIPS, GDN_IN // N_CHIPS), (1, 0, 2))
        else:
            G["in"][i], _ = matmul(y, dproj, "tn", f"dw_in{i}", shards=N_CHIPS)
        ride = [sibling_half_carry([G["in"][0]]), chips_carry([P["out"][0]])] if i == 0 else []
        dy, got = matmul(dproj, w_in, "nt", f"d_in_proj{i}", carries=ride)
        ride = []
        if i == 0:
            P["in"][0] = add_core_halves(G["in"][0], got[0][0], core, "add_cores_in0")
            slots["out"][0] = got[1][0]
            ride = [chips_carry([P["in"][0]])]
        (dh, dhb, d_nmix[i]), got = norm_backward(h_in, nmix, dy, dh1, f"d_norm_mix{i}", ride)
        if i == 0:
            slots["in"][0] = got[0][0]
    grad_x = dh.reshape(x.shape)

    by_weight = {}
    for kind in ("in", "out", "up", "down"):
        for i in range(DEPTH):
            by_weight.setdefault(_layer_weight(kind, i)[0], []).append(slots[kind][i])
    mine = {name: add_chip_slots(by_weight[name], f"add_chips_{name}") for name in BIG}
    small = {
        "gdn_conv": jnp.stack(d_conv), "gdn_a_log": jnp.stack(d_alog), "gdn_dt_bias": jnp.stack(d_dtb),
        "gdn_onorm": jnp.stack(d_onorm), "hgrn_lb_logits": d_lbl, "hgrn_gnorm": jnp.stack(d_gnorm),
        "norm_mix": jnp.concatenate(d_nmix, axis=0), "norm_mlp": jnp.concatenate(d_nmlp, axis=0), "norm_final": d_nf[0],
    }
    theirs, (blocks,) = run_carries([sibling_copy_carry([mine[name] for name in BIG]),
                                     gather_all_carry(_pack_small(small, loss_row))], "share_cores")
    theirs = dict(zip(BIG, theirs))

    total = add_slots(blocks, "add_small")
    loss = total[_SMALL_LAYOUT["loss"][0], 0]
    grads, delta, new_m, new_v = {}, {}, {}, {}
    for name in WEIGHTS:
        if name in BIG:
            grads[name], delta[name], new_m[name], new_v[name] = adamw_halves(
                p[name], m[name], v[name], mine[name], theirs[name], f"adamw_{name}")
            continue
        if name == "gdn_conv":
            full = _unpack_small(total, name, (2, CONV_K, 3 * d))
            grads[name] = lax.dynamic_slice_in_dim(full, chip * (3 * d // N_CHIPS), 3 * d // N_CHIPS, axis=2)
        elif name == "hgrn_gnorm":
            full = _unpack_small(total, name, (2, d))
            grads[name] = lax.dynamic_slice_in_dim(full, chip * (d // N_CHIPS), d // N_CHIPS, axis=1)
        else:
            grads[name] = _unpack_small(total, name, p[name].shape)
        delta[name], new_m[name], new_v[name] = adamw(p[name], grads[name], m[name], v[name], f"adamw_{name}")
    return (loss, grad_x, *[grads[n] for n in WEIGHTS], *[delta[n] for n in WEIGHTS],
            *[new_m[n] for n in WEIGHTS], *[new_v[n] for n in WEIGHTS])
```

```python
import functools
import math

import jax
import jax.numpy as jnp
from jax import lax
from jax.experimental import pallas as pl
from jax.experimental.pallas import tpu as pltpu

F32 = jnp.float32
BF16 = jnp.bfloat16
HI = lax.Precision.HIGHEST

D_MODEL = 1024
HEADS = 8
HEAD_DIM = 128
CHUNK = 64
SUB = 16
CONV_K = 4
HALO = 16
DEPTH = 4
EPS = 1e-6
MLP_HIDDEN = 4 * D_MODEL
GDN_MAIN = 4 * D_MODEL
GDN_IN = GDN_MAIN + 2 * HEADS
GDN_IN_PAD = GDN_MAIN + 128
NEG = -1e30

ADAM_LR = 0.001
ADAM_B1 = 0.9
ADAM_B2 = 0.999
ADAM_EPS = 1e-08
ADAM_WD = 0.01
ADAM_STEP = 10

VMEM_LIMIT = 48 * 1024 * 1024

MESH = pl.DeviceIdType.MESH


def _cparams(sem=None, **kw):
    if sem is not None:
        kw["dimension_semantics"] = sem
    return pltpu.CompilerParams(vmem_limit_bytes=VMEM_LIMIT, **kw)


def _iota(shape, dim):
    return lax.broadcasted_iota(jnp.int32, shape, dim)


_DIMS = {"nn": (((1,), (0,)), ((), ())), "nt": (((1,), (1,)), ((), ())), "tn": (((0,), (0,)), ((), ()))}


def _dot(a, b, mode):
    return lax.dot_general(a.astype(BF16), b.astype(BF16), _DIMS[mode], preferred_element_type=F32)


@functools.partial(jax.custom_vjp, nondiff_argnums=(2,))
def _mmx(a, b, mode):
    return _dot(a, b, mode)


def _mmx_fwd(a, b, mode):
    return _dot(a, b, mode), (a, b)


def _mmx_bwd(mode, res, g):
    a, b = res
    if mode == "nn":
        return _dot(g, b, "nt"), _dot(a, g, "tn")
    if mode == "nt":
        return _dot(g, b, "nn"), _dot(g, a, "tn")
    return _dot(b, g, "nt"), _dot(a, g, "nn")


_mmx.defvjp(_mmx_fwd, _mmx_bwd)


def _mm_f32(a, b):
    return lax.dot_general(a, b, _DIMS["nn"], precision=HI, preferred_element_type=F32)


def _mm(a, b):
    return _mmx(a, b, "nn")


def _mm_nt(a, b):
    return _mmx(a, b, "nt")


def _mm_tn(a, b):
    return _mmx(a, b, "tn")


@functools.partial(jax.custom_vjp, nondiff_argnums=(1,))
def _roll_rows(x, d):
    return pltpu.roll(x, d, 0)


def _roll_rows_fwd(x, d):
    return pltpu.roll(x, d, 0), None


def _roll_rows_bwd(d, _, g):
    return (pltpu.roll(g, g.shape[0] - d, 0),)


_roll_rows.defvjp(_roll_rows_fwd, _roll_rows_bwd)


def _sigmoid(x):
    return 1.0 / (1.0 + jnp.exp(-x))


def _silu(x):
    return x * _sigmoid(x)


def _softplus(x):
    return jnp.maximum(x, 0.0) + jnp.log(1.0 + jnp.exp(-jnp.abs(x)))


def _log_sigmoid(x):
    return jnp.minimum(x, 0.0) - jnp.log(1.0 + jnp.exp(-jnp.abs(x)))


def _logaddexp(a, b):
    return jnp.maximum(a, b) + jnp.log(1.0 + jnp.exp(-jnp.abs(a - b)))


def _row_to_col(row):
    n = row.shape[1]
    eye = _iota((n, n), 0) == _iota((n, n), 1)
    return jnp.sum(jnp.where(eye, jnp.broadcast_to(row, (n, n)), 0.0), axis=1, keepdims=True)


def _col_to_row(col):
    n = col.shape[0]
    eye = _iota((n, n), 0) == _iota((n, n), 1)
    return jnp.sum(jnp.where(eye, jnp.broadcast_to(col, (n, n)), 0.0), axis=0, keepdims=True)


def _pick_row(x, r):
    return jnp.sum(jnp.where(_iota(x.shape, 0) == r, x, 0.0), axis=0, keepdims=True)


def _pick_lane(x, l):
    return jnp.sum(jnp.where(_iota(x.shape, 1) == l, x, 0.0), axis=1, keepdims=True)


def _each(f, *lists):
    return [f(*t) for t in zip(*lists)]


def _unit_lower_inverse(Ls):
    n = Ls[0].shape[0]
    r, c = _iota((n, n), 0), _iota((n, n), 1)
    eye = jnp.where(r == c, 1.0, 0.0).astype(F32)
    Ld = _each(lambda L: jnp.where((r // SUB) == (c // SUB), L, 0.0), Ls)
    Lo = _each(lambda L, d: L - d, Ls, Ld)
    P = _each(lambda d: eye - d, Ld)
    Lp = Ld
    for _ in range(int(math.log2(SUB)) - 1):
        Lp = _each(lambda x: _mm(x, x), Lp)
        P = _each(lambda p, x: p + _mm(p, x), P, Lp)
    N = _each(_mm, P, Lo)
    N2 = _each(lambda x: _mm(x, x), N)
    X = _each(lambda x, x2: (eye - x) + _mm(eye - x, x2), N, N2)
    return _each(_mm, X, P)


def _shift_rows(x, halo, d):
    if d == 0:
        return x
    xr = _roll_rows(x, d)
    hr = _roll_rows(halo, d)
    hfull = jnp.concatenate([hr, jnp.zeros((x.shape[0] - HALO, x.shape[1]), F32)], axis=0)
    return jnp.where(_iota(x.shape, 0) >= d, xr, hfull)


def _causal_conv_chunk(x, halo, w):
    y = None
    for kk in range(CONV_K):
        t = _shift_rows(x, halo, CONV_K - 1 - kk) * _pick_row(w, kk)
        y = t if y is None else y + t
    return y


def _gdn_chunk(xq, xk, xv, hq, hk, hv, gate, ab, S, cwq, cwk, cwv, alog, dtb, onorm, *, heads):
    C = xq[0].shape[0]
    q = _each(lambda x, h, w: _silu(_causal_conv_chunk(x, h, w)), xq, hq, cwq)
    k = _each(lambda x, h, w: _silu(_causal_conv_chunk(x, h, w)), xk, hk, cwk)
    v = _each(lambda x, h, w: _silu(_causal_conv_chunk(x, h, w)), xv, hv, cwv)
    q = _each(lambda t: t * lax.rsqrt(jnp.sum(t * t, axis=1, keepdims=True) + EPS) * (HEAD_DIM ** -0.5), q)
    k = _each(lambda t: t * lax.rsqrt(jnp.sum(t * t, axis=1, keepdims=True) + EPS), k)
    beta_all = _sigmoid(ab)
    g_all = -jnp.exp(alog) * _softplus(ab + dtb)
    beta = [_pick_lane(beta_all, HEADS + h) for h in heads]
    g = [_pick_lane(g_all, h) for h in heads]
    r, c = _iota((C, C), 0), _iota((C, C), 1)
    gc = _each(lambda t: jnp.sum(jnp.where(c <= r, jnp.broadcast_to(_col_to_row(t), (C, C)), 0.0), axis=1,
                                 keepdims=True), g)
    gc_row = _each(lambda t: jnp.sum(jnp.where(r <= c, jnp.broadcast_to(t, (C, C)), 0.0), axis=0, keepdims=True), g)
    decay = _each(lambda a, b: jnp.exp(jnp.where(r >= c, a - b, NEG)), gc, gc_row)
    kb = _each(lambda a, b: a * b, k, beta)
    L = _each(lambda a, b, d: jnp.where(r > c, _mm_nt(a, b) * d, 0.0), kb, k, decay)
    A = _each(lambda a, b, d: jnp.where(r >= c, _mm_nt(a, b) * d, 0.0), q, k, decay)
    T = _unit_lower_inverse(L)
    egc = _each(jnp.exp, gc)
    u = _each(lambda t, a, b: _mm(t, a * b), T, v, beta)
    w = _each(lambda t, a, e: _mm(t, a * e), T, kb, egc)
    gl = _each(lambda t: _pick_row(t, C - 1), gc)
    v_new = _each(lambda a, b, s: a - _mm(b, s), u, w, S)
    o = _each(lambda a, e, s, m, vn: _mm(a * e, s) + _mm(m, vn), q, egc, S, A, v_new)
    S_next = _each(lambda s, l, a, t, vn: s * jnp.exp(l) + _mm_tn(a * jnp.exp(l - t), vn), S, gl, k, gc, v_new)
    o = _each(lambda t, gt: t * lax.rsqrt(jnp.mean(t * t, axis=1, keepdims=True) + EPS) * onorm * _silu(gt), o, gate)
    return o, S_next


def _hgrn_lower_bound(lbl, layer):
    e = jnp.exp(lbl - jnp.max(lbl, axis=0, keepdims=True))
    sm = e / jnp.sum(e, axis=0, keepdims=True)
    r = _iota(lbl.shape, 0)
    return jnp.sum(jnp.where((r >= 1) & (r <= layer), sm, 0.0), axis=0, keepdims=True)


_LEVELS = (1, 2, 4, 8, 16, 32)


def _prefix_matrix(n):
    r, c = _iota((n, n), 0), _iota((n, n), 1)
    parts = [jnp.where(c <= r, 1.0, 0.0)]
    for s in _LEVELS:
        parts.append(jnp.where(c < (r // (2 * s)) * (2 * s) + s, 1.0, 0.0))
    return jnp.concatenate(parts, axis=0).astype(F32)


def _prefix_sums_of(x):
    n = x.shape[0]
    y = lax.dot_general(_prefix_matrix(n), x, _DIMS["nn"], precision=HI, preferred_element_type=F32)
    return tuple(y[t * n:(t + 1) * n] for t in range(len(_LEVELS) + 1))


@jax.custom_vjp
def _prefix_sums(x):
    return _prefix_sums_of(x)


def _prefix_sums_fwd(x):
    return _prefix_sums_of(x), None


def _prefix_sums_bwd(_, gs):
    g = jnp.concatenate(gs, axis=0)
    return (lax.dot_general(_prefix_matrix(gs[0].shape[0]), g, _DIMS["tn"], precision=HI, preferred_element_type=F32),)


_prefix_sums.defvjp(_prefix_sums_fwd, _prefix_sums_bwd)


def _hgrn_chunk(qp, fp, v, S, lbl, *, layer):
    C = qp[0].shape[0]
    lb = _each(lambda l: _hgrn_lower_bound(l, layer), lbl)
    lf = _each(lambda l, f: _logaddexp(jnp.log(l), jnp.log(1.0 - l) + _log_sigmoid(f)), lb, fp)
    k = _each(lambda l, f: (1.0 - l) * _sigmoid(-f), lb, fp)
    q = _each(lambda x: _silu(x) * (HEAD_DIM ** -0.5), qp)
    r, c = _iota((C, C), 0), _iota((C, C), 1)
    row = _iota(qp[0].shape, 0)
    sums = _each(_prefix_sums, lf)
    gc = [t[0] for t in sums]
    a = _each(lambda x, y: jnp.where(r == c, _mm_nt(x, y), 0.0), q, k)
    for n, s in enumerate(_LEVELS):
        ref = [t[n + 1] for t in sums]
        upper = (row % (2 * s)) >= s
        same = (r // (2 * s)) == (c // (2 * s))
        q_s = _each(lambda x, g, m: x * jnp.exp(jnp.where(upper, g - m, NEG)), q, gc, ref)
        k_s = _each(lambda x, g, m: x * jnp.exp(jnp.where(upper, NEG, m - g)), k, gc, ref)
        a = _each(lambda t, x, y: t + jnp.where(same, _mm_nt(x, y), 0.0), a, q_s, k_s)
    o = _each(lambda t, x, g, vv, st: _mm(t, vv) + _mm(x * jnp.exp(g), st), a, q, gc, v, S)
    gl = _each(lambda g: _pick_row(g, C - 1), gc)
    S_next = _each(lambda st, l, x, g, vv: st * _row_to_col(jnp.exp(l)) + _mm_tn(x * jnp.exp(l - g), vv),
                   S, gl, k, gc, v)
    return o, S_next


N_CHIPS = 4
N_DEV = 8
_ANY = pl.BlockSpec(memory_space=pl.ANY)


def _mesh_pos():
    return lax.axis_index("x"), lax.axis_index("y"), lax.axis_index("c")


def _other_chips(x, y):
    ps = [(1 - x, y), (x, 1 - y), (1 - x, 1 - y)]
    return [(p, 2 * p[0] + p[1]) for p in ps]


def _remote(src, dst, send_sem, recv_sem, dev):
    return pltpu.make_async_remote_copy(src_ref=src, dst_ref=dst, send_sem=send_sem, recv_sem=recv_sem,
                                        device_id=dev, device_id_type=MESH)


class Carry:
    def __init__(self, ins, out_shapes, sems, start, finish):
        self.ins, self.out_shapes, self.sems, self.start, self.finish = list(ins), list(out_shapes), list(sems), start, finish


def _pcall(body, *, name, grid, in_specs, out_specs, out_shape, scratch_shapes=(), dims, args, carries=()):
    in_specs, out_specs, out_shape = list(in_specs), list(out_specs), list(out_shape)
    scratch_shapes, args = list(scratch_shapes), list(args)
    n_in, n_out, n_scr = len(in_specs), len(out_shape), len(scratch_shapes)
    carries = [c for c in carries if c is not None]
    if not carries:
        res = pl.pallas_call(body, name=name, grid=grid, in_specs=in_specs, out_specs=out_specs, out_shape=out_shape,
                             scratch_shapes=scratch_shapes, compiler_params=_cparams(dims))(*args)
        return list(res), []
    ci = [len(c.ins) for c in carries]
    co = [len(c.out_shapes) for c in carries]
    cs = [len(c.sems) for c in carries]

    def split(seq, sizes):
        out, k = [], 0
        for s in sizes:
            out.append(seq[k:k + s])
            k += s
        return out

    def carried(*refs):
        ins, cins, outs, couts, scr, sems = split(refs, [n_in, sum(ci), n_out, sum(co), n_scr, sum(cs)])
        cins, couts, sems = split(cins, ci), split(couts, co), split(sems, cs)
        ids = [pl.program_id(a) for a in range(len(grid))]
        first, last = ids[0] == 0, ids[0] == grid[0] - 1
        for a in range(1, len(grid)):
            first, last = first & (ids[a] == 0), last & (ids[a] == grid[a] - 1)

        @pl.when(first)
        def _():
            for c, i, o, s in zip(carries, cins, couts, sems):
                c.start(i, o, s)

        body(*ins, *outs, *scr)

        @pl.when(last)
        def _():
            for c, i, o, s in zip(carries, cins, couts, sems):
                c.finish(i, o, s)

    res = pl.pallas_call(
        carried, name=name, grid=grid,
        in_specs=in_specs + [_ANY] * sum(ci), out_specs=out_specs + [_ANY] * sum(co),
        out_shape=out_shape + [s for c in carries for s in c.out_shapes],
        scratch_shapes=scratch_shapes + [s for c in carries for s in c.sems],
        compiler_params=_cparams(("arbitrary",) * len(grid)),
    )(*args, *[a for c in carries for a in c.ins])
    return list(res[:n_out]), split(list(res[n_out:]), co)


def run_carries(carries, name):
    ci = [len(c.ins) for c in carries]
    co = [len(c.out_shapes) for c in carries]
    cs = [len(c.sems) for c in carries]

    def split(seq, sizes):
        out, k = [], 0
        for s in sizes:
            out.append(seq[k:k + s])
            k += s
        return out

    def body(*refs):
        cins, couts, sems = split(refs, [sum(ci), sum(co), sum(cs)])
        cins, couts, sems = split(cins, ci), split(couts, co), split(sems, cs)
        for c, i, o, s in zip(carries, cins, couts, sems):
            c.start(i, o, s)
        for c, i, o, s in zip(carries, cins, couts, sems):
            c.finish(i, o, s)

    res = pl.pallas_call(
        body, name=name, in_specs=[_ANY] * sum(ci), out_specs=[_ANY] * sum(co),
        out_shape=[s for c in carries for s in c.out_shapes], scratch_shapes=[s for c in carries for s in c.sems],
    )(*[a for c in carries for a in c.ins])
    return split(list(res), co)


def gather_carry(arrs):
    n = len(arrs)
    split = [a.ndim == 2 and a.shape[0] % 32 == 0 for a in arrs]

    def plan(ins, outs, sems):
        send_sems, recv_sems, pass_send, pass_recv, local_sems = sems
        x, y, c = _mesh_pos()
        me = 2 * x + y
        peers = _other_chips(x, y)
        locs = [pltpu.make_async_copy(ins[a], outs[a].at[me], local_sems.at[a]) for a in range(n)]
        sends, recvs, passes, pass_recvs = [], [], [], []
        for a in range(n):
            half = arrs[a].shape[0] // 2
            mine, other = pl.ds(c * half, half), pl.ds((1 - c) * half, half)
            for j, ((px, py), t) in enumerate(peers):
                sem = (send_sems.at[a, j], recv_sems.at[a, j])
                if split[a]:
                    sends.append(_remote(ins[a].at[mine], outs[a].at[me, mine], *sem, (px, py, c)))
                    recvs.append(_remote(ins[a].at[mine], outs[a].at[t, mine], *sem, (px, py, c)))
                    psem = (pass_send.at[a, j], pass_recv.at[a, j])
                    passes.append(_remote(outs[a].at[t, mine], outs[a].at[t, mine], *psem, (x, y, 1 - c)))
                    pass_recvs.append(_remote(outs[a].at[t, other], outs[a].at[t, other], *psem, (x, y, 1 - c)))
                else:
                    sends.append(_remote(ins[a], outs[a].at[me], *sem, (px, py, c)))
                    recvs.append(_remote(ins[a], outs[a].at[t], *sem, (px, py, c)))
                    passes.append(None)
                    pass_recvs.append(None)
        return locs, sends, recvs, passes, pass_recvs

    def start(ins, outs, sems):
        locs, sends, _, _, _ = plan(ins, outs, sems)
        for cp in locs + sends:
            cp.start()

    def finish(ins, outs, sems):
        locs, sends, recvs, passes, pass_recvs = plan(ins, outs, sems)
        for r, p in zip(recvs, passes):
            r.wait_recv()
            if p is not None:
                p.start()
        for p in pass_recvs:
            if p is not None:
                p.wait_recv()
        for r in sends + [p for p in passes if p is not None]:
            r.wait_send()
        for cp in locs:
            cp.wait()

    return Carry(arrs, [jax.ShapeDtypeStruct((N_CHIPS,) + a.shape, a.dtype) for a in arrs],
                 [pltpu.SemaphoreType.DMA((n, 3)), pltpu.SemaphoreType.DMA((n, 3)), pltpu.SemaphoreType.DMA((n, 3)),
                  pltpu.SemaphoreType.DMA((n, 3)), pltpu.SemaphoreType.DMA((n,))], start, finish)


def sibling_half_carry(gs):
    n = len(gs)

    def copies(ins, outs, sems):
        send_sems, recv_sems = sems
        x, y, c = _mesh_pos()
        out = []
        for a in range(n):
            half = gs[a].shape[1] // 2
            out.append(_remote(ins[a].at[:, pl.ds((1 - c) * half, half), :], outs[a], send_sems.at[a], recv_sems.at[a],
                               (x, y, 1 - c)))
        return out

    def start(ins, outs, sems):
        for r in copies(ins, outs, sems):
            r.start()

    def finish(ins, outs, sems):
        cps = copies(ins, outs, sems)
        for r in cps:
            r.wait_recv()
        for r in cps:
            r.wait_send()

    return Carry(gs, [jax.ShapeDtypeStruct((g.shape[0], g.shape[1] // 2, g.shape[2]), g.dtype) for g in gs],
                 [pltpu.SemaphoreType.DMA((n,)), pltpu.SemaphoreType.DMA((n,))], start, finish)


def chips_carry(ps):
    n = len(ps)

    def copies(ins, outs, sems):
        send_sems, recv_sems, local_sems = sems
        x, y, c = _mesh_pos()
        me = 2 * x + y
        peers = _other_chips(x, y)
        locs = [pltpu.make_async_copy(ins[a].at[me], outs[a].at[me], local_sems.at[a]) for a in range(n)]
        sends = [_remote(ins[a].at[t], outs[a].at[me], send_sems.at[a, j], recv_sems.at[a, j], (px, py, c))
                 for a in range(n) for j, ((px, py), t) in enumerate(peers)]
        recvs = [_remote(ins[a].at[t], outs[a].at[t], send_sems.at[a, j], recv_sems.at[a, j], (px, py, c))
                 for a in range(n) for j, ((px, py), t) in enumerate(peers)]
        return locs, sends, recvs

    def start(ins, outs, sems):
        locs, sends, _ = copies(ins, outs, sems)
        for cp in locs + sends:
            cp.start()

    def finish(ins, outs, sems):
        locs, sends, recvs = copies(ins, outs, sems)
        for r in recvs:
            r.wait_recv()
        for r in sends:
            r.wait_send()
        for cp in locs:
            cp.wait()

    return Carry(ps, [jax.ShapeDtypeStruct(p.shape, p.dtype) for p in ps],
                 [pltpu.SemaphoreType.DMA((n, 3)), pltpu.SemaphoreType.DMA((n, 3)), pltpu.SemaphoreType.DMA((n,))],
                 start, finish)


def sibling_copy_carry(arrs):
    n = len(arrs)

    def copies(ins, outs, sems):
        send_sems, recv_sems = sems
        x, y, c = _mesh_pos()
        return [_remote(ins[a], outs[a], send_sems.at[a], recv_sems.at[a], (x, y, 1 - c)) for a in range(n)]

    def start(ins, outs, sems):
        for r in copies(ins, outs, sems):
            r.start()

    def finish(ins, outs, sems):
        cps = copies(ins, outs, sems)
        for r in cps:
            r.wait_recv()
        for r in cps:
            r.wait_send()

    return Carry(arrs, [jax.ShapeDtypeStruct(a.shape, a.dtype) for a in arrs],
                 [pltpu.SemaphoreType.DMA((n,)), pltpu.SemaphoreType.DMA((n,))], start, finish)


def gather_all_carry(small):
    flips = [(fx, fy, fc) for fx in (0, 1) for fy in (0, 1) for fc in (0, 1)][1:]

    def copies(ins, outs, sems):
        send_sems, recv_sems, local_sem = sems
        in_ref, out_ref = ins[0], outs[0]
        x, y, c = _mesh_pos()
        me = 4 * x + 2 * y + c
        peers = [((1 - x) if fx else x, (1 - y) if fy else y, (1 - c) if fc else c) for fx, fy, fc in flips]
        loc = pltpu.make_async_copy(in_ref, out_ref.at[me], local_sem.at[0])
        sends = [_remote(in_ref, out_ref.at[me], send_sems.at[k], recv_sems.at[k], p) for k, p in enumerate(peers)]
        recvs = [_remote(in_ref, out_ref.at[4 * p[0] + 2 * p[1] + p[2]], send_sems.at[k], recv_sems.at[k], p)
                 for k, p in enumerate(peers)]
        return loc, sends, recvs

    def start(ins, outs, sems):
        loc, sends, _ = copies(ins, outs, sems)
        loc.start()
        for r in sends:
            r.start()

    def finish(ins, outs, sems):
        loc, sends, recvs = copies(ins, outs, sems)
        for r in recvs:
            r.wait_recv()
        for r in sends:
            r.wait_send()
        loc.wait()

    return Carry([small], [jax.ShapeDtypeStruct((N_DEV,) + small.shape, small.dtype)],
                 [pltpu.SemaphoreType.DMA((N_DEV - 1,)), pltpu.SemaphoreType.DMA((N_DEV - 1,)),
                  pltpu.SemaphoreType.DMA((1,))], start, finish)


HB = 8
HBW = HB * HEAD_DIM


def _chunk_specs(nc, col0, rev=False):
    return pl.BlockSpec((CHUNK, HBW), lambda b, n, g: (b * nc + ((nc - 1 - n) if rev else n), col0 // HB + g))


def _chunk_specs_rev(nc, col0):
    return _chunk_specs(nc, col0, True)


def _halo_spec(nc, col0, rev):
    per = CHUNK // HALO

    def imap(b, n, g):
        nn = (nc - 1 - n) if rev else n
        return (jnp.maximum((b * nc + nn) * per - 1, 0), col0 // HB + g)
    return pl.BlockSpec((HALO, HBW), imap)


def _const_spec(shape):
    nd = len(shape)
    return pl.BlockSpec(shape, lambda b, n, g: (0,) * nd)


def _state_spec(nc, rev=False):
    return pl.BlockSpec((None, None, HB, HEAD_DIM, HEAD_DIM),
                        lambda b, n, g: (b, (nc - 1 - n) if rev else n, g, 0, 0))


def _lanes(hh):
    return slice(hh * HEAD_DIM, (hh + 1) * HEAD_DIM)


def _head(g, hh):
    return hh if HB == HEADS else g * HB + hh


def gdn_forward(proj, conv_w, alog, dtb, onorm, bsz, name, carries=()):
    n_tok = proj.shape[0]
    nc = n_tok // bsz // CHUNK

    def body(xq, xk, xv, hq, hk, hv, gate, ab, cwq, cwk, cwv, al, dt, on, o_ref, ssave_ref, s_ref):
        n, g = pl.program_id(1), pl.program_id(2)
        keep = jnp.where(n > 0, 1.0, 0.0).astype(F32)

        @pl.when(n == 0)
        def _():
            for hh in range(HB):
                s_ref[_head(g, hh)] = jnp.zeros((HEAD_DIM, HEAD_DIM), F32)

        heads = [_head(g, hh) for hh in range(HB)]
        per_head = lambda ref, scale=None: [ref[:, _lanes(hh)].astype(F32) if scale is None
                                            else ref[:, _lanes(hh)].astype(F32) * scale for hh in range(HB)]
        S = [s_ref[h] for h in heads]
        for hh in range(HB):
            ssave_ref[hh] = S[hh]
        outs, s_next = _gdn_chunk(per_head(xq), per_head(xk), per_head(xv), per_head(hq, keep), per_head(hk, keep),
                                  per_head(hv, keep), per_head(gate), ab[...].astype(F32), S, per_head(cwq), per_head(cwk),
                                  per_head(cwv), al[...], dt[...], on[...], heads=heads)
        for hh in range(HB):
            o_ref[:, _lanes(hh)] = outs[hh].astype(o_ref.dtype)
            s_ref[heads[hh]] = s_next[hh]

    hb = HEADS
    cw_spec = lambda col0: pl.BlockSpec((CONV_K, HBW), lambda b, n, g: (0, col0 // HB + g))
    in_specs = [_chunk_specs(nc, 0), _chunk_specs(nc, hb), _chunk_specs(nc, 2 * hb),
                _halo_spec(nc, 0, False), _halo_spec(nc, hb, False), _halo_spec(nc, 2 * hb, False),
                _chunk_specs(nc, 3 * hb),
                pl.BlockSpec((CHUNK, 128), lambda b, n, g: (b * nc + n, 4 * hb)),
                cw_spec(0), cw_spec(hb), cw_spec(2 * hb),
                _const_spec((1, 128)), _const_spec((1, 128)), _const_spec((1, 128))]
    out_specs = [_chunk_specs(nc, 0), _state_spec(nc)]
    return _pcall(
        body, name=name, grid=(bsz, nc, HEADS // HB), in_specs=in_specs, out_specs=out_specs,
        out_shape=(jax.ShapeDtypeStruct((n_tok, D_MODEL), BF16),
                   jax.ShapeDtypeStruct((bsz, nc, HEADS, HEAD_DIM, HEAD_DIM), F32)),
        scratch_shapes=[pltpu.VMEM((HEADS, HEAD_DIM, HEAD_DIM), F32)],
        dims=("arbitrary", "arbitrary", "arbitrary"),
        args=(proj, proj, proj, proj, proj, proj, proj, proj, conv_w, conv_w, conv_w, alog, dtb, onorm),
        carries=carries)


def gdn_backward(proj, conv_w, alog, dtb, onorm, s_saved, d_out, bsz, name, carries=()):
    n_tok = proj.shape[0]
    nc = n_tok // bsz // CHUNK
    assert HB == HEADS

    def body(xq, xk, xv, hq, hk, hv, gate, ab, cwq, cwk, cwv, al, dt, on, ssave, do,
             dp_ref, dcw_ref, dal_ref, ddt_ref, don_ref, ds_ref, dhalo_ref):
        b, n, g = pl.program_id(0), pl.program_id(1), pl.program_id(2)
        nr = nc - 1 - n

        @pl.when((b == 0) & (n == 0) & (g == 0))
        def _():
            dcw_ref[...] = jnp.zeros_like(dcw_ref)
            dal_ref[...] = jnp.zeros_like(dal_ref)
            ddt_ref[...] = jnp.zeros_like(ddt_ref)
            don_ref[...] = jnp.zeros_like(don_ref)

        @pl.when(n == 0)
        def _():
            for hh in range(HB):
                ds_ref[_head(g, hh)] = jnp.zeros((HEAD_DIM, HEAD_DIM), F32)
                dhalo_ref[_head(g, hh)] = jnp.zeros((3, HALO, HEAD_DIM), F32)

        keep = jnp.where(nr > 0, 1.0, 0.0).astype(F32)
        pad = jnp.zeros((CHUNK - HALO, HEAD_DIM), F32)
        heads = [_head(g, hh) for hh in range(HB)]
        per_head = lambda ref, scale=None: [ref[:, _lanes(hh)].astype(F32) if scale is None
                                            else ref[:, _lanes(hh)].astype(F32) * scale for hh in range(HB)]
        args = (per_head(xq), per_head(xk), per_head(xv), per_head(hq, keep), per_head(hk, keep), per_head(hv, keep),
                per_head(gate), ab[...].astype(F32), [ssave[hh] for hh in range(HB)], per_head(cwq), per_head(cwk), per_head(cwv),
                al[...], dt[...], on[...])
        _, vjp = jax.vjp(functools.partial(_gdn_chunk, heads=heads), *args)
        (gxq, gxk, gxv, ghq, ghk, ghv, ggate, gab_sum, gS, gcq, gck, gcv, gal_sum, gdt_sum, gon_sum) = vjp(
            ([do[:, _lanes(hh)] for hh in range(HB)], [ds_ref[h] for h in heads]))
        d = D_MODEL
        for hh in range(HB):
            head, c0 = heads[hh], hh * HEAD_DIM
            for part, gx in enumerate((gxq, gxk, gxv)):
                full = gx[hh] + jnp.concatenate([pad, dhalo_ref[head, part]], axis=0)
                dp_ref[:, part * d + c0:part * d + c0 + HEAD_DIM] = full.astype(dp_ref.dtype)
            dp_ref[:, 3 * d + c0:3 * d + c0 + HEAD_DIM] = ggate[hh].astype(dp_ref.dtype)
            dcw_ref[head, 0] += gcq[hh]
            dcw_ref[head, 1] += gck[hh]
            dcw_ref[head, 2] += gcv[hh]
            ds_ref[head] = gS[hh]
            dhalo_ref[head, 0] = ghq[hh] * keep
            dhalo_ref[head, 1] = ghk[hh] * keep
            dhalo_ref[head, 2] = ghv[hh] * keep

        dp_ref[:, GDN_MAIN:GDN_IN_PAD] = gab_sum.astype(dp_ref.dtype)
        dal_ref[...] += gal_sum
        ddt_ref[...] += gdt_sum
        don_ref[...] += gon_sum

    hb = HEADS
    cw_spec = lambda col0: pl.BlockSpec((CONV_K, HBW), lambda b, n, g: (0, col0 // HB + g))
    in_specs = [_chunk_specs_rev(nc, 0), _chunk_specs_rev(nc, hb), _chunk_specs_rev(nc, 2 * hb),
                _halo_spec(nc, 0, True), _halo_spec(nc, hb, True), _halo_spec(nc, 2 * hb, True),
                _chunk_specs_rev(nc, 3 * hb),
                pl.BlockSpec((CHUNK, 128), lambda b, n, g: (b * nc + (nc - 1 - n), 4 * hb)),
                cw_spec(0), cw_spec(hb), cw_spec(2 * hb),
                _const_spec((1, 128)), _const_spec((1, 128)), _const_spec((1, 128)),
                _state_spec(nc, True),
                _chunk_specs_rev(nc, 0)]
    out_specs = [
        pl.BlockSpec((CHUNK, GDN_IN_PAD), lambda b, n, g: (b * nc + (nc - 1 - n), 0)),
        _const_spec((HEADS, 3, CONV_K, HEAD_DIM)), _const_spec((1, 128)), _const_spec((1, 128)), _const_spec((1, 128))]
    row = jax.ShapeDtypeStruct((1, 128), F32)
    return _pcall(
        body, name=name, grid=(bsz, nc, HEADS // HB), in_specs=in_specs, out_specs=out_specs,
        out_shape=(jax.ShapeDtypeStruct((n_tok, GDN_IN_PAD), BF16),
                   jax.ShapeDtypeStruct((HEADS, 3, CONV_K, HEAD_DIM), F32), row, row, row),
        scratch_shapes=[pltpu.VMEM((HEADS, HEAD_DIM, HEAD_DIM), F32), pltpu.VMEM((HEADS, 3, HALO, HEAD_DIM), F32)],
        dims=("arbitrary", "arbitrary", "arbitrary"),
        args=(proj, proj, proj, proj, proj, proj, proj, proj, conv_w, conv_w, conv_w, alog, dtb, onorm, s_saved, d_out),
        carries=carries)


def hgrn_forward(proj, lbl, layer, bsz, carries=()):
    n_tok = proj.shape[0]
    nc = n_tok // bsz // CHUNK

    def body(qp, fp, vi, lb, o_ref, ssave_ref, s_ref):
        n, g = pl.program_id(1), pl.program_id(2)

        @pl.when(n == 0)
        def _():
            for hh in range(HB):
                s_ref[_head(g, hh)] = jnp.zeros((HEAD_DIM, HEAD_DIM), F32)

        heads = [_head(g, hh) for hh in range(HB)]
        per_head = lambda ref: [ref[:, _lanes(hh)].astype(F32) for hh in range(HB)]
        S = [s_ref[h] for h in heads]
        for hh in range(HB):
            ssave_ref[hh] = S[hh]
        outs, s_next = _hgrn_chunk(per_head(qp), per_head(fp), per_head(vi), S, per_head(lb), layer=layer)
        for hh in range(HB):
            o_ref[:, _lanes(hh)] = outs[hh]
            s_ref[heads[hh]] = s_next[hh]

    hb = HEADS
    in_specs = [_chunk_specs(nc, 0), _chunk_specs(nc, hb), _chunk_specs(nc, 2 * hb),
                pl.BlockSpec((DEPTH, HBW), lambda b, n, g: (0, g))]
    out_specs = [_chunk_specs(nc, 0), _state_spec(nc)]
    return _pcall(
        body, name=f"hgrn_fwd{layer}", grid=(bsz, nc, HEADS // HB), in_specs=in_specs, out_specs=out_specs,
        out_shape=(jax.ShapeDtypeStruct((n_tok, D_MODEL), F32),
                   jax.ShapeDtypeStruct((bsz, nc, HEADS, HEAD_DIM, HEAD_DIM), F32)),
        scratch_shapes=[pltpu.VMEM((HEADS, HEAD_DIM, HEAD_DIM), F32)],
        dims=("arbitrary", "arbitrary", "arbitrary"), args=(proj, proj, proj, lbl), carries=carries)


def hgrn_backward(proj, lbl, s_saved, d_o, d_gate, layer, bsz, carries=()):
    n_tok = proj.shape[0]
    nc = n_tok // bsz // CHUNK
    assert HB == HEADS

    def body(qp, fp, vi, lb, ssave, do, dgt, dp_ref, dlb_ref, ds_ref):
        b, n, g = pl.program_id(0), pl.program_id(1), pl.program_id(2)

        @pl.when((b == 0) & (n == 0) & (g == 0))
        def _():
            dlb_ref[...] = jnp.zeros_like(dlb_ref)

        @pl.when(n == 0)
        def _():
            for hh in range(HB):
                ds_ref[_head(g, hh)] = jnp.zeros((HEAD_DIM, HEAD_DIM), F32)

        heads = [_head(g, hh) for hh in range(HB)]
        per_head = lambda ref: [ref[:, _lanes(hh)].astype(F32) for hh in range(HB)]
        _, vjp = jax.vjp(functools.partial(_hgrn_chunk, layer=layer), per_head(qp), per_head(fp), per_head(vi),
                         [ssave[hh] for hh in range(HB)], per_head(lb))
        gq, gf, gv, gS, glb = vjp((per_head(do), [ds_ref[h] for h in heads]))
        for hh in range(HB):
            head, c0, d = heads[hh], hh * HEAD_DIM, D_MODEL
            dp_ref[:, c0:c0 + HEAD_DIM] = gq[hh].astype(dp_ref.dtype)
            dp_ref[:, d + c0:d + c0 + HEAD_DIM] = gf[hh].astype(dp_ref.dtype)
            dp_ref[:, 2 * d + c0:2 * d + c0 + HEAD_DIM] = gv[hh].astype(dp_ref.dtype)
            dlb_ref[head] += glb[hh]
            ds_ref[head] = gS[hh]
        dp_ref[:, 3 * D_MODEL:4 * D_MODEL] = dgt[...]

    hb = HEADS
    in_specs = [_chunk_specs_rev(nc, 0), _chunk_specs_rev(nc, hb), _chunk_specs_rev(nc, 2 * hb),
                pl.BlockSpec((DEPTH, HBW), lambda b, n, g: (0, g)),
                _state_spec(nc, True),
                _chunk_specs_rev(nc, 0), _chunk_specs_rev(nc, 0)]
    out_specs = [pl.BlockSpec((CHUNK, 4 * D_MODEL), lambda b, n, g: (b * nc + (nc - 1 - n), 0)),
                 _const_spec((HEADS, DEPTH, HEAD_DIM))]
    return _pcall(
        body, name=f"hgrn_bwd{layer}", grid=(bsz, nc, HEADS // HB), in_specs=in_specs, out_specs=out_specs,
        out_shape=(jax.ShapeDtypeStruct((n_tok, 4 * D_MODEL), BF16), jax.ShapeDtypeStruct((HEADS, DEPTH, HEAD_DIM), F32)),
        scratch_shapes=[pltpu.VMEM((HEADS, HEAD_DIM, HEAD_DIM), F32)],
        dims=("arbitrary", "arbitrary", "arbitrary"), args=(proj, proj, proj, lbl, s_saved, d_o, d_gate),
        carries=carries)


ROW_TILE = 512
MM_ROW_TILE = 1024
MM_VMEM_BUDGET = 36 * 1024 * 1024


def _tile(n):
    for cand in (1024, 512, 1408, 384, 256, 128):
        if n % cand == 0:
            return cand
    return n


def _rmsnorm(x, w):
    return x * lax.rsqrt(jnp.mean(x * x, axis=1, keepdims=True) + EPS) * w


def norm_matmul(h, nw, w, relu, name, carries=()):
    n_tok, d = h.shape
    slots = w.ndim == 3
    n_out = w.shape[0] * w.shape[2] if slots else w.shape[1]
    tm, tn = min(n_tok, MM_ROW_TILE), _tile(w.shape[2] if slots else n_out)
    if slots:
        per = w.shape[2] // tn
        w_spec = pl.BlockSpec((None, d, tn), lambda i, j: (j // per, 0, j % per))
    else:
        w_spec = pl.BlockSpec((d, tn), lambda i, j: (0, j))

    def body(h_ref, nw_ref, w_ref, y_ref, *outs):
        @pl.when(pl.program_id(1) == 0)
        def _():
            y_ref[...] = _rmsnorm(h_ref[...], nw_ref[...]).astype(BF16)

        acc = jnp.dot(y_ref[...], w_ref[...], preferred_element_type=F32)
        outs[0][...] = (jnp.maximum(acc, 0.0) if relu else acc).astype(BF16)

    o_spec = pl.BlockSpec((tm, tn), lambda i, j: (i, j))
    outs = (jax.ShapeDtypeStruct((n_tok, n_out), BF16),)
    return _pcall(
        body, name=name, grid=(n_tok // tm, n_out // tn),
        in_specs=[pl.BlockSpec((tm, d), lambda i, j: (i, 0)), pl.BlockSpec((1, d), lambda i, j: (0, 0)), w_spec],
        out_specs=[pl.BlockSpec((tm, d), lambda i, j: (i, 0))] + [o_spec] * len(outs),
        out_shape=(jax.ShapeDtypeStruct((n_tok, d), BF16),) + outs,
        dims=("parallel", "arbitrary"), args=(h, nw, w), carries=carries)


def _mm_tiles(m, n, k, extra_bytes):
    tn = _tile(n)
    for tm in (1024, 512, 256, 128):
        if m % tm == 0 and 2 * (2 * tm * k + 2 * k * tn + (4 + extra_bytes) * tm * tn) <= MM_VMEM_BUDGET:
            return tm, tn
    return min(m, 128), tn


def matmul(a, b, mode, name, out_dtype=F32, extra=None, epilogue=None, shards=1, carries=(), square_a=False):
    slots = b.shape[0] if (mode == "nt" and b.ndim == 3) else 0
    if mode == "nn":
        (m, k), n = a.shape, b.shape[1]
    elif mode == "nt":
        (m, k), n = a.shape, (b.shape[1] if slots else b.shape[0])
    else:
        (k, m), n = a.shape, b.shape[1]
    tm, tn = _mm_tiles(m, n // shards, k, 0 if extra is None else extra.dtype.itemsize)
    a_spec = pl.BlockSpec((k, tm), lambda i, j: (0, i)) if mode == "tn" else pl.BlockSpec((tm, k), lambda i, j: (i, 0))
    if slots:
        kb = b.shape[2]
        b_specs = [pl.BlockSpec((None, tn, kb), lambda i, j, s=s: (s, j, 0)) for s in range(slots)]
    elif mode == "nt":
        b_specs = [pl.BlockSpec((tn, k), lambda i, j: (j, 0))]
    else:
        b_specs = [pl.BlockSpec((k, tn), lambda i, j: (0, j))]
    nb = len(b_specs)
    dims = _DIMS[mode]

    def body(*refs):
        a_ref, b_refs = refs[0], refs[1:1 + nb]
        e_ref = refs[1 + nb] if extra is not None else None
        o_ref = refs[-1]
        if slots:
            acc = None
            for s in range(slots):
                part = lax.dot_general(a_ref[:, s * kb:(s + 1) * kb].astype(BF16), b_refs[s][...].astype(BF16), dims,
                                       preferred_element_type=F32)
                acc = part if acc is None else acc + part
        else:
            av = a_ref[...].astype(BF16)
            if square_a:
                av = av * av
            acc = lax.dot_general(av, b_refs[0][...].astype(BF16), dims, preferred_element_type=F32)
        if epilogue == "add":
            acc = e_ref[...] + acc
        elif epilogue == "mul2":
            acc = acc * (2.0 * e_ref[...].astype(F32))
        o_ref[...] = acc.astype(o_ref.dtype)

    in_specs = [a_spec] + b_specs
    args = [a] + [b] * nb
    if extra is not None:
        in_specs.append(pl.BlockSpec((tm, tn), lambda i, j: (i, j)))
        args.append(extra)
    if shards > 1:
        per = n // shards // tn
        out_spec = pl.BlockSpec((None, tm, tn), lambda i, j: (j // per, i, j % per))
        out_shape = jax.ShapeDtypeStruct((shards, m, n // shards), out_dtype)
    else:
        out_spec = pl.BlockSpec((tm, tn), lambda i, j: (i, j))
        out_shape = jax.ShapeDtypeStruct((m, n), out_dtype)
    (res,), cres = _pcall(
        body, name=name, grid=(m // tm, n // tn), in_specs=in_specs, out_specs=[out_spec], out_shape=[out_shape],
        dims=("parallel", "arbitrary"), args=args, carries=carries)
    return res, cres


def norm_backward(h, nw, dy, dres, name, carries=()):
    n_tok, d = h.shape
    tm = min(n_tok, ROW_TILE)

    def body(h_ref, nw_ref, dy_ref, dr_ref, dh_ref, dhb_ref, dnw_ref):
        @pl.when(pl.program_id(0) == 0)
        def _():
            dnw_ref[...] = jnp.zeros_like(dnw_ref)

        _, vjp = jax.vjp(_rmsnorm, h_ref[...], nw_ref[...])
        gh, gw = vjp(dy_ref[...])
        dh = dr_ref[...] + gh
        dh_ref[...] = dh
        dhb_ref[...] = dh.astype(BF16)
        dnw_ref[...] += gw

    row = pl.BlockSpec((tm, d), lambda i: (i, 0))
    vec = pl.BlockSpec((1, d), lambda i: (0, 0))
    return _pcall(
        body, name=name, grid=(n_tok // tm,), in_specs=[row, vec, row, row], out_specs=[row, row, vec],
        out_shape=(jax.ShapeDtypeStruct((n_tok, d), F32), jax.ShapeDtypeStruct((n_tok, d), BF16),
                   jax.ShapeDtypeStruct((1, d), F32)),
        dims=("arbitrary",), args=(h, nw, dy, dres), carries=carries)


def _hgrn_post(o, gate, gw):
    return _rmsnorm(o, gw) * _silu(gate)


def hgrn_post_forward(o, proj, gw, name):
    n_tok, d = o.shape
    tm = min(n_tok, ROW_TILE)

    def body(o_ref, g_ref, w_ref, y_ref):
        y_ref[...] = _hgrn_post(o_ref[...], g_ref[...].astype(F32), w_ref[...]).astype(BF16)

    row = pl.BlockSpec((tm, d), lambda i: (i, 0))
    return pl.pallas_call(
        body, name=name, grid=(n_tok // tm,),
        in_specs=[row, pl.BlockSpec((tm, d), lambda i: (i, 3)), pl.BlockSpec((1, d), lambda i: (0, 0))],
        out_specs=row, out_shape=jax.ShapeDtypeStruct((n_tok, d), BF16),
        compiler_params=_cparams(("parallel",)),
    )(o, proj, gw)


def hgrn_post_backward(o, proj, gw, dy, name):
    n_tok, d = o.shape
    tm = min(n_tok, ROW_TILE)

    def body(o_ref, g_ref, w_ref, dy_ref, do_ref, dg_ref, dw_ref):
        @pl.when(pl.program_id(0) == 0)
        def _():
            dw_ref[...] = jnp.zeros_like(dw_ref)

        _, vjp = jax.vjp(_hgrn_post, o_ref[...], g_ref[...].astype(F32), w_ref[...])
        go, gg, gw_ = vjp(dy_ref[...])
        do_ref[...] = go
        dg_ref[...] = gg.astype(BF16)
        dw_ref[...] += gw_

    row = pl.BlockSpec((tm, d), lambda i: (i, 0))
    vec = pl.BlockSpec((1, d), lambda i: (0, 0))
    return pl.pallas_call(
        body, name=name, grid=(n_tok // tm,),
        in_specs=[row, pl.BlockSpec((tm, d), lambda i: (i, 3)), vec, row], out_specs=[row, row, vec],
        out_shape=(jax.ShapeDtypeStruct((n_tok, d), F32), jax.ShapeDtypeStruct((n_tok, d), BF16),
                   jax.ShapeDtypeStruct((1, d), F32)),
        compiler_params=_cparams(("arbitrary",)),
    )(o, proj, gw, dy)


def loss_head(h, nw, target):
    n_tok, d = h.shape
    tm = min(n_tok, ROW_TILE)

    def body(h_ref, nw_ref, t_ref, loss_ref, dh_ref, dhb_ref, dnw_ref):
        @pl.when(pl.program_id(0) == 0)
        def _():
            dnw_ref[...] = jnp.zeros_like(dnw_ref)
            loss_ref[...] = jnp.zeros_like(loss_ref)

        out, vjp = jax.vjp(_rmsnorm, h_ref[...], nw_ref[...])
        err = out - t_ref[...]
        part = 0.5 * jnp.sum(jnp.sum(err * err, axis=1, keepdims=True), axis=0, keepdims=True) / d
        loss_ref[...] += jnp.broadcast_to(part, loss_ref.shape)
        gh, gw = vjp(err / d)
        dh_ref[...] = gh
        dhb_ref[...] = gh.astype(BF16)
        dnw_ref[...] += gw

    row = pl.BlockSpec((tm, d), lambda i: (i, 0))
    vec = pl.BlockSpec((1, d), lambda i: (0, 0))
    return pl.pallas_call(
        body, name="loss_head", grid=(n_tok // tm,), in_specs=[row, vec, row],
        out_specs=[pl.BlockSpec((1, 128), lambda i: (0, 0)), row, row, vec],
        out_shape=(jax.ShapeDtypeStruct((1, 128), F32), jax.ShapeDtypeStruct((n_tok, d), F32),
                   jax.ShapeDtypeStruct((n_tok, d), BF16), jax.ShapeDtypeStruct((1, d), F32)),
        compiler_params=_cparams(("arbitrary",)),
    )(h, nw, target)


def _rows2d(shape):
    if len(shape) == 1:
        return (1, shape[0])
    return (math.prod(shape[:-1]), shape[-1])


def adamw(w, g, m, v, name):
    shape = w.shape
    r, c = _rows2d(shape)
    tr = r if r <= 256 else 256
    c1 = 1.0 / (1.0 - ADAM_B1 ** ADAM_STEP)
    c2 = 1.0 / (1.0 - ADAM_B2 ** ADAM_STEP)

    def body(w_ref, g_ref, m_ref, v_ref, d_ref, nm_ref, nv_ref):
        gg = g_ref[...]
        nm = ADAM_B1 * m_ref[...] + (1.0 - ADAM_B1) * gg
        nv = ADAM_B2 * v_ref[...] + (1.0 - ADAM_B2) * (gg * gg)
        d_ref[...] = -ADAM_LR * ((nm * c1) / (jnp.sqrt(nv * c2) + ADAM_EPS) + ADAM_WD * w_ref[...])
        nm_ref[...] = nm
        nv_ref[...] = nv

    spec = pl.BlockSpec((tr, c), lambda i: (i, 0))
    sds = jax.ShapeDtypeStruct((r, c), F32)
    outs = pl.pallas_call(
        body, name=name, grid=(r // tr,), in_specs=[spec] * 4, out_specs=[spec] * 3, out_shape=(sds,) * 3,
        compiler_params=_cparams(("parallel",)),
    )(w.reshape(r, c), g.reshape(r, c), m.reshape(r, c), v.reshape(r, c))
    return tuple(o.reshape(shape) for o in outs)


def add_slots(parts, name):
    s, r, c = parts.shape
    tr = r if r <= 256 else 256

    def body(p_ref, o_ref):
        acc = p_ref[0]
        for t in range(1, s):
            acc = acc + p_ref[t]
        o_ref[...] = acc

    return pl.pallas_call(
        body, name=name, grid=(r // tr,), in_specs=[pl.BlockSpec((s, tr, c), lambda i: (0, i, 0))],
        out_specs=pl.BlockSpec((tr, c), lambda i: (i, 0)), out_shape=jax.ShapeDtypeStruct((r, c), F32),
        compiler_params=_cparams(("parallel",)),
    )(parts)


def add_pair(a, b, name):
    s, r, c = a.shape
    tr = r if r <= 256 else 256

    def body(a_ref, b_ref, o_ref):
        o_ref[...] = a_ref[...] + b_ref[...]

    spec = pl.BlockSpec((None, tr, c), lambda t, i: (t, i, 0))
    return pl.pallas_call(
        body, name=name, grid=(s, r // tr), in_specs=[spec, spec], out_specs=spec,
        out_shape=jax.ShapeDtypeStruct((s, r, c), F32), compiler_params=_cparams(("parallel", "parallel")),
    )(a, b)


N_CHIPS = 4
N_DEV = 8
_ANY = pl.BlockSpec(memory_space=pl.ANY)


def _mesh_pos():
    return lax.axis_index("x"), lax.axis_index("y"), lax.axis_index("c")


def _other_chips(x, y):
    ps = [(1 - x, y), (x, 1 - y), (1 - x, 1 - y)]
    return [(p, 2 * p[0] + p[1]) for p in ps]


def _remote(src, dst, send_sem, recv_sem, dev):
    return pltpu.make_async_remote_copy(src_ref=src, dst_ref=dst, send_sem=send_sem, recv_sem=recv_sem,
                                        device_id=dev, device_id_type=MESH)


def gather_chips(arrs, name):
    n = len(arrs)

    def body(*refs):
        ins, outs = refs[:n], refs[n:2 * n]
        send_sems, recv_sems, local_sems = refs[2 * n:]
        x, y, c = _mesh_pos()
        me = 2 * x + y
        peers = _other_chips(x, y)
        locs, sends = [], []
        for a in range(n):
            cp = pltpu.make_async_copy(ins[a], outs[a].at[me], local_sems.at[a])
            cp.start()
            locs.append(cp)
            for j, ((px, py), _) in enumerate(peers):
                r = _remote(ins[a], outs[a].at[me], send_sems.at[a, j], recv_sems.at[a, j], (px, py, c))
                r.start()
                sends.append(r)
        for a in range(n):
            for j, ((px, py), t) in enumerate(peers):
                _remote(ins[a], outs[a].at[t], send_sems.at[a, j], recv_sems.at[a, j], (px, py, c)).wait_recv()
        for r in sends:
            r.wait_send()
        for cp in locs:
            cp.wait()

    return pl.pallas_call(
        body, name=name, in_specs=[_ANY] * n, out_specs=[_ANY] * n,
        out_shape=[jax.ShapeDtypeStruct((N_CHIPS,) + a.shape, a.dtype) for a in arrs],
        scratch_shapes=[pltpu.SemaphoreType.DMA((n, 3)), pltpu.SemaphoreType.DMA((n, 3)), pltpu.SemaphoreType.DMA((n,))],
    )(*arrs)


def exchange_sibling_half(gs, name):
    n = len(gs)

    def body(*refs):
        ins, outs = refs[:n], refs[n:2 * n]
        send_sems, recv_sems = refs[2 * n:]
        x, y, c = _mesh_pos()
        sends = []
        for a in range(n):
            half = gs[a].shape[1] // 2
            src = ins[a].at[:, pl.ds((1 - c) * half, half), :]
            r = _remote(src, outs[a], send_sems.at[a], recv_sems.at[a], (x, y, 1 - c))
            r.start()
            sends.append(r)
        for r in sends:
            r.wait_recv()
        for r in sends:
            r.wait_send()

    return pl.pallas_call(
        body, name=name, in_specs=[_ANY] * n, out_specs=[_ANY] * n,
        out_shape=[jax.ShapeDtypeStruct((g.shape[0], g.shape[1] // 2, g.shape[2]), g.dtype) for g in gs],
        scratch_shapes=[pltpu.SemaphoreType.DMA((n,)), pltpu.SemaphoreType.DMA((n,))],
    )(*gs)


def exchange_chips(ps, name):
    n = len(ps)

    def body(*refs):
        ins, outs = refs[:n], refs[n:2 * n]
        send_sems, recv_sems, local_sems = refs[2 * n:]
        x, y, c = _mesh_pos()
        me = 2 * x + y
        peers = _other_chips(x, y)
        locs, sends = [], []
        for a in range(n):
            cp = pltpu.make_async_copy(ins[a].at[me], outs[a].at[me], local_sems.at[a])
            cp.start()
            locs.append(cp)
            for j, ((px, py), t) in enumerate(peers):
                r = _remote(ins[a].at[t], outs[a].at[me], send_sems.at[a, j], recv_sems.at[a, j], (px, py, c))
                r.start()
                sends.append(r)
        for a in range(n):
            for j, ((px, py), t) in enumerate(peers):
                _remote(ins[a].at[t], outs[a].at[t], send_sems.at[a, j], recv_sems.at[a, j], (px, py, c)).wait_recv()
        for r in sends:
            r.wait_send()
        for cp in locs:
            cp.wait()

    return pl.pallas_call(
        body, name=name, in_specs=[_ANY] * n, out_specs=[_ANY] * n,
        out_shape=[jax.ShapeDtypeStruct(p.shape, p.dtype) for p in ps],
        scratch_shapes=[pltpu.SemaphoreType.DMA((n, 3)), pltpu.SemaphoreType.DMA((n, 3)), pltpu.SemaphoreType.DMA((n,))],
    )(*ps)


def share_sibling(groups, name):
    flat = [f for grp in groups for f in grp]
    n = len(flat)
    nw = len(groups)

    def body(*refs):
        ins, outs = refs[:n], refs[n:n + nw]
        send_sems, recv_sems, local_sems = refs[n + nw:]
        x, y, c = _mesh_pos()
        locs, sends, k = [], [], 0
        for w, grp in enumerate(groups):
            for l in range(len(grp)):
                half = grp[l].shape[0]
                dst = outs[w].at[l, pl.ds(c * half, half), :]
                cp = pltpu.make_async_copy(ins[k], dst, local_sems.at[k])
                cp.start()
                locs.append(cp)
                r = _remote(ins[k], dst, send_sems.at[k], recv_sems.at[k], (x, y, 1 - c))
                r.start()
                sends.append(r)
                k += 1
        k = 0
        for w, grp in enumerate(groups):
            for l in range(len(grp)):
                half = grp[l].shape[0]
                theirs = outs[w].at[l, pl.ds((1 - c) * half, half), :]
                _remote(ins[k], theirs, send_sems.at[k], recv_sems.at[k], (x, y, 1 - c)).wait_recv()
                k += 1
        for r in sends:
            r.wait_send()
        for cp in locs:
            cp.wait()

    return pl.pallas_call(
        body, name=name, in_specs=[_ANY] * n, out_specs=[_ANY] * nw,
        out_shape=[jax.ShapeDtypeStruct((len(grp), 2 * grp[0].shape[0], grp[0].shape[1]), F32) for grp in groups],
        scratch_shapes=[pltpu.SemaphoreType.DMA((n,)), pltpu.SemaphoreType.DMA((n,)), pltpu.SemaphoreType.DMA((n,))],
    )(*flat)


def gather_all(small, name):
    flips = [(fx, fy, fc) for fx in (0, 1) for fy in (0, 1) for fc in (0, 1)][1:]

    def body(in_ref, out_ref, send_sems, recv_sems, local_sem):
        x, y, c = _mesh_pos()
        me = 4 * x + 2 * y + c
        cp = pltpu.make_async_copy(in_ref, out_ref.at[me], local_sem)
        cp.start()
        peers = [((1 - x) if fx else x, (1 - y) if fy else y, (1 - c) if fc else c) for fx, fy, fc in flips]
        sends = []
        for k, p in enumerate(peers):
            r = _remote(in_ref, out_ref.at[me], send_sems.at[k], recv_sems.at[k], p)
            r.start()
            sends.append(r)
        for k, p in enumerate(peers):
            _remote(in_ref, out_ref.at[4 * p[0] + 2 * p[1] + p[2]], send_sems.at[k], recv_sems.at[k], p).wait_recv()
        for r in sends:
            r.wait_send()
        cp.wait()

    return pl.pallas_call(
        body, name=name, in_specs=[_ANY], out_specs=_ANY,
        out_shape=jax.ShapeDtypeStruct((N_DEV,) + small.shape, small.dtype),
        scratch_shapes=[pltpu.SemaphoreType.DMA((N_DEV - 1,)), pltpu.SemaphoreType.DMA((N_DEV - 1,)),
                        pltpu.SemaphoreType.DMA],
    )(small)


BIG = ("gdn_w_in", "gdn_w_out", "hgrn_w_in", "hgrn_w_out", "mlp_w_up", "mlp_w_down")
WEIGHTS = ("gdn_w_in", "gdn_conv", "gdn_a_log", "gdn_dt_bias", "gdn_onorm", "gdn_w_out", "hgrn_w_in", "hgrn_lb_logits",
           "hgrn_gnorm", "hgrn_w_out", "norm_mix", "norm_mlp", "mlp_w_up", "mlp_w_down", "norm_final")
SMALL_ROWS = 96


def _pad_lanes(v, n):
    return jnp.pad(v, [(0, 0)] * (v.ndim - 1) + [(0, n - v.shape[-1])])


def _local_grads(x, target, w):
    bsz, t_len, d = x.shape
    n_tok = bsz * t_len
    h = x.reshape(n_tok, d)
    lbl = w["hgrn_lb_logits"]
    saved = []
    for i in range(DEPTH):
        j = i // 2
        nmix = w["norm_mix"][i][None, :]
        if i % 2 == 0:
            y, proj = norm_matmul(h, nmix, w["gdn_w_in"][j], False, f"in_proj{i}")
            al = _pad_lanes(w["gdn_a_log"][j][None, :], 128)
            dtb = _pad_lanes(w["gdn_dt_bias"][j][None, :], 128)
            on = w["gdn_onorm"][j][None, :]
            og, ssave = gdn_forward(proj, w["gdn_conv"][j], al, dtb, on, bsz)
            mix = (proj, ssave, al, dtb, on)
            w_out = w["gdn_w_out"][j]
        else:
            y, proj = norm_matmul(h, nmix, w["hgrn_w_in"][j], False, f"in_proj{i}")
            o, ssave = hgrn_forward(proj, lbl, i, bsz)
            gn = w["hgrn_gnorm"][j][None, :]
            og = hgrn_post_forward(o, proj, gn, f"hgrn_post{i}")
            mix = (proj, ssave, o, gn)
            w_out = w["hgrn_w_out"][j]
        h1 = matmul(og, w_out, "nn", f"out_proj{i}", extra=h, epilogue="add")
        nmlp = w["norm_mlp"][i][None, :]
        z, a, r = norm_matmul(h1, nmlp, w["mlp_w_up"][i], True, f"mlp_up{i}")
        h2 = matmul(a, w["mlp_w_down"][i], "nn", f"mlp_down{i}", extra=h1, epilogue="add")
        saved.append((h, nmix, y, mix, og, w_out, h1, nmlp, z, a, r))
        h = h2
    loss_row, dh, d_nf = loss_head(h, w["norm_final"][None, :], target.reshape(n_tok, d))

    big = {k: [None] * (DEPTH if k.startswith("mlp") else DEPTH // 2) for k in BIG}
    d_nmix, d_nmlp = [None] * DEPTH, [None] * DEPTH
    d_conv, d_alog, d_dtb, d_onorm, d_gnorm = [None] * 2, [None] * 2, [None] * 2, [None] * 2, [None] * 2
    d_lbl = jnp.zeros((DEPTH, d), F32)
    for i in reversed(range(DEPTH)):
        j = i // 2
        h_in, nmix, y, mix, og, w_out, h1, nmlp, z, a, r = saved[i]
        du = matmul(dh, w["mlp_w_down"][i], "nt", f"d_mlp_act{i}", out_dtype=BF16, extra=r, epilogue="mul2")
        big["mlp_w_down"][i] = matmul(a, dh, "tn", f"dw_down{i}").reshape(N_CHIPS, -1, d)
        dz = matmul(du, w["mlp_w_up"][i], "nt", f"d_mlp_in{i}")
        big["mlp_w_up"][i] = matmul(z, du, "tn", f"dw_up{i}", shards=N_CHIPS)
        dh1, d_nmlp[i] = norm_backward(h1, nmlp, dz, dh, f"d_norm_mlp{i}")
        dog = matmul(dh1, w_out, "nt", f"d_mix_out{i}")
        dw_out = matmul(og, dh1, "tn", f"dw_out{i}").reshape(N_CHIPS, -1, d)
        if i % 2 == 0:
            proj, ssave, al, dtb, on = mix
            dq, dk, dv, dg, dab, dcw, dal, ddt, don = gdn_backward(proj, w["gdn_conv"][j], al, dtb, on, ssave, dog, bsz)
            dproj = jnp.concatenate([dq, dk, dv, dg, dab], axis=1)
            dy = matmul(dproj, w["gdn_w_in"][j], "nt", f"d_in_proj{i}")
            dw_in = matmul(y, dproj, "tn", f"dw_in{i}")[:, :GDN_IN]
            big["gdn_w_in"][j] = jnp.transpose(dw_in.reshape(d, N_CHIPS, GDN_IN // N_CHIPS), (1, 0, 2))
            big["gdn_w_out"][j] = dw_out
            d_conv[j] = jnp.transpose(dcw, (2, 1, 0, 3)).reshape(CONV_K, 3 * d)
            d_alog[j], d_dtb[j], d_onorm[j] = dal[0, :HEADS], ddt[0, :HEADS], don[0]
        else:
            proj, ssave, o, gn = mix
            do_raw, dgate, dgn = hgrn_post_backward(o, proj, gn, dog, f"d_hgrn_post{i}")
            dq, df, dv, dlb = hgrn_backward(proj, lbl, ssave, do_raw, i, bsz)
            dproj = jnp.concatenate([dq, df, dv, dgate], axis=1)
            dy = matmul(dproj, w["hgrn_w_in"][j], "nt", f"d_in_proj{i}")
            big["hgrn_w_in"][j] = matmul(y, dproj, "tn", f"dw_in{i}", shards=N_CHIPS)
            big["hgrn_w_out"][j] = dw_out
            d_gnorm[j] = dgn[0]
            d_lbl = d_lbl + jnp.transpose(dlb, (1, 0, 2)).reshape(DEPTH, d)
        dh, d_nmix[i] = norm_backward(h_in, nmix, dy, dh1, f"d_norm_mix{i}")
    small = {
        "gdn_conv": jnp.stack(d_conv), "gdn_a_log": jnp.stack(d_alog), "gdn_dt_bias": jnp.stack(d_dtb),
        "gdn_onorm": jnp.stack(d_onorm), "hgrn_lb_logits": d_lbl, "hgrn_gnorm": jnp.stack(d_gnorm),
        "norm_mix": jnp.concatenate(d_nmix, axis=0), "norm_mlp": jnp.concatenate(d_nmlp, axis=0), "norm_final": d_nf[0],
    }
    return loss_row, dh.reshape(x.shape), big, small


_SMALL_LAYOUT = {
    "norm_mix": (0, 4, D_MODEL), "norm_mlp": (8, 4, D_MODEL), "norm_final": (16, 1, D_MODEL),
    "hgrn_lb_logits": (24, 4, D_MODEL), "gdn_onorm": (32, 2, 128), "gdn_a_log": (40, 2, HEADS),
    "gdn_dt_bias": (48, 2, HEADS), "loss": (56, 1, 128), "gdn_conv": (64, 24, D_MODEL), "hgrn_gnorm": (88, 2, D_MODEL),
}


def _pack_small(small, loss_row):
    rows = []
    for name, (first, nrow, lanes) in _SMALL_LAYOUT.items():
        v = loss_row if name == "loss" else small[name]
        v = _pad_lanes(v.reshape(nrow, -1), D_MODEL)
        rows.append(jnp.pad(v, ((0, -nrow % 8), (0, 0))))
    return jnp.concatenate(rows, axis=0)


def _unpack_small(packed, name, shape):
    first, nrow, lanes = _SMALL_LAYOUT[name]
    return packed[first:first + nrow, :lanes].reshape(shape)


def kernel(x, gdn_w_in, gdn_conv, gdn_a_log, gdn_dt_bias, gdn_onorm, gdn_w_out, hgrn_w_in, hgrn_lb_logits, hgrn_gnorm, hgrn_w_out, norm_mix, norm_mlp, mlp_w_up, mlp_w_down, norm_final, loss_target, m_gdn_w_in, m_gdn_conv, m_gdn_a_log, m_gdn_dt_bias, m_gdn_onorm, m_gdn_w_out, m_hgrn_w_in, m_hgrn_lb_logits, m_hgrn_gnorm, m_hgrn_w_out, m_norm_mix, m_norm_mlp, m_mlp_w_up, m_mlp_w_down, m_norm_final, v_gdn_w_in, v_gdn_conv, v_gdn_a_log, v_gdn_dt_bias, v_gdn_onorm, v_gdn_w_out, v_hgrn_w_in, v_hgrn_lb_logits, v_hgrn_gnorm, v_hgrn_w_out, v_norm_mix, v_norm_mlp, v_mlp_w_up, v_mlp_w_down, v_norm_final):
    p = dict(gdn_w_in=gdn_w_in, gdn_conv=gdn_conv, gdn_a_log=gdn_a_log, gdn_dt_bias=gdn_dt_bias, gdn_onorm=gdn_onorm,
             gdn_w_out=gdn_w_out, hgrn_w_in=hgrn_w_in, hgrn_lb_logits=hgrn_lb_logits, hgrn_gnorm=hgrn_gnorm,
             hgrn_w_out=hgrn_w_out, norm_mix=norm_mix, norm_mlp=norm_mlp, mlp_w_up=mlp_w_up, mlp_w_down=mlp_w_down,
             norm_final=norm_final)
    m = dict(gdn_w_in=m_gdn_w_in, gdn_conv=m_gdn_conv, gdn_a_log=m_gdn_a_log, gdn_dt_bias=m_gdn_dt_bias,
             gdn_onorm=m_gdn_onorm, gdn_w_out=m_gdn_w_out, hgrn_w_in=m_hgrn_w_in, hgrn_lb_logits=m_hgrn_lb_logits,
             hgrn_gnorm=m_hgrn_gnorm, hgrn_w_out=m_hgrn_w_out, norm_mix=m_norm_mix, norm_mlp=m_norm_mlp,
             mlp_w_up=m_mlp_w_up, mlp_w_down=m_mlp_w_down, norm_final=m_norm_final)
    v = dict(gdn_w_in=v_gdn_w_in, gdn_conv=v_gdn_conv, gdn_a_log=v_gdn_a_log, gdn_dt_bias=v_gdn_dt_bias,
             gdn_onorm=v_gdn_onorm, gdn_w_out=v_gdn_w_out, hgrn_w_in=v_hgrn_w_in, hgrn_lb_logits=v_hgrn_lb_logits,
             hgrn_gnorm=v_hgrn_gnorm, hgrn_w_out=v_hgrn_w_out, norm_mix=v_norm_mix, norm_mlp=v_norm_mlp,
             mlp_w_up=v_mlp_w_up, mlp_w_down=v_mlp_w_down, norm_final=v_norm_final)
    xi, yi, ci = _mesh_pos()
    chip = 2 * xi + yi
    d = D_MODEL

    sharded = list(BIG) + ["gdn_conv", "hgrn_gnorm"]
    got = dict(zip(sharded, gather_chips([p[k].astype(BF16) if k in BIG else p[k] for k in sharded], "gather_weights")))
    cols = lambda g: jnp.transpose(g, (1, 2, 0, 3)).reshape(g.shape[1], g.shape[2], -1)
    rows = lambda g: jnp.transpose(g, (1, 0, 2, 3)).reshape(g.shape[1], -1, g.shape[3])
    w = dict(p)
    w["gdn_w_in"] = _pad_lanes(cols(got["gdn_w_in"]), GDN_IN_PAD)
    w["hgrn_w_in"] = cols(got["hgrn_w_in"])
    w["mlp_w_up"] = cols(got["mlp_w_up"])
    w["gdn_conv"] = cols(got["gdn_conv"])
    w["gdn_w_out"] = rows(got["gdn_w_out"])
    w["hgrn_w_out"] = rows(got["hgrn_w_out"])
    w["mlp_w_down"] = rows(got["mlp_w_down"])
    w["hgrn_gnorm"] = jnp.transpose(got["hgrn_gnorm"], (1, 0, 2)).reshape(got["hgrn_gnorm"].shape[1], -1)

    loss_row, grad_x, big, small = _local_grads(x, loss_target, w)

    order = [(k, l) for k in BIG for l in range(len(big[k]))]
    gs = [big[k][l] for k, l in order]
    theirs = exchange_sibling_half(gs, "reduce_cores")
    mine = [lax.dynamic_slice_in_dim(g, ci * (g.shape[1] // 2), g.shape[1] // 2, axis=1) for g in gs]
    ps = [add_pair(a, b, f"add_cores{n}") for n, (a, b) in enumerate(zip(mine, theirs))]
    slots = exchange_chips(ps, "reduce_chips")
    halves = [add_slots(s, f"add_chips{n}") for n, s in enumerate(slots)]
    groups = [[halves[n] for n, (k, l) in enumerate(order) if k == name] for name in BIG]
    grads = dict(zip(BIG, share_sibling(groups, "share_cores")))

    total = add_slots(gather_all(_pack_small(small, loss_row), "gather_small"), "add_small")
    loss = total[_SMALL_LAYOUT["loss"][0], 0]
    for name in WEIGHTS:
        if name in BIG:
            continue
        if name == "gdn_conv":
            full = _unpack_small(total, name, (2, CONV_K, 3 * d))
            grads[name] = lax.dynamic_slice_in_dim(full, chip * (3 * d // N_CHIPS), 3 * d // N_CHIPS, axis=2)
        elif name == "hgrn_gnorm":
            full = _unpack_small(total, name, (2, d))
            grads[name] = lax.dynamic_slice_in_dim(full, chip * (d // N_CHIPS), d // N_CHIPS, axis=1)
        else:
            grads[name] = _unpack_small(total, name, p[name].shape)

    delta, new_m, new_v = {}, {}, {}
    for name in WEIGHTS:
        delta[name], new_m[name], new_v[name] = adamw(p[name], grads[name], m[name], v[name], f"adamw_{name}")
    return (loss, grad_x, *[grads[n] for n in WEIGHTS], *[delta[n] for n in WEIGHTS],
            *[new_m[n] for n in WEIGHTS], *[new_v[n] for n in WEIGHTS])


def add_core_halves(g, theirs, core, name):
    s, r, c = g.shape
    r2 = r // 2
    tr = min(r2, 256)
    nb = r2 // tr

    def body(core_ref, g_ref, t_ref, o_ref):
        o_ref[...] = g_ref[...] + t_ref[...]

    grid_spec = pltpu.PrefetchScalarGridSpec(
        num_scalar_prefetch=1, grid=(s, nb),
        in_specs=[pl.BlockSpec((None, tr, c), lambda t, i, cr: (t, cr[0] * nb + i, 0)),
                  pl.BlockSpec((None, tr, c), lambda t, i, cr: (t, i, 0))],
        out_specs=pl.BlockSpec((None, tr, c), lambda t, i, cr: (t, i, 0)))
    return pl.pallas_call(body, name=name, grid_spec=grid_spec, out_shape=jax.ShapeDtypeStruct((s, r2, c), F32),
                          compiler_params=_cparams(("parallel", "parallel")))(core, g, theirs)


def add_chip_slots(slots, name):
    n_l = len(slots)
    s, r2, c = slots[0].shape
    tr = min(r2, 256)
    nb = r2 // tr

    def body(*refs):
        ins, o_ref = refs[:n_l], refs[n_l]
        for k in range(n_l):
            @pl.when(pl.program_id(0) == k)
            def _(k=k):
                acc = ins[k][0]
                for t in range(1, s):
                    acc = acc + ins[k][t]
                o_ref[...] = acc

    in_specs = [pl.BlockSpec((s, tr, c), lambda l, i, k=k: (0, jnp.where(l == k, i, 0), 0)) for k in range(n_l)]
    return pl.pallas_call(
        body, name=name, grid=(n_l, nb), in_specs=in_specs, out_specs=pl.BlockSpec((None, tr, c), lambda l, i: (l, i, 0)),
        out_shape=jax.ShapeDtypeStruct((n_l, r2, c), F32), compiler_params=_cparams(("arbitrary", "arbitrary")),
    )(*slots)


def adamw_halves(w, m, v, mine, theirs, name):
    n_l, r, c = w.shape
    r2 = r // 2
    tr = min(r2, 256)
    nb = r2 // tr
    c1 = 1.0 / (1.0 - ADAM_B1 ** ADAM_STEP)
    c2 = 1.0 / (1.0 - ADAM_B2 ** ADAM_STEP)

    def body(w_ref, m_ref, v_ref, mine_ref, theirs_ref, g_ref, d_ref, nm_ref, nv_ref):
        my_half = (pl.program_id(1) // nb) == lax.axis_index("c")
        gg = jnp.where(my_half, mine_ref[...], theirs_ref[...])
        nm = ADAM_B1 * m_ref[...] + (1.0 - ADAM_B1) * gg
        nv = ADAM_B2 * v_ref[...] + (1.0 - ADAM_B2) * (gg * gg)
        g_ref[...] = gg
        d_ref[...] = -ADAM_LR * ((nm * c1) / (jnp.sqrt(nv * c2) + ADAM_EPS) + ADAM_WD * w_ref[...])
        nm_ref[...] = nm
        nv_ref[...] = nv

    full = pl.BlockSpec((None, tr, c), lambda l, i: (l, i, 0))
    half = pl.BlockSpec((None, tr, c), lambda l, i: (l, i % nb, 0))
    sds = jax.ShapeDtypeStruct((n_l, r, c), F32)
    return pl.pallas_call(
        body, name=name, grid=(n_l, r // tr), in_specs=[full, full, full, half, half], out_specs=[full] * 4,
        out_shape=(sds,) * 4, compiler_params=_cparams(("parallel", "parallel")),
    )(w, m, v, mine, theirs)


def _layer_weight(kind, i):
    if kind == "up":
        return "mlp_w_up", i
    if kind == "down":
        return "mlp_w_down", i
    return ("gdn_w_" if i % 2 == 0 else "hgrn_w_") + kind, i // 2


def kernel(x, gdn_w_in, gdn_conv, gdn_a_log, gdn_dt_bias, gdn_onorm, gdn_w_out, hgrn_w_in, hgrn_lb_logits, hgrn_gnorm, hgrn_w_out, norm_mix, norm_mlp, mlp_w_up, mlp_w_down, norm_final, loss_target, m_gdn_w_in, m_gdn_conv, m_gdn_a_log, m_gdn_dt_bias, m_gdn_onorm, m_gdn_w_out, m_hgrn_w_in, m_hgrn_lb_logits, m_hgrn_gnorm, m_hgrn_w_out, m_norm_mix, m_norm_mlp, m_mlp_w_up, m_mlp_w_down, m_norm_final, v_gdn_w_in, v_gdn_conv, v_gdn_a_log, v_gdn_dt_bias, v_gdn_onorm, v_gdn_w_out, v_hgrn_w_in, v_hgrn_lb_logits, v_hgrn_gnorm, v_hgrn_w_out, v_norm_mix, v_norm_mlp, v_mlp_w_up, v_mlp_w_down, v_norm_final):
    p = dict(gdn_w_in=gdn_w_in, gdn_conv=gdn_conv, gdn_a_log=gdn_a_log, gdn_dt_bias=gdn_dt_bias, gdn_onorm=gdn_onorm,
             gdn_w_out=gdn_w_out, hgrn_w_in=hgrn_w_in, hgrn_lb_logits=hgrn_lb_logits, hgrn_gnorm=hgrn_gnorm,
             hgrn_w_out=hgrn_w_out, norm_mix=norm_mix, norm_mlp=norm_mlp, mlp_w_up=mlp_w_up, mlp_w_down=mlp_w_down,
             norm_final=norm_final)
    m = dict(gdn_w_in=m_gdn_w_in, gdn_conv=m_gdn_conv, gdn_a_log=m_gdn_a_log, gdn_dt_bias=m_gdn_dt_bias,
             gdn_onorm=m_gdn_onorm, gdn_w_out=m_gdn_w_out, hgrn_w_in=m_hgrn_w_in, hgrn_lb_logits=m_hgrn_lb_logits,
             hgrn_gnorm=m_hgrn_gnorm, hgrn_w_out=m_hgrn_w_out, norm_mix=m_norm_mix, norm_mlp=m_norm_mlp,
             mlp_w_up=m_mlp_w_up, mlp_w_down=m_mlp_w_down, norm_final=m_norm_final)
    v = dict(gdn_w_in=v_gdn_w_in, gdn_conv=v_gdn_conv, gdn_a_log=v_gdn_a_log, gdn_dt_bias=v_gdn_dt_bias,
             gdn_onorm=v_gdn_onorm, gdn_w_out=v_gdn_w_out, hgrn_w_in=v_hgrn_w_in, hgrn_lb_logits=v_hgrn_lb_logits,
             hgrn_gnorm=v_hgrn_gnorm, hgrn_w_out=v_hgrn_w_out, norm_mix=v_norm_mix, norm_mlp=v_norm_mlp,
             mlp_w_up=v_mlp_w_up, mlp_w_down=v_mlp_w_down, norm_final=v_norm_final)
    xi, yi, ci = _mesh_pos()
    chip = 2 * xi + yi
    core = jnp.reshape(ci, (1,)).astype(jnp.int32)
    d = D_MODEL
    bsz, t_len, _ = x.shape
    n_tok = bsz * t_len

    def shard(kind, i):
        name, idx = _layer_weight(kind, i)
        return p[name][idx].astype(BF16)

    def w_in_of(i, slots):
        if i % 2 == 0:
            return _pad_lanes(jnp.transpose(slots, (1, 0, 2)).reshape(d, GDN_IN), GDN_IN_PAD)
        return slots

    (first,) = run_carries([gather_carry([shard("in", 0), p["gdn_conv"], p["hgrn_gnorm"]])], "gather_first")
    conv = jnp.transpose(first[1], (1, 2, 0, 3)).reshape(DEPTH // 2, CONV_K, 3 * d)
    gnorm = jnp.transpose(first[2], (1, 0, 2)).reshape(DEPTH // 2, d)
    lbl = p["hgrn_lb_logits"]
    h = x.reshape(n_tok, d)
    next_in, next_out = first[0], None
    saved = []
    for i in range(DEPTH):
        j = i // 2
        w_in = w_in_of(i, next_in)
        nmix = p["norm_mix"][i][None, :]
        (y, proj), got = norm_matmul(h, nmix, w_in, False, f"in_proj{i}",
                                     [gather_carry([shard("out", 0)])] if i == 0 else [])
        if i == 0:
            next_out = got[0][0]
        ride = [gather_carry([shard("up", i), shard("down", i)])]
        if i % 2 == 0:
            al = _pad_lanes(p["gdn_a_log"][j][None, :], 128)
            dtb = _pad_lanes(p["gdn_dt_bias"][j][None, :], 128)
            on = p["gdn_onorm"][j][None, :]
            (og, ssave), got = gdn_forward(proj, conv[j], al, dtb, on, bsz, f"gdn_fwd{i}", ride)
            mix = (proj, ssave, al, dtb, on)
        else:
            (o, ssave), got = hgrn_forward(proj, lbl, i, bsz, ride)
            gn = gnorm[j][None, :]
            og = hgrn_post_forward(o, proj, gn, f"hgrn_post{i}")
            mix = (proj, ssave, o, gn)
        w_up, w_down = got[0][0], got[0][1].reshape(MLP_HIDDEN, d)
        w_out = next_out.reshape(d, d)
        h1, _ = matmul(og, w_out, "nn", f"out_proj{i}", extra=h, epilogue="add")
        nmlp = p["norm_mlp"][i][None, :]
        (z, r), got = norm_matmul(h1, nmlp, w_up, True, f"mlp_up{i}",
                                     [gather_carry([shard("in", i + 1)])] if i + 1 < DEPTH else [])
        if i + 1 < DEPTH:
            next_in = got[0][0]
        h2, got = matmul(r, w_down, "nn", f"mlp_down{i}", extra=h1, epilogue="add", square_a=True,
                         carries=[gather_carry([shard("out", i + 1)])] if i + 1 < DEPTH else [])
        if i + 1 < DEPTH:
            next_out = got[0][0]
        saved.append((h, nmix, y, mix, og, w_in, w_out, h1, nmlp, z, r, w_up, w_down))
        h = h2
    loss_row, dh, dhb, d_nf = loss_head(h, p["norm_final"][None, :], loss_target.reshape(n_tok, d))

    G = {k: [None] * DEPTH for k in ("in", "out", "up", "down")}
    P = {k: [None] * DEPTH for k in ("in", "out", "up", "down")}
    slots = {k: [None] * DEPTH for k in ("in", "out", "up", "down")}
    d_nmix, d_nmlp = [None] * DEPTH, [None] * DEPTH
    d_conv, d_alog, d_dtb, d_onorm, d_gnorm = [None] * 2, [None] * 2, [None] * 2, [None] * 2, [None] * 2
    d_lbl = jnp.zeros((DEPTH, d), F32)
    for i in reversed(range(DEPTH)):
        j = i // 2
        h_in, nmix, y, mix, og, w_in, w_out, h1, nmlp, z, r, w_up, w_down = saved[i]
        ride = [sibling_half_carry([G["in"][i + 1]])] if i + 1 < DEPTH else []
        du, got = matmul(dhb, w_down, "nt", f"d_mlp_act{i}", out_dtype=BF16, extra=r, epilogue="mul2", carries=ride)
        if i + 1 < DEPTH:
            P["in"][i + 1] = add_core_halves(G["in"][i + 1], got[0][0], core, f"add_cores_in{i + 1}")
        G["down"][i] = matmul(r, dhb, "tn", f"dw_down{i}", square_a=True)[0].reshape(N_CHIPS, -1, d)
        dz, _ = matmul(du, w_up, "nt", f"d_mlp_in{i}")
        G["up"][i], _ = matmul(z, du, "tn", f"dw_up{i}", shards=N_CHIPS)
        (dh1, dh1b, d_nmlp[i]), got = norm_backward(h1, nmlp, dz, dh, f"d_norm_mlp{i}",
                                                    [sibling_half_carry([G["up"][i], G["down"][i]])])
        P["up"][i] = add_core_halves(G["up"][i], got[0][0], core, f"add_cores_up{i}")
        P["down"][i] = add_core_halves(G["down"][i], got[0][1], core, f"add_cores_down{i}")
        dog, _ = matmul(dh1b, w_out, "nt", f"d_mix_out{i}")
        G["out"][i] = matmul(og, dh1b, "tn", f"dw_out{i}")[0].reshape(N_CHIPS, -1, d)
        to_chips = [("up", i), ("down", i)] + ([("in", i + 1), ("out", i + 1)] if i + 1 < DEPTH else [])
        ride = [sibling_half_carry([G["out"][i]]), chips_carry([P[k][l] for k, l in to_chips])]
        if i % 2 == 0:
            proj, ssave, al, dtb, on = mix
            (dproj, dcw, dal, ddt, don), got = gdn_backward(proj, conv[j], al, dtb, on, ssave, dog, bsz, f"gdn_bwd{i}", ride)
            d_conv[j] = jnp.transpose(dcw, (2, 1, 0, 3)).reshape(CONV_K, 3 * d)
            d_alog[j], d_dtb[j], d_onorm[j] = dal[0, :HEADS], ddt[0, :HEADS], don[0]
        else:
            proj, ssave, o, gn = mix
            do_raw, dgate, dgn = hgrn_post_backward(o, proj, gn, dog, f"d_hgrn_post{i}")
            (dproj, dlb), got = hgrn_backward(proj, lbl, ssave, do_raw, dgate, i, bsz, ride)
            d_gnorm[j] = dgn[0]
            d_lbl = d_lbl + jnp.transpose(dlb, (1, 0, 2)).reshape(DEPTH, d)
        P["out"][i] = add_core_halves(G["out"][i], got[0][0], core, f"add_cores_out{i}")
        for (k, l), s in zip(to_chips, got[1]):
            slots[k][l] = s
        if i % 2 == 0:
            dw_in = matmul(y, dproj, "tn", f"dw_in{i}")[0][:, :GDN_IN]
            G["in"][i] = jnp.transpose(dw_in.reshape(d, N_CHIPS, GDN_IN // N_CHIPS), (1, 0, 2))
        else:
            G["in"][i], _ = matmul(y, dproj, "tn", f"dw_in{i}", shards=N_CHIPS)
        ride = [sibling_half_carry([G["in"][0]]), chips_carry([P["out"][0]])] if i == 0 else []
        dy, got = matmul(dproj, w_in, "nt", f"d_in_proj{i}", carries=ride)
        ride = []
        if i == 0:
            P["in"][0] = add_core_halves(G["in"][0], got[0][0], core, "add_cores_in0")
            slots["out"][0] = got[1][0]
            ride = [chips_carry([P["in"][0]])]
        (dh, dhb, d_nmix[i]), got = norm_backward(h_in, nmix, dy, dh1, f"d_norm_mix{i}", ride)
        if i == 0:
            slots["in"][0] = got[0][0]
    grad_x = dh.reshape(x.shape)

    by_weight = {}
    for kind in ("in", "out", "up", "down"):
        for i in range(DEPTH):
            by_weight.setdefault(_layer_weight(kind, i)[0], []).append(slots[kind][i])
    mine = {name: add_chip_slots(by_weight[name], f"add_chips_{name}") for name in BIG}
    small = {
        "gdn_conv": jnp.stack(d_conv), "gdn_a_log": jnp.stack(d_alog), "gdn_dt_bias": jnp.stack(d_dtb),
        "gdn_onorm": jnp.stack(d_onorm), "hgrn_lb_logits": d_lbl, "hgrn_gnorm": jnp.stack(d_gnorm),
        "norm_mix": jnp.concatenate(d_nmix, axis=0), "norm_mlp": jnp.concatenate(d_nmlp, axis=0), "norm_final": d_nf[0],
    }
    theirs, (blocks,) = run_carries([sibling_copy_carry([mine[name] for name in BIG]),
                                     gather_all_carry(_pack_small(small, loss_row))], "share_cores")
    theirs = dict(zip(BIG, theirs))

    total = add_slots(blocks, "add_small")
    loss = total[_SMALL_LAYOUT["loss"][0], 0]
    grads, delta, new_m, new_v = {}, {}, {}, {}
    for name in WEIGHTS:
        if name in BIG:
            grads[name], delta[name], new_m[name], new_v[name] = adamw_halves(
                p[name], m[name], v[name], mine[name], theirs[name], f"adamw_{name}")
            continue
        if name == "gdn_conv":
            full = _unpack_small(total, name, (2, CONV_K, 3 * d))
            grads[name] = lax.dynamic_slice_in_dim(full, chip * (3 * d // N_CHIPS), 3 * d // N_CHIPS, axis=2)
        elif name == "hgrn_gnorm":
            full = _unpack_small(total, name, (2, d))
            grads[name] = lax.dynamic_slice_in_dim(full, chip * (d // N_CHIPS), d // N_CHIPS, axis=1)
        else:
            grads[name] = _unpack_small(total, name, p[name].shape)
        delta[name], new_m[name], new_v[name] = adamw(p[name], grads[name], m[name], v[name], f"adamw_{name}")
    return (loss, grad_x, *[grads[n] for n in WEIGHTS], *[delta[n] for n in WEIGHTS],
            *[new_m[n] for n in WEIGHTS], *[new_v[n] for n in WEIGHTS])
```

```python
import functools
import math

import jax
import jax.numpy as jnp
from jax import lax
from jax.experimental import pallas as pl
from jax.experimental.pallas import tpu as pltpu

F32 = jnp.float32
BF16 = jnp.bfloat16
HI = lax.Precision.HIGHEST

D_MODEL = 1024
HEADS = 8
HEAD_DIM = 128
CHUNK = 64
SUB = 16
CONV_K = 4
HALO = 16
DEPTH = 4
EPS = 1e-6
MLP_HIDDEN = 4 * D_MODEL
GDN_MAIN = 4 * D_MODEL
GDN_IN = GDN_MAIN + 2 * HEADS
GDN_IN_PAD = GDN_MAIN + 128
NEG = -1e30

ADAM_LR = 0.001
ADAM_B1 = 0.9
ADAM_B2 = 0.999
ADAM_EPS = 1e-08
ADAM_WD = 0.01
ADAM_STEP = 10

VMEM_LIMIT = 48 * 1024 * 1024

MESH = pl.DeviceIdType.MESH


def _cparams(sem=None, **kw):
    if sem is not None:
        kw["dimension_semantics"] = sem
    return pltpu.CompilerParams(vmem_limit_bytes=VMEM_LIMIT, **kw)


def _iota(shape, dim):
    return lax.broadcasted_iota(jnp.int32, shape, dim)


_DIMS = {"nn": (((1,), (0,)), ((), ())), "nt": (((1,), (1,)), ((), ())), "tn": (((0,), (0,)), ((), ()))}


def _dot(a, b, mode):
    return lax.dot_general(a.astype(BF16), b.astype(BF16), _DIMS[mode], preferred_element_type=F32)


@functools.partial(jax.custom_vjp, nondiff_argnums=(2,))
def _mmx(a, b, mode):
    return _dot(a, b, mode)


def _mmx_fwd(a, b, mode):
    return _dot(a, b, mode), (a, b)


def _mmx_bwd(mode, res, g):
    a, b = res
    if mode == "nn":
        return _dot(g, b, "nt"), _dot(a, g, "tn")
    if mode == "nt":
        return _dot(g, b, "nn"), _dot(g, a, "tn")
    return _dot(b, g, "nt"), _dot(a, g, "nn")


_mmx.defvjp(_mmx_fwd, _mmx_bwd)


def _mm(a, b):
    return _mmx(a, b, "nn")


def _mm_nt(a, b):
    return _mmx(a, b, "nt")


def _mm_tn(a, b):
    return _mmx(a, b, "tn")


@functools.partial(jax.custom_vjp, nondiff_argnums=(1,))
def _roll_rows(x, d):
    return pltpu.roll(x, d, 0)


def _roll_rows_fwd(x, d):
    return pltpu.roll(x, d, 0), None


def _roll_rows_bwd(d, _, g):
    return (pltpu.roll(g, g.shape[0] - d, 0),)


_roll_rows.defvjp(_roll_rows_fwd, _roll_rows_bwd)


def _sig(x):
    return 1.0 / (1.0 + jnp.exp(-x))


@jax.custom_vjp
def _sigmoid(x):
    return _sig(x)


def _sigmoid_fwd(x):
    s = _sig(x)
    return s, s


def _sigmoid_bwd(s, g):
    return (g * s * (1.0 - s),)


_sigmoid.defvjp(_sigmoid_fwd, _sigmoid_bwd)


@jax.custom_vjp
def _silu(x):
    return x * _sig(x)


def _silu_fwd(x):
    s = _sig(x)
    return x * s, (x, s)


def _silu_bwd(res, g):
    x, s = res
    return (g * s * (1.0 + x * (1.0 - s)),)


_silu.defvjp(_silu_fwd, _silu_bwd)


@jax.custom_vjp
def _softplus(x):
    return jnp.maximum(x, 0.0) + jnp.log(1.0 + jnp.exp(-jnp.abs(x)))


def _softplus_fwd(x):
    return _softplus(x), x


def _softplus_bwd(x, g):
    return (g * _sig(x),)


_softplus.defvjp(_softplus_fwd, _softplus_bwd)


@jax.custom_vjp
def _log_sigmoid(x):
    return jnp.minimum(x, 0.0) - jnp.log(1.0 + jnp.exp(-jnp.abs(x)))


def _log_sigmoid_fwd(x):
    return _log_sigmoid(x), x


def _log_sigmoid_bwd(x, g):
    return (g * _sig(-x),)


_log_sigmoid.defvjp(_log_sigmoid_fwd, _log_sigmoid_bwd)


@jax.custom_vjp
def _logaddexp(a, b):
    return jnp.maximum(a, b) + jnp.log(1.0 + jnp.exp(-jnp.abs(a - b)))


def _logaddexp_fwd(a, b):
    return _logaddexp(a, b), (a, b)


def _unbroadcast(g, like):
    for ax in range(g.ndim):
        if like.shape[ax] == 1 and g.shape[ax] != 1:
            g = jnp.sum(g, axis=ax, keepdims=True)
    return g


def _logaddexp_bwd(res, g):
    a, b = res
    s = _sig(a - b)
    return _unbroadcast(g * s, a), _unbroadcast(g * (1.0 - s), b)


_logaddexp.defvjp(_logaddexp_fwd, _logaddexp_bwd)


def _row_to_col(row):
    n = row.shape[1]
    eye = _iota((n, n), 0) == _iota((n, n), 1)
    return jnp.sum(jnp.where(eye, jnp.broadcast_to(row, (n, n)), 0.0), axis=1, keepdims=True)


def _col_to_row(col):
    n = col.shape[0]
    eye = _iota((n, n), 0) == _iota((n, n), 1)
    return jnp.sum(jnp.where(eye, jnp.broadcast_to(col, (n, n)), 0.0), axis=0, keepdims=True)


def _pick_row(x, r):
    return jnp.sum(jnp.where(_iota(x.shape, 0) == r, x, 0.0), axis=0, keepdims=True)


def _pick_lane(x, l):
    return jnp.sum(jnp.where(_iota(x.shape, 1) == l, x, 0.0), axis=1, keepdims=True)


def _each(f, *lists):
    return [f(*t) for t in zip(*lists)]


def _unit_lower_inverse(Ls):
    n = Ls[0].shape[0]
    r, c = _iota((n, n), 0), _iota((n, n), 1)
    eye = jnp.where(r == c, 1.0, 0.0).astype(F32)
    Ld = _each(lambda L: jnp.where((r // SUB) == (c // SUB), L, 0.0), Ls)
    Lo = _each(lambda L, d: L - d, Ls, Ld)
    P = _each(lambda d: eye - d, Ld)
    Lp = Ld
    for _ in range(int(math.log2(SUB)) - 1):
        Lp = _each(lambda x: _mm(x, x), Lp)
        P = _each(lambda p, x: p + _mm(p, x), P, Lp)
    N = _each(_mm, P, Lo)
    N2 = _each(lambda x: _mm(x, x), N)
    X = _each(lambda x, x2: (eye - x) + _mm(eye - x, x2), N, N2)
    return _each(_mm, X, P)


def _shift_rows(x, halo, d):
    if d == 0:
        return x
    xr = _roll_rows(x, d)
    hr = _roll_rows(halo, d)
    hfull = jnp.concatenate([hr, jnp.zeros((x.shape[0] - HALO, x.shape[1]), F32)], axis=0)
    return jnp.where(_iota(x.shape, 0) >= d, xr, hfull)


def _causal_conv_chunk(x, halo, w):
    y = None
    for kk in range(CONV_K):
        t = _shift_rows(x, halo, CONV_K - 1 - kk) * _pick_row(w, kk)
        y = t if y is None else y + t
    return y


def _gdn_chunk(xq, xk, xv, hq, hk, hv, gate, ab, S, cwq, cwk, cwv, alog, dtb, onorm, *, heads, seqs):
    C = xq[0].shape[0]
    q = _each(lambda x, h, w: _silu(_causal_conv_chunk(x, h, w)), xq, hq, cwq)
    k = _each(lambda x, h, w: _silu(_causal_conv_chunk(x, h, w)), xk, hk, cwk)
    v = _each(lambda x, h, w: _silu(_causal_conv_chunk(x, h, w)), xv, hv, cwv)
    q = _each(lambda t: t * lax.rsqrt(jnp.sum(t * t, axis=1, keepdims=True) + EPS) * (HEAD_DIM ** -0.5), q)
    k = _each(lambda t: t * lax.rsqrt(jnp.sum(t * t, axis=1, keepdims=True) + EPS), k)
    beta_all = _each(_sigmoid, ab)
    g_all = _each(lambda t: -jnp.exp(alog) * _softplus(t + dtb), ab)
    beta = [_pick_lane(beta_all[b], HEADS + h) for b, h in zip(seqs, heads)]
    g = [_pick_lane(g_all[b], h) for b, h in zip(seqs, heads)]
    r, c = _iota((C, C), 0), _iota((C, C), 1)
    gc = _each(lambda t: jnp.sum(jnp.where(c <= r, jnp.broadcast_to(_col_to_row(t), (C, C)), 0.0), axis=1,
                                 keepdims=True), g)
    gc_row = _each(lambda t: jnp.sum(jnp.where(r <= c, jnp.broadcast_to(t, (C, C)), 0.0), axis=0, keepdims=True), g)
    decay = _each(lambda a, b: jnp.exp(jnp.where(r >= c, a - b, NEG)), gc, gc_row)
    kb = _each(lambda a, b: a * b, k, beta)
    L = _each(lambda a, b, d: jnp.where(r > c, _mm_nt(a, b) * d, 0.0), kb, k, decay)
    A = _each(lambda a, b, d: jnp.where(r >= c, _mm_nt(a, b) * d, 0.0), q, k, decay)
    T = _unit_lower_inverse(L)
    egc = _each(jnp.exp, gc)
    u = _each(lambda t, a, b: _mm(t, a * b), T, v, beta)
    w = _each(lambda t, a, e: _mm(t, a * e), T, kb, egc)
    gl = _each(lambda t: _pick_row(t, C - 1), gc)
    v_new = _each(lambda a, b, s: a - _mm(b, s), u, w, S)
    o = _each(lambda a, e, s, m, vn: _mm(a * e, s) + _mm(m, vn), q, egc, S, A, v_new)
    S_next = _each(lambda s, l, a, t, vn: s * jnp.exp(l) + _mm_tn(a * jnp.exp(l - t), vn), S, gl, k, gc, v_new)
    o = _each(lambda t, gt: t * lax.rsqrt(jnp.mean(t * t, axis=1, keepdims=True) + EPS) * onorm * _silu(gt), o, gate)
    return o, S_next


def _hgrn_lower_bound(lbl, layer):
    e = jnp.exp(lbl - jnp.max(lbl, axis=0, keepdims=True))
    sm = e / jnp.sum(e, axis=0, keepdims=True)
    r = _iota(lbl.shape, 0)
    return jnp.sum(jnp.where((r >= 1) & (r <= layer), sm, 0.0), axis=0, keepdims=True)


_LEVELS = (1, 2, 4, 8, 16, 32)


def _prefix_matrix(n):
    r, c = _iota((n, n), 0), _iota((n, n), 1)
    parts = [jnp.where(c <= r, 1.0, 0.0)]
    for s in _LEVELS:
        parts.append(jnp.where(c < (r // (2 * s)) * (2 * s) + s, 1.0, 0.0))
    return jnp.concatenate(parts, axis=0).astype(F32)


def _prefix_sums_of(x):
    n = x.shape[0]
    y = lax.dot_general(_prefix_matrix(n), x, _DIMS["nn"], precision=HI, preferred_element_type=F32)
    return tuple(y[t * n:(t + 1) * n] for t in range(len(_LEVELS) + 1))


@jax.custom_vjp
def _prefix_sums(x):
    return _prefix_sums_of(x)


def _prefix_sums_fwd(x):
    return _prefix_sums_of(x), None


def _prefix_sums_bwd(_, gs):
    g = jnp.concatenate(gs, axis=0)
    return (lax.dot_general(_prefix_matrix(gs[0].shape[0]), g, _DIMS["tn"], precision=HI, preferred_element_type=F32),)


_prefix_sums.defvjp(_prefix_sums_fwd, _prefix_sums_bwd)


def _hgrn_chunk(qp, fp, v, S, lbl, *, layer):
    C = qp[0].shape[0]
    lb = _each(lambda l: _hgrn_lower_bound(l, layer), lbl)
    lf = _each(lambda l, f: _logaddexp(jnp.log(l), jnp.log(1.0 - l) + _log_sigmoid(f)), lb, fp)
    k = _each(lambda l, f: (1.0 - l) * _sigmoid(-f), lb, fp)
    q = _each(lambda x: _silu(x) * (HEAD_DIM ** -0.5), qp)
    r, c = _iota((C, C), 0), _iota((C, C), 1)
    row = _iota(qp[0].shape, 0)
    sums = _each(_prefix_sums, lf)
    gc = [t[0] for t in sums]
    a = _each(lambda x, y: jnp.where(r == c, _mm_nt(x, y), 0.0), q, k)
    for n, s in enumerate(_LEVELS):
        ref = [t[n + 1] for t in sums]
        upper = (row % (2 * s)) >= s
        same = (r // (2 * s)) == (c // (2 * s))
        q_s = _each(lambda x, g, m: x * jnp.exp(jnp.where(upper, g - m, NEG)), q, gc, ref)
        k_s = _each(lambda x, g, m: x * jnp.exp(jnp.where(upper, NEG, m - g)), k, gc, ref)
        a = _each(lambda t, x, y: t + jnp.where(same, _mm_nt(x, y), 0.0), a, q_s, k_s)
    o = _each(lambda t, x, g, vv, st: _mm(t, vv) + _mm(x * jnp.exp(g), st), a, q, gc, v, S)
    gl = _each(lambda g: _pick_row(g, C - 1), gc)
    S_next = _each(lambda st, l, x, g, vv: st * _row_to_col(jnp.exp(l)) + _mm_tn(x * jnp.exp(l - g), vv),
                   S, gl, k, gc, v)
    return o, S_next


N_CHIPS = 4
N_DEV = 8
_ANY = pl.BlockSpec(memory_space=pl.ANY)


def _mesh_pos():
    return lax.axis_index("x"), lax.axis_index("y"), lax.axis_index("c")


def _other_chips(x, y):
    ps = [(1 - x, y), (x, 1 - y), (1 - x, 1 - y)]
    return [(p, 2 * p[0] + p[1]) for p in ps]


def _remote(src, dst, send_sem, recv_sem, dev):
    return pltpu.make_async_remote_copy(src_ref=src, dst_ref=dst, send_sem=send_sem, recv_sem=recv_sem,
                                        device_id=dev, device_id_type=MESH)


class Carry:
    def __init__(self, ins, out_shapes, sems, start, finish):
        self.ins, self.out_shapes, self.sems, self.start, self.finish = list(ins), list(out_shapes), list(sems), start, finish


def _pcall(body, *, name, grid, in_specs, out_specs, out_shape, scratch_shapes=(), dims, args, carries=()):
    in_specs, out_specs, out_shape = list(in_specs), list(out_specs), list(out_shape)
    scratch_shapes, args = list(scratch_shapes), list(args)
    n_in, n_out, n_scr = len(in_specs), len(out_shape), len(scratch_shapes)
    carries = [c for c in carries if c is not None]
    if not carries:
        res = pl.pallas_call(body, name=name, grid=grid, in_specs=in_specs, out_specs=out_specs, out_shape=out_shape,
                             scratch_shapes=scratch_shapes, compiler_params=_cparams(dims))(*args)
        return list(res), []
    ci = [len(c.ins) for c in carries]
    co = [len(c.out_shapes) for c in carries]
    cs = [len(c.sems) for c in carries]

    def split(seq, sizes):
        out, k = [], 0
        for s in sizes:
            out.append(seq[k:k + s])
            k += s
        return out

    def carried(*refs):
        ins, cins, outs, couts, scr, sems = split(refs, [n_in, sum(ci), n_out, sum(co), n_scr, sum(cs)])
        cins, couts, sems = split(cins, ci), split(couts, co), split(sems, cs)
        ids = [pl.program_id(a) for a in range(len(grid))]
        first, last = ids[0] == 0, ids[0] == grid[0] - 1
        for a in range(1, len(grid)):
            first, last = first & (ids[a] == 0), last & (ids[a] == grid[a] - 1)

        @pl.when(first)
        def _():
            for c, i, o, s in zip(carries, cins, couts, sems):
                c.start(i, o, s)

        body(*ins, *outs, *scr)

        @pl.when(last)
        def _():
            for c, i, o, s in zip(carries, cins, couts, sems):
                c.finish(i, o, s)

    res = pl.pallas_call(
        carried, name=name, grid=grid,
        in_specs=in_specs + [_ANY] * sum(ci), out_specs=out_specs + [_ANY] * sum(co),
        out_shape=out_shape + [s for c in carries for s in c.out_shapes],
        scratch_shapes=scratch_shapes + [s for c in carries for s in c.sems],
        compiler_params=_cparams(("arbitrary",) * len(grid)),
    )(*args, *[a for c in carries for a in c.ins])
    return list(res[:n_out]), split(list(res[n_out:]), co)


def run_carries(carries, name):
    ci = [len(c.ins) for c in carries]
    co = [len(c.out_shapes) for c in carries]
    cs = [len(c.sems) for c in carries]

    def split(seq, sizes):
        out, k = [], 0
        for s in sizes:
            out.append(seq[k:k + s])
            k += s
        return out

    def body(*refs):
        cins, couts, sems = split(refs, [sum(ci), sum(co), sum(cs)])
        cins, couts, sems = split(cins, ci), split(couts, co), split(sems, cs)
        for c, i, o, s in zip(carries, cins, couts, sems):
            c.start(i, o, s)
        for c, i, o, s in zip(carries, cins, couts, sems):
            c.finish(i, o, s)

    res = pl.pallas_call(
        body, name=name, in_specs=[_ANY] * sum(ci), out_specs=[_ANY] * sum(co),
        out_shape=[s for c in carries for s in c.out_shapes], scratch_shapes=[s for c in carries for s in c.sems],
    )(*[a for c in carries for a in c.ins])
    return split(list(res), co)


def gather_carry(arrs):
    n = len(arrs)
    split = [a.ndim == 2 and a.shape[0] % 32 == 0 for a in arrs]

    def plan(ins, outs, sems):
        send_sems, recv_sems, pass_send, pass_recv, local_sems = sems
        x, y, c = _mesh_pos()
        me = 2 * x + y
        peers = _other_chips(x, y)
        locs = [pltpu.make_async_copy(ins[a], outs[a].at[me], local_sems.at[a]) for a in range(n)]
        sends, recvs, passes, pass_recvs = [], [], [], []
        for a in range(n):
            half = arrs[a].shape[0] // 2
            mine, other = pl.ds(c * half, half), pl.ds((1 - c) * half, half)
            for j, ((px, py), t) in enumerate(peers):
                sem = (send_sems.at[a, j], recv_sems.at[a, j])
                if split[a]:
                    sends.append(_remote(ins[a].at[mine], outs[a].at[me, mine], *sem, (px, py, c)))
                    recvs.append(_remote(ins[a].at[mine], outs[a].at[t, mine], *sem, (px, py, c)))
                    psem = (pass_send.at[a, j], pass_recv.at[a, j])
                    passes.append(_remote(outs[a].at[t, mine], outs[a].at[t, mine], *psem, (x, y, 1 - c)))
                    pass_recvs.append(_remote(outs[a].at[t, other], outs[a].at[t, other], *psem, (x, y, 1 - c)))
                else:
                    sends.append(_remote(ins[a], outs[a].at[me], *sem, (px, py, c)))
                    recvs.append(_remote(ins[a], outs[a].at[t], *sem, (px, py, c)))
                    passes.append(None)
                    pass_recvs.append(None)
        return locs, sends, recvs, passes, pass_recvs

    def start(ins, outs, sems):
        locs, sends, _, _, _ = plan(ins, outs, sems)
        for cp in locs + sends:
            cp.start()

    def finish(ins, outs, sems):
        locs, sends, recvs, passes, pass_recvs = plan(ins, outs, sems)
        for r, p in zip(recvs, passes):
            r.wait_recv()
            if p is not None:
                p.start()
        for p in pass_recvs:
            if p is not None:
                p.wait_recv()
        for r in sends + [p for p in passes if p is not None]:
            r.wait_send()
        for cp in locs:
            cp.wait()

    return Carry(arrs, [jax.ShapeDtypeStruct((N_CHIPS,) + a.shape, a.dtype) for a in arrs],
                 [pltpu.SemaphoreType.DMA((n, 3)), pltpu.SemaphoreType.DMA((n, 3)), pltpu.SemaphoreType.DMA((n, 3)),
                  pltpu.SemaphoreType.DMA((n, 3)), pltpu.SemaphoreType.DMA((n,))], start, finish)


def sibling_half_carry(gs):
    n = len(gs)

    def copies(ins, outs, sems):
        send_sems, recv_sems = sems
        x, y, c = _mesh_pos()
        out = []
        for a in range(n):
            half = gs[a].shape[1] // 2
            out.append(_remote(ins[a].at[:, pl.ds((1 - c) * half, half), :], outs[a], send_sems.at[a], recv_sems.at[a],
                               (x, y, 1 - c)))
        return out

    def start(ins, outs, sems):
        for r in copies(ins, outs, sems):
            r.start()

    def finish(ins, outs, sems):
        cps = copies(ins, outs, sems)
        for r in cps:
            r.wait_recv()
        for r in cps:
            r.wait_send()

    return Carry(gs, [jax.ShapeDtypeStruct((g.shape[0], g.shape[1] // 2, g.shape[2]), g.dtype) for g in gs],
                 [pltpu.SemaphoreType.DMA((n,)), pltpu.SemaphoreType.DMA((n,))], start, finish)


def chips_carry(ps):
    n = len(ps)

    def copies(ins, outs, sems):
        send_sems, recv_sems, local_sems = sems
        x, y, c = _mesh_pos()
        me = 2 * x + y
        peers = _other_chips(x, y)
        locs = [pltpu.make_async_copy(ins[a].at[me], outs[a].at[me], local_sems.at[a]) for a in range(n)]
        sends = [_remote(ins[a].at[t], outs[a].at[me], send_sems.at[a, j], recv_sems.at[a, j], (px, py, c))
                 for a in range(n) for j, ((px, py), t) in enumerate(peers)]
        recvs = [_remote(ins[a].at[t], outs[a].at[t], send_sems.at[a, j], recv_sems.at[a, j], (px, py, c))
                 for a in range(n) for j, ((px, py), t) in enumerate(peers)]
        return locs, sends, recvs

    def start(ins, outs, sems):
        locs, sends, _ = copies(ins, outs, sems)
        for cp in locs + sends:
            cp.start()

    def finish(ins, outs, sems):
        locs, sends, recvs = copies(ins, outs, sems)
        for r in recvs:
            r.wait_recv()
        for r in sends:
            r.wait_send()
        for cp in locs:
            cp.wait()

    return Carry(ps, [jax.ShapeDtypeStruct(p.shape, p.dtype) for p in ps],
                 [pltpu.SemaphoreType.DMA((n, 3)), pltpu.SemaphoreType.DMA((n, 3)), pltpu.SemaphoreType.DMA((n,))],
                 start, finish)


def sibling_copy_carry(arrs):
    n = len(arrs)

    def copies(ins, outs, sems):
        send_sems, recv_sems = sems
        x, y, c = _mesh_pos()
        return [_remote(ins[a], outs[a], send_sems.at[a], recv_sems.at[a], (x, y, 1 - c)) for a in range(n)]

    def start(ins, outs, sems):
        for r in copies(ins, outs, sems):
            r.start()

    def finish(ins, outs, sems):
        cps = copies(ins, outs, sems)
        for r in cps:
            r.wait_recv()
        for r in cps:
            r.wait_send()

    return Carry(arrs, [jax.ShapeDtypeStruct(a.shape, a.dtype) for a in arrs],
                 [pltpu.SemaphoreType.DMA((n,)), pltpu.SemaphoreType.DMA((n,))], start, finish)


def gather_all_carry(small):
    flips = [(fx, fy, fc) for fx in (0, 1) for fy in (0, 1) for fc in (0, 1)][1:]

    def copies(ins, outs, sems):
        send_sems, recv_sems, local_sem = sems
        in_ref, out_ref = ins[0], outs[0]
        x, y, c = _mesh_pos()
        me = 4 * x + 2 * y + c
        peers = [((1 - x) if fx else x, (1 - y) if fy else y, (1 - c) if fc else c) for fx, fy, fc in flips]
        loc = pltpu.make_async_copy(in_ref, out_ref.at[me], local_sem.at[0])
        sends = [_remote(in_ref, out_ref.at[me], send_sems.at[k], recv_sems.at[k], p) for k, p in enumerate(peers)]
        recvs = [_remote(in_ref, out_ref.at[4 * p[0] + 2 * p[1] + p[2]], send_sems.at[k], recv_sems.at[k], p)
                 for k, p in enumerate(peers)]
        return loc, sends, recvs

    def start(ins, outs, sems):
        loc, sends, _ = copies(ins, outs, sems)
        loc.start()
        for r in sends:
            r.start()

    def finish(ins, outs, sems):
        loc, sends, recvs = copies(ins, outs, sems)
        for r in recvs:
            r.wait_recv()
        for r in sends:
            r.wait_send()
        loc.wait()

    return Carry([small], [jax.ShapeDtypeStruct((N_DEV,) + small.shape, small.dtype)],
                 [pltpu.SemaphoreType.DMA((N_DEV - 1,)), pltpu.SemaphoreType.DMA((N_DEV - 1,)),
                  pltpu.SemaphoreType.DMA((1,))], start, finish)


def _chunk_spec(bsz, nc, col, rev=False, rows=CHUNK, width=D_MODEL):
    return pl.BlockSpec((bsz, rows, width), lambda n: (0, (nc - 1 - n) if rev else n, col))


def _halo_spec(bsz, nc, col, rev):
    per = CHUNK // HALO
    return pl.BlockSpec((bsz, HALO, D_MODEL),
                        lambda n: (0, jnp.maximum(((nc - 1 - n) if rev else n) * per - 1, 0), col))


def _const_spec(shape):
    nd = len(shape)
    return pl.BlockSpec(shape, lambda n: (0,) * nd)


def _state_spec(bsz, nc, rev=False):
    return pl.BlockSpec((bsz, None, HEADS, HEAD_DIM, HEAD_DIM), lambda n: (0, (nc - 1 - n) if rev else n, 0, 0, 0))


def _lanes(hh):
    return slice(hh * HEAD_DIM, (hh + 1) * HEAD_DIM)


def _chains(bsz):
    return [(b, hh) for b in range(bsz) for hh in range(HEADS)]


def _per_chain(ref, bsz, scale=None):
    vals = [ref[b, :, _lanes(hh)].astype(F32) for b, hh in _chains(bsz)]
    return vals if scale is None else [v * scale for v in vals]


def _per_head(ref, bsz):
    return [ref[:, _lanes(hh)].astype(F32) for _, hh in _chains(bsz)]


def gdn_forward(proj, conv_w, alog, dtb, onorm, bsz, name, carries=()):
    n_tok = proj.shape[0]
    t_len = n_tok // bsz
    nc = t_len // CHUNK
    chains = _chains(bsz)

    def body(xq, xk, xv, hq, hk, hv, gate, ab, cwq, cwk, cwv, al, dt, on, o_ref, ssave_ref, s_ref):
        n = pl.program_id(0)
        keep = jnp.where(n > 0, 1.0, 0.0).astype(F32)

        @pl.when(n == 0)
        def _():
            s_ref[...] = jnp.zeros_like(s_ref)

        S = [s_ref[b, hh] for b, hh in chains]
        for (b, hh), s in zip(chains, S):
            ssave_ref[b, hh] = s
        outs, s_next = _gdn_chunk(
            _per_chain(xq, bsz), _per_chain(xk, bsz), _per_chain(xv, bsz), _per_chain(hq, bsz, keep),
            _per_chain(hk, bsz, keep), _per_chain(hv, bsz, keep), _per_chain(gate, bsz),
            [ab[b].astype(F32) for b in range(bsz)], S, _per_head(cwq, bsz), _per_head(cwk, bsz), _per_head(cwv, bsz),
            al[...], dt[...], on[...], heads=[hh for _, hh in chains], seqs=[b for b, _ in chains])
        for (b, hh), o, s in zip(chains, outs, s_next):
            o_ref[b, :, _lanes(hh)] = o.astype(o_ref.dtype)
            s_ref[b, hh] = s

    cw_spec = lambda col: pl.BlockSpec((CONV_K, D_MODEL), lambda n: (0, col))
    in_specs = [_chunk_spec(bsz, nc, 0), _chunk_spec(bsz, nc, 1), _chunk_spec(bsz, nc, 2),
                _halo_spec(bsz, nc, 0, False), _halo_spec(bsz, nc, 1, False), _halo_spec(bsz, nc, 2, False),
                _chunk_spec(bsz, nc, 3), _chunk_spec(bsz, nc, GDN_MAIN // 128, width=128),
                cw_spec(0), cw_spec(1), cw_spec(2),
                _const_spec((1, 128)), _const_spec((1, 128)), _const_spec((1, 128))]
    out_specs = [_chunk_spec(bsz, nc, 0), _state_spec(bsz, nc)]
    p3 = proj.reshape(bsz, t_len, -1)
    (o, ssave), got = _pcall(
        body, name=name, grid=(nc,), in_specs=in_specs, out_specs=out_specs,
        out_shape=(jax.ShapeDtypeStruct((bsz, t_len, D_MODEL), BF16),
                   jax.ShapeDtypeStruct((bsz, nc, HEADS, HEAD_DIM, HEAD_DIM), F32)),
        scratch_shapes=[pltpu.VMEM((bsz, HEADS, HEAD_DIM, HEAD_DIM), F32)], dims=("arbitrary",),
        args=(p3, p3, p3, p3, p3, p3, p3, p3, conv_w, conv_w, conv_w, alog, dtb, onorm), carries=carries)
    return (o.reshape(n_tok, D_MODEL), ssave), got


def gdn_backward(proj, conv_w, alog, dtb, onorm, s_saved, d_out, bsz, name, carries=()):
    n_tok = proj.shape[0]
    t_len = n_tok // bsz
    nc = t_len // CHUNK
    chains = _chains(bsz)

    def body(xq, xk, xv, hq, hk, hv, gate, ab, cwq, cwk, cwv, al, dt, on, ssave, do,
             dp_ref, dcw_ref, dal_ref, ddt_ref, don_ref, ds_ref, dhalo_ref):
        n = pl.program_id(0)
        nr = nc - 1 - n

        @pl.when(n == 0)
        def _():
            dcw_ref[...] = jnp.zeros_like(dcw_ref)
            dal_ref[...] = jnp.zeros_like(dal_ref)
            ddt_ref[...] = jnp.zeros_like(ddt_ref)
            don_ref[...] = jnp.zeros_like(don_ref)
            ds_ref[...] = jnp.zeros_like(ds_ref)
            dhalo_ref[...] = jnp.zeros_like(dhalo_ref)

        keep = jnp.where(nr > 0, 1.0, 0.0).astype(F32)
        pad = jnp.zeros((CHUNK - HALO, HEAD_DIM), F32)
        args = (_per_chain(xq, bsz), _per_chain(xk, bsz), _per_chain(xv, bsz), _per_chain(hq, bsz, keep),
                _per_chain(hk, bsz, keep), _per_chain(hv, bsz, keep), _per_chain(gate, bsz),
                [ab[b].astype(F32) for b in range(bsz)], [ssave[b, hh] for b, hh in chains],
                _per_head(cwq, bsz), _per_head(cwk, bsz), _per_head(cwv, bsz), al[...], dt[...], on[...])
        _, vjp = jax.vjp(functools.partial(_gdn_chunk, heads=[hh for _, hh in chains], seqs=[b for b, _ in chains]),
                         *args)
        (gxq, gxk, gxv, ghq, ghk, ghv, ggate, gab, gS, gcq, gck, gcv, gal, gdt, gon) = vjp(
            (_per_chain(do, bsz), [ds_ref[b, hh] for b, hh in chains]))
        d = D_MODEL
        for e, (b, hh) in enumerate(chains):
            c0 = hh * HEAD_DIM
            for part, (gx, gh, gc) in enumerate(((gxq, ghq, gcq), (gxk, ghk, gck), (gxv, ghv, gcv))):
                full = gx[e] + jnp.concatenate([pad, dhalo_ref[b, hh, part]], axis=0)
                dp_ref[b, :, part * d + c0:part * d + c0 + HEAD_DIM] = full.astype(dp_ref.dtype)
                dhalo_ref[b, hh, part] = gh[e] * keep
                dcw_ref[hh, part] += gc[e]
            dp_ref[b, :, 3 * d + c0:3 * d + c0 + HEAD_DIM] = ggate[e].astype(dp_ref.dtype)
            ds_ref[b, hh] = gS[e]
        for b in range(bsz):
            dp_ref[b, :, GDN_MAIN:GDN_IN_PAD] = gab[b].astype(dp_ref.dtype)
        dal_ref[...] += gal
        ddt_ref[...] += gdt
        don_ref[...] += gon

    cw_spec = lambda col: pl.BlockSpec((CONV_K, D_MODEL), lambda n: (0, col))
    in_specs = [_chunk_spec(bsz, nc, 0, True), _chunk_spec(bsz, nc, 1, True), _chunk_spec(bsz, nc, 2, True),
                _halo_spec(bsz, nc, 0, True), _halo_spec(bsz, nc, 1, True), _halo_spec(bsz, nc, 2, True),
                _chunk_spec(bsz, nc, 3, True), _chunk_spec(bsz, nc, GDN_MAIN // 128, True, width=128),
                cw_spec(0), cw_spec(1), cw_spec(2),
                _const_spec((1, 128)), _const_spec((1, 128)), _const_spec((1, 128)),
                _state_spec(bsz, nc, True), _chunk_spec(bsz, nc, 0, True)]
    out_specs = [_chunk_spec(bsz, nc, 0, True, width=GDN_IN_PAD),
                 _const_spec((HEADS, 3, CONV_K, HEAD_DIM)), _const_spec((1, 128)), _const_spec((1, 128)),
                 _const_spec((1, 128))]
    row = jax.ShapeDtypeStruct((1, 128), F32)
    p3 = proj.reshape(bsz, t_len, -1)
    (dp, dcw, dal, ddt, don), got = _pcall(
        body, name=name, grid=(nc,), in_specs=in_specs, out_specs=out_specs,
        out_shape=(jax.ShapeDtypeStruct((bsz, t_len, GDN_IN_PAD), BF16),
                   jax.ShapeDtypeStruct((HEADS, 3, CONV_K, HEAD_DIM), F32), row, row, row),
        scratch_shapes=[pltpu.VMEM((bsz, HEADS, HEAD_DIM, HEAD_DIM), F32),
                        pltpu.VMEM((bsz, HEADS, 3, HALO, HEAD_DIM), F32)],
        dims=("arbitrary",),
        args=(p3, p3, p3, p3, p3, p3, p3, p3, conv_w, conv_w, conv_w, alog, dtb, onorm, s_saved,
              d_out.reshape(bsz, t_len, D_MODEL)),
        carries=carries)
    return (dp.reshape(n_tok, GDN_IN_PAD), dcw, dal, ddt, don), got


def hgrn_forward(proj, lbl, layer, bsz, carries=()):
    n_tok = proj.shape[0]
    t_len = n_tok // bsz
    nc = t_len // CHUNK
    chains = _chains(bsz)

    def body(qp, fp, vi, lb, o_ref, ssave_ref, s_ref):
        @pl.when(pl.program_id(0) == 0)
        def _():
            s_ref[...] = jnp.zeros_like(s_ref)

        S = [s_ref[b, hh] for b, hh in chains]
        for (b, hh), s in zip(chains, S):
            ssave_ref[b, hh] = s
        outs, s_next = _hgrn_chunk(_per_chain(qp, bsz), _per_chain(fp, bsz), _per_chain(vi, bsz), S,
                                   _per_head(lb, bsz), layer=layer)
        for (b, hh), o, s in zip(chains, outs, s_next):
            o_ref[b, :, _lanes(hh)] = o
            s_ref[b, hh] = s

    in_specs = [_chunk_spec(bsz, nc, 0), _chunk_spec(bsz, nc, 1), _chunk_spec(bsz, nc, 2),
                _const_spec((DEPTH, D_MODEL))]
    out_specs = [_chunk_spec(bsz, nc, 0), _state_spec(bsz, nc)]
    p3 = proj.reshape(bsz, t_len, -1)
    (o, ssave), got = _pcall(
        body, name=f"hgrn_fwd{layer}", grid=(nc,), in_specs=in_specs, out_specs=out_specs,
        out_shape=(jax.ShapeDtypeStruct((bsz, t_len, D_MODEL), F32),
                   jax.ShapeDtypeStruct((bsz, nc, HEADS, HEAD_DIM, HEAD_DIM), F32)),
        scratch_shapes=[pltpu.VMEM((bsz, HEADS, HEAD_DIM, HEAD_DIM), F32)], dims=("arbitrary",),
        args=(p3, p3, p3, lbl), carries=carries)
    return (o.reshape(n_tok, D_MODEL), ssave), got


def hgrn_backward(proj, lbl, s_saved, d_o, d_gate, layer, bsz, carries=()):
    n_tok = proj.shape[0]
    t_len = n_tok // bsz
    nc = t_len // CHUNK
    chains = _chains(bsz)

    def body(qp, fp, vi, lb, ssave, do, dgt, dp_ref, dlb_ref, ds_ref):
        @pl.when(pl.program_id(0) == 0)
        def _():
            dlb_ref[...] = jnp.zeros_like(dlb_ref)
            ds_ref[...] = jnp.zeros_like(ds_ref)

        _, vjp = jax.vjp(functools.partial(_hgrn_chunk, layer=layer), _per_chain(qp, bsz), _per_chain(fp, bsz),
                         _per_chain(vi, bsz), [ssave[b, hh] for b, hh in chains], _per_head(lb, bsz))
        gq, gf, gv, gS, glb = vjp((_per_chain(do, bsz), [ds_ref[b, hh] for b, hh in chains]))
        d = D_MODEL
        for e, (b, hh) in enumerate(chains):
            c0 = hh * HEAD_DIM
            dp_ref[b, :, c0:c0 + HEAD_DIM] = gq[e].astype(dp_ref.dtype)
            dp_ref[b, :, d + c0:d + c0 + HEAD_DIM] = gf[e].astype(dp_ref.dtype)
            dp_ref[b, :, 2 * d + c0:2 * d + c0 + HEAD_DIM] = gv[e].astype(dp_ref.dtype)
            dlb_ref[hh] += glb[e]
            ds_ref[b, hh] = gS[e]
        dp_ref[:, :, 3 * d:4 * d] = dgt[...]

    in_specs = [_chunk_spec(bsz, nc, 0, True), _chunk_spec(bsz, nc, 1, True), _chunk_spec(bsz, nc, 2, True),
                _const_spec((DEPTH, D_MODEL)), _state_spec(bsz, nc, True),
                _chunk_spec(bsz, nc, 0, True), _chunk_spec(bsz, nc, 0, True)]
    out_specs = [_chunk_spec(bsz, nc, 0, True, width=4 * D_MODEL), _const_spec((HEADS, DEPTH, HEAD_DIM))]
    p3 = proj.reshape(bsz, t_len, -1)
    (dp, dlb), got = _pcall(
        body, name=f"hgrn_bwd{layer}", grid=(nc,), in_specs=in_specs, out_specs=out_specs,
        out_shape=(jax.ShapeDtypeStruct((bsz, t_len, 4 * D_MODEL), BF16),
                   jax.ShapeDtypeStruct((HEADS, DEPTH, HEAD_DIM), F32)),
        scratch_shapes=[pltpu.VMEM((bsz, HEADS, HEAD_DIM, HEAD_DIM), F32)], dims=("arbitrary",),
        args=(p3, p3, p3, lbl, s_saved, d_o.reshape(bsz, t_len, D_MODEL), d_gate.reshape(bsz, t_len, D_MODEL)),
        carries=carries)
    return (dp.reshape(n_tok, 4 * D_MODEL), dlb), got


ROW_TILE = 512
MM_VMEM_BUDGET = 36 * 1024 * 1024


def _tile(n):
    for cand in (1024, 512, 1408, 384, 256, 128):
        if n % cand == 0:
            return cand
    return n


def _rmsnorm(x, w):
    return x * lax.rsqrt(jnp.mean(x * x, axis=1, keepdims=True) + EPS) * w


def norm_matmul(h, nw, w, relu, name, carries=()):
    n_tok, d = h.shape
    slots = w.ndim == 3
    n_out = w.shape[0] * w.shape[2] if slots else w.shape[1]
    tm, tn = min(n_tok, 2 * ROW_TILE), _tile(w.shape[2] if slots else n_out)
    if slots:
        per = w.shape[2] // tn
        w_spec = pl.BlockSpec((None, d, tn), lambda i, j: (j // per, 0, j % per))
    else:
        w_spec = pl.BlockSpec((d, tn), lambda i, j: (0, j))

    def body(h_ref, nw_ref, w_ref, y_ref, *outs):
        @pl.when(pl.program_id(1) == 0)
        def _():
            y_ref[...] = _rmsnorm(h_ref[...], nw_ref[...]).astype(BF16)

        acc = jnp.dot(y_ref[...], w_ref[...], preferred_element_type=F32)
        outs[0][...] = (jnp.maximum(acc, 0.0) if relu else acc).astype(BF16)

    o_spec = pl.BlockSpec((tm, tn), lambda i, j: (i, j))
    outs = (jax.ShapeDtypeStruct((n_tok, n_out), BF16),)
    return _pcall(
        body, name=name, grid=(n_tok // tm, n_out // tn),
        in_specs=[pl.BlockSpec((tm, d), lambda i, j: (i, 0)), pl.BlockSpec((1, d), lambda i, j: (0, 0)), w_spec],
        out_specs=[pl.BlockSpec((tm, d), lambda i, j: (i, 0))] + [o_spec] * len(outs),
        out_shape=(jax.ShapeDtypeStruct((n_tok, d), BF16),) + outs,
        dims=("parallel", "arbitrary"), args=(h, nw, w), carries=carries)


def _mm_tiles(m, n, k, extra_bytes):
    tn = _tile(n)
    for tm in (1024, 512, 256, 128):
        if m % tm == 0 and 2 * (2 * tm * k + 2 * k * tn + (4 + extra_bytes) * tm * tn) <= MM_VMEM_BUDGET:
            return tm, tn
    return min(m, 128), tn


def matmul(a, b, mode, name, out_dtype=F32, extra=None, epilogue=None, shards=1, carries=(), square_a=False):
    slots = b.shape[0] if (mode == "nt" and b.ndim == 3) else 0
    if mode == "nn":
        (m, k), n = a.shape, b.shape[1]
    elif mode == "nt":
        (m, k), n = a.shape, (b.shape[1] if slots else b.shape[0])
    else:
        (k, m), n = a.shape, b.shape[1]
    tm, tn = _mm_tiles(m, n // shards, k, 0 if extra is None else extra.dtype.itemsize)
    a_spec = pl.BlockSpec((k, tm), lambda i, j: (0, i)) if mode == "tn" else pl.BlockSpec((tm, k), lambda i, j: (i, 0))
    if slots:
        kb = b.shape[2]
        b_specs = [pl.BlockSpec((None, tn, kb), lambda i, j, s=s: (s, j, 0)) for s in range(slots)]
    elif mode == "nt":
        b_specs = [pl.BlockSpec((tn, k), lambda i, j: (j, 0))]
    else:
        b_specs = [pl.BlockSpec((k, tn), lambda i, j: (0, j))]
    nb = len(b_specs)
    dims = _DIMS[mode]

    def body(*refs):
        a_ref, b_refs = refs[0], refs[1:1 + nb]
        e_ref = refs[1 + nb] if extra is not None else None
        o_ref = refs[-1]
        if slots:
            acc = None
            for s in range(slots):
                part = lax.dot_general(a_ref[:, s * kb:(s + 1) * kb].astype(BF16), b_refs[s][...].astype(BF16), dims,
                                       preferred_element_type=F32)
                acc = part if acc is None else acc + part
        else:
            av = a_ref[...].astype(BF16)
            if square_a:
                av = av * av
            acc = lax.dot_general(av, b_refs[0][...].astype(BF16), dims, preferred_element_type=F32)
        if epilogue == "add":
            acc = e_ref[...] + acc
        elif epilogue == "mul2":
            acc = acc * (2.0 * e_ref[...].astype(F32))
        o_ref[...] = acc.astype(o_ref.dtype)

    in_specs = [a_spec] + b_specs
    args = [a] + [b] * nb
    if extra is not None:
        in_specs.append(pl.BlockSpec((tm, tn), lambda i, j: (i, j)))
        args.append(extra)
    if shards > 1:
        per = n // shards // tn
        out_spec = pl.BlockSpec((None, tm, tn), lambda i, j: (j // per, i, j % per))
        out_shape = jax.ShapeDtypeStruct((shards, m, n // shards), out_dtype)
    else:
        out_spec = pl.BlockSpec((tm, tn), lambda i, j: (i, j))
        out_shape = jax.ShapeDtypeStruct((m, n), out_dtype)
    (res,), cres = _pcall(
        body, name=name, grid=(m // tm, n // tn), in_specs=in_specs, out_specs=[out_spec], out_shape=[out_shape],
        dims=("parallel", "arbitrary"), args=args, carries=carries)
    return res, cres


def norm_backward(h, nw, dy, dres, name, carries=()):
    n_tok, d = h.shape
    tm = min(n_tok, ROW_TILE)

    def body(h_ref, nw_ref, dy_ref, dr_ref, dh_ref, dhb_ref, dnw_ref):
        @pl.when(pl.program_id(0) == 0)
        def _():
            dnw_ref[...] = jnp.zeros_like(dnw_ref)

        _, vjp = jax.vjp(_rmsnorm, h_ref[...], nw_ref[...])
        gh, gw = vjp(dy_ref[...])
        dh = dr_ref[...] + gh
        dh_ref[...] = dh
        dhb_ref[...] = dh.astype(BF16)
        dnw_ref[...] += gw

    row = pl.BlockSpec((tm, d), lambda i: (i, 0))
    vec = pl.BlockSpec((1, d), lambda i: (0, 0))
    return _pcall(
        body, name=name, grid=(n_tok // tm,), in_specs=[row, vec, row, row], out_specs=[row, row, vec],
        out_shape=(jax.ShapeDtypeStruct((n_tok, d), F32), jax.ShapeDtypeStruct((n_tok, d), BF16),
                   jax.ShapeDtypeStruct((1, d), F32)),
        dims=("arbitrary",), args=(h, nw, dy, dres), carries=carries)


def _hgrn_post(o, gate, gw):
    return _rmsnorm(o, gw) * _silu(gate)


def hgrn_post_forward(o, proj, gw, name):
    n_tok, d = o.shape
    tm = min(n_tok, ROW_TILE)

    def body(o_ref, g_ref, w_ref, y_ref):
        y_ref[...] = _hgrn_post(o_ref[...], g_ref[...].astype(F32), w_ref[...]).astype(BF16)

    row = pl.BlockSpec((tm, d), lambda i: (i, 0))
    return pl.pallas_call(
        body, name=name, grid=(n_tok // tm,),
        in_specs=[row, pl.BlockSpec((tm, d), lambda i: (i, 3)), pl.BlockSpec((1, d), lambda i: (0, 0))],
        out_specs=row, out_shape=jax.ShapeDtypeStruct((n_tok, d), BF16),
        compiler_params=_cparams(("parallel",)),
    )(o, proj, gw)


def hgrn_post_backward(o, proj, gw, dy, name):
    n_tok, d = o.shape
    tm = min(n_tok, ROW_TILE)

    def body(o_ref, g_ref, w_ref, dy_ref, do_ref, dg_ref, dw_ref):
        @pl.when(pl.program_id(0) == 0)
        def _():
            dw_ref[...] = jnp.zeros_like(dw_ref)

        _, vjp = jax.vjp(_hgrn_post, o_ref[...], g_ref[...].astype(F32), w_ref[...])
        go, gg, gw_ = vjp(dy_ref[...])
        do_ref[...] = go
        dg_ref[...] = gg.astype(BF16)
        dw_ref[...] += gw_

    row = pl.BlockSpec((tm, d), lambda i: (i, 0))
    vec = pl.BlockSpec((1, d), lambda i: (0, 0))
    return pl.pallas_call(
        body, name=name, grid=(n_tok // tm,),
        in_specs=[row, pl.BlockSpec((tm, d), lambda i: (i, 3)), vec, row], out_specs=[row, row, vec],
        out_shape=(jax.ShapeDtypeStruct((n_tok, d), F32), jax.ShapeDtypeStruct((n_tok, d), BF16),
                   jax.ShapeDtypeStruct((1, d), F32)),
        compiler_params=_cparams(("arbitrary",)),
    )(o, proj, gw, dy)


def loss_head(h, nw, target):
    n_tok, d = h.shape
    tm = min(n_tok, ROW_TILE)

    def body(h_ref, nw_ref, t_ref, loss_ref, dh_ref, dhb_ref, dnw_ref):
        @pl.when(pl.program_id(0) == 0)
        def _():
            dnw_ref[...] = jnp.zeros_like(dnw_ref)
            loss_ref[...] = jnp.zeros_like(loss_ref)

        out, vjp = jax.vjp(_rmsnorm, h_ref[...], nw_ref[...])
        err = out - t_ref[...]
        part = 0.5 * jnp.sum(jnp.sum(err * err, axis=1, keepdims=True), axis=0, keepdims=True) / d
        loss_ref[...] += jnp.broadcast_to(part, loss_ref.shape)
        gh, gw = vjp(err / d)
        dh_ref[...] = gh
        dhb_ref[...] = gh.astype(BF16)
        dnw_ref[...] += gw

    row = pl.BlockSpec((tm, d), lambda i: (i, 0))
    vec = pl.BlockSpec((1, d), lambda i: (0, 0))
    return pl.pallas_call(
        body, name="loss_head", grid=(n_tok // tm,), in_specs=[row, vec, row],
        out_specs=[pl.BlockSpec((1, 128), lambda i: (0, 0)), row, row, vec],
        out_shape=(jax.ShapeDtypeStruct((1, 128), F32), jax.ShapeDtypeStruct((n_tok, d), F32),
                   jax.ShapeDtypeStruct((n_tok, d), BF16), jax.ShapeDtypeStruct((1, d), F32)),
        compiler_params=_cparams(("arbitrary",)),
    )(h, nw, target)


def _rows2d(shape):
    if len(shape) == 1:
        return (1, shape[0])
    return (math.prod(shape[:-1]), shape[-1])


def adamw(w, g, m, v, name):
    shape = w.shape
    r, c = _rows2d(shape)
    tr = r if r <= 256 else 256
    c1 = 1.0 / (1.0 - ADAM_B1 ** ADAM_STEP)
    c2 = 1.0 / (1.0 - ADAM_B2 ** ADAM_STEP)

    def body(w_ref, g_ref, m_ref, v_ref, d_ref, nm_ref, nv_ref):
        gg = g_ref[...]
        nm = ADAM_B1 * m_ref[...] + (1.0 - ADAM_B1) * gg
        nv = ADAM_B2 * v_ref[...] + (1.0 - ADAM_B2) * (gg * gg)
        d_ref[...] = -ADAM_LR * ((nm * c1) / (jnp.sqrt(nv * c2) + ADAM_EPS) + ADAM_WD * w_ref[...])
        nm_ref[...] = nm
        nv_ref[...] = nv

    spec = pl.BlockSpec((tr, c), lambda i: (i, 0))
    sds = jax.ShapeDtypeStruct((r, c), F32)
    outs = pl.pallas_call(
        body, name=name, grid=(r // tr,), in_specs=[spec] * 4, out_specs=[spec] * 3, out_shape=(sds,) * 3,
        compiler_params=_cparams(("parallel",)),
    )(w.reshape(r, c), g.reshape(r, c), m.reshape(r, c), v.reshape(r, c))
    return tuple(o.reshape(shape) for o in outs)


def add_slots(parts, name):
    s, r, c = parts.shape
    tr = r if r <= 256 else 256

    def body(p_ref, o_ref):
        acc = p_ref[0]
        for t in range(1, s):
            acc = acc + p_ref[t]
        o_ref[...] = acc

    return pl.pallas_call(
        body, name=name, grid=(r // tr,), in_specs=[pl.BlockSpec((s, tr, c), lambda i: (0, i, 0))],
        out_specs=pl.BlockSpec((tr, c), lambda i: (i, 0)), out_shape=jax.ShapeDtypeStruct((r, c), F32),
        compiler_params=_cparams(("parallel",)),
    )(parts)


BIG = ("gdn_w_in", "gdn_w_out", "hgrn_w_in", "hgrn_w_out", "mlp_w_up", "mlp_w_down")
WEIGHTS = ("gdn_w_in", "gdn_conv", "gdn_a_log", "gdn_dt_bias", "gdn_onorm", "gdn_w_out", "hgrn_w_in", "hgrn_lb_logits",
           "hgrn_gnorm", "hgrn_w_out", "norm_mix", "norm_mlp", "mlp_w_up", "mlp_w_down", "norm_final")


def _pad_lanes(v, n):
    return jnp.pad(v, [(0, 0)] * (v.ndim - 1) + [(0, n - v.shape[-1])])


_SMALL_LAYOUT = {
    "norm_mix": (0, 4, D_MODEL), "norm_mlp": (8, 4, D_MODEL), "norm_final": (16, 1, D_MODEL),
    "hgrn_lb_logits": (24, 4, D_MODEL), "gdn_onorm": (32, 2, 128), "gdn_a_log": (40, 2, HEADS),
    "gdn_dt_bias": (48, 2, HEADS), "loss": (56, 1, 128), "gdn_conv": (64, 24, D_MODEL), "hgrn_gnorm": (88, 2, D_MODEL),
}


def _pack_small(small, loss_row):
    rows = []
    for name, (first, nrow, lanes) in _SMALL_LAYOUT.items():
        v = loss_row if name == "loss" else small[name]
        v = _pad_lanes(v.reshape(nrow, -1), D_MODEL)
        rows.append(jnp.pad(v, ((0, -nrow % 8), (0, 0))))
    return jnp.concatenate(rows, axis=0)


def _unpack_small(packed, name, shape):
    first, nrow, lanes = _SMALL_LAYOUT[name]
    return packed[first:first + nrow, :lanes].reshape(shape)


def add_core_halves(g, theirs, core, name):
    s, r, c = g.shape
    r2 = r // 2
    tr = min(r2, 256)
    nb = r2 // tr

    def body(core_ref, g_ref, t_ref, o_ref):
        o_ref[...] = g_ref[...] + t_ref[...]

    grid_spec = pltpu.PrefetchScalarGridSpec(
        num_scalar_prefetch=1, grid=(s, nb),
        in_specs=[pl.BlockSpec((None, tr, c), lambda t, i, cr: (t, cr[0] * nb + i, 0)),
                  pl.BlockSpec((None, tr, c), lambda t, i, cr: (t, i, 0))],
        out_specs=pl.BlockSpec((None, tr, c), lambda t, i, cr: (t, i, 0)))
    return pl.pallas_call(body, name=name, grid_spec=grid_spec, out_shape=jax.ShapeDtypeStruct((s, r2, c), F32),
                          compiler_params=_cparams(("parallel", "parallel")))(core, g, theirs)


def add_chip_slots(slots, name):
    n_l = len(slots)
    s, r2, c = slots[0].shape
    tr = min(r2, 256)
    nb = r2 // tr

    def body(*refs):
        ins, o_ref = refs[:n_l], refs[n_l]
        for k in range(n_l):
            @pl.when(pl.program_id(0) == k)
            def _(k=k):
                acc = ins[k][0]
                for t in range(1, s):
                    acc = acc + ins[k][t]
                o_ref[...] = acc

    in_specs = [pl.BlockSpec((s, tr, c), lambda l, i, k=k: (0, jnp.where(l == k, i, 0), 0)) for k in range(n_l)]
    return pl.pallas_call(
        body, name=name, grid=(n_l, nb), in_specs=in_specs, out_specs=pl.BlockSpec((None, tr, c), lambda l, i: (l, i, 0)),
        out_shape=jax.ShapeDtypeStruct((n_l, r2, c), F32), compiler_params=_cparams(("arbitrary", "arbitrary")),
    )(*slots)


def adamw_halves(w, m, v, mine, theirs, name):
    n_l, r, c = w.shape
    r2 = r // 2
    tr = min(r2, 256)
    nb = r2 // tr
    c1 = 1.0 / (1.0 - ADAM_B1 ** ADAM_STEP)
    c2 = 1.0 / (1.0 - ADAM_B2 ** ADAM_STEP)

    def body(w_ref, m_ref, v_ref, mine_ref, theirs_ref, g_ref, d_ref, nm_ref, nv_ref):
        my_half = (pl.program_id(1) // nb) == lax.axis_index("c")
        gg = jnp.where(my_half, mine_ref[...], theirs_ref[...])
        nm = ADAM_B1 * m_ref[...] + (1.0 - ADAM_B1) * gg
        nv = ADAM_B2 * v_ref[...] + (1.0 - ADAM_B2) * (gg * gg)
        g_ref[...] = gg
        d_ref[...] = -ADAM_LR * ((nm * c1) / (jnp.sqrt(nv * c2) + ADAM_EPS) + ADAM_WD * w_ref[...])
        nm_ref[...] = nm
        nv_ref[...] = nv

    full = pl.BlockSpec((None, tr, c), lambda l, i: (l, i, 0))
    half = pl.BlockSpec((None, tr, c), lambda l, i: (l, i % nb, 0))
    sds = jax.ShapeDtypeStruct((n_l, r, c), F32)
    return pl.pallas_call(
        body, name=name, grid=(n_l, r // tr), in_specs=[full, full, full, half, half], out_specs=[full] * 4,
        out_shape=(sds,) * 4, compiler_params=_cparams(("parallel", "parallel")),
    )(w, m, v, mine, theirs)


def _layer_weight(kind, i):
    if kind == "up":
        return "mlp_w_up", i
    if kind == "down":
        return "mlp_w_down", i
    return ("gdn_w_" if i % 2 == 0 else "hgrn_w_") + kind, i // 2


def kernel(x, gdn_w_in, gdn_conv, gdn_a_log, gdn_dt_bias, gdn_onorm, gdn_w_out, hgrn_w_in, hgrn_lb_logits, hgrn_gnorm, hgrn_w_out, norm_mix, norm_mlp, mlp_w_up, mlp_w_down, norm_final, loss_target, m_gdn_w_in, m_gdn_conv, m_gdn_a_log, m_gdn_dt_bias, m_gdn_onorm, m_gdn_w_out, m_hgrn_w_in, m_hgrn_lb_logits, m_hgrn_gnorm, m_hgrn_w_out, m_norm_mix, m_norm_mlp, m_mlp_w_up, m_mlp_w_down, m_norm_final, v_gdn_w_in, v_gdn_conv, v_gdn_a_log, v_gdn_dt_bias, v_gdn_onorm, v_gdn_w_out, v_hgrn_w_in, v_hgrn_lb_logits, v_hgrn_gnorm, v_hgrn_w_out, v_norm_mix, v_norm_mlp, v_mlp_w_up, v_mlp_w_down, v_norm_final):
    p = dict(gdn_w_in=gdn_w_in, gdn_conv=gdn_conv, gdn_a_log=gdn_a_log, gdn_dt_bias=gdn_dt_bias, gdn_onorm=gdn_onorm,
             gdn_w_out=gdn_w_out, hgrn_w_in=hgrn_w_in, hgrn_lb_logits=hgrn_lb_logits, hgrn_gnorm=hgrn_gnorm,
             hgrn_w_out=hgrn_w_out, norm_mix=norm_mix, norm_mlp=norm_mlp, mlp_w_up=mlp_w_up, mlp_w_down=mlp_w_down,
             norm_final=norm_final)
    m = dict(gdn_w_in=m_gdn_w_in, gdn_conv=m_gdn_conv, gdn_a_log=m_gdn_a_log, gdn_dt_bias=m_gdn_dt_bias,
             gdn_onorm=m_gdn_onorm, gdn_w_out=m_gdn_w_out, hgrn_w_in=m_hgrn_w_in, hgrn_lb_logits=m_hgrn_lb_logits,
             hgrn_gnorm=m_hgrn_gnorm, hgrn_w_out=m_hgrn_w_out, norm_mix=m_norm_mix, norm_mlp=m_norm_mlp,
             mlp_w_up=m_mlp_w_up, mlp_w_down=m_mlp_w_down, norm_final=m_norm_final)
    v = dict(gdn_w_in=v_gdn_w_in, gdn_conv=v_gdn_conv, gdn_a_log=v_gdn_a_log, gdn_dt_bias=v_gdn_dt_bias,
             gdn_onorm=v_gdn_onorm, gdn_w_out=v_gdn_w_out, hgrn_w_in=v_hgrn_w_in, hgrn_lb_logits=v_hgrn_lb_logits,
             hgrn_gnorm=v_hgrn_gnorm, hgrn_w_out=v_hgrn_w_out, norm_mix=v_norm_mix, norm_mlp=v_norm_mlp,
             mlp_w_up=v_mlp_w_up, mlp_w_down=v_mlp_w_down, norm_final=v_norm_final)
    xi, yi, ci = _mesh_pos()
    chip = 2 * xi + yi
    core = jnp.reshape(ci, (1,)).astype(jnp.int32)
    d = D_MODEL
    bsz, t_len, _ = x.shape
    n_tok = bsz * t_len

    def shard(kind, i):
        name, idx = _layer_weight(kind, i)
        return p[name][idx].astype(BF16)

    def w_in_of(i, slots):
        if i % 2 == 0:
            return _pad_lanes(jnp.transpose(slots, (1, 0, 2)).reshape(d, GDN_IN), GDN_IN_PAD)
        return slots

    (first,) = run_carries([gather_carry([shard("in", 0), p["gdn_conv"], p["hgrn_gnorm"]])], "gather_first")
    conv = jnp.transpose(first[1], (1, 2, 0, 3)).reshape(DEPTH // 2, CONV_K, 3 * d)
    gnorm = jnp.transpose(first[2], (1, 0, 2)).reshape(DEPTH // 2, d)
    lbl = p["hgrn_lb_logits"]
    h = x.reshape(n_tok, d)
    next_in, next_out = first[0], None
    saved = []
    for i in range(DEPTH):
        j = i // 2
        w_in = w_in_of(i, next_in)
        nmix = p["norm_mix"][i][None, :]
        (y, proj), got = norm_matmul(h, nmix, w_in, False, f"in_proj{i}",
                                     [gather_carry([shard("out", 0)])] if i == 0 else [])
        if i == 0:
            next_out = got[0][0]
        ride = [gather_carry([shard("up", i), shard("down", i)])]
        if i % 2 == 0:
            al = _pad_lanes(p["gdn_a_log"][j][None, :], 128)
            dtb = _pad_lanes(p["gdn_dt_bias"][j][None, :], 128)
            on = p["gdn_onorm"][j][None, :]
            (og, ssave), got = gdn_forward(proj, conv[j], al, dtb, on, bsz, f"gdn_fwd{i}", ride)
            mix = (proj, ssave, al, dtb, on)
        else:
            (o, ssave), got = hgrn_forward(proj, lbl, i, bsz, ride)
            gn = gnorm[j][None, :]
            og = hgrn_post_forward(o, proj, gn, f"hgrn_post{i}")
            mix = (proj, ssave, o, gn)
        w_up, w_down = got[0][0], got[0][1].reshape(MLP_HIDDEN, d)
        w_out = next_out.reshape(d, d)
        h1, _ = matmul(og, w_out, "nn", f"out_proj{i}", extra=h, epilogue="add")
        nmlp = p["norm_mlp"][i][None, :]
        (z, r), got = norm_matmul(h1, nmlp, w_up, True, f"mlp_up{i}",
                                     [gather_carry([shard("in", i + 1)])] if i + 1 < DEPTH else [])
        if i + 1 < DEPTH:
            next_in = got[0][0]
        h2, got = matmul(r, w_down, "nn", f"mlp_down{i}", extra=h1, epilogue="add", square_a=True,
                         carries=[gather_carry([shard("out", i + 1)])] if i + 1 < DEPTH else [])
        if i + 1 < DEPTH:
            next_out = got[0][0]
        saved.append((h, nmix, y, mix, og, w_in, w_out, h1, nmlp, z, r, w_up, w_down))
        h = h2
    loss_row, dh, dhb, d_nf = loss_head(h, p["norm_final"][None, :], loss_target.reshape(n_tok, d))

    G = {k: [None] * DEPTH for k in ("in", "out", "up", "down")}
    P = {k: [None] * DEPTH for k in ("in", "out", "up", "down")}
    slots = {k: [None] * DEPTH for k in ("in", "out", "up", "down")}
    d_nmix, d_nmlp = [None] * DEPTH, [None] * DEPTH
    d_conv, d_alog, d_dtb, d_onorm, d_gnorm = [None] * 2, [None] * 2, [None] * 2, [None] * 2, [None] * 2
    d_lbl = jnp.zeros((DEPTH, d), F32)
    for i in reversed(range(DEPTH)):
        j = i // 2
        h_in, nmix, y, mix, og, w_in, w_out, h1, nmlp, z, r, w_up, w_down = saved[i]
        ride = [sibling_half_carry([G["in"][i + 1]])] if i + 1 < DEPTH else []
        du, got = matmul(dhb, w_down, "nt", f"d_mlp_act{i}", out_dtype=BF16, extra=r, epilogue="mul2", carries=ride)
        if i + 1 < DEPTH:
            P["in"][i + 1] = add_core_halves(G["in"][i + 1], got[0][0], core, f"add_cores_in{i + 1}")
        G["down"][i] = matmul(r, dhb, "tn", f"dw_down{i}", square_a=True)[0].reshape(N_CHIPS, -1, d)
        dz, _ = matmul(du, w_up, "nt", f"d_mlp_in{i}")
        G["up"][i], _ = matmul(z, du, "tn", f"dw_up{i}", shards=N_CHIPS)
        (dh1, dh1b, d_nmlp[i]), got = norm_backward(h1, nmlp, dz, dh, f"d_norm_mlp{i}",
                                                    [sibling_half_carry([G["up"][i], G["down"][i]])])
        P["up"][i] = add_core_halves(G["up"][i], got[0][0], core, f"add_cores_up{i}")
        P["down"][i] = add_core_halves(G["down"][i], got[0][1], core, f"add_cores_down{i}")
        dog, _ = matmul(dh1b, w_out, "nt", f"d_mix_out{i}")
        G["out"][i] = matmul(og, dh1b, "tn", f"dw_out{i}")[0].reshape(N_CHIPS, -1, d)
        to_chips = [("up", i), ("down", i)] + ([("in", i + 1), ("out", i + 1)] if i + 1 < DEPTH else [])
        ride = [sibling_half_carry([G["out"][i]]), chips_carry([P[k][l] for k, l in to_chips])]
        if i % 2 == 0:
            proj, ssave, al, dtb, on = mix
            (dproj, dcw, dal, ddt, don), got = gdn_backward(proj, conv[j], al, dtb, on, ssave, dog, bsz, f"gdn_bwd{i}", ride)
            d_conv[j] = jnp.transpose(dcw, (2, 1, 0, 3)).reshape(CONV_K, 3 * d)
            d_alog[j], d_dtb[j], d_onorm[j] = dal[0, :HEADS], ddt[0, :HEADS], don[0]
        else:
            proj, ssave, o, gn = mix
            do_raw, dgate, dgn = hgrn_post_backward(o, proj, gn, dog, f"d_hgrn_post{i}")
            (dproj, dlb), got = hgrn_backward(proj, lbl, ssave, do_raw, dgate, i, bsz, ride)
            d_gnorm[j] = dgn[0]
            d_lbl = d_lbl + jnp.transpose(dlb, (1, 0, 2)).reshape(DEPTH, d)
        P["out"][i] = add_core_halves(G["out"][i], got[0][0], core, f"add_cores_out{i}")
        for (k, l), s in zip(to_chips, got[1]):
            slots[k][l] = s
        if i % 2 == 0:
            dw_in = matmul(y, dproj, "tn", f"dw_in{i}")[0][:, :GDN_IN]
            G["in"][i] = jnp.transpose(dw_in.reshape(d, N_CHIPS, GDN_IN // N_CHIPS), (1, 0, 2))
        else:
            G["in"][i], _ = matmul(y, dproj, "tn", f"dw_in{i}", shards=N_CHIPS)
        ride = [sibling_half_carry([G["in"][0]]), chips_carry([P["out"][0]])] if i == 0 else []
        dy, got = matmul(dproj, w_in, "nt", f"d_in_proj{i}", carries=ride)
        ride = []
        if i == 0:
            P["in"][0] = add_core_halves(G["in"][0], got[0][0], core, "add_cores_in0")
            slots["out"][0] = got[1][0]
            ride = [chips_carry([P["in"][0]])]
        (dh, dhb, d_nmix[i]), got = norm_backward(h_in, nmix, dy, dh1, f"d_norm_mix{i}", ride)
        if i == 0:
            slots["in"][0] = got[0][0]
    grad_x = dh.reshape(x.shape)

    by_weight = {}
    for kind in ("in", "out", "up", "down"):
        for i in range(DEPTH):
            by_weight.setdefault(_layer_weight(kind, i)[0], []).append(slots[kind][i])
    mine = {name: add_chip_slots(by_weight[name], f"add_chips_{name}") for name in BIG}
    small = {
        "gdn_conv": jnp.stack(d_conv), "gdn_a_log": jnp.stack(d_alog), "gdn_dt_bias": jnp.stack(d_dtb),
        "gdn_onorm": jnp.stack(d_onorm), "hgrn_lb_logits": d_lbl, "hgrn_gnorm": jnp.stack(d_gnorm),
        "norm_mix": jnp.concatenate(d_nmix, axis=0), "norm_mlp": jnp.concatenate(d_nmlp, axis=0), "norm_final": d_nf[0],
    }
    theirs, (blocks,) = run_carries([sibling_copy_carry([mine[name] for name in BIG]),
                                     gather_all_carry(_pack_small(small, loss_row))], "share_cores")
    theirs = dict(zip(BIG, theirs))

    total = add_slots(blocks, "add_small")
    loss = total[_SMALL_LAYOUT["loss"][0], 0]
    grads, delta, new_m, new_v = {}, {}, {}, {}
    for name in WEIGHTS:
        if name in BIG:
            grads[name], delta[name], new_m[name], new_v[name] = adamw_halves(
                p[name], m[name], v[name], mine[name], theirs[name], f"adamw_{name}")
            continue
        if name == "gdn_conv":
            full = _unpack_small(total, name, (2, CONV_K, 3 * d))
            grads[name] = lax.dynamic_slice_in_dim(full, chip * (3 * d // N_CHIPS), 3 * d // N_CHIPS, axis=2)
        elif name == "hgrn_gnorm":
            full = _unpack_small(total, name, (2, d))
            grads[name] = lax.dynamic_slice_in_dim(full, chip * (d // N_CHIPS), d // N_CHIPS, axis=1)
        else:
            grads[name] = _unpack_small(total, name, p[name].shape)
        delta[name], new_m[name], new_v[name] = adamw(p[name], grads[name], m[name], v[name], f"adamw_{name}")
    return (loss, grad_x, *[grads[n] for n in WEIGHTS], *[delta[n] for n in WEIGHTS],
            *[new_m[n] for n in WEIGHTS], *[new_v[n] for n in WEIGHTS])
```

```python
import functools
import math

import jax
import jax.numpy as jnp
from jax import lax
from jax.experimental import pallas as pl
from jax.experimental.pallas import tpu as pltpu

F32 = jnp.float32
BF16 = jnp.bfloat16
HI = lax.Precision.HIGHEST

D_MODEL = 1024
HEADS = 8
HEAD_DIM = 128
CHUNK = 64
SUB = 16
CONV_K = 4
HALO = 16
DEPTH = 4
EPS = 1e-6
MLP_HIDDEN = 4 * D_MODEL
GDN_MAIN = 4 * D_MODEL
GDN_IN = GDN_MAIN + 2 * HEADS
GDN_IN_PAD = GDN_MAIN + 128
NEG = -1e30

ADAM_LR = 0.001
ADAM_B1 = 0.9
ADAM_B2 = 0.999
ADAM_EPS = 1e-08
ADAM_WD = 0.01
ADAM_STEP = 10

VMEM_LIMIT = 48 * 1024 * 1024

MESH = pl.DeviceIdType.MESH


def _cparams(sem=None, **kw):
    if sem is not None:
        kw["dimension_semantics"] = sem
    return pltpu.CompilerParams(vmem_limit_bytes=VMEM_LIMIT, **kw)


def _iota(shape, dim):
    return lax.broadcasted_iota(jnp.int32, shape, dim)


_DIMS = {"nn": (((1,), (0,)), ((), ())), "nt": (((1,), (1,)), ((), ())), "tn": (((0,), (0,)), ((), ()))}


def _dot(a, b, mode):
    return lax.dot_general(a.astype(BF16), b.astype(BF16), _DIMS[mode], preferred_element_type=F32)


@functools.partial(jax.custom_vjp, nondiff_argnums=(2,))
def _mmx(a, b, mode):
    return _dot(a, b, mode)


def _mmx_fwd(a, b, mode):
    return _dot(a, b, mode), (a, b)


def _mmx_bwd(mode, res, g):
    a, b = res
    if mode == "nn":
        return _dot(g, b, "nt"), _dot(a, g, "tn")
    if mode == "nt":
        return _dot(g, b, "nn"), _dot(g, a, "tn")
    return _dot(b, g, "nt"), _dot(a, g, "nn")


_mmx.defvjp(_mmx_fwd, _mmx_bwd)


def _mm(a, b):
    return _mmx(a, b, "nn")


def _mm_nt(a, b):
    return _mmx(a, b, "nt")


def _mm_tn(a, b):
    return _mmx(a, b, "tn")


@functools.partial(jax.custom_vjp, nondiff_argnums=(1,))
def _roll_rows(x, d):
    return pltpu.roll(x, d, 0)


def _roll_rows_fwd(x, d):
    return pltpu.roll(x, d, 0), None


def _roll_rows_bwd(d, _, g):
    return (pltpu.roll(g, g.shape[0] - d, 0),)


_roll_rows.defvjp(_roll_rows_fwd, _roll_rows_bwd)


def _sig(x):
    return 1.0 / (1.0 + jnp.exp(-x))


@jax.custom_vjp
def _sigmoid(x):
    return _sig(x)


def _sigmoid_fwd(x):
    s = _sig(x)
    return s, s


def _sigmoid_bwd(s, g):
    return (g * s * (1.0 - s),)


_sigmoid.defvjp(_sigmoid_fwd, _sigmoid_bwd)


@jax.custom_vjp
def _silu(x):
    return x * _sig(x)


def _silu_fwd(x):
    s = _sig(x)
    return x * s, (x, s)


def _silu_bwd(res, g):
    x, s = res
    return (g * s * (1.0 + x * (1.0 - s)),)


_silu.defvjp(_silu_fwd, _silu_bwd)


@jax.custom_vjp
def _softplus(x):
    return jnp.maximum(x, 0.0) + jnp.log(1.0 + jnp.exp(-jnp.abs(x)))


def _softplus_fwd(x):
    return _softplus(x), x


def _softplus_bwd(x, g):
    return (g * _sig(x),)


_softplus.defvjp(_softplus_fwd, _softplus_bwd)


@jax.custom_vjp
def _log_sigmoid(x):
    return jnp.minimum(x, 0.0) - jnp.log(1.0 + jnp.exp(-jnp.abs(x)))


def _log_sigmoid_fwd(x):
    return _log_sigmoid(x), x


def _log_sigmoid_bwd(x, g):
    return (g * _sig(-x),)


_log_sigmoid.defvjp(_log_sigmoid_fwd, _log_sigmoid_bwd)


@jax.custom_vjp
def _logaddexp(a, b):
    return jnp.maximum(a, b) + jnp.log(1.0 + jnp.exp(-jnp.abs(a - b)))


def _logaddexp_fwd(a, b):
    return _logaddexp(a, b), (a, b)


def _unbroadcast(g, like):
    for ax in range(g.ndim):
        if like.shape[ax] == 1 and g.shape[ax] != 1:
            g = jnp.sum(g, axis=ax, keepdims=True)
    return g


def _logaddexp_bwd(res, g):
    a, b = res
    s = _sig(a - b)
    return _unbroadcast(g * s, a), _unbroadcast(g * (1.0 - s), b)


_logaddexp.defvjp(_logaddexp_fwd, _logaddexp_bwd)


def _row_to_col(row):
    n = row.shape[1]
    eye = _iota((n, n), 0) == _iota((n, n), 1)
    return jnp.sum(jnp.where(eye, jnp.broadcast_to(row, (n, n)), 0.0), axis=1, keepdims=True)


def _col_to_row(col):
    n = col.shape[0]
    eye = _iota((n, n), 0) == _iota((n, n), 1)
    return jnp.sum(jnp.where(eye, jnp.broadcast_to(col, (n, n)), 0.0), axis=0, keepdims=True)


def _pick_row(x, r):
    return jnp.sum(jnp.where(_iota(x.shape, 0) == r, x, 0.0), axis=0, keepdims=True)


def _pick_lane(x, l):
    return jnp.sum(jnp.where(_iota(x.shape, 1) == l, x, 0.0), axis=1, keepdims=True)


def _each(f, *lists):
    return [f(*t) for t in zip(*lists)]


def _unit_lower_inverse(Ls):
    n = Ls[0].shape[0]
    r, c = _iota((n, n), 0), _iota((n, n), 1)
    eye = jnp.where(r == c, 1.0, 0.0).astype(F32)
    Ld = _each(lambda L: jnp.where((r // SUB) == (c // SUB), L, 0.0), Ls)
    Lo = _each(lambda L, d: L - d, Ls, Ld)
    P = _each(lambda d: eye - d, Ld)
    Lp = Ld
    for _ in range(int(math.log2(SUB)) - 1):
        Lp = _each(lambda x: _mm(x, x), Lp)
        P = _each(lambda p, x: p + _mm(p, x), P, Lp)
    N = _each(_mm, P, Lo)
    N2 = _each(lambda x: _mm(x, x), N)
    X = _each(lambda x, x2: (eye - x) + _mm(eye - x, x2), N, N2)
    return _each(_mm, X, P)


def _shift_rows(x, halo, d):
    if d == 0:
        return x
    xr = _roll_rows(x, d)
    hr = _roll_rows(halo, d)
    hfull = jnp.concatenate([hr, jnp.zeros((x.shape[0] - HALO, x.shape[1]), F32)], axis=0)
    return jnp.where(_iota(x.shape, 0) >= d, xr, hfull)


def _causal_conv_chunk(x, halo, w):
    y = None
    for kk in range(CONV_K):
        t = _shift_rows(x, halo, CONV_K - 1 - kk) * _pick_row(w, kk)
        y = t if y is None else y + t
    return y


def _gdn_chunk(xq, xk, xv, hq, hk, hv, gate, ab, S, cwq, cwk, cwv, alog, dtb, onorm, *, heads, seqs):
    C = xq[0].shape[0]
    q = _each(lambda x, h, w: _silu(_causal_conv_chunk(x, h, w)), xq, hq, cwq)
    k = _each(lambda x, h, w: _silu(_causal_conv_chunk(x, h, w)), xk, hk, cwk)
    v = _each(lambda x, h, w: _silu(_causal_conv_chunk(x, h, w)), xv, hv, cwv)
    q = _each(lambda t: t * lax.rsqrt(jnp.sum(t * t, axis=1, keepdims=True) + EPS) * (HEAD_DIM ** -0.5), q)
    k = _each(lambda t: t * lax.rsqrt(jnp.sum(t * t, axis=1, keepdims=True) + EPS), k)
    beta_all = _each(_sigmoid, ab)
    g_all = _each(lambda t: -jnp.exp(alog) * _softplus(t + dtb), ab)
    beta = [_pick_lane(beta_all[b], HEADS + h) for b, h in zip(seqs, heads)]
    g = [_pick_lane(g_all[b], h) for b, h in zip(seqs, heads)]
    r, c = _iota((C, C), 0), _iota((C, C), 1)
    gc = _each(lambda t: jnp.sum(jnp.where(c <= r, jnp.broadcast_to(_col_to_row(t), (C, C)), 0.0), axis=1,
                                 keepdims=True), g)
    gc_row = _each(lambda t: jnp.sum(jnp.where(r <= c, jnp.broadcast_to(t, (C, C)), 0.0), axis=0, keepdims=True), g)
    decay = _each(lambda a, b: jnp.exp(jnp.where(r >= c, a - b, NEG)), gc, gc_row)
    kb = _each(lambda a, b: a * b, k, beta)
    L = _each(lambda a, b, d: jnp.where(r > c, _mm_nt(a, b) * d, 0.0), kb, k, decay)
    A = _each(lambda a, b, d: jnp.where(r >= c, _mm_nt(a, b) * d, 0.0), q, k, decay)
    T = _unit_lower_inverse(L)
    egc = _each(jnp.exp, gc)
    u = _each(lambda t, a, b: _mm(t, a * b), T, v, beta)
    w = _each(lambda t, a, e: _mm(t, a * e), T, kb, egc)
    gl = _each(lambda t: _pick_row(t, C - 1), gc)
    v_new = _each(lambda a, b, s: a - _mm(b, s), u, w, S)
    o = _each(lambda a, e, s, m, vn: _mm(a * e, s) + _mm(m, vn), q, egc, S, A, v_new)
    S_next = _each(lambda s, l, a, t, vn: s * jnp.exp(l) + _mm_tn(a * jnp.exp(l - t), vn), S, gl, k, gc, v_new)
    o = _each(lambda t, gt: t * lax.rsqrt(jnp.mean(t * t, axis=1, keepdims=True) + EPS) * onorm * _silu(gt), o, gate)
    return o, S_next


def _hgrn_lower_bound(lbl, layer):
    e = jnp.exp(lbl - jnp.max(lbl, axis=0, keepdims=True))
    sm = e / jnp.sum(e, axis=0, keepdims=True)
    r = _iota(lbl.shape, 0)
    return jnp.sum(jnp.where((r >= 1) & (r <= layer), sm, 0.0), axis=0, keepdims=True)


_LEVELS = (1, 2, 4, 8, 16, 32)


def _prefix_matrix(n):
    r, c = _iota((n, n), 0), _iota((n, n), 1)
    parts = [jnp.where(c <= r, 1.0, 0.0)]
    for s in _LEVELS:
        parts.append(jnp.where(c < (r // (2 * s)) * (2 * s) + s, 1.0, 0.0))
    return jnp.concatenate(parts, axis=0).astype(F32)


def _prefix_sums_of(x):
    n = x.shape[0]
    y = lax.dot_general(_prefix_matrix(n), x, _DIMS["nn"], precision=HI, preferred_element_type=F32)
    return tuple(y[t * n:(t + 1) * n] for t in range(len(_LEVELS) + 1))


@jax.custom_vjp
def _prefix_sums(x):
    return _prefix_sums_of(x)


def _prefix_sums_fwd(x):
    return _prefix_sums_of(x), None


def _prefix_sums_bwd(_, gs):
    g = jnp.concatenate(gs, axis=0)
    return (lax.dot_general(_prefix_matrix(gs[0].shape[0]), g, _DIMS["tn"], precision=HI, preferred_element_type=F32),)


_prefix_sums.defvjp(_prefix_sums_fwd, _prefix_sums_bwd)


def _hgrn_chunk(qp, fp, v, S, lbl, *, layer):
    C = qp[0].shape[0]
    lb = _each(lambda l: _hgrn_lower_bound(l, layer), lbl)
    lf = _each(lambda l, f: _logaddexp(jnp.log(l), jnp.log(1.0 - l) + _log_sigmoid(f)), lb, fp)
    k = _each(lambda l, f: (1.0 - l) * _sigmoid(-f), lb, fp)
    q = _each(lambda x: _silu(x) * (HEAD_DIM ** -0.5), qp)
    r, c = _iota((C, C), 0), _iota((C, C), 1)
    row = _iota(qp[0].shape, 0)
    sums = _each(_prefix_sums, lf)
    gc = [t[0] for t in sums]
    a = _each(lambda x, y: jnp.where(r == c, _mm_nt(x, y), 0.0), q, k)
    for n, s in enumerate(_LEVELS):
        ref = [t[n + 1] for t in sums]
        upper = (row % (2 * s)) >= s
        same = (r // (2 * s)) == (c // (2 * s))
        q_s = _each(lambda x, g, m: x * jnp.exp(jnp.where(upper, g - m, NEG)), q, gc, ref)
        k_s = _each(lambda x, g, m: x * jnp.exp(jnp.where(upper, NEG, m - g)), k, gc, ref)
        a = _each(lambda t, x, y: t + jnp.where(same, _mm_nt(x, y), 0.0), a, q_s, k_s)
    o = _each(lambda t, x, g, vv, st: _mm(t, vv) + _mm(x * jnp.exp(g), st), a, q, gc, v, S)
    gl = _each(lambda g: _pick_row(g, C - 1), gc)
    S_next = _each(lambda st, l, x, g, vv: st * _row_to_col(jnp.exp(l)) + _mm_tn(x * jnp.exp(l - g), vv),
                   S, gl, k, gc, v)
    return o, S_next


N_CHIPS = 4
N_DEV = 8
_ANY = pl.BlockSpec(memory_space=pl.ANY)


def _mesh_pos():
    return lax.axis_index("x"), lax.axis_index("y"), lax.axis_index("c")


def _other_chips(x, y):
    ps = [(1 - x, y), (x, 1 - y), (1 - x, 1 - y)]
    return [(p, 2 * p[0] + p[1]) for p in ps]


def _remote(src, dst, send_sem, recv_sem, dev):
    return pltpu.make_async_remote_copy(src_ref=src, dst_ref=dst, send_sem=send_sem, recv_sem=recv_sem,
                                        device_id=dev, device_id_type=MESH)


class Carry:
    def __init__(self, ins, out_shapes, sems, start, finish):
        self.ins, self.out_shapes, self.sems, self.start, self.finish = list(ins), list(out_shapes), list(sems), start, finish


def _pcall(body, *, name, grid, in_specs, out_specs, out_shape, scratch_shapes=(), dims, args, carries=()):
    in_specs, out_specs, out_shape = list(in_specs), list(out_specs), list(out_shape)
    scratch_shapes, args = list(scratch_shapes), list(args)
    n_in, n_out, n_scr = len(in_specs), len(out_shape), len(scratch_shapes)
    carries = [c for c in carries if c is not None]
    if not carries:
        res = pl.pallas_call(body, name=name, grid=grid, in_specs=in_specs, out_specs=out_specs, out_shape=out_shape,
                             scratch_shapes=scratch_shapes, compiler_params=_cparams(dims))(*args)
        return list(res), []
    ci = [len(c.ins) for c in carries]
    co = [len(c.out_shapes) for c in carries]
    cs = [len(c.sems) for c in carries]

    def split(seq, sizes):
        out, k = [], 0
        for s in sizes:
            out.append(seq[k:k + s])
            k += s
        return out

    def carried(*refs):
        ins, cins, outs, couts, scr, sems = split(refs, [n_in, sum(ci), n_out, sum(co), n_scr, sum(cs)])
        cins, couts, sems = split(cins, ci), split(couts, co), split(sems, cs)
        ids = [pl.program_id(a) for a in range(len(grid))]
        first, last = ids[0] == 0, ids[0] == grid[0] - 1
        for a in range(1, len(grid)):
            first, last = first & (ids[a] == 0), last & (ids[a] == grid[a] - 1)

        @pl.when(first)
        def _():
            for c, i, o, s in zip(carries, cins, couts, sems):
                c.start(i, o, s)

        body(*ins, *outs, *scr)

        @pl.when(last)
        def _():
            for c, i, o, s in zip(carries, cins, couts, sems):
                c.finish(i, o, s)

    res = pl.pallas_call(
        carried, name=name, grid=grid,
        in_specs=in_specs + [_ANY] * sum(ci), out_specs=out_specs + [_ANY] * sum(co),
        out_shape=out_shape + [s for c in carries for s in c.out_shapes],
        scratch_shapes=scratch_shapes + [s for c in carries for s in c.sems],
        compiler_params=_cparams(("arbitrary",) * len(grid)),
    )(*args, *[a for c in carries for a in c.ins])
    return list(res[:n_out]), split(list(res[n_out:]), co)


def run_carries(carries, name):
    ci = [len(c.ins) for c in carries]
    co = [len(c.out_shapes) for c in carries]
    cs = [len(c.sems) for c in carries]

    def split(seq, sizes):
        out, k = [], 0
        for s in sizes:
            out.append(seq[k:k + s])
            k += s
        return out

    def body(*refs):
        cins, couts, sems = split(refs, [sum(ci), sum(co), sum(cs)])
        cins, couts, sems = split(cins, ci), split(couts, co), split(sems, cs)
        for c, i, o, s in zip(carries, cins, couts, sems):
            c.start(i, o, s)
        for c, i, o, s in zip(carries, cins, couts, sems):
            c.finish(i, o, s)

    res = pl.pallas_call(
        body, name=name, in_specs=[_ANY] * sum(ci), out_specs=[_ANY] * sum(co),
        out_shape=[s for c in carries for s in c.out_shapes], scratch_shapes=[s for c in carries for s in c.sems],
    )(*[a for c in carries for a in c.ins])
    return split(list(res), co)


def gather_carry(arrs):
    n = len(arrs)
    split = [a.ndim == 2 and a.shape[0] % 32 == 0 for a in arrs]

    def plan(ins, outs, sems):
        send_sems, recv_sems, pass_send, pass_recv, local_sems = sems
        x, y, c = _mesh_pos()
        me = 2 * x + y
        peers = _other_chips(x, y)
        locs = [pltpu.make_async_copy(ins[a], outs[a].at[me], local_sems.at[a]) for a in range(n)]
        sends, recvs, passes, pass_recvs = [], [], [], []
        for a in range(n):
            half = arrs[a].shape[0] // 2
            mine, other = pl.ds(c * half, half), pl.ds((1 - c) * half, half)
            for j, ((px, py), t) in enumerate(peers):
                sem = (send_sems.at[a, j], recv_sems.at[a, j])
                if split[a]:
                    sends.append(_remote(ins[a].at[mine], outs[a].at[me, mine], *sem, (px, py, c)))
                    recvs.append(_remote(ins[a].at[mine], outs[a].at[t, mine], *sem, (px, py, c)))
                    psem = (pass_send.at[a, j], pass_recv.at[a, j])
                    passes.append(_remote(outs[a].at[t, mine], outs[a].at[t, mine], *psem, (x, y, 1 - c)))
                    pass_recvs.append(_remote(outs[a].at[t, other], outs[a].at[t, other], *psem, (x, y, 1 - c)))
                else:
                    sends.append(_remote(ins[a], outs[a].at[me], *sem, (px, py, c)))
                    recvs.append(_remote(ins[a], outs[a].at[t], *sem, (px, py, c)))
                    passes.append(None)
                    pass_recvs.append(None)
        return locs, sends, recvs, passes, pass_recvs

    def start(ins, outs, sems):
        locs, sends, _, _, _ = plan(ins, outs, sems)
        for cp in locs + sends:
            cp.start()

    def finish(ins, outs, sems):
        locs, sends, recvs, passes, pass_recvs = plan(ins, outs, sems)
        for r, p in zip(recvs, passes):
            r.wait_recv()
            if p is not None:
                p.start()
        for p in pass_recvs:
            if p is not None:
                p.wait_recv()
        for r in sends + [p for p in passes if p is not None]:
            r.wait_send()
        for cp in locs:
            cp.wait()

    return Carry(arrs, [jax.ShapeDtypeStruct((N_CHIPS,) + a.shape, a.dtype) for a in arrs],
                 [pltpu.SemaphoreType.DMA((n, 3)), pltpu.SemaphoreType.DMA((n, 3)), pltpu.SemaphoreType.DMA((n, 3)),
                  pltpu.SemaphoreType.DMA((n, 3)), pltpu.SemaphoreType.DMA((n,))], start, finish)


def sibling_half_carry(gs):
    n = len(gs)

    def copies(ins, outs, sems):
        send_sems, recv_sems = sems
        x, y, c = _mesh_pos()
        out = []
        for a in range(n):
            half = gs[a].shape[1] // 2
            out.append(_remote(ins[a].at[:, pl.ds((1 - c) * half, half), :], outs[a], send_sems.at[a], recv_sems.at[a],
                               (x, y, 1 - c)))
        return out

    def start(ins, outs, sems):
        for r in copies(ins, outs, sems):
            r.start()

    def finish(ins, outs, sems):
        cps = copies(ins, outs, sems)
        for r in cps:
            r.wait_recv()
        for r in cps:
            r.wait_send()

    return Carry(gs, [jax.ShapeDtypeStruct((g.shape[0], g.shape[1] // 2, g.shape[2]), g.dtype) for g in gs],
                 [pltpu.SemaphoreType.DMA((n,)), pltpu.SemaphoreType.DMA((n,))], start, finish)


def chips_carry(ps):
    n = len(ps)

    def copies(ins, outs, sems):
        send_sems, recv_sems, local_sems = sems
        x, y, c = _mesh_pos()
        me = 2 * x + y
        peers = _other_chips(x, y)
        locs = [pltpu.make_async_copy(ins[a].at[me], outs[a].at[me], local_sems.at[a]) for a in range(n)]
        sends = [_remote(ins[a].at[t], outs[a].at[me], send_sems.at[a, j], recv_sems.at[a, j], (px, py, c))
                 for a in range(n) for j, ((px, py), t) in enumerate(peers)]
        recvs = [_remote(ins[a].at[t], outs[a].at[t], send_sems.at[a, j], recv_sems.at[a, j], (px, py, c))
                 for a in range(n) for j, ((px, py), t) in enumerate(peers)]
        return locs, sends, recvs

    def start(ins, outs, sems):
        locs, sends, _ = copies(ins, outs, sems)
        for cp in locs + sends:
            cp.start()

    def finish(ins, outs, sems):
        locs, sends, recvs = copies(ins, outs, sems)
        for r in recvs:
            r.wait_recv()
        for r in sends:
            r.wait_send()
        for cp in locs:
            cp.wait()

    return Carry(ps, [jax.ShapeDtypeStruct(p.shape, p.dtype) for p in ps],
                 [pltpu.SemaphoreType.DMA((n, 3)), pltpu.SemaphoreType.DMA((n, 3)), pltpu.SemaphoreType.DMA((n,))],
                 start, finish)


def sibling_copy_carry(arrs):
    n = len(arrs)

    def copies(ins, outs, sems):
        send_sems, recv_sems = sems
        x, y, c = _mesh_pos()
        return [_remote(ins[a], outs[a], send_sems.at[a], recv_sems.at[a], (x, y, 1 - c)) for a in range(n)]

    def start(ins, outs, sems):
        for r in copies(ins, outs, sems):
            r.start()

    def finish(ins, outs, sems):
        cps = copies(ins, outs, sems)
        for r in cps:
            r.wait_recv()
        for r in cps:
            r.wait_send()

    return Carry(arrs, [jax.ShapeDtypeStruct(a.shape, a.dtype) for a in arrs],
                 [pltpu.SemaphoreType.DMA((n,)), pltpu.SemaphoreType.DMA((n,))], start, finish)


def gather_all_carry(small):
    flips = [(fx, fy, fc) for fx in (0, 1) for fy in (0, 1) for fc in (0, 1)][1:]

    def copies(ins, outs, sems):
        send_sems, recv_sems, local_sem = sems
        in_ref, out_ref = ins[0], outs[0]
        x, y, c = _mesh_pos()
        me = 4 * x + 2 * y + c
        peers = [((1 - x) if fx else x, (1 - y) if fy else y, (1 - c) if fc else c) for fx, fy, fc in flips]
        loc = pltpu.make_async_copy(in_ref, out_ref.at[me], local_sem.at[0])
        sends = [_remote(in_ref, out_ref.at[me], send_sems.at[k], recv_sems.at[k], p) for k, p in enumerate(peers)]
        recvs = [_remote(in_ref, out_ref.at[4 * p[0] + 2 * p[1] + p[2]], send_sems.at[k], recv_sems.at[k], p)
                 for k, p in enumerate(peers)]
        return loc, sends, recvs

    def start(ins, outs, sems):
        loc, sends, _ = copies(ins, outs, sems)
        loc.start()
        for r in sends:
            r.start()

    def finish(ins, outs, sems):
        loc, sends, recvs = copies(ins, outs, sems)
        for r in recvs:
            r.wait_recv()
        for r in sends:
            r.wait_send()
        loc.wait()

    return Carry([small], [jax.ShapeDtypeStruct((N_DEV,) + small.shape, small.dtype)],
                 [pltpu.SemaphoreType.DMA((N_DEV - 1,)), pltpu.SemaphoreType.DMA((N_DEV - 1,)),
                  pltpu.SemaphoreType.DMA((1,))], start, finish)


def _chunk_spec(bsz, nc, col, rev=False, rows=CHUNK, width=D_MODEL):
    return pl.BlockSpec((bsz, rows, width), lambda n: (0, (nc - 1 - n) if rev else n, col))


def _halo_spec(bsz, nc, col, rev):
    per = CHUNK // HALO
    return pl.BlockSpec((bsz, HALO, D_MODEL),
                        lambda n: (0, jnp.maximum(((nc - 1 - n) if rev else n) * per - 1, 0), col))


def _const_spec(shape):
    nd = len(shape)
    return pl.BlockSpec(shape, lambda n: (0,) * nd)


def _state_spec(bsz, nc, rev=False):
    return pl.BlockSpec((bsz, None, HEADS, HEAD_DIM, HEAD_DIM), lambda n: (0, (nc - 1 - n) if rev else n, 0, 0, 0))


def _lanes(hh):
    return slice(hh * HEAD_DIM, (hh + 1) * HEAD_DIM)


def _chains(bsz):
    return [(b, hh) for b in range(bsz) for hh in range(HEADS)]


def _per_chain(ref, bsz, scale=None):
    vals = [ref[b, :, _lanes(hh)].astype(F32) for b, hh in _chains(bsz)]
    return vals if scale is None else [v * scale for v in vals]


def _per_head(ref, bsz):
    return [ref[:, _lanes(hh)].astype(F32) for _, hh in _chains(bsz)]


def gdn_forward(proj, conv_w, alog, dtb, onorm, bsz, name, carries=()):
    n_tok = proj.shape[0]
    t_len = n_tok // bsz
    nc = t_len // CHUNK
    chains = _chains(bsz)

    def body(xq, xk, xv, hq, hk, hv, gate, ab, cwq, cwk, cwv, al, dt, on, o_ref, ssave_ref, s_ref):
        n = pl.program_id(0)
        keep = jnp.where(n > 0, 1.0, 0.0).astype(F32)

        @pl.when(n == 0)
        def _():
            s_ref[...] = jnp.zeros_like(s_ref)

        S = [s_ref[b, hh] for b, hh in chains]
        for (b, hh), s in zip(chains, S):
            ssave_ref[b, hh] = s
        outs, s_next = _gdn_chunk(
            _per_chain(xq, bsz), _per_chain(xk, bsz), _per_chain(xv, bsz), _per_chain(hq, bsz, keep),
            _per_chain(hk, bsz, keep), _per_chain(hv, bsz, keep), _per_chain(gate, bsz),
            [ab[b].astype(F32) for b in range(bsz)], S, _per_head(cwq, bsz), _per_head(cwk, bsz), _per_head(cwv, bsz),
            al[...], dt[...], on[...], heads=[hh for _, hh in chains], seqs=[b for b, _ in chains])
        for (b, hh), o, s in zip(chains, outs, s_next):
            o_ref[b, :, _lanes(hh)] = o.astype(o_ref.dtype)
            s_ref[b, hh] = s

    cw_spec = lambda col: pl.BlockSpec((CONV_K, D_MODEL), lambda n: (0, col))
    in_specs = [_chunk_spec(bsz, nc, 0), _chunk_spec(bsz, nc, 1), _chunk_spec(bsz, nc, 2),
                _halo_spec(bsz, nc, 0, False), _halo_spec(bsz, nc, 1, False), _halo_spec(bsz, nc, 2, False),
                _chunk_spec(bsz, nc, 3), _chunk_spec(bsz, nc, GDN_MAIN // 128, width=128),
                cw_spec(0), cw_spec(1), cw_spec(2),
                _const_spec((1, 128)), _const_spec((1, 128)), _const_spec((1, 128))]
    out_specs = [_chunk_spec(bsz, nc, 0), _state_spec(bsz, nc)]
    p3 = proj.reshape(bsz, t_len, -1)
    (o, ssave), got = _pcall(
        body, name=name, grid=(nc,), in_specs=in_specs, out_specs=out_specs,
        out_shape=(jax.ShapeDtypeStruct((bsz, t_len, D_MODEL), BF16),
                   jax.ShapeDtypeStruct((bsz, nc, HEADS, HEAD_DIM, HEAD_DIM), F32)),
        scratch_shapes=[pltpu.VMEM((bsz, HEADS, HEAD_DIM, HEAD_DIM), F32)], dims=("arbitrary",),
        args=(p3, p3, p3, p3, p3, p3, p3, p3, conv_w, conv_w, conv_w, alog, dtb, onorm), carries=carries)
    return (o.reshape(n_tok, D_MODEL), ssave), got


def gdn_backward(proj, conv_w, alog, dtb, onorm, s_saved, d_out, bsz, name, carries=()):
    n_tok = proj.shape[0]
    t_len = n_tok // bsz
    nc = t_len // CHUNK
    chains = _chains(bsz)

    def body(xq, xk, xv, hq, hk, hv, gate, ab, cwq, cwk, cwv, al, dt, on, ssave, do,
             dp_ref, dcw_ref, dal_ref, ddt_ref, don_ref, ds_ref, dhalo_ref):
        n = pl.program_id(0)
        nr = nc - 1 - n

        @pl.when(n == 0)
        def _():
            dcw_ref[...] = jnp.zeros_like(dcw_ref)
            dal_ref[...] = jnp.zeros_like(dal_ref)
            ddt_ref[...] = jnp.zeros_like(ddt_ref)
            don_ref[...] = jnp.zeros_like(don_ref)
            ds_ref[...] = jnp.zeros_like(ds_ref)
            dhalo_ref[...] = jnp.zeros_like(dhalo_ref)

        keep = jnp.where(nr > 0, 1.0, 0.0).astype(F32)
        pad = jnp.zeros((CHUNK - HALO, HEAD_DIM), F32)
        args = (_per_chain(xq, bsz), _per_chain(xk, bsz), _per_chain(xv, bsz), _per_chain(hq, bsz, keep),
                _per_chain(hk, bsz, keep), _per_chain(hv, bsz, keep), _per_chain(gate, bsz),
                [ab[b].astype(F32) for b in range(bsz)], [ssave[b, hh] for b, hh in chains],
                _per_head(cwq, bsz), _per_head(cwk, bsz), _per_head(cwv, bsz), al[...], dt[...], on[...])
        _, vjp = jax.vjp(functools.partial(_gdn_chunk, heads=[hh for _, hh in chains], seqs=[b for b, _ in chains]),
                         *args)
        (gxq, gxk, gxv, ghq, ghk, ghv, ggate, gab, gS, gcq, gck, gcv, gal, gdt, gon) = vjp(
            (_per_chain(do, bsz), [ds_ref[b, hh] for b, hh in chains]))
        d = D_MODEL
        for e, (b, hh) in enumerate(chains):
            c0 = hh * HEAD_DIM
            for part, (gx, gh, gc) in enumerate(((gxq, ghq, gcq), (gxk, ghk, gck), (gxv, ghv, gcv))):
                full = gx[e] + jnp.concatenate([pad, dhalo_ref[b, hh, part]], axis=0)
                dp_ref[b, :, part * d + c0:part * d + c0 + HEAD_DIM] = full.astype(dp_ref.dtype)
                dhalo_ref[b, hh, part] = gh[e] * keep
                dcw_ref[hh, part] += gc[e]
            dp_ref[b, :, 3 * d + c0:3 * d + c0 + HEAD_DIM] = ggate[e].astype(dp_ref.dtype)
            ds_ref[b, hh] = gS[e]
        for b in range(bsz):
            dp_ref[b, :, GDN_MAIN:GDN_IN_PAD] = gab[b].astype(dp_ref.dtype)
        dal_ref[...] += gal
        ddt_ref[...] += gdt
        don_ref[...] += gon

    cw_spec = lambda col: pl.BlockSpec((CONV_K, D_MODEL), lambda n: (0, col))
    in_specs = [_chunk_spec(bsz, nc, 0, True), _chunk_spec(bsz, nc, 1, True), _chunk_spec(bsz, nc, 2, True),
                _halo_spec(bsz, nc, 0, True), _halo_spec(bsz, nc, 1, True), _halo_spec(bsz, nc, 2, True),
                _chunk_spec(bsz, nc, 3, True), _chunk_spec(bsz, nc, GDN_MAIN // 128, True, width=128),
                cw_spec(0), cw_spec(1), cw_spec(2),
                _const_spec((1, 128)), _const_spec((1, 128)), _const_spec((1, 128)),
                _state_spec(bsz, nc, True), _chunk_spec(bsz, nc, 0, True)]
    out_specs = [_chunk_spec(bsz, nc, 0, True, width=GDN_IN_PAD),
                 _const_spec((HEADS, 3, CONV_K, HEAD_DIM)), _const_spec((1, 128)), _const_spec((1, 128)),
                 _const_spec((1, 128))]
    row = jax.ShapeDtypeStruct((1, 128), F32)
    p3 = proj.reshape(bsz, t_len, -1)
    (dp, dcw, dal, ddt, don), got = _pcall(
        body, name=name, grid=(nc,), in_specs=in_specs, out_specs=out_specs,
        out_shape=(jax.ShapeDtypeStruct((bsz, t_len, GDN_IN_PAD), BF16),
                   jax.ShapeDtypeStruct((HEADS, 3, CONV_K, HEAD_DIM), F32), row, row, row),
        scratch_shapes=[pltpu.VMEM((bsz, HEADS, HEAD_DIM, HEAD_DIM), F32),
                        pltpu.VMEM((bsz, HEADS, 3, HALO, HEAD_DIM), F32)],
        dims=("arbitrary",),
        args=(p3, p3, p3, p3, p3, p3, p3, p3, conv_w, conv_w, conv_w, alog, dtb, onorm, s_saved,
              d_out.reshape(bsz, t_len, D_MODEL)),
        carries=carries)
    return (dp.reshape(n_tok, GDN_IN_PAD), dcw, dal, ddt, don), got


def hgrn_forward(proj, lbl, layer, bsz, carries=()):
    n_tok = proj.shape[0]
    t_len = n_tok // bsz
    nc = t_len // CHUNK
    chains = _chains(bsz)

    def body(qp, fp, vi, lb, o_ref, ssave_ref, s_ref):
        @pl.when(pl.program_id(0) == 0)
        def _():
            s_ref[...] = jnp.zeros_like(s_ref)

        S = [s_ref[b, hh] for b, hh in chains]
        for (b, hh), s in zip(chains, S):
            ssave_ref[b, hh] = s
        outs, s_next = _hgrn_chunk(_per_chain(qp, bsz), _per_chain(fp, bsz), _per_chain(vi, bsz), S,
                                   _per_head(lb, bsz), layer=layer)
        for (b, hh), o, s in zip(chains, outs, s_next):
            o_ref[b, :, _lanes(hh)] = o
            s_ref[b, hh] = s

    in_specs = [_chunk_spec(bsz, nc, 0), _chunk_spec(bsz, nc, 1), _chunk_spec(bsz, nc, 2),
                _const_spec((DEPTH, D_MODEL))]
    out_specs = [_chunk_spec(bsz, nc, 0), _state_spec(bsz, nc)]
    p3 = proj.reshape(bsz, t_len, -1)
    (o, ssave), got = _pcall(
        body, name=f"hgrn_fwd{layer}", grid=(nc,), in_specs=in_specs, out_specs=out_specs,
        out_shape=(jax.ShapeDtypeStruct((bsz, t_len, D_MODEL), F32),
                   jax.ShapeDtypeStruct((bsz, nc, HEADS, HEAD_DIM, HEAD_DIM), F32)),
        scratch_shapes=[pltpu.VMEM((bsz, HEADS, HEAD_DIM, HEAD_DIM), F32)], dims=("arbitrary",),
        args=(p3, p3, p3, lbl), carries=carries)
    return (o.reshape(n_tok, D_MODEL), ssave), got


def hgrn_backward(proj, lbl, s_saved, d_o, d_gate, layer, bsz, carries=()):
    n_tok = proj.shape[0]
    t_len = n_tok // bsz
    nc = t_len // CHUNK
    chains = _chains(bsz)

    def body(qp, fp, vi, lb, ssave, do, dgt, dp_ref, dlb_ref, ds_ref):
        @pl.when(pl.program_id(0) == 0)
        def _():
            dlb_ref[...] = jnp.zeros_like(dlb_ref)
            ds_ref[...] = jnp.zeros_like(ds_ref)

        _, vjp = jax.vjp(functools.partial(_hgrn_chunk, layer=layer), _per_chain(qp, bsz), _per_chain(fp, bsz),
                         _per_chain(vi, bsz), [ssave[b, hh] for b, hh in chains], _per_head(lb, bsz))
        gq, gf, gv, gS, glb = vjp((_per_chain(do, bsz), [ds_ref[b, hh] for b, hh in chains]))
        d = D_MODEL
        for e, (b, hh) in enumerate(chains):
            c0 = hh * HEAD_DIM
            dp_ref[b, :, c0:c0 + HEAD_DIM] = gq[e].astype(dp_ref.dtype)
            dp_ref[b, :, d + c0:d + c0 + HEAD_DIM] = gf[e].astype(dp_ref.dtype)
            dp_ref[b, :, 2 * d + c0:2 * d + c0 + HEAD_DIM] = gv[e].astype(dp_ref.dtype)
            dlb_ref[hh] += glb[e]
            ds_ref[b, hh] = gS[e]
        dp_ref[:, :, 3 * d:4 * d] = dgt[...]

    in_specs = [_chunk_spec(bsz, nc, 0, True), _chunk_spec(bsz, nc, 1, True), _chunk_spec(bsz, nc, 2, True),
                _const_spec((DEPTH, D_MODEL)), _state_spec(bsz, nc, True),
                _chunk_spec(bsz, nc, 0, True), _chunk_spec(bsz, nc, 0, True)]
    out_specs = [_chunk_spec(bsz, nc, 0, True, width=4 * D_MODEL), _const_spec((HEADS, DEPTH, HEAD_DIM))]
    p3 = proj.reshape(bsz, t_len, -1)
    (dp, dlb), got = _pcall(
        body, name=f"hgrn_bwd{layer}", grid=(nc,), in_specs=in_specs, out_specs=out_specs,
        out_shape=(jax.ShapeDtypeStruct((bsz, t_len, 4 * D_MODEL), BF16),
                   jax.ShapeDtypeStruct((HEADS, DEPTH, HEAD_DIM), F32)),
        scratch_shapes=[pltpu.VMEM((bsz, HEADS, HEAD_DIM, HEAD_DIM), F32)], dims=("arbitrary",),
        args=(p3, p3, p3, lbl, s_saved, d_o.reshape(bsz, t_len, D_MODEL), d_gate.reshape(bsz, t_len, D_MODEL)),
        carries=carries)
    return (dp.reshape(n_tok, 4 * D_MODEL), dlb), got


ROW_TILE = 512
MM_VMEM_BUDGET = 36 * 1024 * 1024


def _tile(n):
    for cand in (1024, 512, 1408, 384, 256, 128):
        if n % cand == 0:
            return cand
    return n


def _rmsnorm(x, w):
    return x * lax.rsqrt(jnp.mean(x * x, axis=1, keepdims=True) + EPS) * w


def norm_matmul(h, nw, w, relu, name, carries=()):
    n_tok, d = h.shape
    slots = w.ndim == 3
    n_out = w.shape[0] * w.shape[2] if slots else w.shape[1]
    tm, tn = min(n_tok, 2 * ROW_TILE), _tile(w.shape[2] if slots else n_out)
    if slots:
        per = w.shape[2] // tn
        w_spec = pl.BlockSpec((None, d, tn), lambda i, j: (j // per, 0, j % per))
    else:
        w_spec = pl.BlockSpec((d, tn), lambda i, j: (0, j))

    def body(h_ref, nw_ref, w_ref, y_ref, *outs):
        @pl.when(pl.program_id(1) == 0)
        def _():
            y_ref[...] = _rmsnorm(h_ref[...], nw_ref[...]).astype(BF16)

        acc = jnp.dot(y_ref[...], w_ref[...], preferred_element_type=F32)
        outs[0][...] = (jnp.maximum(acc, 0.0) if relu else acc).astype(BF16)

    o_spec = pl.BlockSpec((tm, tn), lambda i, j: (i, j))
    outs = (jax.ShapeDtypeStruct((n_tok, n_out), BF16),)
    return _pcall(
        body, name=name, grid=(n_tok // tm, n_out // tn),
        in_specs=[pl.BlockSpec((tm, d), lambda i, j: (i, 0)), pl.BlockSpec((1, d), lambda i, j: (0, 0)), w_spec],
        out_specs=[pl.BlockSpec((tm, d), lambda i, j: (i, 0))] + [o_spec] * len(outs),
        out_shape=(jax.ShapeDtypeStruct((n_tok, d), BF16),) + outs,
        dims=("parallel", "arbitrary"), args=(h, nw, w), carries=carries)


def _mm_tiles(m, n, k, extra_bytes):
    tn = _tile(n)
    for tm in (1024, 512, 256, 128):
        if m % tm == 0 and 2 * (2 * tm * k + 2 * k * tn + (4 + extra_bytes) * tm * tn) <= MM_VMEM_BUDGET:
            return tm, tn
    return min(m, 128), tn


def matmul(a, b, mode, name, out_dtype=F32, extra=None, epilogue=None, shards=1, carries=(), square_a=False):
    slots = b.shape[0] if (mode == "nt" and b.ndim == 3) else 0
    if mode == "nn":
        (m, k), n = a.shape, b.shape[1]
    elif mode == "nt":
        (m, k), n = a.shape, (b.shape[1] if slots else b.shape[0])
    else:
        (k, m), n = a.shape, b.shape[1]
    tm, tn = _mm_tiles(m, n // shards, k, 0 if extra is None else extra.dtype.itemsize)
    a_spec = pl.BlockSpec((k, tm), lambda i, j: (0, i)) if mode == "tn" else pl.BlockSpec((tm, k), lambda i, j: (i, 0))
    if slots:
        kb = b.shape[2]
        b_specs = [pl.BlockSpec((None, tn, kb), lambda i, j, s=s: (s, j, 0)) for s in range(slots)]
    elif mode == "nt":
        b_specs = [pl.BlockSpec((tn, k), lambda i, j: (j, 0))]
    else:
        b_specs = [pl.BlockSpec((k, tn), lambda i, j: (0, j))]
    nb = len(b_specs)
    dims = _DIMS[mode]

    def body(*refs):
        a_ref, b_refs = refs[0], refs[1:1 + nb]
        e_ref = refs[1 + nb] if extra is not None else None
        o_ref = refs[-1]
        if slots:
            acc = None
            for s in range(slots):
                part = lax.dot_general(a_ref[:, s * kb:(s + 1) * kb].astype(BF16), b_refs[s][...].astype(BF16), dims,
                                       preferred_element_type=F32)
                acc = part if acc is None else acc + part
        else:
            av = a_ref[...].astype(BF16)
            if square_a:
                av = av * av
            acc = lax.dot_general(av, b_refs[0][...].astype(BF16), dims, preferred_element_type=F32)
        if epilogue == "add":
            acc = e_ref[...] + acc
        elif epilogue == "mul2":
            acc = acc * (2.0 * e_ref[...].astype(F32))
        o_ref[...] = acc.astype(o_ref.dtype)

    in_specs = [a_spec] + b_specs
    args = [a] + [b] * nb
    if extra is not None:
        in_specs.append(pl.BlockSpec((tm, tn), lambda i, j: (i, j)))
        args.append(extra)
    if shards > 1:
        per = n // shards // tn
        out_spec = pl.BlockSpec((None, tm, tn), lambda i, j: (j // per, i, j % per))
        out_shape = jax.ShapeDtypeStruct((shards, m, n // shards), out_dtype)
    else:
        out_spec = pl.BlockSpec((tm, tn), lambda i, j: (i, j))
        out_shape = jax.ShapeDtypeStruct((m, n), out_dtype)
    (res,), cres = _pcall(
        body, name=name, grid=(m // tm, n // tn), in_specs=in_specs, out_specs=[out_spec], out_shape=[out_shape],
        dims=("parallel", "arbitrary"), args=args, carries=carries)
    return res, cres


def matmul_norm_backward(a, b, h, nw, dres, name, carries=()):
    slots = b.shape[0] if b.ndim == 3 else 0
    n_tok, k = a.shape
    d = h.shape[1]
    tm = min(n_tok, ROW_TILE // 2)
    if slots:
        kb = b.shape[2]
        b_specs = [pl.BlockSpec((None, d, kb), lambda i, s=s: (s, 0, 0)) for s in range(slots)]
    else:
        b_specs = [pl.BlockSpec((d, k), lambda i: (0, 0))]
    nb = len(b_specs)

    def body(*refs):
        a_ref, b_refs = refs[0], refs[1:1 + nb]
        h_ref, nw_ref, dr_ref, dh_ref, dhb_ref, dnw_ref = refs[1 + nb:]

        @pl.when(pl.program_id(0) == 0)
        def _():
            dnw_ref[...] = jnp.zeros_like(dnw_ref)

        if slots:
            dy = None
            for s in range(slots):
                part = lax.dot_general(a_ref[:, s * kb:(s + 1) * kb], b_refs[s][...], _DIMS["nt"],
                                       preferred_element_type=F32)
                dy = part if dy is None else dy + part
        else:
            dy = lax.dot_general(a_ref[...], b_refs[0][...], _DIMS["nt"], preferred_element_type=F32)
        _, vjp = jax.vjp(_rmsnorm, h_ref[...], nw_ref[...])
        gh, gw = vjp(dy)
        dh = dr_ref[...] + gh
        dh_ref[...] = dh
        dhb_ref[...] = dh.astype(BF16)
        dnw_ref[...] += gw

    row = pl.BlockSpec((tm, d), lambda i: (i, 0))
    vec = pl.BlockSpec((1, d), lambda i: (0, 0))
    return _pcall(
        body, name=name, grid=(n_tok // tm,),
        in_specs=[pl.BlockSpec((tm, k), lambda i: (i, 0))] + b_specs + [row, vec, row], out_specs=[row, row, vec],
        out_shape=(jax.ShapeDtypeStruct((n_tok, d), F32), jax.ShapeDtypeStruct((n_tok, d), BF16),
                   jax.ShapeDtypeStruct((1, d), F32)),
        dims=("arbitrary",), args=[a] + [b] * nb + [h, nw, dres], carries=carries)


def _hgrn_post(o, gate, gw):
    return _rmsnorm(o, gw) * _silu(gate)


def hgrn_post_forward(o, proj, gw, name):
    n_tok, d = o.shape
    tm = min(n_tok, ROW_TILE)

    def body(o_ref, g_ref, w_ref, y_ref):
        y_ref[...] = _hgrn_post(o_ref[...], g_ref[...].astype(F32), w_ref[...]).astype(BF16)

    row = pl.BlockSpec((tm, d), lambda i: (i, 0))
    return pl.pallas_call(
        body, name=name, grid=(n_tok // tm,),
        in_specs=[row, pl.BlockSpec((tm, d), lambda i: (i, 3)), pl.BlockSpec((1, d), lambda i: (0, 0))],
        out_specs=row, out_shape=jax.ShapeDtypeStruct((n_tok, d), BF16),
        compiler_params=_cparams(("parallel",)),
    )(o, proj, gw)


def hgrn_post_backward(o, proj, gw, dy, name):
    n_tok, d = o.shape
    tm = min(n_tok, ROW_TILE)

    def body(o_ref, g_ref, w_ref, dy_ref, do_ref, dg_ref, dw_ref):
        @pl.when(pl.program_id(0) == 0)
        def _():
            dw_ref[...] = jnp.zeros_like(dw_ref)

        _, vjp = jax.vjp(_hgrn_post, o_ref[...], g_ref[...].astype(F32), w_ref[...])
        go, gg, gw_ = vjp(dy_ref[...])
        do_ref[...] = go
        dg_ref[...] = gg.astype(BF16)
        dw_ref[...] += gw_

    row = pl.BlockSpec((tm, d), lambda i: (i, 0))
    vec = pl.BlockSpec((1, d), lambda i: (0, 0))
    return pl.pallas_call(
        body, name=name, grid=(n_tok // tm,),
        in_specs=[row, pl.BlockSpec((tm, d), lambda i: (i, 3)), vec, row], out_specs=[row, row, vec],
        out_shape=(jax.ShapeDtypeStruct((n_tok, d), F32), jax.ShapeDtypeStruct((n_tok, d), BF16),
                   jax.ShapeDtypeStruct((1, d), F32)),
        compiler_params=_cparams(("arbitrary",)),
    )(o, proj, gw, dy)


def loss_head(h, nw, target):
    n_tok, d = h.shape
    tm = min(n_tok, ROW_TILE)

    def body(h_ref, nw_ref, t_ref, loss_ref, dh_ref, dhb_ref, dnw_ref):
        @pl.when(pl.program_id(0) == 0)
        def _():
            dnw_ref[...] = jnp.zeros_like(dnw_ref)
            loss_ref[...] = jnp.zeros_like(loss_ref)

        out, vjp = jax.vjp(_rmsnorm, h_ref[...], nw_ref[...])
        err = out - t_ref[...]
        part = 0.5 * jnp.sum(jnp.sum(err * err, axis=1, keepdims=True), axis=0, keepdims=True) / d
        loss_ref[...] += jnp.broadcast_to(part, loss_ref.shape)
        gh, gw = vjp(err / d)
        dh_ref[...] = gh
        dhb_ref[...] = gh.astype(BF16)
        dnw_ref[...] += gw

    row = pl.BlockSpec((tm, d), lambda i: (i, 0))
    vec = pl.BlockSpec((1, d), lambda i: (0, 0))
    return pl.pallas_call(
        body, name="loss_head", grid=(n_tok // tm,), in_specs=[row, vec, row],
        out_specs=[pl.BlockSpec((1, 128), lambda i: (0, 0)), row, row, vec],
        out_shape=(jax.ShapeDtypeStruct((1, 128), F32), jax.ShapeDtypeStruct((n_tok, d), F32),
                   jax.ShapeDtypeStruct((n_tok, d), BF16), jax.ShapeDtypeStruct((1, d), F32)),
        compiler_params=_cparams(("arbitrary",)),
    )(h, nw, target)


def _rows2d(shape):
    if len(shape) == 1:
        return (1, shape[0])
    return (math.prod(shape[:-1]), shape[-1])


def adamw(w, g, m, v, name):
    shape = w.shape
    r, c = _rows2d(shape)
    tr = r if r <= 256 else 256
    c1 = 1.0 / (1.0 - ADAM_B1 ** ADAM_STEP)
    c2 = 1.0 / (1.0 - ADAM_B2 ** ADAM_STEP)

    def body(w_ref, g_ref, m_ref, v_ref, d_ref, nm_ref, nv_ref):
        gg = g_ref[...]
        nm = ADAM_B1 * m_ref[...] + (1.0 - ADAM_B1) * gg
        nv = ADAM_B2 * v_ref[...] + (1.0 - ADAM_B2) * (gg * gg)
        d_ref[...] = -ADAM_LR * ((nm * c1) / (jnp.sqrt(nv * c2) + ADAM_EPS) + ADAM_WD * w_ref[...])
        nm_ref[...] = nm
        nv_ref[...] = nv

    spec = pl.BlockSpec((tr, c), lambda i: (i, 0))
    sds = jax.ShapeDtypeStruct((r, c), F32)
    outs = pl.pallas_call(
        body, name=name, grid=(r // tr,), in_specs=[spec] * 4, out_specs=[spec] * 3, out_shape=(sds,) * 3,
        compiler_params=_cparams(("parallel",)),
    )(w.reshape(r, c), g.reshape(r, c), m.reshape(r, c), v.reshape(r, c))
    return tuple(o.reshape(shape) for o in outs)


def add_slots(parts, name):
    s, r, c = parts.shape
    tr = r if r <= 256 else 256

    def body(p_ref, o_ref):
        acc = p_ref[0]
        for t in range(1, s):
            acc = acc + p_ref[t]
        o_ref[...] = acc

    return pl.pallas_call(
        body, name=name, grid=(r // tr,), in_specs=[pl.BlockSpec((s, tr, c), lambda i: (0, i, 0))],
        out_specs=pl.BlockSpec((tr, c), lambda i: (i, 0)), out_shape=jax.ShapeDtypeStruct((r, c), F32),
        compiler_params=_cparams(("parallel",)),
    )(parts)


BIG = ("gdn_w_in", "gdn_w_out", "hgrn_w_in", "hgrn_w_out", "mlp_w_up", "mlp_w_down")
WEIGHTS = ("gdn_w_in", "gdn_conv", "gdn_a_log", "gdn_dt_bias", "gdn_onorm", "gdn_w_out", "hgrn_w_in", "hgrn_lb_logits",
           "hgrn_gnorm", "hgrn_w_out", "norm_mix", "norm_mlp", "mlp_w_up", "mlp_w_down", "norm_final")


def _pad_lanes(v, n):
    return jnp.pad(v, [(0, 0)] * (v.ndim - 1) + [(0, n - v.shape[-1])])


_SMALL_LAYOUT = {
    "norm_mix": (0, 4, D_MODEL), "norm_mlp": (8, 4, D_MODEL), "norm_final": (16, 1, D_MODEL),
    "hgrn_lb_logits": (24, 4, D_MODEL), "gdn_onorm": (32, 2, 128), "gdn_a_log": (40, 2, HEADS),
    "gdn_dt_bias": (48, 2, HEADS), "loss": (56, 1, 128), "gdn_conv": (64, 24, D_MODEL), "hgrn_gnorm": (88, 2, D_MODEL),
}


def _pack_small(small, loss_row):
    rows = []
    for name, (first, nrow, lanes) in _SMALL_LAYOUT.items():
        v = loss_row if name == "loss" else small[name]
        v = _pad_lanes(v.reshape(nrow, -1), D_MODEL)
        rows.append(jnp.pad(v, ((0, -nrow % 8), (0, 0))))
    return jnp.concatenate(rows, axis=0)


def _unpack_small(packed, name, shape):
    first, nrow, lanes = _SMALL_LAYOUT[name]
    return packed[first:first + nrow, :lanes].reshape(shape)


def add_core_halves(g, theirs, core, name):
    s, r, c = g.shape
    r2 = r // 2
    tr = min(r2, 256)
    nb = r2 // tr

    def body(core_ref, g_ref, t_ref, o_ref):
        o_ref[...] = g_ref[...] + t_ref[...]

    grid_spec = pltpu.PrefetchScalarGridSpec(
        num_scalar_prefetch=1, grid=(s, nb),
        in_specs=[pl.BlockSpec((None, tr, c), lambda t, i, cr: (t, cr[0] * nb + i, 0)),
                  pl.BlockSpec((None, tr, c), lambda t, i, cr: (t, i, 0))],
        out_specs=pl.BlockSpec((None, tr, c), lambda t, i, cr: (t, i, 0)))
    return pl.pallas_call(body, name=name, grid_spec=grid_spec, out_shape=jax.ShapeDtypeStruct((s, r2, c), F32),
                          compiler_params=_cparams(("parallel", "parallel")))(core, g, theirs)


def add_chip_slots(slots, name):
    n_l = len(slots)
    s, r2, c = slots[0].shape
    tr = min(r2, 256)
    nb = r2 // tr

    def body(*refs):
        ins, o_ref = refs[:n_l], refs[n_l]
        for k in range(n_l):
            @pl.when(pl.program_id(0) == k)
            def _(k=k):
                acc = ins[k][0]
                for t in range(1, s):
                    acc = acc + ins[k][t]
                o_ref[...] = acc

    in_specs = [pl.BlockSpec((s, tr, c), lambda l, i, k=k: (0, jnp.where(l == k, i, 0), 0)) for k in range(n_l)]
    return pl.pallas_call(
        body, name=name, grid=(n_l, nb), in_specs=in_specs, out_specs=pl.BlockSpec((None, tr, c), lambda l, i: (l, i, 0)),
        out_shape=jax.ShapeDtypeStruct((n_l, r2, c), F32), compiler_params=_cparams(("arbitrary", "arbitrary")),
    )(*slots)


def adamw_halves(w, m, v, mine, theirs, name):
    n_l, r, c = w.shape
    r2 = r // 2
    tr = min(r2, 256)
    nb = r2 // tr
    c1 = 1.0 / (1.0 - ADAM_B1 ** ADAM_STEP)
    c2 = 1.0 / (1.0 - ADAM_B2 ** ADAM_STEP)

    def body(w_ref, m_ref, v_ref, mine_ref, theirs_ref, g_ref, d_ref, nm_ref, nv_ref):
        my_half = (pl.program_id(1) // nb) == lax.axis_index("c")
        gg = jnp.where(my_half, mine_ref[...], theirs_ref[...])
        nm = ADAM_B1 * m_ref[...] + (1.0 - ADAM_B1) * gg
        nv = ADAM_B2 * v_ref[...] + (1.0 - ADAM_B2) * (gg * gg)
        g_ref[...] = gg
        d_ref[...] = -ADAM_LR * ((nm * c1) / (jnp.sqrt(nv * c2) + ADAM_EPS) + ADAM_WD * w_ref[...])
        nm_ref[...] = nm
        nv_ref[...] = nv

    full = pl.BlockSpec((None, tr, c), lambda l, i: (l, i, 0))
    half = pl.BlockSpec((None, tr, c), lambda l, i: (l, i % nb, 0))
    sds = jax.ShapeDtypeStruct((n_l, r, c), F32)
    return pl.pallas_call(
        body, name=name, grid=(n_l, r // tr), in_specs=[full, full, full, half, half], out_specs=[full] * 4,
        out_shape=(sds,) * 4, compiler_params=_cparams(("parallel", "parallel")),
    )(w, m, v, mine, theirs)


def _layer_weight(kind, i):
    if kind == "up":
        return "mlp_w_up", i
    if kind == "down":
        return "mlp_w_down", i
    return ("gdn_w_" if i % 2 == 0 else "hgrn_w_") + kind, i // 2


def kernel(x, gdn_w_in, gdn_conv, gdn_a_log, gdn_dt_bias, gdn_onorm, gdn_w_out, hgrn_w_in, hgrn_lb_logits, hgrn_gnorm, hgrn_w_out, norm_mix, norm_mlp, mlp_w_up, mlp_w_down, norm_final, loss_target, m_gdn_w_in, m_gdn_conv, m_gdn_a_log, m_gdn_dt_bias, m_gdn_onorm, m_gdn_w_out, m_hgrn_w_in, m_hgrn_lb_logits, m_hgrn_gnorm, m_hgrn_w_out, m_norm_mix, m_norm_mlp, m_mlp_w_up, m_mlp_w_down, m_norm_final, v_gdn_w_in, v_gdn_conv, v_gdn_a_log, v_gdn_dt_bias, v_gdn_onorm, v_gdn_w_out, v_hgrn_w_in, v_hgrn_lb_logits, v_hgrn_gnorm, v_hgrn_w_out, v_norm_mix, v_norm_mlp, v_mlp_w_up, v_mlp_w_down, v_norm_final):
    p = dict(gdn_w_in=gdn_w_in, gdn_conv=gdn_conv, gdn_a_log=gdn_a_log, gdn_dt_bias=gdn_dt_bias, gdn_onorm=gdn_onorm,
             gdn_w_out=gdn_w_out, hgrn_w_in=hgrn_w_in, hgrn_lb_logits=hgrn_lb_logits, hgrn_gnorm=hgrn_gnorm,
             hgrn_w_out=hgrn_w_out, norm_mix=norm_mix, norm_mlp=norm_mlp, mlp_w_up=mlp_w_up, mlp_w_down=mlp_w_down,
             norm_final=norm_final)
    m = dict(gdn_w_in=m_gdn_w_in, gdn_conv=m_gdn_conv, gdn_a_log=m_gdn_a_log, gdn_dt_bias=m_gdn_dt_bias,
             gdn_onorm=m_gdn_onorm, gdn_w_out=m_gdn_w_out, hgrn_w_in=m_hgrn_w_in, hgrn_lb_logits=m_hgrn_lb_logits,
             hgrn_gnorm=m_hgrn_gnorm, hgrn_w_out=m_hgrn_w_out, norm_mix=m_norm_mix, norm_mlp=m_norm_mlp,
             mlp_w_up=m_mlp_w_up, mlp_w_down=m_mlp_w_down, norm_final=m_norm_final)
    v = dict(gdn_w_in=v_gdn_w_in, gdn_conv=v_gdn_conv, gdn_a_log=v_gdn_a_log, gdn_dt_bias=v_gdn_dt_bias,
             gdn_onorm=v_gdn_onorm, gdn_w_out=v_gdn_w_out, hgrn_w_in=v_hgrn_w_in, hgrn_lb_logits=v_hgrn_lb_logits,
             hgrn_gnorm=v_hgrn_gnorm, hgrn_w_out=v_hgrn_w_out, norm_mix=v_norm_mix, norm_mlp=v_norm_mlp,
             mlp_w_up=v_mlp_w_up, mlp_w_down=v_mlp_w_down, norm_final=v_norm_final)
    xi, yi, ci = _mesh_pos()
    chip = 2 * xi + yi
    core = jnp.reshape(ci, (1,)).astype(jnp.int32)
    d = D_MODEL
    bsz, t_len, _ = x.shape
    n_tok = bsz * t_len

    def shard(kind, i):
        name, idx = _layer_weight(kind, i)
        return p[name][idx].astype(BF16)

    def w_in_of(i, slots):
        if i % 2 == 0:
            return _pad_lanes(jnp.transpose(slots, (1, 0, 2)).reshape(d, GDN_IN), GDN_IN_PAD)
        return slots

    (first,) = run_carries([gather_carry([shard("in", 0), p["gdn_conv"], p["hgrn_gnorm"]])], "gather_first")
    conv = jnp.transpose(first[1], (1, 2, 0, 3)).reshape(DEPTH // 2, CONV_K, 3 * d)
    gnorm = jnp.transpose(first[2], (1, 0, 2)).reshape(DEPTH // 2, d)
    lbl = p["hgrn_lb_logits"]
    h = x.reshape(n_tok, d)
    next_in, next_out = first[0], None
    saved = []
    for i in range(DEPTH):
        j = i // 2
        w_in = w_in_of(i, next_in)
        nmix = p["norm_mix"][i][None, :]
        (y, proj), got = norm_matmul(h, nmix, w_in, False, f"in_proj{i}",
                                     [gather_carry([shard("out", 0)])] if i == 0 else [])
        if i == 0:
            next_out = got[0][0]
        ride = [gather_carry([shard("up", i), shard("down", i)])]
        if i % 2 == 0:
            al = _pad_lanes(p["gdn_a_log"][j][None, :], 128)
            dtb = _pad_lanes(p["gdn_dt_bias"][j][None, :], 128)
            on = p["gdn_onorm"][j][None, :]
            (og, ssave), got = gdn_forward(proj, conv[j], al, dtb, on, bsz, f"gdn_fwd{i}", ride)
            mix = (proj, ssave, al, dtb, on)
        else:
            (o, ssave), got = hgrn_forward(proj, lbl, i, bsz, ride)
            gn = gnorm[j][None, :]
            og = hgrn_post_forward(o, proj, gn, f"hgrn_post{i}")
            mix = (proj, ssave, o, gn)
        w_up, w_down = got[0][0], got[0][1].reshape(MLP_HIDDEN, d)
        w_out = next_out.reshape(d, d)
        h1, _ = matmul(og, w_out, "nn", f"out_proj{i}", extra=h, epilogue="add")
        nmlp = p["norm_mlp"][i][None, :]
        (z, r), got = norm_matmul(h1, nmlp, w_up, True, f"mlp_up{i}",
                                     [gather_carry([shard("in", i + 1)])] if i + 1 < DEPTH else [])
        if i + 1 < DEPTH:
            next_in = got[0][0]
        h2, got = matmul(r, w_down, "nn", f"mlp_down{i}", extra=h1, epilogue="add", square_a=True,
                         carries=[gather_carry([shard("out", i + 1)])] if i + 1 < DEPTH else [])
        if i + 1 < DEPTH:
            next_out = got[0][0]
        saved.append((h, nmix, y, mix, og, w_in, w_out, h1, nmlp, z, r, w_up, w_down))
        h = h2
    loss_row, dh, dhb, d_nf = loss_head(h, p["norm_final"][None, :], loss_target.reshape(n_tok, d))

    G = {k: [None] * DEPTH for k in ("in", "out", "up", "down")}
    P = {k: [None] * DEPTH for k in ("in", "out", "up", "down")}
    slots = {k: [None] * DEPTH for k in ("in", "out", "up", "down")}
    d_nmix, d_nmlp = [None] * DEPTH, [None] * DEPTH
    d_conv, d_alog, d_dtb, d_onorm, d_gnorm = [None] * 2, [None] * 2, [None] * 2, [None] * 2, [None] * 2
    d_lbl = jnp.zeros((DEPTH, d), F32)
    for i in reversed(range(DEPTH)):
        j = i // 2
        h_in, nmix, y, mix, og, w_in, w_out, h1, nmlp, z, r, w_up, w_down = saved[i]
        ride = [sibling_half_carry([G["in"][i + 1]])] if i + 1 < DEPTH else []
        du, got = matmul(dhb, w_down, "nt", f"d_mlp_act{i}", out_dtype=BF16, extra=r, epilogue="mul2", carries=ride)
        if i + 1 < DEPTH:
            P["in"][i + 1] = add_core_halves(G["in"][i + 1], got[0][0], core, f"add_cores_in{i + 1}")
        G["down"][i] = matmul(r, dhb, "tn", f"dw_down{i}", square_a=True)[0].reshape(N_CHIPS, -1, d)
        G["up"][i], _ = matmul(z, du, "tn", f"dw_up{i}", shards=N_CHIPS)
        (dh1, dh1b, d_nmlp[i]), got = matmul_norm_backward(du, w_up, h1, nmlp, dh, f"d_mlp_in{i}",
                                                           [sibling_half_carry([G["up"][i], G["down"][i]])])
        P["up"][i] = add_core_halves(G["up"][i], got[0][0], core, f"add_cores_up{i}")
        P["down"][i] = add_core_halves(G["down"][i], got[0][1], core, f"add_cores_down{i}")
        dog, _ = matmul(dh1b, w_out, "nt", f"d_mix_out{i}")
        G["out"][i] = matmul(og, dh1b, "tn", f"dw_out{i}")[0].reshape(N_CHIPS, -1, d)
        to_chips = [("up", i), ("down", i)] + ([("in", i + 1), ("out", i + 1)] if i + 1 < DEPTH else [])
        ride = [sibling_half_carry([G["out"][i]]), chips_carry([P[k][l] for k, l in to_chips])]
        if i % 2 == 0:
            proj, ssave, al, dtb, on = mix
            (dproj, dcw, dal, ddt, don), got = gdn_backward(proj, conv[j], al, dtb, on, ssave, dog, bsz, f"gdn_bwd{i}", ride)
            d_conv[j] = jnp.transpose(dcw, (2, 1, 0, 3)).reshape(CONV_K, 3 * d)
            d_alog[j], d_dtb[j], d_onorm[j] = dal[0, :HEADS], ddt[0, :HEADS], don[0]
        else:
            proj, ssave, o, gn = mix
            do_raw, dgate, dgn = hgrn_post_backward(o, proj, gn, dog, f"d_hgrn_post{i}")
            (dproj, dlb), got = hgrn_backward(proj, lbl, ssave, do_raw, dgate, i, bsz, ride)
            d_gnorm[j] = dgn[0]
            d_lbl = d_lbl + jnp.transpose(dlb, (1, 0, 2)).reshape(DEPTH, d)
        P["out"][i] = add_core_halves(G["out"][i], got[0][0], core, f"add_cores_out{i}")
        for (k, l), s in zip(to_chips, got[1]):
            slots[k][l] = s
        if i % 2 == 0:
            dw_in = matmul(y, dproj, "tn", f"dw_in{i}")[0][:, :GDN_IN]
            G["in"][i] = jnp.transpose(dw_in.reshape(d, N_CHIPS, GDN_IN // N_CHIPS), (1, 0, 2))
        else:
            G["in"][i], _ = matmul(y, dproj, "tn", f"dw_in{i}", shards=N_CHIPS)
        ride = [sibling_half_carry([G["in"][0]]), chips_carry([P["out"][0]])] if i == 0 else []
        (dh, dhb, d_nmix[i]), got = matmul_norm_backward(dproj, w_in, h_in, nmix, dh1, f"d_in_proj{i}", ride)
        if i == 0:
            P["in"][0] = add_core_halves(G["in"][0], got[0][0], core, "add_cores_in0")
            slots["out"][0] = got[1][0]
    grad_x = dh.reshape(x.shape)
    ((slots["in"][0],),) = run_carries([chips_carry([P["in"][0]])], "reduce_chips_last")

    by_weight = {}
    for kind in ("in", "out", "up", "down"):
        for i in range(DEPTH):
            by_weight.setdefault(_layer_weight(kind, i)[0], []).append(slots[kind][i])
    mine = {name: add_chip_slots(by_weight[name], f"add_chips_{name}") for name in BIG}
    small = {
        "gdn_conv": jnp.stack(d_conv), "gdn_a_log": jnp.stack(d_alog), "gdn_dt_bias": jnp.stack(d_dtb),
        "gdn_onorm": jnp.stack(d_onorm), "hgrn_lb_logits": d_lbl, "hgrn_gnorm": jnp.stack(d_gnorm),
        "norm_mix": jnp.concatenate(d_nmix, axis=0), "norm_mlp": jnp.concatenate(d_nmlp, axis=0), "norm_final": d_nf[0],
    }
    theirs, (blocks,) = run_carries([sibling_copy_carry([mine[name] for name in BIG]),
                                     gather_all_carry(_pack_small(small, loss_row))], "share_cores")
    theirs = dict(zip(BIG, theirs))

    total = add_slots(blocks, "add_small")
    loss = total[_SMALL_LAYOUT["loss"][0], 0]
    grads, delta, new_m, new_v = {}, {}, {}, {}
    for name in WEIGHTS:
        if name in BIG:
            grads[name], delta[name], new_m[name], new_v[name] = adamw_halves(
                p[name], m[name], v[name], mine[name], theirs[name], f"adamw_{name}")
            continue
        if name == "gdn_conv":
            full = _unpack_small(total, name, (2, CONV_K, 3 * d))
            grads[name] = lax.dynamic_slice_in_dim(full, chip * (3 * d // N_CHIPS), 3 * d // N_CHIPS, axis=2)
        elif name == "hgrn_gnorm":
            full = _unpack_small(total, name, (2, d))
            grads[name] = lax.dynamic_slice_in_dim(full, chip * (d // N_CHIPS), d // N_CHIPS, axis=1)
        else:
            grads[name] = _unpack_small(total, name, p[name].shape)
        delta[name], new_m[name], new_v[name] = adamw(p[name], grads[name], m[name], v[name], f"adamw_{name}")
    return (loss, grad_x, *[grads[n] for n in WEIGHTS], *[delta[n] for n in WEIGHTS],
            *[new_m[n] for n in WEIGHTS], *[new_v[n] for n in WEIGHTS])
```

```python
import functools
import math

import jax
import jax.numpy as jnp
from jax import lax
from jax.experimental import pallas as pl
from jax.experimental.pallas import tpu as pltpu

F32 = jnp.float32
BF16 = jnp.bfloat16
HI = lax.Precision.HIGHEST

D_MODEL = 1024
HEADS = 8
HEAD_DIM = 128
CHUNK = 64
SUB = 16
CONV_K = 4
HALO = 16
DEPTH = 4
EPS = 1e-6
MLP_HIDDEN = 4 * D_MODEL
GDN_MAIN = 4 * D_MODEL
GDN_IN = GDN_MAIN + 2 * HEADS
GDN_IN_PAD = GDN_MAIN + 128
NEG = -1e30

ADAM_LR = 0.001
ADAM_B1 = 0.9
ADAM_B2 = 0.999
ADAM_EPS = 1e-08
ADAM_WD = 0.01
ADAM_STEP = 10

VMEM_LIMIT = 48 * 1024 * 1024

MESH = pl.DeviceIdType.MESH


def _cparams(sem=None, **kw):
    if sem is not None:
        kw["dimension_semantics"] = sem
    return pltpu.CompilerParams(vmem_limit_bytes=VMEM_LIMIT, **kw)


def _iota(shape, dim):
    return lax.broadcasted_iota(jnp.int32, shape, dim)


_DIMS = {"nn": (((1,), (0,)), ((), ())), "nt": (((1,), (1,)), ((), ())), "tn": (((0,), (0,)), ((), ()))}


def _dot(a, b, mode):
    return lax.dot_general(a.astype(BF16), b.astype(BF16), _DIMS[mode], preferred_element_type=F32)


@functools.partial(jax.custom_vjp, nondiff_argnums=(2,))
def _mmx(a, b, mode):
    return _dot(a, b, mode)


def _mmx_fwd(a, b, mode):
    return _dot(a, b, mode), (a, b)


def _mmx_bwd(mode, res, g):
    a, b = res
    if mode == "nn":
        return _dot(g, b, "nt"), _dot(a, g, "tn")
    if mode == "nt":
        return _dot(g, b, "nn"), _dot(g, a, "tn")
    return _dot(b, g, "nt"), _dot(a, g, "nn")


_mmx.defvjp(_mmx_fwd, _mmx_bwd)


def _mm(a, b):
    return _mmx(a, b, "nn")


def _mm_nt(a, b):
    return _mmx(a, b, "nt")


def _mm_tn(a, b):
    return _mmx(a, b, "tn")


@functools.partial(jax.custom_vjp, nondiff_argnums=(1,))
def _roll_rows(x, d):
    return pltpu.roll(x, d, 0)


def _roll_rows_fwd(x, d):
    return pltpu.roll(x, d, 0), None


def _roll_rows_bwd(d, _, g):
    return (pltpu.roll(g, g.shape[0] - d, 0),)


_roll_rows.defvjp(_roll_rows_fwd, _roll_rows_bwd)


def _sig(x):
    return 1.0 / (1.0 + jnp.exp(-x))


@jax.custom_vjp
def _sigmoid(x):
    return _sig(x)


def _sigmoid_fwd(x):
    s = _sig(x)
    return s, s


def _sigmoid_bwd(s, g):
    return (g * s * (1.0 - s),)


_sigmoid.defvjp(_sigmoid_fwd, _sigmoid_bwd)


@jax.custom_vjp
def _silu(x):
    return x * _sig(x)


def _silu_fwd(x):
    s = _sig(x)
    return x * s, (x, s)


def _silu_bwd(res, g):
    x, s = res
    return (g * s * (1.0 + x * (1.0 - s)),)


_silu.defvjp(_silu_fwd, _silu_bwd)


@jax.custom_vjp
def _softplus(x):
    return jnp.maximum(x, 0.0) + jnp.log(1.0 + jnp.exp(-jnp.abs(x)))


def _softplus_fwd(x):
    return _softplus(x), x


def _softplus_bwd(x, g):
    return (g * _sig(x),)


_softplus.defvjp(_softplus_fwd, _softplus_bwd)


@jax.custom_vjp
def _log_sigmoid(x):
    return jnp.minimum(x, 0.0) - jnp.log(1.0 + jnp.exp(-jnp.abs(x)))


def _log_sigmoid_fwd(x):
    return _log_sigmoid(x), x


def _log_sigmoid_bwd(x, g):
    return (g * _sig(-x),)


_log_sigmoid.defvjp(_log_sigmoid_fwd, _log_sigmoid_bwd)


@jax.custom_vjp
def _logaddexp(a, b):
    return jnp.maximum(a, b) + jnp.log(1.0 + jnp.exp(-jnp.abs(a - b)))


def _logaddexp_fwd(a, b):
    return _logaddexp(a, b), (a, b)


def _unbroadcast(g, like):
    for ax in range(g.ndim):
        if like.shape[ax] == 1 and g.shape[ax] != 1:
            g = jnp.sum(g, axis=ax, keepdims=True)
    return g


def _logaddexp_bwd(res, g):
    a, b = res
    s = _sig(a - b)
    return _unbroadcast(g * s, a), _unbroadcast(g * (1.0 - s), b)


_logaddexp.defvjp(_logaddexp_fwd, _logaddexp_bwd)


def _row_to_col(row):
    n = row.shape[1]
    eye = _iota((n, n), 0) == _iota((n, n), 1)
    return jnp.sum(jnp.where(eye, jnp.broadcast_to(row, (n, n)), 0.0), axis=1, keepdims=True)


def _col_to_row(col):
    n = col.shape[0]
    eye = _iota((n, n), 0) == _iota((n, n), 1)
    return jnp.sum(jnp.where(eye, jnp.broadcast_to(col, (n, n)), 0.0), axis=0, keepdims=True)


def _pick_row(x, r):
    return jnp.sum(jnp.where(_iota(x.shape, 0) == r, x, 0.0), axis=0, keepdims=True)


def _pick_lane(x, l):
    return jnp.sum(jnp.where(_iota(x.shape, 1) == l, x, 0.0), axis=1, keepdims=True)


def _each(f, *lists):
    return [f(*t) for t in zip(*lists)]


def _unit_lower_inverse(Ls):
    n = Ls[0].shape[0]
    r, c = _iota((n, n), 0), _iota((n, n), 1)
    eye = jnp.where(r == c, 1.0, 0.0).astype(F32)
    Ld = _each(lambda L: jnp.where((r // SUB) == (c // SUB), L, 0.0), Ls)
    Lo = _each(lambda L, d: L - d, Ls, Ld)
    P = _each(lambda d: eye - d, Ld)
    Lp = Ld
    for _ in range(int(math.log2(SUB)) - 1):
        Lp = _each(lambda x: _mm(x, x), Lp)
        P = _each(lambda p, x: p + _mm(p, x), P, Lp)
    N = _each(_mm, P, Lo)
    N2 = _each(lambda x: _mm(x, x), N)
    X = _each(lambda x, x2: (eye - x) + _mm(eye - x, x2), N, N2)
    return _each(_mm, X, P)


def _shift_rows(x, halo, d):
    if d == 0:
        return x
    xr = _roll_rows(x, d)
    hr = _roll_rows(halo, d)
    hfull = jnp.concatenate([hr, jnp.zeros((x.shape[0] - HALO, x.shape[1]), F32)], axis=0)
    return jnp.where(_iota(x.shape, 0) >= d, xr, hfull)


def _causal_conv_chunk(x, halo, w):
    y = None
    for kk in range(CONV_K):
        t = _shift_rows(x, halo, CONV_K - 1 - kk) * _pick_row(w, kk)
        y = t if y is None else y + t
    return y


def _gdn_chunk(xq, xk, xv, hq, hk, hv, gate, ab, S, cwq, cwk, cwv, alog, dtb, onorm, *, heads, seqs):
    C = xq[0].shape[0]
    q = _each(lambda x, h, w: _silu(_causal_conv_chunk(x, h, w)), xq, hq, cwq)
    k = _each(lambda x, h, w: _silu(_causal_conv_chunk(x, h, w)), xk, hk, cwk)
    v = _each(lambda x, h, w: _silu(_causal_conv_chunk(x, h, w)), xv, hv, cwv)
    q = _each(lambda t: t * lax.rsqrt(jnp.sum(t * t, axis=1, keepdims=True) + EPS) * (HEAD_DIM ** -0.5), q)
    k = _each(lambda t: t * lax.rsqrt(jnp.sum(t * t, axis=1, keepdims=True) + EPS), k)
    beta_all = _each(_sigmoid, ab)
    g_all = _each(lambda t: -jnp.exp(alog) * _softplus(t + dtb), ab)
    beta = [_pick_lane(beta_all[b], HEADS + h) for b, h in zip(seqs, heads)]
    g = [_pick_lane(g_all[b], h) for b, h in zip(seqs, heads)]
    r, c = _iota((C, C), 0), _iota((C, C), 1)
    gc = _each(lambda t: jnp.sum(jnp.where(c <= r, jnp.broadcast_to(_col_to_row(t), (C, C)), 0.0), axis=1,
                                 keepdims=True), g)
    gc_row = _each(lambda t: jnp.sum(jnp.where(r <= c, jnp.broadcast_to(t, (C, C)), 0.0), axis=0, keepdims=True), g)
    decay = _each(lambda a, b: jnp.exp(jnp.where(r >= c, a - b, NEG)), gc, gc_row)
    kb = _each(lambda a, b: a * b, k, beta)
    L = _each(lambda a, b, d: jnp.where(r > c, _mm_nt(a, b) * d, 0.0), kb, k, decay)
    A = _each(lambda a, b, d: jnp.where(r >= c, _mm_nt(a, b) * d, 0.0), q, k, decay)
    T = _unit_lower_inverse(L)
    egc = _each(jnp.exp, gc)
    u = _each(lambda t, a, b: _mm(t, a * b), T, v, beta)
    w = _each(lambda t, a, e: _mm(t, a * e), T, kb, egc)
    gl = _each(lambda t: _pick_row(t, C - 1), gc)
    v_new = _each(lambda a, b, s: a - _mm(b, s), u, w, S)
    o = _each(lambda a, e, s, m, vn: _mm(a * e, s) + _mm(m, vn), q, egc, S, A, v_new)
    S_next = _each(lambda s, l, a, t, vn: s * jnp.exp(l) + _mm_tn(a * jnp.exp(l - t), vn), S, gl, k, gc, v_new)
    o = _each(lambda t, gt: t * lax.rsqrt(jnp.mean(t * t, axis=1, keepdims=True) + EPS) * onorm * _silu(gt), o, gate)
    return o, S_next


def _hgrn_lower_bound(lbl, layer):
    e = jnp.exp(lbl - jnp.max(lbl, axis=0, keepdims=True))
    sm = e / jnp.sum(e, axis=0, keepdims=True)
    r = _iota(lbl.shape, 0)
    return jnp.sum(jnp.where((r >= 1) & (r <= layer), sm, 0.0), axis=0, keepdims=True)


_LEVELS = (1, 2, 4, 8, 16, 32)


def _prefix_matrix(n):
    r, c = _iota((n, n), 0), _iota((n, n), 1)
    parts = [jnp.where(c <= r, 1.0, 0.0)]
    for s in _LEVELS:
        parts.append(jnp.where(c < (r // (2 * s)) * (2 * s) + s, 1.0, 0.0))
    return jnp.concatenate(parts, axis=0).astype(F32)


def _prefix_sums_of(x):
    n = x.shape[0]
    y = lax.dot_general(_prefix_matrix(n), x, _DIMS["nn"], precision=HI, preferred_element_type=F32)
    return tuple(y[t * n:(t + 1) * n] for t in range(len(_LEVELS) + 1))


@jax.custom_vjp
def _prefix_sums(x):
    return _prefix_sums_of(x)


def _prefix_sums_fwd(x):
    return _prefix_sums_of(x), None


def _prefix_sums_bwd(_, gs):
    g = jnp.concatenate(gs, axis=0)
    return (lax.dot_general(_prefix_matrix(gs[0].shape[0]), g, _DIMS["tn"], precision=HI, preferred_element_type=F32),)


_prefix_sums.defvjp(_prefix_sums_fwd, _prefix_sums_bwd)


def _hgrn_chunk(qp, fp, v, S, lbl, *, layer):
    C = qp[0].shape[0]
    lb = _each(lambda l: _hgrn_lower_bound(l, layer), lbl)
    lf = _each(lambda l, f: _logaddexp(jnp.log(l), jnp.log(1.0 - l) + _log_sigmoid(f)), lb, fp)
    k = _each(lambda l, f: (1.0 - l) * _sigmoid(-f), lb, fp)
    q = _each(lambda x: _silu(x) * (HEAD_DIM ** -0.5), qp)
    r, c = _iota((C, C), 0), _iota((C, C), 1)
    row = _iota(qp[0].shape, 0)
    sums = _each(_prefix_sums, lf)
    gc = [t[0] for t in sums]
    a = _each(lambda x, y: jnp.where(r == c, _mm_nt(x, y), 0.0), q, k)
    for n, s in enumerate(_LEVELS):
        ref = [t[n + 1] for t in sums]
        upper = (row % (2 * s)) >= s
        same = (r // (2 * s)) == (c // (2 * s))
        q_s = _each(lambda x, g, m: x * jnp.exp(jnp.where(upper, g - m, NEG)), q, gc, ref)
        k_s = _each(lambda x, g, m: x * jnp.exp(jnp.where(upper, NEG, m - g)), k, gc, ref)
        a = _each(lambda t, x, y: t + jnp.where(same, _mm_nt(x, y), 0.0), a, q_s, k_s)
    o = _each(lambda t, x, g, vv, st: _mm(t, vv) + _mm(x * jnp.exp(g), st), a, q, gc, v, S)
    gl = _each(lambda g: _pick_row(g, C - 1), gc)
    S_next = _each(lambda st, l, x, g, vv: st * _row_to_col(jnp.exp(l)) + _mm_tn(x * jnp.exp(l - g), vv),
                   S, gl, k, gc, v)
    return o, S_next


N_CHIPS = 4
N_DEV = 8
_ANY = pl.BlockSpec(memory_space=pl.ANY)


def _mesh_pos():
    return lax.axis_index("x"), lax.axis_index("y"), lax.axis_index("c")


def _other_chips(x, y):
    ps = [(1 - x, y), (x, 1 - y), (1 - x, 1 - y)]
    return [(p, 2 * p[0] + p[1]) for p in ps]


def _remote(src, dst, send_sem, recv_sem, dev):
    return pltpu.make_async_remote_copy(src_ref=src, dst_ref=dst, send_sem=send_sem, recv_sem=recv_sem,
                                        device_id=dev, device_id_type=MESH)


class Carry:
    def __init__(self, ins, out_shapes, sems, start, finish):
        self.ins, self.out_shapes, self.sems, self.start, self.finish = list(ins), list(out_shapes), list(sems), start, finish


def _pcall(body, *, name, grid, in_specs, out_specs, out_shape, scratch_shapes=(), dims, args, carries=()):
    in_specs, out_specs, out_shape = list(in_specs), list(out_specs), list(out_shape)
    scratch_shapes, args = list(scratch_shapes), list(args)
    n_in, n_out, n_scr = len(in_specs), len(out_shape), len(scratch_shapes)
    carries = [c for c in carries if c is not None]
    if not carries:
        res = pl.pallas_call(body, name=name, grid=grid, in_specs=in_specs, out_specs=out_specs, out_shape=out_shape,
                             scratch_shapes=scratch_shapes, compiler_params=_cparams(dims))(*args)
        return list(res), []
    ci = [len(c.ins) for c in carries]
    co = [len(c.out_shapes) for c in carries]
    cs = [len(c.sems) for c in carries]

    def split(seq, sizes):
        out, k = [], 0
        for s in sizes:
            out.append(seq[k:k + s])
            k += s
        return out

    def carried(*refs):
        ins, cins, outs, couts, scr, sems = split(refs, [n_in, sum(ci), n_out, sum(co), n_scr, sum(cs)])
        cins, couts, sems = split(cins, ci), split(couts, co), split(sems, cs)
        ids = [pl.program_id(a) for a in range(len(grid))]
        first, last = ids[0] == 0, ids[0] == grid[0] - 1
        for a in range(1, len(grid)):
            first, last = first & (ids[a] == 0), last & (ids[a] == grid[a] - 1)

        @pl.when(first)
        def _():
            for c, i, o, s in zip(carries, cins, couts, sems):
                c.start(i, o, s)

        body(*ins, *outs, *scr)

        @pl.when(last)
        def _():
            for c, i, o, s in zip(carries, cins, couts, sems):
                c.finish(i, o, s)

    res = pl.pallas_call(
        carried, name=name, grid=grid,
        in_specs=in_specs + [_ANY] * sum(ci), out_specs=out_specs + [_ANY] * sum(co),
        out_shape=out_shape + [s for c in carries for s in c.out_shapes],
        scratch_shapes=scratch_shapes + [s for c in carries for s in c.sems],
        compiler_params=_cparams(("arbitrary",) * len(grid)),
    )(*args, *[a for c in carries for a in c.ins])
    return list(res[:n_out]), split(list(res[n_out:]), co)


def run_carries(carries, name):
    ci = [len(c.ins) for c in carries]
    co = [len(c.out_shapes) for c in carries]
    cs = [len(c.sems) for c in carries]

    def split(seq, sizes):
        out, k = [], 0
        for s in sizes:
            out.append(seq[k:k + s])
            k += s
        return out

    def body(*refs):
        cins, couts, sems = split(refs, [sum(ci), sum(co), sum(cs)])
        cins, couts, sems = split(cins, ci), split(couts, co), split(sems, cs)
        for c, i, o, s in zip(carries, cins, couts, sems):
            c.start(i, o, s)
        for c, i, o, s in zip(carries, cins, couts, sems):
            c.finish(i, o, s)

    res = pl.pallas_call(
        body, name=name, in_specs=[_ANY] * sum(ci), out_specs=[_ANY] * sum(co),
        out_shape=[s for c in carries for s in c.out_shapes], scratch_shapes=[s for c in carries for s in c.sems],
    )(*[a for c in carries for a in c.ins])
    return split(list(res), co)


def gather_carry(arrs):
    n = len(arrs)
    split = [a.ndim == 2 and a.shape[0] % 32 == 0 for a in arrs]

    def plan(ins, outs, sems):
        send_sems, recv_sems, pass_send, pass_recv, local_sems = sems
        x, y, c = _mesh_pos()
        me = 2 * x + y
        peers = _other_chips(x, y)
        locs = [pltpu.make_async_copy(ins[a], outs[a].at[me], local_sems.at[a]) for a in range(n)]
        sends, recvs, passes, pass_recvs = [], [], [], []
        for a in range(n):
            half = arrs[a].shape[0] // 2
            mine, other = pl.ds(c * half, half), pl.ds((1 - c) * half, half)
            for j, ((px, py), t) in enumerate(peers):
                sem = (send_sems.at[a, j], recv_sems.at[a, j])
                if split[a]:
                    sends.append(_remote(ins[a].at[mine], outs[a].at[me, mine], *sem, (px, py, c)))
                    recvs.append(_remote(ins[a].at[mine], outs[a].at[t, mine], *sem, (px, py, c)))
                    psem = (pass_send.at[a, j], pass_recv.at[a, j])
                    passes.append(_remote(outs[a].at[t, mine], outs[a].at[t, mine], *psem, (x, y, 1 - c)))
                    pass_recvs.append(_remote(outs[a].at[t, other], outs[a].at[t, other], *psem, (x, y, 1 - c)))
                else:
                    sends.append(_remote(ins[a], outs[a].at[me], *sem, (px, py, c)))
                    recvs.append(_remote(ins[a], outs[a].at[t], *sem, (px, py, c)))
                    passes.append(None)
                    pass_recvs.append(None)
        return locs, sends, recvs, passes, pass_recvs

    def start(ins, outs, sems):
        locs, sends, _, _, _ = plan(ins, outs, sems)
        for cp in locs + sends:
            cp.start()

    def finish(ins, outs, sems):
        locs, sends, recvs, passes, pass_recvs = plan(ins, outs, sems)
        for r, p in zip(recvs, passes):
            r.wait_recv()
            if p is not None:
                p.start()
        for p in pass_recvs:
            if p is not None:
                p.wait_recv()
        for r in sends + [p for p in passes if p is not None]:
            r.wait_send()
        for cp in locs:
            cp.wait()

    return Carry(arrs, [jax.ShapeDtypeStruct((N_CHIPS,) + a.shape, a.dtype) for a in arrs],
                 [pltpu.SemaphoreType.DMA((n, 3)), pltpu.SemaphoreType.DMA((n, 3)), pltpu.SemaphoreType.DMA((n, 3)),
                  pltpu.SemaphoreType.DMA((n, 3)), pltpu.SemaphoreType.DMA((n,))], start, finish)


def sibling_half_carry(gs):
    n = len(gs)

    def copies(ins, outs, sems):
        send_sems, recv_sems = sems
        x, y, c = _mesh_pos()
        out = []
        for a in range(n):
            half = gs[a].shape[1] // 2
            out.append(_remote(ins[a].at[:, pl.ds((1 - c) * half, half), :], outs[a], send_sems.at[a], recv_sems.at[a],
                               (x, y, 1 - c)))
        return out

    def start(ins, outs, sems):
        for r in copies(ins, outs, sems):
            r.start()

    def finish(ins, outs, sems):
        cps = copies(ins, outs, sems)
        for r in cps:
            r.wait_recv()
        for r in cps:
            r.wait_send()

    return Carry(gs, [jax.ShapeDtypeStruct((g.shape[0], g.shape[1] // 2, g.shape[2]), g.dtype) for g in gs],
                 [pltpu.SemaphoreType.DMA((n,)), pltpu.SemaphoreType.DMA((n,))], start, finish)


def chips_carry(ps):
    n = len(ps)

    def copies(ins, outs, sems):
        send_sems, recv_sems, local_sems = sems
        x, y, c = _mesh_pos()
        me = 2 * x + y
        peers = _other_chips(x, y)
        locs = [pltpu.make_async_copy(ins[a].at[me], outs[a].at[me], local_sems.at[a]) for a in range(n)]
        sends = [_remote(ins[a].at[t], outs[a].at[me], send_sems.at[a, j], recv_sems.at[a, j], (px, py, c))
                 for a in range(n) for j, ((px, py), t) in enumerate(peers)]
        recvs = [_remote(ins[a].at[t], outs[a].at[t], send_sems.at[a, j], recv_sems.at[a, j], (px, py, c))
                 for a in range(n) for j, ((px, py), t) in enumerate(peers)]
        return locs, sends, recvs

    def start(ins, outs, sems):
        locs, sends, _ = copies(ins, outs, sems)
        for cp in locs + sends:
            cp.start()

    def finish(ins, outs, sems):
        locs, sends, recvs = copies(ins, outs, sems)
        for r in recvs:
            r.wait_recv()
        for r in sends:
            r.wait_send()
        for cp in locs:
            cp.wait()

    return Carry(ps, [jax.ShapeDtypeStruct(p.shape, p.dtype) for p in ps],
                 [pltpu.SemaphoreType.DMA((n, 3)), pltpu.SemaphoreType.DMA((n, 3)), pltpu.SemaphoreType.DMA((n,))],
                 start, finish)


def sibling_copy_carry(arrs):
    n = len(arrs)

    def copies(ins, outs, sems):
        send_sems, recv_sems = sems
        x, y, c = _mesh_pos()
        return [_remote(ins[a], outs[a], send_sems.at[a], recv_sems.at[a], (x, y, 1 - c)) for a in range(n)]

    def start(ins, outs, sems):
        for r in copies(ins, outs, sems):
            r.start()

    def finish(ins, outs, sems):
        cps = copies(ins, outs, sems)
        for r in cps:
            r.wait_recv()
        for r in cps:
            r.wait_send()

    return Carry(arrs, [jax.ShapeDtypeStruct(a.shape, a.dtype) for a in arrs],
                 [pltpu.SemaphoreType.DMA((n,)), pltpu.SemaphoreType.DMA((n,))], start, finish)


def gather_all_carry(small):
    flips = [(fx, fy, fc) for fx in (0, 1) for fy in (0, 1) for fc in (0, 1)][1:]

    def copies(ins, outs, sems):
        send_sems, recv_sems, local_sem = sems
        in_ref, out_ref = ins[0], outs[0]
        x, y, c = _mesh_pos()
        me = 4 * x + 2 * y + c
        peers = [((1 - x) if fx else x, (1 - y) if fy else y, (1 - c) if fc else c) for fx, fy, fc in flips]
        loc = pltpu.make_async_copy(in_ref, out_ref.at[me], local_sem.at[0])
        sends = [_remote(in_ref, out_ref.at[me], send_sems.at[k], recv_sems.at[k], p) for k, p in enumerate(peers)]
        recvs = [_remote(in_ref, out_ref.at[4 * p[0] + 2 * p[1] + p[2]], send_sems.at[k], recv_sems.at[k], p)
                 for k, p in enumerate(peers)]
        return loc, sends, recvs

    def start(ins, outs, sems):
        loc, sends, _ = copies(ins, outs, sems)
        loc.start()
        for r in sends:
            r.start()

    def finish(ins, outs, sems):
        loc, sends, recvs = copies(ins, outs, sems)
        for r in recvs:
            r.wait_recv()
        for r in sends:
            r.wait_send()
        loc.wait()

    return Carry([small], [jax.ShapeDtypeStruct((N_DEV,) + small.shape, small.dtype)],
                 [pltpu.SemaphoreType.DMA((N_DEV - 1,)), pltpu.SemaphoreType.DMA((N_DEV - 1,)),
                  pltpu.SemaphoreType.DMA((1,))], start, finish)


def _chunk_spec(bsz, nc, col, rev=False, rows=CHUNK, width=D_MODEL):
    return pl.BlockSpec((bsz, rows, width), lambda n: (0, (nc - 1 - n) if rev else n, col))


def _halo_spec(bsz, nc, col, rev):
    per = CHUNK // HALO
    return pl.BlockSpec((bsz, HALO, D_MODEL),
                        lambda n: (0, jnp.maximum(((nc - 1 - n) if rev else n) * per - 1, 0), col))


def _const_spec(shape):
    nd = len(shape)
    return pl.BlockSpec(shape, lambda n: (0,) * nd)


def _state_spec(bsz, nc, rev=False):
    return pl.BlockSpec((bsz, None, HEADS, HEAD_DIM, HEAD_DIM), lambda n: (0, (nc - 1 - n) if rev else n, 0, 0, 0))


def _lanes(hh):
    return slice(hh * HEAD_DIM, (hh + 1) * HEAD_DIM)


def _chains(bsz):
    return [(b, hh) for b in range(bsz) for hh in range(HEADS)]


def _per_chain(ref, bsz, scale=None):
    vals = [ref[b, :, _lanes(hh)].astype(F32) for b, hh in _chains(bsz)]
    return vals if scale is None else [v * scale for v in vals]


def _per_head(ref, bsz):
    return [ref[:, _lanes(hh)].astype(F32) for _, hh in _chains(bsz)]


def gdn_forward(proj, conv_w, alog, dtb, onorm, bsz, name, carries=()):
    n_tok = proj.shape[0]
    t_len = n_tok // bsz
    nc = t_len // CHUNK
    chains = _chains(bsz)

    def body(xq, xk, xv, hq, hk, hv, gate, ab, cwq, cwk, cwv, al, dt, on, o_ref, ssave_ref, s_ref):
        n = pl.program_id(0)
        keep = jnp.where(n > 0, 1.0, 0.0).astype(F32)

        @pl.when(n == 0)
        def _():
            s_ref[...] = jnp.zeros_like(s_ref)

        S = [s_ref[b, hh] for b, hh in chains]
        for (b, hh), s in zip(chains, S):
            ssave_ref[b, hh] = s
        outs, s_next = _gdn_chunk(
            _per_chain(xq, bsz), _per_chain(xk, bsz), _per_chain(xv, bsz), _per_chain(hq, bsz, keep),
            _per_chain(hk, bsz, keep), _per_chain(hv, bsz, keep), _per_chain(gate, bsz),
            [ab[b].astype(F32) for b in range(bsz)], S, _per_head(cwq, bsz), _per_head(cwk, bsz), _per_head(cwv, bsz),
            al[...], dt[...], on[...], heads=[hh for _, hh in chains], seqs=[b for b, _ in chains])
        for (b, hh), o, s in zip(chains, outs, s_next):
            o_ref[b, :, _lanes(hh)] = o.astype(o_ref.dtype)
            s_ref[b, hh] = s

    cw_spec = lambda col: pl.BlockSpec((CONV_K, D_MODEL), lambda n: (0, col))
    in_specs = [_chunk_spec(bsz, nc, 0), _chunk_spec(bsz, nc, 1), _chunk_spec(bsz, nc, 2),
                _halo_spec(bsz, nc, 0, False), _halo_spec(bsz, nc, 1, False), _halo_spec(bsz, nc, 2, False),
                _chunk_spec(bsz, nc, 3), _chunk_spec(bsz, nc, GDN_MAIN // 128, width=128),
                cw_spec(0), cw_spec(1), cw_spec(2),
                _const_spec((1, 128)), _const_spec((1, 128)), _const_spec((1, 128))]
    out_specs = [_chunk_spec(bsz, nc, 0), _state_spec(bsz, nc)]
    p3 = proj.reshape(bsz, t_len, -1)
    (o, ssave), got = _pcall(
        body, name=name, grid=(nc,), in_specs=in_specs, out_specs=out_specs,
        out_shape=(jax.ShapeDtypeStruct((bsz, t_len, D_MODEL), BF16),
                   jax.ShapeDtypeStruct((bsz, nc, HEADS, HEAD_DIM, HEAD_DIM), F32)),
        scratch_shapes=[pltpu.VMEM((bsz, HEADS, HEAD_DIM, HEAD_DIM), F32)], dims=("arbitrary",),
        args=(p3, p3, p3, p3, p3, p3, p3, p3, conv_w, conv_w, conv_w, alog, dtb, onorm), carries=carries)
    return (o.reshape(n_tok, D_MODEL), ssave), got


def gdn_backward(proj, conv_w, alog, dtb, onorm, s_saved, d_out, bsz, name, carries=()):
    n_tok = proj.shape[0]
    t_len = n_tok // bsz
    nc = t_len // CHUNK
    chains = _chains(bsz)

    def body(xq, xk, xv, hq, hk, hv, gate, ab, cwq, cwk, cwv, al, dt, on, ssave, do,
             dp_ref, dcw_ref, dal_ref, ddt_ref, don_ref, ds_ref, dhalo_ref):
        n = pl.program_id(0)
        nr = nc - 1 - n

        @pl.when(n == 0)
        def _():
            dcw_ref[...] = jnp.zeros_like(dcw_ref)
            dal_ref[...] = jnp.zeros_like(dal_ref)
            ddt_ref[...] = jnp.zeros_like(ddt_ref)
            don_ref[...] = jnp.zeros_like(don_ref)
            ds_ref[...] = jnp.zeros_like(ds_ref)
            dhalo_ref[...] = jnp.zeros_like(dhalo_ref)

        keep = jnp.where(nr > 0, 1.0, 0.0).astype(F32)
        pad = jnp.zeros((CHUNK - HALO, HEAD_DIM), F32)
        args = (_per_chain(xq, bsz), _per_chain(xk, bsz), _per_chain(xv, bsz), _per_chain(hq, bsz, keep),
                _per_chain(hk, bsz, keep), _per_chain(hv, bsz, keep), _per_chain(gate, bsz),
                [ab[b].astype(F32) for b in range(bsz)], [ssave[b, hh] for b, hh in chains],
                _per_head(cwq, bsz), _per_head(cwk, bsz), _per_head(cwv, bsz), al[...], dt[...], on[...])
        _, vjp = jax.vjp(functools.partial(_gdn_chunk, heads=[hh for _, hh in chains], seqs=[b for b, _ in chains]),
                         *args)
        (gxq, gxk, gxv, ghq, ghk, ghv, ggate, gab, gS, gcq, gck, gcv, gal, gdt, gon) = vjp(
            (_per_chain(do, bsz), [ds_ref[b, hh] for b, hh in chains]))
        d = D_MODEL
        for e, (b, hh) in enumerate(chains):
            c0 = hh * HEAD_DIM
            for part, (gx, gh, gc) in enumerate(((gxq, ghq, gcq), (gxk, ghk, gck), (gxv, ghv, gcv))):
                full = gx[e] + jnp.concatenate([pad, dhalo_ref[b, hh, part]], axis=0)
                dp_ref[b, :, part * d + c0:part * d + c0 + HEAD_DIM] = full.astype(dp_ref.dtype)
                dhalo_ref[b, hh, part] = gh[e] * keep
                dcw_ref[hh, part] += gc[e]
            dp_ref[b, :, 3 * d + c0:3 * d + c0 + HEAD_DIM] = ggate[e].astype(dp_ref.dtype)
            ds_ref[b, hh] = gS[e]
        for b in range(bsz):
            dp_ref[b, :, GDN_MAIN:GDN_IN_PAD] = gab[b].astype(dp_ref.dtype)
        dal_ref[...] += gal
        ddt_ref[...] += gdt
        don_ref[...] += gon

    cw_spec = lambda col: pl.BlockSpec((CONV_K, D_MODEL), lambda n: (0, col))
    in_specs = [_chunk_spec(bsz, nc, 0, True), _chunk_spec(bsz, nc, 1, True), _chunk_spec(bsz, nc, 2, True),
                _halo_spec(bsz, nc, 0, True), _halo_spec(bsz, nc, 1, True), _halo_spec(bsz, nc, 2, True),
                _chunk_spec(bsz, nc, 3, True), _chunk_spec(bsz, nc, GDN_MAIN // 128, True, width=128),
                cw_spec(0), cw_spec(1), cw_spec(2),
                _const_spec((1, 128)), _const_spec((1, 128)), _const_spec((1, 128)),
                _state_spec(bsz, nc, True), _chunk_spec(bsz, nc, 0, True)]
    out_specs = [_chunk_spec(bsz, nc, 0, True, width=GDN_IN_PAD),
                 _const_spec((HEADS, 3, CONV_K, HEAD_DIM)), _const_spec((1, 128)), _const_spec((1, 128)),
                 _const_spec((1, 128))]
    row = jax.ShapeDtypeStruct((1, 128), F32)
    p3 = proj.reshape(bsz, t_len, -1)
    (dp, dcw, dal, ddt, don), got = _pcall(
        body, name=name, grid=(nc,), in_specs=in_specs, out_specs=out_specs,
        out_shape=(jax.ShapeDtypeStruct((bsz, t_len, GDN_IN_PAD), BF16),
                   jax.ShapeDtypeStruct((HEADS, 3, CONV_K, HEAD_DIM), F32), row, row, row),
        scratch_shapes=[pltpu.VMEM((bsz, HEADS, HEAD_DIM, HEAD_DIM), F32),
                        pltpu.VMEM((bsz, HEADS, 3, HALO, HEAD_DIM), F32)],
        dims=("arbitrary",),
        args=(p3, p3, p3, p3, p3, p3, p3, p3, conv_w, conv_w, conv_w, alog, dtb, onorm, s_saved,
              d_out.reshape(bsz, t_len, D_MODEL)),
        carries=carries)
    return (dp.reshape(n_tok, GDN_IN_PAD), dcw, dal, ddt, don), got


def hgrn_forward(proj, lbl, layer, bsz, carries=()):
    n_tok = proj.shape[0]
    t_len = n_tok // bsz
    nc = t_len // CHUNK
    chains = _chains(bsz)

    def body(qp, fp, vi, lb, o_ref, ssave_ref, s_ref):
        @pl.when(pl.program_id(0) == 0)
        def _():
            s_ref[...] = jnp.zeros_like(s_ref)

        S = [s_ref[b, hh] for b, hh in chains]
        for (b, hh), s in zip(chains, S):
            ssave_ref[b, hh] = s
        outs, s_next = _hgrn_chunk(_per_chain(qp, bsz), _per_chain(fp, bsz), _per_chain(vi, bsz), S,
                                   _per_head(lb, bsz), layer=layer)
        for (b, hh), o, s in zip(chains, outs, s_next):
            o_ref[b, :, _lanes(hh)] = o
            s_ref[b, hh] = s

    in_specs = [_chunk_spec(bsz, nc, 0), _chunk_spec(bsz, nc, 1), _chunk_spec(bsz, nc, 2),
                _const_spec((DEPTH, D_MODEL))]
    out_specs = [_chunk_spec(bsz, nc, 0), _state_spec(bsz, nc)]
    p3 = proj.reshape(bsz, t_len, -1)
    (o, ssave), got = _pcall(
        body, name=f"hgrn_fwd{layer}", grid=(nc,), in_specs=in_specs, out_specs=out_specs,
        out_shape=(jax.ShapeDtypeStruct((bsz, t_len, D_MODEL), F32),
                   jax.ShapeDtypeStruct((bsz, nc, HEADS, HEAD_DIM, HEAD_DIM), F32)),
        scratch_shapes=[pltpu.VMEM((bsz, HEADS, HEAD_DIM, HEAD_DIM), F32)], dims=("arbitrary",),
        args=(p3, p3, p3, lbl), carries=carries)
    return (o.reshape(n_tok, D_MODEL), ssave), got


def hgrn_backward(proj, lbl, s_saved, d_o, d_gate, layer, bsz, carries=()):
    n_tok = proj.shape[0]
    t_len = n_tok // bsz
    nc = t_len // CHUNK
    chains = _chains(bsz)

    def body(qp, fp, vi, lb, ssave, do, dgt, dp_ref, dlb_ref, ds_ref):
        @pl.when(pl.program_id(0) == 0)
        def _():
            dlb_ref[...] = jnp.zeros_like(dlb_ref)
            ds_ref[...] = jnp.zeros_like(ds_ref)

        _, vjp = jax.vjp(functools.partial(_hgrn_chunk, layer=layer), _per_chain(qp, bsz), _per_chain(fp, bsz),
                         _per_chain(vi, bsz), [ssave[b, hh] for b, hh in chains], _per_head(lb, bsz))
        gq, gf, gv, gS, glb = vjp((_per_chain(do, bsz), [ds_ref[b, hh] for b, hh in chains]))
        d = D_MODEL
        for e, (b, hh) in enumerate(chains):
            c0 = hh * HEAD_DIM
            dp_ref[b, :, c0:c0 + HEAD_DIM] = gq[e].astype(dp_ref.dtype)
            dp_ref[b, :, d + c0:d + c0 + HEAD_DIM] = gf[e].astype(dp_ref.dtype)
            dp_ref[b, :, 2 * d + c0:2 * d + c0 + HEAD_DIM] = gv[e].astype(dp_ref.dtype)
            dlb_ref[hh] += glb[e]
            ds_ref[b, hh] = gS[e]
        dp_ref[:, :, 3 * d:4 * d] = dgt[...]

    in_specs = [_chunk_spec(bsz, nc, 0, True), _chunk_spec(bsz, nc, 1, True), _chunk_spec(bsz, nc, 2, True),
                _const_spec((DEPTH, D_MODEL)), _state_spec(bsz, nc, True),
                _chunk_spec(bsz, nc, 0, True), _chunk_spec(bsz, nc, 0, True)]
    out_specs = [_chunk_spec(bsz, nc, 0, True, width=4 * D_MODEL), _const_spec((HEADS, DEPTH, HEAD_DIM))]
    p3 = proj.reshape(bsz, t_len, -1)
    (dp, dlb), got = _pcall(
        body, name=f"hgrn_bwd{layer}", grid=(nc,), in_specs=in_specs, out_specs=out_specs,
        out_shape=(jax.ShapeDtypeStruct((bsz, t_len, 4 * D_MODEL), BF16),
                   jax.ShapeDtypeStruct((HEADS, DEPTH, HEAD_DIM), F32)),
        scratch_shapes=[pltpu.VMEM((bsz, HEADS, HEAD_DIM, HEAD_DIM), F32)], dims=("arbitrary",),
        args=(p3, p3, p3, lbl, s_saved, d_o.reshape(bsz, t_len, D_MODEL), d_gate.reshape(bsz, t_len, D_MODEL)),
        carries=carries)
    return (dp.reshape(n_tok, 4 * D_MODEL), dlb), got


ROW_TILE = 512
MM_VMEM_BUDGET = 36 * 1024 * 1024


def _tile(n):
    for cand in (1024, 512, 1408, 384, 256, 128):
        if n % cand == 0:
            return cand
    return n


def _rmsnorm(x, w):
    return x * lax.rsqrt(jnp.mean(x * x, axis=1, keepdims=True) + EPS) * w


def norm_matmul(h, nw, w, relu, name, carries=()):
    n_tok, d = h.shape
    slots = w.shape[0] if w.ndim == 3 else 0
    nb = w.shape[2] if slots else w.shape[1]
    n_out = max(slots, 1) * nb
    tm = min(n_tok, ROW_TILE)
    if slots:
        w_specs = [pl.BlockSpec((None, d, nb), lambda i, s=s: (s, 0, 0)) for s in range(slots)]
    else:
        w_specs = [pl.BlockSpec((d, nb), lambda i: (0, 0))]
    nw_ = len(w_specs)

    def body(*refs):
        h_ref, nw_ref, w_refs = refs[0], refs[1], refs[2:2 + nw_]
        y_ref, u_ref = refs[2 + nw_:]
        y = _rmsnorm(h_ref[...], nw_ref[...]).astype(BF16)
        y_ref[...] = y
        for s, w_ref in enumerate(w_refs):
            acc = jnp.dot(y, w_ref[...], preferred_element_type=F32)
            u_ref[:, s * nb:(s + 1) * nb] = (jnp.maximum(acc, 0.0) if relu else acc).astype(BF16)

    row = pl.BlockSpec((tm, d), lambda i: (i, 0))
    return _pcall(
        body, name=name, grid=(n_tok // tm,),
        in_specs=[row, pl.BlockSpec((1, d), lambda i: (0, 0))] + w_specs,
        out_specs=[row, pl.BlockSpec((tm, n_out), lambda i: (i, 0))],
        out_shape=(jax.ShapeDtypeStruct((n_tok, d), BF16), jax.ShapeDtypeStruct((n_tok, n_out), BF16)),
        dims=("parallel",), args=[h, nw] + [w] * nw_, carries=carries)


def _mm_tiles(m, n, k, extra_bytes):
    tn = _tile(n)
    for tm in (1024, 512, 256, 128):
        if m % tm == 0 and 2 * (2 * tm * k + 2 * k * tn + (4 + extra_bytes) * tm * tn) <= MM_VMEM_BUDGET:
            return tm, tn
    return min(m, 128), tn


def matmul(a, b, mode, name, out_dtype=F32, extra=None, epilogue=None, shards=1, carries=(), square_a=False):
    slots = b.shape[0] if (mode == "nt" and b.ndim == 3) else 0
    if mode == "nn":
        (m, k), n = a.shape, b.shape[1]
    elif mode == "nt":
        (m, k), n = a.shape, (b.shape[1] if slots else b.shape[0])
    else:
        (k, m), n = a.shape, b.shape[1]
    tm, tn = _mm_tiles(m, n // shards, k, 0 if extra is None else extra.dtype.itemsize)
    a_spec = pl.BlockSpec((k, tm), lambda i, j: (0, i)) if mode == "tn" else pl.BlockSpec((tm, k), lambda i, j: (i, 0))
    if slots:
        kb = b.shape[2]
        b_specs = [pl.BlockSpec((None, tn, kb), lambda i, j, s=s: (s, j, 0)) for s in range(slots)]
    elif mode == "nt":
        b_specs = [pl.BlockSpec((tn, k), lambda i, j: (j, 0))]
    else:
        b_specs = [pl.BlockSpec((k, tn), lambda i, j: (0, j))]
    nb = len(b_specs)
    dims = _DIMS[mode]

    def body(*refs):
        a_ref, b_refs = refs[0], refs[1:1 + nb]
        e_ref = refs[1 + nb] if extra is not None else None
        o_ref = refs[-1]
        if slots:
            acc = None
            for s in range(slots):
                part = lax.dot_general(a_ref[:, s * kb:(s + 1) * kb].astype(BF16), b_refs[s][...].astype(BF16), dims,
                                       preferred_element_type=F32)
                acc = part if acc is None else acc + part
        else:
            av = a_ref[...].astype(BF16)
            if square_a:
                av = av * av
            acc = lax.dot_general(av, b_refs[0][...].astype(BF16), dims, preferred_element_type=F32)
        if epilogue == "add":
            acc = e_ref[...] + acc
        elif epilogue == "mul2":
            acc = acc * (2.0 * e_ref[...].astype(F32))
        o_ref[...] = acc.astype(o_ref.dtype)

    in_specs = [a_spec] + b_specs
    args = [a] + [b] * nb
    if extra is not None:
        in_specs.append(pl.BlockSpec((tm, tn), lambda i, j: (i, j)))
        args.append(extra)
    if shards > 1:
        per = n // shards // tn
        out_spec = pl.BlockSpec((None, tm, tn), lambda i, j: (j // per, i, j % per))
        out_shape = jax.ShapeDtypeStruct((shards, m, n // shards), out_dtype)
    else:
        out_spec = pl.BlockSpec((tm, tn), lambda i, j: (i, j))
        out_shape = jax.ShapeDtypeStruct((m, n), out_dtype)
    (res,), cres = _pcall(
        body, name=name, grid=(m // tm, n // tn), in_specs=in_specs, out_specs=[out_spec], out_shape=[out_shape],
        dims=("parallel", "arbitrary"), args=args, carries=carries)
    return res, cres


def matmul_norm_backward(a, b, h, nw, dres, name, carries=()):
    slots = b.shape[0] if b.ndim == 3 else 0
    n_tok, k = a.shape
    d = h.shape[1]
    tm = min(n_tok, ROW_TILE // 2)
    if slots:
        kb = b.shape[2]
        b_specs = [pl.BlockSpec((None, d, kb), lambda i, s=s: (s, 0, 0)) for s in range(slots)]
    else:
        b_specs = [pl.BlockSpec((d, k), lambda i: (0, 0))]
    nb = len(b_specs)

    def body(*refs):
        a_ref, b_refs = refs[0], refs[1:1 + nb]
        h_ref, nw_ref, dr_ref, dh_ref, dhb_ref, dnw_ref = refs[1 + nb:]

        @pl.when(pl.program_id(0) == 0)
        def _():
            dnw_ref[...] = jnp.zeros_like(dnw_ref)

        if slots:
            dy = None
            for s in range(slots):
                part = lax.dot_general(a_ref[:, s * kb:(s + 1) * kb], b_refs[s][...], _DIMS["nt"],
                                       preferred_element_type=F32)
                dy = part if dy is None else dy + part
        else:
            dy = lax.dot_general(a_ref[...], b_refs[0][...], _DIMS["nt"], preferred_element_type=F32)
        _, vjp = jax.vjp(_rmsnorm, h_ref[...], nw_ref[...])
        gh, gw = vjp(dy)
        dh = dr_ref[...] + gh
        dh_ref[...] = dh
        dhb_ref[...] = dh.astype(BF16)
        dnw_ref[...] += gw

    row = pl.BlockSpec((tm, d), lambda i: (i, 0))
    vec = pl.BlockSpec((1, d), lambda i: (0, 0))
    return _pcall(
        body, name=name, grid=(n_tok // tm,),
        in_specs=[pl.BlockSpec((tm, k), lambda i: (i, 0))] + b_specs + [row, vec, row], out_specs=[row, row, vec],
        out_shape=(jax.ShapeDtypeStruct((n_tok, d), F32), jax.ShapeDtypeStruct((n_tok, d), BF16),
                   jax.ShapeDtypeStruct((1, d), F32)),
        dims=("arbitrary",), args=[a] + [b] * nb + [h, nw, dres], carries=carries)


def _hgrn_post(o, gate, gw):
    return _rmsnorm(o, gw) * _silu(gate)


def hgrn_post_forward(o, proj, gw, name):
    n_tok, d = o.shape
    tm = min(n_tok, ROW_TILE)

    def body(o_ref, g_ref, w_ref, y_ref):
        y_ref[...] = _hgrn_post(o_ref[...], g_ref[...].astype(F32), w_ref[...]).astype(BF16)

    row = pl.BlockSpec((tm, d), lambda i: (i, 0))
    return pl.pallas_call(
        body, name=name, grid=(n_tok // tm,),
        in_specs=[row, pl.BlockSpec((tm, d), lambda i: (i, 3)), pl.BlockSpec((1, d), lambda i: (0, 0))],
        out_specs=row, out_shape=jax.ShapeDtypeStruct((n_tok, d), BF16),
        compiler_params=_cparams(("parallel",)),
    )(o, proj, gw)


def hgrn_post_backward(o, proj, gw, dy, name):
    n_tok, d = o.shape
    tm = min(n_tok, ROW_TILE)

    def body(o_ref, g_ref, w_ref, dy_ref, do_ref, dg_ref, dw_ref):
        @pl.when(pl.program_id(0) == 0)
        def _():
            dw_ref[...] = jnp.zeros_like(dw_ref)

        _, vjp = jax.vjp(_hgrn_post, o_ref[...], g_ref[...].astype(F32), w_ref[...])
        go, gg, gw_ = vjp(dy_ref[...])
        do_ref[...] = go
        dg_ref[...] = gg.astype(BF16)
        dw_ref[...] += gw_

    row = pl.BlockSpec((tm, d), lambda i: (i, 0))
    vec = pl.BlockSpec((1, d), lambda i: (0, 0))
    return pl.pallas_call(
        body, name=name, grid=(n_tok // tm,),
        in_specs=[row, pl.BlockSpec((tm, d), lambda i: (i, 3)), vec, row], out_specs=[row, row, vec],
        out_shape=(jax.ShapeDtypeStruct((n_tok, d), F32), jax.ShapeDtypeStruct((n_tok, d), BF16),
                   jax.ShapeDtypeStruct((1, d), F32)),
        compiler_params=_cparams(("arbitrary",)),
    )(o, proj, gw, dy)


def loss_head(h, nw, target):
    n_tok, d = h.shape
    tm = min(n_tok, ROW_TILE)

    def body(h_ref, nw_ref, t_ref, loss_ref, dh_ref, dhb_ref, dnw_ref):
        @pl.when(pl.program_id(0) == 0)
        def _():
            dnw_ref[...] = jnp.zeros_like(dnw_ref)
            loss_ref[...] = jnp.zeros_like(loss_ref)

        out, vjp = jax.vjp(_rmsnorm, h_ref[...], nw_ref[...])
        err = out - t_ref[...]
        part = 0.5 * jnp.sum(jnp.sum(err * err, axis=1, keepdims=True), axis=0, keepdims=True) / d
        loss_ref[...] += jnp.broadcast_to(part, loss_ref.shape)
        gh, gw = vjp(err / d)
        dh_ref[...] = gh
        dhb_ref[...] = gh.astype(BF16)
        dnw_ref[...] += gw

    row = pl.BlockSpec((tm, d), lambda i: (i, 0))
    vec = pl.BlockSpec((1, d), lambda i: (0, 0))
    return pl.pallas_call(
        body, name="loss_head", grid=(n_tok // tm,), in_specs=[row, vec, row],
        out_specs=[pl.BlockSpec((1, 128), lambda i: (0, 0)), row, row, vec],
        out_shape=(jax.ShapeDtypeStruct((1, 128), F32), jax.ShapeDtypeStruct((n_tok, d), F32),
                   jax.ShapeDtypeStruct((n_tok, d), BF16), jax.ShapeDtypeStruct((1, d), F32)),
        compiler_params=_cparams(("arbitrary",)),
    )(h, nw, target)


def _rows2d(shape):
    if len(shape) == 1:
        return (1, shape[0])
    return (math.prod(shape[:-1]), shape[-1])


def adamw(w, g, m, v, name):
    shape = w.shape
    r, c = _rows2d(shape)
    tr = r if r <= 256 else 256
    c1 = 1.0 / (1.0 - ADAM_B1 ** ADAM_STEP)
    c2 = 1.0 / (1.0 - ADAM_B2 ** ADAM_STEP)

    def body(w_ref, g_ref, m_ref, v_ref, d_ref, nm_ref, nv_ref):
        gg = g_ref[...]
        nm = ADAM_B1 * m_ref[...] + (1.0 - ADAM_B1) * gg
        nv = ADAM_B2 * v_ref[...] + (1.0 - ADAM_B2) * (gg * gg)
        d_ref[...] = -ADAM_LR * ((nm * c1) / (jnp.sqrt(nv * c2) + ADAM_EPS) + ADAM_WD * w_ref[...])
        nm_ref[...] = nm
        nv_ref[...] = nv

    spec = pl.BlockSpec((tr, c), lambda i: (i, 0))
    sds = jax.ShapeDtypeStruct((r, c), F32)
    outs = pl.pallas_call(
        body, name=name, grid=(r // tr,), in_specs=[spec] * 4, out_specs=[spec] * 3, out_shape=(sds,) * 3,
        compiler_params=_cparams(("parallel",)),
    )(w.reshape(r, c), g.reshape(r, c), m.reshape(r, c), v.reshape(r, c))
    return tuple(o.reshape(shape) for o in outs)


def add_slots(parts, name):
    s, r, c = parts.shape
    tr = r if r <= 256 else 256

    def body(p_ref, o_ref):
        acc = p_ref[0]
        for t in range(1, s):
            acc = acc + p_ref[t]
        o_ref[...] = acc

    return pl.pallas_call(
        body, name=name, grid=(r // tr,), in_specs=[pl.BlockSpec((s, tr, c), lambda i: (0, i, 0))],
        out_specs=pl.BlockSpec((tr, c), lambda i: (i, 0)), out_shape=jax.ShapeDtypeStruct((r, c), F32),
        compiler_params=_cparams(("parallel",)),
    )(parts)


BIG = ("gdn_w_in", "gdn_w_out", "hgrn_w_in", "hgrn_w_out", "mlp_w_up", "mlp_w_down")
WEIGHTS = ("gdn_w_in", "gdn_conv", "gdn_a_log", "gdn_dt_bias", "gdn_onorm", "gdn_w_out", "hgrn_w_in", "hgrn_lb_logits",
           "hgrn_gnorm", "hgrn_w_out", "norm_mix", "norm_mlp", "mlp_w_up", "mlp_w_down", "norm_final")


def _pad_lanes(v, n):
    return jnp.pad(v, [(0, 0)] * (v.ndim - 1) + [(0, n - v.shape[-1])])


_SMALL_LAYOUT = {
    "norm_mix": (0, 4, D_MODEL), "norm_mlp": (8, 4, D_MODEL), "norm_final": (16, 1, D_MODEL),
    "hgrn_lb_logits": (24, 4, D_MODEL), "gdn_onorm": (32, 2, 128), "gdn_a_log": (40, 2, HEADS),
    "gdn_dt_bias": (48, 2, HEADS), "loss": (56, 1, 128), "gdn_conv": (64, 24, D_MODEL), "hgrn_gnorm": (88, 2, D_MODEL),
}


def _pack_small(small, loss_row):
    rows = []
    for name, (first, nrow, lanes) in _SMALL_LAYOUT.items():
        v = loss_row if name == "loss" else small[name]
        v = _pad_lanes(v.reshape(nrow, -1), D_MODEL)
        rows.append(jnp.pad(v, ((0, -nrow % 8), (0, 0))))
    return jnp.concatenate(rows, axis=0)


def _unpack_small(packed, name, shape):
    first, nrow, lanes = _SMALL_LAYOUT[name]
    return packed[first:first + nrow, :lanes].reshape(shape)


def add_core_halves(g, theirs, core, name):
    s, r, c = g.shape
    r2 = r // 2
    tr = min(r2, 256)
    nb = r2 // tr

    def body(core_ref, g_ref, t_ref, o_ref):
        o_ref[...] = g_ref[...] + t_ref[...]

    grid_spec = pltpu.PrefetchScalarGridSpec(
        num_scalar_prefetch=1, grid=(s, nb),
        in_specs=[pl.BlockSpec((None, tr, c), lambda t, i, cr: (t, cr[0] * nb + i, 0)),
                  pl.BlockSpec((None, tr, c), lambda t, i, cr: (t, i, 0))],
        out_specs=pl.BlockSpec((None, tr, c), lambda t, i, cr: (t, i, 0)))
    return pl.pallas_call(body, name=name, grid_spec=grid_spec, out_shape=jax.ShapeDtypeStruct((s, r2, c), F32),
                          compiler_params=_cparams(("parallel", "parallel")))(core, g, theirs)


def add_chip_slots(slots, name):
    n_l = len(slots)
    s, r2, c = slots[0].shape
    tr = min(r2, 256)
    nb = r2 // tr

    def body(*refs):
        ins, o_ref = refs[:n_l], refs[n_l]
        for k in range(n_l):
            @pl.when(pl.program_id(0) == k)
            def _(k=k):
                acc = ins[k][0]
                for t in range(1, s):
                    acc = acc + ins[k][t]
                o_ref[...] = acc

    in_specs = [pl.BlockSpec((s, tr, c), lambda l, i, k=k: (0, jnp.where(l == k, i, 0), 0)) for k in range(n_l)]
    return pl.pallas_call(
        body, name=name, grid=(n_l, nb), in_specs=in_specs, out_specs=pl.BlockSpec((None, tr, c), lambda l, i: (l, i, 0)),
        out_shape=jax.ShapeDtypeStruct((n_l, r2, c), F32), compiler_params=_cparams(("arbitrary", "arbitrary")),
    )(*slots)


def adamw_halves(w, m, v, mine, theirs, name):
    n_l, r, c = w.shape
    r2 = r // 2
    tr = min(r2, 256)
    nb = r2 // tr
    c1 = 1.0 / (1.0 - ADAM_B1 ** ADAM_STEP)
    c2 = 1.0 / (1.0 - ADAM_B2 ** ADAM_STEP)

    def body(w_ref, m_ref, v_ref, mine_ref, theirs_ref, g_ref, d_ref, nm_ref, nv_ref):
        my_half = (pl.program_id(1) // nb) == lax.axis_index("c")
        gg = jnp.where(my_half, mine_ref[...], theirs_ref[...])
        nm = ADAM_B1 * m_ref[...] + (1.0 - ADAM_B1) * gg
        nv = ADAM_B2 * v_ref[...] + (1.0 - ADAM_B2) * (gg * gg)
        g_ref[...] = gg
        d_ref[...] = -ADAM_LR * ((nm * c1) / (jnp.sqrt(nv * c2) + ADAM_EPS) + ADAM_WD * w_ref[...])
        nm_ref[...] = nm
        nv_ref[...] = nv

    full = pl.BlockSpec((None, tr, c), lambda l, i: (l, i, 0))
    half = pl.BlockSpec((None, tr, c), lambda l, i: (l, i % nb, 0))
    sds = jax.ShapeDtypeStruct((n_l, r, c), F32)
    return pl.pallas_call(
        body, name=name, grid=(n_l, r // tr), in_specs=[full, full, full, half, half], out_specs=[full] * 4,
        out_shape=(sds,) * 4, compiler_params=_cparams(("parallel", "parallel")),
    )(w, m, v, mine, theirs)


def _layer_weight(kind, i):
    if kind == "up":
        return "mlp_w_up", i
    if kind == "down":
        return "mlp_w_down", i
    return ("gdn_w_" if i % 2 == 0 else "hgrn_w_") + kind, i // 2


def kernel(x, gdn_w_in, gdn_conv, gdn_a_log, gdn_dt_bias, gdn_onorm, gdn_w_out, hgrn_w_in, hgrn_lb_logits, hgrn_gnorm, hgrn_w_out, norm_mix, norm_mlp, mlp_w_up, mlp_w_down, norm_final, loss_target, m_gdn_w_in, m_gdn_conv, m_gdn_a_log, m_gdn_dt_bias, m_gdn_onorm, m_gdn_w_out, m_hgrn_w_in, m_hgrn_lb_logits, m_hgrn_gnorm, m_hgrn_w_out, m_norm_mix, m_norm_mlp, m_mlp_w_up, m_mlp_w_down, m_norm_final, v_gdn_w_in, v_gdn_conv, v_gdn_a_log, v_gdn_dt_bias, v_gdn_onorm, v_gdn_w_out, v_hgrn_w_in, v_hgrn_lb_logits, v_hgrn_gnorm, v_hgrn_w_out, v_norm_mix, v_norm_mlp, v_mlp_w_up, v_mlp_w_down, v_norm_final):
    p = dict(gdn_w_in=gdn_w_in, gdn_conv=gdn_conv, gdn_a_log=gdn_a_log, gdn_dt_bias=gdn_dt_bias, gdn_onorm=gdn_onorm,
             gdn_w_out=gdn_w_out, hgrn_w_in=hgrn_w_in, hgrn_lb_logits=hgrn_lb_logits, hgrn_gnorm=hgrn_gnorm,
             hgrn_w_out=hgrn_w_out, norm_mix=norm_mix, norm_mlp=norm_mlp, mlp_w_up=mlp_w_up, mlp_w_down=mlp_w_down,
             norm_final=norm_final)
    m = dict(gdn_w_in=m_gdn_w_in, gdn_conv=m_gdn_conv, gdn_a_log=m_gdn_a_log, gdn_dt_bias=m_gdn_dt_bias,
             gdn_onorm=m_gdn_onorm, gdn_w_out=m_gdn_w_out, hgrn_w_in=m_hgrn_w_in, hgrn_lb_logits=m_hgrn_lb_logits,
             hgrn_gnorm=m_hgrn_gnorm, hgrn_w_out=m_hgrn_w_out, norm_mix=m_norm_mix, norm_mlp=m_norm_mlp,
             mlp_w_up=m_mlp_w_up, mlp_w_down=m_mlp_w_down, norm_final=m_norm_final)
    v = dict(gdn_w_in=v_gdn_w_in, gdn_conv=v_gdn_conv, gdn_a_log=v_gdn_a_log, gdn_dt_bias=v_gdn_dt_bias,
             gdn_onorm=v_gdn_onorm, gdn_w_out=v_gdn_w_out, hgrn_w_in=v_hgrn_w_in, hgrn_lb_logits=v_hgrn_lb_logits,
             hgrn_gnorm=v_hgrn_gnorm, hgrn_w_out=v_hgrn_w_out, norm_mix=v_norm_mix, norm_mlp=v_norm_mlp,
             mlp_w_up=v_mlp_w_up, mlp_w_down=v_mlp_w_down, norm_final=v_norm_final)
    xi, yi, ci = _mesh_pos()
    chip = 2 * xi + yi
    core = jnp.reshape(ci, (1,)).astype(jnp.int32)
    d = D_MODEL
    bsz, t_len, _ = x.shape
    n_tok = bsz * t_len

    def shard(kind, i):
        name, idx = _layer_weight(kind, i)
        return p[name][idx].astype(BF16)

    def w_in_of(i, slots):
        if i % 2 == 0:
            return _pad_lanes(jnp.transpose(slots, (1, 0, 2)).reshape(d, GDN_IN), GDN_IN_PAD)
        return slots

    (first,) = run_carries([gather_carry([shard("in", 0), p["gdn_conv"], p["hgrn_gnorm"]])], "gather_first")
    conv = jnp.transpose(first[1], (1, 2, 0, 3)).reshape(DEPTH // 2, CONV_K, 3 * d)
    gnorm = jnp.transpose(first[2], (1, 0, 2)).reshape(DEPTH // 2, d)
    lbl = p["hgrn_lb_logits"]
    h = x.reshape(n_tok, d)
    next_in, next_out = first[0], None
    saved = []
    for i in range(DEPTH):
        j = i // 2
        w_in = w_in_of(i, next_in)
        nmix = p["norm_mix"][i][None, :]
        (y, proj), got = norm_matmul(h, nmix, w_in, False, f"in_proj{i}",
                                     [gather_carry([shard("out", 0)])] if i == 0 else [])
        if i == 0:
            next_out = got[0][0]
        ride = [gather_carry([shard("up", i), shard("down", i)])]
        if i % 2 == 0:
            al = _pad_lanes(p["gdn_a_log"][j][None, :], 128)
            dtb = _pad_lanes(p["gdn_dt_bias"][j][None, :], 128)
            on = p["gdn_onorm"][j][None, :]
            (og, ssave), got = gdn_forward(proj, conv[j], al, dtb, on, bsz, f"gdn_fwd{i}", ride)
            mix = (proj, ssave, al, dtb, on)
        else:
            (o, ssave), got = hgrn_forward(proj, lbl, i, bsz, ride)
            gn = gnorm[j][None, :]
            og = hgrn_post_forward(o, proj, gn, f"hgrn_post{i}")
            mix = (proj, ssave, o, gn)
        w_up, w_down = got[0][0], got[0][1].reshape(MLP_HIDDEN, d)
        w_out = next_out.reshape(d, d)
        h1, _ = matmul(og, w_out, "nn", f"out_proj{i}", extra=h, epilogue="add")
        nmlp = p["norm_mlp"][i][None, :]
        (z, r), got = norm_matmul(h1, nmlp, w_up, True, f"mlp_up{i}",
                                     [gather_carry([shard("in", i + 1)])] if i + 1 < DEPTH else [])
        if i + 1 < DEPTH:
            next_in = got[0][0]
        h2, got = matmul(r, w_down, "nn", f"mlp_down{i}", extra=h1, epilogue="add", square_a=True,
                         carries=[gather_carry([shard("out", i + 1)])] if i + 1 < DEPTH else [])
        if i + 1 < DEPTH:
            next_out = got[0][0]
        saved.append((h, nmix, y, mix, og, w_in, w_out, h1, nmlp, z, r, w_up, w_down))
        h = h2
    loss_row, dh, dhb, d_nf = loss_head(h, p["norm_final"][None, :], loss_target.reshape(n_tok, d))

    G = {k: [None] * DEPTH for k in ("in", "out", "up", "down")}
    P = {k: [None] * DEPTH for k in ("in", "out", "up", "down")}
    slots = {k: [None] * DEPTH for k in ("in", "out", "up", "down")}
    d_nmix, d_nmlp = [None] * DEPTH, [None] * DEPTH
    d_conv, d_alog, d_dtb, d_onorm, d_gnorm = [None] * 2, [None] * 2, [None] * 2, [None] * 2, [None] * 2
    d_lbl = jnp.zeros((DEPTH, d), F32)
    for i in reversed(range(DEPTH)):
        j = i // 2
        h_in, nmix, y, mix, og, w_in, w_out, h1, nmlp, z, r, w_up, w_down = saved[i]
        ride = [sibling_half_carry([G["in"][i + 1]])] if i + 1 < DEPTH else []
        du, got = matmul(dhb, w_down, "nt", f"d_mlp_act{i}", out_dtype=BF16, extra=r, epilogue="mul2", carries=ride)
        if i + 1 < DEPTH:
            P["in"][i + 1] = add_core_halves(G["in"][i + 1], got[0][0], core, f"add_cores_in{i + 1}")
        G["down"][i] = matmul(r, dhb, "tn", f"dw_down{i}", square_a=True)[0].reshape(N_CHIPS, -1, d)
        G["up"][i], _ = matmul(z, du, "tn", f"dw_up{i}", shards=N_CHIPS)
        (dh1, dh1b, d_nmlp[i]), got = matmul_norm_backward(du, w_up, h1, nmlp, dh, f"d_mlp_in{i}",
                                                           [sibling_half_carry([G["up"][i], G["down"][i]])])
        P["up"][i] = add_core_halves(G["up"][i], got[0][0], core, f"add_cores_up{i}")
        P["down"][i] = add_core_halves(G["down"][i], got[0][1], core, f"add_cores_down{i}")
        dog, _ = matmul(dh1b, w_out, "nt", f"d_mix_out{i}")
        G["out"][i] = matmul(og, dh1b, "tn", f"dw_out{i}")[0].reshape(N_CHIPS, -1, d)
        to_chips = [("up", i), ("down", i)] + ([("in", i + 1), ("out", i + 1)] if i + 1 < DEPTH else [])
        ride = [sibling_half_carry([G["out"][i]]), chips_carry([P[k][l] for k, l in to_chips])]
        if i % 2 == 0:
            proj, ssave, al, dtb, on = mix
            (dproj, dcw, dal, ddt, don), got = gdn_backward(proj, conv[j], al, dtb, on, ssave, dog, bsz, f"gdn_bwd{i}", ride)
            d_conv[j] = jnp.transpose(dcw, (2, 1, 0, 3)).reshape(CONV_K, 3 * d)
            d_alog[j], d_dtb[j], d_onorm[j] = dal[0, :HEADS], ddt[0, :HEADS], don[0]
        else:
            proj, ssave, o, gn = mix
            do_raw, dgate, dgn = hgrn_post_backward(o, proj, gn, dog, f"d_hgrn_post{i}")
            (dproj, dlb), got = hgrn_backward(proj, lbl, ssave, do_raw, dgate, i, bsz, ride)
            d_gnorm[j] = dgn[0]
            d_lbl = d_lbl + jnp.transpose(dlb, (1, 0, 2)).reshape(DEPTH, d)
        P["out"][i] = add_core_halves(G["out"][i], got[0][0], core, f"add_cores_out{i}")
        for (k, l), s in zip(to_chips, got[1]):
            slots[k][l] = s
        if i % 2 == 0:
            dw_in = matmul(y, dproj, "tn", f"dw_in{i}")[0][:, :GDN_IN]
            G["in"][i] = jnp.transpose(dw_in.reshape(d, N_CHIPS, GDN_IN // N_CHIPS), (1, 0, 2))
        else:
            G["in"][i], _ = matmul(y, dproj, "tn", f"dw_in{i}", shards=N_CHIPS)
        ride = [sibling_half_carry([G["in"][0]]), chips_carry([P["out"][0]])] if i == 0 else []
        (dh, dhb, d_nmix[i]), got = matmul_norm_backward(dproj, w_in, h_in, nmix, dh1, f"d_in_proj{i}", ride)
        if i == 0:
            P["in"][0] = add_core_halves(G["in"][0], got[0][0], core, "add_cores_in0")
            slots["out"][0] = got[1][0]
    grad_x = dh.reshape(x.shape)
    ((slots["in"][0],),) = run_carries([chips_carry([P["in"][0]])], "reduce_chips_last")

    by_weight = {}
    for kind in ("in", "out", "up", "down"):
        for i in range(DEPTH):
            by_weight.setdefault(_layer_weight(kind, i)[0], []).append(slots[kind][i])
    mine = {name: add_chip_slots(by_weight[name], f"add_chips_{name}") for name in BIG}
    small = {
        "gdn_conv": jnp.stack(d_conv), "gdn_a_log": jnp.stack(d_alog), "gdn_dt_bias": jnp.stack(d_dtb),
        "gdn_onorm": jnp.stack(d_onorm), "hgrn_lb_logits": d_lbl, "hgrn_gnorm": jnp.stack(d_gnorm),
        "norm_mix": jnp.concatenate(d_nmix, axis=0), "norm_mlp": jnp.concatenate(d_nmlp, axis=0), "norm_final": d_nf[0],
    }
    theirs, (blocks,) = run_carries([sibling_copy_carry([mine[name] for name in BIG]),
                                     gather_all_carry(_pack_small(small, loss_row))], "share_cores")
    theirs = dict(zip(BIG, theirs))

    total = add_slots(blocks, "add_small")
    loss = total[_SMALL_LAYOUT["loss"][0], 0]
    grads, delta, new_m, new_v = {}, {}, {}, {}
    for name in WEIGHTS:
        if name in BIG:
            grads[name], delta[name], new_m[name], new_v[name] = adamw_halves(
                p[name], m[name], v[name], mine[name], theirs[name], f"adamw_{name}")
            continue
        if name == "gdn_conv":
            full = _unpack_small(total, name, (2, CONV_K, 3 * d))
            grads[name] = lax.dynamic_slice_in_dim(full, chip * (3 * d // N_CHIPS), 3 * d // N_CHIPS, axis=2)
        elif name == "hgrn_gnorm":
            full = _unpack_small(total, name, (2, d))
            grads[name] = lax.dynamic_slice_in_dim(full, chip * (d // N_CHIPS), d // N_CHIPS, axis=1)
        else:
            grads[name] = _unpack_small(total, name, p[name].shape)
        delta[name], new_m[name], new_v[name] = adamw(p[name], grads[name], m[name], v[name], f"adamw_{name}")
    return (loss, grad_x, *[grads[n] for n in WEIGHTS], *[delta[n] for n in WEIGHTS],
            *[new_m[n] for n in WEIGHTS], *[new_v[n] for n in WEIGHTS])
```

```python
import functools
import math

import jax
import jax.numpy as jnp
from jax import lax
from jax.experimental import pallas as pl
from jax.experimental.pallas import tpu as pltpu

F32 = jnp.float32
BF16 = jnp.bfloat16
HI = lax.Precision.HIGHEST

D_MODEL = 1024
HEADS = 8
HEAD_DIM = 128
CHUNK = 64
SUB = 16
CONV_K = 4
HALO = 16
DEPTH = 4
EPS = 1e-6
MLP_HIDDEN = 4 * D_MODEL
GDN_MAIN = 4 * D_MODEL
GDN_IN = GDN_MAIN + 2 * HEADS
GDN_IN_PAD = GDN_MAIN + 128
NEG = -1e30

ADAM_LR = 0.001
ADAM_B1 = 0.9
ADAM_B2 = 0.999
ADAM_EPS = 1e-08
ADAM_WD = 0.01
ADAM_STEP = 10

VMEM_LIMIT = 48 * 1024 * 1024

MESH = pl.DeviceIdType.MESH


def _cparams(sem=None, **kw):
    if sem is not None:
        kw["dimension_semantics"] = sem
    return pltpu.CompilerParams(vmem_limit_bytes=VMEM_LIMIT, **kw)


def _iota(shape, dim):
    return lax.broadcasted_iota(jnp.int32, shape, dim)


_DIMS = {"nn": (((1,), (0,)), ((), ())), "nt": (((1,), (1,)), ((), ())), "tn": (((0,), (0,)), ((), ()))}


def _dot(a, b, mode):
    return lax.dot_general(a.astype(BF16), b.astype(BF16), _DIMS[mode], preferred_element_type=F32)


@functools.partial(jax.custom_vjp, nondiff_argnums=(2,))
def _mmx(a, b, mode):
    return _dot(a, b, mode)


def _mmx_fwd(a, b, mode):
    return _dot(a, b, mode), (a, b)


def _mmx_bwd(mode, res, g):
    a, b = res
    if mode == "nn":
        return _dot(g, b, "nt"), _dot(a, g, "tn")
    if mode == "nt":
        return _dot(g, b, "nn"), _dot(g, a, "tn")
    return _dot(b, g, "nt"), _dot(a, g, "nn")


_mmx.defvjp(_mmx_fwd, _mmx_bwd)


def _mm(a, b):
    return _mmx(a, b, "nn")


def _mm_nt(a, b):
    return _mmx(a, b, "nt")


def _mm_tn(a, b):
    return _mmx(a, b, "tn")


@functools.partial(jax.custom_vjp, nondiff_argnums=(1,))
def _roll_rows(x, d):
    return pltpu.roll(x, d, 0)


def _roll_rows_fwd(x, d):
    return pltpu.roll(x, d, 0), None


def _roll_rows_bwd(d, _, g):
    return (pltpu.roll(g, g.shape[0] - d, 0),)


_roll_rows.defvjp(_roll_rows_fwd, _roll_rows_bwd)


def _sig(x):
    return 1.0 / (1.0 + jnp.exp(-x))


@jax.custom_vjp
def _sigmoid(x):
    return _sig(x)


def _sigmoid_fwd(x):
    s = _sig(x)
    return s, s


def _sigmoid_bwd(s, g):
    return (g * s * (1.0 - s),)


_sigmoid.defvjp(_sigmoid_fwd, _sigmoid_bwd)


@jax.custom_vjp
def _silu(x):
    return x * _sig(x)


def _silu_fwd(x):
    s = _sig(x)
    return x * s, (x, s)


def _silu_bwd(res, g):
    x, s = res
    return (g * s * (1.0 + x * (1.0 - s)),)


_silu.defvjp(_silu_fwd, _silu_bwd)


@jax.custom_vjp
def _softplus(x):
    return jnp.maximum(x, 0.0) + jnp.log(1.0 + jnp.exp(-jnp.abs(x)))


def _softplus_fwd(x):
    return _softplus(x), x


def _softplus_bwd(x, g):
    return (g * _sig(x),)


_softplus.defvjp(_softplus_fwd, _softplus_bwd)


@jax.custom_vjp
def _log_sigmoid(x):
    return jnp.minimum(x, 0.0) - jnp.log(1.0 + jnp.exp(-jnp.abs(x)))


def _log_sigmoid_fwd(x):
    return _log_sigmoid(x), x


def _log_sigmoid_bwd(x, g):
    return (g * _sig(-x),)


_log_sigmoid.defvjp(_log_sigmoid_fwd, _log_sigmoid_bwd)


@jax.custom_vjp
def _logaddexp(a, b):
    return jnp.maximum(a, b) + jnp.log(1.0 + jnp.exp(-jnp.abs(a - b)))


def _logaddexp_fwd(a, b):
    return _logaddexp(a, b), (a, b)


def _unbroadcast(g, like):
    for ax in range(g.ndim):
        if like.shape[ax] == 1 and g.shape[ax] != 1:
            g = jnp.sum(g, axis=ax, keepdims=True)
    return g


def _logaddexp_bwd(res, g):
    a, b = res
    s = _sig(a - b)
    return _unbroadcast(g * s, a), _unbroadcast(g * (1.0 - s), b)


_logaddexp.defvjp(_logaddexp_fwd, _logaddexp_bwd)


def _row_to_col(row):
    n = row.shape[1]
    eye = _iota((n, n), 0) == _iota((n, n), 1)
    return jnp.sum(jnp.where(eye, jnp.broadcast_to(row, (n, n)), 0.0), axis=1, keepdims=True)


def _col_to_row(col):
    n = col.shape[0]
    eye = _iota((n, n), 0) == _iota((n, n), 1)
    return jnp.sum(jnp.where(eye, jnp.broadcast_to(col, (n, n)), 0.0), axis=0, keepdims=True)


def _pick_row(x, r):
    return jnp.sum(jnp.where(_iota(x.shape, 0) == r, x, 0.0), axis=0, keepdims=True)


def _pick_lane(x, l):
    return jnp.sum(jnp.where(_iota(x.shape, 1) == l, x, 0.0), axis=1, keepdims=True)


def _each(f, *lists):
    return [f(*t) for t in zip(*lists)]


def _unit_lower_inverse(Ls):
    n = Ls[0].shape[0]
    r, c = _iota((n, n), 0), _iota((n, n), 1)
    eye = jnp.where(r == c, 1.0, 0.0).astype(F32)
    Ld = _each(lambda L: jnp.where((r // SUB) == (c // SUB), L, 0.0), Ls)
    Lo = _each(lambda L, d: L - d, Ls, Ld)
    P = _each(lambda d: eye - d, Ld)
    Lp = Ld
    for _ in range(int(math.log2(SUB)) - 1):
        Lp = _each(lambda x: _mm(x, x), Lp)
        P = _each(lambda p, x: p + _mm(p, x), P, Lp)
    N = _each(_mm, P, Lo)
    N2 = _each(lambda x: _mm(x, x), N)
    X = _each(lambda x, x2: (eye - x) + _mm(eye - x, x2), N, N2)
    return _each(_mm, X, P)


def _shift_rows(x, halo, d):
    if d == 0:
        return x
    xr = _roll_rows(x, d)
    hr = _roll_rows(halo, d)
    hfull = jnp.concatenate([hr, jnp.zeros((x.shape[0] - HALO, x.shape[1]), F32)], axis=0)
    return jnp.where(_iota(x.shape, 0) >= d, xr, hfull)


def _causal_conv_chunk(x, halo, w):
    y = None
    for kk in range(CONV_K):
        t = _shift_rows(x, halo, CONV_K - 1 - kk) * _pick_row(w, kk)
        y = t if y is None else y + t
    return y


def _gdn_chunk(xq, xk, xv, hq, hk, hv, gate, ab, S, cwq, cwk, cwv, alog, dtb, onorm, *, heads, seqs):
    C = xq[0].shape[0]
    q = _each(lambda x, h, w: _silu(_causal_conv_chunk(x, h, w)), xq, hq, cwq)
    k = _each(lambda x, h, w: _silu(_causal_conv_chunk(x, h, w)), xk, hk, cwk)
    v = _each(lambda x, h, w: _silu(_causal_conv_chunk(x, h, w)), xv, hv, cwv)
    q = _each(lambda t: t * lax.rsqrt(jnp.sum(t * t, axis=1, keepdims=True) + EPS) * (HEAD_DIM ** -0.5), q)
    k = _each(lambda t: t * lax.rsqrt(jnp.sum(t * t, axis=1, keepdims=True) + EPS), k)
    beta_all = _each(_sigmoid, ab)
    g_all = _each(lambda t: -jnp.exp(alog) * _softplus(t + dtb), ab)
    beta = [_pick_lane(beta_all[b], HEADS + h) for b, h in zip(seqs, heads)]
    g = [_pick_lane(g_all[b], h) for b, h in zip(seqs, heads)]
    r, c = _iota((C, C), 0), _iota((C, C), 1)
    gc = _each(lambda t: jnp.sum(jnp.where(c <= r, jnp.broadcast_to(_col_to_row(t), (C, C)), 0.0), axis=1,
                                 keepdims=True), g)
    gc_row = _each(lambda t: jnp.sum(jnp.where(r <= c, jnp.broadcast_to(t, (C, C)), 0.0), axis=0, keepdims=True), g)
    decay = _each(lambda a, b: jnp.exp(jnp.where(r >= c, a - b, NEG)), gc, gc_row)
    kb = _each(lambda a, b: a * b, k, beta)
    L = _each(lambda a, b, d: jnp.where(r > c, _mm_nt(a, b) * d, 0.0), kb, k, decay)
    A = _each(lambda a, b, d: jnp.where(r >= c, _mm_nt(a, b) * d, 0.0), q, k, decay)
    T = _unit_lower_inverse(L)
    egc = _each(jnp.exp, gc)
    u = _each(lambda t, a, b: _mm(t, a * b), T, v, beta)
    w = _each(lambda t, a, e: _mm(t, a * e), T, kb, egc)
    gl = _each(lambda t: _pick_row(t, C - 1), gc)
    v_new = _each(lambda a, b, s: a - _mm(b, s), u, w, S)
    o = _each(lambda a, e, s, m, vn: _mm(a * e, s) + _mm(m, vn), q, egc, S, A, v_new)
    S_next = _each(lambda s, l, a, t, vn: s * jnp.exp(l) + _mm_tn(a * jnp.exp(l - t), vn), S, gl, k, gc, v_new)
    o = _each(lambda t, gt: t * lax.rsqrt(jnp.mean(t * t, axis=1, keepdims=True) + EPS) * onorm * _silu(gt), o, gate)
    return o, S_next


def _hgrn_lower_bound(lbl, layer):
    e = jnp.exp(lbl - jnp.max(lbl, axis=0, keepdims=True))
    sm = e / jnp.sum(e, axis=0, keepdims=True)
    r = _iota(lbl.shape, 0)
    return jnp.sum(jnp.where((r >= 1) & (r <= layer), sm, 0.0), axis=0, keepdims=True)


_LEVELS = (1, 2, 4, 8, 16, 32)


def _prefix_matrix(n):
    r, c = _iota((n, n), 0), _iota((n, n), 1)
    parts = [jnp.where(c <= r, 1.0, 0.0)]
    for s in _LEVELS:
        parts.append(jnp.where(c < (r // (2 * s)) * (2 * s) + s, 1.0, 0.0))
    return jnp.concatenate(parts, axis=0).astype(F32)


def _prefix_sums_of(x):
    n = x.shape[0]
    y = lax.dot_general(_prefix_matrix(n), x, _DIMS["nn"], precision=HI, preferred_element_type=F32)
    return tuple(y[t * n:(t + 1) * n] for t in range(len(_LEVELS) + 1))


@jax.custom_vjp
def _prefix_sums(x):
    return _prefix_sums_of(x)


def _prefix_sums_fwd(x):
    return _prefix_sums_of(x), None


def _prefix_sums_bwd(_, gs):
    g = jnp.concatenate(gs, axis=0)
    return (lax.dot_general(_prefix_matrix(gs[0].shape[0]), g, _DIMS["tn"], precision=HI, preferred_element_type=F32),)


_prefix_sums.defvjp(_prefix_sums_fwd, _prefix_sums_bwd)


def _hgrn_chunk(qp, fp, v, S, lbl, *, layer):
    C = qp[0].shape[0]
    lb = _each(lambda l: _hgrn_lower_bound(l, layer), lbl)
    lf = _each(lambda l, f: _logaddexp(jnp.log(l), jnp.log(1.0 - l) + _log_sigmoid(f)), lb, fp)
    k = _each(lambda l, f: (1.0 - l) * _sigmoid(-f), lb, fp)
    q = _each(lambda x: _silu(x) * (HEAD_DIM ** -0.5), qp)
    r, c = _iota((C, C), 0), _iota((C, C), 1)
    row = _iota(qp[0].shape, 0)
    sums = _each(_prefix_sums, lf)
    gc = [t[0] for t in sums]
    a = _each(lambda x, y: jnp.where(r == c, _mm_nt(x, y), 0.0), q, k)
    for n, s in enumerate(_LEVELS):
        ref = [t[n + 1] for t in sums]
        upper = (row % (2 * s)) >= s
        same = (r // (2 * s)) == (c // (2 * s))
        q_s = _each(lambda x, g, m: x * jnp.exp(jnp.where(upper, g - m, NEG)), q, gc, ref)
        k_s = _each(lambda x, g, m: x * jnp.exp(jnp.where(upper, NEG, m - g)), k, gc, ref)
        a = _each(lambda t, x, y: t + jnp.where(same, _mm_nt(x, y), 0.0), a, q_s, k_s)
    o = _each(lambda t, x, g, vv, st: _mm(t, vv) + _mm(x * jnp.exp(g), st), a, q, gc, v, S)
    gl = _each(lambda g: _pick_row(g, C - 1), gc)
    S_next = _each(lambda st, l, x, g, vv: st * _row_to_col(jnp.exp(l)) + _mm_tn(x * jnp.exp(l - g), vv),
                   S, gl, k, gc, v)
    return o, S_next


N_CHIPS = 4
N_DEV = 8
_ANY = pl.BlockSpec(memory_space=pl.ANY)


def _mesh_pos():
    return lax.axis_index("x"), lax.axis_index("y"), lax.axis_index("c")


def _other_chips(x, y):
    ps = [(1 - x, y), (x, 1 - y), (1 - x, 1 - y)]
    return [(p, 2 * p[0] + p[1]) for p in ps]


def _remote(src, dst, send_sem, recv_sem, dev):
    return pltpu.make_async_remote_copy(src_ref=src, dst_ref=dst, send_sem=send_sem, recv_sem=recv_sem,
                                        device_id=dev, device_id_type=MESH)


class Carry:
    def __init__(self, ins, out_shapes, sems, start, finish):
        self.ins, self.out_shapes, self.sems, self.start, self.finish = list(ins), list(out_shapes), list(sems), start, finish


def _pcall(body, *, name, grid, in_specs, out_specs, out_shape, scratch_shapes=(), dims, args, carries=()):
    in_specs, out_specs, out_shape = list(in_specs), list(out_specs), list(out_shape)
    scratch_shapes, args = list(scratch_shapes), list(args)
    n_in, n_out, n_scr = len(in_specs), len(out_shape), len(scratch_shapes)
    carries = [c for c in carries if c is not None]
    if not carries:
        res = pl.pallas_call(body, name=name, grid=grid, in_specs=in_specs, out_specs=out_specs, out_shape=out_shape,
                             scratch_shapes=scratch_shapes, compiler_params=_cparams(dims))(*args)
        return list(res), []
    ci = [len(c.ins) for c in carries]
    co = [len(c.out_shapes) for c in carries]
    cs = [len(c.sems) for c in carries]

    def split(seq, sizes):
        out, k = [], 0
        for s in sizes:
            out.append(seq[k:k + s])
            k += s
        return out

    def carried(*refs):
        ins, cins, outs, couts, scr, sems = split(refs, [n_in, sum(ci), n_out, sum(co), n_scr, sum(cs)])
        cins, couts, sems = split(cins, ci), split(couts, co), split(sems, cs)
        ids = [pl.program_id(a) for a in range(len(grid))]
        first, last = ids[0] == 0, ids[0] == grid[0] - 1
        for a in range(1, len(grid)):
            first, last = first & (ids[a] == 0), last & (ids[a] == grid[a] - 1)

        @pl.when(first)
        def _():
            for c, i, o, s in zip(carries, cins, couts, sems):
                c.start(i, o, s)

        body(*ins, *outs, *scr)

        @pl.when(last)
        def _():
            for c, i, o, s in zip(carries, cins, couts, sems):
                c.finish(i, o, s)

    res = pl.pallas_call(
        carried, name=name, grid=grid,
        in_specs=in_specs + [_ANY] * sum(ci), out_specs=out_specs + [_ANY] * sum(co),
        out_shape=out_shape + [s for c in carries for s in c.out_shapes],
        scratch_shapes=scratch_shapes + [s for c in carries for s in c.sems],
        compiler_params=_cparams(("arbitrary",) * len(grid)),
    )(*args, *[a for c in carries for a in c.ins])
    return list(res[:n_out]), split(list(res[n_out:]), co)


def run_carries(carries, name):
    ci = [len(c.ins) for c in carries]
    co = [len(c.out_shapes) for c in carries]
    cs = [len(c.sems) for c in carries]

    def split(seq, sizes):
        out, k = [], 0
        for s in sizes:
            out.append(seq[k:k + s])
            k += s
        return out

    def body(*refs):
        cins, couts, sems = split(refs, [sum(ci), sum(co), sum(cs)])
        cins, couts, sems = split(cins, ci), split(couts, co), split(sems, cs)
        for c, i, o, s in zip(carries, cins, couts, sems):
            c.start(i, o, s)
        for c, i, o, s in zip(carries, cins, couts, sems):
            c.finish(i, o, s)

    res = pl.pallas_call(
        body, name=name, in_specs=[_ANY] * sum(ci), out_specs=[_ANY] * sum(co),
        out_shape=[s for c in carries for s in c.out_shapes], scratch_shapes=[s for c in carries for s in c.sems],
    )(*[a for c in carries for a in c.ins])
    return split(list(res), co)


def gather_carry(arrs):
    n = len(arrs)
    split = [a.ndim == 2 and a.shape[0] % 32 == 0 for a in arrs]

    def plan(ins, outs, sems):
        send_sems, recv_sems, pass_send, pass_recv, local_sems = sems
        x, y, c = _mesh_pos()
        me = 2 * x + y
        peers = _other_chips(x, y)
        locs = [pltpu.make_async_copy(ins[a], outs[a].at[me], local_sems.at[a]) for a in range(n)]
        sends, recvs, passes, pass_recvs = [], [], [], []
        for a in range(n):
            half = arrs[a].shape[0] // 2
            mine, other = pl.ds(c * half, half), pl.ds((1 - c) * half, half)
            for j, ((px, py), t) in enumerate(peers):
                sem = (send_sems.at[a, j], recv_sems.at[a, j])
                if split[a]:
                    sends.append(_remote(ins[a].at[mine], outs[a].at[me, mine], *sem, (px, py, c)))
                    recvs.append(_remote(ins[a].at[mine], outs[a].at[t, mine], *sem, (px, py, c)))
                    psem = (pass_send.at[a, j], pass_recv.at[a, j])
                    passes.append(_remote(outs[a].at[t, mine], outs[a].at[t, mine], *psem, (x, y, 1 - c)))
                    pass_recvs.append(_remote(outs[a].at[t, other], outs[a].at[t, other], *psem, (x, y, 1 - c)))
                else:
                    sends.append(_remote(ins[a], outs[a].at[me], *sem, (px, py, c)))
                    recvs.append(_remote(ins[a], outs[a].at[t], *sem, (px, py, c)))
                    passes.append(None)
                    pass_recvs.append(None)
        return locs, sends, recvs, passes, pass_recvs

    def start(ins, outs, sems):
        locs, sends, _, _, _ = plan(ins, outs, sems)
        for cp in locs + sends:
            cp.start()

    def finish(ins, outs, sems):
        locs, sends, recvs, passes, pass_recvs = plan(ins, outs, sems)
        for r, p in zip(recvs, passes):
            r.wait_recv()
            if p is not None:
                p.start()
        for p in pass_recvs:
            if p is not None:
                p.wait_recv()
        for r in sends + [p for p in passes if p is not None]:
            r.wait_send()
        for cp in locs:
            cp.wait()

    return Carry(arrs, [jax.ShapeDtypeStruct((N_CHIPS,) + a.shape, a.dtype) for a in arrs],
                 [pltpu.SemaphoreType.DMA((n, 3)), pltpu.SemaphoreType.DMA((n, 3)), pltpu.SemaphoreType.DMA((n, 3)),
                  pltpu.SemaphoreType.DMA((n, 3)), pltpu.SemaphoreType.DMA((n,))], start, finish)


def sibling_half_carry(gs):
    n = len(gs)

    def copies(ins, outs, sems):
        send_sems, recv_sems = sems
        x, y, c = _mesh_pos()
        out = []
        for a in range(n):
            half = gs[a].shape[1] // 2
            out.append(_remote(ins[a].at[:, pl.ds((1 - c) * half, half), :], outs[a], send_sems.at[a], recv_sems.at[a],
                               (x, y, 1 - c)))
        return out

    def start(ins, outs, sems):
        for r in copies(ins, outs, sems):
            r.start()

    def finish(ins, outs, sems):
        cps = copies(ins, outs, sems)
        for r in cps:
            r.wait_recv()
        for r in cps:
            r.wait_send()

    return Carry(gs, [jax.ShapeDtypeStruct((g.shape[0], g.shape[1] // 2, g.shape[2]), g.dtype) for g in gs],
                 [pltpu.SemaphoreType.DMA((n,)), pltpu.SemaphoreType.DMA((n,))], start, finish)


def chips_carry(ps):
    n = len(ps)

    def copies(ins, outs, sems):
        send_sems, recv_sems, local_sems = sems
        x, y, c = _mesh_pos()
        me = 2 * x + y
        peers = _other_chips(x, y)
        locs = [pltpu.make_async_copy(ins[a].at[me], outs[a].at[me], local_sems.at[a]) for a in range(n)]
        sends = [_remote(ins[a].at[t], outs[a].at[me], send_sems.at[a, j], recv_sems.at[a, j], (px, py, c))
                 for a in range(n) for j, ((px, py), t) in enumerate(peers)]
        recvs = [_remote(ins[a].at[t], outs[a].at[t], send_sems.at[a, j], recv_sems.at[a, j], (px, py, c))
                 for a in range(n) for j, ((px, py), t) in enumerate(peers)]
        return locs, sends, recvs

    def start(ins, outs, sems):
        locs, sends, _ = copies(ins, outs, sems)
        for cp in locs + sends:
            cp.start()

    def finish(ins, outs, sems):
        locs, sends, recvs = copies(ins, outs, sems)
        for r in recvs:
            r.wait_recv()
        for r in sends:
            r.wait_send()
        for cp in locs:
            cp.wait()

    return Carry(ps, [jax.ShapeDtypeStruct(p.shape, p.dtype) for p in ps],
                 [pltpu.SemaphoreType.DMA((n, 3)), pltpu.SemaphoreType.DMA((n, 3)), pltpu.SemaphoreType.DMA((n,))],
                 start, finish)


def sibling_copy_carry(arrs):
    n = len(arrs)

    def copies(ins, outs, sems):
        send_sems, recv_sems = sems
        x, y, c = _mesh_pos()
        return [_remote(ins[a], outs[a], send_sems.at[a], recv_sems.at[a], (x, y, 1 - c)) for a in range(n)]

    def start(ins, outs, sems):
        for r in copies(ins, outs, sems):
            r.start()

    def finish(ins, outs, sems):
        cps = copies(ins, outs, sems)
        for r in cps:
            r.wait_recv()
        for r in cps:
            r.wait_send()

    return Carry(arrs, [jax.ShapeDtypeStruct(a.shape, a.dtype) for a in arrs],
                 [pltpu.SemaphoreType.DMA((n,)), pltpu.SemaphoreType.DMA((n,))], start, finish)


def gather_all_carry(small):
    flips = [(fx, fy, fc) for fx in (0, 1) for fy in (0, 1) for fc in (0, 1)][1:]

    def copies(ins, outs, sems):
        send_sems, recv_sems, local_sem = sems
        in_ref, out_ref = ins[0], outs[0]
        x, y, c = _mesh_pos()
        me = 4 * x + 2 * y + c
        peers = [((1 - x) if fx else x, (1 - y) if fy else y, (1 - c) if fc else c) for fx, fy, fc in flips]
        loc = pltpu.make_async_copy(in_ref, out_ref.at[me], local_sem.at[0])
        sends = [_remote(in_ref, out_ref.at[me], send_sems.at[k], recv_sems.at[k], p) for k, p in enumerate(peers)]
        recvs = [_remote(in_ref, out_ref.at[4 * p[0] + 2 * p[1] + p[2]], send_sems.at[k], recv_sems.at[k], p)
                 for k, p in enumerate(peers)]
        return loc, sends, recvs

    def start(ins, outs, sems):
        loc, sends, _ = copies(ins, outs, sems)
        loc.start()
        for r in sends:
            r.start()

    def finish(ins, outs, sems):
        loc, sends, recvs = copies(ins, outs, sems)
        for r in recvs:
            r.wait_recv()
        for r in sends:
            r.wait_send()
        loc.wait()

    return Carry([small], [jax.ShapeDtypeStruct((N_DEV,) + small.shape, small.dtype)],
                 [pltpu.SemaphoreType.DMA((N_DEV - 1,)), pltpu.SemaphoreType.DMA((N_DEV - 1,)),
                  pltpu.SemaphoreType.DMA((1,))], start, finish)


def _chunk_spec(bsz, nc, col, rev=False, rows=CHUNK, width=D_MODEL):
    return pl.BlockSpec((bsz, rows, width), lambda n: (0, (nc - 1 - n) if rev else n, col))


def _halo_spec(bsz, nc, col, rev):
    per = CHUNK // HALO
    return pl.BlockSpec((bsz, HALO, D_MODEL),
                        lambda n: (0, jnp.maximum(((nc - 1 - n) if rev else n) * per - 1, 0), col))


def _const_spec(shape):
    nd = len(shape)
    return pl.BlockSpec(shape, lambda n: (0,) * nd)


def _state_spec(bsz, nc, rev=False):
    return pl.BlockSpec((bsz, None, HEADS, HEAD_DIM, HEAD_DIM), lambda n: (0, (nc - 1 - n) if rev else n, 0, 0, 0))


def _lanes(hh):
    return slice(hh * HEAD_DIM, (hh + 1) * HEAD_DIM)


def _chains(bsz):
    return [(b, hh) for b in range(bsz) for hh in range(HEADS)]


def _per_chain(ref, bsz, scale=None):
    vals = [ref[b, :, _lanes(hh)].astype(F32) for b, hh in _chains(bsz)]
    return vals if scale is None else [v * scale for v in vals]


def _per_head(ref, bsz):
    return [ref[:, _lanes(hh)].astype(F32) for _, hh in _chains(bsz)]


def gdn_forward(proj, conv_w, alog, dtb, onorm, bsz, name, carries=()):
    n_tok = proj.shape[0]
    t_len = n_tok // bsz
    nc = t_len // CHUNK
    chains = _chains(bsz)

    def body(xq, xk, xv, hq, hk, hv, gate, ab, cwq, cwk, cwv, al, dt, on, o_ref, ssave_ref, s_ref):
        n = pl.program_id(0)
        keep = jnp.where(n > 0, 1.0, 0.0).astype(F32)

        @pl.when(n == 0)
        def _():
            s_ref[...] = jnp.zeros_like(s_ref)

        S = [s_ref[b, hh] for b, hh in chains]
        for (b, hh), s in zip(chains, S):
            ssave_ref[b, hh] = s
        outs, s_next = _gdn_chunk(
            _per_chain(xq, bsz), _per_chain(xk, bsz), _per_chain(xv, bsz), _per_chain(hq, bsz, keep),
            _per_chain(hk, bsz, keep), _per_chain(hv, bsz, keep), _per_chain(gate, bsz),
            [ab[b].astype(F32) for b in range(bsz)], S, _per_head(cwq, bsz), _per_head(cwk, bsz), _per_head(cwv, bsz),
            al[...], dt[...], on[...], heads=[hh for _, hh in chains], seqs=[b for b, _ in chains])
        for (b, hh), o, s in zip(chains, outs, s_next):
            o_ref[b, :, _lanes(hh)] = o.astype(o_ref.dtype)
            s_ref[b, hh] = s

    cw_spec = lambda col: pl.BlockSpec((CONV_K, D_MODEL), lambda n: (0, col))
    in_specs = [_chunk_spec(bsz, nc, 0), _chunk_spec(bsz, nc, 1), _chunk_spec(bsz, nc, 2),
                _halo_spec(bsz, nc, 0, False), _halo_spec(bsz, nc, 1, False), _halo_spec(bsz, nc, 2, False),
                _chunk_spec(bsz, nc, 3), _chunk_spec(bsz, nc, GDN_MAIN // 128, width=128),
                cw_spec(0), cw_spec(1), cw_spec(2),
                _const_spec((1, 128)), _const_spec((1, 128)), _const_spec((1, 128))]
    out_specs = [_chunk_spec(bsz, nc, 0), _state_spec(bsz, nc)]
    p3 = proj.reshape(bsz, t_len, -1)
    (o, ssave), got = _pcall(
        body, name=name, grid=(nc,), in_specs=in_specs, out_specs=out_specs,
        out_shape=(jax.ShapeDtypeStruct((bsz, t_len, D_MODEL), BF16),
                   jax.ShapeDtypeStruct((bsz, nc, HEADS, HEAD_DIM, HEAD_DIM), F32)),
        scratch_shapes=[pltpu.VMEM((bsz, HEADS, HEAD_DIM, HEAD_DIM), F32)], dims=("arbitrary",),
        args=(p3, p3, p3, p3, p3, p3, p3, p3, conv_w, conv_w, conv_w, alog, dtb, onorm), carries=carries)
    return (o.reshape(n_tok, D_MODEL), ssave), got


def gdn_backward(proj, conv_w, alog, dtb, onorm, s_saved, d_out, bsz, name, carries=()):
    n_tok = proj.shape[0]
    t_len = n_tok // bsz
    nc = t_len // CHUNK
    chains = _chains(bsz)

    def body(xq, xk, xv, hq, hk, hv, gate, ab, cwq, cwk, cwv, al, dt, on, ssave, do,
             dp_ref, dcw_ref, dal_ref, ddt_ref, don_ref, ds_ref, dhalo_ref):
        n = pl.program_id(0)
        nr = nc - 1 - n

        @pl.when(n == 0)
        def _():
            dcw_ref[...] = jnp.zeros_like(dcw_ref)
            dal_ref[...] = jnp.zeros_like(dal_ref)
            ddt_ref[...] = jnp.zeros_like(ddt_ref)
            don_ref[...] = jnp.zeros_like(don_ref)
            ds_ref[...] = jnp.zeros_like(ds_ref)
            dhalo_ref[...] = jnp.zeros_like(dhalo_ref)

        keep = jnp.where(nr > 0, 1.0, 0.0).astype(F32)
        pad = jnp.zeros((CHUNK - HALO, HEAD_DIM), F32)
        args = (_per_chain(xq, bsz), _per_chain(xk, bsz), _per_chain(xv, bsz), _per_chain(hq, bsz, keep),
                _per_chain(hk, bsz, keep), _per_chain(hv, bsz, keep), _per_chain(gate, bsz),
                [ab[b].astype(F32) for b in range(bsz)], [ssave[b, hh] for b, hh in chains],
                _per_head(cwq, bsz), _per_head(cwk, bsz), _per_head(cwv, bsz), al[...], dt[...], on[...])
        _, vjp = jax.vjp(functools.partial(_gdn_chunk, heads=[hh for _, hh in chains], seqs=[b for b, _ in chains]),
                         *args)
        (gxq, gxk, gxv, ghq, ghk, ghv, ggate, gab, gS, gcq, gck, gcv, gal, gdt, gon) = vjp(
            (_per_chain(do, bsz), [ds_ref[b, hh] for b, hh in chains]))
        d = D_MODEL
        for e, (b, hh) in enumerate(chains):
            c0 = hh * HEAD_DIM
            for part, (gx, gh, gc) in enumerate(((gxq, ghq, gcq), (gxk, ghk, gck), (gxv, ghv, gcv))):
                full = gx[e] + jnp.concatenate([pad, dhalo_ref[b, hh, part]], axis=0)
                dp_ref[b, :, part * d + c0:part * d + c0 + HEAD_DIM] = full.astype(dp_ref.dtype)
                dhalo_ref[b, hh, part] = gh[e] * keep
                dcw_ref[hh, part] += gc[e]
            dp_ref[b, :, 3 * d + c0:3 * d + c0 + HEAD_DIM] = ggate[e].astype(dp_ref.dtype)
            ds_ref[b, hh] = gS[e]
        for b in range(bsz):
            dp_ref[b, :, GDN_MAIN:GDN_IN_PAD] = gab[b].astype(dp_ref.dtype)
        dal_ref[...] += gal
        ddt_ref[...] += gdt
        don_ref[...] += gon

    cw_spec = lambda col: pl.BlockSpec((CONV_K, D_MODEL), lambda n: (0, col))
    in_specs = [_chunk_spec(bsz, nc, 0, True), _chunk_spec(bsz, nc, 1, True), _chunk_spec(bsz, nc, 2, True),
                _halo_spec(bsz, nc, 0, True), _halo_spec(bsz, nc, 1, True), _halo_spec(bsz, nc, 2, True),
                _chunk_spec(bsz, nc, 3, True), _chunk_spec(bsz, nc, GDN_MAIN // 128, True, width=128),
                cw_spec(0), cw_spec(1), cw_spec(2),
                _const_spec((1, 128)), _const_spec((1, 128)), _const_spec((1, 128)),
                _state_spec(bsz, nc, True), _chunk_spec(bsz, nc, 0, True)]
    out_specs = [_chunk_spec(bsz, nc, 0, True, width=GDN_IN_PAD),
                 _const_spec((HEADS, 3, CONV_K, HEAD_DIM)), _const_spec((1, 128)), _const_spec((1, 128)),
                 _const_spec((1, 128))]
    row = jax.ShapeDtypeStruct((1, 128), F32)
    p3 = proj.reshape(bsz, t_len, -1)
    (dp, dcw, dal, ddt, don), got = _pcall(
        body, name=name, grid=(nc,), in_specs=in_specs, out_specs=out_specs,
        out_shape=(jax.ShapeDtypeStruct((bsz, t_len, GDN_IN_PAD), BF16),
                   jax.ShapeDtypeStruct((HEADS, 3, CONV_K, HEAD_DIM), F32), row, row, row),
        scratch_shapes=[pltpu.VMEM((bsz, HEADS, HEAD_DIM, HEAD_DIM), F32),
                        pltpu.VMEM((bsz, HEADS, 3, HALO, HEAD_DIM), F32)],
        dims=("arbitrary",),
        args=(p3, p3, p3, p3, p3, p3, p3, p3, conv_w, conv_w, conv_w, alog, dtb, onorm, s_saved,
              d_out.reshape(bsz, t_len, D_MODEL)),
        carries=carries)
    return (dp.reshape(n_tok, GDN_IN_PAD), dcw, dal, ddt, don), got


def hgrn_forward(proj, lbl, layer, bsz, carries=()):
    n_tok = proj.shape[0]
    t_len = n_tok // bsz
    nc = t_len // CHUNK
    chains = _chains(bsz)

    def body(qp, fp, vi, lb, o_ref, ssave_ref, s_ref):
        @pl.when(pl.program_id(0) == 0)
        def _():
            s_ref[...] = jnp.zeros_like(s_ref)

        S = [s_ref[b, hh] for b, hh in chains]
        for (b, hh), s in zip(chains, S):
            ssave_ref[b, hh] = s
        outs, s_next = _hgrn_chunk(_per_chain(qp, bsz), _per_chain(fp, bsz), _per_chain(vi, bsz), S,
                                   _per_head(lb, bsz), layer=layer)
        for (b, hh), o, s in zip(chains, outs, s_next):
            o_ref[b, :, _lanes(hh)] = o
            s_ref[b, hh] = s

    in_specs = [_chunk_spec(bsz, nc, 0), _chunk_spec(bsz, nc, 1), _chunk_spec(bsz, nc, 2),
                _const_spec((DEPTH, D_MODEL))]
    out_specs = [_chunk_spec(bsz, nc, 0), _state_spec(bsz, nc)]
    p3 = proj.reshape(bsz, t_len, -1)
    (o, ssave), got = _pcall(
        body, name=f"hgrn_fwd{layer}", grid=(nc,), in_specs=in_specs, out_specs=out_specs,
        out_shape=(jax.ShapeDtypeStruct((bsz, t_len, D_MODEL), F32),
                   jax.ShapeDtypeStruct((bsz, nc, HEADS, HEAD_DIM, HEAD_DIM), F32)),
        scratch_shapes=[pltpu.VMEM((bsz, HEADS, HEAD_DIM, HEAD_DIM), F32)], dims=("arbitrary",),
        args=(p3, p3, p3, lbl), carries=carries)
    return (o.reshape(n_tok, D_MODEL), ssave), got


def hgrn_backward(proj, lbl, s_saved, d_o, d_gate, layer, bsz, carries=()):
    n_tok = proj.shape[0]
    t_len = n_tok // bsz
    nc = t_len // CHUNK
    chains = _chains(bsz)

    def body(qp, fp, vi, lb, ssave, do, dgt, dp_ref, dlb_ref, ds_ref):
        @pl.when(pl.program_id(0) == 0)
        def _():
            dlb_ref[...] = jnp.zeros_like(dlb_ref)
            ds_ref[...] = jnp.zeros_like(ds_ref)

        _, vjp = jax.vjp(functools.partial(_hgrn_chunk, layer=layer), _per_chain(qp, bsz), _per_chain(fp, bsz),
                         _per_chain(vi, bsz), [ssave[b, hh] for b, hh in chains], _per_head(lb, bsz))
        gq, gf, gv, gS, glb = vjp((_per_chain(do, bsz), [ds_ref[b, hh] for b, hh in chains]))
        d = D_MODEL
        for e, (b, hh) in enumerate(chains):
            c0 = hh * HEAD_DIM
            dp_ref[b, :, c0:c0 + HEAD_DIM] = gq[e].astype(dp_ref.dtype)
            dp_ref[b, :, d + c0:d + c0 + HEAD_DIM] = gf[e].astype(dp_ref.dtype)
            dp_ref[b, :, 2 * d + c0:2 * d + c0 + HEAD_DIM] = gv[e].astype(dp_ref.dtype)
            dlb_ref[hh] += glb[e]
            ds_ref[b, hh] = gS[e]
        dp_ref[:, :, 3 * d:4 * d] = dgt[...]

    in_specs = [_chunk_spec(bsz, nc, 0, True), _chunk_spec(bsz, nc, 1, True), _chunk_spec(bsz, nc, 2, True),
                _const_spec((DEPTH, D_MODEL)), _state_spec(bsz, nc, True),
                _chunk_spec(bsz, nc, 0, True), _chunk_spec(bsz, nc, 0, True)]
    out_specs = [_chunk_spec(bsz, nc, 0, True, width=4 * D_MODEL), _const_spec((HEADS, DEPTH, HEAD_DIM))]
    p3 = proj.reshape(bsz, t_len, -1)
    (dp, dlb), got = _pcall(
        body, name=f"hgrn_bwd{layer}", grid=(nc,), in_specs=in_specs, out_specs=out_specs,
        out_shape=(jax.ShapeDtypeStruct((bsz, t_len, 4 * D_MODEL), BF16),
                   jax.ShapeDtypeStruct((HEADS, DEPTH, HEAD_DIM), F32)),
        scratch_shapes=[pltpu.VMEM((bsz, HEADS, HEAD_DIM, HEAD_DIM), F32)], dims=("arbitrary",),
        args=(p3, p3, p3, lbl, s_saved, d_o.reshape(bsz, t_len, D_MODEL), d_gate.reshape(bsz, t_len, D_MODEL)),
        carries=carries)
    return (dp.reshape(n_tok, 4 * D_MODEL), dlb), got


ROW_TILE = 512
MM_VMEM_BUDGET = 36 * 1024 * 1024


def _tile(n):
    for cand in (1024, 512, 1408, 384, 256, 128):
        if n % cand == 0:
            return cand
    return n


def _rmsnorm(x, w):
    return x * lax.rsqrt(jnp.mean(x * x, axis=1, keepdims=True) + EPS) * w


def norm_matmul(h, nw, w, relu, name, carries=()):
    n_tok, d = h.shape
    slots = w.shape[0] if w.ndim == 3 else 0
    nb = w.shape[2] if slots else w.shape[1]
    n_out = max(slots, 1) * nb
    tm = min(n_tok, ROW_TILE)
    if slots:
        w_specs = [pl.BlockSpec((None, d, nb), lambda i, s=s: (s, 0, 0)) for s in range(slots)]
    else:
        w_specs = [pl.BlockSpec((d, nb), lambda i: (0, 0))]
    nw_ = len(w_specs)

    def body(*refs):
        h_ref, nw_ref, w_refs = refs[0], refs[1], refs[2:2 + nw_]
        y_ref, u_ref = refs[2 + nw_:]
        y = _rmsnorm(h_ref[...], nw_ref[...]).astype(BF16)
        y_ref[...] = y
        for s, w_ref in enumerate(w_refs):
            acc = jnp.dot(y, w_ref[...], preferred_element_type=F32)
            u_ref[:, s * nb:(s + 1) * nb] = (jnp.maximum(acc, 0.0) if relu else acc).astype(BF16)

    row = pl.BlockSpec((tm, d), lambda i: (i, 0))
    return _pcall(
        body, name=name, grid=(n_tok // tm,),
        in_specs=[row, pl.BlockSpec((1, d), lambda i: (0, 0))] + w_specs,
        out_specs=[row, pl.BlockSpec((tm, n_out), lambda i: (i, 0))],
        out_shape=(jax.ShapeDtypeStruct((n_tok, d), BF16), jax.ShapeDtypeStruct((n_tok, n_out), BF16)),
        dims=("parallel",), args=[h, nw] + [w] * nw_, carries=carries)


def _mm_tiles(m, n, k, extra_bytes):
    tn = _tile(n)
    for tm in (1024, 512, 256, 128):
        if m % tm == 0 and 2 * (2 * tm * k + 2 * k * tn + (4 + extra_bytes) * tm * tn) <= MM_VMEM_BUDGET:
            return tm, tn
    return min(m, 128), tn


def matmul(a, b, mode, name, out_dtype=F32, extra=None, epilogue=None, shards=1, carries=(), square_a=False):
    if mode == "nn":
        (m, k), n = a.shape, b.shape[1]
    elif mode == "nt":
        (m, k), n = a.shape, b.shape[0]
    else:
        (k, m), n = a.shape, b.shape[1]
    tm, tn = _mm_tiles(m, n // shards, k, 0 if extra is None else extra.dtype.itemsize)
    a_spec = pl.BlockSpec((k, tm), lambda i, j: (0, i)) if mode == "tn" else pl.BlockSpec((tm, k), lambda i, j: (i, 0))
    b_spec = pl.BlockSpec((tn, k), lambda i, j: (j, 0)) if mode == "nt" else pl.BlockSpec((k, tn), lambda i, j: (0, j))
    dims = _DIMS[mode]

    def body(*refs):
        a_ref, b_ref = refs[0], refs[1]
        e_ref = refs[2] if extra is not None else None
        o_ref = refs[-1]
        av = a_ref[...].astype(BF16)
        if square_a:
            av = av * av
        acc = lax.dot_general(av, b_ref[...].astype(BF16), dims, preferred_element_type=F32)
        if epilogue == "add":
            acc = e_ref[...] + acc
        elif epilogue == "mul2":
            acc = acc * (2.0 * e_ref[...].astype(F32))
        o_ref[...] = acc.astype(o_ref.dtype)

    in_specs = [a_spec, b_spec]
    args = [a, b]
    if extra is not None:
        in_specs.append(pl.BlockSpec((tm, tn), lambda i, j: (i, j)))
        args.append(extra)
    if shards > 1:
        per = n // shards // tn
        out_spec = pl.BlockSpec((None, tm, tn), lambda i, j: (j // per, i, j % per))
        out_shape = jax.ShapeDtypeStruct((shards, m, n // shards), out_dtype)
    else:
        out_spec = pl.BlockSpec((tm, tn), lambda i, j: (i, j))
        out_shape = jax.ShapeDtypeStruct((m, n), out_dtype)
    (res,), cres = _pcall(
        body, name=name, grid=(m // tm, n // tn), in_specs=in_specs, out_specs=[out_spec], out_shape=[out_shape],
        dims=("parallel", "arbitrary"), args=args, carries=carries)
    return res, cres


def matmul_norm_backward(a, b, h, nw, dres, name, carries=()):
    slots = b.shape[0] if b.ndim == 3 else 0
    n_tok, k = a.shape
    d = h.shape[1]
    tm = min(n_tok, ROW_TILE // 2)
    if slots:
        kb = b.shape[2]
        b_specs = [pl.BlockSpec((None, d, kb), lambda i, s=s: (s, 0, 0)) for s in range(slots)]
    else:
        b_specs = [pl.BlockSpec((d, k), lambda i: (0, 0))]
    nb = len(b_specs)

    def body(*refs):
        a_ref, b_refs = refs[0], refs[1:1 + nb]
        h_ref, nw_ref, dr_ref, dh_ref, dhb_ref, dnw_ref = refs[1 + nb:]

        @pl.when(pl.program_id(0) == 0)
        def _():
            dnw_ref[...] = jnp.zeros_like(dnw_ref)

        if slots:
            dy = None
            for s in range(slots):
                part = lax.dot_general(a_ref[:, s * kb:(s + 1) * kb], b_refs[s][...], _DIMS["nt"],
                                       preferred_element_type=F32)
                dy = part if dy is None else dy + part
        else:
            dy = lax.dot_general(a_ref[...], b_refs[0][...], _DIMS["nt"], preferred_element_type=F32)
        _, vjp = jax.vjp(_rmsnorm, h_ref[...], nw_ref[...])
        gh, gw = vjp(dy)
        dh = dr_ref[...] + gh
        dh_ref[...] = dh
        dhb_ref[...] = dh.astype(BF16)
        dnw_ref[...] += gw

    row = pl.BlockSpec((tm, d), lambda i: (i, 0))
    vec = pl.BlockSpec((1, d), lambda i: (0, 0))
    return _pcall(
        body, name=name, grid=(n_tok // tm,),
        in_specs=[pl.BlockSpec((tm, k), lambda i: (i, 0))] + b_specs + [row, vec, row], out_specs=[row, row, vec],
        out_shape=(jax.ShapeDtypeStruct((n_tok, d), F32), jax.ShapeDtypeStruct((n_tok, d), BF16),
                   jax.ShapeDtypeStruct((1, d), F32)),
        dims=("arbitrary",), args=[a] + [b] * nb + [h, nw, dres], carries=carries)


def _hgrn_post(o, gate, gw):
    return _rmsnorm(o, gw) * _silu(gate)


def hgrn_post_forward(o, proj, gw, name):
    n_tok, d = o.shape
    tm = min(n_tok, ROW_TILE)

    def body(o_ref, g_ref, w_ref, y_ref):
        y_ref[...] = _hgrn_post(o_ref[...], g_ref[...].astype(F32), w_ref[...]).astype(BF16)

    row = pl.BlockSpec((tm, d), lambda i: (i, 0))
    return pl.pallas_call(
        body, name=name, grid=(n_tok // tm,),
        in_specs=[row, pl.BlockSpec((tm, d), lambda i: (i, 3)), pl.BlockSpec((1, d), lambda i: (0, 0))],
        out_specs=row, out_shape=jax.ShapeDtypeStruct((n_tok, d), BF16),
        compiler_params=_cparams(("parallel",)),
    )(o, proj, gw)


def hgrn_post_backward(o, proj, gw, dy, name):
    n_tok, d = o.shape
    tm = min(n_tok, ROW_TILE)

    def body(o_ref, g_ref, w_ref, dy_ref, do_ref, dg_ref, dw_ref):
        @pl.when(pl.program_id(0) == 0)
        def _():
            dw_ref[...] = jnp.zeros_like(dw_ref)

        _, vjp = jax.vjp(_hgrn_post, o_ref[...], g_ref[...].astype(F32), w_ref[...])
        go, gg, gw_ = vjp(dy_ref[...])
        do_ref[...] = go
        dg_ref[...] = gg.astype(BF16)
        dw_ref[...] += gw_

    row = pl.BlockSpec((tm, d), lambda i: (i, 0))
    vec = pl.BlockSpec((1, d), lambda i: (0, 0))
    return pl.pallas_call(
        body, name=name, grid=(n_tok // tm,),
        in_specs=[row, pl.BlockSpec((tm, d), lambda i: (i, 3)), vec, row], out_specs=[row, row, vec],
        out_shape=(jax.ShapeDtypeStruct((n_tok, d), F32), jax.ShapeDtypeStruct((n_tok, d), BF16),
                   jax.ShapeDtypeStruct((1, d), F32)),
        compiler_params=_cparams(("arbitrary",)),
    )(o, proj, gw, dy)


def loss_head(h, nw, target):
    n_tok, d = h.shape
    tm = min(n_tok, ROW_TILE)

    def body(h_ref, nw_ref, t_ref, loss_ref, dh_ref, dhb_ref, dnw_ref):
        @pl.when(pl.program_id(0) == 0)
        def _():
            dnw_ref[...] = jnp.zeros_like(dnw_ref)
            loss_ref[...] = jnp.zeros_like(loss_ref)

        out, vjp = jax.vjp(_rmsnorm, h_ref[...], nw_ref[...])
        err = out - t_ref[...]
        part = 0.5 * jnp.sum(jnp.sum(err * err, axis=1, keepdims=True), axis=0, keepdims=True) / d
        loss_ref[...] += jnp.broadcast_to(part, loss_ref.shape)
        gh, gw = vjp(err / d)
        dh_ref[...] = gh
        dhb_ref[...] = gh.astype(BF16)
        dnw_ref[...] += gw

    row = pl.BlockSpec((tm, d), lambda i: (i, 0))
    vec = pl.BlockSpec((1, d), lambda i: (0, 0))
    return pl.pallas_call(
        body, name="loss_head", grid=(n_tok // tm,), in_specs=[row, vec, row],
        out_specs=[pl.BlockSpec((1, 128), lambda i: (0, 0)), row, row, vec],
        out_shape=(jax.ShapeDtypeStruct((1, 128), F32), jax.ShapeDtypeStruct((n_tok, d), F32),
                   jax.ShapeDtypeStruct((n_tok, d), BF16), jax.ShapeDtypeStruct((1, d), F32)),
        compiler_params=_cparams(("arbitrary",)),
    )(h, nw, target)


def _rows2d(shape):
    if len(shape) == 1:
        return (1, shape[0])
    return (math.prod(shape[:-1]), shape[-1])


def adamw(w, g, m, v, name):
    shape = w.shape
    r, c = _rows2d(shape)
    tr = r if r <= 256 else 256
    c1 = 1.0 / (1.0 - ADAM_B1 ** ADAM_STEP)
    c2 = 1.0 / (1.0 - ADAM_B2 ** ADAM_STEP)

    def body(w_ref, g_ref, m_ref, v_ref, d_ref, nm_ref, nv_ref):
        gg = g_ref[...]
        nm = ADAM_B1 * m_ref[...] + (1.0 - ADAM_B1) * gg
        nv = ADAM_B2 * v_ref[...] + (1.0 - ADAM_B2) * (gg * gg)
        d_ref[...] = -ADAM_LR * ((nm * c1) / (jnp.sqrt(nv * c2) + ADAM_EPS) + ADAM_WD * w_ref[...])
        nm_ref[...] = nm
        nv_ref[...] = nv

    spec = pl.BlockSpec((tr, c), lambda i: (i, 0))
    sds = jax.ShapeDtypeStruct((r, c), F32)
    outs = pl.pallas_call(
        body, name=name, grid=(r // tr,), in_specs=[spec] * 4, out_specs=[spec] * 3, out_shape=(sds,) * 3,
        compiler_params=_cparams(("parallel",)),
    )(w.reshape(r, c), g.reshape(r, c), m.reshape(r, c), v.reshape(r, c))
    return tuple(o.reshape(shape) for o in outs)


def add_slots(parts, name):
    s, r, c = parts.shape
    tr = r if r <= 256 else 256

    def body(p_ref, o_ref):
        acc = p_ref[0]
        for t in range(1, s):
            acc = acc + p_ref[t]
        o_ref[...] = acc

    return pl.pallas_call(
        body, name=name, grid=(r // tr,), in_specs=[pl.BlockSpec((s, tr, c), lambda i: (0, i, 0))],
        out_specs=pl.BlockSpec((tr, c), lambda i: (i, 0)), out_shape=jax.ShapeDtypeStruct((r, c), F32),
        compiler_params=_cparams(("parallel",)),
    )(parts)


BIG = ("gdn_w_in", "gdn_w_out", "hgrn_w_in", "hgrn_w_out", "mlp_w_up", "mlp_w_down")
WEIGHTS = ("gdn_w_in", "gdn_conv", "gdn_a_log", "gdn_dt_bias", "gdn_onorm", "gdn_w_out", "hgrn_w_in", "hgrn_lb_logits",
           "hgrn_gnorm", "hgrn_w_out", "norm_mix", "norm_mlp", "mlp_w_up", "mlp_w_down", "norm_final")


def _pad_lanes(v, n):
    return jnp.pad(v, [(0, 0)] * (v.ndim - 1) + [(0, n - v.shape[-1])])


_SMALL_LAYOUT = {
    "norm_mix": (0, 4, D_MODEL), "norm_mlp": (8, 4, D_MODEL), "norm_final": (16, 1, D_MODEL),
    "hgrn_lb_logits": (24, 4, D_MODEL), "gdn_onorm": (32, 2, 128), "gdn_a_log": (40, 2, HEADS),
    "gdn_dt_bias": (48, 2, HEADS), "loss": (56, 1, 128), "gdn_conv": (64, 24, D_MODEL), "hgrn_gnorm": (88, 2, D_MODEL),
}


def _pack_small(small, loss_row):
    rows = []
    for name, (first, nrow, lanes) in _SMALL_LAYOUT.items():
        v = loss_row if name == "loss" else small[name]
        v = _pad_lanes(v.reshape(nrow, -1), D_MODEL)
        rows.append(jnp.pad(v, ((0, -nrow % 8), (0, 0))))
    return jnp.concatenate(rows, axis=0)


def _unpack_small(packed, name, shape):
    first, nrow, lanes = _SMALL_LAYOUT[name]
    return packed[first:first + nrow, :lanes].reshape(shape)


def add_core_halves(g, theirs, core, name):
    s, r, c = g.shape
    r2 = r // 2
    tr = min(r2, 256)
    nb = r2 // tr

    def body(core_ref, g_ref, t_ref, o_ref):
        o_ref[...] = g_ref[...] + t_ref[...]

    grid_spec = pltpu.PrefetchScalarGridSpec(
        num_scalar_prefetch=1, grid=(s, nb),
        in_specs=[pl.BlockSpec((None, tr, c), lambda t, i, cr: (t, cr[0] * nb + i, 0)),
                  pl.BlockSpec((None, tr, c), lambda t, i, cr: (t, i, 0))],
        out_specs=pl.BlockSpec((None, tr, c), lambda t, i, cr: (t, i, 0)))
    return pl.pallas_call(body, name=name, grid_spec=grid_spec, out_shape=jax.ShapeDtypeStruct((s, r2, c), F32),
                          compiler_params=_cparams(("parallel", "parallel")))(core, g, theirs)


def add_chip_slots(slots, name):
    n_l = len(slots)
    s, r2, c = slots[0].shape
    tr = min(r2, 256)
    nb = r2 // tr

    def body(*refs):
        ins, o_ref = refs[:n_l], refs[n_l]
        for k in range(n_l):
            @pl.when(pl.program_id(0) == k)
            def _(k=k):
                acc = ins[k][0]
                for t in range(1, s):
                    acc = acc + ins[k][t]
                o_ref[...] = acc

    in_specs = [pl.BlockSpec((s, tr, c), lambda l, i, k=k: (0, jnp.where(l == k, i, 0), 0)) for k in range(n_l)]
    return pl.pallas_call(
        body, name=name, grid=(n_l, nb), in_specs=in_specs, out_specs=pl.BlockSpec((None, tr, c), lambda l, i: (l, i, 0)),
        out_shape=jax.ShapeDtypeStruct((n_l, r2, c), F32), compiler_params=_cparams(("arbitrary", "arbitrary")),
    )(*slots)


def adamw_halves(w, m, v, mine, theirs, name):
    n_l, r, c = w.shape
    r2 = r // 2
    tr = min(r2, 256)
    nb = r2 // tr
    c1 = 1.0 / (1.0 - ADAM_B1 ** ADAM_STEP)
    c2 = 1.0 / (1.0 - ADAM_B2 ** ADAM_STEP)

    def body(w_ref, m_ref, v_ref, mine_ref, theirs_ref, g_ref, d_ref, nm_ref, nv_ref):
        my_half = (pl.program_id(1) // nb) == lax.axis_index("c")
        gg = jnp.where(my_half, mine_ref[...], theirs_ref[...])
        nm = ADAM_B1 * m_ref[...] + (1.0 - ADAM_B1) * gg
        nv = ADAM_B2 * v_ref[...] + (1.0 - ADAM_B2) * (gg * gg)
        g_ref[...] = gg
        d_ref[...] = -ADAM_LR * ((nm * c1) / (jnp.sqrt(nv * c2) + ADAM_EPS) + ADAM_WD * w_ref[...])
        nm_ref[...] = nm
        nv_ref[...] = nv

    full = pl.BlockSpec((None, tr, c), lambda l, i: (l, i, 0))
    half = pl.BlockSpec((None, tr, c), lambda l, i: (l, i % nb, 0))
    sds = jax.ShapeDtypeStruct((n_l, r, c), F32)
    return pl.pallas_call(
        body, name=name, grid=(n_l, r // tr), in_specs=[full, full, full, half, half], out_specs=[full] * 4,
        out_shape=(sds,) * 4, compiler_params=_cparams(("parallel", "parallel")),
    )(w, m, v, mine, theirs)


def _layer_weight(kind, i):
    if kind == "up":
        return "mlp_w_up", i
    if kind == "down":
        return "mlp_w_down", i
    return ("gdn_w_" if i % 2 == 0 else "hgrn_w_") + kind, i // 2


def kernel(x, gdn_w_in, gdn_conv, gdn_a_log, gdn_dt_bias, gdn_onorm, gdn_w_out, hgrn_w_in, hgrn_lb_logits, hgrn_gnorm, hgrn_w_out, norm_mix, norm_mlp, mlp_w_up, mlp_w_down, norm_final, loss_target, m_gdn_w_in, m_gdn_conv, m_gdn_a_log, m_gdn_dt_bias, m_gdn_onorm, m_gdn_w_out, m_hgrn_w_in, m_hgrn_lb_logits, m_hgrn_gnorm, m_hgrn_w_out, m_norm_mix, m_norm_mlp, m_mlp_w_up, m_mlp_w_down, m_norm_final, v_gdn_w_in, v_gdn_conv, v_gdn_a_log, v_gdn_dt_bias, v_gdn_onorm, v_gdn_w_out, v_hgrn_w_in, v_hgrn_lb_logits, v_hgrn_gnorm, v_hgrn_w_out, v_norm_mix, v_norm_mlp, v_mlp_w_up, v_mlp_w_down, v_norm_final):
    p = dict(gdn_w_in=gdn_w_in, gdn_conv=gdn_conv, gdn_a_log=gdn_a_log, gdn_dt_bias=gdn_dt_bias, gdn_onorm=gdn_onorm,
             gdn_w_out=gdn_w_out, hgrn_w_in=hgrn_w_in, hgrn_lb_logits=hgrn_lb_logits, hgrn_gnorm=hgrn_gnorm,
             hgrn_w_out=hgrn_w_out, norm_mix=norm_mix, norm_mlp=norm_mlp, mlp_w_up=mlp_w_up, mlp_w_down=mlp_w_down,
             norm_final=norm_final)
    m = dict(gdn_w_in=m_gdn_w_in, gdn_conv=m_gdn_conv, gdn_a_log=m_gdn_a_log, gdn_dt_bias=m_gdn_dt_bias,
             gdn_onorm=m_gdn_onorm, gdn_w_out=m_gdn_w_out, hgrn_w_in=m_hgrn_w_in, hgrn_lb_logits=m_hgrn_lb_logits,
             hgrn_gnorm=m_hgrn_gnorm, hgrn_w_out=m_hgrn_w_out, norm_mix=m_norm_mix, norm_mlp=m_norm_mlp,
             mlp_w_up=m_mlp_w_up, mlp_w_down=m_mlp_w_down, norm_final=m_norm_final)
    v = dict(gdn_w_in=v_gdn_w_in, gdn_conv=v_gdn_conv, gdn_a_log=v_gdn_a_log, gdn_dt_bias=v_gdn_dt_bias,
             gdn_onorm=v_gdn_onorm, gdn_w_out=v_gdn_w_out, hgrn_w_in=v_hgrn_w_in, hgrn_lb_logits=v_hgrn_lb_logits,
             hgrn_gnorm=v_hgrn_gnorm, hgrn_w_out=v_hgrn_w_out, norm_mix=v_norm_mix, norm_mlp=v_norm_mlp,
             mlp_w_up=v_mlp_w_up, mlp_w_down=v_mlp_w_down, norm_final=v_norm_final)
    xi, yi, ci = _mesh_pos()
    chip = 2 * xi + yi
    core = jnp.reshape(ci, (1,)).astype(jnp.int32)
    d = D_MODEL
    bsz, t_len, _ = x.shape
    n_tok = bsz * t_len

    def shard(kind, i):
        name, idx = _layer_weight(kind, i)
        return p[name][idx].astype(BF16)

    def w_in_of(i, slots):
        if i % 2 == 0:
            return _pad_lanes(jnp.transpose(slots, (1, 0, 2)).reshape(d, GDN_IN), GDN_IN_PAD)
        return slots

    (first,) = run_carries([gather_carry([shard("in", 0), p["gdn_conv"], p["hgrn_gnorm"]])], "gather_first")
    conv = jnp.transpose(first[1], (1, 2, 0, 3)).reshape(DEPTH // 2, CONV_K, 3 * d)
    gnorm = jnp.transpose(first[2], (1, 0, 2)).reshape(DEPTH // 2, d)
    lbl = p["hgrn_lb_logits"]
    h = x.reshape(n_tok, d)
    next_in, next_out = first[0], None
    saved = []
    for i in range(DEPTH):
        j = i // 2
        w_in = w_in_of(i, next_in)
        nmix = p["norm_mix"][i][None, :]
        (y, proj), got = norm_matmul(h, nmix, w_in, False, f"in_proj{i}",
                                     [gather_carry([shard("out", 0)])] if i == 0 else [])
        if i == 0:
            next_out = got[0][0]
        ride = [gather_carry([shard("up", i), shard("down", i)])]
        if i % 2 == 0:
            al = _pad_lanes(p["gdn_a_log"][j][None, :], 128)
            dtb = _pad_lanes(p["gdn_dt_bias"][j][None, :], 128)
            on = p["gdn_onorm"][j][None, :]
            (og, ssave), got = gdn_forward(proj, conv[j], al, dtb, on, bsz, f"gdn_fwd{i}", ride)
            mix = (proj, ssave, al, dtb, on)
        else:
            (o, ssave), got = hgrn_forward(proj, lbl, i, bsz, ride)
            gn = gnorm[j][None, :]
            og = hgrn_post_forward(o, proj, gn, f"hgrn_post{i}")
            mix = (proj, ssave, o, gn)
        w_up, w_down = got[0][0], got[0][1].reshape(MLP_HIDDEN, d)
        w_out = next_out.reshape(d, d)
        h1, _ = matmul(og, w_out, "nn", f"out_proj{i}", extra=h, epilogue="add")
        nmlp = p["norm_mlp"][i][None, :]
        (z, r), got = norm_matmul(h1, nmlp, w_up, True, f"mlp_up{i}",
                                  [gather_carry([shard("out", i + 1)])] if i + 1 < DEPTH else [])
        if i + 1 < DEPTH:
            next_out = got[0][0]
        h2, got = matmul(r, w_down, "nn", f"mlp_down{i}", extra=h1, epilogue="add", square_a=True,
                         carries=[gather_carry([shard("in", i + 1)])] if i + 1 < DEPTH else [])
        if i + 1 < DEPTH:
            next_in = got[0][0]
        saved.append((h, nmix, y, mix, og, w_in, w_out, h1, nmlp, z, r, w_up, w_down))
        h = h2
    loss_row, dh, dhb, d_nf = loss_head(h, p["norm_final"][None, :], loss_target.reshape(n_tok, d))

    G = {k: [None] * DEPTH for k in ("in", "out", "up", "down")}
    P = {k: [None] * DEPTH for k in ("in", "out", "up", "down")}
    slots = {k: [None] * DEPTH for k in ("in", "out", "up", "down")}
    d_nmix, d_nmlp = [None] * DEPTH, [None] * DEPTH
    d_conv, d_alog, d_dtb, d_onorm, d_gnorm = [None] * 2, [None] * 2, [None] * 2, [None] * 2, [None] * 2
    d_lbl = jnp.zeros((DEPTH, d), F32)
    for i in reversed(range(DEPTH)):
        j = i // 2
        h_in, nmix, y, mix, og, w_in, w_out, h1, nmlp, z, r, w_up, w_down = saved[i]
        ride = [sibling_half_carry([G["in"][i + 1]])] if i + 1 < DEPTH else []
        du, got = matmul(dhb, w_down, "nt", f"d_mlp_act{i}", out_dtype=BF16, extra=r, epilogue="mul2", carries=ride)
        if i + 1 < DEPTH:
            P["in"][i + 1] = add_core_halves(G["in"][i + 1], got[0][0], core, f"add_cores_in{i + 1}")
        G["down"][i] = matmul(r, dhb, "tn", f"dw_down{i}", square_a=True)[0].reshape(N_CHIPS, -1, d)
        G["up"][i], _ = matmul(z, du, "tn", f"dw_up{i}", shards=N_CHIPS)
        (dh1, dh1b, d_nmlp[i]), got = matmul_norm_backward(du, w_up, h1, nmlp, dh, f"d_mlp_in{i}",
                                                           [sibling_half_carry([G["up"][i], G["down"][i]])])
        P["up"][i] = add_core_halves(G["up"][i], got[0][0], core, f"add_cores_up{i}")
        P["down"][i] = add_core_halves(G["down"][i], got[0][1], core, f"add_cores_down{i}")
        dog, _ = matmul(dh1b, w_out, "nt", f"d_mix_out{i}")
        G["out"][i] = matmul(og, dh1b, "tn", f"dw_out{i}")[0].reshape(N_CHIPS, -1, d)
        to_chips = [("up", i), ("down", i)] + ([("in", i + 1), ("out", i + 1)] if i + 1 < DEPTH else [])
        ride = [sibling_half_carry([G["out"][i]]), chips_carry([P[k][l] for k, l in to_chips])]
        if i % 2 == 0:
            proj, ssave, al, dtb, on = mix
            (dproj, dcw, dal, ddt, don), got = gdn_backward(proj, conv[j], al, dtb, on, ssave, dog, bsz, f"gdn_bwd{i}", ride)
            d_conv[j] = jnp.transpose(dcw, (2, 1, 0, 3)).reshape(CONV_K, 3 * d)
            d_alog[j], d_dtb[j], d_onorm[j] = dal[0, :HEADS], ddt[0, :HEADS], don[0]
        else:
            proj, ssave, o, gn = mix
            do_raw, dgate, dgn = hgrn_post_backward(o, proj, gn, dog, f"d_hgrn_post{i}")
            (dproj, dlb), got = hgrn_backward(proj, lbl, ssave, do_raw, dgate, i, bsz, ride)
            d_gnorm[j] = dgn[0]
            d_lbl = d_lbl + jnp.transpose(dlb, (1, 0, 2)).reshape(DEPTH, d)
        P["out"][i] = add_core_halves(G["out"][i], got[0][0], core, f"add_cores_out{i}")
        for (k, l), s in zip(to_chips, got[1]):
            slots[k][l] = s
        if i % 2 == 0:
            dw_in = matmul(y, dproj, "tn", f"dw_in{i}")[0][:, :GDN_IN]
            G["in"][i] = jnp.transpose(dw_in.reshape(d, N_CHIPS, GDN_IN // N_CHIPS), (1, 0, 2))
        else:
            G["in"][i], _ = matmul(y, dproj, "tn", f"dw_in{i}", shards=N_CHIPS)
        ride = [sibling_half_carry([G["in"][0]]), chips_carry([P["out"][0]])] if i == 0 else []
        (dh, dhb, d_nmix[i]), got = matmul_norm_backward(dproj, w_in, h_in, nmix, dh1, f"d_in_proj{i}", ride)
        if i == 0:
            P["in"][0] = add_core_halves(G["in"][0], got[0][0], core, "add_cores_in0")
            slots["out"][0] = got[1][0]
    grad_x = dh.reshape(x.shape)
    ((slots["in"][0],),) = run_carries([chips_carry([P["in"][0]])], "reduce_chips_last")

    by_weight = {}
    for kind in ("in", "out", "up", "down"):
        for i in range(DEPTH):
            by_weight.setdefault(_layer_weight(kind, i)[0], []).append(slots[kind][i])
    mine = {name: add_chip_slots(by_weight[name], f"add_chips_{name}") for name in BIG}
    small = {
        "gdn_conv": jnp.stack(d_conv), "gdn_a_log": jnp.stack(d_alog), "gdn_dt_bias": jnp.stack(d_dtb),
        "gdn_onorm": jnp.stack(d_onorm), "hgrn_lb_logits": d_lbl, "hgrn_gnorm": jnp.stack(d_gnorm),
        "norm_mix": jnp.concatenate(d_nmix, axis=0), "norm_mlp": jnp.concatenate(d_nmlp, axis=0), "norm_final": d_nf[0],
    }
    theirs, (blocks,) = run_carries([sibling_copy_carry([mine[name] for name in BIG]),
                                     gather_all_carry(_pack_small(small, loss_row))], "share_cores")
    theirs = dict(zip(BIG, theirs))

    total = add_slots(blocks, "add_small")
    loss = total[_SMALL_LAYOUT["loss"][0], 0]
    grads, delta, new_m, new_v = {}, {}, {}, {}
    for name in WEIGHTS:
        if name in BIG:
            grads[name], delta[name], new_m[name], new_v[name] = adamw_halves(
                p[name], m[name], v[name], mine[name], theirs[name], f"adamw_{name}")
            continue
        if name == "gdn_conv":
            full = _unpack_small(total, name, (2, CONV_K, 3 * d))
            grads[name] = lax.dynamic_slice_in_dim(full, chip * (3 * d // N_CHIPS), 3 * d // N_CHIPS, axis=2)
        elif name == "hgrn_gnorm":
            full = _unpack_small(total, name, (2, d))
            grads[name] = lax.dynamic_slice_in_dim(full, chip * (d // N_CHIPS), d // N_CHIPS, axis=1)
        else:
            grads[name] = _unpack_small(total, name, p[name].shape)
        delta[name], new_m[name], new_v[name] = adamw(p[name], grads[name], m[name], v[name], f"adamw_{name}")
    return (loss, grad_x, *[grads[n] for n in WEIGHTS], *[delta[n] for n in WEIGHTS],
            *[new_m[n] for n in WEIGHTS], *[new_v[n] for n in WEIGHTS])
```

```python
import functools
import math

import jax
import jax.numpy as jnp
from jax import lax
from jax.experimental import pallas as pl
from jax.experimental.pallas import tpu as pltpu

F32 = jnp.float32
BF16 = jnp.bfloat16
HI = lax.Precision.HIGHEST

D_MODEL = 1024
HEADS = 8
HEAD_DIM = 128
CHUNK = 64
SUB = 16
CONV_K = 4
HALO = 16
DEPTH = 4
EPS = 1e-6
MLP_HIDDEN = 4 * D_MODEL
GDN_MAIN = 4 * D_MODEL
GDN_IN = GDN_MAIN + 2 * HEADS
GDN_IN_PAD = GDN_MAIN + 128
NEG = -1e30

ADAM_LR = 0.001
ADAM_B1 = 0.9
ADAM_B2 = 0.999
ADAM_EPS = 1e-08
ADAM_WD = 0.01
ADAM_STEP = 10

VMEM_LIMIT = 48 * 1024 * 1024

MESH = pl.DeviceIdType.MESH


def _cparams(sem=None, **kw):
    if sem is not None:
        kw["dimension_semantics"] = sem
    return pltpu.CompilerParams(vmem_limit_bytes=VMEM_LIMIT, **kw)


def _iota(shape, dim):
    return lax.broadcasted_iota(jnp.int32, shape, dim)


_DIMS = {"nn": (((1,), (0,)), ((), ())), "nt": (((1,), (1,)), ((), ())), "tn": (((0,), (0,)), ((), ()))}


def _dot(a, b, mode):
    return lax.dot_general(a.astype(BF16), b.astype(BF16), _DIMS[mode], preferred_element_type=F32)


@functools.partial(jax.custom_vjp, nondiff_argnums=(2,))
def _mmx(a, b, mode):
    return _dot(a, b, mode)


def _mmx_fwd(a, b, mode):
    return _dot(a, b, mode), (a, b)


def _mmx_bwd(mode, res, g):
    a, b = res
    if mode == "nn":
        return _dot(g, b, "nt"), _dot(a, g, "tn")
    if mode == "nt":
        return _dot(g, b, "nn"), _dot(g, a, "tn")
    return _dot(b, g, "nt"), _dot(a, g, "nn")


_mmx.defvjp(_mmx_fwd, _mmx_bwd)


def _mm(a, b):
    return _mmx(a, b, "nn")


def _mm_nt(a, b):
    return _mmx(a, b, "nt")


def _mm_tn(a, b):
    return _mmx(a, b, "tn")


@functools.partial(jax.custom_vjp, nondiff_argnums=(1,))
def _roll_rows(x, d):
    return pltpu.roll(x, d, 0)


def _roll_rows_fwd(x, d):
    return pltpu.roll(x, d, 0), None


def _roll_rows_bwd(d, _, g):
    return (pltpu.roll(g, g.shape[0] - d, 0),)


_roll_rows.defvjp(_roll_rows_fwd, _roll_rows_bwd)


def _sig(x):
    return 1.0 / (1.0 + jnp.exp(-x))


@jax.custom_vjp
def _sigmoid(x):
    return _sig(x)


def _sigmoid_fwd(x):
    s = _sig(x)
    return s, s


def _sigmoid_bwd(s, g):
    return (g * s * (1.0 - s),)


_sigmoid.defvjp(_sigmoid_fwd, _sigmoid_bwd)


@jax.custom_vjp
def _silu(x):
    return x * _sig(x)


def _silu_fwd(x):
    s = _sig(x)
    return x * s, (x, s)


def _silu_bwd(res, g):
    x, s = res
    return (g * s * (1.0 + x * (1.0 - s)),)


_silu.defvjp(_silu_fwd, _silu_bwd)


@jax.custom_vjp
def _softplus(x):
    return jnp.maximum(x, 0.0) + jnp.log(1.0 + jnp.exp(-jnp.abs(x)))


def _softplus_fwd(x):
    return _softplus(x), x


def _softplus_bwd(x, g):
    return (g * _sig(x),)


_softplus.defvjp(_softplus_fwd, _softplus_bwd)


@jax.custom_vjp
def _log_sigmoid(x):
    return jnp.minimum(x, 0.0) - jnp.log(1.0 + jnp.exp(-jnp.abs(x)))


def _log_sigmoid_fwd(x):
    return _log_sigmoid(x), x


def _log_sigmoid_bwd(x, g):
    return (g * _sig(-x),)


_log_sigmoid.defvjp(_log_sigmoid_fwd, _log_sigmoid_bwd)


@jax.custom_vjp
def _logaddexp(a, b):
    return jnp.maximum(a, b) + jnp.log(1.0 + jnp.exp(-jnp.abs(a - b)))


def _logaddexp_fwd(a, b):
    return _logaddexp(a, b), (a, b)


def _unbroadcast(g, like):
    for ax in range(g.ndim):
        if like.shape[ax] == 1 and g.shape[ax] != 1:
            g = jnp.sum(g, axis=ax, keepdims=True)
    return g


def _logaddexp_bwd(res, g):
    a, b = res
    s = _sig(a - b)
    return _unbroadcast(g * s, a), _unbroadcast(g * (1.0 - s), b)


_logaddexp.defvjp(_logaddexp_fwd, _logaddexp_bwd)


def _row_to_col(row):
    n = row.shape[1]
    eye = _iota((n, n), 0) == _iota((n, n), 1)
    return jnp.sum(jnp.where(eye, jnp.broadcast_to(row, (n, n)), 0.0), axis=1, keepdims=True)


def _col_to_row(col):
    n = col.shape[0]
    eye = _iota((n, n), 0) == _iota((n, n), 1)
    return jnp.sum(jnp.where(eye, jnp.broadcast_to(col, (n, n)), 0.0), axis=0, keepdims=True)


def _pick_row(x, r):
    return jnp.sum(jnp.where(_iota(x.shape, 0) == r, x, 0.0), axis=0, keepdims=True)


def _pick_lane(x, l):
    return jnp.sum(jnp.where(_iota(x.shape, 1) == l, x, 0.0), axis=1, keepdims=True)


def _each(f, *lists):
    return [f(*t) for t in zip(*lists)]


def _unit_lower_inverse(Ls):
    n = Ls[0].shape[0]
    r, c = _iota((n, n), 0), _iota((n, n), 1)
    eye = jnp.where(r == c, 1.0, 0.0).astype(F32)
    Ld = _each(lambda L: jnp.where((r // SUB) == (c // SUB), L, 0.0), Ls)
    Lo = _each(lambda L, d: L - d, Ls, Ld)
    P = _each(lambda d: eye - d, Ld)
    Lp = Ld
    for _ in range(int(math.log2(SUB)) - 1):
        Lp = _each(lambda x: _mm(x, x), Lp)
        P = _each(lambda p, x: p + _mm(p, x), P, Lp)
    N = _each(_mm, P, Lo)
    N2 = _each(lambda x: _mm(x, x), N)
    X = _each(lambda x, x2: (eye - x) + _mm(eye - x, x2), N, N2)
    return _each(_mm, X, P)


def _shift_rows(x, halo, d):
    if d == 0:
        return x
    xr = _roll_rows(x, d)
    hr = _roll_rows(halo, d)
    hfull = jnp.concatenate([hr, jnp.zeros((x.shape[0] - HALO, x.shape[1]), F32)], axis=0)
    return jnp.where(_iota(x.shape, 0) >= d, xr, hfull)


def _causal_conv_chunk(x, halo, w):
    y = None
    for kk in range(CONV_K):
        t = _shift_rows(x, halo, CONV_K - 1 - kk) * _pick_row(w, kk)
        y = t if y is None else y + t
    return y


def _gdn_chunk(xq, xk, xv, hq, hk, hv, gate, ab, S, cwq, cwk, cwv, alog, dtb, onorm, *, heads, seqs):
    C = xq[0].shape[0]
    q = _each(lambda x, h, w: _silu(_causal_conv_chunk(x, h, w)), xq, hq, cwq)
    k = _each(lambda x, h, w: _silu(_causal_conv_chunk(x, h, w)), xk, hk, cwk)
    v = _each(lambda x, h, w: _silu(_causal_conv_chunk(x, h, w)), xv, hv, cwv)
    q = _each(lambda t: t * lax.rsqrt(jnp.sum(t * t, axis=1, keepdims=True) + EPS) * (HEAD_DIM ** -0.5), q)
    k = _each(lambda t: t * lax.rsqrt(jnp.sum(t * t, axis=1, keepdims=True) + EPS), k)
    beta_all = _each(_sigmoid, ab)
    g_all = _each(lambda t: -jnp.exp(alog) * _softplus(t + dtb), ab)
    beta = [_pick_lane(beta_all[b], HEADS + h) for b, h in zip(seqs, heads)]
    g = [_pick_lane(g_all[b], h) for b, h in zip(seqs, heads)]
    r, c = _iota((C, C), 0), _iota((C, C), 1)
    gc = _each(lambda t: jnp.sum(jnp.where(c <= r, jnp.broadcast_to(_col_to_row(t), (C, C)), 0.0), axis=1,
                                 keepdims=True), g)
    gc_row = _each(lambda t: jnp.sum(jnp.where(r <= c, jnp.broadcast_to(t, (C, C)), 0.0), axis=0, keepdims=True), g)
    decay = _each(lambda a, b: jnp.exp(jnp.where(r >= c, a - b, NEG)), gc, gc_row)
    kb = _each(lambda a, b: a * b, k, beta)
    L = _each(lambda a, b, d: jnp.where(r > c, _mm_nt(a, b) * d, 0.0), kb, k, decay)
    A = _each(lambda a, b, d: jnp.where(r >= c, _mm_nt(a, b) * d, 0.0), q, k, decay)
    T = _unit_lower_inverse(L)
    egc = _each(jnp.exp, gc)
    u = _each(lambda t, a, b: _mm(t, a * b), T, v, beta)
    w = _each(lambda t, a, e: _mm(t, a * e), T, kb, egc)
    gl = _each(lambda t: _pick_row(t, C - 1), gc)
    v_new = _each(lambda a, b, s: a - _mm(b, s), u, w, S)
    o = _each(lambda a, e, s, m, vn: _mm(a * e, s) + _mm(m, vn), q, egc, S, A, v_new)
    S_next = _each(lambda s, l, a, t, vn: s * jnp.exp(l) + _mm_tn(a * jnp.exp(l - t), vn), S, gl, k, gc, v_new)
    o = _each(lambda t, gt: t * lax.rsqrt(jnp.mean(t * t, axis=1, keepdims=True) + EPS) * onorm * _silu(gt), o, gate)
    return o, S_next


def _hgrn_lower_bound(lbl, layer):
    e = jnp.exp(lbl - jnp.max(lbl, axis=0, keepdims=True))
    sm = e / jnp.sum(e, axis=0, keepdims=True)
    r = _iota(lbl.shape, 0)
    return jnp.sum(jnp.where((r >= 1) & (r <= layer), sm, 0.0), axis=0, keepdims=True)


_LEVELS = (1, 2, 4, 8, 16, 32)


def _prefix_matrix(n):
    r, c = _iota((n, n), 0), _iota((n, n), 1)
    parts = [jnp.where(c <= r, 1.0, 0.0)]
    for s in _LEVELS:
        parts.append(jnp.where(c < (r // (2 * s)) * (2 * s) + s, 1.0, 0.0))
    return jnp.concatenate(parts, axis=0).astype(F32)


def _prefix_sums_of(x):
    n = x.shape[0]
    y = lax.dot_general(_prefix_matrix(n), x, _DIMS["nn"], precision=HI, preferred_element_type=F32)
    return tuple(y[t * n:(t + 1) * n] for t in range(len(_LEVELS) + 1))


@jax.custom_vjp
def _prefix_sums(x):
    return _prefix_sums_of(x)


def _prefix_sums_fwd(x):
    return _prefix_sums_of(x), None


def _prefix_sums_bwd(_, gs):
    g = jnp.concatenate(gs, axis=0)
    return (lax.dot_general(_prefix_matrix(gs[0].shape[0]), g, _DIMS["tn"], precision=HI, preferred_element_type=F32),)


_prefix_sums.defvjp(_prefix_sums_fwd, _prefix_sums_bwd)


def _hgrn_chunk(qp, fp, v, S, lbl, *, layer):
    C = qp[0].shape[0]
    lb = _each(lambda l: _hgrn_lower_bound(l, layer), lbl)
    lf = _each(lambda l, f: _logaddexp(jnp.log(l), jnp.log(1.0 - l) + _log_sigmoid(f)), lb, fp)
    k = _each(lambda l, f: (1.0 - l) * _sigmoid(-f), lb, fp)
    q = _each(lambda x: _silu(x) * (HEAD_DIM ** -0.5), qp)
    r, c = _iota((C, C), 0), _iota((C, C), 1)
    row = _iota(qp[0].shape, 0)
    sums = _each(_prefix_sums, lf)
    gc = [t[0] for t in sums]
    a = _each(lambda x, y: jnp.where(r == c, _mm_nt(x, y), 0.0), q, k)
    for n, s in enumerate(_LEVELS):
        ref = [t[n + 1] for t in sums]
        upper = (row % (2 * s)) >= s
        same = (r // (2 * s)) == (c // (2 * s))
        q_s = _each(lambda x, g, m: x * jnp.exp(jnp.where(upper, g - m, NEG)), q, gc, ref)
        k_s = _each(lambda x, g, m: x * jnp.exp(jnp.where(upper, NEG, m - g)), k, gc, ref)
        a = _each(lambda t, x, y: t + jnp.where(same, _mm_nt(x, y), 0.0), a, q_s, k_s)
    o = _each(lambda t, x, g, vv, st: _mm(t, vv) + _mm(x * jnp.exp(g), st), a, q, gc, v, S)
    gl = _each(lambda g: _pick_row(g, C - 1), gc)
    S_next = _each(lambda st, l, x, g, vv: st * _row_to_col(jnp.exp(l)) + _mm_tn(x * jnp.exp(l - g), vv),
                   S, gl, k, gc, v)
    return o, S_next


N_CHIPS = 4
N_DEV = 8
_ANY = pl.BlockSpec(memory_space=pl.ANY)


def _mesh_pos():
    return lax.axis_index("x"), lax.axis_index("y"), lax.axis_index("c")


def _other_chips(x, y):
    ps = [(1 - x, y), (x, 1 - y), (1 - x, 1 - y)]
    return [(p, 2 * p[0] + p[1]) for p in ps]


def _remote(src, dst, send_sem, recv_sem, dev):
    return pltpu.make_async_remote_copy(src_ref=src, dst_ref=dst, send_sem=send_sem, recv_sem=recv_sem,
                                        device_id=dev, device_id_type=MESH)


class Carry:
    def __init__(self, ins, out_shapes, sems, start, finish):
        self.ins, self.out_shapes, self.sems, self.start, self.finish = list(ins), list(out_shapes), list(sems), start, finish


def _pcall(body, *, name, grid, in_specs, out_specs, out_shape, scratch_shapes=(), dims, args, carries=()):
    in_specs, out_specs, out_shape = list(in_specs), list(out_specs), list(out_shape)
    scratch_shapes, args = list(scratch_shapes), list(args)
    n_in, n_out, n_scr = len(in_specs), len(out_shape), len(scratch_shapes)
    carries = [c for c in carries if c is not None]
    if not carries:
        res = pl.pallas_call(body, name=name, grid=grid, in_specs=in_specs, out_specs=out_specs, out_shape=out_shape,
                             scratch_shapes=scratch_shapes, compiler_params=_cparams(dims))(*args)
        return list(res), []
    ci = [len(c.ins) for c in carries]
    co = [len(c.out_shapes) for c in carries]
    cs = [len(c.sems) for c in carries]

    def split(seq, sizes):
        out, k = [], 0
        for s in sizes:
            out.append(seq[k:k + s])
            k += s
        return out

    def carried(*refs):
        ins, cins, outs, couts, scr, sems = split(refs, [n_in, sum(ci), n_out, sum(co), n_scr, sum(cs)])
        cins, couts, sems = split(cins, ci), split(couts, co), split(sems, cs)
        ids = [pl.program_id(a) for a in range(len(grid))]
        first, last = ids[0] == 0, ids[0] == grid[0] - 1
        for a in range(1, len(grid)):
            first, last = first & (ids[a] == 0), last & (ids[a] == grid[a] - 1)

        @pl.when(first)
        def _():
            for c, i, o, s in zip(carries, cins, couts, sems):
                c.start(i, o, s)

        body(*ins, *outs, *scr)

        @pl.when(last)
        def _():
            for c, i, o, s in zip(carries, cins, couts, sems):
                c.finish(i, o, s)

    res = pl.pallas_call(
        carried, name=name, grid=grid,
        in_specs=in_specs + [_ANY] * sum(ci), out_specs=out_specs + [_ANY] * sum(co),
        out_shape=out_shape + [s for c in carries for s in c.out_shapes],
        scratch_shapes=scratch_shapes + [s for c in carries for s in c.sems],
        compiler_params=_cparams(("arbitrary",) * len(grid)),
    )(*args, *[a for c in carries for a in c.ins])
    return list(res[:n_out]), split(list(res[n_out:]), co)


def run_carries(carries, name):
    ci = [len(c.ins) for c in carries]
    co = [len(c.out_shapes) for c in carries]
    cs = [len(c.sems) for c in carries]

    def split(seq, sizes):
        out, k = [], 0
        for s in sizes:
            out.append(seq[k:k + s])
            k += s
        return out

    def body(*refs):
        cins, couts, sems = split(refs, [sum(ci), sum(co), sum(cs)])
        cins, couts, sems = split(cins, ci), split(couts, co), split(sems, cs)
        for c, i, o, s in zip(carries, cins, couts, sems):
            c.start(i, o, s)
        for c, i, o, s in zip(carries, cins, couts, sems):
            c.finish(i, o, s)

    res = pl.pallas_call(
        body, name=name, in_specs=[_ANY] * sum(ci), out_specs=[_ANY] * sum(co),
        out_shape=[s for c in carries for s in c.out_shapes], scratch_shapes=[s for c in carries for s in c.sems],
    )(*[a for c in carries for a in c.ins])
    return split(list(res), co)


def gather_carry(arrs):
    n = len(arrs)
    split = [a.ndim == 2 and a.shape[0] % 32 == 0 for a in arrs]

    def plan(ins, outs, sems):
        send_sems, recv_sems, pass_send, pass_recv, local_sems = sems
        x, y, c = _mesh_pos()
        me = 2 * x + y
        peers = _other_chips(x, y)
        locs = [pltpu.make_async_copy(ins[a], outs[a].at[me], local_sems.at[a]) for a in range(n)]
        sends, recvs, passes, pass_recvs = [], [], [], []
        for a in range(n):
            half = arrs[a].shape[0] // 2
            mine, other = pl.ds(c * half, half), pl.ds((1 - c) * half, half)
            for j, ((px, py), t) in enumerate(peers):
                sem = (send_sems.at[a, j], recv_sems.at[a, j])
                if split[a]:
                    sends.append(_remote(ins[a].at[mine], outs[a].at[me, mine], *sem, (px, py, c)))
                    recvs.append(_remote(ins[a].at[mine], outs[a].at[t, mine], *sem, (px, py, c)))
                    psem = (pass_send.at[a, j], pass_recv.at[a, j])
                    passes.append(_remote(outs[a].at[t, mine], outs[a].at[t, mine], *psem, (x, y, 1 - c)))
                    pass_recvs.append(_remote(outs[a].at[t, other], outs[a].at[t, other], *psem, (x, y, 1 - c)))
                else:
                    sends.append(_remote(ins[a], outs[a].at[me], *sem, (px, py, c)))
                    recvs.append(_remote(ins[a], outs[a].at[t], *sem, (px, py, c)))
                    passes.append(None)
                    pass_recvs.append(None)
        return locs, sends, recvs, passes, pass_recvs

    def start(ins, outs, sems):
        locs, sends, _, _, _ = plan(ins, outs, sems)
        for cp in locs + sends:
            cp.start()

    def finish(ins, outs, sems):
        locs, sends, recvs, passes, pass_recvs = plan(ins, outs, sems)
        for r, p in zip(recvs, passes):
            r.wait_recv()
            if p is not None:
                p.start()
        for p in pass_recvs:
            if p is not None:
                p.wait_recv()
        for r in sends + [p for p in passes if p is not None]:
            r.wait_send()
        for cp in locs:
            cp.wait()

    return Carry(arrs, [jax.ShapeDtypeStruct((N_CHIPS,) + a.shape, a.dtype) for a in arrs],
                 [pltpu.SemaphoreType.DMA((n, 3)), pltpu.SemaphoreType.DMA((n, 3)), pltpu.SemaphoreType.DMA((n, 3)),
                  pltpu.SemaphoreType.DMA((n, 3)), pltpu.SemaphoreType.DMA((n,))], start, finish)


def sibling_half_carry(gs):
    n = len(gs)

    def copies(ins, outs, sems):
        send_sems, recv_sems = sems
        x, y, c = _mesh_pos()
        out = []
        for a in range(n):
            half = gs[a].shape[1] // 2
            out.append(_remote(ins[a].at[:, pl.ds((1 - c) * half, half), :], outs[a], send_sems.at[a], recv_sems.at[a],
                               (x, y, 1 - c)))
        return out

    def start(ins, outs, sems):
        for r in copies(ins, outs, sems):
            r.start()

    def finish(ins, outs, sems):
        cps = copies(ins, outs, sems)
        for r in cps:
            r.wait_recv()
        for r in cps:
            r.wait_send()

    return Carry(gs, [jax.ShapeDtypeStruct((g.shape[0], g.shape[1] // 2, g.shape[2]), g.dtype) for g in gs],
                 [pltpu.SemaphoreType.DMA((n,)), pltpu.SemaphoreType.DMA((n,))], start, finish)


def chips_carry(ps):
    n = len(ps)

    def copies(ins, outs, sems):
        send_sems, recv_sems, local_sems = sems
        x, y, c = _mesh_pos()
        me = 2 * x + y
        peers = _other_chips(x, y)
        locs = [pltpu.make_async_copy(ins[a].at[me], outs[a].at[me], local_sems.at[a]) for a in range(n)]
        sends = [_remote(ins[a].at[t], outs[a].at[me], send_sems.at[a, j], recv_sems.at[a, j], (px, py, c))
                 for a in range(n) for j, ((px, py), t) in enumerate(peers)]
        recvs = [_remote(ins[a].at[t], outs[a].at[t], send_sems.at[a, j], recv_sems.at[a, j], (px, py, c))
                 for a in range(n) for j, ((px, py), t) in enumerate(peers)]
        return locs, sends, recvs

    def start(ins, outs, sems):
        locs, sends, _ = copies(ins, outs, sems)
        for cp in locs + sends:
            cp.start()

    def finish(ins, outs, sems):
        locs, sends, recvs = copies(ins, outs, sems)
        for r in recvs:
            r.wait_recv()
        for r in sends:
            r.wait_send()
        for cp in locs:
            cp.wait()

    return Carry(ps, [jax.ShapeDtypeStruct(p.shape, p.dtype) for p in ps],
                 [pltpu.SemaphoreType.DMA((n, 3)), pltpu.SemaphoreType.DMA((n, 3)), pltpu.SemaphoreType.DMA((n,))],
                 start, finish)


def sibling_copy_carry(arrs):
    n = len(arrs)

    def copies(ins, outs, sems):
        send_sems, recv_sems = sems
        x, y, c = _mesh_pos()
        return [_remote(ins[a], outs[a], send_sems.at[a], recv_sems.at[a], (x, y, 1 - c)) for a in range(n)]

    def start(ins, outs, sems):
        for r in copies(ins, outs, sems):
            r.start()

    def finish(ins, outs, sems):
        cps = copies(ins, outs, sems)
        for r in cps:
            r.wait_recv()
        for r in cps:
            r.wait_send()

    return Carry(arrs, [jax.ShapeDtypeStruct(a.shape, a.dtype) for a in arrs],
                 [pltpu.SemaphoreType.DMA((n,)), pltpu.SemaphoreType.DMA((n,))], start, finish)


def gather_all_carry(small):
    flips = [(fx, fy, fc) for fx in (0, 1) for fy in (0, 1) for fc in (0, 1)][1:]

    def copies(ins, outs, sems):
        send_sems, recv_sems, local_sem = sems
        in_ref, out_ref = ins[0], outs[0]
        x, y, c = _mesh_pos()
        me = 4 * x + 2 * y + c
        peers = [((1 - x) if fx else x, (1 - y) if fy else y, (1 - c) if fc else c) for fx, fy, fc in flips]
        loc = pltpu.make_async_copy(in_ref, out_ref.at[me], local_sem.at[0])
        sends = [_remote(in_ref, out_ref.at[me], send_sems.at[k], recv_sems.at[k], p) for k, p in enumerate(peers)]
        recvs = [_remote(in_ref, out_ref.at[4 * p[0] + 2 * p[1] + p[2]], send_sems.at[k], recv_sems.at[k], p)
                 for k, p in enumerate(peers)]
        return loc, sends, recvs

    def start(ins, outs, sems):
        loc, sends, _ = copies(ins, outs, sems)
        loc.start()
        for r in sends:
            r.start()

    def finish(ins, outs, sems):
        loc, sends, recvs = copies(ins, outs, sems)
        for r in recvs:
            r.wait_recv()
        for r in sends:
            r.wait_send()
        loc.wait()

    return Carry([small], [jax.ShapeDtypeStruct((N_DEV,) + small.shape, small.dtype)],
                 [pltpu.SemaphoreType.DMA((N_DEV - 1,)), pltpu.SemaphoreType.DMA((N_DEV - 1,)),
                  pltpu.SemaphoreType.DMA((1,))], start, finish)


def _chunk_spec(bsz, nc, col, rev=False, rows=CHUNK, width=D_MODEL):
    return pl.BlockSpec((bsz, rows, width), lambda n: (0, (nc - 1 - n) if rev else n, col))


def _halo_spec(bsz, nc, col, rev):
    per = CHUNK // HALO
    return pl.BlockSpec((bsz, HALO, D_MODEL),
                        lambda n: (0, jnp.maximum(((nc - 1 - n) if rev else n) * per - 1, 0), col))


def _const_spec(shape):
    nd = len(shape)
    return pl.BlockSpec(shape, lambda n: (0,) * nd)


def _state_spec(bsz, nc, rev=False):
    return pl.BlockSpec((bsz, None, HEADS, HEAD_DIM, HEAD_DIM), lambda n: (0, (nc - 1 - n) if rev else n, 0, 0, 0))


def _lanes(hh):
    return slice(hh * HEAD_DIM, (hh + 1) * HEAD_DIM)


def _chains(bsz):
    return [(b, hh) for b in range(bsz) for hh in range(HEADS)]


def _per_chain(ref, bsz, scale=None):
    vals = [ref[b, :, _lanes(hh)].astype(F32) for b, hh in _chains(bsz)]
    return vals if scale is None else [v * scale for v in vals]


def _per_head(ref, bsz):
    return [ref[:, _lanes(hh)].astype(F32) for _, hh in _chains(bsz)]


def gdn_forward(proj, conv_w, alog, dtb, onorm, bsz, name, carries=()):
    n_tok = proj.shape[0]
    t_len = n_tok // bsz
    nc = t_len // CHUNK
    chains = _chains(bsz)

    def body(xq, xk, xv, hq, hk, hv, gate, ab, cwq, cwk, cwv, al, dt, on, o_ref, ssave_ref, s_ref):
        n = pl.program_id(0)
        keep = jnp.where(n > 0, 1.0, 0.0).astype(F32)

        @pl.when(n == 0)
        def _():
            s_ref[...] = jnp.zeros_like(s_ref)

        S = [s_ref[b, hh] for b, hh in chains]
        for (b, hh), s in zip(chains, S):
            ssave_ref[b, hh] = s
        outs, s_next = _gdn_chunk(
            _per_chain(xq, bsz), _per_chain(xk, bsz), _per_chain(xv, bsz), _per_chain(hq, bsz, keep),
            _per_chain(hk, bsz, keep), _per_chain(hv, bsz, keep), _per_chain(gate, bsz),
            [ab[b].astype(F32) for b in range(bsz)], S, _per_head(cwq, bsz), _per_head(cwk, bsz), _per_head(cwv, bsz),
            al[...], dt[...], on[...], heads=[hh for _, hh in chains], seqs=[b for b, _ in chains])
        for (b, hh), o, s in zip(chains, outs, s_next):
            o_ref[b, :, _lanes(hh)] = o.astype(o_ref.dtype)
            s_ref[b, hh] = s

    cw_spec = lambda col: pl.BlockSpec((CONV_K, D_MODEL), lambda n: (0, col))
    in_specs = [_chunk_spec(bsz, nc, 0), _chunk_spec(bsz, nc, 1), _chunk_spec(bsz, nc, 2),
                _halo_spec(bsz, nc, 0, False), _halo_spec(bsz, nc, 1, False), _halo_spec(bsz, nc, 2, False),
                _chunk_spec(bsz, nc, 3), _chunk_spec(bsz, nc, GDN_MAIN // 128, width=128),
                cw_spec(0), cw_spec(1), cw_spec(2),
                _const_spec((1, 128)), _const_spec((1, 128)), _const_spec((1, 128))]
    out_specs = [_chunk_spec(bsz, nc, 0), _state_spec(bsz, nc)]
    p3 = proj.reshape(bsz, t_len, -1)
    (o, ssave), got = _pcall(
        body, name=name, grid=(nc,), in_specs=in_specs, out_specs=out_specs,
        out_shape=(jax.ShapeDtypeStruct((bsz, t_len, D_MODEL), BF16),
                   jax.ShapeDtypeStruct((bsz, nc, HEADS, HEAD_DIM, HEAD_DIM), F32)),
        scratch_shapes=[pltpu.VMEM((bsz, HEADS, HEAD_DIM, HEAD_DIM), F32)], dims=("arbitrary",),
        args=(p3, p3, p3, p3, p3, p3, p3, p3, conv_w, conv_w, conv_w, alog, dtb, onorm), carries=carries)
    return (o.reshape(n_tok, D_MODEL), ssave), got


def gdn_backward(proj, conv_w, alog, dtb, onorm, s_saved, d_out, bsz, name, carries=()):
    n_tok = proj.shape[0]
    t_len = n_tok // bsz
    nc = t_len // CHUNK
    chains = _chains(bsz)

    def body(xq, xk, xv, hq, hk, hv, gate, ab, cwq, cwk, cwv, al, dt, on, ssave, do,
             dp_ref, dcw_ref, dal_ref, ddt_ref, don_ref, ds_ref, dhalo_ref):
        n = pl.program_id(0)
        nr = nc - 1 - n

        @pl.when(n == 0)
        def _():
            dcw_ref[...] = jnp.zeros_like(dcw_ref)
            dal_ref[...] = jnp.zeros_like(dal_ref)
            ddt_ref[...] = jnp.zeros_like(ddt_ref)
            don_ref[...] = jnp.zeros_like(don_ref)
            ds_ref[...] = jnp.zeros_like(ds_ref)
            dhalo_ref[...] = jnp.zeros_like(dhalo_ref)

        keep = jnp.where(nr > 0, 1.0, 0.0).astype(F32)
        pad = jnp.zeros((CHUNK - HALO, HEAD_DIM), F32)
        args = (_per_chain(xq, bsz), _per_chain(xk, bsz), _per_chain(xv, bsz), _per_chain(hq, bsz, keep),
                _per_chain(hk, bsz, keep), _per_chain(hv, bsz, keep), _per_chain(gate, bsz),
                [ab[b].astype(F32) for b in range(bsz)], [ssave[b, hh] for b, hh in chains],
                _per_head(cwq, bsz), _per_head(cwk, bsz), _per_head(cwv, bsz), al[...], dt[...], on[...])
        _, vjp = jax.vjp(functools.partial(_gdn_chunk, heads=[hh for _, hh in chains], seqs=[b for b, _ in chains]),
                         *args)
        (gxq, gxk, gxv, ghq, ghk, ghv, ggate, gab, gS, gcq, gck, gcv, gal, gdt, gon) = vjp(
            (_per_chain(do, bsz), [ds_ref[b, hh] for b, hh in chains]))
        d = D_MODEL
        for e, (b, hh) in enumerate(chains):
            c0 = hh * HEAD_DIM
            for part, (gx, gh, gc) in enumerate(((gxq, ghq, gcq), (gxk, ghk, gck), (gxv, ghv, gcv))):
                full = gx[e] + jnp.concatenate([pad, dhalo_ref[b, hh, part]], axis=0)
                dp_ref[b, :, part * d + c0:part * d + c0 + HEAD_DIM] = full.astype(dp_ref.dtype)
                dhalo_ref[b, hh, part] = gh[e] * keep
                dcw_ref[hh, part] += gc[e]
            dp_ref[b, :, 3 * d + c0:3 * d + c0 + HEAD_DIM] = ggate[e].astype(dp_ref.dtype)
            ds_ref[b, hh] = gS[e]
        for b in range(bsz):
            dp_ref[b, :, GDN_MAIN:GDN_IN_PAD] = gab[b].astype(dp_ref.dtype)
        dal_ref[...] += gal
        ddt_ref[...] += gdt
        don_ref[...] += gon

    cw_spec = lambda col: pl.BlockSpec((CONV_K, D_MODEL), lambda n: (0, col))
    in_specs = [_chunk_spec(bsz, nc, 0, True), _chunk_spec(bsz, nc, 1, True), _chunk_spec(bsz, nc, 2, True),
                _halo_spec(bsz, nc, 0, True), _halo_spec(bsz, nc, 1, True), _halo_spec(bsz, nc, 2, True),
                _chunk_spec(bsz, nc, 3, True), _chunk_spec(bsz, nc, GDN_MAIN // 128, True, width=128),
                cw_spec(0), cw_spec(1), cw_spec(2),
                _const_spec((1, 128)), _const_spec((1, 128)), _const_spec((1, 128)),
                _state_spec(bsz, nc, True), _chunk_spec(bsz, nc, 0, True)]
    out_specs = [_chunk_spec(bsz, nc, 0, True, width=GDN_IN_PAD),
                 _const_spec((HEADS, 3, CONV_K, HEAD_DIM)), _const_spec((1, 128)), _const_spec((1, 128)),
                 _const_spec((1, 128))]
    row = jax.ShapeDtypeStruct((1, 128), F32)
    p3 = proj.reshape(bsz, t_len, -1)
    (dp, dcw, dal, ddt, don), got = _pcall(
        body, name=name, grid=(nc,), in_specs=in_specs, out_specs=out_specs,
        out_shape=(jax.ShapeDtypeStruct((bsz, t_len, GDN_IN_PAD), BF16),
                   jax.ShapeDtypeStruct((HEADS, 3, CONV_K, HEAD_DIM), F32), row, row, row),
        scratch_shapes=[pltpu.VMEM((bsz, HEADS, HEAD_DIM, HEAD_DIM), F32),
                        pltpu.VMEM((bsz, HEADS, 3, HALO, HEAD_DIM), F32)],
        dims=("arbitrary",),
        args=(p3, p3, p3, p3, p3, p3, p3, p3, conv_w, conv_w, conv_w, alog, dtb, onorm, s_saved,
              d_out.reshape(bsz, t_len, D_MODEL)),
        carries=carries)
    return (dp.reshape(n_tok, GDN_IN_PAD), dcw, dal, ddt, don), got


def hgrn_forward(proj, lbl, layer, bsz, carries=()):
    n_tok = proj.shape[0]
    t_len = n_tok // bsz
    nc = t_len // CHUNK
    chains = _chains(bsz)

    def body(qp, fp, vi, lb, o_ref, ssave_ref, s_ref):
        @pl.when(pl.program_id(0) == 0)
        def _():
            s_ref[...] = jnp.zeros_like(s_ref)

        S = [s_ref[b, hh] for b, hh in chains]
        for (b, hh), s in zip(chains, S):
            ssave_ref[b, hh] = s
        outs, s_next = _hgrn_chunk(_per_chain(qp, bsz), _per_chain(fp, bsz), _per_chain(vi, bsz), S,
                                   _per_head(lb, bsz), layer=layer)
        for (b, hh), o, s in zip(chains, outs, s_next):
            o_ref[b, :, _lanes(hh)] = o
            s_ref[b, hh] = s

    in_specs = [_chunk_spec(bsz, nc, 0), _chunk_spec(bsz, nc, 1), _chunk_spec(bsz, nc, 2),
                _const_spec((DEPTH, D_MODEL))]
    out_specs = [_chunk_spec(bsz, nc, 0), _state_spec(bsz, nc)]
    p3 = proj.reshape(bsz, t_len, -1)
    (o, ssave), got = _pcall(
        body, name=f"hgrn_fwd{layer}", grid=(nc,), in_specs=in_specs, out_specs=out_specs,
        out_shape=(jax.ShapeDtypeStruct((bsz, t_len, D_MODEL), F32),
                   jax.ShapeDtypeStruct((bsz, nc, HEADS, HEAD_DIM, HEAD_DIM), F32)),
        scratch_shapes=[pltpu.VMEM((bsz, HEADS, HEAD_DIM, HEAD_DIM), F32)], dims=("arbitrary",),
        args=(p3, p3, p3, lbl), carries=carries)
    return (o.reshape(n_tok, D_MODEL), ssave), got


def hgrn_backward(proj, lbl, s_saved, d_o, d_gate, layer, bsz, carries=()):
    n_tok = proj.shape[0]
    t_len = n_tok // bsz
    nc = t_len // CHUNK
    chains = _chains(bsz)

    def body(qp, fp, vi, lb, ssave, do, dgt, dp_ref, dlb_ref, ds_ref):
        @pl.when(pl.program_id(0) == 0)
        def _():
            dlb_ref[...] = jnp.zeros_like(dlb_ref)
            ds_ref[...] = jnp.zeros_like(ds_ref)

        _, vjp = jax.vjp(functools.partial(_hgrn_chunk, layer=layer), _per_chain(qp, bsz), _per_chain(fp, bsz),
                         _per_chain(vi, bsz), [ssave[b, hh] for b, hh in chains], _per_head(lb, bsz))
        gq, gf, gv, gS, glb = vjp((_per_chain(do, bsz), [ds_ref[b, hh] for b, hh in chains]))
        d = D_MODEL
        for e, (b, hh) in enumerate(chains):
            c0 = hh * HEAD_DIM
            dp_ref[b, :, c0:c0 + HEAD_DIM] = gq[e].astype(dp_ref.dtype)
            dp_ref[b, :, d + c0:d + c0 + HEAD_DIM] = gf[e].astype(dp_ref.dtype)
            dp_ref[b, :, 2 * d + c0:2 * d + c0 + HEAD_DIM] = gv[e].astype(dp_ref.dtype)
            dlb_ref[hh] += glb[e]
            ds_ref[b, hh] = gS[e]
        dp_ref[:, :, 3 * d:4 * d] = dgt[...]

    in_specs = [_chunk_spec(bsz, nc, 0, True), _chunk_spec(bsz, nc, 1, True), _chunk_spec(bsz, nc, 2, True),
                _const_spec((DEPTH, D_MODEL)), _state_spec(bsz, nc, True),
                _chunk_spec(bsz, nc, 0, True), _chunk_spec(bsz, nc, 0, True)]
    out_specs = [_chunk_spec(bsz, nc, 0, True, width=4 * D_MODEL), _const_spec((HEADS, DEPTH, HEAD_DIM))]
    p3 = proj.reshape(bsz, t_len, -1)
    (dp, dlb), got = _pcall(
        body, name=f"hgrn_bwd{layer}", grid=(nc,), in_specs=in_specs, out_specs=out_specs,
        out_shape=(jax.ShapeDtypeStruct((bsz, t_len, 4 * D_MODEL), BF16),
                   jax.ShapeDtypeStruct((HEADS, DEPTH, HEAD_DIM), F32)),
        scratch_shapes=[pltpu.VMEM((bsz, HEADS, HEAD_DIM, HEAD_DIM), F32)], dims=("arbitrary",),
        args=(p3, p3, p3, lbl, s_saved, d_o.reshape(bsz, t_len, D_MODEL), d_gate.reshape(bsz, t_len, D_MODEL)),
        carries=carries)
    return (dp.reshape(n_tok, 4 * D_MODEL), dlb), got


ROW_TILE = 512
MM_VMEM_BUDGET = 36 * 1024 * 1024


def _tile(n):
    for cand in (1024, 512, 1408, 384, 256, 128):
        if n % cand == 0:
            return cand
    return n


def _rmsnorm(x, w):
    return x * lax.rsqrt(jnp.mean(x * x, axis=1, keepdims=True) + EPS) * w


def norm_matmul(h, nw, w, relu, name, carries=()):
    n_tok, d = h.shape
    slots = w.shape[0] if w.ndim == 3 else 0
    nb = w.shape[2] if slots else w.shape[1]
    n_out = max(slots, 1) * nb
    tm = min(n_tok, ROW_TILE)
    if slots:
        w_specs = [pl.BlockSpec((None, d, nb), lambda i, s=s: (s, 0, 0)) for s in range(slots)]
    else:
        w_specs = [pl.BlockSpec((d, nb), lambda i: (0, 0))]
    nw_ = len(w_specs)

    def body(*refs):
        h_ref, nw_ref, w_refs = refs[0], refs[1], refs[2:2 + nw_]
        y_ref, u_ref = refs[2 + nw_:]
        y = _rmsnorm(h_ref[...], nw_ref[...]).astype(BF16)
        y_ref[...] = y
        for s, w_ref in enumerate(w_refs):
            acc = jnp.dot(y, w_ref[...], preferred_element_type=F32)
            u_ref[:, s * nb:(s + 1) * nb] = (jnp.maximum(acc, 0.0) if relu else acc).astype(BF16)

    row = pl.BlockSpec((tm, d), lambda i: (i, 0))
    return _pcall(
        body, name=name, grid=(n_tok // tm,),
        in_specs=[row, pl.BlockSpec((1, d), lambda i: (0, 0))] + w_specs,
        out_specs=[row, pl.BlockSpec((tm, n_out), lambda i: (i, 0))],
        out_shape=(jax.ShapeDtypeStruct((n_tok, d), BF16), jax.ShapeDtypeStruct((n_tok, n_out), BF16)),
        dims=("parallel",), args=[h, nw] + [w] * nw_, carries=carries)


def _mm_tiles(m, n, k, extra_bytes):
    tn = _tile(n)
    for tm in (1024, 512, 256, 128):
        if m % tm == 0 and 2 * (2 * tm * k + 2 * k * tn + (4 + extra_bytes) * tm * tn) <= MM_VMEM_BUDGET:
            return tm, tn
    return min(m, 128), tn


def matmul(a, b, mode, name, out_dtype=F32, extra=None, epilogue=None, shards=1, carries=(), square_a=False):
    slots = b.shape[0] if (mode == "nt" and b.ndim == 3) else 0
    if mode == "nn":
        (m, k), n = a.shape, b.shape[1]
    elif mode == "nt":
        (m, k), n = a.shape, (b.shape[1] if slots else b.shape[0])
    else:
        (k, m), n = a.shape, b.shape[1]
    tm, tn = _mm_tiles(m, n // shards, k, 0 if extra is None else extra.dtype.itemsize)
    a_spec = pl.BlockSpec((k, tm), lambda i, j: (0, i)) if mode == "tn" else pl.BlockSpec((tm, k), lambda i, j: (i, 0))
    if slots:
        kb = b.shape[2]
        b_specs = [pl.BlockSpec((None, tn, kb), lambda i, j, s=s: (s, j, 0)) for s in range(slots)]
    elif mode == "nt":
        b_specs = [pl.BlockSpec((tn, k), lambda i, j: (j, 0))]
    else:
        b_specs = [pl.BlockSpec((k, tn), lambda i, j: (0, j))]
    nb = len(b_specs)
    dims = _DIMS[mode]

    def body(*refs):
        a_ref, b_refs = refs[0], refs[1:1 + nb]
        e_ref = refs[1 + nb] if extra is not None else None
        o_ref = refs[-1]
        if slots:
            acc = None
            for s in range(slots):
                part = lax.dot_general(a_ref[:, s * kb:(s + 1) * kb].astype(BF16), b_refs[s][...].astype(BF16), dims,
                                       preferred_element_type=F32)
                acc = part if acc is None else acc + part
        else:
            av = a_ref[...].astype(BF16)
            if square_a:
                av = av * av
            acc = lax.dot_general(av, b_refs[0][...].astype(BF16), dims, preferred_element_type=F32)
        if epilogue == "add":
            acc = e_ref[...] + acc
        elif epilogue == "mul2":
            acc = acc * (2.0 * e_ref[...].astype(F32))
        o_ref[...] = acc.astype(o_ref.dtype)

    in_specs = [a_spec] + b_specs
    args = [a] + [b] * nb
    if extra is not None:
        in_specs.append(pl.BlockSpec((tm, tn), lambda i, j: (i, j)))
        args.append(extra)
    if shards > 1:
        per = n // shards // tn
        out_spec = pl.BlockSpec((None, tm, tn), lambda i, j: (j // per, i, j % per))
        out_shape = jax.ShapeDtypeStruct((shards, m, n // shards), out_dtype)
    else:
        out_spec = pl.BlockSpec((tm, tn), lambda i, j: (i, j))
        out_shape = jax.ShapeDtypeStruct((m, n), out_dtype)
    (res,), cres = _pcall(
        body, name=name, grid=(m // tm, n // tn), in_specs=in_specs, out_specs=[out_spec], out_shape=[out_shape],
        dims=("parallel", "arbitrary"), args=args, carries=carries)
    return res, cres


def matmul_norm_backward(a, b, h, nw, dres, name, carries=()):
    slots = b.shape[0] if b.ndim == 3 else 0
    n_tok, k = a.shape
    d = h.shape[1]
    tm = min(n_tok, ROW_TILE // 2)
    if slots:
        kb = b.shape[2]
        b_specs = [pl.BlockSpec((None, d, kb), lambda i, s=s: (s, 0, 0)) for s in range(slots)]
    else:
        b_specs = [pl.BlockSpec((d, k), lambda i: (0, 0))]
    nb = len(b_specs)

    def body(*refs):
        a_ref, b_refs = refs[0], refs[1:1 + nb]
        h_ref, nw_ref, dr_ref, dh_ref, dhb_ref, dnw_ref = refs[1 + nb:]

        @pl.when(pl.program_id(0) == 0)
        def _():
            dnw_ref[...] = jnp.zeros_like(dnw_ref)

        if slots:
            dy = None
            for s in range(slots):
                part = lax.dot_general(a_ref[:, s * kb:(s + 1) * kb], b_refs[s][...], _DIMS["nt"],
                                       preferred_element_type=F32)
                dy = part if dy is None else dy + part
        else:
            dy = lax.dot_general(a_ref[...], b_refs[0][...], _DIMS["nt"], preferred_element_type=F32)
        _, vjp = jax.vjp(_rmsnorm, h_ref[...], nw_ref[...])
        gh, gw = vjp(dy)
        dh = dr_ref[...] + gh
        dh_ref[...] = dh
        dhb_ref[...] = dh.astype(BF16)
        dnw_ref[...] += gw

    row = pl.BlockSpec((tm, d), lambda i: (i, 0))
    vec = pl.BlockSpec((1, d), lambda i: (0, 0))
    return _pcall(
        body, name=name, grid=(n_tok // tm,),
        in_specs=[pl.BlockSpec((tm, k), lambda i: (i, 0))] + b_specs + [row, vec, row], out_specs=[row, row, vec],
        out_shape=(jax.ShapeDtypeStruct((n_tok, d), F32), jax.ShapeDtypeStruct((n_tok, d), BF16),
                   jax.ShapeDtypeStruct((1, d), F32)),
        dims=("arbitrary",), args=[a] + [b] * nb + [h, nw, dres], carries=carries)


def _hgrn_post(o, gate, gw):
    return _rmsnorm(o, gw) * _silu(gate)


def hgrn_post_forward(o, proj, gw, name):
    n_tok, d = o.shape
    tm = min(n_tok, ROW_TILE)

    def body(o_ref, g_ref, w_ref, y_ref):
        y_ref[...] = _hgrn_post(o_ref[...], g_ref[...].astype(F32), w_ref[...]).astype(BF16)

    row = pl.BlockSpec((tm, d), lambda i: (i, 0))
    return pl.pallas_call(
        body, name=name, grid=(n_tok // tm,),
        in_specs=[row, pl.BlockSpec((tm, d), lambda i: (i, 3)), pl.BlockSpec((1, d), lambda i: (0, 0))],
        out_specs=row, out_shape=jax.ShapeDtypeStruct((n_tok, d), BF16),
        compiler_params=_cparams(("parallel",)),
    )(o, proj, gw)


def hgrn_post_backward(o, proj, gw, dy, name):
    n_tok, d = o.shape
    tm = min(n_tok, ROW_TILE)

    def body(o_ref, g_ref, w_ref, dy_ref, do_ref, dg_ref, dw_ref):
        @pl.when(pl.program_id(0) == 0)
        def _():
            dw_ref[...] = jnp.zeros_like(dw_ref)

        _, vjp = jax.vjp(_hgrn_post, o_ref[...], g_ref[...].astype(F32), w_ref[...])
        go, gg, gw_ = vjp(dy_ref[...])
        do_ref[...] = go
        dg_ref[...] = gg.astype(BF16)
        dw_ref[...] += gw_

    row = pl.BlockSpec((tm, d), lambda i: (i, 0))
    vec = pl.BlockSpec((1, d), lambda i: (0, 0))
    return pl.pallas_call(
        body, name=name, grid=(n_tok // tm,),
        in_specs=[row, pl.BlockSpec((tm, d), lambda i: (i, 3)), vec, row], out_specs=[row, row, vec],
        out_shape=(jax.ShapeDtypeStruct((n_tok, d), F32), jax.ShapeDtypeStruct((n_tok, d), BF16),
                   jax.ShapeDtypeStruct((1, d), F32)),
        compiler_params=_cparams(("arbitrary",)),
    )(o, proj, gw, dy)


def loss_head(h, nw, target):
    n_tok, d = h.shape
    tm = min(n_tok, ROW_TILE)

    def body(h_ref, nw_ref, t_ref, loss_ref, dh_ref, dhb_ref, dnw_ref):
        @pl.when(pl.program_id(0) == 0)
        def _():
            dnw_ref[...] = jnp.zeros_like(dnw_ref)
            loss_ref[...] = jnp.zeros_like(loss_ref)

        out, vjp = jax.vjp(_rmsnorm, h_ref[...], nw_ref[...])
        err = out - t_ref[...]
        part = 0.5 * jnp.sum(jnp.sum(err * err, axis=1, keepdims=True), axis=0, keepdims=True) / d
        loss_ref[...] += jnp.broadcast_to(part, loss_ref.shape)
        gh, gw = vjp(err / d)
        dh_ref[...] = gh
        dhb_ref[...] = gh.astype(BF16)
        dnw_ref[...] += gw

    row = pl.BlockSpec((tm, d), lambda i: (i, 0))
    vec = pl.BlockSpec((1, d), lambda i: (0, 0))
    return pl.pallas_call(
        body, name="loss_head", grid=(n_tok // tm,), in_specs=[row, vec, row],
        out_specs=[pl.BlockSpec((1, 128), lambda i: (0, 0)), row, row, vec],
        out_shape=(jax.ShapeDtypeStruct((1, 128), F32), jax.ShapeDtypeStruct((n_tok, d), F32),
                   jax.ShapeDtypeStruct((n_tok, d), BF16), jax.ShapeDtypeStruct((1, d), F32)),
        compiler_params=_cparams(("arbitrary",)),
    )(h, nw, target)


def _rows2d(shape):
    if len(shape) == 1:
        return (1, shape[0])
    return (math.prod(shape[:-1]), shape[-1])


def adamw(w, g, m, v, name):
    shape = w.shape
    r, c = _rows2d(shape)
    tr = r if r <= 256 else 256
    c1 = 1.0 / (1.0 - ADAM_B1 ** ADAM_STEP)
    c2 = 1.0 / (1.0 - ADAM_B2 ** ADAM_STEP)

    def body(w_ref, g_ref, m_ref, v_ref, d_ref, nm_ref, nv_ref):
        gg = g_ref[...]
        nm = ADAM_B1 * m_ref[...] + (1.0 - ADAM_B1) * gg
        nv = ADAM_B2 * v_ref[...] + (1.0 - ADAM_B2) * (gg * gg)
        d_ref[...] = -ADAM_LR * ((nm * c1) / (jnp.sqrt(nv * c2) + ADAM_EPS) + ADAM_WD * w_ref[...])
        nm_ref[...] = nm
        nv_ref[...] = nv

    spec = pl.BlockSpec((tr, c), lambda i: (i, 0))
    sds = jax.ShapeDtypeStruct((r, c), F32)
    outs = pl.pallas_call(
        body, name=name, grid=(r // tr,), in_specs=[spec] * 4, out_specs=[spec] * 3, out_shape=(sds,) * 3,
        compiler_params=_cparams(("parallel",)),
    )(w.reshape(r, c), g.reshape(r, c), m.reshape(r, c), v.reshape(r, c))
    return tuple(o.reshape(shape) for o in outs)


def add_slots(parts, name):
    s, r, c = parts.shape
    tr = r if r <= 256 else 256

    def body(p_ref, o_ref):
        acc = p_ref[0]
        for t in range(1, s):
            acc = acc + p_ref[t]
        o_ref[...] = acc

    return pl.pallas_call(
        body, name=name, grid=(r // tr,), in_specs=[pl.BlockSpec((s, tr, c), lambda i: (0, i, 0))],
        out_specs=pl.BlockSpec((tr, c), lambda i: (i, 0)), out_shape=jax.ShapeDtypeStruct((r, c), F32),
        compiler_params=_cparams(("parallel",)),
    )(parts)


BIG = ("gdn_w_in", "gdn_w_out", "hgrn_w_in", "hgrn_w_out", "mlp_w_up", "mlp_w_down")
WEIGHTS = ("gdn_w_in", "gdn_conv", "gdn_a_log", "gdn_dt_bias", "gdn_onorm", "gdn_w_out", "hgrn_w_in", "hgrn_lb_logits",
           "hgrn_gnorm", "hgrn_w_out", "norm_mix", "norm_mlp", "mlp_w_up", "mlp_w_down", "norm_final")


def _pad_lanes(v, n):
    return jnp.pad(v, [(0, 0)] * (v.ndim - 1) + [(0, n - v.shape[-1])])


_SMALL_LAYOUT = {
    "norm_mix": (0, 4, D_MODEL), "norm_mlp": (8, 4, D_MODEL), "norm_final": (16, 1, D_MODEL),
    "hgrn_lb_logits": (24, 4, D_MODEL), "gdn_onorm": (32, 2, 128), "gdn_a_log": (40, 2, HEADS),
    "gdn_dt_bias": (48, 2, HEADS), "loss": (56, 1, 128), "gdn_conv": (64, 24, D_MODEL), "hgrn_gnorm": (88, 2, D_MODEL),
}


def _pack_small(small, loss_row):
    rows = []
    for name, (first, nrow, lanes) in _SMALL_LAYOUT.items():
        v = loss_row if name == "loss" else small[name]
        v = _pad_lanes(v.reshape(nrow, -1), D_MODEL)
        rows.append(jnp.pad(v, ((0, -nrow % 8), (0, 0))))
    return jnp.concatenate(rows, axis=0)


def _unpack_small(packed, name, shape):
    first, nrow, lanes = _SMALL_LAYOUT[name]
    return packed[first:first + nrow, :lanes].reshape(shape)


def add_core_halves(g, theirs, core, name):
    s, r, c = g.shape
    r2 = r // 2
    tr = min(r2, 256)
    nb = r2 // tr

    def body(core_ref, g_ref, t_ref, o_ref):
        o_ref[...] = (g_ref[...] + t_ref[...]).astype(o_ref.dtype)

    grid_spec = pltpu.PrefetchScalarGridSpec(
        num_scalar_prefetch=1, grid=(s, nb),
        in_specs=[pl.BlockSpec((None, tr, c), lambda t, i, cr: (t, cr[0] * nb + i, 0)),
                  pl.BlockSpec((None, tr, c), lambda t, i, cr: (t, i, 0))],
        out_specs=pl.BlockSpec((None, tr, c), lambda t, i, cr: (t, i, 0)))
    return pl.pallas_call(body, name=name, grid_spec=grid_spec, out_shape=jax.ShapeDtypeStruct((s, r2, c), BF16),
                          compiler_params=_cparams(("parallel", "parallel")))(core, g, theirs)


def add_chip_slots(slots, name):
    n_l = len(slots)
    s, r2, c = slots[0].shape
    tr = min(r2, 256)
    nb = r2 // tr

    def body(*refs):
        ins, o_ref = refs[:n_l], refs[n_l]
        for k in range(n_l):
            @pl.when(pl.program_id(0) == k)
            def _(k=k):
                acc = ins[k][0].astype(F32)
                for t in range(1, s):
                    acc = acc + ins[k][t].astype(F32)
                o_ref[...] = acc

    in_specs = [pl.BlockSpec((s, tr, c), lambda l, i, k=k: (0, jnp.where(l == k, i, 0), 0)) for k in range(n_l)]
    return pl.pallas_call(
        body, name=name, grid=(n_l, nb), in_specs=in_specs, out_specs=pl.BlockSpec((None, tr, c), lambda l, i: (l, i, 0)),
        out_shape=jax.ShapeDtypeStruct((n_l, r2, c), F32), compiler_params=_cparams(("arbitrary", "arbitrary")),
    )(*slots)


def adamw_halves(w, m, v, mine, theirs, name):
    n_l, r, c = w.shape
    r2 = r // 2
    tr = min(r2, 256)
    nb = r2 // tr
    c1 = 1.0 / (1.0 - ADAM_B1 ** ADAM_STEP)
    c2 = 1.0 / (1.0 - ADAM_B2 ** ADAM_STEP)

    def body(w_ref, m_ref, v_ref, mine_ref, theirs_ref, g_ref, d_ref, nm_ref, nv_ref):
        my_half = (pl.program_id(1) // nb) == lax.axis_index("c")
        gg = jnp.where(my_half, mine_ref[...], theirs_ref[...])
        nm = ADAM_B1 * m_ref[...] + (1.0 - ADAM_B1) * gg
        nv = ADAM_B2 * v_ref[...] + (1.0 - ADAM_B2) * (gg * gg)
        g_ref[...] = gg
        d_ref[...] = -ADAM_LR * ((nm * c1) / (jnp.sqrt(nv * c2) + ADAM_EPS) + ADAM_WD * w_ref[...])
        nm_ref[...] = nm
        nv_ref[...] = nv

    full = pl.BlockSpec((None, tr, c), lambda l, i: (l, i, 0))
    half = pl.BlockSpec((None, tr, c), lambda l, i: (l, i % nb, 0))
    sds = jax.ShapeDtypeStruct((n_l, r, c), F32)
    return pl.pallas_call(
        body, name=name, grid=(n_l, r // tr), in_specs=[full, full, full, half, half], out_specs=[full] * 4,
        out_shape=(sds,) * 4, compiler_params=_cparams(("parallel", "parallel")),
    )(w, m, v, mine, theirs)


def _layer_weight(kind, i):
    if kind == "up":
        return "mlp_w_up", i
    if kind == "down":
        return "mlp_w_down", i
    return ("gdn_w_" if i % 2 == 0 else "hgrn_w_") + kind, i // 2


def kernel(x, gdn_w_in, gdn_conv, gdn_a_log, gdn_dt_bias, gdn_onorm, gdn_w_out, hgrn_w_in, hgrn_lb_logits, hgrn_gnorm, hgrn_w_out, norm_mix, norm_mlp, mlp_w_up, mlp_w_down, norm_final, loss_target, m_gdn_w_in, m_gdn_conv, m_gdn_a_log, m_gdn_dt_bias, m_gdn_onorm, m_gdn_w_out, m_hgrn_w_in, m_hgrn_lb_logits, m_hgrn_gnorm, m_hgrn_w_out, m_norm_mix, m_norm_mlp, m_mlp_w_up, m_mlp_w_down, m_norm_final, v_gdn_w_in, v_gdn_conv, v_gdn_a_log, v_gdn_dt_bias, v_gdn_onorm, v_gdn_w_out, v_hgrn_w_in, v_hgrn_lb_logits, v_hgrn_gnorm, v_hgrn_w_out, v_norm_mix, v_norm_mlp, v_mlp_w_up, v_mlp_w_down, v_norm_final):
    p = dict(gdn_w_in=gdn_w_in, gdn_conv=gdn_conv, gdn_a_log=gdn_a_log, gdn_dt_bias=gdn_dt_bias, gdn_onorm=gdn_onorm,
             gdn_w_out=gdn_w_out, hgrn_w_in=hgrn_w_in, hgrn_lb_logits=hgrn_lb_logits, hgrn_gnorm=hgrn_gnorm,
             hgrn_w_out=hgrn_w_out, norm_mix=norm_mix, norm_mlp=norm_mlp, mlp_w_up=mlp_w_up, mlp_w_down=mlp_w_down,
             norm_final=norm_final)
    m = dict(gdn_w_in=m_gdn_w_in, gdn_conv=m_gdn_conv, gdn_a_log=m_gdn_a_log, gdn_dt_bias=m_gdn_dt_bias,
             gdn_onorm=m_gdn_onorm, gdn_w_out=m_gdn_w_out, hgrn_w_in=m_hgrn_w_in, hgrn_lb_logits=m_hgrn_lb_logits,
             hgrn_gnorm=m_hgrn_gnorm, hgrn_w_out=m_hgrn_w_out, norm_mix=m_norm_mix, norm_mlp=m_norm_mlp,
             mlp_w_up=m_mlp_w_up, mlp_w_down=m_mlp_w_down, norm_final=m_norm_final)
    v = dict(gdn_w_in=v_gdn_w_in, gdn_conv=v_gdn_conv, gdn_a_log=v_gdn_a_log, gdn_dt_bias=v_gdn_dt_bias,
             gdn_onorm=v_gdn_onorm, gdn_w_out=v_gdn_w_out, hgrn_w_in=v_hgrn_w_in, hgrn_lb_logits=v_hgrn_lb_logits,
             hgrn_gnorm=v_hgrn_gnorm, hgrn_w_out=v_hgrn_w_out, norm_mix=v_norm_mix, norm_mlp=v_norm_mlp,
             mlp_w_up=v_mlp_w_up, mlp_w_down=v_mlp_w_down, norm_final=v_norm_final)
    xi, yi, ci = _mesh_pos()
    chip = 2 * xi + yi
    core = jnp.reshape(ci, (1,)).astype(jnp.int32)
    d = D_MODEL
    bsz, t_len, _ = x.shape
    n_tok = bsz * t_len

    def shard(kind, i):
        name, idx = _layer_weight(kind, i)
        return p[name][idx].astype(BF16)

    def w_in_of(i, slots):
        if i % 2 == 0:
            return _pad_lanes(jnp.transpose(slots, (1, 0, 2)).reshape(d, GDN_IN), GDN_IN_PAD)
        return slots

    (first,) = run_carries([gather_carry([shard("in", 0), p["gdn_conv"], p["hgrn_gnorm"]])], "gather_first")
    conv = jnp.transpose(first[1], (1, 2, 0, 3)).reshape(DEPTH // 2, CONV_K, 3 * d)
    gnorm = jnp.transpose(first[2], (1, 0, 2)).reshape(DEPTH // 2, d)
    lbl = p["hgrn_lb_logits"]
    h = x.reshape(n_tok, d)
    next_in, next_out = first[0], None
    saved = []
    for i in range(DEPTH):
        j = i // 2
        w_in = w_in_of(i, next_in)
        nmix = p["norm_mix"][i][None, :]
        (y, proj), got = norm_matmul(h, nmix, w_in, False, f"in_proj{i}",
                                     [gather_carry([shard("out", 0)])] if i == 0 else [])
        if i == 0:
            next_out = got[0][0]
        ride = [gather_carry([shard("up", i), shard("down", i)])]
        if i % 2 == 0:
            al = _pad_lanes(p["gdn_a_log"][j][None, :], 128)
            dtb = _pad_lanes(p["gdn_dt_bias"][j][None, :], 128)
            on = p["gdn_onorm"][j][None, :]
            (og, ssave), got = gdn_forward(proj, conv[j], al, dtb, on, bsz, f"gdn_fwd{i}", ride)
            mix = (proj, ssave, al, dtb, on)
        else:
            (o, ssave), got = hgrn_forward(proj, lbl, i, bsz, ride)
            gn = gnorm[j][None, :]
            og = hgrn_post_forward(o, proj, gn, f"hgrn_post{i}")
            mix = (proj, ssave, o, gn)
        w_up, w_down = got[0][0], got[0][1].reshape(MLP_HIDDEN, d)
        w_out = next_out.reshape(d, d)
        h1, _ = matmul(og, w_out, "nn", f"out_proj{i}", extra=h, epilogue="add")
        nmlp = p["norm_mlp"][i][None, :]
        (z, r), got = norm_matmul(h1, nmlp, w_up, True, f"mlp_up{i}",
                                     [gather_carry([shard("in", i + 1)])] if i + 1 < DEPTH else [])
        if i + 1 < DEPTH:
            next_in = got[0][0]
        h2, got = matmul(r, w_down, "nn", f"mlp_down{i}", extra=h1, epilogue="add", square_a=True,
                         carries=[gather_carry([shard("out", i + 1)])] if i + 1 < DEPTH else [])
        if i + 1 < DEPTH:
            next_out = got[0][0]
        saved.append((h, nmix, y, mix, og, w_in, w_out, h1, nmlp, z, r, w_up, w_down))
        h = h2
    loss_row, dh, dhb, d_nf = loss_head(h, p["norm_final"][None, :], loss_target.reshape(n_tok, d))

    G = {k: [None] * DEPTH for k in ("in", "out", "up", "down")}
    P = {k: [None] * DEPTH for k in ("in", "out", "up", "down")}
    slots = {k: [None] * DEPTH for k in ("in", "out", "up", "down")}
    d_nmix, d_nmlp = [None] * DEPTH, [None] * DEPTH
    d_conv, d_alog, d_dtb, d_onorm, d_gnorm = [None] * 2, [None] * 2, [None] * 2, [None] * 2, [None] * 2
    d_lbl = jnp.zeros((DEPTH, d), F32)
    for i in reversed(range(DEPTH)):
        j = i // 2
        h_in, nmix, y, mix, og, w_in, w_out, h1, nmlp, z, r, w_up, w_down = saved[i]
        ride = [sibling_half_carry([G["in"][i + 1]])] if i + 1 < DEPTH else []
        du, got = matmul(dhb, w_down, "nt", f"d_mlp_act{i}", out_dtype=BF16, extra=r, epilogue="mul2", carries=ride)
        if i + 1 < DEPTH:
            P["in"][i + 1] = add_core_halves(G["in"][i + 1], got[0][0], core, f"add_cores_in{i + 1}")
        G["down"][i] = matmul(r, dhb, "tn", f"dw_down{i}", square_a=True)[0].reshape(N_CHIPS, -1, d)
        G["up"][i], _ = matmul(z, du, "tn", f"dw_up{i}", shards=N_CHIPS)
        (dh1, dh1b, d_nmlp[i]), got = matmul_norm_backward(du, w_up, h1, nmlp, dh, f"d_mlp_in{i}",
                                                           [sibling_half_carry([G["up"][i], G["down"][i]])])
        P["up"][i] = add_core_halves(G["up"][i], got[0][0], core, f"add_cores_up{i}")
        P["down"][i] = add_core_halves(G["down"][i], got[0][1], core, f"add_cores_down{i}")
        dog, _ = matmul(dh1b, w_out, "nt", f"d_mix_out{i}")
        G["out"][i] = matmul(og, dh1b, "tn", f"dw_out{i}")[0].reshape(N_CHIPS, -1, d)
        to_chips = [("up", i), ("down", i)] + ([("in", i + 1), ("out", i + 1)] if i + 1 < DEPTH else [])
        ride = [sibling_half_carry([G["out"][i]]), chips_carry([P[k][l] for k, l in to_chips])]
        if i % 2 == 0:
            proj, ssave, al, dtb, on = mix
            (dproj, dcw, dal, ddt, don), got = gdn_backward(proj, conv[j], al, dtb, on, ssave, dog, bsz, f"gdn_bwd{i}", ride)
            d_conv[j] = jnp.transpose(dcw, (2, 1, 0, 3)).reshape(CONV_K, 3 * d)
            d_alog[j], d_dtb[j], d_onorm[j] = dal[0, :HEADS], ddt[0, :HEADS], don[0]
        else:
            proj, ssave, o, gn = mix
            do_raw, dgate, dgn = hgrn_post_backward(o, proj, gn, dog, f"d_hgrn_post{i}")
            (dproj, dlb), got = hgrn_backward(proj, lbl, ssave, do_raw, dgate, i, bsz, ride)
            d_gnorm[j] = dgn[0]
            d_lbl = d_lbl + jnp.transpose(dlb, (1, 0, 2)).reshape(DEPTH, d)
        P["out"][i] = add_core_halves(G["out"][i], got[0][0], core, f"add_cores_out{i}")
        for (k, l), s in zip(to_chips, got[1]):
            slots[k][l] = s
        if i % 2 == 0:
            dw_in = matmul(y, dproj, "tn", f"dw_in{i}")[0][:, :GDN_IN]
            G["in"][i] = jnp.transpose(dw_in.reshape(d, N_CHIPS, GDN_IN // N_CHIPS), (1, 0, 2))
        else:
            G["in"][i], _ = matmul(y, dproj, "tn", f"dw_in{i}", shards=N_CHIPS)
        ride = [sibling_half_carry([G["in"][0]]), chips_carry([P["out"][0]])] if i == 0 else []
        (dh, dhb, d_nmix[i]), got = matmul_norm_backward(dproj, w_in, h_in, nmix, dh1, f"d_in_proj{i}", ride)
        if i == 0:
            P["in"][0] = add_core_halves(G["in"][0], got[0][0], core, "add_cores_in0")
            slots["out"][0] = got[1][0]
    grad_x = dh.reshape(x.shape)
    ((slots["in"][0],),) = run_carries([chips_carry([P["in"][0]])], "reduce_chips_last")

    by_weight = {}
    for kind in ("in", "out", "up", "down"):
        for i in range(DEPTH):
            by_weight.setdefault(_layer_weight(kind, i)[0], []).append(slots[kind][i])
    mine = {name: add_chip_slots(by_weight[name], f"add_chips_{name}") for name in BIG}
    small = {
        "gdn_conv": jnp.stack(d_conv), "gdn_a_log": jnp.stack(d_alog), "gdn_dt_bias": jnp.stack(d_dtb),
        "gdn_onorm": jnp.stack(d_onorm), "hgrn_lb_logits": d_lbl, "hgrn_gnorm": jnp.stack(d_gnorm),
        "norm_mix": jnp.concatenate(d_nmix, axis=0), "norm_mlp": jnp.concatenate(d_nmlp, axis=0), "norm_final": d_nf[0],
    }
    theirs, (blocks,) = run_carries([sibling_copy_carry([mine[name] for name in BIG]),
                                     gather_all_carry(_pack_small(small, loss_row))], "share_cores")
    theirs = dict(zip(BIG, theirs))

    total = add_slots(blocks, "add_small")
    loss = total[_SMALL_LAYOUT["loss"][0], 0]
    grads, delta, new_m, new_v = {}, {}, {}, {}
    for name in WEIGHTS:
        if name in BIG:
            grads[name], delta[name], new_m[name], new_v[name] = adamw_halves(
                p[name], m[name], v[name], mine[name], theirs[name], f"adamw_{name}")
            continue
        if name == "gdn_conv":
            full = _unpack_small(total, name, (2, CONV_K, 3 * d))
            grads[name] = lax.dynamic_slice_in_dim(full, chip * (3 * d // N_CHIPS), 3 * d // N_CHIPS, axis=2)
        elif name == "hgrn_gnorm":
            full = _unpack_small(total, name, (2, d))
            grads[name] = lax.dynamic_slice_in_dim(full, chip * (d // N_CHIPS), d // N_CHIPS, axis=1)
        else:
            grads[name] = _unpack_small(total, name, p[name].shape)
        delta[name], new_m[name], new_v[name] = adamw(p[name], grads[name], m[name], v[name], f"adamw_{name}")
    return (loss, grad_x, *[grads[n] for n in WEIGHTS], *[delta[n] for n in WEIGHTS],
            *[new_m[n] for n in WEIGHTS], *[new_v[n] for n in WEIGHTS])
```

```python
import functools
import math

import jax
import jax.numpy as jnp
from jax import lax
from jax.experimental import pallas as pl
from jax.experimental.pallas import tpu as pltpu

F32 = jnp.float32
BF16 = jnp.bfloat16
HI = lax.Precision.HIGHEST

D_MODEL = 1024
HEADS = 8
HEAD_DIM = 128
CHUNK = 64
SUB = 16
CONV_K = 4
HALO = 16
DEPTH = 4
EPS = 1e-6
MLP_HIDDEN = 4 * D_MODEL
GDN_MAIN = 4 * D_MODEL
GDN_IN = GDN_MAIN + 2 * HEADS
GDN_IN_PAD = GDN_MAIN + 128
NEG = -1e30

ADAM_LR = 0.001
ADAM_B1 = 0.9
ADAM_B2 = 0.999
ADAM_EPS = 1e-08
ADAM_WD = 0.01
ADAM_STEP = 10

VMEM_LIMIT = 48 * 1024 * 1024

MESH = pl.DeviceIdType.MESH


def _cparams(sem=None, **kw):
    if sem is not None:
        kw["dimension_semantics"] = sem
    return pltpu.CompilerParams(vmem_limit_bytes=VMEM_LIMIT, **kw)


def _iota(shape, dim):
    return lax.broadcasted_iota(jnp.int32, shape, dim)


_DIMS = {"nn": (((1,), (0,)), ((), ())), "nt": (((1,), (1,)), ((), ())), "tn": (((0,), (0,)), ((), ()))}


def _dot(a, b, mode):
    return lax.dot_general(a.astype(BF16), b.astype(BF16), _DIMS[mode], preferred_element_type=F32)


@functools.partial(jax.custom_vjp, nondiff_argnums=(2,))
def _mmx(a, b, mode):
    return _dot(a, b, mode)


def _mmx_fwd(a, b, mode):
    return _dot(a, b, mode), (a, b)


def _mmx_bwd(mode, res, g):
    a, b = res
    if mode == "nn":
        return _dot(g, b, "nt"), _dot(a, g, "tn")
    if mode == "nt":
        return _dot(g, b, "nn"), _dot(g, a, "tn")
    return _dot(b, g, "nt"), _dot(a, g, "nn")


_mmx.defvjp(_mmx_fwd, _mmx_bwd)


def _mm(a, b):
    return _mmx(a, b, "nn")


def _mm_nt(a, b):
    return _mmx(a, b, "nt")


def _mm_tn(a, b):
    return _mmx(a, b, "tn")


@functools.partial(jax.custom_vjp, nondiff_argnums=(1,))
def _roll_rows(x, d):
    return pltpu.roll(x, d, 0)


def _roll_rows_fwd(x, d):
    return pltpu.roll(x, d, 0), None


def _roll_rows_bwd(d, _, g):
    return (pltpu.roll(g, g.shape[0] - d, 0),)


_roll_rows.defvjp(_roll_rows_fwd, _roll_rows_bwd)


def _sig(x):
    return 1.0 / (1.0 + jnp.exp(-x))


@jax.custom_vjp
def _sigmoid(x):
    return _sig(x)


def _sigmoid_fwd(x):
    s = _sig(x)
    return s, s


def _sigmoid_bwd(s, g):
    return (g * s * (1.0 - s),)


_sigmoid.defvjp(_sigmoid_fwd, _sigmoid_bwd)


@jax.custom_vjp
def _silu(x):
    return x * _sig(x)


def _silu_fwd(x):
    s = _sig(x)
    return x * s, (x, s)


def _silu_bwd(res, g):
    x, s = res
    return (g * s * (1.0 + x * (1.0 - s)),)


_silu.defvjp(_silu_fwd, _silu_bwd)


@jax.custom_vjp
def _softplus(x):
    return jnp.maximum(x, 0.0) + jnp.log(1.0 + jnp.exp(-jnp.abs(x)))


def _softplus_fwd(x):
    return _softplus(x), x


def _softplus_bwd(x, g):
    return (g * _sig(x),)


_softplus.defvjp(_softplus_fwd, _softplus_bwd)


@jax.custom_vjp
def _log_sigmoid(x):
    return jnp.minimum(x, 0.0) - jnp.log(1.0 + jnp.exp(-jnp.abs(x)))


def _log_sigmoid_fwd(x):
    return _log_sigmoid(x), x


def _log_sigmoid_bwd(x, g):
    return (g * _sig(-x),)


_log_sigmoid.defvjp(_log_sigmoid_fwd, _log_sigmoid_bwd)


@jax.custom_vjp
def _logaddexp(a, b):
    return jnp.maximum(a, b) + jnp.log(1.0 + jnp.exp(-jnp.abs(a - b)))


def _logaddexp_fwd(a, b):
    return _logaddexp(a, b), (a, b)


def _unbroadcast(g, like):
    for ax in range(g.ndim):
        if like.shape[ax] == 1 and g.shape[ax] != 1:
            g = jnp.sum(g, axis=ax, keepdims=True)
    return g


def _logaddexp_bwd(res, g):
    a, b = res
    s = _sig(a - b)
    return _unbroadcast(g * s, a), _unbroadcast(g * (1.0 - s), b)


_logaddexp.defvjp(_logaddexp_fwd, _logaddexp_bwd)


def _row_to_col(row):
    n = row.shape[1]
    eye = _iota((n, n), 0) == _iota((n, n), 1)
    return jnp.sum(jnp.where(eye, jnp.broadcast_to(row, (n, n)), 0.0), axis=1, keepdims=True)


def _col_to_row(col):
    n = col.shape[0]
    eye = _iota((n, n), 0) == _iota((n, n), 1)
    return jnp.sum(jnp.where(eye, jnp.broadcast_to(col, (n, n)), 0.0), axis=0, keepdims=True)


def _pick_row(x, r):
    return jnp.sum(jnp.where(_iota(x.shape, 0) == r, x, 0.0), axis=0, keepdims=True)


def _pick_lane(x, l):
    return jnp.sum(jnp.where(_iota(x.shape, 1) == l, x, 0.0), axis=1, keepdims=True)


def _each(f, *lists):
    return [f(*t) for t in zip(*lists)]


def _unit_lower_inverse(Ls):
    n = Ls[0].shape[0]
    r, c = _iota((n, n), 0), _iota((n, n), 1)
    eye = jnp.where(r == c, 1.0, 0.0).astype(F32)
    Ld = _each(lambda L: jnp.where((r // SUB) == (c // SUB), L, 0.0), Ls)
    Lo = _each(lambda L, d: L - d, Ls, Ld)
    P = _each(lambda d: eye - d, Ld)
    Lp = Ld
    for _ in range(int(math.log2(SUB)) - 1):
        Lp = _each(lambda x: _mm(x, x), Lp)
        P = _each(lambda p, x: p + _mm(p, x), P, Lp)
    N = _each(_mm, P, Lo)
    N2 = _each(lambda x: _mm(x, x), N)
    X = _each(lambda x, x2: (eye - x) + _mm(eye - x, x2), N, N2)
    return _each(_mm, X, P)


def _shift_rows(x, halo, d):
    if d == 0:
        return x
    xr = _roll_rows(x, d)
    hr = _roll_rows(halo, d)
    hfull = jnp.concatenate([hr, jnp.zeros((x.shape[0] - HALO, x.shape[1]), F32)], axis=0)
    return jnp.where(_iota(x.shape, 0) >= d, xr, hfull)


def _causal_conv_chunk(x, halo, w):
    y = None
    for kk in range(CONV_K):
        t = _shift_rows(x, halo, CONV_K - 1 - kk) * _pick_row(w, kk)
        y = t if y is None else y + t
    return y


def _gdn_chunk(xq, xk, xv, hq, hk, hv, gate, ab, S, cwq, cwk, cwv, alog, dtb, onorm, *, heads, seqs):
    C = xq[0].shape[0]
    q = _each(lambda x, h, w: _silu(_causal_conv_chunk(x, h, w)), xq, hq, cwq)
    k = _each(lambda x, h, w: _silu(_causal_conv_chunk(x, h, w)), xk, hk, cwk)
    v = _each(lambda x, h, w: _silu(_causal_conv_chunk(x, h, w)), xv, hv, cwv)
    q = _each(lambda t: t * lax.rsqrt(jnp.sum(t * t, axis=1, keepdims=True) + EPS) * (HEAD_DIM ** -0.5), q)
    k = _each(lambda t: t * lax.rsqrt(jnp.sum(t * t, axis=1, keepdims=True) + EPS), k)
    beta_all = _each(_sigmoid, ab)
    g_all = _each(lambda t: -jnp.exp(alog) * _softplus(t + dtb), ab)
    beta = [_pick_lane(beta_all[b], HEADS + h) for b, h in zip(seqs, heads)]
    g = [_pick_lane(g_all[b], h) for b, h in zip(seqs, heads)]
    r, c = _iota((C, C), 0), _iota((C, C), 1)
    gc = _each(lambda t: jnp.sum(jnp.where(c <= r, jnp.broadcast_to(_col_to_row(t), (C, C)), 0.0), axis=1,
                                 keepdims=True), g)
    gc_row = _each(lambda t: jnp.sum(jnp.where(r <= c, jnp.broadcast_to(t, (C, C)), 0.0), axis=0, keepdims=True), g)
    decay = _each(lambda a, b: jnp.exp(jnp.where(r >= c, a - b, NEG)), gc, gc_row)
    kb = _each(lambda a, b: a * b, k, beta)
    L = _each(lambda a, b, d: jnp.where(r > c, _mm_nt(a, b) * d, 0.0), kb, k, decay)
    A = _each(lambda a, b, d: jnp.where(r >= c, _mm_nt(a, b) * d, 0.0), q, k, decay)
    T = _unit_lower_inverse(L)
    egc = _each(jnp.exp, gc)
    u = _each(lambda t, a, b: _mm(t, a * b), T, v, beta)
    w = _each(lambda t, a, e: _mm(t, a * e), T, kb, egc)
    gl = _each(lambda t: _pick_row(t, C - 1), gc)
    v_new = _each(lambda a, b, s: a - _mm(b, s), u, w, S)
    o = _each(lambda a, e, s, m, vn: _mm(a * e, s) + _mm(m, vn), q, egc, S, A, v_new)
    S_next = _each(lambda s, l, a, t, vn: s * jnp.exp(l) + _mm_tn(a * jnp.exp(l - t), vn), S, gl, k, gc, v_new)
    o = _each(lambda t, gt: t * lax.rsqrt(jnp.mean(t * t, axis=1, keepdims=True) + EPS) * onorm * _silu(gt), o, gate)
    return o, S_next


def _hgrn_lower_bound(lbl, layer):
    e = jnp.exp(lbl - jnp.max(lbl, axis=0, keepdims=True))
    sm = e / jnp.sum(e, axis=0, keepdims=True)
    r = _iota(lbl.shape, 0)
    return jnp.sum(jnp.where((r >= 1) & (r <= layer), sm, 0.0), axis=0, keepdims=True)


_LEVELS = (1, 2, 4, 8, 16, 32)


def _prefix_matrix(n):
    r, c = _iota((n, n), 0), _iota((n, n), 1)
    parts = [jnp.where(c <= r, 1.0, 0.0)]
    for s in _LEVELS:
        parts.append(jnp.where(c < (r // (2 * s)) * (2 * s) + s, 1.0, 0.0))
    return jnp.concatenate(parts, axis=0).astype(F32)


def _prefix_sums_of(x):
    n = x.shape[0]
    y = lax.dot_general(_prefix_matrix(n), x, _DIMS["nn"], precision=HI, preferred_element_type=F32)
    return tuple(y[t * n:(t + 1) * n] for t in range(len(_LEVELS) + 1))


@jax.custom_vjp
def _prefix_sums(x):
    return _prefix_sums_of(x)


def _prefix_sums_fwd(x):
    return _prefix_sums_of(x), None


def _prefix_sums_bwd(_, gs):
    g = jnp.concatenate(gs, axis=0)
    return (lax.dot_general(_prefix_matrix(gs[0].shape[0]), g, _DIMS["tn"], precision=HI, preferred_element_type=F32),)


_prefix_sums.defvjp(_prefix_sums_fwd, _prefix_sums_bwd)


def _hgrn_chunk(qp, fp, v, S, lbl, *, layer):
    C = qp[0].shape[0]
    lb = _each(lambda l: _hgrn_lower_bound(l, layer), lbl)
    lf = _each(lambda l, f: _logaddexp(jnp.log(l), jnp.log(1.0 - l) + _log_sigmoid(f)), lb, fp)
    k = _each(lambda l, f: (1.0 - l) * _sigmoid(-f), lb, fp)
    q = _each(lambda x: _silu(x) * (HEAD_DIM ** -0.5), qp)
    r, c = _iota((C, C), 0), _iota((C, C), 1)
    row = _iota(qp[0].shape, 0)
    sums = _each(_prefix_sums, lf)
    gc = [t[0] for t in sums]
    a = _each(lambda x, y: jnp.where(r == c, _mm_nt(x, y), 0.0), q, k)
    for n, s in enumerate(_LEVELS):
        ref = [t[n + 1] for t in sums]
        upper = (row % (2 * s)) >= s
        same = (r // (2 * s)) == (c // (2 * s))
        q_s = _each(lambda x, g, m: x * jnp.exp(jnp.where(upper, g - m, NEG)), q, gc, ref)
        k_s = _each(lambda x, g, m: x * jnp.exp(jnp.where(upper, NEG, m - g)), k, gc, ref)
        a = _each(lambda t, x, y: t + jnp.where(same, _mm_nt(x, y), 0.0), a, q_s, k_s)
    o = _each(lambda t, x, g, vv, st: _mm(t, vv) + _mm(x * jnp.exp(g), st), a, q, gc, v, S)
    gl = _each(lambda g: _pick_row(g, C - 1), gc)
    S_next = _each(lambda st, l, x, g, vv: st * _row_to_col(jnp.exp(l)) + _mm_tn(x * jnp.exp(l - g), vv),
                   S, gl, k, gc, v)
    return o, S_next


N_CHIPS = 4
N_DEV = 8
_ANY = pl.BlockSpec(memory_space=pl.ANY)


def _mesh_pos():
    return lax.axis_index("x"), lax.axis_index("y"), lax.axis_index("c")


def _other_chips(x, y):
    ps = [(1 - x, y), (x, 1 - y), (1 - x, 1 - y)]
    return [(p, 2 * p[0] + p[1]) for p in ps]


def _remote(src, dst, send_sem, recv_sem, dev):
    return pltpu.make_async_remote_copy(src_ref=src, dst_ref=dst, send_sem=send_sem, recv_sem=recv_sem,
                                        device_id=dev, device_id_type=MESH)


class Carry:
    def __init__(self, ins, out_shapes, sems, start, finish):
        self.ins, self.out_shapes, self.sems, self.start, self.finish = list(ins), list(out_shapes), list(sems), start, finish


def _pcall(body, *, name, grid, in_specs, out_specs, out_shape, scratch_shapes=(), dims, args, carries=()):
    in_specs, out_specs, out_shape = list(in_specs), list(out_specs), list(out_shape)
    scratch_shapes, args = list(scratch_shapes), list(args)
    n_in, n_out, n_scr = len(in_specs), len(out_shape), len(scratch_shapes)
    carries = [c for c in carries if c is not None]
    if not carries:
        res = pl.pallas_call(body, name=name, grid=grid, in_specs=in_specs, out_specs=out_specs, out_shape=out_shape,
                             scratch_shapes=scratch_shapes, compiler_params=_cparams(dims))(*args)
        return list(res), []
    ci = [len(c.ins) for c in carries]
    co = [len(c.out_shapes) for c in carries]
    cs = [len(c.sems) for c in carries]

    def split(seq, sizes):
        out, k = [], 0
        for s in sizes:
            out.append(seq[k:k + s])
            k += s
        return out

    def carried(*refs):
        ins, cins, outs, couts, scr, sems = split(refs, [n_in, sum(ci), n_out, sum(co), n_scr, sum(cs)])
        cins, couts, sems = split(cins, ci), split(couts, co), split(sems, cs)
        ids = [pl.program_id(a) for a in range(len(grid))]
        first, last = ids[0] == 0, ids[0] == grid[0] - 1
        for a in range(1, len(grid)):
            first, last = first & (ids[a] == 0), last & (ids[a] == grid[a] - 1)

        @pl.when(first)
        def _():
            for c, i, o, s in zip(carries, cins, couts, sems):
                c.start(i, o, s)

        body(*ins, *outs, *scr)

        @pl.when(last)
        def _():
            for c, i, o, s in zip(carries, cins, couts, sems):
                c.finish(i, o, s)

    res = pl.pallas_call(
        carried, name=name, grid=grid,
        in_specs=in_specs + [_ANY] * sum(ci), out_specs=out_specs + [_ANY] * sum(co),
        out_shape=out_shape + [s for c in carries for s in c.out_shapes],
        scratch_shapes=scratch_shapes + [s for c in carries for s in c.sems],
        compiler_params=_cparams(("arbitrary",) * len(grid)),
    )(*args, *[a for c in carries for a in c.ins])
    return list(res[:n_out]), split(list(res[n_out:]), co)


def run_carries(carries, name):
    ci = [len(c.ins) for c in carries]
    co = [len(c.out_shapes) for c in carries]
    cs = [len(c.sems) for c in carries]

    def split(seq, sizes):
        out, k = [], 0
        for s in sizes:
            out.append(seq[k:k + s])
            k += s
        return out

    def body(*refs):
        cins, couts, sems = split(refs, [sum(ci), sum(co), sum(cs)])
        cins, couts, sems = split(cins, ci), split(couts, co), split(sems, cs)
        for c, i, o, s in zip(carries, cins, couts, sems):
            c.start(i, o, s)
        for c, i, o, s in zip(carries, cins, couts, sems):
            c.finish(i, o, s)

    res = pl.pallas_call(
        body, name=name, in_specs=[_ANY] * sum(ci), out_specs=[_ANY] * sum(co),
        out_shape=[s for c in carries for s in c.out_shapes], scratch_shapes=[s for c in carries for s in c.sems],
    )(*[a for c in carries for a in c.ins])
    return split(list(res), co)


def gather_carry(arrs):
    n = len(arrs)
    split = [a.ndim == 2 and a.shape[0] % 32 == 0 for a in arrs]

    def plan(ins, outs, sems):
        send_sems, recv_sems, pass_send, pass_recv, local_sems = sems
        x, y, c = _mesh_pos()
        me = 2 * x + y
        peers = _other_chips(x, y)
        locs = [pltpu.make_async_copy(ins[a], outs[a].at[me], local_sems.at[a]) for a in range(n)]
        sends, recvs, passes, pass_recvs = [], [], [], []
        for a in range(n):
            half = arrs[a].shape[0] // 2
            mine, other = pl.ds(c * half, half), pl.ds((1 - c) * half, half)
            for j, ((px, py), t) in enumerate(peers):
                sem = (send_sems.at[a, j], recv_sems.at[a, j])
                if split[a]:
                    sends.append(_remote(ins[a].at[mine], outs[a].at[me, mine], *sem, (px, py, c)))
                    recvs.append(_remote(ins[a].at[mine], outs[a].at[t, mine], *sem, (px, py, c)))
                    psem = (pass_send.at[a, j], pass_recv.at[a, j])
                    passes.append(_remote(outs[a].at[t, mine], outs[a].at[t, mine], *psem, (x, y, 1 - c)))
                    pass_recvs.append(_remote(outs[a].at[t, other], outs[a].at[t, other], *psem, (x, y, 1 - c)))
                else:
                    sends.append(_remote(ins[a], outs[a].at[me], *sem, (px, py, c)))
                    recvs.append(_remote(ins[a], outs[a].at[t], *sem, (px, py, c)))
                    passes.append(None)
                    pass_recvs.append(None)
        return locs, sends, recvs, passes, pass_recvs

    def start(ins, outs, sems):
        locs, sends, _, _, _ = plan(ins, outs, sems)
        for cp in locs + sends:
            cp.start()

    def finish(ins, outs, sems):
        locs, sends, recvs, passes, pass_recvs = plan(ins, outs, sems)
        for r, p in zip(recvs, passes):
            r.wait_recv()
            if p is not None:
                p.start()
        for p in pass_recvs:
            if p is not None:
                p.wait_recv()
        for r in sends + [p for p in passes if p is not None]:
            r.wait_send()
        for cp in locs:
            cp.wait()

    return Carry(arrs, [jax.ShapeDtypeStruct((N_CHIPS,) + a.shape, a.dtype) for a in arrs],
                 [pltpu.SemaphoreType.DMA((n, 3)), pltpu.SemaphoreType.DMA((n, 3)), pltpu.SemaphoreType.DMA((n, 3)),
                  pltpu.SemaphoreType.DMA((n, 3)), pltpu.SemaphoreType.DMA((n,))], start, finish)


def sibling_half_carry(gs):
    n = len(gs)

    def copies(ins, outs, sems):
        send_sems, recv_sems = sems
        x, y, c = _mesh_pos()
        out = []
        for a in range(n):
            half = gs[a].shape[1] // 2
            out.append(_remote(ins[a].at[:, pl.ds((1 - c) * half, half), :], outs[a], send_sems.at[a], recv_sems.at[a],
                               (x, y, 1 - c)))
        return out

    def start(ins, outs, sems):
        for r in copies(ins, outs, sems):
            r.start()

    def finish(ins, outs, sems):
        cps = copies(ins, outs, sems)
        for r in cps:
            r.wait_recv()
        for r in cps:
            r.wait_send()

    return Carry(gs, [jax.ShapeDtypeStruct((g.shape[0], g.shape[1] // 2, g.shape[2]), g.dtype) for g in gs],
                 [pltpu.SemaphoreType.DMA((n,)), pltpu.SemaphoreType.DMA((n,))], start, finish)


def chips_carry(ps):
    n = len(ps)

    def copies(ins, outs, sems):
        send_sems, recv_sems, local_sems = sems
        x, y, c = _mesh_pos()
        me = 2 * x + y
        peers = _other_chips(x, y)
        locs = [pltpu.make_async_copy(ins[a].at[me], outs[a].at[me], local_sems.at[a]) for a in range(n)]
        sends = [_remote(ins[a].at[t], outs[a].at[me], send_sems.at[a, j], recv_sems.at[a, j], (px, py, c))
                 for a in range(n) for j, ((px, py), t) in enumerate(peers)]
        recvs = [_remote(ins[a].at[t], outs[a].at[t], send_sems.at[a, j], recv_sems.at[a, j], (px, py, c))
                 for a in range(n) for j, ((px, py), t) in enumerate(peers)]
        return locs, sends, recvs

    def start(ins, outs, sems):
        locs, sends, _ = copies(ins, outs, sems)
        for cp in locs + sends:
            cp.start()

    def finish(ins, outs, sems):
        locs, sends, recvs = copies(ins, outs, sems)
        for r in recvs:
            r.wait_recv()
        for r in sends:
            r.wait_send()
        for cp in locs:
            cp.wait()

    return Carry(ps, [jax.ShapeDtypeStruct(p.shape, p.dtype) for p in ps],
                 [pltpu.SemaphoreType.DMA((n, 3)), pltpu.SemaphoreType.DMA((n, 3)), pltpu.SemaphoreType.DMA((n,))],
                 start, finish)


def sibling_copy_carry(arrs):
    n = len(arrs)

    def copies(ins, outs, sems):
        send_sems, recv_sems = sems
        x, y, c = _mesh_pos()
        return [_remote(ins[a], outs[a], send_sems.at[a], recv_sems.at[a], (x, y, 1 - c)) for a in range(n)]

    def start(ins, outs, sems):
        for r in copies(ins, outs, sems):
            r.start()

    def finish(ins, outs, sems):
        cps = copies(ins, outs, sems)
        for r in cps:
            r.wait_recv()
        for r in cps:
            r.wait_send()

    return Carry(arrs, [jax.ShapeDtypeStruct(a.shape, a.dtype) for a in arrs],
                 [pltpu.SemaphoreType.DMA((n,)), pltpu.SemaphoreType.DMA((n,))], start, finish)


def gather_all_carry(small):
    flips = [(fx, fy, fc) for fx in (0, 1) for fy in (0, 1) for fc in (0, 1)][1:]

    def copies(ins, outs, sems):
        send_sems, recv_sems, local_sem = sems
        in_ref, out_ref = ins[0], outs[0]
        x, y, c = _mesh_pos()
        me = 4 * x + 2 * y + c
        peers = [((1 - x) if fx else x, (1 - y) if fy else y, (1 - c) if fc else c) for fx, fy, fc in flips]
        loc = pltpu.make_async_copy(in_ref, out_ref.at[me], local_sem.at[0])
        sends = [_remote(in_ref, out_ref.at[me], send_sems.at[k], recv_sems.at[k], p) for k, p in enumerate(peers)]
        recvs = [_remote(in_ref, out_ref.at[4 * p[0] + 2 * p[1] + p[2]], send_sems.at[k], recv_sems.at[k], p)
                 for k, p in enumerate(peers)]
        return loc, sends, recvs

    def start(ins, outs, sems):
        loc, sends, _ = copies(ins, outs, sems)
        loc.start()
        for r in sends:
            r.start()

    def finish(ins, outs, sems):
        loc, sends, recvs = copies(ins, outs, sems)
        for r in recvs:
            r.wait_recv()
        for r in sends:
            r.wait_send()
        loc.wait()

    return Carry([small], [jax.ShapeDtypeStruct((N_DEV,) + small.shape, small.dtype)],
                 [pltpu.SemaphoreType.DMA((N_DEV - 1,)), pltpu.SemaphoreType.DMA((N_DEV - 1,)),
                  pltpu.SemaphoreType.DMA((1,))], start, finish)


def _chunk_spec(bsz, nc, col, rev=False, rows=CHUNK, width=D_MODEL):
    return pl.BlockSpec((bsz, rows, width), lambda n: (0, (nc - 1 - n) if rev else n, col))


def _halo_spec(bsz, nc, col, rev):
    per = CHUNK // HALO
    return pl.BlockSpec((bsz, HALO, D_MODEL),
                        lambda n: (0, jnp.maximum(((nc - 1 - n) if rev else n) * per - 1, 0), col))


def _const_spec(shape):
    nd = len(shape)
    return pl.BlockSpec(shape, lambda n: (0,) * nd)


def _state_spec(bsz, nc, rev=False):
    return pl.BlockSpec((bsz, None, HEADS, HEAD_DIM, HEAD_DIM), lambda n: (0, (nc - 1 - n) if rev else n, 0, 0, 0))


def _lanes(hh):
    return slice(hh * HEAD_DIM, (hh + 1) * HEAD_DIM)


def _chains(bsz):
    return [(b, hh) for b in range(bsz) for hh in range(HEADS)]


def _per_chain(ref, bsz, scale=None):
    vals = [ref[b, :, _lanes(hh)].astype(F32) for b, hh in _chains(bsz)]
    return vals if scale is None else [v * scale for v in vals]


def _per_head(ref, bsz):
    return [ref[:, _lanes(hh)].astype(F32) for _, hh in _chains(bsz)]


def gdn_forward(proj, conv_w, alog, dtb, onorm, bsz, name, carries=()):
    n_tok = proj.shape[0]
    t_len = n_tok // bsz
    nc = t_len // CHUNK
    chains = _chains(bsz)

    def body(xq, xk, xv, hq, hk, hv, gate, ab, cwq, cwk, cwv, al, dt, on, o_ref, ssave_ref, s_ref):
        n = pl.program_id(0)
        keep = jnp.where(n > 0, 1.0, 0.0).astype(F32)

        @pl.when(n == 0)
        def _():
            s_ref[...] = jnp.zeros_like(s_ref)

        S = [s_ref[b, hh] for b, hh in chains]
        for (b, hh), s in zip(chains, S):
            ssave_ref[b, hh] = s
        outs, s_next = _gdn_chunk(
            _per_chain(xq, bsz), _per_chain(xk, bsz), _per_chain(xv, bsz), _per_chain(hq, bsz, keep),
            _per_chain(hk, bsz, keep), _per_chain(hv, bsz, keep), _per_chain(gate, bsz),
            [ab[b].astype(F32) for b in range(bsz)], S, _per_head(cwq, bsz), _per_head(cwk, bsz), _per_head(cwv, bsz),
            al[...], dt[...], on[...], heads=[hh for _, hh in chains], seqs=[b for b, _ in chains])
        for (b, hh), o, s in zip(chains, outs, s_next):
            o_ref[b, :, _lanes(hh)] = o.astype(o_ref.dtype)
            s_ref[b, hh] = s

    cw_spec = lambda col: pl.BlockSpec((CONV_K, D_MODEL), lambda n: (0, col))
    in_specs = [_chunk_spec(bsz, nc, 0), _chunk_spec(bsz, nc, 1), _chunk_spec(bsz, nc, 2),
                _halo_spec(bsz, nc, 0, False), _halo_spec(bsz, nc, 1, False), _halo_spec(bsz, nc, 2, False),
                _chunk_spec(bsz, nc, 3), _chunk_spec(bsz, nc, GDN_MAIN // 128, width=128),
                cw_spec(0), cw_spec(1), cw_spec(2),
                _const_spec((1, 128)), _const_spec((1, 128)), _const_spec((1, 128))]
    out_specs = [_chunk_spec(bsz, nc, 0), _state_spec(bsz, nc)]
    p3 = proj.reshape(bsz, t_len, -1)
    (o, ssave), got = _pcall(
        body, name=name, grid=(nc,), in_specs=in_specs, out_specs=out_specs,
        out_shape=(jax.ShapeDtypeStruct((bsz, t_len, D_MODEL), BF16),
                   jax.ShapeDtypeStruct((bsz, nc, HEADS, HEAD_DIM, HEAD_DIM), F32)),
        scratch_shapes=[pltpu.VMEM((bsz, HEADS, HEAD_DIM, HEAD_DIM), F32)], dims=("arbitrary",),
        args=(p3, p3, p3, p3, p3, p3, p3, p3, conv_w, conv_w, conv_w, alog, dtb, onorm), carries=carries)
    return (o.reshape(n_tok, D_MODEL), ssave), got


def gdn_backward(proj, conv_w, alog, dtb, onorm, s_saved, d_out, bsz, name, carries=()):
    n_tok = proj.shape[0]
    t_len = n_tok // bsz
    nc = t_len // CHUNK
    chains = _chains(bsz)

    def body(xq, xk, xv, hq, hk, hv, gate, ab, cwq, cwk, cwv, al, dt, on, ssave, do,
             dp_ref, dcw_ref, dal_ref, ddt_ref, don_ref, ds_ref, dhalo_ref):
        n = pl.program_id(0)
        nr = nc - 1 - n

        @pl.when(n == 0)
        def _():
            dcw_ref[...] = jnp.zeros_like(dcw_ref)
            dal_ref[...] = jnp.zeros_like(dal_ref)
            ddt_ref[...] = jnp.zeros_like(ddt_ref)
            don_ref[...] = jnp.zeros_like(don_ref)
            ds_ref[...] = jnp.zeros_like(ds_ref)
            dhalo_ref[...] = jnp.zeros_like(dhalo_ref)

        keep = jnp.where(nr > 0, 1.0, 0.0).astype(F32)
        pad = jnp.zeros((CHUNK - HALO, HEAD_DIM), F32)
        args = (_per_chain(xq, bsz), _per_chain(xk, bsz), _per_chain(xv, bsz), _per_chain(hq, bsz, keep),
                _per_chain(hk, bsz, keep), _per_chain(hv, bsz, keep), _per_chain(gate, bsz),
                [ab[b].astype(F32) for b in range(bsz)], [ssave[b, hh] for b, hh in chains],
                _per_head(cwq, bsz), _per_head(cwk, bsz), _per_head(cwv, bsz), al[...], dt[...], on[...])
        _, vjp = jax.vjp(functools.partial(_gdn_chunk, heads=[hh for _, hh in chains], seqs=[b for b, _ in chains]),
                         *args)
        (gxq, gxk, gxv, ghq, ghk, ghv, ggate, gab, gS, gcq, gck, gcv, gal, gdt, gon) = vjp(
            (_per_chain(do, bsz), [ds_ref[b, hh] for b, hh in chains]))
        d = D_MODEL
        for e, (b, hh) in enumerate(chains):
            c0 = hh * HEAD_DIM
            for part, (gx, gh, gc) in enumerate(((gxq, ghq, gcq), (gxk, ghk, gck), (gxv, ghv, gcv))):
                full = gx[e] + jnp.concatenate([pad, dhalo_ref[b, hh, part]], axis=0)
                dp_ref[b, :, part * d + c0:part * d + c0 + HEAD_DIM] = full.astype(dp_ref.dtype)
                dhalo_ref[b, hh, part] = gh[e] * keep
                dcw_ref[hh, part] += gc[e]
            dp_ref[b, :, 3 * d + c0:3 * d + c0 + HEAD_DIM] = ggate[e].astype(dp_ref.dtype)
            ds_ref[b, hh] = gS[e]
        for b in range(bsz):
            dp_ref[b, :, GDN_MAIN:GDN_IN_PAD] = gab[b].astype(dp_ref.dtype)
        dal_ref[...] += gal
        ddt_ref[...] += gdt
        don_ref[...] += gon

    cw_spec = lambda col: pl.BlockSpec((CONV_K, D_MODEL), lambda n: (0, col))
    in_specs = [_chunk_spec(bsz, nc, 0, True), _chunk_spec(bsz, nc, 1, True), _chunk_spec(bsz, nc, 2, True),
                _halo_spec(bsz, nc, 0, True), _halo_spec(bsz, nc, 1, True), _halo_spec(bsz, nc, 2, True),
                _chunk_spec(bsz, nc, 3, True), _chunk_spec(bsz, nc, GDN_MAIN // 128, True, width=128),
                cw_spec(0), cw_spec(1), cw_spec(2),
                _const_spec((1, 128)), _const_spec((1, 128)), _const_spec((1, 128)),
                _state_spec(bsz, nc, True), _chunk_spec(bsz, nc, 0, True)]
    out_specs = [_chunk_spec(bsz, nc, 0, True, width=GDN_IN_PAD),
                 _const_spec((HEADS, 3, CONV_K, HEAD_DIM)), _const_spec((1, 128)), _const_spec((1, 128)),
                 _const_spec((1, 128))]
    row = jax.ShapeDtypeStruct((1, 128), F32)
    p3 = proj.reshape(bsz, t_len, -1)
    (dp, dcw, dal, ddt, don), got = _pcall(
        body, name=name, grid=(nc,), in_specs=in_specs, out_specs=out_specs,
        out_shape=(jax.ShapeDtypeStruct((bsz, t_len, GDN_IN_PAD), BF16),
                   jax.ShapeDtypeStruct((HEADS, 3, CONV_K, HEAD_DIM), F32), row, row, row),
        scratch_shapes=[pltpu.VMEM((bsz, HEADS, HEAD_DIM, HEAD_DIM), F32),
                        pltpu.VMEM((bsz, HEADS, 3, HALO, HEAD_DIM), F32)],
        dims=("arbitrary",),
        args=(p3, p3, p3, p3, p3, p3, p3, p3, conv_w, conv_w, conv_w, alog, dtb, onorm, s_saved,
              d_out.reshape(bsz, t_len, D_MODEL)),
        carries=carries)
    return (dp.reshape(n_tok, GDN_IN_PAD), dcw, dal, ddt, don), got


def hgrn_forward(proj, lbl, layer, bsz, carries=()):
    n_tok = proj.shape[0]
    t_len = n_tok // bsz
    nc = t_len // CHUNK
    chains = _chains(bsz)

    def body(qp, fp, vi, lb, o_ref, ssave_ref, s_ref):
        @pl.when(pl.program_id(0) == 0)
        def _():
            s_ref[...] = jnp.zeros_like(s_ref)

        S = [s_ref[b, hh] for b, hh in chains]
        for (b, hh), s in zip(chains, S):
            ssave_ref[b, hh] = s
        outs, s_next = _hgrn_chunk(_per_chain(qp, bsz), _per_chain(fp, bsz), _per_chain(vi, bsz), S,
                                   _per_head(lb, bsz), layer=layer)
        for (b, hh), o, s in zip(chains, outs, s_next):
            o_ref[b, :, _lanes(hh)] = o
            s_ref[b, hh] = s

    in_specs = [_chunk_spec(bsz, nc, 0), _chunk_spec(bsz, nc, 1), _chunk_spec(bsz, nc, 2),
                _const_spec((DEPTH, D_MODEL))]
    out_specs = [_chunk_spec(bsz, nc, 0), _state_spec(bsz, nc)]
    p3 = proj.reshape(bsz, t_len, -1)
    (o, ssave), got = _pcall(
        body, name=f"hgrn_fwd{layer}", grid=(nc,), in_specs=in_specs, out_specs=out_specs,
        out_shape=(jax.ShapeDtypeStruct((bsz, t_len, D_MODEL), F32),
                   jax.ShapeDtypeStruct((bsz, nc, HEADS, HEAD_DIM, HEAD_DIM), F32)),
        scratch_shapes=[pltpu.VMEM((bsz, HEADS, HEAD_DIM, HEAD_DIM), F32)], dims=("arbitrary",),
        args=(p3, p3, p3, lbl), carries=carries)
    return (o.reshape(n_tok, D_MODEL), ssave), got


def hgrn_backward(proj, lbl, s_saved, d_o, d_gate, layer, bsz, carries=()):
    n_tok = proj.shape[0]
    t_len = n_tok // bsz
    nc = t_len // CHUNK
    chains = _chains(bsz)

    def body(qp, fp, vi, lb, ssave, do, dgt, dp_ref, dlb_ref, ds_ref):
        @pl.when(pl.program_id(0) == 0)
        def _():
            dlb_ref[...] = jnp.zeros_like(dlb_ref)
            ds_ref[...] = jnp.zeros_like(ds_ref)

        _, vjp = jax.vjp(functools.partial(_hgrn_chunk, layer=layer), _per_chain(qp, bsz), _per_chain(fp, bsz),
                         _per_chain(vi, bsz), [ssave[b, hh] for b, hh in chains], _per_head(lb, bsz))
        gq, gf, gv, gS, glb = vjp((_per_chain(do, bsz), [ds_ref[b, hh] for b, hh in chains]))
        d = D_MODEL
        for e, (b, hh) in enumerate(chains):
            c0 = hh * HEAD_DIM
            dp_ref[b, :, c0:c0 + HEAD_DIM] = gq[e].astype(dp_ref.dtype)
            dp_ref[b, :, d + c0:d + c0 + HEAD_DIM] = gf[e].astype(dp_ref.dtype)
            dp_ref[b, :, 2 * d + c0:2 * d + c0 + HEAD_DIM] = gv[e].astype(dp_ref.dtype)
            dlb_ref[hh] += glb[e]
            ds_ref[b, hh] = gS[e]
        dp_ref[:, :, 3 * d:4 * d] = dgt[...]

    in_specs = [_chunk_spec(bsz, nc, 0, True), _chunk_spec(bsz, nc, 1, True), _chunk_spec(bsz, nc, 2, True),
                _const_spec((DEPTH, D_MODEL)), _state_spec(bsz, nc, True),
                _chunk_spec(bsz, nc, 0, True), _chunk_spec(bsz, nc, 0, True)]
    out_specs = [_chunk_spec(bsz, nc, 0, True, width=4 * D_MODEL), _const_spec((HEADS, DEPTH, HEAD_DIM))]
    p3 = proj.reshape(bsz, t_len, -1)
    (dp, dlb), got = _pcall(
        body, name=f"hgrn_bwd{layer}", grid=(nc,), in_specs=in_specs, out_specs=out_specs,
        out_shape=(jax.ShapeDtypeStruct((bsz, t_len, 4 * D_MODEL), BF16),
                   jax.ShapeDtypeStruct((HEADS, DEPTH, HEAD_DIM), F32)),
        scratch_shapes=[pltpu.VMEM((bsz, HEADS, HEAD_DIM, HEAD_DIM), F32)], dims=("arbitrary",),
        args=(p3, p3, p3, lbl, s_saved, d_o.reshape(bsz, t_len, D_MODEL), d_gate.reshape(bsz, t_len, D_MODEL)),
        carries=carries)
    return (dp.reshape(n_tok, 4 * D_MODEL), dlb), got


ROW_TILE = 512
MM_VMEM_BUDGET = 36 * 1024 * 1024


def _tile(n):
    for cand in (1024, 512, 1408, 384, 256, 128):
        if n % cand == 0:
            return cand
    return n


def _rmsnorm(x, w):
    return x * lax.rsqrt(jnp.mean(x * x, axis=1, keepdims=True) + EPS) * w


def norm_matmul(h, nw, w, relu, name, carries=()):
    n_tok, d = h.shape
    slots = w.shape[0] if w.ndim == 3 else 0
    nb = w.shape[2] if slots else w.shape[1]
    n_out = max(slots, 1) * nb
    tm = min(n_tok, ROW_TILE)
    if slots:
        w_specs = [pl.BlockSpec((None, d, nb), lambda i, s=s: (s, 0, 0)) for s in range(slots)]
    else:
        w_specs = [pl.BlockSpec((d, nb), lambda i: (0, 0))]
    nw_ = len(w_specs)

    def body(*refs):
        h_ref, nw_ref, w_refs = refs[0], refs[1], refs[2:2 + nw_]
        y_ref, u_ref = refs[2 + nw_:]
        y = _rmsnorm(h_ref[...], nw_ref[...]).astype(BF16)
        y_ref[...] = y
        for s, w_ref in enumerate(w_refs):
            acc = jnp.dot(y, w_ref[...], preferred_element_type=F32)
            u_ref[:, s * nb:(s + 1) * nb] = (jnp.maximum(acc, 0.0) if relu else acc).astype(BF16)

    row = pl.BlockSpec((tm, d), lambda i: (i, 0))
    return _pcall(
        body, name=name, grid=(n_tok // tm,),
        in_specs=[row, pl.BlockSpec((1, d), lambda i: (0, 0))] + w_specs,
        out_specs=[row, pl.BlockSpec((tm, n_out), lambda i: (i, 0))],
        out_shape=(jax.ShapeDtypeStruct((n_tok, d), BF16), jax.ShapeDtypeStruct((n_tok, n_out), BF16)),
        dims=("parallel",), args=[h, nw] + [w] * nw_, carries=carries)


def _mm_tiles(m, n, k, extra_bytes):
    tn = _tile(n)
    for tm in (1024, 512, 256, 128):
        if m % tm == 0 and 2 * (2 * tm * k + 2 * k * tn + (4 + extra_bytes) * tm * tn) <= MM_VMEM_BUDGET:
            return tm, tn
    return min(m, 128), tn


def matmul(a, b, mode, name, out_dtype=F32, extra=None, epilogue=None, shards=1, carries=(), square_a=False):
    slots = b.shape[0] if (mode == "nt" and b.ndim == 3) else 0
    if mode == "nn":
        (m, k), n = a.shape, b.shape[1]
    elif mode == "nt":
        (m, k), n = a.shape, (b.shape[1] if slots else b.shape[0])
    else:
        (k, m), n = a.shape, b.shape[1]
    tm, tn = _mm_tiles(m, n // shards, k, 0 if extra is None else extra.dtype.itemsize)
    a_spec = pl.BlockSpec((k, tm), lambda i, j: (0, i)) if mode == "tn" else pl.BlockSpec((tm, k), lambda i, j: (i, 0))
    if slots:
        kb = b.shape[2]
        b_specs = [pl.BlockSpec((None, tn, kb), lambda i, j, s=s: (s, j, 0)) for s in range(slots)]
    elif mode == "nt":
        b_specs = [pl.BlockSpec((tn, k), lambda i, j: (j, 0))]
    else:
        b_specs = [pl.BlockSpec((k, tn), lambda i, j: (0, j))]
    nb = len(b_specs)
    dims = _DIMS[mode]

    def body(*refs):
        a_ref, b_refs = refs[0], refs[1:1 + nb]
        e_ref = refs[1 + nb] if extra is not None else None
        o_ref = refs[-1]
        if slots:
            acc = None
            for s in range(slots):
                part = lax.dot_general(a_ref[:, s * kb:(s + 1) * kb].astype(BF16), b_refs[s][...].astype(BF16), dims,
                                       preferred_element_type=F32)
                acc = part if acc is None else acc + part
        else:
            av = a_ref[...].astype(BF16)
            if square_a:
                av = av * av
            acc = lax.dot_general(av, b_refs[0][...].astype(BF16), dims, preferred_element_type=F32)
        if epilogue == "add":
            acc = e_ref[...] + acc
        elif epilogue == "mul2":
            acc = acc * (2.0 * e_ref[...].astype(F32))
        o_ref[...] = acc.astype(o_ref.dtype)

    in_specs = [a_spec] + b_specs
    args = [a] + [b] * nb
    if extra is not None:
        in_specs.append(pl.BlockSpec((tm, tn), lambda i, j: (i, j)))
        args.append(extra)
    if shards > 1:
        per = n // shards // tn
        out_spec = pl.BlockSpec((None, tm, tn), lambda i, j: (j // per, i, j % per))
        out_shape = jax.ShapeDtypeStruct((shards, m, n // shards), out_dtype)
    else:
        out_spec = pl.BlockSpec((tm, tn), lambda i, j: (i, j))
        out_shape = jax.ShapeDtypeStruct((m, n), out_dtype)
    (res,), cres = _pcall(
        body, name=name, grid=(m // tm, n // tn), in_specs=in_specs, out_specs=[out_spec], out_shape=[out_shape],
        dims=("parallel", "arbitrary"), args=args, carries=carries)
    return res, cres


def matmul_norm_backward(a, b, h, nw, dres, name, carries=()):
    slots = b.shape[0] if b.ndim == 3 else 0
    n_tok, k = a.shape
    d = h.shape[1]
    tm = min(n_tok, ROW_TILE)
    if slots:
        kb = b.shape[2]
        b_specs = [pl.BlockSpec((None, d, kb), lambda i, s=s: (s, 0, 0)) for s in range(slots)]
    else:
        b_specs = [pl.BlockSpec((d, k), lambda i: (0, 0))]
    nb = len(b_specs)

    def body(*refs):
        a_ref, b_refs = refs[0], refs[1:1 + nb]
        h_ref, nw_ref, dr_ref, dh_ref, dhb_ref, dnw_ref = refs[1 + nb:]

        @pl.when(pl.program_id(0) == 0)
        def _():
            dnw_ref[...] = jnp.zeros_like(dnw_ref)

        if slots:
            dy = None
            for s in range(slots):
                part = lax.dot_general(a_ref[:, s * kb:(s + 1) * kb], b_refs[s][...], _DIMS["nt"],
                                       preferred_element_type=F32)
                dy = part if dy is None else dy + part
        else:
            dy = lax.dot_general(a_ref[...], b_refs[0][...], _DIMS["nt"], preferred_element_type=F32)
        _, vjp = jax.vjp(_rmsnorm, h_ref[...], nw_ref[...])
        gh, gw = vjp(dy)
        dh = dr_ref[...] + gh
        dh_ref[...] = dh
        dhb_ref[...] = dh.astype(BF16)
        dnw_ref[...] += gw

    row = pl.BlockSpec((tm, d), lambda i: (i, 0))
    vec = pl.BlockSpec((1, d), lambda i: (0, 0))
    return _pcall(
        body, name=name, grid=(n_tok // tm,),
        in_specs=[pl.BlockSpec((tm, k), lambda i: (i, 0))] + b_specs + [row, vec, row], out_specs=[row, row, vec],
        out_shape=(jax.ShapeDtypeStruct((n_tok, d), F32), jax.ShapeDtypeStruct((n_tok, d), BF16),
                   jax.ShapeDtypeStruct((1, d), F32)),
        dims=("arbitrary",), args=[a] + [b] * nb + [h, nw, dres], carries=carries)


def _hgrn_post(o, gate, gw):
    return _rmsnorm(o, gw) * _silu(gate)


def hgrn_post_forward(o, proj, gw, name):
    n_tok, d = o.shape
    tm = min(n_tok, ROW_TILE)

    def body(o_ref, g_ref, w_ref, y_ref):
        y_ref[...] = _hgrn_post(o_ref[...], g_ref[...].astype(F32), w_ref[...]).astype(BF16)

    row = pl.BlockSpec((tm, d), lambda i: (i, 0))
    return pl.pallas_call(
        body, name=name, grid=(n_tok // tm,),
        in_specs=[row, pl.BlockSpec((tm, d), lambda i: (i, 3)), pl.BlockSpec((1, d), lambda i: (0, 0))],
        out_specs=row, out_shape=jax.ShapeDtypeStruct((n_tok, d), BF16),
        compiler_params=_cparams(("parallel",)),
    )(o, proj, gw)


def hgrn_post_backward(o, proj, gw, dy, name):
    n_tok, d = o.shape
    tm = min(n_tok, ROW_TILE)

    def body(o_ref, g_ref, w_ref, dy_ref, do_ref, dg_ref, dw_ref):
        @pl.when(pl.program_id(0) == 0)
        def _():
            dw_ref[...] = jnp.zeros_like(dw_ref)

        _, vjp = jax.vjp(_hgrn_post, o_ref[...], g_ref[...].astype(F32), w_ref[...])
        go, gg, gw_ = vjp(dy_ref[...])
        do_ref[...] = go
        dg_ref[...] = gg.astype(BF16)
        dw_ref[...] += gw_

    row = pl.BlockSpec((tm, d), lambda i: (i, 0))
    vec = pl.BlockSpec((1, d), lambda i: (0, 0))
    return pl.pallas_call(
        body, name=name, grid=(n_tok // tm,),
        in_specs=[row, pl.BlockSpec((tm, d), lambda i: (i, 3)), vec, row], out_specs=[row, row, vec],
        out_shape=(jax.ShapeDtypeStruct((n_tok, d), F32), jax.ShapeDtypeStruct((n_tok, d), BF16),
                   jax.ShapeDtypeStruct((1, d), F32)),
        compiler_params=_cparams(("arbitrary",)),
    )(o, proj, gw, dy)


def loss_head(h, nw, target):
    n_tok, d = h.shape
    tm = min(n_tok, ROW_TILE)

    def body(h_ref, nw_ref, t_ref, loss_ref, dh_ref, dhb_ref, dnw_ref):
        @pl.when(pl.program_id(0) == 0)
        def _():
            dnw_ref[...] = jnp.zeros_like(dnw_ref)
            loss_ref[...] = jnp.zeros_like(loss_ref)

        out, vjp = jax.vjp(_rmsnorm, h_ref[...], nw_ref[...])
        err = out - t_ref[...]
        part = 0.5 * jnp.sum(jnp.sum(err * err, axis=1, keepdims=True), axis=0, keepdims=True) / d
        loss_ref[...] += jnp.broadcast_to(part, loss_ref.shape)
        gh, gw = vjp(err / d)
        dh_ref[...] = gh
        dhb_ref[...] = gh.astype(BF16)
        dnw_ref[...] += gw

    row = pl.BlockSpec((tm, d), lambda i: (i, 0))
    vec = pl.BlockSpec((1, d), lambda i: (0, 0))
    return pl.pallas_call(
        body, name="loss_head", grid=(n_tok // tm,), in_specs=[row, vec, row],
        out_specs=[pl.BlockSpec((1, 128), lambda i: (0, 0)), row, row, vec],
        out_shape=(jax.ShapeDtypeStruct((1, 128), F32), jax.ShapeDtypeStruct((n_tok, d), F32),
                   jax.ShapeDtypeStruct((n_tok, d), BF16), jax.ShapeDtypeStruct((1, d), F32)),
        compiler_params=_cparams(("arbitrary",)),
    )(h, nw, target)


def _rows2d(shape):
    if len(shape) == 1:
        return (1, shape[0])
    return (math.prod(shape[:-1]), shape[-1])


def adamw(w, g, m, v, name):
    shape = w.shape
    r, c = _rows2d(shape)
    tr = r if r <= 256 else 256
    c1 = 1.0 / (1.0 - ADAM_B1 ** ADAM_STEP)
    c2 = 1.0 / (1.0 - ADAM_B2 ** ADAM_STEP)

    def body(w_ref, g_ref, m_ref, v_ref, d_ref, nm_ref, nv_ref):
        gg = g_ref[...]
        nm = ADAM_B1 * m_ref[...] + (1.0 - ADAM_B1) * gg
        nv = ADAM_B2 * v_ref[...] + (1.0 - ADAM_B2) * (gg * gg)
        d_ref[...] = -ADAM_LR * ((nm * c1) / (jnp.sqrt(nv * c2) + ADAM_EPS) + ADAM_WD * w_ref[...])
        nm_ref[...] = nm
        nv_ref[...] = nv

    spec = pl.BlockSpec((tr, c), lambda i: (i, 0))
    sds = jax.ShapeDtypeStruct((r, c), F32)
    outs = pl.pallas_call(
        body, name=name, grid=(r // tr,), in_specs=[spec] * 4, out_specs=[spec] * 3, out_shape=(sds,) * 3,
        compiler_params=_cparams(("parallel",)),
    )(w.reshape(r, c), g.reshape(r, c), m.reshape(r, c), v.reshape(r, c))
    return tuple(o.reshape(shape) for o in outs)


def add_slots(parts, name):
    s, r, c = parts.shape
    tr = r if r <= 256 else 256

    def body(p_ref, o_ref):
        acc = p_ref[0]
        for t in range(1, s):
            acc = acc + p_ref[t]
        o_ref[...] = acc

    return pl.pallas_call(
        body, name=name, grid=(r // tr,), in_specs=[pl.BlockSpec((s, tr, c), lambda i: (0, i, 0))],
        out_specs=pl.BlockSpec((tr, c), lambda i: (i, 0)), out_shape=jax.ShapeDtypeStruct((r, c), F32),
        compiler_params=_cparams(("parallel",)),
    )(parts)


BIG = ("gdn_w_in", "gdn_w_out", "hgrn_w_in", "hgrn_w_out", "mlp_w_up", "mlp_w_down")
WEIGHTS = ("gdn_w_in", "gdn_conv", "gdn_a_log", "gdn_dt_bias", "gdn_onorm", "gdn_w_out", "hgrn_w_in", "hgrn_lb_logits",
           "hgrn_gnorm", "hgrn_w_out", "norm_mix", "norm_mlp", "mlp_w_up", "mlp_w_down", "norm_final")


def _pad_lanes(v, n):
    return jnp.pad(v, [(0, 0)] * (v.ndim - 1) + [(0, n - v.shape[-1])])


_SMALL_LAYOUT = {
    "norm_mix": (0, 4, D_MODEL), "norm_mlp": (8, 4, D_MODEL), "norm_final": (16, 1, D_MODEL),
    "hgrn_lb_logits": (24, 4, D_MODEL), "gdn_onorm": (32, 2, 128), "gdn_a_log": (40, 2, HEADS),
    "gdn_dt_bias": (48, 2, HEADS), "loss": (56, 1, 128), "gdn_conv": (64, 24, D_MODEL), "hgrn_gnorm": (88, 2, D_MODEL),
}


def _pack_small(small, loss_row):
    rows = []
    for name, (first, nrow, lanes) in _SMALL_LAYOUT.items():
        v = loss_row if name == "loss" else small[name]
        v = _pad_lanes(v.reshape(nrow, -1), D_MODEL)
        rows.append(jnp.pad(v, ((0, -nrow % 8), (0, 0))))
    return jnp.concatenate(rows, axis=0)


def _unpack_small(packed, name, shape):
    first, nrow, lanes = _SMALL_LAYOUT[name]
    return packed[first:first + nrow, :lanes].reshape(shape)


def add_core_halves(g, theirs, core, name):
    s, r, c = g.shape
    r2 = r // 2
    tr = min(r2, 256)
    nb = r2 // tr

    def body(core_ref, g_ref, t_ref, o_ref):
        o_ref[...] = (g_ref[...] + t_ref[...]).astype(o_ref.dtype)

    grid_spec = pltpu.PrefetchScalarGridSpec(
        num_scalar_prefetch=1, grid=(s, nb),
        in_specs=[pl.BlockSpec((None, tr, c), lambda t, i, cr: (t, cr[0] * nb + i, 0)),
                  pl.BlockSpec((None, tr, c), lambda t, i, cr: (t, i, 0))],
        out_specs=pl.BlockSpec((None, tr, c), lambda t, i, cr: (t, i, 0)))
    return pl.pallas_call(body, name=name, grid_spec=grid_spec, out_shape=jax.ShapeDtypeStruct((s, r2, c), BF16),
                          compiler_params=_cparams(("parallel", "parallel")))(core, g, theirs)


def add_chip_slots(slots, name):
    n_l = len(slots)
    s, r2, c = slots[0].shape
    tr = min(r2, 256)
    nb = r2 // tr

    def body(*refs):
        ins, o_ref = refs[:n_l], refs[n_l]
        for k in range(n_l):
            @pl.when(pl.program_id(0) == k)
            def _(k=k):
                acc = ins[k][0].astype(F32)
                for t in range(1, s):
                    acc = acc + ins[k][t].astype(F32)
                o_ref[...] = acc

    in_specs = [pl.BlockSpec((s, tr, c), lambda l, i, k=k: (0, jnp.where(l == k, i, 0), 0)) for k in range(n_l)]
    return pl.pallas_call(
        body, name=name, grid=(n_l, nb), in_specs=in_specs, out_specs=pl.BlockSpec((None, tr, c), lambda l, i: (l, i, 0)),
        out_shape=jax.ShapeDtypeStruct((n_l, r2, c), F32), compiler_params=_cparams(("arbitrary", "arbitrary")),
    )(*slots)


def adamw_halves(w, m, v, mine, theirs, name):
    n_l, r, c = w.shape
    r2 = r // 2
    tr = min(r2, 256)
    nb = r2 // tr
    c1 = 1.0 / (1.0 - ADAM_B1 ** ADAM_STEP)
    c2 = 1.0 / (1.0 - ADAM_B2 ** ADAM_STEP)

    def body(w_ref, m_ref, v_ref, mine_ref, theirs_ref, g_ref, d_ref, nm_ref, nv_ref):
        my_half = (pl.program_id(1) // nb) == lax.axis_index("c")
        gg = jnp.where(my_half, mine_ref[...], theirs_ref[...])
        nm = ADAM_B1 * m_ref[...] + (1.0 - ADAM_B1) * gg
        nv = ADAM_B2 * v_ref[...] + (1.0 - ADAM_B2) * (gg * gg)
        g_ref[...] = gg
        d_ref[...] = -ADAM_LR * ((nm * c1) / (jnp.sqrt(nv * c2) + ADAM_EPS) + ADAM_WD * w_ref[...])
        nm_ref[...] = nm
        nv_ref[...] = nv

    full = pl.BlockSpec((None, tr, c), lambda l, i: (l, i, 0))
    half = pl.BlockSpec((None, tr, c), lambda l, i: (l, i % nb, 0))
    sds = jax.ShapeDtypeStruct((n_l, r, c), F32)
    return pl.pallas_call(
        body, name=name, grid=(n_l, r // tr), in_specs=[full, full, full, half, half], out_specs=[full] * 4,
        out_shape=(sds,) * 4, compiler_params=_cparams(("parallel", "parallel")),
    )(w, m, v, mine, theirs)


def _layer_weight(kind, i):
    if kind == "up":
        return "mlp_w_up", i
    if kind == "down":
        return "mlp_w_down", i
    return ("gdn_w_" if i % 2 == 0 else "hgrn_w_") + kind, i // 2


def kernel(x, gdn_w_in, gdn_conv, gdn_a_log, gdn_dt_bias, gdn_onorm, gdn_w_out, hgrn_w_in, hgrn_lb_logits, hgrn_gnorm, hgrn_w_out, norm_mix, norm_mlp, mlp_w_up, mlp_w_down, norm_final, loss_target, m_gdn_w_in, m_gdn_conv, m_gdn_a_log, m_gdn_dt_bias, m_gdn_onorm, m_gdn_w_out, m_hgrn_w_in, m_hgrn_lb_logits, m_hgrn_gnorm, m_hgrn_w_out, m_norm_mix, m_norm_mlp, m_mlp_w_up, m_mlp_w_down, m_norm_final, v_gdn_w_in, v_gdn_conv, v_gdn_a_log, v_gdn_dt_bias, v_gdn_onorm, v_gdn_w_out, v_hgrn_w_in, v_hgrn_lb_logits, v_hgrn_gnorm, v_hgrn_w_out, v_norm_mix, v_norm_mlp, v_mlp_w_up, v_mlp_w_down, v_norm_final):
    p = dict(gdn_w_in=gdn_w_in, gdn_conv=gdn_conv, gdn_a_log=gdn_a_log, gdn_dt_bias=gdn_dt_bias, gdn_onorm=gdn_onorm,
             gdn_w_out=gdn_w_out, hgrn_w_in=hgrn_w_in, hgrn_lb_logits=hgrn_lb_logits, hgrn_gnorm=hgrn_gnorm,
             hgrn_w_out=hgrn_w_out, norm_mix=norm_mix, norm_mlp=norm_mlp, mlp_w_up=mlp_w_up, mlp_w_down=mlp_w_down,
             norm_final=norm_final)
    m = dict(gdn_w_in=m_gdn_w_in, gdn_conv=m_gdn_conv, gdn_a_log=m_gdn_a_log, gdn_dt_bias=m_gdn_dt_bias,
             gdn_onorm=m_gdn_onorm, gdn_w_out=m_gdn_w_out, hgrn_w_in=m_hgrn_w_in, hgrn_lb_logits=m_hgrn_lb_logits,
             hgrn_gnorm=m_hgrn_gnorm, hgrn_w_out=m_hgrn_w_out, norm_mix=m_norm_mix, norm_mlp=m_norm_mlp,
             mlp_w_up=m_mlp_w_up, mlp_w_down=m_mlp_w_down, norm_final=m_norm_final)
    v = dict(gdn_w_in=v_gdn_w_in, gdn_conv=v_gdn_conv, gdn_a_log=v_gdn_a_log, gdn_dt_bias=v_gdn_dt_bias,
             gdn_onorm=v_gdn_onorm, gdn_w_out=v_gdn_w_out, hgrn_w_in=v_hgrn_w_in, hgrn_lb_logits=v_hgrn_lb_logits,
             hgrn_gnorm=v_hgrn_gnorm, hgrn_w_out=v_hgrn_w_out, norm_mix=v_norm_mix, norm_mlp=v_norm_mlp,
             mlp_w_up=v_mlp_w_up, mlp_w_down=v_mlp_w_down, norm_final=v_norm_final)
    xi, yi, ci = _mesh_pos()
    chip = 2 * xi + yi
    core = jnp.reshape(ci, (1,)).astype(jnp.int32)
    d = D_MODEL
    bsz, t_len, _ = x.shape
    n_tok = bsz * t_len

    def shard(kind, i):
        name, idx = _layer_weight(kind, i)
        return p[name][idx].astype(BF16)

    def w_in_of(i, slots):
        if i % 2 == 0:
            return _pad_lanes(jnp.transpose(slots, (1, 0, 2)).reshape(d, GDN_IN), GDN_IN_PAD)
        return slots

    (first,) = run_carries([gather_carry([shard("in", 0), p["gdn_conv"], p["hgrn_gnorm"]])], "gather_first")
    conv = jnp.transpose(first[1], (1, 2, 0, 3)).reshape(DEPTH // 2, CONV_K, 3 * d)
    gnorm = jnp.transpose(first[2], (1, 0, 2)).reshape(DEPTH // 2, d)
    lbl = p["hgrn_lb_logits"]
    h = x.reshape(n_tok, d)
    next_in, next_out = first[0], None
    saved = []
    for i in range(DEPTH):
        j = i // 2
        w_in = w_in_of(i, next_in)
        nmix = p["norm_mix"][i][None, :]
        (y, proj), got = norm_matmul(h, nmix, w_in, False, f"in_proj{i}",
                                     [gather_carry([shard("out", 0)])] if i == 0 else [])
        if i == 0:
            next_out = got[0][0]
        ride = [gather_carry([shard("up", i), shard("down", i)])]
        if i % 2 == 0:
            al = _pad_lanes(p["gdn_a_log"][j][None, :], 128)
            dtb = _pad_lanes(p["gdn_dt_bias"][j][None, :], 128)
            on = p["gdn_onorm"][j][None, :]
            (og, ssave), got = gdn_forward(proj, conv[j], al, dtb, on, bsz, f"gdn_fwd{i}", ride)
            mix = (proj, ssave, al, dtb, on)
        else:
            (o, ssave), got = hgrn_forward(proj, lbl, i, bsz, ride)
            gn = gnorm[j][None, :]
            og = hgrn_post_forward(o, proj, gn, f"hgrn_post{i}")
            mix = (proj, ssave, o, gn)
        w_up, w_down = got[0][0], got[0][1].reshape(MLP_HIDDEN, d)
        w_out = next_out.reshape(d, d)
        h1, _ = matmul(og, w_out, "nn", f"out_proj{i}", extra=h, epilogue="add")
        nmlp = p["norm_mlp"][i][None, :]
        (z, r), got = norm_matmul(h1, nmlp, w_up, True, f"mlp_up{i}",
                                     [gather_carry([shard("in", i + 1)])] if i + 1 < DEPTH else [])
        if i + 1 < DEPTH:
            next_in = got[0][0]
        h2, got = matmul(r, w_down, "nn", f"mlp_down{i}", extra=h1, epilogue="add", square_a=True,
                         carries=[gather_carry([shard("out", i + 1)])] if i + 1 < DEPTH else [])
        if i + 1 < DEPTH:
            next_out = got[0][0]
        saved.append((h, nmix, y, mix, og, w_in, w_out, h1, nmlp, z, r, w_up, w_down))
        h = h2
    loss_row, dh, dhb, d_nf = loss_head(h, p["norm_final"][None, :], loss_target.reshape(n_tok, d))

    G = {k: [None] * DEPTH for k in ("in", "out", "up", "down")}
    P = {k: [None] * DEPTH for k in ("in", "out", "up", "down")}
    slots = {k: [None] * DEPTH for k in ("in", "out", "up", "down")}
    d_nmix, d_nmlp = [None] * DEPTH, [None] * DEPTH
    d_conv, d_alog, d_dtb, d_onorm, d_gnorm = [None] * 2, [None] * 2, [None] * 2, [None] * 2, [None] * 2
    d_lbl = jnp.zeros((DEPTH, d), F32)
    for i in reversed(range(DEPTH)):
        j = i // 2
        h_in, nmix, y, mix, og, w_in, w_out, h1, nmlp, z, r, w_up, w_down = saved[i]
        ride = [sibling_half_carry([G["in"][i + 1]])] if i + 1 < DEPTH else []
        du, got = matmul(dhb, w_down, "nt", f"d_mlp_act{i}", out_dtype=BF16, extra=r, epilogue="mul2", carries=ride)
        if i + 1 < DEPTH:
            P["in"][i + 1] = add_core_halves(G["in"][i + 1], got[0][0], core, f"add_cores_in{i + 1}")
        G["down"][i] = matmul(r, dhb, "tn", f"dw_down{i}", square_a=True)[0].reshape(N_CHIPS, -1, d)
        G["up"][i], _ = matmul(z, du, "tn", f"dw_up{i}", shards=N_CHIPS)
        (dh1, dh1b, d_nmlp[i]), got = matmul_norm_backward(du, w_up, h1, nmlp, dh, f"d_mlp_in{i}",
                                                           [sibling_half_carry([G["up"][i], G["down"][i]])])
        P["up"][i] = add_core_halves(G["up"][i], got[0][0], core, f"add_cores_up{i}")
        P["down"][i] = add_core_halves(G["down"][i], got[0][1], core, f"add_cores_down{i}")
        dog, _ = matmul(dh1b, w_out, "nt", f"d_mix_out{i}")
        G["out"][i] = matmul(og, dh1b, "tn", f"dw_out{i}")[0].reshape(N_CHIPS, -1, d)
        to_chips = [("up", i), ("down", i)] + ([("in", i + 1), ("out", i + 1)] if i + 1 < DEPTH else [])
        ride = [sibling_half_carry([G["out"][i]]), chips_carry([P[k][l] for k, l in to_chips])]
        if i % 2 == 0:
            proj, ssave, al, dtb, on = mix
            (dproj, dcw, dal, ddt, don), got = gdn_backward(proj, conv[j], al, dtb, on, ssave, dog, bsz, f"gdn_bwd{i}", ride)
            d_conv[j] = jnp.transpose(dcw, (2, 1, 0, 3)).reshape(CONV_K, 3 * d)
            d_alog[j], d_dtb[j], d_onorm[j] = dal[0, :HEADS], ddt[0, :HEADS], don[0]
        else:
            proj, ssave, o, gn = mix
            do_raw, dgate, dgn = hgrn_post_backward(o, proj, gn, dog, f"d_hgrn_post{i}")
            (dproj, dlb), got = hgrn_backward(proj, lbl, ssave, do_raw, dgate, i, bsz, ride)
            d_gnorm[j] = dgn[0]
            d_lbl = d_lbl + jnp.transpose(dlb, (1, 0, 2)).reshape(DEPTH, d)
        P["out"][i] = add_core_halves(G["out"][i], got[0][0], core, f"add_cores_out{i}")
        for (k, l), s in zip(to_chips, got[1]):
            slots[k][l] = s
        if i % 2 == 0:
            dw_in = matmul(y, dproj, "tn", f"dw_in{i}")[0][:, :GDN_IN]
            G["in"][i] = jnp.transpose(dw_in.reshape(d, N_CHIPS, GDN_IN // N_CHIPS), (1, 0, 2))
        else:
            G["in"][i], _ = matmul(y, dproj, "tn", f"dw_in{i}", shards=N_CHIPS)
        ride = [sibling_half_carry([G["in"][0]]), chips_carry([P["out"][0]])] if i == 0 else []
        (dh, dhb, d_nmix[i]), got = matmul_norm_backward(dproj, w_in, h_in, nmix, dh1, f"d_in_proj{i}", ride)
        if i == 0:
            P["in"][0] = add_core_halves(G["in"][0], got[0][0], core, "add_cores_in0")
            slots["out"][0] = got[1][0]
    grad_x = dh.reshape(x.shape)
    ((slots["in"][0],),) = run_carries([chips_carry([P["in"][0]])], "reduce_chips_last")

    by_weight = {}
    for kind in ("in", "out", "up", "down"):
        for i in range(DEPTH):
            by_weight.setdefault(_layer_weight(kind, i)[0], []).append(slots[kind][i])
    mine = {name: add_chip_slots(by_weight[name], f"add_chips_{name}") for name in BIG}
    small = {
        "gdn_conv": jnp.stack(d_conv), "gdn_a_log": jnp.stack(d_alog), "gdn_dt_bias": jnp.stack(d_dtb),
        "gdn_onorm": jnp.stack(d_onorm), "hgrn_lb_logits": d_lbl, "hgrn_gnorm": jnp.stack(d_gnorm),
        "norm_mix": jnp.concatenate(d_nmix, axis=0), "norm_mlp": jnp.concatenate(d_nmlp, axis=0), "norm_final": d_nf[0],
    }
    theirs, (blocks,) = run_carries([sibling_copy_carry([mine[name] for name in BIG]),
                                     gather_all_carry(_pack_small(small, loss_row))], "share_cores")
    theirs = dict(zip(BIG, theirs))

    total = add_slots(blocks, "add_small")
    loss = total[_SMALL_LAYOUT["loss"][0], 0]
    grads, delta, new_m, new_v = {}, {}, {}, {}
    for name in WEIGHTS:
        if name in BIG:
            grads[name], delta[name], new_m[name], new_v[name] = adamw_halves(
                p[name], m[name], v[name], mine[name], theirs[name], f"adamw_{name}")
            continue
        if name == "gdn_conv":
            full = _unpack_small(total, name, (2, CONV_K, 3 * d))
            grads[name] = lax.dynamic_slice_in_dim(full, chip * (3 * d // N_CHIPS), 3 * d // N_CHIPS, axis=2)
        elif name == "hgrn_gnorm":
            full = _unpack_small(total, name, (2, d))
            grads[name] = lax.dynamic_slice_in_dim(full, chip * (d // N_CHIPS), d // N_CHIPS, axis=1)
        else:
            grads[name] = _unpack_small(total, name, p[name].shape)
        delta[name], new_m[name], new_v[name] = adamw(p[name], grads[name], m[name], v[name], f"adamw_{name}")
    return (loss, grad_x, *[grads[n] for n in WEIGHTS], *[delta[n] for n in WEIGHTS],
            *[new_m[n] for n in WEIGHTS], *[new_v[n] for n in WEIGHTS])
```
